```python
import jax, jax.numpy as jnp
from jax import lax
import numpy as np

D_MODEL = 2048
BATCH = 8
SEQ = 8192
DEPTH = 4

CHUNK = 64
LEFT_CHUNKS = 8
BAND = (LEFT_CHUNKS + 1) * CHUNK
MAX_REL = 128
N_MEM = 256

MIX_WIDTH = D_MODEL
A_WIDTH = MIX_WIDTH // 2
A_HEADS = 8
A_HEAD_DIM = A_WIDTH // A_HEADS
B_WIDTH = MIX_WIDTH // 4
B_HEADS = 4
B_DV = B_WIDTH // B_HEADS
B_DK = B_DV // 2
B_KEY_WIDTH = B_HEADS * B_DK
GATE_RANK = 16
GATE_TAU = 16.0
M_WIDTH = MIX_WIDTH // 4
M_HEADS = 4
M_HEAD_DIM = M_WIDTH // M_HEADS

IN_SPLITS = (A_WIDTH, A_WIDTH, A_WIDTH, A_WIDTH,
             B_KEY_WIDTH, B_KEY_WIDTH, B_WIDTH, B_WIDTH, GATE_RANK,
             M_WIDTH, M_WIDTH)
IN_WIDTH = sum(IN_SPLITS)
VALUE_SPLITS = (2, 6)

DEEPNORM_ALPHA = (2.0 * DEPTH) ** 0.25
DEEPNORM_BETA = (8.0 * DEPTH) ** -0.25
LN_EPS = 1e-5
RMS_EPS = 1e-6
NEG_INF = -1e30

kernel_name = "hybrid_chunk_attn_gla_mem_deepnorm"


def layer_norm(x, g, b):
    xf = x.astype(jnp.float32)
    mu = jnp.mean(xf, axis=-1, keepdims=True)
    var = jnp.mean(jnp.square(xf - mu), axis=-1, keepdims=True)
    y = (xf - mu) * lax.rsqrt(var + LN_EPS) * g.astype(jnp.float32) + b.astype(jnp.float32)
    return y.astype(x.dtype)


def chunk_band_attention(q, k, v, rel_table):
    B, S, H, Dh = q.shape
    nc = S // CHUNK
    qc = q.reshape(B, nc, CHUNK, H, Dh)
    pad = ((0, 0), (LEFT_CHUNKS * CHUNK, 0), (0, 0), (0, 0))
    kp = jnp.pad(k, pad).reshape(B, nc + LEFT_CHUNKS, CHUNK, H, Dh)
    vp = jnp.pad(v, pad).reshape(B, nc + LEFT_CHUNKS, CHUNK, H, Dh)
    k_band = jnp.concatenate([kp[:, i:i + nc] for i in range(LEFT_CHUNKS + 1)], axis=2)
    v_band = jnp.concatenate([vp[:, i:i + nc] for i in range(LEFT_CHUNKS + 1)], axis=2)
    scores = jnp.einsum('bnqhd,bnkhd->bnhqk', qc, k_band).astype(jnp.float32) * (Dh ** -0.5)
    dist = jnp.arange(CHUNK)[:, None] + LEFT_CHUNKS * CHUNK - jnp.arange(BAND)[None, :]
    rel_idx = jnp.clip(dist, -MAX_REL, MAX_REL) + MAX_REL
    bias = rel_table[:, rel_idx].astype(jnp.float32)
    key_chunk = jnp.arange(nc)[:, None] - LEFT_CHUNKS + (jnp.arange(BAND) // CHUNK)[None, :]
    valid = key_chunk >= 0
    scores = jnp.where(valid[None, :, None, None, :], scores + bias[None, None], NEG_INF)
    p = jax.nn.softmax(scores, axis=-1).astype(v.dtype)
    out = jnp.einsum('bnhqk,bnkhd->bnqhd', p, v_band)
    return out.reshape(B, S, H, Dh)


def gla_chunk_recurrence(q, k, v, log_g):
    B, S, H, DK = q.shape
    DV = v.shape[-1]
    nc = S // CHUNK
    f32 = jnp.float32

    def to_chunks(t):
        return jnp.moveaxis(t.astype(f32).reshape(B, nc, CHUNK, H, t.shape[-1]), 1, 0)

    qs = to_chunks(q * (DK ** -0.5))
    ks, vs, gs = to_chunks(k), to_chunks(v), to_chunks(log_g)

    def step(state, inp):
        qc, kc, vc, gc = inp
        b = jnp.cumsum(gc, axis=1)
        decay = jnp.exp(-jnp.abs(b[:, :, None] - b[:, None, :]))
        attn = jnp.einsum('bihd,bjhd,bijhd->bhij', qc, kc, decay)
        o_intra = jnp.einsum('bhij,bjhv->bihv', attn, vc)
        o_inter = jnp.einsum('bihd,bhdv->bihv', qc * jnp.exp(b), state)
        b_last = b[:, -1]
        k_dec = kc * jnp.exp(b_last[:, None] - b)
        new_state = jnp.exp(b_last)[..., None] * state + jnp.einsum('bjhd,bjhv->bhdv', k_dec, vc)
        return new_state, o_intra + o_inter

    state0 = jnp.zeros((B, H, DK, DV), f32)
    _, out = lax.scan(step, state0, (qs, ks, vs, gs))
    return jnp.moveaxis(out, 0, 1).reshape(B, S, H, DV)


def memory_attention(q, mk, mv):
    s = jnp.einsum('bshd,bmhd->bhsm', q, mk).astype(jnp.float32) * (q.shape[-1] ** -0.5)
    p = jax.nn.softmax(s, axis=-1).astype(mv.dtype)
    return jnp.einsum('bhsm,bmhd->bshd', p, mv)


def hybrid_layer(x, mem, w_in, rel_table, gate_w, gate_b, gla_norm_g, w_mem_kv, w_out, ln_g, ln_b):
    B, S, _ = x.shape
    h = x @ w_in
    cuts = [int(c) for c in np.cumsum(IN_SPLITS)[:-1]]
    a_q, a_k, a_v, a_z, b_q, b_k, b_v, b_z, b_lr, m_q, m_z = jnp.split(h, cuts, axis=-1)

    heads_a = lambda t: t.reshape(B, S, A_HEADS, A_HEAD_DIM)
    y_a = chunk_band_attention(heads_a(a_q), heads_a(a_k), heads_a(a_v), rel_table).reshape(B, S, A_WIDTH)

    gate_logit = (b_lr @ gate_w + gate_b).astype(jnp.float32)
    log_g = jax.nn.log_sigmoid(gate_logit) / GATE_TAU
    heads_k = lambda t: t.reshape(B, S, B_HEADS, B_DK)
    o_b = gla_chunk_recurrence(heads_k(b_q), heads_k(b_k), b_v.reshape(B, S, B_HEADS, B_DV), heads_k(log_g))
    o_b = o_b * lax.rsqrt(jnp.mean(jnp.square(o_b), axis=-1, keepdims=True) + RMS_EPS) * gla_norm_g.astype(jnp.float32)
    y_b = o_b.reshape(B, S, B_WIDTH).astype(x.dtype)

    mkv = mem @ w_mem_kv
    mk, mv = jnp.split(mkv, 2, axis=-1)
    heads_m = lambda t: t.reshape(t.shape[0], t.shape[1], M_HEADS, M_HEAD_DIM)
    y_m = memory_attention(heads_m(m_q), heads_m(mk), heads_m(mv)).reshape(B, S, M_WIDTH)

    y = jnp.concatenate([y_a * jax.nn.silu(a_z), y_b * jax.nn.silu(b_z), y_m * jax.nn.silu(m_z)], axis=-1)
    out = y @ w_out
    return layer_norm(DEEPNORM_ALPHA * x + out, ln_g, ln_b)


def _fwd_setup_inputs(seed: int = 0) -> dict:
    key = jax.random.key(seed)
    ks = jax.random.split(key, 12)
    f32 = jnp.float32
    x = jax.random.normal(ks[0], (BATCH, SEQ, D_MODEL), f32)
    mem = jax.random.normal(ks[1], (N_MEM, D_MODEL), f32)[None].repeat(BATCH, axis=0) \
        + 0.1 * jax.random.normal(ks[2], (BATCH, N_MEM, D_MODEL), f32)
    col_scale = jnp.concatenate([
        jnp.full((n,), DEEPNORM_BETA if i in VALUE_SPLITS else 1.0, f32) for i, n in enumerate(IN_SPLITS)])
    w_in = jax.random.normal(ks[3], (DEPTH, D_MODEL, IN_WIDTH), f32) * (D_MODEL ** -0.5) * col_scale
    a_rel_bias = 0.1 * jax.random.normal(ks[4], (DEPTH, A_HEADS, 2 * MAX_REL + 1), f32)
    b_gate_w = jax.random.normal(ks[5], (DEPTH, GATE_RANK, B_KEY_WIDTH), f32) * (GATE_RANK ** -0.5)
    b_gate_b = 0.1 * jax.random.normal(ks[6], (DEPTH, B_KEY_WIDTH), f32)
    b_norm_g = 1.0 + 0.02 * jax.random.normal(ks[7], (DEPTH, B_DV), f32)
    kv_scale = jnp.concatenate([jnp.ones((M_WIDTH,), f32), jnp.full((M_WIDTH,), DEEPNORM_BETA, f32)])
    w_mem_kv = jax.random.normal(ks[8], (DEPTH, D_MODEL, 2 * M_WIDTH), f32) * (D_MODEL ** -0.5) * kv_scale
    w_out = jax.random.normal(ks[9], (DEPTH, MIX_WIDTH, D_MODEL), f32) * (MIX_WIDTH ** -0.5) * DEEPNORM_BETA
    ln_g = 1.0 + 0.02 * jax.random.normal(ks[10], (DEPTH, D_MODEL), f32)
    ln_b = 0.02 * jax.random.normal(ks[11], (DEPTH, D_MODEL), f32)
    return {"x": x, "mem": mem, "w_in": w_in, "a_rel_bias": a_rel_bias, "b_gate_w": b_gate_w,
            "b_gate_b": b_gate_b, "b_norm_g": b_norm_g, "w_mem_kv": w_mem_kv, "w_out": w_out,
            "ln_g": ln_g, "ln_b": ln_b}


def _fwd_reference(x, mem, w_in, a_rel_bias, b_gate_w, b_gate_b, b_norm_g, w_mem_kv, w_out, ln_g, ln_b):
    h = x
    for l in range(DEPTH):
        h = hybrid_layer(h, mem, w_in[l], a_rel_bias[l], b_gate_w[l], b_gate_b[l], b_norm_g[l],
                         w_mem_kv[l], w_out[l], ln_g[l], ln_b[l])
    return h


import jax as _jax
import jax.numpy as _jnp

TWIN_FORMAT = 'train_step'
FWD_PARAMS = ['x', 'mem', 'w_in', 'a_rel_bias', 'b_gate_w', 'b_gate_b', 'b_norm_g', 'w_mem_kv', 'w_out', 'ln_g', 'ln_b']
TWIN_WEIGHTS = ['w_in', 'a_rel_bias', 'b_gate_w', 'b_gate_b', 'b_norm_g', 'w_mem_kv', 'w_out', 'ln_g', 'ln_b']
TWIN_DIFF_INPUT = 'x'
TWIN_INPUTS = ['x', 'mem', 'w_in', 'a_rel_bias', 'b_gate_w', 'b_gate_b', 'b_norm_g', 'w_mem_kv', 'w_out', 'ln_g', 'ln_b', 'loss_target', 'm_w_in', 'm_a_rel_bias', 'm_b_gate_w', 'm_b_gate_b', 'm_b_norm_g', 'm_w_mem_kv', 'm_w_out', 'm_ln_g', 'm_ln_b', 'v_w_in', 'v_a_rel_bias', 'v_b_gate_w', 'v_b_gate_b', 'v_b_norm_g', 'v_w_mem_kv', 'v_w_out', 'v_ln_g', 'v_ln_b']
TWIN_OUTPUTS = ['loss', 'grad_x', 'grad_w_in', 'grad_a_rel_bias', 'grad_b_gate_w', 'grad_b_gate_b', 'grad_b_norm_g', 'grad_w_mem_kv', 'grad_w_out', 'grad_ln_g', 'grad_ln_b', 'delta_w_in', 'delta_a_rel_bias', 'delta_b_gate_w', 'delta_b_gate_b', 'delta_b_norm_g', 'delta_w_mem_kv', 'delta_w_out', 'delta_ln_g', 'delta_ln_b', 'new_m_w_in', 'new_m_a_rel_bias', 'new_m_b_gate_w', 'new_m_b_gate_b', 'new_m_b_norm_g', 'new_m_w_mem_kv', 'new_m_w_out', 'new_m_ln_g', 'new_m_ln_b', 'new_v_w_in', 'new_v_a_rel_bias', 'new_v_b_gate_w', 'new_v_b_gate_b', 'new_v_b_norm_g', 'new_v_w_mem_kv', 'new_v_w_out', 'new_v_ln_g', 'new_v_ln_b']
TWIN_LEAF_KINDS = {'loss': 'loss', 'grad_x': 'grad_x', 'grad_w_in': 'grad_w', 'grad_a_rel_bias': 'grad_w', 'grad_b_gate_w': 'grad_w', 'grad_b_gate_b': 'grad_w', 'grad_b_norm_g': 'grad_w', 'grad_w_mem_kv': 'grad_w', 'grad_w_out': 'grad_w', 'grad_ln_g': 'grad_w', 'grad_ln_b': 'grad_w', 'delta_w_in': 'delta_w', 'delta_a_rel_bias': 'delta_w', 'delta_b_gate_w': 'delta_w', 'delta_b_gate_b': 'delta_w', 'delta_b_norm_g': 'delta_w', 'delta_w_mem_kv': 'delta_w', 'delta_w_out': 'delta_w', 'delta_ln_g': 'delta_w', 'delta_ln_b': 'delta_w', 'new_m_w_in': 'new_m', 'new_m_a_rel_bias': 'new_m', 'new_m_b_gate_w': 'new_m', 'new_m_b_gate_b': 'new_m', 'new_m_b_norm_g': 'new_m', 'new_m_w_mem_kv': 'new_m', 'new_m_w_out': 'new_m', 'new_m_ln_g': 'new_m', 'new_m_ln_b': 'new_m', 'new_v_w_in': 'new_v', 'new_v_a_rel_bias': 'new_v', 'new_v_b_gate_w': 'new_v', 'new_v_b_gate_b': 'new_v', 'new_v_b_norm_g': 'new_v', 'new_v_w_mem_kv': 'new_v', 'new_v_w_out': 'new_v', 'new_v_ln_g': 'new_v', 'new_v_ln_b': 'new_v'}


def _forward(args):
    return _fwd_reference(*[args[k] for k in FWD_PARAMS])


def _output_shape():
    def fwd():
        inp = _fwd_setup_inputs(0)
        return _fwd_reference(*[inp[k] for k in FWD_PARAMS])
    out = _jax.eval_shape(fwd)
    return out.shape, out.dtype

N_MICROBATCH = 1
ADAM_LR = 0.001
ADAM_B1 = 0.9
ADAM_B2 = 0.999
ADAM_EPS = 1e-08
ADAM_WD = 0.01
ADAM_STEP = 10
PER_EXAMPLE_BATCH_AXIS = {'x': 0, 'mem': 0, 'loss_target': 0}
SHARED_INPUTS = []
_WEIGHT_DTYPES = {'w_in': _jnp.float32, 'a_rel_bias': _jnp.float32, 'b_gate_w': _jnp.float32, 'b_gate_b': _jnp.float32, 'b_norm_g': _jnp.float32, 'w_mem_kv': _jnp.float32, 'w_out': _jnp.float32, 'ln_g': _jnp.float32, 'ln_b': _jnp.float32}
MOMENT_SCALE = {'w_in': 1.563319e-02, 'a_rel_bias': 7.692382e-04, 'b_gate_w': 2.961618e-03, 'b_gate_b': 1.221086e-02, 'b_norm_g': 4.376684e-02, 'w_mem_kv': 1.729360e-03, 'w_out': 2.257188e-02, 'ln_g': 1.605857e+01, 'ln_b': 6.639695e-01}


def _to_microbatches(a, axis):
    t = _jnp.moveaxis(a, axis, 0)
    t = t.reshape((N_MICROBATCH, t.shape[0] // N_MICROBATCH) + t.shape[1:])
    return _jnp.moveaxis(t, 1, axis + 1)


def setup_inputs(seed: int = 0) -> dict:
    inp = _fwd_setup_inputs(seed)
    key = _jax.random.fold_in(_jax.random.key(seed), 7919)
    shape, _ = _output_shape()
    out = dict(inp)
    out["loss_target"] = _jax.random.normal(_jax.random.fold_in(key, 0), shape, _jnp.float32)
    for i, name in enumerate(TWIN_WEIGHTS):
        w = inp[name].astype(_jnp.float32)
        if MOMENT_SCALE is None:
            s = _jnp.sqrt(_jnp.mean(_jnp.square(w)) + 1e-30)
        else:
            s = MOMENT_SCALE[name]
        km, kv = _jax.random.split(_jax.random.fold_in(key, i + 1))
        out[name] = w
        out["m_" + name] = s * _jax.random.normal(km, w.shape, _jnp.float32)
        out["v_" + name] = (s * s) * _jax.random.uniform(kv, w.shape, _jnp.float32, 0.5, 1.5)
    if N_MICROBATCH > 1:
        for name, axis in PER_EXAMPLE_BATCH_AXIS.items():
            out[name] = _to_microbatches(out[name], axis)
    return {'x': out['x'], 'mem': out['mem'], 'w_in': out['w_in'], 'a_rel_bias': out['a_rel_bias'], 'b_gate_w': out['b_gate_w'], 'b_gate_b': out['b_gate_b'], 'b_norm_g': out['b_norm_g'], 'w_mem_kv': out['w_mem_kv'], 'w_out': out['w_out'], 'ln_g': out['ln_g'], 'ln_b': out['ln_b'], 'loss_target': out['loss_target'], 'm_w_in': out['m_w_in'], 'm_a_rel_bias': out['m_a_rel_bias'], 'm_b_gate_w': out['m_b_gate_w'], 'm_b_gate_b': out['m_b_gate_b'], 'm_b_norm_g': out['m_b_norm_g'], 'm_w_mem_kv': out['m_w_mem_kv'], 'm_w_out': out['m_w_out'], 'm_ln_g': out['m_ln_g'], 'm_ln_b': out['m_ln_b'], 'v_w_in': out['v_w_in'], 'v_a_rel_bias': out['v_a_rel_bias'], 'v_b_gate_w': out['v_b_gate_w'], 'v_b_gate_b': out['v_b_gate_b'], 'v_b_norm_g': out['v_b_norm_g'], 'v_w_mem_kv': out['v_w_mem_kv'], 'v_w_out': out['v_w_out'], 'v_ln_g': out['v_ln_g'], 'v_ln_b': out['v_ln_b']}


def _loss(weights, diff, rest, loss_target):
    with _jax.named_scope("forward"):
        args = {**rest, TWIN_DIFF_INPUT: diff, **{k: w.astype(_WEIGHT_DTYPES[k]) for k, w in weights.items()}}
        y = _forward(args)
    with _jax.named_scope("loss_head"):
        err = _jnp.square(y.astype(_jnp.float32) - loss_target)
        return 0.5 * _jnp.sum(_jnp.mean(err, axis=-1)) if err.ndim else 0.5 * err


def _adamw(w, g, m, v):
    m = ADAM_B1 * m + (1.0 - ADAM_B1) * g
    v = ADAM_B2 * v + (1.0 - ADAM_B2) * _jnp.square(g)
    m_hat = m / (1.0 - ADAM_B1 ** ADAM_STEP)
    v_hat = v / (1.0 - ADAM_B2 ** ADAM_STEP)
    delta = -ADAM_LR * (m_hat / (_jnp.sqrt(v_hat) + ADAM_EPS) + ADAM_WD * w)
    return delta, m, v


def reference(x, mem, w_in, a_rel_bias, b_gate_w, b_gate_b, b_norm_g, w_mem_kv, w_out, ln_g, ln_b, loss_target, m_w_in, m_a_rel_bias, m_b_gate_w, m_b_gate_b, m_b_norm_g, m_w_mem_kv, m_w_out, m_ln_g, m_ln_b, v_w_in, v_a_rel_bias, v_b_gate_w, v_b_gate_b, v_b_norm_g, v_w_mem_kv, v_w_out, v_ln_g, v_ln_b):
    given = dict(x=x, mem=mem, w_in=w_in, a_rel_bias=a_rel_bias, b_gate_w=b_gate_w, b_gate_b=b_gate_b, b_norm_g=b_norm_g, w_mem_kv=w_mem_kv, w_out=w_out, ln_g=ln_g, ln_b=ln_b, loss_target=loss_target, m_w_in=m_w_in, m_a_rel_bias=m_a_rel_bias, m_b_gate_w=m_b_gate_w, m_b_gate_b=m_b_gate_b, m_b_norm_g=m_b_norm_g, m_w_mem_kv=m_w_mem_kv, m_w_out=m_w_out, m_ln_g=m_ln_g, m_ln_b=m_ln_b, v_w_in=v_w_in, v_a_rel_bias=v_a_rel_bias, v_b_gate_w=v_b_gate_w, v_b_gate_b=v_b_gate_b, v_b_norm_g=v_b_norm_g, v_w_mem_kv=v_w_mem_kv, v_w_out=v_w_out, v_ln_g=v_ln_g, v_ln_b=v_ln_b)
    weights = {n: given[n] for n in TWIN_WEIGHTS}
    shared = {n: given[n] for n in SHARED_INPUTS}
    per_example = {n: given[n] for n in ['x', 'mem']}
    grad_fn = _jax.value_and_grad(_loss, argnums=(0, 1))

    def one_microbatch(ex, loss_target):
        ex = dict(ex)
        diff = ex.pop(TWIN_DIFF_INPUT)
        return grad_fn(weights, diff, {**shared, **ex}, loss_target)

    if N_MICROBATCH == 1:
        loss, (grad_w, grad_x) = one_microbatch(per_example, given["loss_target"])
    else:
        def body(carry, xs):
            loss_sum, grad_sum = carry
            l_k, (gw_k, gx_k) = one_microbatch(xs[0], xs[1])
            with _jax.named_scope("update"):
                return (loss_sum + l_k, _jax.tree.map(_jnp.add, grad_sum, gw_k)), gx_k

        init = (_jnp.zeros((), _jnp.float32), _jax.tree.map(_jnp.zeros_like, weights))
        (loss, grad_w), grad_x = _jax.lax.scan(body, init, (per_example, given["loss_target"]))
    with _jax.named_scope("update"):
        delta_w, new_m, new_v = {}, {}, {}
        for n in TWIN_WEIGHTS:
            delta_w[n], new_m[n], new_v[n] = _adamw(weights[n], grad_w[n], given["m_" + n], given["v_" + n])
    return (loss, grad_x, *[grad_w[n] for n in TWIN_WEIGHTS], *[delta_w[n] for n in TWIN_WEIGHTS],
            *[new_m[n] for n in TWIN_WEIGHTS], *[new_v[n] for n in TWIN_WEIGHTS])
```

```python
import functools
import math

import numpy as np
import jax
import jax.numpy as jnp
from jax import lax
from jax.experimental import pallas as pl
from jax.experimental.pallas import tpu as pltpu

F32 = jnp.float32
BF16 = jnp.bfloat16

N_DEV = 8
D = 2048
CH = 64
LEFT = 8
MAX_REL = 128
N_MEM = 256
A_HEADS, A_DH, A_W = 8, 128, 1024
B_HEADS, B_DK, B_DV, B_KW, B_W = 4, 64, 128, 256, 512
GATE_RANK, GATE_TAU = 16, 16.0
M_HEADS, M_DH, M_W = 4, 128, 512
IN_W = 6672
NAT_SPLIT = 5648
H_W = 7168
H_PAD = H_W - IN_W
A_Q, A_K, A_V, A_Z = 0, 1024, 2048, 3072
B_BASE = 4096
M_BASE = 6144
ALPHA = (2.0 * 4) ** 0.25
LN_EPS = 1e-5
RMS_EPS = 1e-6
NEG = -1e30
QB = 256
KB = 3 * QB
ADAM_LR, ADAM_B1, ADAM_B2, ADAM_EPS, ADAM_WD, ADAM_STEP = 0.001, 0.9, 0.999, 1e-08, 0.01, 10
VMEM_MB = 1024 * 1024


def _params(sem, vmem_mb=48):
    return pltpu.CompilerParams(dimension_semantics=sem, vmem_limit_bytes=vmem_mb * VMEM_MB)


def _sigmoid(x):
    return 1.0 / (1.0 + jnp.exp(-x))


def _dot(a, b, ca, cb, precision=None):
    return lax.dot_general(a, b, (((ca,), (cb,)), ((), ())), preferred_element_type=F32, precision=precision)


def _mm(a, b, *, ta=False, tb=False, out_dtype, tm, tn, tk, name, adds=(), vmem_mb=48):
    m = a.shape[1] if ta else a.shape[0]
    k = a.shape[0] if ta else a.shape[1]
    n = b.shape[0] if tb else b.shape[1]
    assert k == (b.shape[1] if tb else b.shape[0])
    tm, tn, tk = min(tm, m), min(tn, n), min(tk, k)
    assert m % tm == 0 and n % tn == 0 and k % tk == 0, (name, m, n, k)
    nk = k // tk
    n_add = len(adds)
    scales = [s for _, s in adds]

    def body(a_ref, b_ref, *rest):
        add_refs, o_ref, acc_ref = rest[:n_add], rest[n_add], rest[n_add + 1]
        p = _dot(a_ref[...].astype(BF16), b_ref[...].astype(BF16), 0 if ta else 1, 1 if tb else 0)
        kk = pl.program_id(2)

        @pl.when(kk == 0)
        def _():
            acc_ref[...] = p

        @pl.when(kk > 0)
        def _():
            acc_ref[...] += p

        @pl.when(kk == nk - 1)
        def _():
            r = acc_ref[...]
            for ref, s in zip(add_refs, scales):
                r = r + s * ref[...].astype(F32)
            o_ref[...] = r.astype(out_dtype)

    a_spec = pl.BlockSpec((tk, tm), lambda i, j, kk: (kk, i)) if ta else pl.BlockSpec((tm, tk), lambda i, j, kk: (i, kk))
    b_spec = pl.BlockSpec((tn, tk), lambda i, j, kk: (j, kk)) if tb else pl.BlockSpec((tk, tn), lambda i, j, kk: (kk, j))
    add_specs = [pl.BlockSpec((tm, tn), lambda i, j, kk: (i, j)) for _ in adds]
    return pl.pallas_call(
        body, name=name,
        out_shape=jax.ShapeDtypeStruct((m, n), out_dtype),
        grid=(m // tm, n // tn, nk),
        in_specs=[a_spec, b_spec] + add_specs,
        out_specs=pl.BlockSpec((tm, tn), lambda i, j, kk: (i, j)),
        scratch_shapes=[pltpu.VMEM((tm, tn), F32)],
        compiler_params=_params(("parallel", "parallel", "arbitrary"), vmem_mb),
    )(a, b, *[x for x, _ in adds])


def _band_bias(table):
    i = np.arange(QB)[:, None]
    j = np.arange(KB)[None, :]
    dist = i + 2 * QB - j
    idx = np.clip(dist, -MAX_REL, MAX_REL) + MAX_REL
    qc = i // CH + 2 * QB // CH
    kc = j // CH
    valid = (kc <= qc) & (kc >= qc - LEFT)
    bias = table[:, idx]
    return jnp.where(valid[None], bias, NEG).astype(F32)


def _bias_grad(dbias):
    h = dbias.shape[0]
    flipped = dbias[:, ::-1, :]
    padded = jnp.pad(flipped, ((0, 0), (0, 0), (0, 1))).reshape(h, QB * (KB + 1))
    padded = jnp.pad(padded, ((0, 0), (0, (QB + 1) * KB - QB * (KB + 1))))
    diag = padded.reshape(h, QB + 1, KB).sum(axis=1)
    jm = np.arange(KB) - (QB - 1)
    jm = np.where(jm <= -CH, jm + KB, jm)
    didx = np.clip(2 * QB - jm, -MAX_REL, MAX_REL) + MAX_REL
    onehot = np.zeros((KB, 2 * MAX_REL + 1), np.float32)
    onehot[np.arange(KB), didx] = 1.0
    return jnp.dot(diag, jnp.asarray(onehot), precision=lax.Precision.HIGHEST)


def _attn_fwd(h, r_in, bias):
    s = h.shape[0]
    nq = s // QB
    scale = A_DH ** -0.5

    def body(q_ref, k0_ref, k1_ref, k2_ref, v0_ref, v1_ref, v2_ref, bias_hbm, r_any, o_ref, lse_ref, bias_ref):
        del r_any
        m = pl.program_id(0)
        k_refs, v_refs = (k0_ref, k1_ref, k2_ref), (v0_ref, v1_ref, v2_ref)

        @pl.when(m == 0)
        def _():
            pltpu.sync_copy(bias_hbm, bias_ref)

        for hd in range(A_HEADS):
            cs = slice(hd * A_DH, (hd + 1) * A_DH)
            q = q_ref[:, cs]
            parts = []
            for t in range(3):
                sc = _dot(q, k_refs[t][:, cs], 1, 1) * scale
                parts.append(jnp.where(m + t - 2 >= 0, sc, NEG))
            sc = jnp.concatenate(parts, axis=1) + bias_ref[hd]
            mx = jnp.max(sc, axis=1, keepdims=True)
            p = jnp.exp(sc - mx)
            l = jnp.sum(p, axis=1, keepdims=True)
            pb = p.astype(BF16)
            o = _dot(pb[:, :QB], v_refs[0][:, cs], 1, 0)
            o += _dot(pb[:, QB:2 * QB], v_refs[1][:, cs], 1, 0)
            o += _dot(pb[:, 2 * QB:], v_refs[2][:, cs], 1, 0)
            o_ref[:, cs] = (o / l).astype(BF16)
            lse_ref[:, hd:hd + 1] = mx + jnp.log(l)

    def kv_spec(col, t):
        return pl.BlockSpec((QB, A_W), lambda m: (jnp.maximum(m + t - 2, 0), col))

    return pl.pallas_call(
        body, name="attn_fwd",
        out_shape=(jax.ShapeDtypeStruct(r_in.shape, BF16), jax.ShapeDtypeStruct((s, A_HEADS), F32)),
        grid=(nq,),
        in_specs=[pl.BlockSpec((QB, A_W), lambda m: (m, 0))]
        + [kv_spec(1, t) for t in range(3)] + [kv_spec(2, t) for t in range(3)]
        + [pl.BlockSpec(memory_space=pl.ANY), pl.BlockSpec(memory_space=pl.ANY)],
        out_specs=(pl.BlockSpec((QB, A_W), lambda m: (m, 0)), pl.BlockSpec((QB, A_HEADS), lambda m: (m, 0))),
        scratch_shapes=[pltpu.VMEM((A_HEADS, QB, KB), F32)],
        input_output_aliases={8: 0},
        compiler_params=_params(("arbitrary",)),
    )(h, h, h, h, h, h, h, bias, r_in)


def _attn_bwd(h, r, dy, lse, bias, dh_in):
    s = h.shape[0]
    nq = s // QB
    scale = A_DH ** -0.5

    def body(q_ref, k0_ref, k1_ref, k2_ref, v0_ref, v1_ref, v2_ref, z_ref, r_ref, dy_ref, lse_ref,
             zl_ref, rl_ref, dyl_ref, bias_hbm, dh_any, dh_ref, dbias_hbm, dk_acc, dv_acc, dq_hist, bias_ref, dbias_ref):
        del dh_any
        m = pl.program_id(0)
        k_refs, v_refs = (k0_ref, k1_ref, k2_ref), (v0_ref, v1_ref, v2_ref)

        @pl.when(m == 0)
        def _():
            dk_acc[...] = jnp.zeros_like(dk_acc)
            dv_acc[...] = jnp.zeros_like(dv_acc)
            dq_hist[...] = jnp.zeros_like(dq_hist)
            dbias_ref[...] = jnp.zeros_like(dbias_ref)
            pltpu.sync_copy(bias_hbm, bias_ref)

        @pl.when(m < nq)
        def _():
            z = z_ref[...].astype(F32)
            do_all = dy_ref[...].astype(F32) * (z * _sigmoid(z))
            o_all = r_ref[...].astype(F32)
            for hd in range(A_HEADS):
                cs = slice(hd * A_DH, (hd + 1) * A_DH)
                q = q_ref[:, cs]
                do = do_all[:, cs]
                dob = do.astype(BF16)
                delta = jnp.sum(do * o_all[:, cs], axis=1, keepdims=True)
                parts, dps = [], []
                for t in range(3):
                    sc = _dot(q, k_refs[t][:, cs], 1, 1) * scale
                    parts.append(jnp.where(m + t - 2 >= 0, sc, NEG))
                    dps.append(_dot(dob, v_refs[t][:, cs], 1, 1))
                sc = jnp.concatenate(parts, axis=1) + bias_ref[hd]
                p = jnp.exp(sc - lse_ref[:, hd:hd + 1])
                ds = p * (jnp.concatenate(dps, axis=1) - delta)
                dbias_ref[hd] += ds
                pb, dsb = p.astype(BF16), ds.astype(BF16)
                dq = jnp.zeros((QB, A_DH), F32)
                for t in range(3):
                    ts = slice(t * QB, (t + 1) * QB)
                    dq += _dot(dsb[:, ts], k_refs[t][:, cs], 1, 0)
                    dk_acc[ts, cs] += _dot(dsb[:, ts], q, 0, 0) * scale
                    dv_acc[ts, cs] += _dot(pb[:, ts], dob, 0, 0)
                dq_hist[2, :, cs] = dq * scale

        zl = zl_ref[...].astype(F32)
        sg = _sigmoid(zl)
        dz = dyl_ref[...].astype(F32) * rl_ref[...].astype(F32) * (sg * (1.0 + zl * (1.0 - sg)))
        dh_ref[:, A_Q:A_Q + A_W] = dq_hist[0].astype(BF16)
        dh_ref[:, A_K:A_K + A_W] = dk_acc[0:QB, :].astype(BF16)
        dh_ref[:, A_V:A_V + A_W] = dv_acc[0:QB, :].astype(BF16)
        dh_ref[:, A_Z:A_Z + A_W] = dz.astype(BF16)
        dq_hist[0] = dq_hist[1]
        dq_hist[1] = dq_hist[2]
        dk_acc[0:2 * QB, :] = dk_acc[QB:3 * QB, :]
        dv_acc[0:2 * QB, :] = dv_acc[QB:3 * QB, :]
        dk_acc[2 * QB:3 * QB, :] = jnp.zeros((QB, A_W), F32)
        dv_acc[2 * QB:3 * QB, :] = jnp.zeros((QB, A_W), F32)

        @pl.when(m == nq + 1)
        def _():
            pltpu.sync_copy(dbias_ref, dbias_hbm)

    last = nq - 1

    def cur(col):
        return pl.BlockSpec((QB, A_W), lambda m: (jnp.minimum(m, last), col))

    def kv_spec(col, t):
        return pl.BlockSpec((QB, A_W), lambda m: (jnp.clip(m + t - 2, 0, last), col))

    def lag(col):
        return pl.BlockSpec((QB, A_W), lambda m: (jnp.clip(m - 2, 0, last), col))

    return pl.pallas_call(
        body, name="attn_bwd",
        out_shape=(jax.ShapeDtypeStruct(dh_in.shape, BF16), jax.ShapeDtypeStruct((A_HEADS, QB, KB), F32)),
        grid=(nq + 2,),
        in_specs=[cur(0)] + [kv_spec(1, t) for t in range(3)] + [kv_spec(2, t) for t in range(3)]
        + [cur(3), cur(0), cur(0), pl.BlockSpec((QB, A_HEADS), lambda m: (jnp.minimum(m, last), 0)),
           lag(3), lag(0), lag(0),
           pl.BlockSpec(memory_space=pl.ANY), pl.BlockSpec(memory_space=pl.ANY)],
        out_specs=(pl.BlockSpec((QB, 4 * A_W), lambda m: (jnp.clip(m - 2, 0, last), 0)),
                   pl.BlockSpec(memory_space=pl.ANY)),
        scratch_shapes=[pltpu.VMEM((KB, A_W), F32), pltpu.VMEM((KB, A_W), F32), pltpu.VMEM((3, QB, A_W), F32),
                        pltpu.VMEM((A_HEADS, QB, KB), F32), pltpu.VMEM((A_HEADS, QB, KB), F32)],
        input_output_aliases={15: 0},
        compiler_params=_params(("arbitrary",), 56),
    )(h, h, h, h, h, h, h, h, r, dy, lse, h, r, dy, bias, dh_in)


GB = 256
N_PAIR = B_HEADS // 2


def _gla_gates(lr, gw_ref, gb_ref):
    logit = _dot(lr, gw_ref[...], 1, 0) + gb_ref[...]
    lg = (jnp.minimum(logit, 0.0) - jnp.log(1.0 + jnp.exp(-jnp.abs(logit)))) / GATE_TAU
    row = lax.broadcasted_iota(jnp.int32, (GB, GB), 0)
    col = lax.broadcasted_iota(jnp.int32, (GB, GB), 1)
    tri = jnp.where((row // CH == col // CH) & (col <= row), 1.0, 0.0).astype(F32)
    return logit, _dot(tri, lg, 1, 0, precision=lax.Precision.HIGHEST)


def _gla_factors(hb_ref, b_all, c):
    rs = slice(c * CH, (c + 1) * CH)
    q = hb_ref[rs, 0:B_KW].astype(F32) * (B_DK ** -0.5)
    k = hb_ref[rs, B_KW:2 * B_KW].astype(F32)
    b = b_all[rs]
    bm, bl = b[CH // 2:CH // 2 + 1, :], b[CH - 1:CH, :]
    e1, e2, eb, ek = jnp.exp(b - bm), jnp.exp(bm - b), jnp.exp(b), jnp.exp(bl - b)
    el = jnp.exp(bl)
    return dict(ql=q * e1, kl=k * e2, qu=q * e2, ku=k * e1, qt=q * eb, kh=k * ek, e1=e1, e2=e2, eb=eb, ek=ek, el=el)


def _gla_fwd(h, r_in, gw, gb, ng):
    s = h.shape[0]
    nb = s // GB
    cpb = GB // CH

    def body(hb_ref, gw_ref, gb_ref, ng_ref, r_any, o_ref, opre_ref, st_ref, state):
        del r_any

        @pl.when(pl.program_id(0) == 0)
        def _():
            state[...] = jnp.zeros_like(state)

        _, b_all = _gla_gates(hb_ref[:, 1536:1664], gw_ref, gb_ref)
        lane = lax.broadcasted_iota(jnp.int32, (CH, 128), 1)
        ri = lax.broadcasted_iota(jnp.int32, (CH, CH), 0)
        ci = lax.broadcasted_iota(jnp.int32, (CH, CH), 1)
        for c in range(cpb):
            rs = slice(c * CH, (c + 1) * CH)
            f = _gla_factors(hb_ref, b_all, c)
            for p in range(N_PAIR):
                ls = slice(p * 128, (p + 1) * 128)
                st = state[p]
                st_ref[c, p] = st
                stb = st.astype(BF16)
                klp, kup = f["kl"][:, ls].astype(BF16), f["ku"][:, ls].astype(BF16)
                upd = st * f["el"][:, ls]
                for sh in range(2):
                    hd = 2 * p + sh
                    msk = (lane // B_DK) == sh
                    qlm = jnp.where(msk, f["ql"][:, ls], 0.0).astype(BF16)
                    qum = jnp.where(msk, f["qu"][:, ls], 0.0).astype(BF16)
                    qtm = jnp.where(msk, f["qt"][:, ls], 0.0).astype(BF16)
                    khm = jnp.where(msk, f["kh"][:, ls], 0.0).astype(BF16)
                    v = hb_ref[rs, 512 + hd * B_DV:512 + (hd + 1) * B_DV]
                    a = jnp.where(ri >= ci, _dot(qlm, klp, 1, 1), _dot(qum, kup, 1, 1))
                    o = _dot(a.astype(BF16), v, 1, 0) + _dot(qtm, stb, 1, 1)
                    upd = upd + _dot(v, khm, 0, 0)
                    hs = slice(hd * B_DV, (hd + 1) * B_DV)
                    opre_ref[rs, hs] = o
                    rinv = lax.rsqrt(jnp.mean(o * o, axis=1, keepdims=True) + RMS_EPS)
                    o_ref[rs, hs] = (o * rinv * ng_ref[...]).astype(BF16)
                state[p] = upd

    return pl.pallas_call(
        body, name="gla_fwd",
        out_shape=(jax.ShapeDtypeStruct(r_in.shape, BF16), jax.ShapeDtypeStruct((s, B_W), F32),
                   jax.ShapeDtypeStruct((s // CH, N_PAIR, 128, 128), F32)),
        grid=(nb,),
        in_specs=[pl.BlockSpec((GB, 2048), lambda i: (i, B_BASE // 2048)),
                  pl.BlockSpec((128, B_KW), lambda i: (0, 0)), pl.BlockSpec((1, B_KW), lambda i: (0, 0)),
                  pl.BlockSpec((1, B_DV), lambda i: (0, 0)), pl.BlockSpec(memory_space=pl.ANY)],
        out_specs=(pl.BlockSpec((GB, B_W), lambda i: (i, 1024 // B_W)), pl.BlockSpec((GB, B_W), lambda i: (i, 0)),
                   pl.BlockSpec((cpb, N_PAIR, 128, 128), lambda i: (i, 0, 0, 0))),
        scratch_shapes=[pltpu.VMEM((N_PAIR, 128, 128), F32)],
        input_output_aliases={4: 0},
        compiler_params=_params(("arbitrary",)),
    )(h, gw, gb, ng, r_in)


def _gla_bwd(h, dy, opre, states, gw, gb, ng, dh_in):
    s = h.shape[0]
    nb = s // GB
    cpb = GB // CH

    def body(hb_ref, dy_ref, opre_ref, st_ref, gw_ref, gb_ref, ng_ref, dh_any,
             dh_ref, dgw_ref, dgb_ref, dng_ref, dstate, db_scr, do_scr):
        del dh_any

        @pl.when(pl.program_id(0) == 0)
        def _():
            dstate[...] = jnp.zeros_like(dstate)
            dgw_ref[...] = jnp.zeros_like(dgw_ref)
            dgb_ref[...] = jnp.zeros_like(dgb_ref)
            dng_ref[...] = jnp.zeros_like(dng_ref)

        lr = hb_ref[:, 1536:1664]
        logit, b_all = _gla_gates(lr, gw_ref, gb_ref)
        z = hb_ref[:, 1024:1536].astype(F32)
        sg = _sigmoid(z)
        dyb = dy_ref[...].astype(F32)
        dng = jnp.zeros((1, B_DV), F32)
        for hd in range(B_HEADS):
            hs = slice(hd * B_DV, (hd + 1) * B_DV)
            o = opre_ref[:, hs]
            rinv = lax.rsqrt(jnp.mean(o * o, axis=1, keepdims=True) + RMS_EPS)
            on = o * rinv
            dr = dyb[:, hs] * (z[:, hs] * sg[:, hs])
            dh_ref[:, 1024 + hd * B_DV:1024 + (hd + 1) * B_DV] = (
                dyb[:, hs] * (on * ng_ref[...]) * (sg[:, hs] * (1.0 + z[:, hs] * (1.0 - sg[:, hs])))).astype(BF16)
            dng += jnp.sum(dr * on, axis=0, keepdims=True)
            dn = dr * ng_ref[...]
            do_scr[:, hs] = rinv * (dn - on * jnp.mean(dn * on, axis=1, keepdims=True))
        dng_ref[...] += dng

        lane = lax.broadcasted_iota(jnp.int32, (CH, 128), 1)
        ri = lax.broadcasted_iota(jnp.int32, (CH, CH), 0)
        ci = lax.broadcasted_iota(jnp.int32, (CH, CH), 1)
        rowi = lax.broadcasted_iota(jnp.int32, (CH, 128), 0)
        for c in reversed(range(cpb)):
            rs = slice(c * CH, (c + 1) * CH)
            f = _gla_factors(hb_ref, b_all, c)
            for p in range(N_PAIR):
                ls = slice(p * 128, (p + 1) * 128)
                fp = {n: x[:, ls] for n, x in f.items()}
                st_prev = st_ref[c, p]
                dst_new = dstate[p]
                stb, dstb = st_prev.astype(BF16), dst_new.astype(BF16)
                klp, kup = fp["kl"].astype(BF16), fp["ku"].astype(BF16)
                dst_prev = dst_new * fp["el"]
                db_last = fp["el"] * jnp.sum(dst_new * st_prev, axis=0, keepdims=True)
                d_ql = jnp.zeros((CH, 128), F32)
                d_qu, d_qt, d_kl, d_ku, d_kh = d_ql, d_ql, d_ql, d_ql, d_ql
                for sh in range(2):
                    hd = 2 * p + sh
                    msk = (lane // B_DK) == sh
                    qlm = jnp.where(msk, fp["ql"], 0.0).astype(BF16)
                    qum = jnp.where(msk, fp["qu"], 0.0).astype(BF16)
                    qtm = jnp.where(msk, fp["qt"], 0.0).astype(BF16)
                    khm = jnp.where(msk, fp["kh"], 0.0).astype(BF16)
                    v = hb_ref[rs, 512 + hd * B_DV:512 + (hd + 1) * B_DV]
                    dob = do_scr[rs, hd * B_DV:(hd + 1) * B_DV].astype(BF16)
                    at = jnp.where(ci >= ri, _dot(klp, qlm, 1, 1), _dot(kup, qum, 1, 1))
                    da = _dot(dob, v, 1, 1)
                    dat = _dot(v, dob, 1, 1)
                    dl = jnp.where(ri >= ci, da, 0.0).astype(BF16)
                    du = jnp.where(ri < ci, da, 0.0).astype(BF16)
                    dlt = jnp.where(ci >= ri, dat, 0.0).astype(BF16)
                    dut = jnp.where(ci < ri, dat, 0.0).astype(BF16)
                    dv = _dot(at.astype(BF16), dob, 1, 0) + _dot(khm, dstb, 1, 1)
                    dh_ref[rs, 512 + hd * B_DV:512 + (hd + 1) * B_DV] = dv.astype(BF16)
                    d_qt += jnp.where(msk, _dot(dob, stb, 1, 0), 0.0)
                    d_kh += jnp.where(msk, _dot(v, dstb, 1, 0), 0.0)
                    d_ql += jnp.where(msk, _dot(dl, klp, 1, 0), 0.0)
                    d_qu += jnp.where(msk, _dot(du, kup, 1, 0), 0.0)
                    d_kl += _dot(dlt, qlm, 1, 0)
                    d_ku += _dot(dut, qum, 1, 0)
                    dst_prev += _dot(dob, qtm, 0, 0)
                dstate[p] = dst_prev
                dq = (d_ql * fp["e1"] + d_qu * fp["e2"] + d_qt * fp["eb"]) * (B_DK ** -0.5)
                dk = d_kl * fp["e2"] + d_ku * fp["e1"] + d_kh * fp["ek"]
                dkh_kh = d_kh * fp["kh"]
                db = d_ql * fp["ql"] - d_qu * fp["qu"] + d_qt * fp["qt"] - d_kl * fp["kl"] + d_ku * fp["ku"] - dkh_kh
                db_last += jnp.sum(dkh_kh, axis=0, keepdims=True)
                db = jnp.where(rowi == CH - 1, db + db_last, db)
                dh_ref[rs, p * 128:(p + 1) * 128] = dq.astype(BF16)
                dh_ref[rs, B_KW + p * 128:B_KW + (p + 1) * 128] = dk.astype(BF16)
                db_scr[rs, ls] = db

        row = lax.broadcasted_iota(jnp.int32, (GB, GB), 0)
        col = lax.broadcasted_iota(jnp.int32, (GB, GB), 1)
        trit = jnp.where((row // CH == col // CH) & (col >= row), 1.0, 0.0).astype(F32)
        dlg = _dot(trit, db_scr[...], 1, 0, precision=lax.Precision.HIGHEST)
        dlogit = dlg * (_sigmoid(-logit) / GATE_TAU)
        dlb = dlogit.astype(BF16)
        dgw_ref[...] += _dot(lr, dlb, 0, 0)
        dgb_ref[...] += jnp.sum(dlogit, axis=0, keepdims=True)
        dh_ref[:, 1536:1664] = _dot(dlb, gw_ref[...], 1, 1).astype(BF16)
        dh_ref[:, 1664:2048] = jnp.zeros((GB, 384), BF16)

    rev = lambda i: nb - 1 - i
    return pl.pallas_call(
        body, name="gla_bwd",
        out_shape=(jax.ShapeDtypeStruct(dh_in.shape, BF16), jax.ShapeDtypeStruct((128, B_KW), F32),
                   jax.ShapeDtypeStruct((1, B_KW), F32), jax.ShapeDtypeStruct((1, B_DV), F32)),
        grid=(nb,),
        in_specs=[pl.BlockSpec((GB, 2048), lambda i: (rev(i), B_BASE // 2048)),
                  pl.BlockSpec((GB, B_W), lambda i: (rev(i), 1024 // B_W)),
                  pl.BlockSpec((GB, B_W), lambda i: (rev(i), 0)),
                  pl.BlockSpec((cpb, N_PAIR, 128, 128), lambda i: (rev(i), 0, 0, 0)),
                  pl.BlockSpec((128, B_KW), lambda i: (0, 0)), pl.BlockSpec((1, B_KW), lambda i: (0, 0)),
                  pl.BlockSpec((1, B_DV), lambda i: (0, 0)), pl.BlockSpec(memory_space=pl.ANY)],
        out_specs=(pl.BlockSpec((GB, 2048), lambda i: (rev(i), B_BASE // 2048)),
                   pl.BlockSpec((128, B_KW), lambda i: (0, 0)), pl.BlockSpec((1, B_KW), lambda i: (0, 0)),
                   pl.BlockSpec((1, B_DV), lambda i: (0, 0))),
        scratch_shapes=[pltpu.VMEM((N_PAIR, 128, 128), F32), pltpu.VMEM((GB, B_KW), F32), pltpu.VMEM((GB, B_W), F32)],
        input_output_aliases={7: 0},
        compiler_params=_params(("arbitrary",)),
    )(h, dy, opre, states, gw, gb, ng, dh_in)


MB = 512


def _mem_probs(q, mk, scale):
    sc = _dot(q, mk, 1, 1) * scale
    p = jnp.exp(sc - jnp.max(sc, axis=1, keepdims=True))
    return p / jnp.sum(p, axis=1, keepdims=True)


def _mem_fwd(h, r_in, mkv):
    s = h.shape[0]
    scale = M_DH ** -0.5

    def body(q_ref, mkv_ref, r_any, o_ref):
        del r_any
        for hd in range(M_HEADS):
            cs = slice(hd * M_DH, (hd + 1) * M_DH)
            p = _mem_probs(q_ref[:, cs], mkv_ref[:, cs], scale)
            o_ref[:, cs] = _dot(p.astype(BF16), mkv_ref[:, M_W + hd * M_DH:M_W + (hd + 1) * M_DH], 1, 0).astype(BF16)

    return pl.pallas_call(
        body, name="mem_fwd",
        out_shape=jax.ShapeDtypeStruct(r_in.shape, BF16),
        grid=(s // MB,),
        in_specs=[pl.BlockSpec((MB, M_W), lambda i: (i, M_BASE // M_W)),
                  pl.BlockSpec((N_MEM, 2 * M_W), lambda i: (0, 0)), pl.BlockSpec(memory_space=pl.ANY)],
        out_specs=pl.BlockSpec((MB, M_W), lambda i: (i, 1536 // M_W)),
        input_output_aliases={2: 0},
        compiler_params=_params(("arbitrary",)),
    )(h, mkv, r_in)


def _mem_bwd(h, r, dy, mkv, dh_in):
    s = h.shape[0]
    scale = M_DH ** -0.5

    def body(q_ref, z_ref, r_ref, dy_ref, mkv_ref, dh_any, dh_ref, dmkv_ref):
        del dh_any

        @pl.when(pl.program_id(0) == 0)
        def _():
            dmkv_ref[...] = jnp.zeros_like(dmkv_ref)

        z = z_ref[...].astype(F32)
        sg = _sigmoid(z)
        dyv = dy_ref[...].astype(F32)
        do_all = dyv * (z * sg)
        dh_ref[:, M_W:2 * M_W] = (dyv * r_ref[...].astype(F32) * (sg * (1.0 + z * (1.0 - sg)))).astype(BF16)
        for hd in range(M_HEADS):
            cs = slice(hd * M_DH, (hd + 1) * M_DH)
            vs = slice(M_W + hd * M_DH, M_W + (hd + 1) * M_DH)
            q = q_ref[:, cs]
            p = _mem_probs(q, mkv_ref[:, cs], scale)
            dob = do_all[:, cs].astype(BF16)
            dp = _dot(dob, mkv_ref[:, vs], 1, 1)
            ds = p * (dp - jnp.sum(p * dp, axis=1, keepdims=True))
            dsb = ds.astype(BF16)
            dh_ref[:, cs] = (_dot(dsb, mkv_ref[:, cs], 1, 0) * scale).astype(BF16)
            dmkv_ref[:, cs] += _dot(dsb, q, 0, 0) * scale
            dmkv_ref[:, vs] += _dot(p.astype(BF16), dob, 0, 0)

    return pl.pallas_call(
        body, name="mem_bwd",
        out_shape=(jax.ShapeDtypeStruct(dh_in.shape, BF16), jax.ShapeDtypeStruct((N_MEM, 2 * M_W), F32)),
        grid=(s // MB,),
        in_specs=[pl.BlockSpec((MB, M_W), lambda i: (i, M_BASE // M_W)),
                  pl.BlockSpec((MB, M_W), lambda i: (i, M_BASE // M_W + 1)),
                  pl.BlockSpec((MB, M_W), lambda i: (i, 1536 // M_W)),
                  pl.BlockSpec((MB, M_W), lambda i: (i, 1536 // M_W)),
                  pl.BlockSpec((N_MEM, 2 * M_W), lambda i: (0, 0)), pl.BlockSpec(memory_space=pl.ANY)],
        out_specs=(pl.BlockSpec((MB, 2 * M_W), lambda i: (i, M_BASE // (2 * M_W))),
                   pl.BlockSpec((N_MEM, 2 * M_W), lambda i: (0, 0))),
        input_output_aliases={5: 0},
        compiler_params=_params(("arbitrary",)),
    )(h, h, r, dy, mkv, dh_in)


OB = 256


def _outproj_ln(h, r, w_out, x, ln_g, ln_b):
    s = h.shape[0]

    def body(za_ref, zb_ref, zm_ref, r_ref, w_ref, x_ref, g_ref, b_ref, xn_ref, xh_ref, rstd_ref, y_ref):
        z = jnp.concatenate([za_ref[...], zb_ref[...], zm_ref[...]], axis=1).astype(F32)
        y = (r_ref[...].astype(F32) * (z * _sigmoid(z))).astype(BF16)
        y_ref[...] = y
        u = ALPHA * x_ref[...] + _dot(y, w_ref[...], 1, 0)
        mu = jnp.mean(u, axis=1, keepdims=True)
        uc = u - mu
        rstd = lax.rsqrt(jnp.mean(uc * uc, axis=1, keepdims=True) + LN_EPS)
        xh = uc * rstd
        xh_ref[...] = xh
        rstd_ref[...] = rstd
        xn_ref[...] = xh * g_ref[...] + b_ref[...]

    row = lambda w, c: pl.BlockSpec((OB, w), lambda i: (i, c))
    vec = pl.BlockSpec((1, D), lambda i: (0, 0))
    return pl.pallas_call(
        body, name="outproj_ln",
        out_shape=(jax.ShapeDtypeStruct((s, D), F32), jax.ShapeDtypeStruct((s, D), F32),
                   jax.ShapeDtypeStruct((s, 1), F32), jax.ShapeDtypeStruct((s, D), BF16)),
        grid=(s // OB,),
        in_specs=[row(A_W, A_Z // A_W), row(B_W, (B_BASE + 1024) // B_W), row(M_W, (M_BASE + M_W) // M_W), row(D, 0),
                  pl.BlockSpec((D, D), lambda i: (0, 0)), row(D, 0), vec, vec],
        out_specs=(row(D, 0), row(D, 0), pl.BlockSpec((OB, 1), lambda i: (i, 0)), row(D, 0)),
        compiler_params=_params(("arbitrary",), 56),
    )(h, h, h, r, w_out, x, ln_g, ln_b)


def _ln_bwd(g, xh, rstd, ln_g):
    s = g.shape[0]

    def body(g_ref, xh_ref, rstd_ref, lg_ref, du_ref, dg_ref, db_ref):
        @pl.when(pl.program_id(0) == 0)
        def _():
            dg_ref[...] = jnp.zeros_like(dg_ref)
            db_ref[...] = jnp.zeros_like(db_ref)

        gv, xh = g_ref[...], xh_ref[...]
        dg_ref[...] += jnp.sum(gv * xh, axis=0, keepdims=True)
        db_ref[...] += jnp.sum(gv, axis=0, keepdims=True)
        dxh = gv * lg_ref[...]
        du_ref[...] = rstd_ref[...] * (dxh - jnp.mean(dxh, axis=1, keepdims=True)
                                       - xh * jnp.mean(dxh * xh, axis=1, keepdims=True))

    row = pl.BlockSpec((OB, D), lambda i: (i, 0))
    vec = pl.BlockSpec((1, D), lambda i: (0, 0))
    return pl.pallas_call(
        body, name="ln_bwd",
        out_shape=(jax.ShapeDtypeStruct((s, D), F32), jax.ShapeDtypeStruct((1, D), F32), jax.ShapeDtypeStruct((1, D), F32)),
        grid=(s // OB,),
        in_specs=[row, row, pl.BlockSpec((OB, 1), lambda i: (i, 0)), vec],
        out_specs=(row, vec, vec),
        compiler_params=_params(("arbitrary",)),
    )(g, xh, rstd, ln_g)


def _loss_grad(y, target):
    s = y.shape[0]

    def body(y_ref, t_ref, l_ref, dy_ref):
        @pl.when(pl.program_id(0) == 0)
        def _():
            l_ref[...] = jnp.zeros_like(l_ref)

        e = y_ref[...] - t_ref[...]
        dy_ref[...] = e / D
        l_ref[...] += 0.5 * jnp.sum(jnp.mean(e * e, axis=1, keepdims=True))

    row = pl.BlockSpec((OB, D), lambda i: (i, 0))
    return pl.pallas_call(
        body, name="loss_grad",
        out_shape=(jax.ShapeDtypeStruct((1, 128), F32), jax.ShapeDtypeStruct((s, D), F32)),
        grid=(s // OB,),
        in_specs=[row, row],
        out_specs=(pl.BlockSpec((1, 128), lambda i: (0, 0)), row),
        compiler_params=_params(("arbitrary",)),
    )(y, target)


def _local_step(x, mem, target, w_in_p, w_out_f, w_kv_f, rel, gate_w, gate_b, norm_g, ln_g, ln_b):
    depth = len(w_in_p)
    s = x.shape[0]
    saved = []
    xl = x
    for l in range(depth):
        hmat = _mm(xl, w_in_p[l], out_dtype=BF16, tm=1024, tn=512, tk=D, name="in_proj")
        mkv = _mm(mem, w_kv_f[l], out_dtype=BF16, tm=N_MEM, tn=1024, tk=D, name="mem_kv")
        bias = _band_bias(rel[l])
        gw = jnp.zeros((128, B_KW), F32).at[:GATE_RANK].set(gate_w[l]).astype(BF16)
        gb, ng = gate_b[l][None, :], norm_g[l][None, :]
        r = jnp.zeros((s, D), BF16)
        r, lse = _attn_fwd(hmat, r, bias)
        r, opre, states = _gla_fwd(hmat, r, gw, gb, ng)
        r = _mem_fwd(hmat, r, mkv)
        xn, xh, rstd, y = _outproj_ln(hmat, r, w_out_f[l], xl, ln_g[l][None, :], ln_b[l][None, :])
        saved.append(dict(x=xl, h=hmat, mkv=mkv, bias=bias, gw=gw, gb=gb, ng=ng, r=r, lse=lse, opre=opre,
                          states=states, xh=xh, rstd=rstd, y=y))
        xl = xn
    loss, g = _loss_grad(xl, target)

    grads = [None] * depth
    for l in reversed(range(depth)):
        sv = saved[l]
        du, d_lng, d_lnb = _ln_bwd(g, sv["xh"], sv["rstd"], ln_g[l][None, :])
        d_wout = _mm(sv["y"], du, ta=True, out_dtype=F32, tm=1024, tn=1024, tk=1024, name="d_w_out")
        dy = _mm(du, w_out_f[l], tb=True, out_dtype=BF16, tm=1024, tn=1024, tk=D, name="d_y")
        dh = jnp.zeros((s, H_W), BF16)
        dh, dbias = _attn_bwd(sv["h"], sv["r"], dy, sv["lse"], sv["bias"], dh)
        dh, d_gw, d_gb, d_ng = _gla_bwd(sv["h"], dy, sv["opre"], sv["states"], sv["gw"], sv["gb"], sv["ng"], dh)
        dh, d_mkv = _mem_bwd(sv["h"], sv["r"], dy, sv["mkv"], dh)
        d_wkv = _mm(mem, d_mkv, ta=True, out_dtype=F32, tm=1024, tn=1024, tk=N_MEM, name="d_w_kv")
        d_win = _mm(sv["x"], dh, ta=True, out_dtype=F32, tm=1024, tn=1792, tk=1024, name="d_w_in")
        g = _mm(dh, w_in_p[l], tb=True, out_dtype=F32, tm=1024, tn=1024, tk=1792, name="d_x", adds=((du, ALPHA),))
        grads[l] = dict(w_in=d_win, w_out=d_wout, w_kv=d_wkv, rel=_bias_grad(dbias), gate_w=d_gw[:GATE_RANK],
                        gate_b=d_gb[0], norm_g=d_ng[0], ln_g=d_lng[0], ln_b=d_lnb[0])
    return loss, g, grads


MESH = pl.DeviceIdType.MESH


def _exchange(xs, scatter, name):
    n = len(xs)

    def body(*refs):
        x_refs, o_refs = refs[:n], refs[n:2 * n]
        send_sems, recv_sems, local_sems = refs[2 * n:]
        mx, my, mc = lax.axis_index("x"), lax.axis_index("y"), lax.axis_index("c")
        me = 4 * mx + 2 * my + mc
        peers = []
        for k in range(1, N_DEV):
            px = 1 - mx if k & 4 else mx
            py = 1 - my if k & 2 else my
            pc = 1 - mc if k & 1 else mc
            peers.append(((px, py, pc), 4 * px + 2 * py + pc))

        def remote(a, k):
            dev, idx = peers[k]
            return pltpu.make_async_remote_copy(
                src_ref=x_refs[a].at[idx] if scatter else x_refs[a], dst_ref=o_refs[a].at[me],
                send_sem=send_sems.at[a, k], recv_sem=recv_sems.at[a, k], device_id=dev, device_id_type=MESH)

        def arrival(a, k):
            dev, idx = peers[k]
            return pltpu.make_async_remote_copy(
                src_ref=x_refs[a].at[idx] if scatter else x_refs[a], dst_ref=o_refs[a].at[idx],
                send_sem=send_sems.at[a, k], recv_sem=recv_sems.at[a, k], device_id=dev, device_id_type=MESH)

        own = [pltpu.make_async_copy(x_refs[a].at[me] if scatter else x_refs[a], o_refs[a].at[me], local_sems.at[a])
               for a in range(n)]
        for cp in own:
            cp.start()
        sends = [remote(a, k) for k in range(N_DEV - 1) for a in range(n)]
        for cp in sends:
            cp.start()
        for cp in sends:
            cp.wait_send()
        for k in range(N_DEV - 1):
            for a in range(n):
                arrival(a, k).wait_recv()
        for cp in own:
            cp.wait()

    hbm = pl.BlockSpec(memory_space=pltpu.HBM)
    out_shapes = tuple(jax.ShapeDtypeStruct((N_DEV,) + (x.shape[1:] if scatter else x.shape), x.dtype) for x in xs)
    return pl.pallas_call(
        body, name=name,
        out_shape=out_shapes,
        in_specs=[hbm] * n, out_specs=tuple([hbm] * n),
        scratch_shapes=[pltpu.SemaphoreType.DMA((n, N_DEV - 1)), pltpu.SemaphoreType.DMA((n, N_DEV - 1)),
                        pltpu.SemaphoreType.DMA((n,))],
    )(*xs)


def _adamw(parts, w, m, v, rows_per_step, name):
    n, rows, cols = parts.shape
    tr = min(rows_per_step, rows)
    assert rows % tr == 0

    def body(p_ref, w_ref, m_ref, v_ref, g_ref, d_ref, nm_ref, nv_ref):
        g = p_ref[0]
        for j in range(1, n):
            g = g + p_ref[j]
        nm = ADAM_B1 * m_ref[...] + (1.0 - ADAM_B1) * g
        nv = ADAM_B2 * v_ref[...] + (1.0 - ADAM_B2) * (g * g)
        m_hat = nm / (1.0 - ADAM_B1 ** ADAM_STEP)
        v_hat = nv / (1.0 - ADAM_B2 ** ADAM_STEP)
        g_ref[...] = g
        nm_ref[...] = nm
        nv_ref[...] = nv
        d_ref[...] = -ADAM_LR * (m_hat / (jnp.sqrt(v_hat) + ADAM_EPS) + ADAM_WD * w_ref[...])

    blk = pl.BlockSpec((tr, cols), lambda i: (i, 0))
    shape = jax.ShapeDtypeStruct((rows, cols), F32)
    return pl.pallas_call(
        body, name=name,
        out_shape=(shape, shape, shape, shape),
        grid=(rows // tr,),
        in_specs=[pl.BlockSpec((n, tr, cols), lambda i: (0, i, 0)), blk, blk, blk],
        out_specs=(blk, blk, blk, blk),
        compiler_params=_params(("parallel",)),
    )(parts, w, m, v)


SMALL = (("rel", A_HEADS * (2 * MAX_REL + 1)), ("gate_w", GATE_RANK * B_KW), ("gate_b", B_KW), ("norm_g", B_DV),
         ("ln_g", D), ("ln_b", D))


def _pack_small(parts, depth):
    rows = []
    for name, size in SMALL:
        flat = parts[name].reshape(depth * size).astype(F32)
        rows.append(jnp.pad(flat, (0, -(depth * size) % 128)).reshape(-1, 128))
    packed = jnp.concatenate(rows, axis=0)
    return jnp.pad(packed, ((0, -packed.shape[0] % 8), (0, 0)))


def _unpack_small(packed, depth, shapes):
    out, row = {}, 0
    for name, size in SMALL:
        nrow = -(-(depth * size) // 128)
        out[name] = packed[row:row + nrow].reshape(-1)[:depth * size].reshape(shapes[name])
        row += nrow
    return out


def kernel(x, mem, w_in, a_rel_bias, b_gate_w, b_gate_b, b_norm_g, w_mem_kv, w_out, ln_g, ln_b, loss_target, m_w_in, m_a_rel_bias, m_b_gate_w, m_b_gate_b, m_b_norm_g, m_w_mem_kv, m_w_out, m_ln_g, m_ln_b, v_w_in, v_a_rel_bias, v_b_gate_w, v_b_gate_b, v_b_norm_g, v_w_mem_kv, v_w_out, v_ln_g, v_ln_b):
    depth = w_in.shape[0]
    sh_in = w_in.shape[2]
    sh_gw = b_gate_w.shape[2]
    me = 4 * lax.axis_index("x") + 2 * lax.axis_index("y") + lax.axis_index("c")

    g_in, g_out, g_kv, g_gw = _exchange(
        [w_in.astype(BF16), w_out.astype(BF16), w_mem_kv.astype(BF16), b_gate_w], False, "gather_weights")
    w_in_p, w_out_f, w_kv_f = [], [], []
    for l in range(depth):
        nat = jnp.transpose(g_in[:, l], (1, 0, 2)).reshape(D, N_DEV * sh_in)
        w_in_p.append(jnp.concatenate([nat[:, :NAT_SPLIT], jnp.zeros((D, H_PAD), BF16), nat[:, NAT_SPLIT:]], axis=1))
        w_out_f.append(g_out[:, l].reshape(D, D))
        w_kv_f.append(g_kv[:, l].reshape(D, 2 * M_W))
    gate_w_full = jnp.transpose(g_gw, (1, 2, 0, 3)).reshape(depth, GATE_RANK, N_DEV * sh_gw)

    loss_dev, dx, grads = _local_step(x[0], mem[0], loss_target[0], w_in_p, w_out_f, w_kv_f,
                                      a_rel_bias, gate_w_full, b_gate_b, b_norm_g, ln_g, ln_b)
    loss = lax.psum(loss_dev[0, 0], ("x", "y", "c"))

    big = {"w_in": [], "w_kv": [], "w_out": []}
    for l in range(depth):
        gr = grads[l]
        nat = jnp.concatenate([gr["w_in"][:, :NAT_SPLIT], gr["w_in"][:, NAT_SPLIT + H_PAD:]], axis=1)
        blocks_in = jnp.transpose(nat.reshape(D, N_DEV, sh_in), (1, 0, 2))
        r_in, r_out, r_kv = _exchange(
            [blocks_in, gr["w_out"].reshape(N_DEV, D // N_DEV, D), gr["w_kv"].reshape(N_DEV, D // N_DEV, 2 * M_W)],
            True, "scatter_grads")
        big["w_in"].append(_adamw(r_in, w_in[l], m_w_in[l], v_w_in[l], 128, "adamw_w_in"))
        big["w_out"].append(_adamw(r_out, w_out[l], m_w_out[l], v_w_out[l], 64, "adamw_w_out"))
        big["w_kv"].append(_adamw(r_kv, w_mem_kv[l], m_w_mem_kv[l], v_w_mem_kv[l], 128, "adamw_w_kv"))
    big = {n: [jnp.stack([res[l][t] for l in range(depth)]) for t in range(4)] for n, res in big.items()}

    shapes = {"rel": a_rel_bias.shape, "gate_w": (depth, GATE_RANK, N_DEV * sh_gw), "gate_b": b_gate_b.shape,
              "norm_g": b_norm_g.shape, "ln_g": ln_g.shape, "ln_b": ln_b.shape}
    part = _pack_small({n: jnp.stack([grads[l][n] for l in range(depth)]) for n, _ in SMALL}, depth)
    (all_parts,) = _exchange([part], False, "gather_small")
    zeros_gw = jnp.zeros(shapes["gate_w"], F32)
    w_s = _pack_small(dict(rel=a_rel_bias, gate_w=zeros_gw, gate_b=b_gate_b, norm_g=b_norm_g, ln_g=ln_g, ln_b=ln_b), depth)
    m_s = _pack_small(dict(rel=m_a_rel_bias, gate_w=zeros_gw, gate_b=m_b_gate_b, norm_g=m_b_norm_g, ln_g=m_ln_g, ln_b=m_ln_b), depth)
    v_s = _pack_small(dict(rel=v_a_rel_bias, gate_w=zeros_gw, gate_b=v_b_gate_b, norm_g=v_b_norm_g, ln_g=v_ln_g, ln_b=v_ln_b), depth)
    small = [_unpack_small(t, depth, shapes) for t in _adamw(all_parts, w_s, m_s, v_s, all_parts.shape[1], "adamw_small")]
    gw_grad = lax.dynamic_slice_in_dim(small[0]["gate_w"], me * sh_gw, sh_gw, axis=2).reshape(1, depth * GATE_RANK, sh_gw)
    flat = lambda t: t.reshape(depth * GATE_RANK, sh_gw)
    gw_res = [t.reshape(depth, GATE_RANK, sh_gw)
              for t in _adamw(gw_grad, flat(b_gate_w), flat(m_b_gate_w), flat(v_b_gate_w), depth * GATE_RANK, "adamw_gate_w")]

    def leaves(t):
        return (big["w_in"][t], small[t]["rel"], gw_res[t], small[t]["gate_b"], small[t]["norm_g"],
                big["w_kv"][t], big["w_out"][t], small[t]["ln_g"], small[t]["ln_b"])

    return (loss, dx[None]) + leaves(0) + leaves(1) + leaves(2) + leaves(3)
```

```python
import functools
import math

import numpy as np
import jax
import jax.numpy as jnp
from jax import lax
from jax.experimental import pallas as pl
from jax.experimental.pallas import tpu as pltpu

F32 = jnp.float32
BF16 = jnp.bfloat16

N_DEV = 8
D = 2048
CH = 64
LEFT = 8
MAX_REL = 128
N_MEM = 256
A_HEADS, A_DH, A_W = 8, 128, 1024
B_HEADS, B_DK, B_DV, B_KW, B_W = 4, 64, 128, 256, 512
GATE_RANK, GATE_TAU = 16, 16.0
M_HEADS, M_DH, M_W = 4, 128, 512
IN_W = 6672
NAT_SPLIT = 5648
H_W = 7168
H_PAD = H_W - IN_W
A_Q, A_K, A_V, A_Z = 0, 1024, 2048, 3072
B_BASE = 4096
M_BASE = 6144
ALPHA = (2.0 * 4) ** 0.25
LN_EPS = 1e-5
RMS_EPS = 1e-6
NEG = -1e30
QB = 256
KB = 3 * QB
ADAM_LR, ADAM_B1, ADAM_B2, ADAM_EPS, ADAM_WD, ADAM_STEP = 0.001, 0.9, 0.999, 1e-08, 0.01, 10
VMEM_MB = 1024 * 1024


def _params(sem, vmem_mb=48):
    return pltpu.CompilerParams(dimension_semantics=sem, vmem_limit_bytes=vmem_mb * VMEM_MB)


def _sigmoid(x):
    return 1.0 / (1.0 + jnp.exp(-x))


def _dot(a, b, ca, cb, precision=None):
    return lax.dot_general(a, b, (((ca,), (cb,)), ((), ())), preferred_element_type=F32, precision=precision)


MESH = pl.DeviceIdType.MESH


class _Comm:
    def __init__(self, xs, scatter):
        self.xs, self.scatter, self.n = list(xs), scatter, len(xs)
        self.out_shape = [jax.ShapeDtypeStruct((N_DEV,) + (x.shape[1:] if scatter else x.shape), x.dtype) for x in xs]
        self.specs = [pl.BlockSpec(memory_space=pltpu.HBM)] * self.n
        self.scratch = [pltpu.SemaphoreType.DMA((self.n, N_DEV - 1)), pltpu.SemaphoreType.DMA((self.n, N_DEV - 1)),
                        pltpu.SemaphoreType.DMA((self.n,))]

    def _copies(self, x_refs, o_refs, sems):
        send_sems, recv_sems, local_sems = sems
        mx, my, mc = lax.axis_index("x"), lax.axis_index("y"), lax.axis_index("c")
        me = 4 * mx + 2 * my + mc
        sends, arrivals = [], []
        for k in range(N_DEV - 1):
            px = 1 - mx if (k + 1) & 4 else mx
            py = 1 - my if (k + 1) & 2 else my
            pc = 1 - mc if (k + 1) & 1 else mc
            idx = 4 * px + 2 * py + pc
            for a in range(self.n):
                src = x_refs[a].at[idx] if self.scatter else x_refs[a]
                for dst, group in ((o_refs[a].at[me], sends), (o_refs[a].at[idx], arrivals)):
                    group.append(pltpu.make_async_remote_copy(
                        src_ref=src, dst_ref=dst, send_sem=send_sems.at[a, k], recv_sem=recv_sems.at[a, k],
                        device_id=(px, py, pc), device_id_type=MESH))
        own = [pltpu.make_async_copy(x_refs[a].at[me] if self.scatter else x_refs[a], o_refs[a].at[me], local_sems.at[a])
               for a in range(self.n)]
        return own, sends, arrivals

    def start(self, x_refs, o_refs, sems):
        own, sends, _ = self._copies(x_refs, o_refs, sems)
        for cp in own + sends:
            cp.start()

    def finish(self, x_refs, o_refs, sems):
        own, sends, arrivals = self._copies(x_refs, o_refs, sems)
        for cp in sends:
            cp.wait_send()
        for cp in arrivals:
            cp.wait_recv()
        for cp in own:
            cp.wait()


def _exchange(xs, scatter, name):
    comm = _Comm(xs, scatter)

    def body(*refs):
        x_refs, o_refs, sems = refs[:comm.n], refs[comm.n:2 * comm.n], refs[2 * comm.n:]
        comm.start(x_refs, o_refs, sems)
        comm.finish(x_refs, o_refs, sems)

    return pl.pallas_call(body, name=name, out_shape=tuple(comm.out_shape), in_specs=comm.specs,
                          out_specs=tuple(comm.specs), scratch_shapes=comm.scratch)(*comm.xs)


def _hosted(body, comm, *, name, grid, in_specs, out_specs, out_shape, scratch_shapes, compiler_params, inputs,
            input_output_aliases=None):
    out_specs, out_shape = tuple(out_specs), tuple(out_shape)
    aliases = input_output_aliases or {}
    if comm is None:
        outs = pl.pallas_call(body, name=name, grid=grid, in_specs=list(in_specs), out_specs=out_specs, out_shape=out_shape,
                              scratch_shapes=list(scratch_shapes), compiler_params=compiler_params,
                              input_output_aliases=aliases)(*inputs)
        return tuple(outs), ()
    ni, no, ns, nc = len(in_specs), len(out_specs), len(scratch_shapes), comm.n

    def wrapped(*refs):
        ins, c_in = refs[:ni], refs[ni:ni + nc]
        outs, c_out = refs[ni + nc:ni + nc + no], refs[ni + nc + no:ni + 2 * nc + no]
        scr, sems = refs[ni + 2 * nc + no:ni + 2 * nc + no + ns], refs[ni + 2 * nc + no + ns:]
        first = functools.reduce(jnp.logical_and, [pl.program_id(d) == 0 for d in range(len(grid))])
        last = functools.reduce(jnp.logical_and, [pl.program_id(d) == grid[d] - 1 for d in range(len(grid))])

        @pl.when(first)
        def _():
            comm.start(c_in, c_out, sems)

        body(*ins, *outs, *scr)

        @pl.when(last)
        def _():
            comm.finish(c_in, c_out, sems)

    params = pltpu.CompilerParams(dimension_semantics=("arbitrary",) * len(grid),
                                  vmem_limit_bytes=compiler_params.vmem_limit_bytes)
    outs = pl.pallas_call(wrapped, name=name, grid=grid, in_specs=list(in_specs) + comm.specs,
                          out_specs=out_specs + tuple(comm.specs), out_shape=out_shape + tuple(comm.out_shape),
                          scratch_shapes=list(scratch_shapes) + comm.scratch, compiler_params=params,
                          input_output_aliases=aliases)(*inputs, *comm.xs)
    return tuple(outs[:no]), tuple(outs[no:])


def _mm(a, b, *, ta=False, tb=False, out_dtype, tm, tn, tk, name, adds=(), vmem_mb=48, comm=None):
    m = a.shape[1] if ta else a.shape[0]
    k = a.shape[0] if ta else a.shape[1]
    n = b.shape[0] if tb else b.shape[1]
    assert k == (b.shape[1] if tb else b.shape[0])
    tm, tn, tk = min(tm, m), min(tn, n), min(tk, k)
    assert m % tm == 0 and n % tn == 0 and k % tk == 0, (name, m, n, k)
    nk = k // tk
    n_add = len(adds)
    scales = [s for _, s in adds]

    def body(a_ref, b_ref, *rest):
        add_refs, o_ref, acc_ref = rest[:n_add], rest[n_add], rest[n_add + 1]
        p = _dot(a_ref[...].astype(BF16), b_ref[...].astype(BF16), 0 if ta else 1, 1 if tb else 0)
        kk = pl.program_id(2)

        @pl.when(kk == 0)
        def _():
            acc_ref[...] = p

        @pl.when(kk > 0)
        def _():
            acc_ref[...] += p

        @pl.when(kk == nk - 1)
        def _():
            r = acc_ref[...]
            for ref, s in zip(add_refs, scales):
                r = r + s * ref[...].astype(F32)
            o_ref[...] = r.astype(out_dtype)

    a_spec = pl.BlockSpec((tk, tm), lambda i, j, kk: (kk, i)) if ta else pl.BlockSpec((tm, tk), lambda i, j, kk: (i, kk))
    b_spec = pl.BlockSpec((tn, tk), lambda i, j, kk: (j, kk)) if tb else pl.BlockSpec((tk, tn), lambda i, j, kk: (kk, j))
    add_specs = [pl.BlockSpec((tm, tn), lambda i, j, kk: (i, j)) for _ in adds]
    (out,), c_out = _hosted(
        body, comm, name=name,
        out_shape=[jax.ShapeDtypeStruct((m, n), out_dtype)],
        grid=(m // tm, n // tn, nk),
        in_specs=[a_spec, b_spec] + add_specs,
        out_specs=[pl.BlockSpec((tm, tn), lambda i, j, kk: (i, j))],
        scratch_shapes=[pltpu.VMEM((tm, tn), F32)],
        compiler_params=_params(("parallel", "parallel", "arbitrary"), vmem_mb),
        inputs=(a, b, *[x for x, _ in adds]))
    return out, c_out


def _band_bias(table):
    i = np.arange(QB)[:, None]
    j = np.arange(KB)[None, :]
    qc = i // CH + 2 * QB // CH
    kc = j // CH
    valid = (kc <= qc) & (kc >= qc - LEFT)
    n = QB + KB
    c = np.arange(n)
    onehot = np.zeros((2 * MAX_REL + 1, n), np.float32)
    onehot[np.clip(2 * QB - (c - (QB - 1)), -MAX_REL, MAX_REL) + MAX_REL, c] = 1.0
    row = jnp.dot(table.astype(F32), jnp.asarray(onehot), precision=lax.Precision.HIGHEST)
    flow = jnp.tile(row, (1, QB))[:, :QB * (n - 1)].reshape(table.shape[0], QB, n - 1)
    return jnp.where(valid[None], flow[:, :, QB - 1:], NEG)


def _bias_grad(dbias):
    h = dbias.shape[0]
    flipped = dbias[:, ::-1, :]
    padded = jnp.pad(flipped, ((0, 0), (0, 0), (0, 1))).reshape(h, QB * (KB + 1))
    padded = jnp.pad(padded, ((0, 0), (0, (QB + 1) * KB - QB * (KB + 1))))
    diag = padded.reshape(h, QB + 1, KB).sum(axis=1)
    jm = np.arange(KB) - (QB - 1)
    jm = np.where(jm <= -CH, jm + KB, jm)
    didx = np.clip(2 * QB - jm, -MAX_REL, MAX_REL) + MAX_REL
    onehot = np.zeros((KB, 2 * MAX_REL + 1), np.float32)
    onehot[np.arange(KB), didx] = 1.0
    return jnp.dot(diag, jnp.asarray(onehot), precision=lax.Precision.HIGHEST)


def _attn_fwd(h, r_in, bias, comm=None):
    s = h.shape[0]
    nq = s // QB
    scale = A_DH ** -0.5

    def body(q_ref, k0_ref, k1_ref, k2_ref, v0_ref, v1_ref, v2_ref, bias_hbm, r_any, o_ref, lse_ref, bias_ref):
        del r_any
        m = pl.program_id(0)
        k_refs, v_refs = (k0_ref, k1_ref, k2_ref), (v0_ref, v1_ref, v2_ref)

        @pl.when(m == 0)
        def _():
            pltpu.sync_copy(bias_hbm, bias_ref)

        for hd in range(A_HEADS):
            cs = slice(hd * A_DH, (hd + 1) * A_DH)
            q = q_ref[:, cs]
            parts = []
            for t in range(3):
                sc = _dot(q, k_refs[t][:, cs], 1, 1) * scale
                parts.append(jnp.where(m + t - 2 >= 0, sc, NEG))
            sc = jnp.concatenate(parts, axis=1) + bias_ref[hd]
            mx = jnp.max(sc, axis=1, keepdims=True)
            p = jnp.exp(sc - mx)
            l = jnp.sum(p, axis=1, keepdims=True)
            pb = p.astype(BF16)
            o = _dot(pb[:, :QB], v_refs[0][:, cs], 1, 0)
            o += _dot(pb[:, QB:2 * QB], v_refs[1][:, cs], 1, 0)
            o += _dot(pb[:, 2 * QB:], v_refs[2][:, cs], 1, 0)
            o_ref[:, cs] = (o / l).astype(BF16)
            lse_ref[:, hd:hd + 1] = mx + jnp.log(l)

    def kv_spec(col, t):
        return pl.BlockSpec((QB, A_W), lambda m: (jnp.maximum(m + t - 2, 0), col))

    (r, lse), c_out = _hosted(
        body, comm, name="attn_fwd",
        out_shape=(jax.ShapeDtypeStruct(r_in.shape, BF16), jax.ShapeDtypeStruct((s, A_HEADS), F32)),
        grid=(nq,),
        in_specs=[pl.BlockSpec((QB, A_W), lambda m: (m, 0))]
        + [kv_spec(1, t) for t in range(3)] + [kv_spec(2, t) for t in range(3)]
        + [pl.BlockSpec(memory_space=pl.ANY), pl.BlockSpec(memory_space=pl.ANY)],
        out_specs=(pl.BlockSpec((QB, A_W), lambda m: (m, 0)), pl.BlockSpec((QB, A_HEADS), lambda m: (m, 0))),
        scratch_shapes=[pltpu.VMEM((A_HEADS, QB, KB), F32)],
        input_output_aliases={8: 0},
        compiler_params=_params(("arbitrary",)),
        inputs=(h, h, h, h, h, h, h, bias, r_in))
    return r, lse, c_out


def _attn_bwd(h, r, dy, lse, bias, dh_in, comm=None):
    s = h.shape[0]
    nq = s // QB
    scale = A_DH ** -0.5

    def body(q_ref, k0_ref, k1_ref, k2_ref, v0_ref, v1_ref, v2_ref, z_ref, r_ref, dy_ref, lse_ref,
             zl_ref, rl_ref, dyl_ref, bias_hbm, dh_any, dh_ref, dbias_hbm, dk_acc, dv_acc, dq_hist, bias_ref, dbias_ref):
        del dh_any
        m = pl.program_id(0)
        k_refs, v_refs = (k0_ref, k1_ref, k2_ref), (v0_ref, v1_ref, v2_ref)

        @pl.when(m == 0)
        def _():
            dk_acc[...] = jnp.zeros_like(dk_acc)
            dv_acc[...] = jnp.zeros_like(dv_acc)
            dq_hist[...] = jnp.zeros_like(dq_hist)
            dbias_ref[...] = jnp.zeros_like(dbias_ref)
            pltpu.sync_copy(bias_hbm, bias_ref)

        @pl.when(m < nq)
        def _():
            z = z_ref[...].astype(F32)
            do_all = dy_ref[...].astype(F32) * (z * _sigmoid(z))
            o_all = r_ref[...].astype(F32)
            for hd in range(A_HEADS):
                cs = slice(hd * A_DH, (hd + 1) * A_DH)
                q = q_ref[:, cs]
                do = do_all[:, cs]
                dob = do.astype(BF16)
                delta = jnp.sum(do * o_all[:, cs], axis=1, keepdims=True)
                parts, dps = [], []
                for t in range(3):
                    sc = _dot(q, k_refs[t][:, cs], 1, 1) * scale
                    parts.append(jnp.where(m + t - 2 >= 0, sc, NEG))
                    dps.append(_dot(dob, v_refs[t][:, cs], 1, 1))
                sc = jnp.concatenate(parts, axis=1) + bias_ref[hd]
                p = jnp.exp(sc - lse_ref[:, hd:hd + 1])
                ds = p * (jnp.concatenate(dps, axis=1) - delta)
                dbias_ref[hd] += ds
                pb, dsb = p.astype(BF16), ds.astype(BF16)
                dq = jnp.zeros((QB, A_DH), F32)
                for t in range(3):
                    ts = slice(t * QB, (t + 1) * QB)
                    dq += _dot(dsb[:, ts], k_refs[t][:, cs], 1, 0)
                    dk_acc[ts, cs] += _dot(dsb[:, ts], q, 0, 0) * scale
                    dv_acc[ts, cs] += _dot(pb[:, ts], dob, 0, 0)
                dq_hist[2, :, cs] = dq * scale

        zl = zl_ref[...].astype(F32)
        sg = _sigmoid(zl)
        dz = dyl_ref[...].astype(F32) * rl_ref[...].astype(F32) * (sg * (1.0 + zl * (1.0 - sg)))
        dh_ref[:, A_Q:A_Q + A_W] = dq_hist[0].astype(BF16)
        dh_ref[:, A_K:A_K + A_W] = dk_acc[0:QB, :].astype(BF16)
        dh_ref[:, A_V:A_V + A_W] = dv_acc[0:QB, :].astype(BF16)
        dh_ref[:, A_Z:A_Z + A_W] = dz.astype(BF16)
        dq_hist[0] = dq_hist[1]
        dq_hist[1] = dq_hist[2]
        dk_acc[0:2 * QB, :] = dk_acc[QB:3 * QB, :]
        dv_acc[0:2 * QB, :] = dv_acc[QB:3 * QB, :]
        dk_acc[2 * QB:3 * QB, :] = jnp.zeros((QB, A_W), F32)
        dv_acc[2 * QB:3 * QB, :] = jnp.zeros((QB, A_W), F32)

        @pl.when(m == nq + 1)
        def _():
            pltpu.sync_copy(dbias_ref, dbias_hbm)

    last = nq - 1

    def cur(col):
        return pl.BlockSpec((QB, A_W), lambda m: (jnp.minimum(m, last), col))

    def kv_spec(col, t):
        return pl.BlockSpec((QB, A_W), lambda m: (jnp.clip(m + t - 2, 0, last), col))

    def lag(col):
        return pl.BlockSpec((QB, A_W), lambda m: (jnp.clip(m - 2, 0, last), col))

    (dh, dbias), c_out = _hosted(
        body, comm, name="attn_bwd",
        out_shape=(jax.ShapeDtypeStruct(dh_in.shape, BF16), jax.ShapeDtypeStruct((A_HEADS, QB, KB), F32)),
        grid=(nq + 2,),
        in_specs=[cur(0)] + [kv_spec(1, t) for t in range(3)] + [kv_spec(2, t) for t in range(3)]
        + [cur(3), cur(0), cur(0), pl.BlockSpec((QB, A_HEADS), lambda m: (jnp.minimum(m, last), 0)),
           lag(3), lag(0), lag(0),
           pl.BlockSpec(memory_space=pl.ANY), pl.BlockSpec(memory_space=pl.ANY)],
        out_specs=(pl.BlockSpec((QB, 4 * A_W), lambda m: (jnp.clip(m - 2, 0, last), 0)),
                   pl.BlockSpec(memory_space=pl.ANY)),
        scratch_shapes=[pltpu.VMEM((KB, A_W), F32), pltpu.VMEM((KB, A_W), F32), pltpu.VMEM((3, QB, A_W), F32),
                        pltpu.VMEM((A_HEADS, QB, KB), F32), pltpu.VMEM((A_HEADS, QB, KB), F32)],
        input_output_aliases={15: 0},
        compiler_params=_params(("arbitrary",), 56),
        inputs=(h, h, h, h, h, h, h, h, r, dy, lse, h, r, dy, bias, dh_in))
    return dh, dbias, c_out


GB = 256
N_PAIR = B_HEADS // 2


def _gla_gates(lr, gw_ref, gb_ref):
    logit = _dot(lr, gw_ref[...], 1, 0) + gb_ref[...]
    lg = (jnp.minimum(logit, 0.0) - jnp.log(1.0 + jnp.exp(-jnp.abs(logit)))) / GATE_TAU
    row = lax.broadcasted_iota(jnp.int32, (GB, GB), 0)
    col = lax.broadcasted_iota(jnp.int32, (GB, GB), 1)
    tri = jnp.where((row // CH == col // CH) & (col <= row), 1.0, 0.0).astype(F32)
    return logit, _dot(tri, lg, 1, 0, precision=lax.Precision.HIGHEST)


def _gla_factors(hb_ref, b_all, c):
    rs = slice(c * CH, (c + 1) * CH)
    q = hb_ref[rs, 0:B_KW].astype(F32) * (B_DK ** -0.5)
    k = hb_ref[rs, B_KW:2 * B_KW].astype(F32)
    b = b_all[rs]
    bm, bl = b[CH // 2:CH // 2 + 1, :], b[CH - 1:CH, :]
    e1, e2, eb, ek = jnp.exp(b - bm), jnp.exp(bm - b), jnp.exp(b), jnp.exp(bl - b)
    el = jnp.exp(bl)
    return dict(ql=q * e1, kl=k * e2, qu=q * e2, ku=k * e1, qt=q * eb, kh=k * ek, e1=e1, e2=e2, eb=eb, ek=ek, el=el)


def _gla_fwd(h, r_in, gw, gb, ng):
    s = h.shape[0]
    nb = s // GB
    cpb = GB // CH

    def body(hb_ref, gw_ref, gb_ref, ng_ref, r_any, o_ref, opre_ref, st_ref, state):
        del r_any

        @pl.when(pl.program_id(0) == 0)
        def _():
            state[...] = jnp.zeros_like(state)

        _, b_all = _gla_gates(hb_ref[:, 1536:1664], gw_ref, gb_ref)
        lane = lax.broadcasted_iota(jnp.int32, (CH, 128), 1)
        ri = lax.broadcasted_iota(jnp.int32, (CH, CH), 0)
        ci = lax.broadcasted_iota(jnp.int32, (CH, CH), 1)
        for c in range(cpb):
            rs = slice(c * CH, (c + 1) * CH)
            f = _gla_factors(hb_ref, b_all, c)
            for p in range(N_PAIR):
                ls = slice(p * 128, (p + 1) * 128)
                st = state[p]
                st_ref[c, p] = st
                stb = st.astype(BF16)
                klp, kup = f["kl"][:, ls].astype(BF16), f["ku"][:, ls].astype(BF16)
                upd = st * f["el"][:, ls]
                for sh in range(2):
                    hd = 2 * p + sh
                    msk = (lane // B_DK) == sh
                    qlm = jnp.where(msk, f["ql"][:, ls], 0.0).astype(BF16)
                    qum = jnp.where(msk, f["qu"][:, ls], 0.0).astype(BF16)
                    qtm = jnp.where(msk, f["qt"][:, ls], 0.0).astype(BF16)
                    khm = jnp.where(msk, f["kh"][:, ls], 0.0).astype(BF16)
                    v = hb_ref[rs, 512 + hd * B_DV:512 + (hd + 1) * B_DV]
                    a = jnp.where(ri >= ci, _dot(qlm, klp, 1, 1), _dot(qum, kup, 1, 1))
                    o = _dot(a.astype(BF16), v, 1, 0) + _dot(qtm, stb, 1, 1)
                    upd = upd + _dot(v, khm, 0, 0)
                    hs = slice(hd * B_DV, (hd + 1) * B_DV)
                    opre_ref[rs, hs] = o
                    rinv = lax.rsqrt(jnp.mean(o * o, axis=1, keepdims=True) + RMS_EPS)
                    o_ref[rs, hs] = (o * rinv * ng_ref[...]).astype(BF16)
                state[p] = upd

    return pl.pallas_call(
        body, name="gla_fwd",
        out_shape=(jax.ShapeDtypeStruct(r_in.shape, BF16), jax.ShapeDtypeStruct((s, B_W), F32),
                   jax.ShapeDtypeStruct((s // CH, N_PAIR, 128, 128), F32)),
        grid=(nb,),
        in_specs=[pl.BlockSpec((GB, 2048), lambda i: (i, B_BASE // 2048)),
                  pl.BlockSpec((128, B_KW), lambda i: (0, 0)), pl.BlockSpec((1, B_KW), lambda i: (0, 0)),
                  pl.BlockSpec((1, B_DV), lambda i: (0, 0)), pl.BlockSpec(memory_space=pl.ANY)],
        out_specs=(pl.BlockSpec((GB, B_W), lambda i: (i, 1024 // B_W)), pl.BlockSpec((GB, B_W), lambda i: (i, 0)),
                   pl.BlockSpec((cpb, N_PAIR, 128, 128), lambda i: (i, 0, 0, 0))),
        scratch_shapes=[pltpu.VMEM((N_PAIR, 128, 128), F32)],
        input_output_aliases={4: 0},
        compiler_params=_params(("arbitrary",)),
    )(h, gw, gb, ng, r_in)


def _gla_bwd(h, dy, opre, states, gw, gb, ng, dh_in):
    s = h.shape[0]
    nb = s // GB
    cpb = GB // CH

    def body(hb_ref, dy_ref, opre_ref, st_ref, gw_ref, gb_ref, ng_ref, dh_any,
             dh_ref, dgw_ref, dgb_ref, dng_ref, dstate, db_scr, do_scr):
        del dh_any

        @pl.when(pl.program_id(0) == 0)
        def _():
            dstate[...] = jnp.zeros_like(dstate)
            dgw_ref[...] = jnp.zeros_like(dgw_ref)
            dgb_ref[...] = jnp.zeros_like(dgb_ref)
            dng_ref[...] = jnp.zeros_like(dng_ref)

        lr = hb_ref[:, 1536:1664]
        logit, b_all = _gla_gates(lr, gw_ref, gb_ref)
        z = hb_ref[:, 1024:1536].astype(F32)
        sg = _sigmoid(z)
        dyb = dy_ref[...].astype(F32)
        dng = jnp.zeros((1, B_DV), F32)
        for hd in range(B_HEADS):
            hs = slice(hd * B_DV, (hd + 1) * B_DV)
            o = opre_ref[:, hs]
            rinv = lax.rsqrt(jnp.mean(o * o, axis=1, keepdims=True) + RMS_EPS)
            on = o * rinv
            dr = dyb[:, hs] * (z[:, hs] * sg[:, hs])
            dh_ref[:, 1024 + hd * B_DV:1024 + (hd + 1) * B_DV] = (
                dyb[:, hs] * (on * ng_ref[...]) * (sg[:, hs] * (1.0 + z[:, hs] * (1.0 - sg[:, hs])))).astype(BF16)
            dng += jnp.sum(dr * on, axis=0, keepdims=True)
            dn = dr * ng_ref[...]
            do_scr[:, hs] = rinv * (dn - on * jnp.mean(dn * on, axis=1, keepdims=True))
        dng_ref[...] += dng

        lane = lax.broadcasted_iota(jnp.int32, (CH, 128), 1)
        ri = lax.broadcasted_iota(jnp.int32, (CH, CH), 0)
        ci = lax.broadcasted_iota(jnp.int32, (CH, CH), 1)
        rowi = lax.broadcasted_iota(jnp.int32, (CH, 128), 0)
        for c in reversed(range(cpb)):
            rs = slice(c * CH, (c + 1) * CH)
            f = _gla_factors(hb_ref, b_all, c)
            for p in range(N_PAIR):
                ls = slice(p * 128, (p + 1) * 128)
                fp = {n: x[:, ls] for n, x in f.items()}
                st_prev = st_ref[c, p]
                dst_new = dstate[p]
                stb, dstb = st_prev.astype(BF16), dst_new.astype(BF16)
                klp, kup = fp["kl"].astype(BF16), fp["ku"].astype(BF16)
                dst_prev = dst_new * fp["el"]
                db_last = fp["el"] * jnp.sum(dst_new * st_prev, axis=0, keepdims=True)
                d_ql = jnp.zeros((CH, 128), F32)
                d_qu, d_qt, d_kl, d_ku, d_kh = d_ql, d_ql, d_ql, d_ql, d_ql
                for sh in range(2):
                    hd = 2 * p + sh
                    msk = (lane // B_DK) == sh
                    qlm = jnp.where(msk, fp["ql"], 0.0).astype(BF16)
                    qum = jnp.where(msk, fp["qu"], 0.0).astype(BF16)
                    qtm = jnp.where(msk, fp["qt"], 0.0).astype(BF16)
                    khm = jnp.where(msk, fp["kh"], 0.0).astype(BF16)
                    v = hb_ref[rs, 512 + hd * B_DV:512 + (hd + 1) * B_DV]
                    dob = do_scr[rs, hd * B_DV:(hd + 1) * B_DV].astype(BF16)
                    at = jnp.where(ci >= ri, _dot(klp, qlm, 1, 1), _dot(kup, qum, 1, 1))
                    da = _dot(dob, v, 1, 1)
                    dat = _dot(v, dob, 1, 1)
                    dl = jnp.where(ri >= ci, da, 0.0).astype(BF16)
                    du = jnp.where(ri < ci, da, 0.0).astype(BF16)
                    dlt = jnp.where(ci >= ri, dat, 0.0).astype(BF16)
                    dut = jnp.where(ci < ri, dat, 0.0).astype(BF16)
                    dv = _dot(at.astype(BF16), dob, 1, 0) + _dot(khm, dstb, 1, 1)
                    dh_ref[rs, 512 + hd * B_DV:512 + (hd + 1) * B_DV] = dv.astype(BF16)
                    d_qt += jnp.where(msk, _dot(dob, stb, 1, 0), 0.0)
                    d_kh += jnp.where(msk, _dot(v, dstb, 1, 0), 0.0)
                    d_ql += jnp.where(msk, _dot(dl, klp, 1, 0), 0.0)
                    d_qu += jnp.where(msk, _dot(du, kup, 1, 0), 0.0)
                    d_kl += _dot(dlt, qlm, 1, 0)
                    d_ku += _dot(dut, qum, 1, 0)
                    dst_prev += _dot(dob, qtm, 0, 0)
                dstate[p] = dst_prev
                dq = (d_ql * fp["e1"] + d_qu * fp["e2"] + d_qt * fp["eb"]) * (B_DK ** -0.5)
                dk = d_kl * fp["e2"] + d_ku * fp["e1"] + d_kh * fp["ek"]
                dkh_kh = d_kh * fp["kh"]
                db = d_ql * fp["ql"] - d_qu * fp["qu"] + d_qt * fp["qt"] - d_kl * fp["kl"] + d_ku * fp["ku"] - dkh_kh
                db_last += jnp.sum(dkh_kh, axis=0, keepdims=True)
                db = jnp.where(rowi == CH - 1, db + db_last, db)
                dh_ref[rs, p * 128:(p + 1) * 128] = dq.astype(BF16)
                dh_ref[rs, B_KW + p * 128:B_KW + (p + 1) * 128] = dk.astype(BF16)
                db_scr[rs, ls] = db

        row = lax.broadcasted_iota(jnp.int32, (GB, GB), 0)
        col = lax.broadcasted_iota(jnp.int32, (GB, GB), 1)
        trit = jnp.where((row // CH == col // CH) & (col >= row), 1.0, 0.0).astype(F32)
        dlg = _dot(trit, db_scr[...], 1, 0, precision=lax.Precision.HIGHEST)
        dlogit = dlg * (_sigmoid(-logit) / GATE_TAU)
        dlb = dlogit.astype(BF16)
        dgw_ref[...] += _dot(lr, dlb, 0, 0)
        dgb_ref[...] += jnp.sum(dlogit, axis=0, keepdims=True)
        dh_ref[:, 1536:1664] = _dot(dlb, gw_ref[...], 1, 1).astype(BF16)
        dh_ref[:, 1664:2048] = jnp.zeros((GB, 384), BF16)

    rev = lambda i: nb - 1 - i
    return pl.pallas_call(
        body, name="gla_bwd",
        out_shape=(jax.ShapeDtypeStruct(dh_in.shape, BF16), jax.ShapeDtypeStruct((128, B_KW), F32),
                   jax.ShapeDtypeStruct((1, B_KW), F32), jax.ShapeDtypeStruct((1, B_DV), F32)),
        grid=(nb,),
        in_specs=[pl.BlockSpec((GB, 2048), lambda i: (rev(i), B_BASE // 2048)),
                  pl.BlockSpec((GB, B_W), lambda i: (rev(i), 1024 // B_W)),
                  pl.BlockSpec((GB, B_W), lambda i: (rev(i), 0)),
                  pl.BlockSpec((cpb, N_PAIR, 128, 128), lambda i: (rev(i), 0, 0, 0)),
                  pl.BlockSpec((128, B_KW), lambda i: (0, 0)), pl.BlockSpec((1, B_KW), lambda i: (0, 0)),
                  pl.BlockSpec((1, B_DV), lambda i: (0, 0)), pl.BlockSpec(memory_space=pl.ANY)],
        out_specs=(pl.BlockSpec((GB, 2048), lambda i: (rev(i), B_BASE // 2048)),
                   pl.BlockSpec((128, B_KW), lambda i: (0, 0)), pl.BlockSpec((1, B_KW), lambda i: (0, 0)),
                   pl.BlockSpec((1, B_DV), lambda i: (0, 0))),
        scratch_shapes=[pltpu.VMEM((N_PAIR, 128, 128), F32), pltpu.VMEM((GB, B_KW), F32), pltpu.VMEM((GB, B_W), F32)],
        input_output_aliases={7: 0},
        compiler_params=_params(("arbitrary",)),
    )(h, dy, opre, states, gw, gb, ng, dh_in)


MB = 512


def _mem_probs(q, mk, scale):
    sc = _dot(q, mk, 1, 1) * scale
    p = jnp.exp(sc - jnp.max(sc, axis=1, keepdims=True))
    return p / jnp.sum(p, axis=1, keepdims=True)


def _mem_fwd(h, r_in, mkv):
    s = h.shape[0]
    scale = M_DH ** -0.5

    def body(q_ref, mkv_ref, r_any, o_ref):
        del r_any
        for hd in range(M_HEADS):
            cs = slice(hd * M_DH, (hd + 1) * M_DH)
            p = _mem_probs(q_ref[:, cs], mkv_ref[:, cs], scale)
            o_ref[:, cs] = _dot(p.astype(BF16), mkv_ref[:, M_W + hd * M_DH:M_W + (hd + 1) * M_DH], 1, 0).astype(BF16)

    return pl.pallas_call(
        body, name="mem_fwd",
        out_shape=jax.ShapeDtypeStruct(r_in.shape, BF16),
        grid=(s // MB,),
        in_specs=[pl.BlockSpec((MB, M_W), lambda i: (i, M_BASE // M_W)),
                  pl.BlockSpec((N_MEM, 2 * M_W), lambda i: (0, 0)), pl.BlockSpec(memory_space=pl.ANY)],
        out_specs=pl.BlockSpec((MB, M_W), lambda i: (i, 1536 // M_W)),
        input_output_aliases={2: 0},
        compiler_params=_params(("arbitrary",)),
    )(h, mkv, r_in)


def _mem_bwd(h, r, dy, mkv, dh_in):
    s = h.shape[0]
    scale = M_DH ** -0.5

    def body(q_ref, z_ref, r_ref, dy_ref, mkv_ref, dh_any, dh_ref, dmkv_ref):
        del dh_any

        @pl.when(pl.program_id(0) == 0)
        def _():
            dmkv_ref[...] = jnp.zeros_like(dmkv_ref)

        z = z_ref[...].astype(F32)
        sg = _sigmoid(z)
        dyv = dy_ref[...].astype(F32)
        do_all = dyv * (z * sg)
        dh_ref[:, M_W:2 * M_W] = (dyv * r_ref[...].astype(F32) * (sg * (1.0 + z * (1.0 - sg)))).astype(BF16)
        for hd in range(M_HEADS):
            cs = slice(hd * M_DH, (hd + 1) * M_DH)
            vs = slice(M_W + hd * M_DH, M_W + (hd + 1) * M_DH)
            q = q_ref[:, cs]
            p = _mem_probs(q, mkv_ref[:, cs], scale)
            dob = do_all[:, cs].astype(BF16)
            dp = _dot(dob, mkv_ref[:, vs], 1, 1)
            ds = p * (dp - jnp.sum(p * dp, axis=1, keepdims=True))
            dsb = ds.astype(BF16)
            dh_ref[:, cs] = (_dot(dsb, mkv_ref[:, cs], 1, 0) * scale).astype(BF16)
            dmkv_ref[:, cs] += _dot(dsb, q, 0, 0) * scale
            dmkv_ref[:, vs] += _dot(p.astype(BF16), dob, 0, 0)

    return pl.pallas_call(
        body, name="mem_bwd",
        out_shape=(jax.ShapeDtypeStruct(dh_in.shape, BF16), jax.ShapeDtypeStruct((N_MEM, 2 * M_W), F32)),
        grid=(s // MB,),
        in_specs=[pl.BlockSpec((MB, M_W), lambda i: (i, M_BASE // M_W)),
                  pl.BlockSpec((MB, M_W), lambda i: (i, M_BASE // M_W + 1)),
                  pl.BlockSpec((MB, M_W), lambda i: (i, 1536 // M_W)),
                  pl.BlockSpec((MB, M_W), lambda i: (i, 1536 // M_W)),
                  pl.BlockSpec((N_MEM, 2 * M_W), lambda i: (0, 0)), pl.BlockSpec(memory_space=pl.ANY)],
        out_specs=(pl.BlockSpec((MB, 2 * M_W), lambda i: (i, M_BASE // (2 * M_W))),
                   pl.BlockSpec((N_MEM, 2 * M_W), lambda i: (0, 0))),
        input_output_aliases={5: 0},
        compiler_params=_params(("arbitrary",)),
    )(h, h, r, dy, mkv, dh_in)


OB = 256


def _outproj_ln(h, r, w_out, x, ln_g, ln_b):
    s = h.shape[0]

    def body(za_ref, zb_ref, zm_ref, r_ref, w_ref, x_ref, g_ref, b_ref, xn_ref, xh_ref, rstd_ref, y_ref):
        z = jnp.concatenate([za_ref[...], zb_ref[...], zm_ref[...]], axis=1).astype(F32)
        y = (r_ref[...].astype(F32) * (z * _sigmoid(z))).astype(BF16)
        y_ref[...] = y
        u = ALPHA * x_ref[...] + _dot(y, w_ref[...], 1, 0)
        mu = jnp.mean(u, axis=1, keepdims=True)
        uc = u - mu
        rstd = lax.rsqrt(jnp.mean(uc * uc, axis=1, keepdims=True) + LN_EPS)
        xh = uc * rstd
        xh_ref[...] = xh
        rstd_ref[...] = rstd
        xn_ref[...] = xh * g_ref[...] + b_ref[...]

    row = lambda w, c: pl.BlockSpec((OB, w), lambda i: (i, c))
    vec = pl.BlockSpec((1, D), lambda i: (0, 0))
    return pl.pallas_call(
        body, name="outproj_ln",
        out_shape=(jax.ShapeDtypeStruct((s, D), F32), jax.ShapeDtypeStruct((s, D), F32),
                   jax.ShapeDtypeStruct((s, 1), F32), jax.ShapeDtypeStruct((s, D), BF16)),
        grid=(s // OB,),
        in_specs=[row(A_W, A_Z // A_W), row(B_W, (B_BASE + 1024) // B_W), row(M_W, (M_BASE + M_W) // M_W), row(D, 0),
                  pl.BlockSpec((D, D), lambda i: (0, 0)), row(D, 0), vec, vec],
        out_specs=(row(D, 0), row(D, 0), pl.BlockSpec((OB, 1), lambda i: (i, 0)), row(D, 0)),
        compiler_params=_params(("arbitrary",), 56),
    )(h, h, h, r, w_out, x, ln_g, ln_b)


def _ln_bwd(g, xh, rstd, ln_g):
    s = g.shape[0]

    def body(g_ref, xh_ref, rstd_ref, lg_ref, du_ref, dg_ref, db_ref):
        @pl.when(pl.program_id(0) == 0)
        def _():
            dg_ref[...] = jnp.zeros_like(dg_ref)
            db_ref[...] = jnp.zeros_like(db_ref)

        gv, xh = g_ref[...], xh_ref[...]
        dg_ref[...] += jnp.sum(gv * xh, axis=0, keepdims=True)
        db_ref[...] += jnp.sum(gv, axis=0, keepdims=True)
        dxh = gv * lg_ref[...]
        du_ref[...] = rstd_ref[...] * (dxh - jnp.mean(dxh, axis=1, keepdims=True)
                                       - xh * jnp.mean(dxh * xh, axis=1, keepdims=True))

    row = pl.BlockSpec((OB, D), lambda i: (i, 0))
    vec = pl.BlockSpec((1, D), lambda i: (0, 0))
    return pl.pallas_call(
        body, name="ln_bwd",
        out_shape=(jax.ShapeDtypeStruct((s, D), F32), jax.ShapeDtypeStruct((1, D), F32), jax.ShapeDtypeStruct((1, D), F32)),
        grid=(s // OB,),
        in_specs=[row, row, pl.BlockSpec((OB, 1), lambda i: (i, 0)), vec],
        out_specs=(row, vec, vec),
        compiler_params=_params(("arbitrary",)),
    )(g, xh, rstd, ln_g)


def _loss_grad(y, target):
    s = y.shape[0]

    def body(y_ref, t_ref, l_ref, dy_ref):
        @pl.when(pl.program_id(0) == 0)
        def _():
            l_ref[...] = jnp.zeros_like(l_ref)

        e = y_ref[...] - t_ref[...]
        dy_ref[...] = e / D
        l_ref[...] += 0.5 * jnp.sum(jnp.mean(e * e, axis=1, keepdims=True))

    row = pl.BlockSpec((OB, D), lambda i: (i, 0))
    return pl.pallas_call(
        body, name="loss_grad",
        out_shape=(jax.ShapeDtypeStruct((1, 128), F32), jax.ShapeDtypeStruct((s, D), F32)),
        grid=(s // OB,),
        in_specs=[row, row],
        out_specs=(pl.BlockSpec((1, 128), lambda i: (0, 0)), row),
        compiler_params=_params(("arbitrary",)),
    )(y, target)


class _LocalWeights:
    def __init__(self, w_in_p, w_out_f, w_kv_f):
        self.w = list(zip(w_in_p, w_out_f, w_kv_f))
        self.depth = len(self.w)
        self.grads = [dict() for _ in self.w]

    def weights(self, l):
        return self.w[l]

    def host(self, where, l, payload=None):
        if payload is not None:
            self.grads[l][where] = payload
        return None

    def landed(self, where, l, outs):
        pass


def _pad_in(nat):
    return jnp.concatenate([nat[:, :NAT_SPLIT], jnp.zeros((D, H_PAD), nat.dtype), nat[:, NAT_SPLIT:]], axis=1)


def _unpad_in(w):
    return jnp.concatenate([w[:, :NAT_SPLIT], w[:, NAT_SPLIT + H_PAD:]], axis=1)


class _Fsdp:
    def __init__(self, w_in, w_out, w_kv, extra):
        self.sh = (w_in, w_out, w_kv)
        self.depth = w_in.shape[0]
        self.raw = [dict() for _ in range(self.depth)]
        self.recv = [dict() for _ in range(self.depth)]
        g_in, g_out, g_kv, *self.extra = _exchange([w_in[0], w_out[0], w_kv[0]] + list(extra), False, "gather_layer0")
        self.raw[0] = dict(w_in=g_in, w_out=g_out, w_kv=g_kv)

    def weights(self, l):
        raw = self.raw[l]
        nat = jnp.transpose(raw["w_in"], (1, 0, 2)).reshape(D, IN_W)
        return _pad_in(nat), raw["w_out"].reshape(D, D), raw["w_kv"].reshape(D, 2 * M_W)

    def host(self, where, l, payload=None):
        w_in, w_out, w_kv = self.sh
        if where == "in_proj" and l + 1 < self.depth:
            return _Comm([w_in[l + 1]], False)
        if where == "attn_fwd" and l + 1 < self.depth:
            return _Comm([w_out[l + 1], w_kv[l + 1]], False)
        if where == "attn_bwd":
            return _Comm([payload.reshape(N_DEV, D // N_DEV, D)], True)
        if where == "d_w_in":
            return _Comm([payload.reshape(N_DEV, D // N_DEV, 2 * M_W)], True)
        if where == "d_x":
            nat = _unpad_in(payload)
            return _Comm([jnp.transpose(nat.reshape(D, N_DEV, IN_W // N_DEV), (1, 0, 2))], True)
        return None

    def landed(self, where, l, outs):
        if where == "in_proj" and outs:
            self.raw[l + 1]["w_in"] = outs[0]
        elif where == "attn_fwd" and outs:
            self.raw[l + 1]["w_out"], self.raw[l + 1]["w_kv"] = outs
        elif where == "attn_bwd":
            self.recv[l]["w_out"] = outs[0]
        elif where == "d_w_in":
            self.recv[l]["w_kv"] = outs[0]
        elif where == "d_x":
            self.recv[l]["w_in"] = outs[0]


def _local_step(x, mem, target, pipe, rel, gate_w, gate_b, norm_g, ln_g, ln_b):
    depth = pipe.depth
    s = x.shape[0]
    saved = []
    xl = x
    for l in range(depth):
        w_in_p, w_out_f, w_kv_f = pipe.weights(l)
        hmat, landed = _mm(xl, w_in_p, out_dtype=BF16, tm=1024, tn=512, tk=D, name="in_proj", comm=pipe.host("in_proj", l))
        pipe.landed("in_proj", l, landed)
        mkv, _ = _mm(mem, w_kv_f, out_dtype=BF16, tm=N_MEM, tn=1024, tk=D, name="mem_kv")
        bias = _band_bias(rel[l])
        gw = jnp.zeros((128, B_KW), F32).at[:GATE_RANK].set(gate_w[l]).astype(BF16)
        gb, ng = gate_b[l][None, :], norm_g[l][None, :]
        r, lse, landed = _attn_fwd(hmat, lax.empty((s, D), BF16), bias, pipe.host("attn_fwd", l))
        pipe.landed("attn_fwd", l, landed)
        r, opre, states = _gla_fwd(hmat, r, gw, gb, ng)
        r = _mem_fwd(hmat, r, mkv)
        xn, xh, rstd, y = _outproj_ln(hmat, r, w_out_f, xl, ln_g[l][None, :], ln_b[l][None, :])
        saved.append(dict(x=xl, h=hmat, mkv=mkv, bias=bias, gw=gw, gb=gb, ng=ng, r=r, lse=lse, opre=opre,
                          states=states, xh=xh, rstd=rstd, y=y, w_in_p=w_in_p, w_out_f=w_out_f))
        xl = xn
    loss, g = _loss_grad(xl, target)

    grads = [None] * depth
    for l in reversed(range(depth)):
        sv = saved[l]
        du, d_lng, d_lnb = _ln_bwd(g, sv["xh"], sv["rstd"], ln_g[l][None, :])
        d_wout, _ = _mm(sv["y"], du, ta=True, out_dtype=BF16, tm=1024, tn=1024, tk=1024, name="d_w_out")
        dy, _ = _mm(du, sv["w_out_f"], tb=True, out_dtype=BF16, tm=1024, tn=1024, tk=D, name="d_y")
        dh, dbias, landed = _attn_bwd(sv["h"], sv["r"], dy, sv["lse"], sv["bias"], lax.empty((s, H_W), BF16),
                                      pipe.host("attn_bwd", l, d_wout))
        pipe.landed("attn_bwd", l, landed)
        dh, d_gw, d_gb, d_ng = _gla_bwd(sv["h"], dy, sv["opre"], sv["states"], sv["gw"], sv["gb"], sv["ng"], dh)
        dh, d_mkv = _mem_bwd(sv["h"], sv["r"], dy, sv["mkv"], dh)
        d_wkv, _ = _mm(mem, d_mkv, ta=True, out_dtype=BF16, tm=1024, tn=1024, tk=N_MEM, name="d_w_kv")
        d_win, landed = _mm(sv["x"], dh, ta=True, out_dtype=BF16, tm=1024, tn=1792, tk=1024, name="d_w_in",
                            comm=pipe.host("d_w_in", l, d_wkv))
        pipe.landed("d_w_in", l, landed)
        g, landed = _mm(dh, sv["w_in_p"], tb=True, out_dtype=F32, tm=1024, tn=1024, tk=1792, name="d_x",
                        adds=((du, ALPHA),), comm=pipe.host("d_x", l, d_win))
        pipe.landed("d_x", l, landed)
        grads[l] = dict(rel=_bias_grad(dbias), gate_w=d_gw[:GATE_RANK], gate_b=d_gb[0], norm_g=d_ng[0],
                        ln_g=d_lng[0], ln_b=d_lnb[0])
    return loss, g, grads


def _adamw(parts, w, m, v, rows_per_step, name):
    n, rows, cols = parts.shape
    tr = min(rows_per_step, rows)
    assert rows % tr == 0

    def body(p_ref, w_ref, m_ref, v_ref, g_ref, d_ref, nm_ref, nv_ref):
        g = p_ref[0].astype(F32)
        for j in range(1, n):
            g = g + p_ref[j].astype(F32)
        nm = ADAM_B1 * m_ref[...] + (1.0 - ADAM_B1) * g
        nv = ADAM_B2 * v_ref[...] + (1.0 - ADAM_B2) * (g * g)
        m_hat = nm / (1.0 - ADAM_B1 ** ADAM_STEP)
        v_hat = nv / (1.0 - ADAM_B2 ** ADAM_STEP)
        g_ref[...] = g
        nm_ref[...] = nm
        nv_ref[...] = nv
        d_ref[...] = -ADAM_LR * (m_hat / (jnp.sqrt(v_hat) + ADAM_EPS) + ADAM_WD * w_ref[...])

    blk = pl.BlockSpec((tr, cols), lambda i: (i, 0))
    shape = jax.ShapeDtypeStruct((rows, cols), F32)
    return pl.pallas_call(
        body, name=name,
        out_shape=(shape, shape, shape, shape),
        grid=(rows // tr,),
        in_specs=[pl.BlockSpec((n, tr, cols), lambda i: (0, i, 0)), blk, blk, blk],
        out_specs=(blk, blk, blk, blk),
        compiler_params=_params(("parallel",)),
    )(parts, w, m, v)


SMALL = (("rel", A_HEADS * (2 * MAX_REL + 1)), ("gate_w", GATE_RANK * B_KW), ("gate_b", B_KW), ("norm_g", B_DV),
         ("ln_g", D), ("ln_b", D))


def _pack_small(parts, depth):
    rows = []
    for name, size in SMALL:
        flat = parts[name].reshape(depth * size).astype(F32)
        rows.append(jnp.pad(flat, (0, -(depth * size) % 128)).reshape(-1, 128))
    packed = jnp.concatenate(rows, axis=0)
    return jnp.pad(packed, ((0, -packed.shape[0] % 8), (0, 0)))


def _unpack_small(packed, depth, shapes):
    out, row = {}, 0
    for name, size in SMALL:
        nrow = -(-(depth * size) // 128)
        out[name] = packed[row:row + nrow].reshape(-1)[:depth * size].reshape(shapes[name])
        row += nrow
    return out


def kernel(x, mem, w_in, a_rel_bias, b_gate_w, b_gate_b, b_norm_g, w_mem_kv, w_out, ln_g, ln_b, loss_target, m_w_in, m_a_rel_bias, m_b_gate_w, m_b_gate_b, m_b_norm_g, m_w_mem_kv, m_w_out, m_ln_g, m_ln_b, v_w_in, v_a_rel_bias, v_b_gate_w, v_b_gate_b, v_b_norm_g, v_w_mem_kv, v_w_out, v_ln_g, v_ln_b):
    depth = w_in.shape[0]
    sh_in = w_in.shape[2]
    sh_gw = b_gate_w.shape[2]
    me = 4 * lax.axis_index("x") + 2 * lax.axis_index("y") + lax.axis_index("c")

    pipe = _Fsdp(w_in.astype(BF16), w_out.astype(BF16), w_mem_kv.astype(BF16), [b_gate_w])
    gate_w_full = jnp.transpose(pipe.extra[0], (1, 2, 0, 3)).reshape(depth, GATE_RANK, N_DEV * sh_gw)
    loss_dev, dx, grads = _local_step(x[0], mem[0], loss_target[0], pipe,
                                      a_rel_bias, gate_w_full, b_gate_b, b_norm_g, ln_g, ln_b)
    loss = lax.psum(loss_dev[0, 0], ("x", "y", "c"))

    big = {"w_in": [], "w_kv": [], "w_out": []}
    for l in range(depth):
        rc = pipe.recv[l]
        big["w_in"].append(_adamw(rc["w_in"], w_in[l], m_w_in[l], v_w_in[l], 128, "adamw_w_in"))
        big["w_out"].append(_adamw(rc["w_out"], w_out[l], m_w_out[l], v_w_out[l], 64, "adamw_w_out"))
        big["w_kv"].append(_adamw(rc["w_kv"], w_mem_kv[l], m_w_mem_kv[l], v_w_mem_kv[l], 128, "adamw_w_kv"))
    big = {n: [jnp.stack([res[l][t] for l in range(depth)]) for t in range(4)] for n, res in big.items()}

    shapes = {"rel": a_rel_bias.shape, "gate_w": (depth, GATE_RANK, N_DEV * sh_gw), "gate_b": b_gate_b.shape,
              "norm_g": b_norm_g.shape, "ln_g": ln_g.shape, "ln_b": ln_b.shape}
    part = _pack_small({n: jnp.stack([grads[l][n] for l in range(depth)]) for n, _ in SMALL}, depth)
    (all_parts,) = _exchange([part], False, "gather_small")
    zeros_gw = jnp.zeros(shapes["gate_w"], F32)
    w_s = _pack_small(dict(rel=a_rel_bias, gate_w=zeros_gw, gate_b=b_gate_b, norm_g=b_norm_g, ln_g=ln_g, ln_b=ln_b), depth)
    m_s = _pack_small(dict(rel=m_a_rel_bias, gate_w=zeros_gw, gate_b=m_b_gate_b, norm_g=m_b_norm_g, ln_g=m_ln_g, ln_b=m_ln_b), depth)
    v_s = _pack_small(dict(rel=v_a_rel_bias, gate_w=zeros_gw, gate_b=v_b_gate_b, norm_g=v_b_norm_g, ln_g=v_ln_g, ln_b=v_ln_b), depth)
    small = [_unpack_small(t, depth, shapes) for t in _adamw(all_parts, w_s, m_s, v_s, all_parts.shape[1], "adamw_small")]
    gw_grad = lax.dynamic_slice_in_dim(small[0]["gate_w"], me * sh_gw, sh_gw, axis=2).reshape(1, depth * GATE_RANK, sh_gw)
    flat = lambda t: t.reshape(depth * GATE_RANK, sh_gw)
    gw_res = [t.reshape(depth, GATE_RANK, sh_gw)
              for t in _adamw(gw_grad, flat(b_gate_w), flat(m_b_gate_w), flat(v_b_gate_w), depth * GATE_RANK, "adamw_gate_w")]

    def leaves(t):
        return (big["w_in"][t], small[t]["rel"], gw_res[t], small[t]["gate_b"], small[t]["norm_g"],
                big["w_kv"][t], big["w_out"][t], small[t]["ln_g"], small[t]["ln_b"])

    return (loss, dx[None]) + leaves(0) + leaves(1) + leaves(2) + leaves(3)
```

```python
import functools
import math

import numpy as np
import jax
import jax.numpy as jnp
from jax import lax
from jax.experimental import pallas as pl
from jax.experimental.pallas import tpu as pltpu

F32 = jnp.float32
BF16 = jnp.bfloat16

N_DEV = 8
D = 2048
CH = 64
LEFT = 8
MAX_REL = 128
N_MEM = 256
A_HEADS, A_DH, A_W = 8, 128, 1024
B_HEADS, B_DK, B_DV, B_KW, B_W = 4, 64, 128, 256, 512
GATE_RANK, GATE_TAU = 16, 16.0
M_HEADS, M_DH, M_W = 4, 128, 512
IN_W = 6672
NAT_SPLIT = 5648
H_W = 7168
H_PAD = H_W - IN_W
A_Q, A_K, A_V, A_Z = 0, 1024, 2048, 3072
B_BASE = 4096
M_BASE = 6144
ALPHA = (2.0 * 4) ** 0.25
LN_EPS = 1e-5
RMS_EPS = 1e-6
NEG = -1e30
QB = 256
KB = 3 * QB
ADAM_LR, ADAM_B1, ADAM_B2, ADAM_EPS, ADAM_WD, ADAM_STEP = 0.001, 0.9, 0.999, 1e-08, 0.01, 10
VMEM_MB = 1024 * 1024


def _params(sem, vmem_mb=48):
    return pltpu.CompilerParams(dimension_semantics=sem, vmem_limit_bytes=vmem_mb * VMEM_MB)


def _sigmoid(x):
    return 1.0 / (1.0 + jnp.exp(-x))


def _dot(a, b, ca, cb, precision=None):
    return lax.dot_general(a, b, (((ca,), (cb,)), ((), ())), preferred_element_type=F32, precision=precision)


MESH = pl.DeviceIdType.MESH


class _Comm:
    def __init__(self, xs, scatter):
        self.xs, self.scatter, self.n = list(xs), scatter, len(xs)
        self.out_shape = [jax.ShapeDtypeStruct((N_DEV,) + (x.shape[1:] if scatter else x.shape), x.dtype) for x in xs]
        self.specs = [pl.BlockSpec(memory_space=pltpu.HBM)] * self.n
        self.scratch = [pltpu.SemaphoreType.DMA((self.n, N_DEV - 1)), pltpu.SemaphoreType.DMA((self.n, N_DEV - 1)),
                        pltpu.SemaphoreType.DMA((self.n,))]

    def _copies(self, x_refs, o_refs, sems):
        send_sems, recv_sems, local_sems = sems
        mx, my, mc = lax.axis_index("x"), lax.axis_index("y"), lax.axis_index("c")
        me = 4 * mx + 2 * my + mc
        sends, arrivals = [], []
        for k in range(N_DEV - 1):
            px = 1 - mx if (k + 1) & 4 else mx
            py = 1 - my if (k + 1) & 2 else my
            pc = 1 - mc if (k + 1) & 1 else mc
            idx = 4 * px + 2 * py + pc
            for a in range(self.n):
                src = x_refs[a].at[idx] if self.scatter else x_refs[a]
                for dst, group in ((o_refs[a].at[me], sends), (o_refs[a].at[idx], arrivals)):
                    group.append(pltpu.make_async_remote_copy(
                        src_ref=src, dst_ref=dst, send_sem=send_sems.at[a, k], recv_sem=recv_sems.at[a, k],
                        device_id=(px, py, pc), device_id_type=MESH))
        own = [pltpu.make_async_copy(x_refs[a].at[me] if self.scatter else x_refs[a], o_refs[a].at[me], local_sems.at[a])
               for a in range(self.n)]
        return own, sends, arrivals

    def start(self, x_refs, o_refs, sems):
        own, sends, _ = self._copies(x_refs, o_refs, sems)
        for cp in own + sends:
            cp.start()

    def finish(self, x_refs, o_refs, sems):
        own, sends, arrivals = self._copies(x_refs, o_refs, sems)
        for cp in sends:
            cp.wait_send()
        for cp in arrivals:
            cp.wait_recv()
        for cp in own:
            cp.wait()


def _exchange(xs, scatter, name):
    comm = _Comm(xs, scatter)

    def body(*refs):
        x_refs, o_refs, sems = refs[:comm.n], refs[comm.n:2 * comm.n], refs[2 * comm.n:]
        comm.start(x_refs, o_refs, sems)
        comm.finish(x_refs, o_refs, sems)

    return pl.pallas_call(body, name=name, out_shape=tuple(comm.out_shape), in_specs=comm.specs,
                          out_specs=tuple(comm.specs), scratch_shapes=comm.scratch)(*comm.xs)


def _hosted(body, comm, *, name, grid, in_specs, out_specs, out_shape, scratch_shapes, compiler_params, inputs,
            input_output_aliases=None):
    out_specs, out_shape = tuple(out_specs), tuple(out_shape)
    aliases = input_output_aliases or {}
    if comm is None:
        outs = pl.pallas_call(body, name=name, grid=grid, in_specs=list(in_specs), out_specs=out_specs, out_shape=out_shape,
                              scratch_shapes=list(scratch_shapes), compiler_params=compiler_params,
                              input_output_aliases=aliases)(*inputs)
        return tuple(outs), ()
    ni, no, ns, nc = len(in_specs), len(out_specs), len(scratch_shapes), comm.n

    def wrapped(*refs):
        ins, c_in = refs[:ni], refs[ni:ni + nc]
        outs, c_out = refs[ni + nc:ni + nc + no], refs[ni + nc + no:ni + 2 * nc + no]
        scr, sems = refs[ni + 2 * nc + no:ni + 2 * nc + no + ns], refs[ni + 2 * nc + no + ns:]
        first = functools.reduce(jnp.logical_and, [pl.program_id(d) == 0 for d in range(len(grid))])
        last = functools.reduce(jnp.logical_and, [pl.program_id(d) == grid[d] - 1 for d in range(len(grid))])

        @pl.when(first)
        def _():
            comm.start(c_in, c_out, sems)

        body(*ins, *outs, *scr)

        @pl.when(last)
        def _():
            comm.finish(c_in, c_out, sems)

    params = pltpu.CompilerParams(dimension_semantics=("arbitrary",) * len(grid),
                                  vmem_limit_bytes=compiler_params.vmem_limit_bytes)
    outs = pl.pallas_call(wrapped, name=name, grid=grid, in_specs=list(in_specs) + comm.specs,
                          out_specs=out_specs + tuple(comm.specs), out_shape=out_shape + tuple(comm.out_shape),
                          scratch_shapes=list(scratch_shapes) + comm.scratch, compiler_params=params,
                          input_output_aliases=aliases)(*inputs, *comm.xs)
    return tuple(outs[:no]), tuple(outs[no:])


def _mm(a, b, *, ta=False, tb=False, out_dtype, tm, tn, tk, name, adds=(), vmem_mb=48, comm=None):
    m = a.shape[1] if ta else a.shape[0]
    k = a.shape[0] if ta else a.shape[1]
    n = b.shape[0] if tb else b.shape[1]
    assert k == (b.shape[1] if tb else b.shape[0])
    tm, tn, tk = min(tm, m), min(tn, n), min(tk, k)
    assert m % tm == 0 and n % tn == 0 and k % tk == 0, (name, m, n, k)
    nk = k // tk
    n_add = len(adds)
    scales = [s for _, s in adds]

    def body(a_ref, b_ref, *rest):
        add_refs, o_ref, acc_ref = rest[:n_add], rest[n_add], rest[n_add + 1]
        p = _dot(a_ref[...].astype(BF16), b_ref[...].astype(BF16), 0 if ta else 1, 1 if tb else 0)
        kk = pl.program_id(2)

        @pl.when(kk == 0)
        def _():
            acc_ref[...] = p

        @pl.when(kk > 0)
        def _():
            acc_ref[...] += p

        @pl.when(kk == nk - 1)
        def _():
            r = acc_ref[...]
            for ref, s in zip(add_refs, scales):
                r = r + s * ref[...].astype(F32)
            o_ref[...] = r.astype(out_dtype)

    a_spec = pl.BlockSpec((tk, tm), lambda i, j, kk: (kk, i)) if ta else pl.BlockSpec((tm, tk), lambda i, j, kk: (i, kk))
    b_spec = pl.BlockSpec((tn, tk), lambda i, j, kk: (j, kk)) if tb else pl.BlockSpec((tk, tn), lambda i, j, kk: (kk, j))
    add_specs = [pl.BlockSpec((tm, tn), lambda i, j, kk: (i, j)) for _ in adds]
    (out,), c_out = _hosted(
        body, comm, name=name,
        out_shape=[jax.ShapeDtypeStruct((m, n), out_dtype)],
        grid=(m // tm, n // tn, nk),
        in_specs=[a_spec, b_spec] + add_specs,
        out_specs=[pl.BlockSpec((tm, tn), lambda i, j, kk: (i, j))],
        scratch_shapes=[pltpu.VMEM((tm, tn), F32)],
        compiler_params=_params(("parallel", "parallel", "arbitrary"), vmem_mb),
        inputs=(a, b, *[x for x, _ in adds]))
    return out, c_out


def _band_bias(table):
    i = np.arange(QB)[:, None]
    j = np.arange(KB)[None, :]
    qc = i // CH + 2 * QB // CH
    kc = j // CH
    valid = (kc <= qc) & (kc >= qc - LEFT)
    n = QB + KB
    c = np.arange(n)
    onehot = np.zeros((2 * MAX_REL + 1, n), np.float32)
    onehot[np.clip(2 * QB - (c - (QB - 1)), -MAX_REL, MAX_REL) + MAX_REL, c] = 1.0
    row = jnp.dot(table.astype(F32), jnp.asarray(onehot), precision=lax.Precision.HIGHEST)
    flow = jnp.tile(row, (1, QB))[:, :QB * (n - 1)].reshape(table.shape[0], QB, n - 1)
    return jnp.where(valid[None], flow[:, :, QB - 1:], NEG)


def _bias_grad(dbias):
    h = dbias.shape[0]
    flipped = dbias[:, ::-1, :]
    padded = jnp.pad(flipped, ((0, 0), (0, 0), (0, 1))).reshape(h, QB * (KB + 1))
    padded = jnp.pad(padded, ((0, 0), (0, (QB + 1) * KB - QB * (KB + 1))))
    diag = padded.reshape(h, QB + 1, KB).sum(axis=1)
    jm = np.arange(KB) - (QB - 1)
    jm = np.where(jm <= -CH, jm + KB, jm)
    didx = np.clip(2 * QB - jm, -MAX_REL, MAX_REL) + MAX_REL
    onehot = np.zeros((KB, 2 * MAX_REL + 1), np.float32)
    onehot[np.arange(KB), didx] = 1.0
    return jnp.dot(diag, jnp.asarray(onehot), precision=lax.Precision.HIGHEST)


def _attn_fwd(h, r_in, bias, comm=None):
    s = h.shape[0]
    nq = s // QB
    scale = A_DH ** -0.5

    def body(q_ref, k0_ref, k1_ref, k2_ref, v0_ref, v1_ref, v2_ref, bias_hbm, r_any, o_ref, lse_ref, bias_ref):
        del r_any
        m = pl.program_id(0)
        k_refs, v_refs = (k0_ref, k1_ref, k2_ref), (v0_ref, v1_ref, v2_ref)

        @pl.when(m == 0)
        def _():
            pltpu.sync_copy(bias_hbm, bias_ref)

        for hd in range(A_HEADS):
            cs = slice(hd * A_DH, (hd + 1) * A_DH)
            q = q_ref[:, cs]
            parts = []
            for t in range(3):
                sc = _dot(q, k_refs[t][:, cs], 1, 1) * scale
                parts.append(jnp.where(m + t - 2 >= 0, sc, NEG))
            sc = jnp.concatenate(parts, axis=1) + bias_ref[hd]
            mx = jnp.max(sc, axis=1, keepdims=True)
            p = jnp.exp(sc - mx)
            l = jnp.sum(p, axis=1, keepdims=True)
            pb = p.astype(BF16)
            o = _dot(pb[:, :QB], v_refs[0][:, cs], 1, 0)
            o += _dot(pb[:, QB:2 * QB], v_refs[1][:, cs], 1, 0)
            o += _dot(pb[:, 2 * QB:], v_refs[2][:, cs], 1, 0)
            o_ref[:, cs] = (o / l).astype(BF16)
            lse_ref[:, hd:hd + 1] = mx + jnp.log(l)

    def kv_spec(col, t):
        return pl.BlockSpec((QB, A_W), lambda m: (jnp.maximum(m + t - 2, 0), col))

    (r, lse), c_out = _hosted(
        body, comm, name="attn_fwd",
        out_shape=(jax.ShapeDtypeStruct(r_in.shape, BF16), jax.ShapeDtypeStruct((s, A_HEADS), F32)),
        grid=(nq,),
        in_specs=[pl.BlockSpec((QB, A_W), lambda m: (m, 0))]
        + [kv_spec(1, t) for t in range(3)] + [kv_spec(2, t) for t in range(3)]
        + [pl.BlockSpec(memory_space=pl.ANY), pl.BlockSpec(memory_space=pl.ANY)],
        out_specs=(pl.BlockSpec((QB, A_W), lambda m: (m, 0)), pl.BlockSpec((QB, A_HEADS), lambda m: (m, 0))),
        scratch_shapes=[pltpu.VMEM((A_HEADS, QB, KB), F32)],
        input_output_aliases={8: 0},
        compiler_params=_params(("arbitrary",)),
        inputs=(h, h, h, h, h, h, h, bias, r_in))
    return r, lse, c_out


def _attn_bwd(h, r, dy, lse, bias, dh_in, comm=None):
    s = h.shape[0]
    nq = s // QB
    scale = A_DH ** -0.5

    def body(q_ref, k0_ref, k1_ref, k2_ref, v0_ref, v1_ref, v2_ref, z_ref, r_ref, dy_ref, lse_ref,
             zl_ref, rl_ref, dyl_ref, bias_hbm, dh_any, dh_ref, dbias_hbm, dk_acc, dv_acc, dq_hist, bias_ref, dbias_ref):
        del dh_any
        m = pl.program_id(0)
        k_refs, v_refs = (k0_ref, k1_ref, k2_ref), (v0_ref, v1_ref, v2_ref)

        @pl.when(m == 0)
        def _():
            dk_acc[...] = jnp.zeros_like(dk_acc)
            dv_acc[...] = jnp.zeros_like(dv_acc)
            dq_hist[...] = jnp.zeros_like(dq_hist)
            dbias_ref[...] = jnp.zeros_like(dbias_ref)
            pltpu.sync_copy(bias_hbm, bias_ref)

        @pl.when(m < nq)
        def _():
            z = z_ref[...].astype(F32)
            do_all = dy_ref[...].astype(F32) * (z * _sigmoid(z))
            o_all = r_ref[...].astype(F32)
            for hd in range(A_HEADS):
                cs = slice(hd * A_DH, (hd + 1) * A_DH)
                q = q_ref[:, cs]
                do = do_all[:, cs]
                dob = do.astype(BF16)
                delta = jnp.sum(do * o_all[:, cs], axis=1, keepdims=True)
                parts, dps = [], []
                for t in range(3):
                    sc = _dot(q, k_refs[t][:, cs], 1, 1) * scale
                    parts.append(jnp.where(m + t - 2 >= 0, sc, NEG))
                    dps.append(_dot(dob, v_refs[t][:, cs], 1, 1))
                sc = jnp.concatenate(parts, axis=1) + bias_ref[hd]
                p = jnp.exp(sc - lse_ref[:, hd:hd + 1])
                ds = p * (jnp.concatenate(dps, axis=1) - delta)
                dbias_ref[hd] += ds
                pb, dsb = p.astype(BF16), ds.astype(BF16)
                dq = jnp.zeros((QB, A_DH), F32)
                for t in range(3):
                    ts = slice(t * QB, (t + 1) * QB)
                    dq += _dot(dsb[:, ts], k_refs[t][:, cs], 1, 0)
                    dk_acc[ts, cs] += _dot(dsb[:, ts], q, 0, 0) * scale
                    dv_acc[ts, cs] += _dot(pb[:, ts], dob, 0, 0)
                dq_hist[2, :, cs] = dq * scale

        zl = zl_ref[...].astype(F32)
        sg = _sigmoid(zl)
        dz = dyl_ref[...].astype(F32) * rl_ref[...].astype(F32) * (sg * (1.0 + zl * (1.0 - sg)))
        dh_ref[:, A_Q:A_Q + A_W] = dq_hist[0].astype(BF16)
        dh_ref[:, A_K:A_K + A_W] = dk_acc[0:QB, :].astype(BF16)
        dh_ref[:, A_V:A_V + A_W] = dv_acc[0:QB, :].astype(BF16)
        dh_ref[:, A_Z:A_Z + A_W] = dz.astype(BF16)
        dq_hist[0] = dq_hist[1]
        dq_hist[1] = dq_hist[2]
        dk_acc[0:2 * QB, :] = dk_acc[QB:3 * QB, :]
        dv_acc[0:2 * QB, :] = dv_acc[QB:3 * QB, :]
        dk_acc[2 * QB:3 * QB, :] = jnp.zeros((QB, A_W), F32)
        dv_acc[2 * QB:3 * QB, :] = jnp.zeros((QB, A_W), F32)

        @pl.when(m == nq + 1)
        def _():
            pltpu.sync_copy(dbias_ref, dbias_hbm)

    last = nq - 1

    def cur(col):
        return pl.BlockSpec((QB, A_W), lambda m: (jnp.minimum(m, last), col))

    def kv_spec(col, t):
        return pl.BlockSpec((QB, A_W), lambda m: (jnp.clip(m + t - 2, 0, last), col))

    def lag(col):
        return pl.BlockSpec((QB, A_W), lambda m: (jnp.clip(m - 2, 0, last), col))

    (dh, dbias), c_out = _hosted(
        body, comm, name="attn_bwd",
        out_shape=(jax.ShapeDtypeStruct(dh_in.shape, BF16), jax.ShapeDtypeStruct((A_HEADS, QB, KB), F32)),
        grid=(nq + 2,),
        in_specs=[cur(0)] + [kv_spec(1, t) for t in range(3)] + [kv_spec(2, t) for t in range(3)]
        + [cur(3), cur(0), cur(0), pl.BlockSpec((QB, A_HEADS), lambda m: (jnp.minimum(m, last), 0)),
           lag(3), lag(0), lag(0),
           pl.BlockSpec(memory_space=pl.ANY), pl.BlockSpec(memory_space=pl.ANY)],
        out_specs=(pl.BlockSpec((QB, 4 * A_W), lambda m: (jnp.clip(m - 2, 0, last), 0)),
                   pl.BlockSpec(memory_space=pl.ANY)),
        scratch_shapes=[pltpu.VMEM((KB, A_W), F32), pltpu.VMEM((KB, A_W), F32), pltpu.VMEM((3, QB, A_W), F32),
                        pltpu.VMEM((A_HEADS, QB, KB), F32), pltpu.VMEM((A_HEADS, QB, KB), F32)],
        input_output_aliases={15: 0},
        compiler_params=_params(("arbitrary",), 56),
        inputs=(h, h, h, h, h, h, h, h, r, dy, lse, h, r, dy, bias, dh_in))
    return dh, dbias, c_out


GB = 256
N_PAIR = B_HEADS // 2


def _gla_gates(lr, gw_ref, gb_ref):
    logit = _dot(lr, gw_ref[...], 1, 0) + gb_ref[...]
    lg = (jnp.minimum(logit, 0.0) - jnp.log(1.0 + jnp.exp(-jnp.abs(logit)))) / GATE_TAU
    row = lax.broadcasted_iota(jnp.int32, (GB, GB), 0)
    col = lax.broadcasted_iota(jnp.int32, (GB, GB), 1)
    tri = jnp.where((row // CH == col // CH) & (col <= row), 1.0, 0.0).astype(F32)
    return logit, _dot(tri, lg, 1, 0, precision=lax.Precision.HIGHEST)


def _gla_factors(hb_ref, b_all, c):
    rs = slice(c * CH, (c + 1) * CH)
    q = hb_ref[rs, 0:B_KW].astype(F32) * (B_DK ** -0.5)
    k = hb_ref[rs, B_KW:2 * B_KW].astype(F32)
    b = b_all[rs]
    bm, bl = b[CH // 2:CH // 2 + 1, :], b[CH - 1:CH, :]
    e1, e2, eb, ek = jnp.exp(b - bm), jnp.exp(bm - b), jnp.exp(b), jnp.exp(bl - b)
    el = jnp.exp(bl)
    return dict(ql=q * e1, kl=k * e2, qu=q * e2, ku=k * e1, qt=q * eb, kh=k * ek, e1=e1, e2=e2, eb=eb, ek=ek, el=el)


def _gla_fwd(h, r_in, gw, gb, ng):
    s = h.shape[0]
    nb = s // GB
    cpb = GB // CH

    def body(hb_ref, gw_ref, gb_ref, ng_ref, r_any, o_ref, opre_ref, st_ref, state):
        del r_any

        @pl.when(pl.program_id(0) == 0)
        def _():
            state[...] = jnp.zeros_like(state)

        _, b_all = _gla_gates(hb_ref[:, 1536:1664], gw_ref, gb_ref)
        lane = lax.broadcasted_iota(jnp.int32, (CH, 128), 1)
        ri = lax.broadcasted_iota(jnp.int32, (CH, CH), 0)
        ci = lax.broadcasted_iota(jnp.int32, (CH, CH), 1)
        for c in range(cpb):
            rs = slice(c * CH, (c + 1) * CH)
            f = _gla_factors(hb_ref, b_all, c)
            for p in range(N_PAIR):
                ls = slice(p * 128, (p + 1) * 128)
                st = state[p]
                st_ref[c, p] = st
                stb = st.astype(BF16)
                klp, kup = f["kl"][:, ls].astype(BF16), f["ku"][:, ls].astype(BF16)
                upd = st * f["el"][:, ls]
                for sh in range(2):
                    hd = 2 * p + sh
                    msk = (lane // B_DK) == sh
                    qlm = jnp.where(msk, f["ql"][:, ls], 0.0).astype(BF16)
                    qum = jnp.where(msk, f["qu"][:, ls], 0.0).astype(BF16)
                    qtm = jnp.where(msk, f["qt"][:, ls], 0.0).astype(BF16)
                    khm = jnp.where(msk, f["kh"][:, ls], 0.0).astype(BF16)
                    v = hb_ref[rs, 512 + hd * B_DV:512 + (hd + 1) * B_DV]
                    a = jnp.where(ri >= ci, _dot(qlm, klp, 1, 1), _dot(qum, kup, 1, 1))
                    o = _dot(a.astype(BF16), v, 1, 0) + _dot(qtm, stb, 1, 1)
                    upd = upd + _dot(v, khm, 0, 0)
                    hs = slice(hd * B_DV, (hd + 1) * B_DV)
                    opre_ref[rs, hs] = o
                    rinv = lax.rsqrt(jnp.mean(o * o, axis=1, keepdims=True) + RMS_EPS)
                    o_ref[rs, hs] = (o * rinv * ng_ref[...]).astype(BF16)
                state[p] = upd

    return pl.pallas_call(
        body, name="gla_fwd",
        out_shape=(jax.ShapeDtypeStruct(r_in.shape, BF16), jax.ShapeDtypeStruct((s, B_W), F32),
                   jax.ShapeDtypeStruct((s // CH, N_PAIR, 128, 128), F32)),
        grid=(nb,),
        in_specs=[pl.BlockSpec((GB, 2048), lambda i: (i, B_BASE // 2048)),
                  pl.BlockSpec((128, B_KW), lambda i: (0, 0)), pl.BlockSpec((1, B_KW), lambda i: (0, 0)),
                  pl.BlockSpec((1, B_DV), lambda i: (0, 0)), pl.BlockSpec(memory_space=pl.ANY)],
        out_specs=(pl.BlockSpec((GB, B_W), lambda i: (i, 1024 // B_W)), pl.BlockSpec((GB, B_W), lambda i: (i, 0)),
                   pl.BlockSpec((cpb, N_PAIR, 128, 128), lambda i: (i, 0, 0, 0))),
        scratch_shapes=[pltpu.VMEM((N_PAIR, 128, 128), F32)],
        input_output_aliases={4: 0},
        compiler_params=_params(("arbitrary",)),
    )(h, gw, gb, ng, r_in)


def _gla_bwd(h, dy, opre, states, gw, gb, ng, dh_in):
    s = h.shape[0]
    nb = s // GB
    cpb = GB // CH

    def body(hb_ref, dy_ref, opre_ref, st_ref, gw_ref, gb_ref, ng_ref, dh_any,
             dh_ref, dgw_ref, dgb_ref, dng_ref, dstate, db_scr, do_scr):
        del dh_any

        @pl.when(pl.program_id(0) == 0)
        def _():
            dstate[...] = jnp.zeros_like(dstate)
            dgw_ref[...] = jnp.zeros_like(dgw_ref)
            dgb_ref[...] = jnp.zeros_like(dgb_ref)
            dng_ref[...] = jnp.zeros_like(dng_ref)

        lr = hb_ref[:, 1536:1664]
        logit, b_all = _gla_gates(lr, gw_ref, gb_ref)
        z = hb_ref[:, 1024:1536].astype(F32)
        sg = _sigmoid(z)
        dyb = dy_ref[...].astype(F32)
        dng = jnp.zeros((1, B_DV), F32)
        for hd in range(B_HEADS):
            hs = slice(hd * B_DV, (hd + 1) * B_DV)
            o = opre_ref[:, hs]
            rinv = lax.rsqrt(jnp.mean(o * o, axis=1, keepdims=True) + RMS_EPS)
            on = o * rinv
            dr = dyb[:, hs] * (z[:, hs] * sg[:, hs])
            dh_ref[:, 1024 + hd * B_DV:1024 + (hd + 1) * B_DV] = (
                dyb[:, hs] * (on * ng_ref[...]) * (sg[:, hs] * (1.0 + z[:, hs] * (1.0 - sg[:, hs])))).astype(BF16)
            dng += jnp.sum(dr * on, axis=0, keepdims=True)
            dn = dr * ng_ref[...]
            do_scr[:, hs] = rinv * (dn - on * jnp.mean(dn * on, axis=1, keepdims=True))
        dng_ref[...] += dng

        lane = lax.broadcasted_iota(jnp.int32, (CH, 128), 1)
        ri = lax.broadcasted_iota(jnp.int32, (CH, CH), 0)
        ci = lax.broadcasted_iota(jnp.int32, (CH, CH), 1)
        rowi = lax.broadcasted_iota(jnp.int32, (CH, 128), 0)
        for c in reversed(range(cpb)):
            rs = slice(c * CH, (c + 1) * CH)
            f = _gla_factors(hb_ref, b_all, c)
            for p in range(N_PAIR):
                ls = slice(p * 128, (p + 1) * 128)
                fp = {n: x[:, ls] for n, x in f.items()}
                st_prev = st_ref[c, p]
                dst_new = dstate[p]
                stb, dstb = st_prev.astype(BF16), dst_new.astype(BF16)
                klp, kup = fp["kl"].astype(BF16), fp["ku"].astype(BF16)
                dst_prev = dst_new * fp["el"]
                db_last = fp["el"] * jnp.sum(dst_new * st_prev, axis=0, keepdims=True)
                d_ql = jnp.zeros((CH, 128), F32)
                d_qu, d_qt, d_kl, d_ku, d_kh = d_ql, d_ql, d_ql, d_ql, d_ql
                for sh in range(2):
                    hd = 2 * p + sh
                    msk = (lane // B_DK) == sh
                    qlm = jnp.where(msk, fp["ql"], 0.0).astype(BF16)
                    qum = jnp.where(msk, fp["qu"], 0.0).astype(BF16)
                    qtm = jnp.where(msk, fp["qt"], 0.0).astype(BF16)
                    khm = jnp.where(msk, fp["kh"], 0.0).astype(BF16)
                    v = hb_ref[rs, 512 + hd * B_DV:512 + (hd + 1) * B_DV]
                    dob = do_scr[rs, hd * B_DV:(hd + 1) * B_DV].astype(BF16)
                    at = jnp.where(ci >= ri, _dot(klp, qlm, 1, 1), _dot(kup, qum, 1, 1))
                    da = _dot(dob, v, 1, 1)
                    dat = _dot(v, dob, 1, 1)
                    dl = jnp.where(ri >= ci, da, 0.0).astype(BF16)
                    du = jnp.where(ri < ci, da, 0.0).astype(BF16)
                    dlt = jnp.where(ci >= ri, dat, 0.0).astype(BF16)
                    dut = jnp.where(ci < ri, dat, 0.0).astype(BF16)
                    dv = _dot(at.astype(BF16), dob, 1, 0) + _dot(khm, dstb, 1, 1)
                    dh_ref[rs, 512 + hd * B_DV:512 + (hd + 1) * B_DV] = dv.astype(BF16)
                    d_qt += jnp.where(msk, _dot(dob, stb, 1, 0), 0.0)
                    d_kh += jnp.where(msk, _dot(v, dstb, 1, 0), 0.0)
                    d_ql += jnp.where(msk, _dot(dl, klp, 1, 0), 0.0)
                    d_qu += jnp.where(msk, _dot(du, kup, 1, 0), 0.0)
                    d_kl += _dot(dlt, qlm, 1, 0)
                    d_ku += _dot(dut, qum, 1, 0)
                    dst_prev += _dot(dob, qtm, 0, 0)
                dstate[p] = dst_prev
                dq = (d_ql * fp["e1"] + d_qu * fp["e2"] + d_qt * fp["eb"]) * (B_DK ** -0.5)
                dk = d_kl * fp["e2"] + d_ku * fp["e1"] + d_kh * fp["ek"]
                dkh_kh = d_kh * fp["kh"]
                db = d_ql * fp["ql"] - d_qu * fp["qu"] + d_qt * fp["qt"] - d_kl * fp["kl"] + d_ku * fp["ku"] - dkh_kh
                db_last += jnp.sum(dkh_kh, axis=0, keepdims=True)
                db = jnp.where(rowi == CH - 1, db + db_last, db)
                dh_ref[rs, p * 128:(p + 1) * 128] = dq.astype(BF16)
                dh_ref[rs, B_KW + p * 128:B_KW + (p + 1) * 128] = dk.astype(BF16)
                db_scr[rs, ls] = db

        row = lax.broadcasted_iota(jnp.int32, (GB, GB), 0)
        col = lax.broadcasted_iota(jnp.int32, (GB, GB), 1)
        trit = jnp.where((row // CH == col // CH) & (col >= row), 1.0, 0.0).astype(F32)
        dlg = _dot(trit, db_scr[...], 1, 0, precision=lax.Precision.HIGHEST)
        dlogit = dlg * (_sigmoid(-logit) / GATE_TAU)
        dlb = dlogit.astype(BF16)
        dgw_ref[...] += _dot(lr, dlb, 0, 0)
        dgb_ref[...] += jnp.sum(dlogit, axis=0, keepdims=True)
        dh_ref[:, 1536:1664] = _dot(dlb, gw_ref[...], 1, 1).astype(BF16)
        dh_ref[:, 1664:2048] = jnp.zeros((GB, 384), BF16)

    rev = lambda i: nb - 1 - i
    return pl.pallas_call(
        body, name="gla_bwd",
        out_shape=(jax.ShapeDtypeStruct(dh_in.shape, BF16), jax.ShapeDtypeStruct((128, B_KW), F32),
                   jax.ShapeDtypeStruct((1, B_KW), F32), jax.ShapeDtypeStruct((1, B_DV), F32)),
        grid=(nb,),
        in_specs=[pl.BlockSpec((GB, 2048), lambda i: (rev(i), B_BASE // 2048)),
                  pl.BlockSpec((GB, B_W), lambda i: (rev(i), 1024 // B_W)),
                  pl.BlockSpec((GB, B_W), lambda i: (rev(i), 0)),
                  pl.BlockSpec((cpb, N_PAIR, 128, 128), lambda i: (rev(i), 0, 0, 0)),
                  pl.BlockSpec((128, B_KW), lambda i: (0, 0)), pl.BlockSpec((1, B_KW), lambda i: (0, 0)),
                  pl.BlockSpec((1, B_DV), lambda i: (0, 0)), pl.BlockSpec(memory_space=pl.ANY)],
        out_specs=(pl.BlockSpec((GB, 2048), lambda i: (rev(i), B_BASE // 2048)),
                   pl.BlockSpec((128, B_KW), lambda i: (0, 0)), pl.BlockSpec((1, B_KW), lambda i: (0, 0)),
                   pl.BlockSpec((1, B_DV), lambda i: (0, 0))),
        scratch_shapes=[pltpu.VMEM((N_PAIR, 128, 128), F32), pltpu.VMEM((GB, B_KW), F32), pltpu.VMEM((GB, B_W), F32)],
        input_output_aliases={7: 0},
        compiler_params=_params(("arbitrary",)),
    )(h, dy, opre, states, gw, gb, ng, dh_in)


MB = 512


def _mem_probs(q, mk, scale):
    sc = _dot(q, mk, 1, 1) * scale
    p = jnp.exp(sc - jnp.max(sc, axis=1, keepdims=True))
    return p / jnp.sum(p, axis=1, keepdims=True)


def _mem_fwd(h, r_in, mkv):
    s = h.shape[0]
    scale = M_DH ** -0.5

    def body(q_ref, mkv_ref, r_any, o_ref):
        del r_any
        for hd in range(M_HEADS):
            cs = slice(hd * M_DH, (hd + 1) * M_DH)
            p = _mem_probs(q_ref[:, cs], mkv_ref[:, cs], scale)
            o_ref[:, cs] = _dot(p.astype(BF16), mkv_ref[:, M_W + hd * M_DH:M_W + (hd + 1) * M_DH], 1, 0).astype(BF16)

    return pl.pallas_call(
        body, name="mem_fwd",
        out_shape=jax.ShapeDtypeStruct(r_in.shape, BF16),
        grid=(s // MB,),
        in_specs=[pl.BlockSpec((MB, M_W), lambda i: (i, M_BASE // M_W)),
                  pl.BlockSpec((N_MEM, 2 * M_W), lambda i: (0, 0)), pl.BlockSpec(memory_space=pl.ANY)],
        out_specs=pl.BlockSpec((MB, M_W), lambda i: (i, 1536 // M_W)),
        input_output_aliases={2: 0},
        compiler_params=_params(("arbitrary",)),
    )(h, mkv, r_in)


def _mem_bwd(h, r, dy, mkv, dh_in):
    s = h.shape[0]
    scale = M_DH ** -0.5

    def body(q_ref, z_ref, r_ref, dy_ref, mkv_ref, dh_any, dh_ref, dmkv_ref):
        del dh_any

        @pl.when(pl.program_id(0) == 0)
        def _():
            dmkv_ref[...] = jnp.zeros_like(dmkv_ref)

        z = z_ref[...].astype(F32)
        sg = _sigmoid(z)
        dyv = dy_ref[...].astype(F32)
        do_all = dyv * (z * sg)
        dh_ref[:, M_W:2 * M_W] = (dyv * r_ref[...].astype(F32) * (sg * (1.0 + z * (1.0 - sg)))).astype(BF16)
        for hd in range(M_HEADS):
            cs = slice(hd * M_DH, (hd + 1) * M_DH)
            vs = slice(M_W + hd * M_DH, M_W + (hd + 1) * M_DH)
            q = q_ref[:, cs]
            p = _mem_probs(q, mkv_ref[:, cs], scale)
            dob = do_all[:, cs].astype(BF16)
            dp = _dot(dob, mkv_ref[:, vs], 1, 1)
            ds = p * (dp - jnp.sum(p * dp, axis=1, keepdims=True))
            dsb = ds.astype(BF16)
            dh_ref[:, cs] = (_dot(dsb, mkv_ref[:, cs], 1, 0) * scale).astype(BF16)
            dmkv_ref[:, cs] += _dot(dsb, q, 0, 0) * scale
            dmkv_ref[:, vs] += _dot(p.astype(BF16), dob, 0, 0)

    return pl.pallas_call(
        body, name="mem_bwd",
        out_shape=(jax.ShapeDtypeStruct(dh_in.shape, BF16), jax.ShapeDtypeStruct((N_MEM, 2 * M_W), F32)),
        grid=(s // MB,),
        in_specs=[pl.BlockSpec((MB, M_W), lambda i: (i, M_BASE // M_W)),
                  pl.BlockSpec((MB, M_W), lambda i: (i, M_BASE // M_W + 1)),
                  pl.BlockSpec((MB, M_W), lambda i: (i, 1536 // M_W)),
                  pl.BlockSpec((MB, M_W), lambda i: (i, 1536 // M_W)),
                  pl.BlockSpec((N_MEM, 2 * M_W), lambda i: (0, 0)), pl.BlockSpec(memory_space=pl.ANY)],
        out_specs=(pl.BlockSpec((MB, 2 * M_W), lambda i: (i, M_BASE // (2 * M_W))),
                   pl.BlockSpec((N_MEM, 2 * M_W), lambda i: (0, 0))),
        input_output_aliases={5: 0},
        compiler_params=_params(("arbitrary",)),
    )(h, h, r, dy, mkv, dh_in)


OB = 256


def _outproj_ln(h, r, w_out, x, ln_g, ln_b):
    s = h.shape[0]

    def body(za_ref, zb_ref, zm_ref, r_ref, w_ref, x_ref, g_ref, b_ref, xn_ref, xh_ref, rstd_ref, y_ref):
        z = jnp.concatenate([za_ref[...], zb_ref[...], zm_ref[...]], axis=1).astype(F32)
        y = (r_ref[...].astype(F32) * (z * _sigmoid(z))).astype(BF16)
        y_ref[...] = y
        u = ALPHA * x_ref[...] + _dot(y, w_ref[...], 1, 0)
        mu = jnp.mean(u, axis=1, keepdims=True)
        uc = u - mu
        rstd = lax.rsqrt(jnp.mean(uc * uc, axis=1, keepdims=True) + LN_EPS)
        xh = uc * rstd
        xh_ref[...] = xh
        rstd_ref[...] = rstd
        xn_ref[...] = xh * g_ref[...] + b_ref[...]

    row = lambda w, c: pl.BlockSpec((OB, w), lambda i: (i, c))
    vec = pl.BlockSpec((1, D), lambda i: (0, 0))
    return pl.pallas_call(
        body, name="outproj_ln",
        out_shape=(jax.ShapeDtypeStruct((s, D), F32), jax.ShapeDtypeStruct((s, D), F32),
                   jax.ShapeDtypeStruct((s, 1), F32), jax.ShapeDtypeStruct((s, D), BF16)),
        grid=(s // OB,),
        in_specs=[row(A_W, A_Z // A_W), row(B_W, (B_BASE + 1024) // B_W), row(M_W, (M_BASE + M_W) // M_W), row(D, 0),
                  pl.BlockSpec((D, D), lambda i: (0, 0)), row(D, 0), vec, vec],
        out_specs=(row(D, 0), row(D, 0), pl.BlockSpec((OB, 1), lambda i: (i, 0)), row(D, 0)),
        compiler_params=_params(("arbitrary",), 56),
    )(h, h, h, r, w_out, x, ln_g, ln_b)


def _ln_bwd(g, xh, rstd, ln_g):
    s = g.shape[0]

    def body(g_ref, xh_ref, rstd_ref, lg_ref, du_ref, dg_ref, db_ref):
        @pl.when(pl.program_id(0) == 0)
        def _():
            dg_ref[...] = jnp.zeros_like(dg_ref)
            db_ref[...] = jnp.zeros_like(db_ref)

        gv, xh = g_ref[...], xh_ref[...]
        dg_ref[...] += jnp.sum(gv * xh, axis=0, keepdims=True)
        db_ref[...] += jnp.sum(gv, axis=0, keepdims=True)
        dxh = gv * lg_ref[...]
        du_ref[...] = rstd_ref[...] * (dxh - jnp.mean(dxh, axis=1, keepdims=True)
                                       - xh * jnp.mean(dxh * xh, axis=1, keepdims=True))

    row = pl.BlockSpec((OB, D), lambda i: (i, 0))
    vec = pl.BlockSpec((1, D), lambda i: (0, 0))
    return pl.pallas_call(
        body, name="ln_bwd",
        out_shape=(jax.ShapeDtypeStruct((s, D), F32), jax.ShapeDtypeStruct((1, D), F32), jax.ShapeDtypeStruct((1, D), F32)),
        grid=(s // OB,),
        in_specs=[row, row, pl.BlockSpec((OB, 1), lambda i: (i, 0)), vec],
        out_specs=(row, vec, vec),
        compiler_params=_params(("arbitrary",)),
    )(g, xh, rstd, ln_g)


def _loss_grad(y, target):
    s = y.shape[0]

    def body(y_ref, t_ref, l_ref, dy_ref):
        @pl.when(pl.program_id(0) == 0)
        def _():
            l_ref[...] = jnp.zeros_like(l_ref)

        e = y_ref[...] - t_ref[...]
        dy_ref[...] = e / D
        l_ref[...] += 0.5 * jnp.sum(jnp.mean(e * e, axis=1, keepdims=True))

    row = pl.BlockSpec((OB, D), lambda i: (i, 0))
    return pl.pallas_call(
        body, name="loss_grad",
        out_shape=(jax.ShapeDtypeStruct((1, 128), F32), jax.ShapeDtypeStruct((s, D), F32)),
        grid=(s // OB,),
        in_specs=[row, row],
        out_specs=(pl.BlockSpec((1, 128), lambda i: (0, 0)), row),
        compiler_params=_params(("arbitrary",)),
    )(y, target)


class _LocalWeights:
    def __init__(self, w_in_p, w_out_f, w_kv_f):
        self.w = list(zip(w_in_p, w_out_f, w_kv_f))
        self.depth = len(self.w)
        self.grads = [dict() for _ in self.w]

    def weights(self, l):
        return self.w[l]

    def host(self, where, l, payload=None):
        if payload is not None:
            self.grads[l][where] = payload
        return None

    def landed(self, where, l, outs):
        pass


def _shards_to_padded(raw):
    sh = raw.shape[2]
    parts = []
    for j in range(N_DEV):
        lo, hi = j * sh, (j + 1) * sh
        if lo < NAT_SPLIT < hi:
            parts += [raw[j][:, :NAT_SPLIT - lo], jnp.zeros((D, H_PAD), raw.dtype), raw[j][:, NAT_SPLIT - lo:]]
        else:
            parts.append(raw[j])
            if hi == NAT_SPLIT:
                parts.append(jnp.zeros((D, H_PAD), raw.dtype))
    return jnp.concatenate(parts, axis=1)


def _padded_to_shards(w):
    sh = IN_W // N_DEV
    pos = lambda c: c if c < NAT_SPLIT else c + H_PAD
    blocks = []
    for j in range(N_DEV):
        lo, hi = j * sh, (j + 1) * sh
        if lo < NAT_SPLIT < hi:
            blocks.append(jnp.concatenate([w[:, lo:NAT_SPLIT], w[:, pos(NAT_SPLIT):pos(hi - 1) + 1]], axis=1))
        else:
            blocks.append(w[:, pos(lo):pos(lo) + sh])
    return jnp.stack(blocks)


class _Fsdp:
    def __init__(self, w_in, w_out, w_kv, extra):
        self.sh = (w_in, w_out, w_kv)
        self.depth = w_in.shape[0]
        self.raw = [dict() for _ in range(self.depth)]
        self.recv = [dict() for _ in range(self.depth)]
        g_in, g_out, g_kv, *self.extra = _exchange([w_in[0], w_out[0], w_kv[0]] + list(extra), False, "gather_layer0")
        self.raw[0] = dict(w_in=g_in, w_out=g_out, w_kv=g_kv)

    def weights(self, l):
        raw = self.raw[l]
        return _shards_to_padded(raw["w_in"]), raw["w_out"].reshape(D, D), raw["w_kv"].reshape(D, 2 * M_W)

    def host(self, where, l, payload=None):
        w_in, w_out, w_kv = self.sh
        if where == "in_proj" and l + 1 < self.depth:
            return _Comm([w_in[l + 1]], False)
        if where == "attn_fwd" and l + 1 < self.depth:
            return _Comm([w_out[l + 1], w_kv[l + 1]], False)
        if where == "attn_bwd":
            return _Comm([payload.reshape(N_DEV, D // N_DEV, D)], True)
        if where == "d_w_in":
            return _Comm([payload.reshape(N_DEV, D // N_DEV, 2 * M_W)], True)
        if where == "d_x":
            return _Comm([_padded_to_shards(payload)], True)
        return None

    def landed(self, where, l, outs):
        if where == "in_proj" and outs:
            self.raw[l + 1]["w_in"] = outs[0]
        elif where == "attn_fwd" and outs:
            self.raw[l + 1]["w_out"], self.raw[l + 1]["w_kv"] = outs
        elif where == "attn_bwd":
            self.recv[l]["w_out"] = outs[0]
        elif where == "d_w_in":
            self.recv[l]["w_kv"] = outs[0]
        elif where == "d_x":
            self.recv[l]["w_in"] = outs[0]


def _local_step(x, mem, target, pipe, rel, gate_w, gate_b, norm_g, ln_g, ln_b):
    depth = pipe.depth
    s = x.shape[0]
    saved = []
    xl = x
    for l in range(depth):
        w_in_p, w_out_f, w_kv_f = pipe.weights(l)
        hmat, landed = _mm(xl, w_in_p, out_dtype=BF16, tm=1024, tn=512, tk=D, name="in_proj", comm=pipe.host("in_proj", l))
        pipe.landed("in_proj", l, landed)
        mkv, _ = _mm(mem, w_kv_f, out_dtype=BF16, tm=N_MEM, tn=1024, tk=D, name="mem_kv")
        bias = _band_bias(rel[l])
        gw = jnp.zeros((128, B_KW), F32).at[:GATE_RANK].set(gate_w[l]).astype(BF16)
        gb, ng = gate_b[l][None, :], norm_g[l][None, :]
        r, lse, landed = _attn_fwd(hmat, lax.empty((s, D), BF16), bias, pipe.host("attn_fwd", l))
        pipe.landed("attn_fwd", l, landed)
        r, opre, states = _gla_fwd(hmat, r, gw, gb, ng)
        r = _mem_fwd(hmat, r, mkv)
        xn, xh, rstd, y = _outproj_ln(hmat, r, w_out_f, xl, ln_g[l][None, :], ln_b[l][None, :])
        saved.append(dict(x=xl, h=hmat, mkv=mkv, bias=bias, gw=gw, gb=gb, ng=ng, r=r, lse=lse, opre=opre,
                          states=states, xh=xh, rstd=rstd, y=y, w_in_p=w_in_p, w_out_f=w_out_f))
        xl = xn
    loss, g = _loss_grad(xl, target)

    grads = [None] * depth
    for l in reversed(range(depth)):
        sv = saved[l]
        du, d_lng, d_lnb = _ln_bwd(g, sv["xh"], sv["rstd"], ln_g[l][None, :])
        d_wout, _ = _mm(sv["y"], du, ta=True, out_dtype=BF16, tm=1024, tn=1024, tk=1024, name="d_w_out")
        dy, _ = _mm(du, sv["w_out_f"], tb=True, out_dtype=BF16, tm=1024, tn=1024, tk=D, name="d_y")
        dh, dbias, landed = _attn_bwd(sv["h"], sv["r"], dy, sv["lse"], sv["bias"], lax.empty((s, H_W), BF16),
                                      pipe.host("attn_bwd", l, d_wout))
        pipe.landed("attn_bwd", l, landed)
        dh, d_gw, d_gb, d_ng = _gla_bwd(sv["h"], dy, sv["opre"], sv["states"], sv["gw"], sv["gb"], sv["ng"], dh)
        dh, d_mkv = _mem_bwd(sv["h"], sv["r"], dy, sv["mkv"], dh)
        d_wkv, _ = _mm(mem, d_mkv, ta=True, out_dtype=BF16, tm=1024, tn=1024, tk=N_MEM, name="d_w_kv")
        d_win, landed = _mm(sv["x"], dh, ta=True, out_dtype=BF16, tm=1024, tn=1792, tk=1024, name="d_w_in",
                            comm=pipe.host("d_w_in", l, d_wkv))
        pipe.landed("d_w_in", l, landed)
        g, landed = _mm(dh, sv["w_in_p"], tb=True, out_dtype=F32, tm=1024, tn=1024, tk=1792, name="d_x",
                        adds=((du, ALPHA),), comm=pipe.host("d_x", l, d_win))
        pipe.landed("d_x", l, landed)
        grads[l] = dict(rel=_bias_grad(dbias), gate_w=d_gw[:GATE_RANK], gate_b=d_gb[0], norm_g=d_ng[0],
                        ln_g=d_lng[0], ln_b=d_lnb[0])
    return loss, g, grads


def _adamw(parts, w, m, v, rows_per_step, name):
    depth, rows, cols = w.shape
    n = parts[0].shape[0]
    tr = min(rows_per_step, rows)
    assert rows % tr == 0 and len(parts) == depth

    def body(*refs):
        p_refs = refs[:depth]
        w_ref, m_ref, v_ref, g_ref, d_ref, nm_ref, nv_ref = refs[depth:]
        for l in range(depth):
            @pl.when(pl.program_id(0) == l)
            def _(p_ref=p_refs[l]):
                g = p_ref[0].astype(F32)
                for j in range(1, n):
                    g = g + p_ref[j].astype(F32)
                nm = ADAM_B1 * m_ref[...] + (1.0 - ADAM_B1) * g
                nv = ADAM_B2 * v_ref[...] + (1.0 - ADAM_B2) * (g * g)
                m_hat = nm / (1.0 - ADAM_B1 ** ADAM_STEP)
                v_hat = nv / (1.0 - ADAM_B2 ** ADAM_STEP)
                g_ref[...] = g
                nm_ref[...] = nm
                nv_ref[...] = nv
                d_ref[...] = -ADAM_LR * (m_hat / (jnp.sqrt(v_hat) + ADAM_EPS) + ADAM_WD * w_ref[...])

    def part_spec(l):
        return pl.BlockSpec((n, tr, cols), lambda ll, i: (0, jnp.where(ll == l, i, 0), 0))

    blk = pl.BlockSpec((None, tr, cols), lambda ll, i: (ll, i, 0))
    shape = jax.ShapeDtypeStruct((depth, rows, cols), F32)
    return pl.pallas_call(
        body, name=name,
        out_shape=(shape, shape, shape, shape),
        grid=(depth, rows // tr),
        in_specs=[part_spec(l) for l in range(depth)] + [blk, blk, blk],
        out_specs=(blk, blk, blk, blk),
        compiler_params=_params(("arbitrary", "arbitrary")),
    )(*parts, w, m, v)


SMALL = (("rel", A_HEADS * (2 * MAX_REL + 1)), ("gate_w", GATE_RANK * B_KW), ("gate_b", B_KW), ("norm_g", B_DV),
         ("ln_g", D), ("ln_b", D))


def _pack_small(parts, depth):
    rows = []
    for name, size in SMALL:
        flat = parts[name].reshape(depth * size).astype(F32)
        rows.append(jnp.pad(flat, (0, -(depth * size) % 128)).reshape(-1, 128))
    packed = jnp.concatenate(rows, axis=0)
    return jnp.pad(packed, ((0, -packed.shape[0] % 8), (0, 0)))


def _unpack_small(packed, depth, shapes):
    out, row = {}, 0
    for name, size in SMALL:
        nrow = -(-(depth * size) // 128)
        out[name] = packed[row:row + nrow].reshape(-1)[:depth * size].reshape(shapes[name])
        row += nrow
    return out


def kernel(x, mem, w_in, a_rel_bias, b_gate_w, b_gate_b, b_norm_g, w_mem_kv, w_out, ln_g, ln_b, loss_target, m_w_in, m_a_rel_bias, m_b_gate_w, m_b_gate_b, m_b_norm_g, m_w_mem_kv, m_w_out, m_ln_g, m_ln_b, v_w_in, v_a_rel_bias, v_b_gate_w, v_b_gate_b, v_b_norm_g, v_w_mem_kv, v_w_out, v_ln_g, v_ln_b):
    depth = w_in.shape[0]
    sh_in = w_in.shape[2]
    sh_gw = b_gate_w.shape[2]
    me = 4 * lax.axis_index("x") + 2 * lax.axis_index("y") + lax.axis_index("c")

    pipe = _Fsdp(w_in.astype(BF16), w_out.astype(BF16), w_mem_kv.astype(BF16), [b_gate_w])
    gate_w_full = jnp.transpose(pipe.extra[0], (1, 2, 0, 3)).reshape(depth, GATE_RANK, N_DEV * sh_gw)
    loss_dev, dx, grads = _local_step(x[0], mem[0], loss_target[0], pipe,
                                      a_rel_bias, gate_w_full, b_gate_b, b_norm_g, ln_g, ln_b)
    loss = lax.psum(loss_dev[0, 0], ("x", "y", "c"))

    recv = lambda n: [pipe.recv[l][n] for l in range(depth)]
    big = {"w_in": _adamw(recv("w_in"), w_in, m_w_in, v_w_in, 128, "adamw_w_in"),
           "w_out": _adamw(recv("w_out"), w_out, m_w_out, v_w_out, 64, "adamw_w_out"),
           "w_kv": _adamw(recv("w_kv"), w_mem_kv, m_w_mem_kv, v_w_mem_kv, 128, "adamw_w_kv")}

    shapes = {"rel": a_rel_bias.shape, "gate_w": (depth, GATE_RANK, N_DEV * sh_gw), "gate_b": b_gate_b.shape,
              "norm_g": b_norm_g.shape, "ln_g": ln_g.shape, "ln_b": ln_b.shape}
    part = _pack_small({n: jnp.stack([grads[l][n] for l in range(depth)]) for n, _ in SMALL}, depth)
    (all_parts,) = _exchange([part], False, "gather_small")
    zeros_gw = jnp.zeros(shapes["gate_w"], F32)
    w_s = _pack_small(dict(rel=a_rel_bias, gate_w=zeros_gw, gate_b=b_gate_b, norm_g=b_norm_g, ln_g=ln_g, ln_b=ln_b), depth)
    m_s = _pack_small(dict(rel=m_a_rel_bias, gate_w=zeros_gw, gate_b=m_b_gate_b, norm_g=m_b_norm_g, ln_g=m_ln_g, ln_b=m_ln_b), depth)
    v_s = _pack_small(dict(rel=v_a_rel_bias, gate_w=zeros_gw, gate_b=v_b_gate_b, norm_g=v_b_norm_g, ln_g=v_ln_g, ln_b=v_ln_b), depth)
    small = [_unpack_small(t[0], depth, shapes)
             for t in _adamw([all_parts], w_s[None], m_s[None], v_s[None], all_parts.shape[1], "adamw_small")]
    gw_grad = lax.dynamic_slice_in_dim(small[0]["gate_w"], me * sh_gw, sh_gw, axis=2).reshape(1, depth * GATE_RANK, sh_gw)
    flat = lambda t: t.reshape(1, depth * GATE_RANK, sh_gw)
    gw_res = [t.reshape(depth, GATE_RANK, sh_gw)
              for t in _adamw([gw_grad], flat(b_gate_w), flat(m_b_gate_w), flat(v_b_gate_w), depth * GATE_RANK, "adamw_gate_w")]

    def leaves(t):
        return (big["w_in"][t], small[t]["rel"], gw_res[t], small[t]["gate_b"], small[t]["norm_g"],
                big["w_kv"][t], big["w_out"][t], small[t]["ln_g"], small[t]["ln_b"])

    return (loss, dx[None]) + leaves(0) + leaves(1) + leaves(2) + leaves(3)
```

```python
import functools
import math

import numpy as np
import jax
import jax.numpy as jnp
from jax import lax
from jax.experimental import pallas as pl
from jax.experimental.pallas import tpu as pltpu

F32 = jnp.float32
BF16 = jnp.bfloat16

N_DEV = 8
D = 2048
CH = 64
LEFT = 8
MAX_REL = 128
N_MEM = 256
A_HEADS, A_DH, A_W = 8, 128, 1024
B_HEADS, B_DK, B_DV, B_KW, B_W = 4, 64, 128, 256, 512
GATE_RANK, GATE_TAU = 16, 16.0
M_HEADS, M_DH, M_W = 4, 128, 512
IN_W = 6672
NAT_SPLIT = 5648
H_W = 7168
H_PAD = H_W - IN_W
A_Q, A_K, A_V, A_Z = 0, 1024, 2048, 3072
B_BASE = 4096
M_BASE = 6144
ALPHA = (2.0 * 4) ** 0.25
LN_EPS = 1e-5
RMS_EPS = 1e-6
NEG = -1e30
QB = 256
KB = 3 * QB
ADAM_LR, ADAM_B1, ADAM_B2, ADAM_EPS, ADAM_WD, ADAM_STEP = 0.001, 0.9, 0.999, 1e-08, 0.01, 10
VMEM_MB = 1024 * 1024


def _params(sem, vmem_mb=48):
    return pltpu.CompilerParams(dimension_semantics=sem, vmem_limit_bytes=vmem_mb * VMEM_MB)


def _sigmoid(x):
    return 1.0 / (1.0 + jnp.exp(-x))


def _dot(a, b, ca, cb, precision=None):
    return lax.dot_general(a, b, (((ca,), (cb,)), ((), ())), preferred_element_type=F32, precision=precision)


MESH = pl.DeviceIdType.MESH


class _Comm:
    def __init__(self, xs, scatter):
        self.xs, self.scatter, self.n = list(xs), scatter, len(xs)
        self.out_shape = [jax.ShapeDtypeStruct((N_DEV,) + (x.shape[1:] if scatter else x.shape), x.dtype) for x in xs]
        self.specs = [pl.BlockSpec(memory_space=pltpu.HBM)] * self.n
        self.scratch = [pltpu.SemaphoreType.DMA((self.n, N_DEV - 1)), pltpu.SemaphoreType.DMA((self.n, N_DEV - 1)),
                        pltpu.SemaphoreType.DMA((self.n,))]

    def _copies(self, x_refs, o_refs, sems):
        send_sems, recv_sems, local_sems = sems
        mx, my, mc = lax.axis_index("x"), lax.axis_index("y"), lax.axis_index("c")
        me = 4 * mx + 2 * my + mc
        sends, arrivals = [], []
        for k in range(N_DEV - 1):
            px = 1 - mx if (k + 1) & 4 else mx
            py = 1 - my if (k + 1) & 2 else my
            pc = 1 - mc if (k + 1) & 1 else mc
            idx = 4 * px + 2 * py + pc
            for a in range(self.n):
                src = x_refs[a].at[idx] if self.scatter else x_refs[a]
                for dst, group in ((o_refs[a].at[me], sends), (o_refs[a].at[idx], arrivals)):
                    group.append(pltpu.make_async_remote_copy(
                        src_ref=src, dst_ref=dst, send_sem=send_sems.at[a, k], recv_sem=recv_sems.at[a, k],
                        device_id=(px, py, pc), device_id_type=MESH))
        own = [pltpu.make_async_copy(x_refs[a].at[me] if self.scatter else x_refs[a], o_refs[a].at[me], local_sems.at[a])
               for a in range(self.n)]
        return own, sends, arrivals

    def start(self, x_refs, o_refs, sems):
        own, sends, _ = self._copies(x_refs, o_refs, sems)
        for cp in own + sends:
            cp.start()

    def finish(self, x_refs, o_refs, sems):
        own, sends, arrivals = self._copies(x_refs, o_refs, sems)
        for cp in sends:
            cp.wait_send()
        for cp in arrivals:
            cp.wait_recv()
        for cp in own:
            cp.wait()


def _exchange(xs, scatter, name):
    comm = _Comm(xs, scatter)

    def body(*refs):
        x_refs, o_refs, sems = refs[:comm.n], refs[comm.n:2 * comm.n], refs[2 * comm.n:]
        comm.start(x_refs, o_refs, sems)
        comm.finish(x_refs, o_refs, sems)

    return pl.pallas_call(body, name=name, out_shape=tuple(comm.out_shape), in_specs=comm.specs,
                          out_specs=tuple(comm.specs), scratch_shapes=comm.scratch)(*comm.xs)


def _hosted(body, comm, *, name, grid, in_specs, out_specs, out_shape, scratch_shapes, compiler_params, inputs,
            input_output_aliases=None):
    out_specs, out_shape = tuple(out_specs), tuple(out_shape)
    aliases = input_output_aliases or {}
    if comm is None:
        outs = pl.pallas_call(body, name=name, grid=grid, in_specs=list(in_specs), out_specs=out_specs, out_shape=out_shape,
                              scratch_shapes=list(scratch_shapes), compiler_params=compiler_params,
                              input_output_aliases=aliases)(*inputs)
        return tuple(outs), ()
    ni, no, ns, nc = len(in_specs), len(out_specs), len(scratch_shapes), comm.n

    def wrapped(*refs):
        ins, c_in = refs[:ni], refs[ni:ni + nc]
        outs, c_out = refs[ni + nc:ni + nc + no], refs[ni + nc + no:ni + 2 * nc + no]
        scr, sems = refs[ni + 2 * nc + no:ni + 2 * nc + no + ns], refs[ni + 2 * nc + no + ns:]
        first = functools.reduce(jnp.logical_and, [pl.program_id(d) == 0 for d in range(len(grid))])
        last = functools.reduce(jnp.logical_and, [pl.program_id(d) == grid[d] - 1 for d in range(len(grid))])

        @pl.when(first)
        def _():
            comm.start(c_in, c_out, sems)

        body(*ins, *outs, *scr)

        @pl.when(last)
        def _():
            comm.finish(c_in, c_out, sems)

    params = pltpu.CompilerParams(dimension_semantics=("arbitrary",) * len(grid),
                                  vmem_limit_bytes=compiler_params.vmem_limit_bytes)
    outs = pl.pallas_call(wrapped, name=name, grid=grid, in_specs=list(in_specs) + comm.specs,
                          out_specs=out_specs + tuple(comm.specs), out_shape=out_shape + tuple(comm.out_shape),
                          scratch_shapes=list(scratch_shapes) + comm.scratch, compiler_params=params,
                          input_output_aliases=aliases)(*inputs, *comm.xs)
    return tuple(outs[:no]), tuple(outs[no:])


def _mm(a, b, *, ta=False, tb=False, out_dtype, tm, tn, tk, name, adds=(), vmem_mb=48, comm=None):
    m = a.shape[1] if ta else a.shape[0]
    k = a.shape[0] if ta else a.shape[1]
    n = b.shape[0] if tb else b.shape[1]
    assert k == (b.shape[1] if tb else b.shape[0])
    tm, tn, tk = min(tm, m), min(tn, n), min(tk, k)
    assert m % tm == 0 and n % tn == 0 and k % tk == 0, (name, m, n, k)
    nk = k // tk
    n_add = len(adds)
    scales = [s for _, s in adds]

    def body(a_ref, b_ref, *rest):
        add_refs, o_ref, acc_ref = rest[:n_add], rest[n_add], rest[n_add + 1]
        p = _dot(a_ref[...].astype(BF16), b_ref[...].astype(BF16), 0 if ta else 1, 1 if tb else 0)
        kk = pl.program_id(2)

        @pl.when(kk == 0)
        def _():
            acc_ref[...] = p

        @pl.when(kk > 0)
        def _():
            acc_ref[...] += p

        @pl.when(kk == nk - 1)
        def _():
            r = acc_ref[...]
            for ref, s in zip(add_refs, scales):
                r = r + s * ref[...].astype(F32)
            o_ref[...] = r.astype(out_dtype)

    a_spec = pl.BlockSpec((tk, tm), lambda i, j, kk: (kk, i)) if ta else pl.BlockSpec((tm, tk), lambda i, j, kk: (i, kk))
    b_spec = pl.BlockSpec((tn, tk), lambda i, j, kk: (j, kk)) if tb else pl.BlockSpec((tk, tn), lambda i, j, kk: (kk, j))
    add_specs = [pl.BlockSpec((tm, tn), lambda i, j, kk: (i, j)) for _ in adds]
    (out,), c_out = _hosted(
        body, comm, name=name,
        out_shape=[jax.ShapeDtypeStruct((m, n), out_dtype)],
        grid=(m // tm, n // tn, nk),
        in_specs=[a_spec, b_spec] + add_specs,
        out_specs=[pl.BlockSpec((tm, tn), lambda i, j, kk: (i, j))],
        scratch_shapes=[pltpu.VMEM((tm, tn), F32)],
        compiler_params=_params(("parallel", "parallel", "arbitrary"), vmem_mb),
        inputs=(a, b, *[x for x, _ in adds]))
    return out, c_out


NKB = KB // QB
LEAD = NKB - 1


def _band_bias(table):
    i = np.arange(QB)[:, None]
    j = np.arange(KB)[None, :]
    qc = i // CH + LEAD * QB // CH
    kc = j // CH
    valid = (kc <= qc) & (kc >= qc - LEFT)
    n = QB + KB
    c = np.arange(n)
    onehot = np.zeros((2 * MAX_REL + 1, n), np.float32)
    onehot[np.clip(LEAD * QB - (c - (QB - 1)), -MAX_REL, MAX_REL) + MAX_REL, c] = 1.0
    row = jnp.dot(table.astype(F32), jnp.asarray(onehot), precision=lax.Precision.HIGHEST)
    flow = jnp.tile(row, (1, QB))[:, :QB * (n - 1)].reshape(table.shape[0], QB, n - 1)
    return jnp.where(valid[None], flow[:, :, QB - 1:], NEG)


def _bias_grad(dbias):
    h = dbias.shape[0]
    flipped = dbias[:, ::-1, :]
    padded = jnp.pad(flipped, ((0, 0), (0, 0), (0, 1))).reshape(h, QB * (KB + 1))
    padded = jnp.pad(padded, ((0, 0), (0, (QB + 1) * KB - QB * (KB + 1))))
    diag = padded.reshape(h, QB + 1, KB).sum(axis=1)
    jm = np.arange(KB) - (QB - 1)
    jm = np.where(jm <= -CH, jm + KB, jm)
    didx = np.clip(LEAD * QB - jm, -MAX_REL, MAX_REL) + MAX_REL
    onehot = np.zeros((KB, 2 * MAX_REL + 1), np.float32)
    onehot[np.arange(KB), didx] = 1.0
    return jnp.dot(diag, jnp.asarray(onehot), precision=lax.Precision.HIGHEST)


def _attn_scores(q, k_refs, cs, bias_h, m, masked, scale):
    parts = []
    for t in range(NKB):
        sc = _dot(q, k_refs[t][:, cs], 1, 1) * scale
        parts.append(jnp.where(m + t - LEAD >= 0, sc, NEG) if masked else sc)
    return jnp.concatenate(parts, axis=1) + bias_h


def _attn_fwd(h, r_in, bias, comm=None):
    s = h.shape[0]
    nq = s // QB
    scale = A_DH ** -0.5

    def body(q_ref, *rest):
        k_refs, v_refs = rest[:NKB], rest[NKB:2 * NKB]
        bias_hbm, r_any, o_ref, lse_ref, bias_ref = rest[2 * NKB:]
        del r_any
        m = pl.program_id(0)

        @pl.when(m == 0)
        def _():
            pltpu.sync_copy(bias_hbm, bias_ref)

        def scores(hd, masked):
            cs = slice(hd * A_DH, (hd + 1) * A_DH)
            return _attn_scores(q_ref[:, cs], k_refs, cs, bias_ref[hd], m, masked, scale)

        def step(masked):
            sc_next = scores(0, masked)
            for hd in range(A_HEADS):
                cs = slice(hd * A_DH, (hd + 1) * A_DH)
                sc = sc_next
                if hd + 1 < A_HEADS:
                    sc_next = scores(hd + 1, masked)
                mx = jnp.max(sc, axis=1, keepdims=True)
                p = jnp.exp(sc - mx)
                l = jnp.sum(p, axis=1, keepdims=True)
                pb = p.astype(BF16)
                o = _dot(pb[:, :QB], v_refs[0][:, cs], 1, 0)
                for t in range(1, NKB):
                    o += _dot(pb[:, t * QB:(t + 1) * QB], v_refs[t][:, cs], 1, 0)
                o_ref[:, cs] = (o / l).astype(BF16)
                lse_ref[:, hd:hd + 1] = mx + jnp.log(l)

        pl.when(m < LEAD)(functools.partial(step, True))
        pl.when(m >= LEAD)(functools.partial(step, False))

    def kv_spec(col, t):
        return pl.BlockSpec((QB, A_W), lambda m: (jnp.maximum(m + t - LEAD, 0), col))

    (r, lse), c_out = _hosted(
        body, comm, name="attn_fwd",
        out_shape=(jax.ShapeDtypeStruct(r_in.shape, BF16), jax.ShapeDtypeStruct((s, A_HEADS), F32)),
        grid=(nq,),
        in_specs=[pl.BlockSpec((QB, A_W), lambda m: (m, 0))]
        + [kv_spec(1, t) for t in range(NKB)] + [kv_spec(2, t) for t in range(NKB)]
        + [pl.BlockSpec(memory_space=pl.ANY), pl.BlockSpec(memory_space=pl.ANY)],
        out_specs=(pl.BlockSpec((QB, A_W), lambda m: (m, 0)), pl.BlockSpec((QB, A_HEADS), lambda m: (m, 0))),
        scratch_shapes=[pltpu.VMEM((A_HEADS, QB, KB), F32)],
        input_output_aliases={2 * NKB + 2: 0},
        compiler_params=_params(("arbitrary",)),
        inputs=(h,) * (2 * NKB + 1) + (bias, r_in))
    return r, lse, c_out


def _attn_bwd(h, r, dy, lse, bias, dh_in, comm=None):
    s = h.shape[0]
    nq = s // QB
    scale = A_DH ** -0.5

    def body(q_ref, *rest):
        k_refs, v_refs = rest[:NKB], rest[NKB:2 * NKB]
        (z_ref, r_ref, dy_ref, lse_ref, zl_ref, rl_ref, dyl_ref, bias_hbm, dh_any,
         dh_ref, dbias_hbm, dk_acc, dv_acc, dq_ring, bias_ref, dbias_ref) = rest[2 * NKB:]
        del dh_any
        m = pl.program_id(0)

        def slot(b):
            return pl.multiple_of(lax.rem(b + NKB, NKB) * QB, QB)

        @pl.when(m == 0)
        def _():
            dk_acc[...] = jnp.zeros_like(dk_acc)
            dv_acc[...] = jnp.zeros_like(dv_acc)
            dq_ring[...] = jnp.zeros_like(dq_ring)
            dbias_ref[...] = jnp.zeros_like(dbias_ref)
            pltpu.sync_copy(bias_hbm, bias_ref)

        def step(masked):
            z = z_ref[...].astype(F32)
            do_all = dy_ref[...].astype(F32) * (z * _sigmoid(z))
            o_all = r_ref[...].astype(F32)

            def products(hd):
                cs = slice(hd * A_DH, (hd + 1) * A_DH)
                dob = do_all[:, cs].astype(BF16)
                sc = _attn_scores(q_ref[:, cs], k_refs, cs, bias_ref[hd], m, masked, scale)
                return sc, jnp.concatenate([_dot(dob, v_refs[t][:, cs], 1, 1) for t in range(NKB)], axis=1)

            ahead = products(0)
            for hd in range(A_HEADS):
                cs = slice(hd * A_DH, (hd + 1) * A_DH)
                q = q_ref[:, cs]
                do = do_all[:, cs]
                dob = do.astype(BF16)
                delta = jnp.sum(do * o_all[:, cs], axis=1, keepdims=True)
                sc, dp = ahead
                if hd + 1 < A_HEADS:
                    ahead = products(hd + 1)
                p = jnp.exp(sc - lse_ref[:, hd:hd + 1])
                ds = p * (dp - delta)
                dbias_ref[hd] += ds
                pb, dsb = p.astype(BF16), ds.astype(BF16)
                dq = jnp.zeros((QB, A_DH), F32)
                for t in range(NKB):
                    ts = slice(t * QB, (t + 1) * QB)
                    rows = pl.ds(slot(m - LEAD + t), QB)
                    dq += _dot(dsb[:, ts], k_refs[t][:, cs], 1, 0)
                    dk_acc[rows, cs] += _dot(dsb[:, ts], q, 0, 0) * scale
                    dv_acc[rows, cs] += _dot(pb[:, ts], dob, 0, 0)
                dq_ring[pl.ds(slot(m), QB), cs] = dq * scale

        pl.when(m < LEAD)(functools.partial(step, True))
        pl.when((m >= LEAD) & (m < nq))(functools.partial(step, False))

        done = pl.ds(slot(m - LEAD), QB)
        zl = zl_ref[...].astype(F32)
        sg = _sigmoid(zl)
        dz = dyl_ref[...].astype(F32) * rl_ref[...].astype(F32) * (sg * (1.0 + zl * (1.0 - sg)))
        dh_ref[:, A_Q:A_Q + A_W] = dq_ring[done, :].astype(BF16)
        dh_ref[:, A_K:A_K + A_W] = dk_acc[done, :].astype(BF16)
        dh_ref[:, A_V:A_V + A_W] = dv_acc[done, :].astype(BF16)
        dh_ref[:, A_Z:A_Z + A_W] = dz.astype(BF16)
        dk_acc[done, :] = jnp.zeros((QB, A_W), F32)
        dv_acc[done, :] = jnp.zeros((QB, A_W), F32)

        @pl.when(m == nq + LEAD - 1)
        def _():
            pltpu.sync_copy(dbias_ref, dbias_hbm)

    last = nq - 1

    def cur(col):
        return pl.BlockSpec((QB, A_W), lambda m: (jnp.minimum(m, last), col))

    def kv_spec(col, t):
        return pl.BlockSpec((QB, A_W), lambda m: (jnp.clip(m + t - LEAD, 0, last), col))

    def lag(col):
        return pl.BlockSpec((QB, A_W), lambda m: (jnp.clip(m - LEAD, 0, last), col))

    (dh, dbias), c_out = _hosted(
        body, comm, name="attn_bwd",
        out_shape=(jax.ShapeDtypeStruct(dh_in.shape, BF16), jax.ShapeDtypeStruct((A_HEADS, QB, KB), F32)),
        grid=(nq + LEAD,),
        in_specs=[cur(0)] + [kv_spec(1, t) for t in range(NKB)] + [kv_spec(2, t) for t in range(NKB)]
        + [cur(3), cur(0), cur(0), pl.BlockSpec((QB, A_HEADS), lambda m: (jnp.minimum(m, last), 0)),
           lag(3), lag(0), lag(0),
           pl.BlockSpec(memory_space=pl.ANY), pl.BlockSpec(memory_space=pl.ANY)],
        out_specs=(pl.BlockSpec((QB, 4 * A_W), lambda m: (jnp.clip(m - LEAD, 0, last), 0)),
                   pl.BlockSpec(memory_space=pl.ANY)),
        scratch_shapes=[pltpu.VMEM((KB, A_W), F32), pltpu.VMEM((KB, A_W), F32), pltpu.VMEM((KB, A_W), F32),
                        pltpu.VMEM((A_HEADS, QB, KB), F32), pltpu.VMEM((A_HEADS, QB, KB), F32)],
        input_output_aliases={2 * NKB + 9: 0},
        compiler_params=_params(("arbitrary",), 56),
        inputs=(h,) * (2 * NKB + 2) + (r, dy, lse, h, r, dy, bias, dh_in))
    return dh, dbias, c_out


GB = 256
N_PAIR = B_HEADS // 2


def _gla_gates(lr, gw_ref, gb_ref):
    logit = _dot(lr, gw_ref[...], 1, 0) + gb_ref[...]
    lg = (jnp.minimum(logit, 0.0) - jnp.log(1.0 + jnp.exp(-jnp.abs(logit)))) / GATE_TAU
    row = lax.broadcasted_iota(jnp.int32, (GB, GB), 0)
    col = lax.broadcasted_iota(jnp.int32, (GB, GB), 1)
    tri = jnp.where((row // CH == col // CH) & (col <= row), 1.0, 0.0).astype(F32)
    return logit, _dot(tri, lg, 1, 0, precision=lax.Precision.HIGHEST)


def _gla_factors(hb_ref, b_all, c):
    rs = slice(c * CH, (c + 1) * CH)
    q = hb_ref[rs, 0:B_KW].astype(F32) * (B_DK ** -0.5)
    k = hb_ref[rs, B_KW:2 * B_KW].astype(F32)
    b = b_all[rs]
    bm, bl = b[CH // 2:CH // 2 + 1, :], b[CH - 1:CH, :]
    e1, e2, eb, ek = jnp.exp(b - bm), jnp.exp(bm - b), jnp.exp(b), jnp.exp(bl - b)
    el = jnp.exp(bl)
    return dict(ql=q * e1, kl=k * e2, qu=q * e2, ku=k * e1, qt=q * eb, kh=k * ek, e1=e1, e2=e2, eb=eb, ek=ek, el=el)


def _gla_fwd(h, r_in, gw, gb, ng):
    s = h.shape[0]
    nb = s // GB
    cpb = GB // CH

    def body(hb_ref, gw_ref, gb_ref, ng_ref, r_any, o_ref, opre_ref, st_ref, state):
        del r_any

        @pl.when(pl.program_id(0) == 0)
        def _():
            state[...] = jnp.zeros_like(state)

        _, b_all = _gla_gates(hb_ref[:, 1536:1664], gw_ref, gb_ref)
        lane = lax.broadcasted_iota(jnp.int32, (CH, 128), 1)
        ri = lax.broadcasted_iota(jnp.int32, (CH, CH), 0)
        ci = lax.broadcasted_iota(jnp.int32, (CH, CH), 1)
        fs = [_gla_factors(hb_ref, b_all, c) for c in range(cpb)]
        heads = [(c, p, sh) for c in range(cpb) for p in range(N_PAIR) for sh in range(2)]

        def pick(c, p, sh, name):
            return jnp.where((lane // B_DK) == sh, fs[c][name][:, p * 128:(p + 1) * 128], 0.0).astype(BF16)

        def pair(c, p, name):
            return fs[c][name][:, p * 128:(p + 1) * 128].astype(BF16)

        def vals(c, p, sh):
            hd = 2 * p + sh
            return hb_ref[c * CH:(c + 1) * CH, 512 + hd * B_DV:512 + (hd + 1) * B_DV]

        a = {k: jnp.where(ri >= ci, _dot(pick(*k, "ql"), pair(k[0], k[1], "kl"), 1, 1),
                          _dot(pick(*k, "qu"), pair(k[0], k[1], "ku"), 1, 1)).astype(BF16) for k in heads}
        o_intra = {k: _dot(a[k], vals(*k), 1, 0) for k in heads}
        upd = {k: _dot(vals(*k), pick(*k, "kh"), 0, 0) for k in heads}
        st = [state[p] for p in range(N_PAIR)]
        entering = {}
        for c in range(cpb):
            for p in range(N_PAIR):
                entering[c, p] = st[p]
                st_ref[c, p] = st[p]
                st[p] = st[p] * fs[c]["el"][:, p * 128:(p + 1) * 128] + upd[c, p, 0] + upd[c, p, 1]
        for p in range(N_PAIR):
            state[p] = st[p]
        for k in heads:
            c, p, sh = k
            o = o_intra[k] + _dot(pick(*k, "qt"), entering[c, p].astype(BF16), 1, 1)
            rs, hs = slice(c * CH, (c + 1) * CH), slice((2 * p + sh) * B_DV, (2 * p + sh + 1) * B_DV)
            opre_ref[rs, hs] = o
            rinv = lax.rsqrt(jnp.mean(o * o, axis=1, keepdims=True) + RMS_EPS)
            o_ref[rs, hs] = (o * rinv * ng_ref[...]).astype(BF16)

    return pl.pallas_call(
        body, name="gla_fwd",
        out_shape=(jax.ShapeDtypeStruct(r_in.shape, BF16), jax.ShapeDtypeStruct((s, B_W), F32),
                   jax.ShapeDtypeStruct((s // CH, N_PAIR, 128, 128), F32)),
        grid=(nb,),
        in_specs=[pl.BlockSpec((GB, 2048), lambda i: (i, B_BASE // 2048)),
                  pl.BlockSpec((128, B_KW), lambda i: (0, 0)), pl.BlockSpec((1, B_KW), lambda i: (0, 0)),
                  pl.BlockSpec((1, B_DV), lambda i: (0, 0)), pl.BlockSpec(memory_space=pl.ANY)],
        out_specs=(pl.BlockSpec((GB, B_W), lambda i: (i, 1024 // B_W)), pl.BlockSpec((GB, B_W), lambda i: (i, 0)),
                   pl.BlockSpec((cpb, N_PAIR, 128, 128), lambda i: (i, 0, 0, 0))),
        scratch_shapes=[pltpu.VMEM((N_PAIR, 128, 128), F32)],
        input_output_aliases={4: 0},
        compiler_params=_params(("arbitrary",)),
    )(h, gw, gb, ng, r_in)


def _gla_bwd(h, dy, opre, states, gw, gb, ng, dh_in):
    s = h.shape[0]
    nb = s // GB
    cpb = GB // CH

    def body(hb_ref, dy_ref, opre_ref, st_ref, gw_ref, gb_ref, ng_ref, dh_any,
             dh_ref, dgw_ref, dgb_ref, dng_ref, dstate, db_scr, do_scr):
        del dh_any

        @pl.when(pl.program_id(0) == 0)
        def _():
            dstate[...] = jnp.zeros_like(dstate)
            dgw_ref[...] = jnp.zeros_like(dgw_ref)
            dgb_ref[...] = jnp.zeros_like(dgb_ref)
            dng_ref[...] = jnp.zeros_like(dng_ref)

        lr = hb_ref[:, 1536:1664]
        logit, b_all = _gla_gates(lr, gw_ref, gb_ref)
        z = hb_ref[:, 1024:1536].astype(F32)
        sg = _sigmoid(z)
        dyb = dy_ref[...].astype(F32)
        dng = jnp.zeros((1, B_DV), F32)
        for hd in range(B_HEADS):
            hs = slice(hd * B_DV, (hd + 1) * B_DV)
            o = opre_ref[:, hs]
            rinv = lax.rsqrt(jnp.mean(o * o, axis=1, keepdims=True) + RMS_EPS)
            on = o * rinv
            dr = dyb[:, hs] * (z[:, hs] * sg[:, hs])
            dh_ref[:, 1024 + hd * B_DV:1024 + (hd + 1) * B_DV] = (
                dyb[:, hs] * (on * ng_ref[...]) * (sg[:, hs] * (1.0 + z[:, hs] * (1.0 - sg[:, hs])))).astype(BF16)
            dng += jnp.sum(dr * on, axis=0, keepdims=True)
            dn = dr * ng_ref[...]
            do_scr[:, hs] = rinv * (dn - on * jnp.mean(dn * on, axis=1, keepdims=True))
        dng_ref[...] += dng

        lane = lax.broadcasted_iota(jnp.int32, (CH, 128), 1)
        ri = lax.broadcasted_iota(jnp.int32, (CH, CH), 0)
        ci = lax.broadcasted_iota(jnp.int32, (CH, CH), 1)
        rowi = lax.broadcasted_iota(jnp.int32, (CH, 128), 0)
        fs = [_gla_factors(hb_ref, b_all, c) for c in range(cpb)]
        heads = [(c, p, sh) for c in reversed(range(cpb)) for p in range(N_PAIR) for sh in range(2)]

        def mask(sh, x):
            return jnp.where((lane // B_DK) == sh, x, 0.0)

        def pick(c, p, sh, name):
            return mask(sh, fs[c][name][:, p * 128:(p + 1) * 128]).astype(BF16)

        def pair(c, p, name):
            return fs[c][name][:, p * 128:(p + 1) * 128]

        def vals(c, p, sh):
            hd = 2 * p + sh
            return hb_ref[c * CH:(c + 1) * CH, 512 + hd * B_DV:512 + (hd + 1) * B_DV]

        def douts(c, p, sh):
            hd = 2 * p + sh
            return do_scr[c * CH:(c + 1) * CH, hd * B_DV:(hd + 1) * B_DV].astype(BF16)

        at, da, dat, dst_own, d_qt = {}, {}, {}, {}, {}
        for k in heads:
            c, p, sh = k
            klp, kup = pair(c, p, "kl").astype(BF16), pair(c, p, "ku").astype(BF16)
            at[k] = jnp.where(ci >= ri, _dot(klp, pick(*k, "ql"), 1, 1), _dot(kup, pick(*k, "qu"), 1, 1)).astype(BF16)
            da[k] = _dot(douts(*k), vals(*k), 1, 1)
            dat[k] = _dot(vals(*k), douts(*k), 1, 1)
            dst_own[k] = _dot(douts(*k), pick(*k, "qt"), 0, 0)
            d_qt[k] = mask(sh, _dot(douts(*k), st_ref[c, p].astype(BF16), 1, 0))
        dv, d_ql, d_qu, d_kl, d_ku = {}, {}, {}, {}, {}
        for k in heads:
            c, p, sh = k
            klp, kup = pair(c, p, "kl").astype(BF16), pair(c, p, "ku").astype(BF16)
            dv[k] = _dot(at[k], douts(*k), 1, 0)
            d_ql[k] = mask(sh, _dot(jnp.where(ri >= ci, da[k], 0.0).astype(BF16), klp, 1, 0))
            d_qu[k] = mask(sh, _dot(jnp.where(ri < ci, da[k], 0.0).astype(BF16), kup, 1, 0))
            d_kl[k] = _dot(jnp.where(ci >= ri, dat[k], 0.0).astype(BF16), pick(*k, "ql"), 1, 0)
            d_ku[k] = _dot(jnp.where(ci < ri, dat[k], 0.0).astype(BF16), pick(*k, "qu"), 1, 0)
        dst = [dstate[p] for p in range(N_PAIR)]
        leaving = {}
        for c in reversed(range(cpb)):
            for p in range(N_PAIR):
                leaving[c, p] = dst[p]
                dst[p] = dst[p] * pair(c, p, "el") + dst_own[c, p, 0] + dst_own[c, p, 1]
        for p in range(N_PAIR):
            dstate[p] = dst[p]
        d_kh = {}
        for k in heads:
            c, p, sh = k
            dstb = leaving[c, p].astype(BF16)
            hd = 2 * p + sh
            dh_ref[c * CH:(c + 1) * CH, 512 + hd * B_DV:512 + (hd + 1) * B_DV] = (
                dv[k] + _dot(pick(*k, "kh"), dstb, 1, 1)).astype(BF16)
            d_kh[k] = mask(sh, _dot(vals(*k), dstb, 1, 0))
        for c in reversed(range(cpb)):
            rs = slice(c * CH, (c + 1) * CH)
            for p in range(N_PAIR):
                both = lambda d: d[c, p, 0] + d[c, p, 1]
                g_ql, g_qu, g_qt, g_kl, g_ku, g_kh = both(d_ql), both(d_qu), both(d_qt), both(d_kl), both(d_ku), both(d_kh)
                dq = (g_ql * pair(c, p, "e1") + g_qu * pair(c, p, "e2") + g_qt * pair(c, p, "eb")) * (B_DK ** -0.5)
                dk = g_kl * pair(c, p, "e2") + g_ku * pair(c, p, "e1") + g_kh * pair(c, p, "ek")
                dkh_kh = g_kh * pair(c, p, "kh")
                db = (g_ql * pair(c, p, "ql") - g_qu * pair(c, p, "qu") + g_qt * pair(c, p, "qt")
                      - g_kl * pair(c, p, "kl") + g_ku * pair(c, p, "ku") - dkh_kh)
                db_last = (pair(c, p, "el") * jnp.sum(leaving[c, p] * st_ref[c, p], axis=0, keepdims=True)
                           + jnp.sum(dkh_kh, axis=0, keepdims=True))
                db = jnp.where(rowi == CH - 1, db + db_last, db)
                dh_ref[rs, p * 128:(p + 1) * 128] = dq.astype(BF16)
                dh_ref[rs, B_KW + p * 128:B_KW + (p + 1) * 128] = dk.astype(BF16)
                db_scr[rs, p * 128:(p + 1) * 128] = db

        row = lax.broadcasted_iota(jnp.int32, (GB, GB), 0)
        col = lax.broadcasted_iota(jnp.int32, (GB, GB), 1)
        trit = jnp.where((row // CH == col // CH) & (col >= row), 1.0, 0.0).astype(F32)
        dlg = _dot(trit, db_scr[...], 1, 0, precision=lax.Precision.HIGHEST)
        dlogit = dlg * (_sigmoid(-logit) / GATE_TAU)
        dlb = dlogit.astype(BF16)
        dgw_ref[...] += _dot(lr, dlb, 0, 0)
        dgb_ref[...] += jnp.sum(dlogit, axis=0, keepdims=True)
        dh_ref[:, 1536:1664] = _dot(dlb, gw_ref[...], 1, 1).astype(BF16)
        dh_ref[:, 1664:2048] = jnp.zeros((GB, 384), BF16)

    rev = lambda i: nb - 1 - i
    return pl.pallas_call(
        body, name="gla_bwd",
        out_shape=(jax.ShapeDtypeStruct(dh_in.shape, BF16), jax.ShapeDtypeStruct((128, B_KW), F32),
                   jax.ShapeDtypeStruct((1, B_KW), F32), jax.ShapeDtypeStruct((1, B_DV), F32)),
        grid=(nb,),
        in_specs=[pl.BlockSpec((GB, 2048), lambda i: (rev(i), B_BASE // 2048)),
                  pl.BlockSpec((GB, B_W), lambda i: (rev(i), 1024 // B_W)),
                  pl.BlockSpec((GB, B_W), lambda i: (rev(i), 0)),
                  pl.BlockSpec((cpb, N_PAIR, 128, 128), lambda i: (rev(i), 0, 0, 0)),
                  pl.BlockSpec((128, B_KW), lambda i: (0, 0)), pl.BlockSpec((1, B_KW), lambda i: (0, 0)),
                  pl.BlockSpec((1, B_DV), lambda i: (0, 0)), pl.BlockSpec(memory_space=pl.ANY)],
        out_specs=(pl.BlockSpec((GB, 2048), lambda i: (rev(i), B_BASE // 2048)),
                   pl.BlockSpec((128, B_KW), lambda i: (0, 0)), pl.BlockSpec((1, B_KW), lambda i: (0, 0)),
                   pl.BlockSpec((1, B_DV), lambda i: (0, 0))),
        scratch_shapes=[pltpu.VMEM((N_PAIR, 128, 128), F32), pltpu.VMEM((GB, B_KW), F32), pltpu.VMEM((GB, B_W), F32)],
        input_output_aliases={7: 0},
        compiler_params=_params(("arbitrary",)),
    )(h, dy, opre, states, gw, gb, ng, dh_in)


MB = 512


def _mem_probs(q, mk, scale):
    sc = _dot(q, mk, 1, 1) * scale
    p = jnp.exp(sc - jnp.max(sc, axis=1, keepdims=True))
    return p / jnp.sum(p, axis=1, keepdims=True)


def _mem_fwd(h, r_in, mkv):
    s = h.shape[0]
    scale = M_DH ** -0.5

    def body(q_ref, mkv_ref, r_any, o_ref):
        del r_any
        for hd in range(M_HEADS):
            cs = slice(hd * M_DH, (hd + 1) * M_DH)
            p = _mem_probs(q_ref[:, cs], mkv_ref[:, cs], scale)
            o_ref[:, cs] = _dot(p.astype(BF16), mkv_ref[:, M_W + hd * M_DH:M_W + (hd + 1) * M_DH], 1, 0).astype(BF16)

    return pl.pallas_call(
        body, name="mem_fwd",
        out_shape=jax.ShapeDtypeStruct(r_in.shape, BF16),
        grid=(s // MB,),
        in_specs=[pl.BlockSpec((MB, M_W), lambda i: (i, M_BASE // M_W)),
                  pl.BlockSpec((N_MEM, 2 * M_W), lambda i: (0, 0)), pl.BlockSpec(memory_space=pl.ANY)],
        out_specs=pl.BlockSpec((MB, M_W), lambda i: (i, 1536 // M_W)),
        input_output_aliases={2: 0},
        compiler_params=_params(("arbitrary",)),
    )(h, mkv, r_in)


def _mem_bwd(h, r, dy, mkv, dh_in):
    s = h.shape[0]
    scale = M_DH ** -0.5

    def body(q_ref, z_ref, r_ref, dy_ref, mkv_ref, dh_any, dh_ref, dmkv_ref):
        del dh_any

        @pl.when(pl.program_id(0) == 0)
        def _():
            dmkv_ref[...] = jnp.zeros_like(dmkv_ref)

        z = z_ref[...].astype(F32)
        sg = _sigmoid(z)
        dyv = dy_ref[...].astype(F32)
        do_all = dyv * (z * sg)
        dh_ref[:, M_W:2 * M_W] = (dyv * r_ref[...].astype(F32) * (sg * (1.0 + z * (1.0 - sg)))).astype(BF16)
        for hd in range(M_HEADS):
            cs = slice(hd * M_DH, (hd + 1) * M_DH)
            vs = slice(M_W + hd * M_DH, M_W + (hd + 1) * M_DH)
            q = q_ref[:, cs]
            p = _mem_probs(q, mkv_ref[:, cs], scale)
            dob = do_all[:, cs].astype(BF16)
            dp = _dot(dob, mkv_ref[:, vs], 1, 1)
            ds = p * (dp - jnp.sum(p * dp, axis=1, keepdims=True))
            dsb = ds.astype(BF16)
            dh_ref[:, cs] = (_dot(dsb, mkv_ref[:, cs], 1, 0) * scale).astype(BF16)
            dmkv_ref[:, cs] += _dot(dsb, q, 0, 0) * scale
            dmkv_ref[:, vs] += _dot(p.astype(BF16), dob, 0, 0)

    return pl.pallas_call(
        body, name="mem_bwd",
        out_shape=(jax.ShapeDtypeStruct(dh_in.shape, BF16), jax.ShapeDtypeStruct((N_MEM, 2 * M_W), F32)),
        grid=(s // MB,),
        in_specs=[pl.BlockSpec((MB, M_W), lambda i: (i, M_BASE // M_W)),
                  pl.BlockSpec((MB, M_W), lambda i: (i, M_BASE // M_W + 1)),
                  pl.BlockSpec((MB, M_W), lambda i: (i, 1536 // M_W)),
                  pl.BlockSpec((MB, M_W), lambda i: (i, 1536 // M_W)),
                  pl.BlockSpec((N_MEM, 2 * M_W), lambda i: (0, 0)), pl.BlockSpec(memory_space=pl.ANY)],
        out_specs=(pl.BlockSpec((MB, 2 * M_W), lambda i: (i, M_BASE // (2 * M_W))),
                   pl.BlockSpec((N_MEM, 2 * M_W), lambda i: (0, 0))),
        input_output_aliases={5: 0},
        compiler_params=_params(("arbitrary",)),
    )(h, h, r, dy, mkv, dh_in)


OB = 256


def _outproj_ln(h, r, w_out, x, ln_g, ln_b):
    s = h.shape[0]

    def body(za_ref, zb_ref, zm_ref, r_ref, w_ref, x_ref, g_ref, b_ref, xn_ref, xh_ref, rstd_ref, y_ref):
        z = jnp.concatenate([za_ref[...], zb_ref[...], zm_ref[...]], axis=1).astype(F32)
        y = (r_ref[...].astype(F32) * (z * _sigmoid(z))).astype(BF16)
        y_ref[...] = y
        u = ALPHA * x_ref[...] + _dot(y, w_ref[...], 1, 0)
        mu = jnp.mean(u, axis=1, keepdims=True)
        uc = u - mu
        rstd = lax.rsqrt(jnp.mean(uc * uc, axis=1, keepdims=True) + LN_EPS)
        xh = uc * rstd
        xh_ref[...] = xh
        rstd_ref[...] = rstd
        xn_ref[...] = xh * g_ref[...] + b_ref[...]

    row = lambda w, c: pl.BlockSpec((OB, w), lambda i: (i, c))
    vec = pl.BlockSpec((1, D), lambda i: (0, 0))
    return pl.pallas_call(
        body, name="outproj_ln",
        out_shape=(jax.ShapeDtypeStruct((s, D), F32), jax.ShapeDtypeStruct((s, D), F32),
                   jax.ShapeDtypeStruct((s, 1), F32), jax.ShapeDtypeStruct((s, D), BF16)),
        grid=(s // OB,),
        in_specs=[row(A_W, A_Z // A_W), row(B_W, (B_BASE + 1024) // B_W), row(M_W, (M_BASE + M_W) // M_W), row(D, 0),
                  pl.BlockSpec((D, D), lambda i: (0, 0)), row(D, 0), vec, vec],
        out_specs=(row(D, 0), row(D, 0), pl.BlockSpec((OB, 1), lambda i: (i, 0)), row(D, 0)),
        compiler_params=_params(("arbitrary",), 56),
    )(h, h, h, r, w_out, x, ln_g, ln_b)


def _ln_bwd(g, xh, rstd, ln_g):
    s = g.shape[0]

    def body(g_ref, xh_ref, rstd_ref, lg_ref, du_ref, dg_ref, db_ref):
        @pl.when(pl.program_id(0) == 0)
        def _():
            dg_ref[...] = jnp.zeros_like(dg_ref)
            db_ref[...] = jnp.zeros_like(db_ref)

        gv, xh = g_ref[...], xh_ref[...]
        dg_ref[...] += jnp.sum(gv * xh, axis=0, keepdims=True)
        db_ref[...] += jnp.sum(gv, axis=0, keepdims=True)
        dxh = gv * lg_ref[...]
        du_ref[...] = rstd_ref[...] * (dxh - jnp.mean(dxh, axis=1, keepdims=True)
                                       - xh * jnp.mean(dxh * xh, axis=1, keepdims=True))

    row = pl.BlockSpec((OB, D), lambda i: (i, 0))
    vec = pl.BlockSpec((1, D), lambda i: (0, 0))
    return pl.pallas_call(
        body, name="ln_bwd",
        out_shape=(jax.ShapeDtypeStruct((s, D), F32), jax.ShapeDtypeStruct((1, D), F32), jax.ShapeDtypeStruct((1, D), F32)),
        grid=(s // OB,),
        in_specs=[row, row, pl.BlockSpec((OB, 1), lambda i: (i, 0)), vec],
        out_specs=(row, vec, vec),
        compiler_params=_params(("arbitrary",)),
    )(g, xh, rstd, ln_g)


def _loss_grad(y, target):
    s = y.shape[0]

    def body(y_ref, t_ref, l_ref, dy_ref):
        @pl.when(pl.program_id(0) == 0)
        def _():
            l_ref[...] = jnp.zeros_like(l_ref)

        e = y_ref[...] - t_ref[...]
        dy_ref[...] = e / D
        l_ref[...] += 0.5 * jnp.sum(jnp.mean(e * e, axis=1, keepdims=True))

    row = pl.BlockSpec((OB, D), lambda i: (i, 0))
    return pl.pallas_call(
        body, name="loss_grad",
        out_shape=(jax.ShapeDtypeStruct((1, 128), F32), jax.ShapeDtypeStruct((s, D), F32)),
        grid=(s // OB,),
        in_specs=[row, row],
        out_specs=(pl.BlockSpec((1, 128), lambda i: (0, 0)), row),
        compiler_params=_params(("arbitrary",)),
    )(y, target)


class _LocalWeights:
    def __init__(self, w_in_p, w_out_f, w_kv_f):
        self.w = list(zip(w_in_p, w_out_f, w_kv_f))
        self.depth = len(self.w)
        self.grads = [dict() for _ in self.w]

    def weights(self, l):
        return self.w[l]

    def host(self, where, l, payload=None):
        if payload is not None:
            self.grads[l][where] = payload
        return None

    def landed(self, where, l, outs):
        pass


def _shards_to_padded(raw):
    sh = raw.shape[2]
    parts = []
    for j in range(N_DEV):
        lo, hi = j * sh, (j + 1) * sh
        if lo < NAT_SPLIT < hi:
            parts += [raw[j][:, :NAT_SPLIT - lo], jnp.zeros((D, H_PAD), raw.dtype), raw[j][:, NAT_SPLIT - lo:]]
        else:
            parts.append(raw[j])
            if hi == NAT_SPLIT:
                parts.append(jnp.zeros((D, H_PAD), raw.dtype))
    return jnp.concatenate(parts, axis=1)


def _padded_to_shards(w):
    sh = IN_W // N_DEV
    pos = lambda c: c if c < NAT_SPLIT else c + H_PAD
    blocks = []
    for j in range(N_DEV):
        lo, hi = j * sh, (j + 1) * sh
        if lo < NAT_SPLIT < hi:
            blocks.append(jnp.concatenate([w[:, lo:NAT_SPLIT], w[:, pos(NAT_SPLIT):pos(hi - 1) + 1]], axis=1))
        else:
            blocks.append(w[:, pos(lo):pos(lo) + sh])
    return jnp.stack(blocks)


class _Fsdp:
    def __init__(self, w_in, w_out, w_kv, extra):
        self.sh = (w_in, w_out, w_kv)
        self.depth = w_in.shape[0]
        self.raw = [dict() for _ in range(self.depth)]
        self.recv = [dict() for _ in range(self.depth)]
        g_in, g_out, g_kv, *self.extra = _exchange([w_in[0], w_out[0], w_kv[0]] + list(extra), False, "gather_layer0")
        self.raw[0] = dict(w_in=g_in, w_out=g_out, w_kv=g_kv)

    def weights(self, l):
        raw = self.raw[l]
        return _shards_to_padded(raw["w_in"]), raw["w_out"].reshape(D, D), raw["w_kv"].reshape(D, 2 * M_W)

    def host(self, where, l, payload=None):
        w_in, w_out, w_kv = self.sh
        if where == "in_proj" and l + 1 < self.depth:
            return _Comm([w_in[l + 1]], False)
        if where == "attn_fwd" and l + 1 < self.depth:
            return _Comm([w_out[l + 1], w_kv[l + 1]], False)
        if where == "attn_bwd":
            return _Comm([payload.reshape(N_DEV, D // N_DEV, D)], True)
        if where == "d_w_in":
            return _Comm([payload.reshape(N_DEV, D // N_DEV, 2 * M_W)], True)
        if where == "d_x":
            return _Comm([_padded_to_shards(payload)], True)
        return None

    def landed(self, where, l, outs):
        if where == "in_proj" and outs:
            self.raw[l + 1]["w_in"] = outs[0]
        elif where == "attn_fwd" and outs:
            self.raw[l + 1]["w_out"], self.raw[l + 1]["w_kv"] = outs
        elif where == "attn_bwd":
            self.recv[l]["w_out"] = outs[0]
        elif where == "d_w_in":
            self.recv[l]["w_kv"] = outs[0]
        elif where == "d_x":
            self.recv[l]["w_in"] = outs[0]


def _local_step(x, mem, target, pipe, rel, gate_w, gate_b, norm_g, ln_g, ln_b):
    depth = pipe.depth
    s = x.shape[0]
    saved = []
    xl = x
    for l in range(depth):
        w_in_p, w_out_f, w_kv_f = pipe.weights(l)
        hmat, landed = _mm(xl, w_in_p, out_dtype=BF16, tm=1024, tn=512, tk=D, name="in_proj", comm=pipe.host("in_proj", l))
        pipe.landed("in_proj", l, landed)
        mkv, _ = _mm(mem, w_kv_f, out_dtype=BF16, tm=N_MEM, tn=1024, tk=D, name="mem_kv")
        bias = _band_bias(rel[l])
        gw = jnp.zeros((128, B_KW), F32).at[:GATE_RANK].set(gate_w[l]).astype(BF16)
        gb, ng = gate_b[l][None, :], norm_g[l][None, :]
        r, lse, landed = _attn_fwd(hmat, lax.empty((s, D), BF16), bias, pipe.host("attn_fwd", l))
        pipe.landed("attn_fwd", l, landed)
        r, opre, states = _gla_fwd(hmat, r, gw, gb, ng)
        r = _mem_fwd(hmat, r, mkv)
        xn, xh, rstd, y = _outproj_ln(hmat, r, w_out_f, xl, ln_g[l][None, :], ln_b[l][None, :])
        saved.append(dict(x=xl, h=hmat, mkv=mkv, bias=bias, gw=gw, gb=gb, ng=ng, r=r, lse=lse, opre=opre,
                          states=states, xh=xh, rstd=rstd, y=y, w_in_p=w_in_p, w_out_f=w_out_f))
        xl = xn
    loss, g = _loss_grad(xl, target)

    grads = [None] * depth
    for l in reversed(range(depth)):
        sv = saved[l]
        du, d_lng, d_lnb = _ln_bwd(g, sv["xh"], sv["rstd"], ln_g[l][None, :])
        d_wout, _ = _mm(sv["y"], du, ta=True, out_dtype=BF16, tm=1024, tn=1024, tk=1024, name="d_w_out")
        dy, _ = _mm(du, sv["w_out_f"], tb=True, out_dtype=BF16, tm=1024, tn=1024, tk=D, name="d_y")
        dh, dbias, landed = _attn_bwd(sv["h"], sv["r"], dy, sv["lse"], sv["bias"], lax.empty((s, H_W), BF16),
                                      pipe.host("attn_bwd", l, d_wout))
        pipe.landed("attn_bwd", l, landed)
        dh, d_gw, d_gb, d_ng = _gla_bwd(sv["h"], dy, sv["opre"], sv["states"], sv["gw"], sv["gb"], sv["ng"], dh)
        dh, d_mkv = _mem_bwd(sv["h"], sv["r"], dy, sv["mkv"], dh)
        d_wkv, _ = _mm(mem, d_mkv, ta=True, out_dtype=BF16, tm=1024, tn=1024, tk=N_MEM, name="d_w_kv")
        d_win, landed = _mm(sv["x"], dh, ta=True, out_dtype=BF16, tm=1024, tn=1792, tk=1024, name="d_w_in",
                            comm=pipe.host("d_w_in", l, d_wkv))
        pipe.landed("d_w_in", l, landed)
        g, landed = _mm(dh, sv["w_in_p"], tb=True, out_dtype=F32, tm=1024, tn=1024, tk=1792, name="d_x",
                        adds=((du, ALPHA),), comm=pipe.host("d_x", l, d_win))
        pipe.landed("d_x", l, landed)
        grads[l] = dict(rel=_bias_grad(dbias), gate_w=d_gw[:GATE_RANK], gate_b=d_gb[0], norm_g=d_ng[0],
                        ln_g=d_lng[0], ln_b=d_lnb[0])
    return loss, g, grads


def _adamw(parts, w, m, v, rows_per_step, name):
    depth, rows, cols = w.shape
    n = parts[0].shape[0]
    tr = min(rows_per_step, rows)
    assert rows % tr == 0 and len(parts) == depth

    def body(*refs):
        p_refs = refs[:depth]
        w_ref, m_ref, v_ref, g_ref, d_ref, nm_ref, nv_ref = refs[depth:]
        for l in range(depth):
            @pl.when(pl.program_id(0) == l)
            def _(p_ref=p_refs[l]):
                g = p_ref[0].astype(F32)
                for j in range(1, n):
                    g = g + p_ref[j].astype(F32)
                nm = ADAM_B1 * m_ref[...] + (1.0 - ADAM_B1) * g
                nv = ADAM_B2 * v_ref[...] + (1.0 - ADAM_B2) * (g * g)
                m_hat = nm / (1.0 - ADAM_B1 ** ADAM_STEP)
                v_hat = nv / (1.0 - ADAM_B2 ** ADAM_STEP)
                g_ref[...] = g
                nm_ref[...] = nm
                nv_ref[...] = nv
                d_ref[...] = -ADAM_LR * (m_hat / (jnp.sqrt(v_hat) + ADAM_EPS) + ADAM_WD * w_ref[...])

    def part_spec(l):
        return pl.BlockSpec((n, tr, cols), lambda ll, i: (0, jnp.where(ll == l, i, 0), 0))

    blk = pl.BlockSpec((None, tr, cols), lambda ll, i: (ll, i, 0))
    shape = jax.ShapeDtypeStruct((depth, rows, cols), F32)
    return pl.pallas_call(
        body, name=name,
        out_shape=(shape, shape, shape, shape),
        grid=(depth, rows // tr),
        in_specs=[part_spec(l) for l in range(depth)] + [blk, blk, blk],
        out_specs=(blk, blk, blk, blk),
        compiler_params=_params(("arbitrary", "arbitrary")),
    )(*parts, w, m, v)


SMALL = (("rel", A_HEADS * (2 * MAX_REL + 1)), ("gate_w", GATE_RANK * B_KW), ("gate_b", B_KW), ("norm_g", B_DV),
         ("ln_g", D), ("ln_b", D))


def _pack_small(parts, depth):
    rows = []
    for name, size in SMALL:
        flat = parts[name].reshape(depth * size).astype(F32)
        rows.append(jnp.pad(flat, (0, -(depth * size) % 128)).reshape(-1, 128))
    packed = jnp.concatenate(rows, axis=0)
    return jnp.pad(packed, ((0, -packed.shape[0] % 8), (0, 0)))


def _unpack_small(packed, depth, shapes):
    out, row = {}, 0
    for name, size in SMALL:
        nrow = -(-(depth * size) // 128)
        out[name] = packed[row:row + nrow].reshape(-1)[:depth * size].reshape(shapes[name])
        row += nrow
    return out


def kernel(x, mem, w_in, a_rel_bias, b_gate_w, b_gate_b, b_norm_g, w_mem_kv, w_out, ln_g, ln_b, loss_target, m_w_in, m_a_rel_bias, m_b_gate_w, m_b_gate_b, m_b_norm_g, m_w_mem_kv, m_w_out, m_ln_g, m_ln_b, v_w_in, v_a_rel_bias, v_b_gate_w, v_b_gate_b, v_b_norm_g, v_w_mem_kv, v_w_out, v_ln_g, v_ln_b):
    depth = w_in.shape[0]
    sh_in = w_in.shape[2]
    sh_gw = b_gate_w.shape[2]
    me = 4 * lax.axis_index("x") + 2 * lax.axis_index("y") + lax.axis_index("c")

    pipe = _Fsdp(w_in.astype(BF16), w_out.astype(BF16), w_mem_kv.astype(BF16), [b_gate_w])
    gate_w_full = jnp.transpose(pipe.extra[0], (1, 2, 0, 3)).reshape(depth, GATE_RANK, N_DEV * sh_gw)
    loss_dev, dx, grads = _local_step(x[0], mem[0], loss_target[0], pipe,
                                      a_rel_bias, gate_w_full, b_gate_b, b_norm_g, ln_g, ln_b)
    loss = lax.psum(loss_dev[0, 0], ("x", "y", "c"))

    recv = lambda n: [pipe.recv[l][n] for l in range(depth)]
    big = {"w_in": _adamw(recv("w_in"), w_in, m_w_in, v_w_in, 128, "adamw_w_in"),
           "w_out": _adamw(recv("w_out"), w_out, m_w_out, v_w_out, 64, "adamw_w_out"),
           "w_kv": _adamw(recv("w_kv"), w_mem_kv, m_w_mem_kv, v_w_mem_kv, 128, "adamw_w_kv")}

    shapes = {"rel": a_rel_bias.shape, "gate_w": (depth, GATE_RANK, N_DEV * sh_gw), "gate_b": b_gate_b.shape,
              "norm_g": b_norm_g.shape, "ln_g": ln_g.shape, "ln_b": ln_b.shape}
    part = _pack_small({n: jnp.stack([grads[l][n] for l in range(depth)]) for n, _ in SMALL}, depth)
    (all_parts,) = _exchange([part], False, "gather_small")
    zeros_gw = jnp.zeros(shapes["gate_w"], F32)
    w_s = _pack_small(dict(rel=a_rel_bias, gate_w=zeros_gw, gate_b=b_gate_b, norm_g=b_norm_g, ln_g=ln_g, ln_b=ln_b), depth)
    m_s = _pack_small(dict(rel=m_a_rel_bias, gate_w=zeros_gw, gate_b=m_b_gate_b, norm_g=m_b_norm_g, ln_g=m_ln_g, ln_b=m_ln_b), depth)
    v_s = _pack_small(dict(rel=v_a_rel_bias, gate_w=zeros_gw, gate_b=v_b_gate_b, norm_g=v_b_norm_g, ln_g=v_ln_g, ln_b=v_ln_b), depth)
    small = [_unpack_small(t[0], depth, shapes)
             for t in _adamw([all_parts], w_s[None], m_s[None], v_s[None], all_parts.shape[1], "adamw_small")]
    gw_grad = lax.dynamic_slice_in_dim(small[0]["gate_w"], me * sh_gw, sh_gw, axis=2).reshape(1, depth * GATE_RANK, sh_gw)
    flat = lambda t: t.reshape(1, depth * GATE_RANK, sh_gw)
    gw_res = [t.reshape(depth, GATE_RANK, sh_gw)
              for t in _adamw([gw_grad], flat(b_gate_w), flat(m_b_gate_w), flat(v_b_gate_w), depth * GATE_RANK, "adamw_gate_w")]

    def leaves(t):
        return (big["w_in"][t], small[t]["rel"], gw_res[t], small[t]["gate_b"], small[t]["norm_g"],
                big["w_kv"][t], big["w_out"][t], small[t]["ln_g"], small[t]["ln_b"])

    return (loss, dx[None]) + leaves(0) + leaves(1) + leaves(2) + leaves(3)
```

```python
import functools
import math

import numpy as np
import jax
import jax.numpy as jnp
from jax import lax
from jax.experimental import pallas as pl
from jax.experimental.pallas import tpu as pltpu

F32 = jnp.float32
BF16 = jnp.bfloat16

N_DEV = 8
D = 2048
CH = 64
LEFT = 8
MAX_REL = 128
N_MEM = 256
A_HEADS, A_DH, A_W = 8, 128, 1024
B_HEADS, B_DK, B_DV, B_KW, B_W = 4, 64, 128, 256, 512
GATE_RANK, GATE_TAU = 16, 16.0
M_HEADS, M_DH, M_W = 4, 128, 512
IN_W = 6672
NAT_SPLIT = 5648
H_W = 7168
H_PAD = H_W - IN_W
A_Q, A_K, A_V, A_Z = 0, 1024, 2048, 3072
B_BASE = 4096
M_BASE = 6144
ALPHA = (2.0 * 4) ** 0.25
LN_EPS = 1e-5
RMS_EPS = 1e-6
NEG = -1e30
QB = 256
KB = 3 * QB
ADAM_LR, ADAM_B1, ADAM_B2, ADAM_EPS, ADAM_WD, ADAM_STEP = 0.001, 0.9, 0.999, 1e-08, 0.01, 10
VMEM_MB = 1024 * 1024


def _params(sem, vmem_mb=48):
    return pltpu.CompilerParams(dimension_semantics=sem, vmem_limit_bytes=vmem_mb * VMEM_MB)


def _sigmoid(x):
    return 1.0 / (1.0 + jnp.exp(-x))


def _dot(a, b, ca, cb, precision=None):
    return lax.dot_general(a, b, (((ca,), (cb,)), ((), ())), preferred_element_type=F32, precision=precision)


MESH = pl.DeviceIdType.MESH


class _Comm:
    def __init__(self, xs, scatter, relations=tuple(range(N_DEV)), into=None):
        self.xs, self.scatter, self.n = list(xs), scatter, len(xs)
        self.relations, self.into = tuple(relations), list(into or [])
        self.out_shape = [jax.ShapeDtypeStruct((N_DEV,) + (x.shape[1:] if scatter else x.shape), x.dtype) for x in xs]
        self.specs = [pl.BlockSpec(memory_space=pltpu.HBM)] * self.n
        self.scratch = [pltpu.SemaphoreType.DMA((self.n, N_DEV)), pltpu.SemaphoreType.DMA((self.n, N_DEV)),
                        pltpu.SemaphoreType.DMA((self.n,))]

    def _copies(self, x_refs, o_refs, sems):
        send_sems, recv_sems, local_sems = sems
        mx, my, mc = lax.axis_index("x"), lax.axis_index("y"), lax.axis_index("c")
        me = 4 * mx + 2 * my + mc
        own, sends, arrivals = [], [], []
        for k in self.relations:
            if k == 0:
                own = [pltpu.make_async_copy(x_refs[a].at[me] if self.scatter else x_refs[a], o_refs[a].at[me],
                                             local_sems.at[a]) for a in range(self.n)]
                continue
            px = 1 - mx if k & 4 else mx
            py = 1 - my if k & 2 else my
            pc = 1 - mc if k & 1 else mc
            idx = 4 * px + 2 * py + pc
            for a in range(self.n):
                src = x_refs[a].at[idx] if self.scatter else x_refs[a]
                for dst, group in ((o_refs[a].at[me], sends), (o_refs[a].at[idx], arrivals)):
                    group.append(pltpu.make_async_remote_copy(
                        src_ref=src, dst_ref=dst, send_sem=send_sems.at[a, k], recv_sem=recv_sems.at[a, k],
                        device_id=(px, py, pc), device_id_type=MESH))
        return own, sends, arrivals

    def start(self, x_refs, o_refs, sems):
        own, sends, _ = self._copies(x_refs, o_refs, sems)
        for cp in own + sends:
            cp.start()

    def finish(self, x_refs, o_refs, sems):
        own, sends, arrivals = self._copies(x_refs, o_refs, sems)
        for cp in sends:
            cp.wait_send()
        for cp in arrivals:
            cp.wait_recv()
        for cp in own:
            cp.wait()


def _exchange(xs, scatter, name):
    comm = _Comm(xs, scatter)

    def body(*refs):
        x_refs, o_refs, sems = refs[:comm.n], refs[comm.n:2 * comm.n], refs[2 * comm.n:]
        comm.start(x_refs, o_refs, sems)
        comm.finish(x_refs, o_refs, sems)

    return pl.pallas_call(body, name=name, out_shape=tuple(comm.out_shape), in_specs=comm.specs,
                          out_specs=tuple(comm.specs), scratch_shapes=comm.scratch)(*comm.xs)


def _hosted(body, comm, *, name, grid, in_specs, out_specs, out_shape, scratch_shapes, compiler_params, inputs,
            input_output_aliases=None):
    out_specs, out_shape = tuple(out_specs), tuple(out_shape)
    aliases = input_output_aliases or {}
    if comm is None:
        outs = pl.pallas_call(body, name=name, grid=grid, in_specs=list(in_specs), out_specs=out_specs, out_shape=out_shape,
                              scratch_shapes=list(scratch_shapes), compiler_params=compiler_params,
                              input_output_aliases=aliases)(*inputs)
        return tuple(outs), ()
    ni, no, ns, nc = len(in_specs), len(out_specs), len(scratch_shapes), comm.n
    n_into = len(comm.into)
    aliases = {**aliases, **{ni + nc + a: no + a for a in range(n_into)}}

    def wrapped(*refs):
        ins, c_in = refs[:ni], refs[ni:ni + nc]
        refs = refs[ni + nc + n_into:]
        outs, c_out = refs[:no], refs[no:no + nc]
        scr, sems = refs[no + nc:no + nc + ns], refs[no + nc + ns:]
        first = functools.reduce(jnp.logical_and, [pl.program_id(d) == 0 for d in range(len(grid))])
        last = functools.reduce(jnp.logical_and, [pl.program_id(d) == grid[d] - 1 for d in range(len(grid))])

        @pl.when(first)
        def _():
            comm.start(c_in, c_out, sems)

        body(*ins, *outs, *scr)

        @pl.when(last)
        def _():
            comm.finish(c_in, c_out, sems)

    params = pltpu.CompilerParams(dimension_semantics=("arbitrary",) * len(grid),
                                  vmem_limit_bytes=compiler_params.vmem_limit_bytes)
    outs = pl.pallas_call(wrapped, name=name, grid=grid, in_specs=list(in_specs) + comm.specs + comm.specs[:n_into],
                          out_specs=out_specs + tuple(comm.specs), out_shape=out_shape + tuple(comm.out_shape),
                          scratch_shapes=list(scratch_shapes) + comm.scratch, compiler_params=params,
                          input_output_aliases=aliases)(*inputs, *comm.xs, *comm.into)
    return tuple(outs[:no]), tuple(outs[no:])


def _mm(a, b, *, ta=False, tb=False, out_dtype, tm, tn, tk, name, adds=(), vmem_mb=48, comm=None):
    m = a.shape[1] if ta else a.shape[0]
    k = a.shape[0] if ta else a.shape[1]
    n = b.shape[0] if tb else b.shape[1]
    assert k == (b.shape[1] if tb else b.shape[0])
    tm, tn, tk = min(tm, m), min(tn, n), min(tk, k)
    assert m % tm == 0 and n % tn == 0 and k % tk == 0, (name, m, n, k)
    nk = k // tk
    n_add = len(adds)
    scales = [s for _, s in adds]

    def body(a_ref, b_ref, *rest):
        add_refs, o_ref = rest[:n_add], rest[n_add]
        kk = pl.program_id(2)

        def product():
            return _dot(a_ref[...].astype(BF16), b_ref[...].astype(BF16), 0 if ta else 1, 1 if tb else 0)

        def finish(r):
            for ref, s in zip(add_refs, scales):
                r = r + s * ref[...].astype(F32)
            o_ref[...] = r.astype(out_dtype)

        if nk == 1:
            finish(product())
            return
        acc_ref = rest[n_add + 1]

        @pl.when(kk == 0)
        def _():
            acc_ref[...] = product()

        if nk > 2:
            @pl.when((kk > 0) & (kk < nk - 1))
            def _():
                acc_ref[...] += product()

        @pl.when(kk == nk - 1)
        def _():
            finish(acc_ref[...] + product())

    a_spec = pl.BlockSpec((tk, tm), lambda i, j, kk: (kk, i)) if ta else pl.BlockSpec((tm, tk), lambda i, j, kk: (i, kk))
    b_spec = pl.BlockSpec((tn, tk), lambda i, j, kk: (j, kk)) if tb else pl.BlockSpec((tk, tn), lambda i, j, kk: (kk, j))
    add_specs = [pl.BlockSpec((tm, tn), lambda i, j, kk: (i, j)) for _ in adds]
    (out,), c_out = _hosted(
        body, comm, name=name,
        out_shape=[jax.ShapeDtypeStruct((m, n), out_dtype)],
        grid=(m // tm, n // tn, nk),
        in_specs=[a_spec, b_spec] + add_specs,
        out_specs=[pl.BlockSpec((tm, tn), lambda i, j, kk: (i, j))],
        scratch_shapes=[pltpu.VMEM((tm, tn), F32)] if nk > 1 else [],
        compiler_params=_params(("parallel", "parallel", "arbitrary"), vmem_mb),
        inputs=(a, b, *[x for x, _ in adds]))
    return out, c_out


NKB = KB // QB
LEAD = NKB - 1


def _band_bias(table):
    i = np.arange(QB)[:, None]
    j = np.arange(KB)[None, :]
    qc = i // CH + LEAD * QB // CH
    kc = j // CH
    valid = (kc <= qc) & (kc >= qc - LEFT)
    n = QB + KB
    c = np.arange(n)
    onehot = np.zeros((2 * MAX_REL + 1, n), np.float32)
    onehot[np.clip(LEAD * QB - (c - (QB - 1)), -MAX_REL, MAX_REL) + MAX_REL, c] = 1.0
    row = jnp.dot(table.astype(F32), jnp.asarray(onehot), precision=lax.Precision.HIGHEST)
    flow = jnp.tile(row, (1, QB))[:, :QB * (n - 1)].reshape(table.shape[0], QB, n - 1)
    return jnp.where(valid[None], flow[:, :, QB - 1:], NEG)


def _bias_grad(dbias):
    h, n = dbias.shape[0], KB + 1
    flat = jnp.pad(dbias.reshape(h, QB * KB), ((0, 0), (0, -(QB * KB) % n)))
    diag = flat.reshape(h, -1, n).sum(axis=1)
    c = np.arange(n)
    jm = np.where(c < LEAD * QB + CH, c, c - n)
    didx = np.clip(LEAD * QB - jm, -MAX_REL, MAX_REL) + MAX_REL
    onehot = np.zeros((n, 2 * MAX_REL + 1), np.float32)
    onehot[c, didx] = 1.0
    return jnp.dot(diag, jnp.asarray(onehot), precision=lax.Precision.HIGHEST)


def _attn_scores(q, k_refs, cs, bias_h, m, masked, scale):
    parts = []
    for t in range(NKB):
        sc = _dot(q, k_refs[t][:, cs], 1, 1) * scale
        parts.append(jnp.where(m + t - LEAD >= 0, sc, NEG) if masked else sc)
    return jnp.concatenate(parts, axis=1) + bias_h


def _attn_fwd(h, r_in, bias, comm=None):
    s = h.shape[0]
    nq = s // QB
    scale = A_DH ** -0.5

    def body(q_ref, *rest):
        k_refs, v_refs = rest[:NKB], rest[NKB:2 * NKB]
        bias_hbm, r_any, o_ref, lse_ref, bias_ref = rest[2 * NKB:]
        del r_any
        m = pl.program_id(0)

        @pl.when(m == 0)
        def _():
            pltpu.sync_copy(bias_hbm, bias_ref)

        def scores(hd, masked):
            cs = slice(hd * A_DH, (hd + 1) * A_DH)
            return _attn_scores(q_ref[:, cs], k_refs, cs, bias_ref[hd], m, masked, scale)

        def step(masked):
            sc_next = scores(0, masked)
            for hd in range(A_HEADS):
                cs = slice(hd * A_DH, (hd + 1) * A_DH)
                sc = sc_next
                if hd + 1 < A_HEADS:
                    sc_next = scores(hd + 1, masked)
                mx = jnp.max(sc, axis=1, keepdims=True)
                p = jnp.exp(sc - mx)
                l = jnp.sum(p, axis=1, keepdims=True)
                pb = p.astype(BF16)
                o = _dot(pb[:, :QB], v_refs[0][:, cs], 1, 0)
                for t in range(1, NKB):
                    o += _dot(pb[:, t * QB:(t + 1) * QB], v_refs[t][:, cs], 1, 0)
                o_ref[:, cs] = (o / l).astype(BF16)
                lse_ref[:, hd:hd + 1] = mx + jnp.log(l)

        pl.when(m < LEAD)(functools.partial(step, True))
        pl.when(m >= LEAD)(functools.partial(step, False))

    def kv_spec(col, t):
        return pl.BlockSpec((QB, A_W), lambda m: (jnp.maximum(m + t - LEAD, 0), col))

    (r, lse), c_out = _hosted(
        body, comm, name="attn_fwd",
        out_shape=(jax.ShapeDtypeStruct(r_in.shape, BF16), jax.ShapeDtypeStruct((s, A_HEADS), F32)),
        grid=(nq,),
        in_specs=[pl.BlockSpec((QB, A_W), lambda m: (m, 0))]
        + [kv_spec(1, t) for t in range(NKB)] + [kv_spec(2, t) for t in range(NKB)]
        + [pl.BlockSpec(memory_space=pl.ANY), pl.BlockSpec(memory_space=pl.ANY)],
        out_specs=(pl.BlockSpec((QB, A_W), lambda m: (m, 0)), pl.BlockSpec((QB, A_HEADS), lambda m: (m, 0))),
        scratch_shapes=[pltpu.VMEM((A_HEADS, QB, KB), F32)],
        input_output_aliases={2 * NKB + 2: 0},
        compiler_params=_params(("arbitrary",)),
        inputs=(h,) * (2 * NKB + 1) + (bias, r_in))
    return r, lse, c_out


def _attn_bwd(h, r, dy, lse, bias, dh_in, comm=None):
    s = h.shape[0]
    nq = s // QB
    scale = A_DH ** -0.5

    def body(q_ref, *rest):
        k_refs, v_refs = rest[:NKB], rest[NKB:2 * NKB]
        (z_ref, r_ref, dy_ref, lse_ref, zl_ref, rl_ref, dyl_ref, bias_hbm, dh_any,
         dh_ref, dbias_hbm, dk_acc, dv_acc, dq_ring, bias_ref, dbias_ref) = rest[2 * NKB:]
        del dh_any
        m = pl.program_id(0)

        def slot(b):
            return pl.multiple_of(lax.rem(b + NKB, NKB) * QB, QB)

        @pl.when(m == 0)
        def _():
            dk_acc[...] = jnp.zeros_like(dk_acc)
            dv_acc[...] = jnp.zeros_like(dv_acc)
            dq_ring[...] = jnp.zeros_like(dq_ring)
            dbias_ref[...] = jnp.zeros_like(dbias_ref)
            pltpu.sync_copy(bias_hbm, bias_ref)

        def step(masked):
            z = z_ref[...].astype(F32)
            do_all = dy_ref[...].astype(F32) * (z * _sigmoid(z))
            o_all = r_ref[...].astype(F32)

            def products(hd):
                cs = slice(hd * A_DH, (hd + 1) * A_DH)
                dob = do_all[:, cs].astype(BF16)
                sc = _attn_scores(q_ref[:, cs], k_refs, cs, bias_ref[hd], m, masked, scale)
                return sc, jnp.concatenate([_dot(dob, v_refs[t][:, cs], 1, 1) for t in range(NKB)], axis=1)

            ahead = products(0)
            for hd in range(A_HEADS):
                cs = slice(hd * A_DH, (hd + 1) * A_DH)
                q = q_ref[:, cs]
                do = do_all[:, cs]
                dob = do.astype(BF16)
                delta = jnp.sum(do * o_all[:, cs], axis=1, keepdims=True)
                sc, dp = ahead
                if hd + 1 < A_HEADS:
                    ahead = products(hd + 1)
                p = jnp.exp(sc - lse_ref[:, hd:hd + 1])
                ds = p * (dp - delta)
                dbias_ref[hd] += ds
                pb, dsb = p.astype(BF16), ds.astype(BF16)
                dq = jnp.zeros((QB, A_DH), F32)
                for t in range(NKB):
                    ts = slice(t * QB, (t + 1) * QB)
                    rows = pl.ds(slot(m - LEAD + t), QB)
                    dq += _dot(dsb[:, ts], k_refs[t][:, cs], 1, 0)
                    dk_acc[rows, cs] += _dot(dsb[:, ts], q, 0, 0) * scale
                    dv_acc[rows, cs] += _dot(pb[:, ts], dob, 0, 0)
                dq_ring[pl.ds(slot(m), QB), cs] = dq * scale

        pl.when(m < LEAD)(functools.partial(step, True))
        pl.when((m >= LEAD) & (m < nq))(functools.partial(step, False))

        done = pl.ds(slot(m - LEAD), QB)
        zl = zl_ref[...].astype(F32)
        sg = _sigmoid(zl)
        dz = dyl_ref[...].astype(F32) * rl_ref[...].astype(F32) * (sg * (1.0 + zl * (1.0 - sg)))
        dh_ref[:, A_Q:A_Q + A_W] = dq_ring[done, :].astype(BF16)
        dh_ref[:, A_K:A_K + A_W] = dk_acc[done, :].astype(BF16)
        dh_ref[:, A_V:A_V + A_W] = dv_acc[done, :].astype(BF16)
        dh_ref[:, A_Z:A_Z + A_W] = dz.astype(BF16)
        dk_acc[done, :] = jnp.zeros((QB, A_W), F32)
        dv_acc[done, :] = jnp.zeros((QB, A_W), F32)

        @pl.when(m == nq + LEAD - 1)
        def _():
            pltpu.sync_copy(dbias_ref, dbias_hbm)

    last = nq - 1

    def cur(col):
        return pl.BlockSpec((QB, A_W), lambda m: (jnp.minimum(m, last), col))

    def kv_spec(col, t):
        return pl.BlockSpec((QB, A_W), lambda m: (jnp.clip(m + t - LEAD, 0, last), col))

    def lag(col):
        return pl.BlockSpec((QB, A_W), lambda m: (jnp.clip(m - LEAD, 0, last), col))

    (dh, dbias), c_out = _hosted(
        body, comm, name="attn_bwd",
        out_shape=(jax.ShapeDtypeStruct(dh_in.shape, BF16), jax.ShapeDtypeStruct((A_HEADS, QB, KB), F32)),
        grid=(nq + LEAD,),
        in_specs=[cur(0)] + [kv_spec(1, t) for t in range(NKB)] + [kv_spec(2, t) for t in range(NKB)]
        + [cur(3), cur(0), cur(0), pl.BlockSpec((QB, A_HEADS), lambda m: (jnp.minimum(m, last), 0)),
           lag(3), lag(0), lag(0),
           pl.BlockSpec(memory_space=pl.ANY), pl.BlockSpec(memory_space=pl.ANY)],
        out_specs=(pl.BlockSpec((QB, 4 * A_W), lambda m: (jnp.clip(m - LEAD, 0, last), 0)),
                   pl.BlockSpec(memory_space=pl.ANY)),
        scratch_shapes=[pltpu.VMEM((KB, A_W), F32), pltpu.VMEM((KB, A_W), F32), pltpu.VMEM((KB, A_W), F32),
                        pltpu.VMEM((A_HEADS, QB, KB), F32), pltpu.VMEM((A_HEADS, QB, KB), F32)],
        input_output_aliases={2 * NKB + 9: 0},
        compiler_params=_params(("arbitrary",), 56),
        inputs=(h,) * (2 * NKB + 2) + (r, dy, lse, h, r, dy, bias, dh_in))
    return dh, dbias, c_out


GB = 256
N_PAIR = B_HEADS // 2


def _gla_gates(lr, gw_ref, gb_ref):
    logit = _dot(lr, gw_ref[...], 1, 0) + gb_ref[...]
    lg = (jnp.minimum(logit, 0.0) - jnp.log(1.0 + jnp.exp(-jnp.abs(logit)))) / GATE_TAU
    row = lax.broadcasted_iota(jnp.int32, (GB, GB), 0)
    col = lax.broadcasted_iota(jnp.int32, (GB, GB), 1)
    tri = jnp.where((row // CH == col // CH) & (col <= row), 1.0, 0.0).astype(F32)
    return logit, _dot(tri, lg, 1, 0, precision=lax.Precision.HIGHEST)


def _gla_factors(hb_ref, b_all, c):
    rs = slice(c * CH, (c + 1) * CH)
    q = hb_ref[rs, 0:B_KW].astype(F32) * (B_DK ** -0.5)
    k = hb_ref[rs, B_KW:2 * B_KW].astype(F32)
    b = b_all[rs]
    bm, bl = b[CH // 2:CH // 2 + 1, :], b[CH - 1:CH, :]
    e1, e2, eb, ek = jnp.exp(b - bm), jnp.exp(bm - b), jnp.exp(b), jnp.exp(bl - b)
    el = jnp.exp(bl)
    return dict(ql=q * e1, kl=k * e2, qu=q * e2, ku=k * e1, qt=q * eb, kh=k * ek, e1=e1, e2=e2, eb=eb, ek=ek, el=el)


def _gla_fwd(h, r_in, gw, gb, ng, comm=None):
    s = h.shape[0]
    nb = s // GB
    cpb = GB // CH

    def body(hb_ref, gw_ref, gb_ref, ng_ref, r_any, o_ref, opre_ref, st_ref, state):
        del r_any

        @pl.when(pl.program_id(0) == 0)
        def _():
            state[...] = jnp.zeros_like(state)

        _, b_all = _gla_gates(hb_ref[:, 1536:1664], gw_ref, gb_ref)
        lane = lax.broadcasted_iota(jnp.int32, (CH, 128), 1)
        ri = lax.broadcasted_iota(jnp.int32, (CH, CH), 0)
        ci = lax.broadcasted_iota(jnp.int32, (CH, CH), 1)
        fs = [_gla_factors(hb_ref, b_all, c) for c in range(cpb)]
        heads = [(c, p, sh) for c in range(cpb) for p in range(N_PAIR) for sh in range(2)]

        def pick(c, p, sh, name):
            return jnp.where((lane // B_DK) == sh, fs[c][name][:, p * 128:(p + 1) * 128], 0.0).astype(BF16)

        def pair(c, p, name):
            return fs[c][name][:, p * 128:(p + 1) * 128].astype(BF16)

        def vals(c, p, sh):
            hd = 2 * p + sh
            return hb_ref[c * CH:(c + 1) * CH, 512 + hd * B_DV:512 + (hd + 1) * B_DV]

        a = {k: jnp.where(ri >= ci, _dot(pick(*k, "ql"), pair(k[0], k[1], "kl"), 1, 1),
                          _dot(pick(*k, "qu"), pair(k[0], k[1], "ku"), 1, 1)).astype(BF16) for k in heads}
        o_intra = {k: _dot(a[k], vals(*k), 1, 0) for k in heads}
        upd = {k: _dot(vals(*k), pick(*k, "kh"), 0, 0) for k in heads}
        st = [state[p] for p in range(N_PAIR)]
        entering = {}
        for c in range(cpb):
            for p in range(N_PAIR):
                entering[c, p] = st[p]
                st_ref[c, p] = st[p]
                st[p] = st[p] * fs[c]["el"][:, p * 128:(p + 1) * 128] + upd[c, p, 0] + upd[c, p, 1]
        for p in range(N_PAIR):
            state[p] = st[p]
        for k in heads:
            c, p, sh = k
            o = o_intra[k] + _dot(pick(*k, "qt"), entering[c, p].astype(BF16), 1, 1)
            rs, hs = slice(c * CH, (c + 1) * CH), slice((2 * p + sh) * B_DV, (2 * p + sh + 1) * B_DV)
            opre_ref[rs, hs] = o
            rinv = lax.rsqrt(jnp.mean(o * o, axis=1, keepdims=True) + RMS_EPS)
            o_ref[rs, hs] = (o * rinv * ng_ref[...]).astype(BF16)

    (r, opre, states), c_out = _hosted(
        body, comm, name="gla_fwd",
        out_shape=(jax.ShapeDtypeStruct(r_in.shape, BF16), jax.ShapeDtypeStruct((s, B_W), F32),
                   jax.ShapeDtypeStruct((s // CH, N_PAIR, 128, 128), F32)),
        grid=(nb,),
        in_specs=[pl.BlockSpec((GB, 2048), lambda i: (i, B_BASE // 2048)),
                  pl.BlockSpec((128, B_KW), lambda i: (0, 0)), pl.BlockSpec((1, B_KW), lambda i: (0, 0)),
                  pl.BlockSpec((1, B_DV), lambda i: (0, 0)), pl.BlockSpec(memory_space=pl.ANY)],
        out_specs=(pl.BlockSpec((GB, B_W), lambda i: (i, 1024 // B_W)), pl.BlockSpec((GB, B_W), lambda i: (i, 0)),
                   pl.BlockSpec((cpb, N_PAIR, 128, 128), lambda i: (i, 0, 0, 0))),
        scratch_shapes=[pltpu.VMEM((N_PAIR, 128, 128), F32)],
        input_output_aliases={4: 0},
        compiler_params=_params(("arbitrary",)),
        inputs=(h, gw, gb, ng, r_in))
    return r, opre, states, c_out


def _gla_bwd(h, dy, opre, states, gw, gb, ng, dh_in, comm=None):
    s = h.shape[0]
    nb = s // GB
    cpb = GB // CH

    def body(hb_ref, dy_ref, opre_ref, st_ref, gw_ref, gb_ref, ng_ref, dh_any,
             dh_ref, dgw_ref, dgb_ref, dng_ref, dstate, db_scr, do_scr):
        del dh_any

        @pl.when(pl.program_id(0) == 0)
        def _():
            dstate[...] = jnp.zeros_like(dstate)
            dgw_ref[...] = jnp.zeros_like(dgw_ref)
            dgb_ref[...] = jnp.zeros_like(dgb_ref)
            dng_ref[...] = jnp.zeros_like(dng_ref)

        lr = hb_ref[:, 1536:1664]
        logit, b_all = _gla_gates(lr, gw_ref, gb_ref)
        z = hb_ref[:, 1024:1536].astype(F32)
        sg = _sigmoid(z)
        dyb = dy_ref[...].astype(F32)
        dng = jnp.zeros((1, B_DV), F32)
        for hd in range(B_HEADS):
            hs = slice(hd * B_DV, (hd + 1) * B_DV)
            o = opre_ref[:, hs]
            rinv = lax.rsqrt(jnp.mean(o * o, axis=1, keepdims=True) + RMS_EPS)
            on = o * rinv
            dr = dyb[:, hs] * (z[:, hs] * sg[:, hs])
            dh_ref[:, 1024 + hd * B_DV:1024 + (hd + 1) * B_DV] = (
                dyb[:, hs] * (on * ng_ref[...]) * (sg[:, hs] * (1.0 + z[:, hs] * (1.0 - sg[:, hs])))).astype(BF16)
            dng += jnp.sum(dr * on, axis=0, keepdims=True)
            dn = dr * ng_ref[...]
            do_scr[:, hs] = rinv * (dn - on * jnp.mean(dn * on, axis=1, keepdims=True))
        dng_ref[...] += dng

        lane = lax.broadcasted_iota(jnp.int32, (CH, 128), 1)
        ri = lax.broadcasted_iota(jnp.int32, (CH, CH), 0)
        ci = lax.broadcasted_iota(jnp.int32, (CH, CH), 1)
        rowi = lax.broadcasted_iota(jnp.int32, (CH, 128), 0)
        fs = [_gla_factors(hb_ref, b_all, c) for c in range(cpb)]
        heads = [(c, p, sh) for c in reversed(range(cpb)) for p in range(N_PAIR) for sh in range(2)]

        def mask(sh, x):
            return jnp.where((lane // B_DK) == sh, x, 0.0)

        def pick(c, p, sh, name):
            return mask(sh, fs[c][name][:, p * 128:(p + 1) * 128]).astype(BF16)

        def pair(c, p, name):
            return fs[c][name][:, p * 128:(p + 1) * 128]

        def vals(c, p, sh):
            hd = 2 * p + sh
            return hb_ref[c * CH:(c + 1) * CH, 512 + hd * B_DV:512 + (hd + 1) * B_DV]

        def douts(c, p, sh):
            hd = 2 * p + sh
            return do_scr[c * CH:(c + 1) * CH, hd * B_DV:(hd + 1) * B_DV].astype(BF16)

        at, da, dat, dst_own, d_qt = {}, {}, {}, {}, {}
        for k in heads:
            c, p, sh = k
            klp, kup = pair(c, p, "kl").astype(BF16), pair(c, p, "ku").astype(BF16)
            at[k] = jnp.where(ci >= ri, _dot(klp, pick(*k, "ql"), 1, 1), _dot(kup, pick(*k, "qu"), 1, 1)).astype(BF16)
            da[k] = _dot(douts(*k), vals(*k), 1, 1)
            dat[k] = _dot(vals(*k), douts(*k), 1, 1)
            dst_own[k] = _dot(douts(*k), pick(*k, "qt"), 0, 0)
            d_qt[k] = mask(sh, _dot(douts(*k), st_ref[c, p].astype(BF16), 1, 0))
        dv, d_ql, d_qu, d_kl, d_ku = {}, {}, {}, {}, {}
        for k in heads:
            c, p, sh = k
            klp, kup = pair(c, p, "kl").astype(BF16), pair(c, p, "ku").astype(BF16)
            dv[k] = _dot(at[k], douts(*k), 1, 0)
            d_ql[k] = mask(sh, _dot(jnp.where(ri >= ci, da[k], 0.0).astype(BF16), klp, 1, 0))
            d_qu[k] = mask(sh, _dot(jnp.where(ri < ci, da[k], 0.0).astype(BF16), kup, 1, 0))
            d_kl[k] = _dot(jnp.where(ci >= ri, dat[k], 0.0).astype(BF16), pick(*k, "ql"), 1, 0)
            d_ku[k] = _dot(jnp.where(ci < ri, dat[k], 0.0).astype(BF16), pick(*k, "qu"), 1, 0)
        dst = [dstate[p] for p in range(N_PAIR)]
        leaving = {}
        for c in reversed(range(cpb)):
            for p in range(N_PAIR):
                leaving[c, p] = dst[p]
                dst[p] = dst[p] * pair(c, p, "el") + dst_own[c, p, 0] + dst_own[c, p, 1]
        for p in range(N_PAIR):
            dstate[p] = dst[p]
        d_kh = {}
        for k in heads:
            c, p, sh = k
            dstb = leaving[c, p].astype(BF16)
            hd = 2 * p + sh
            dh_ref[c * CH:(c + 1) * CH, 512 + hd * B_DV:512 + (hd + 1) * B_DV] = (
                dv[k] + _dot(pick(*k, "kh"), dstb, 1, 1)).astype(BF16)
            d_kh[k] = mask(sh, _dot(vals(*k), dstb, 1, 0))
        for c in reversed(range(cpb)):
            rs = slice(c * CH, (c + 1) * CH)
            for p in range(N_PAIR):
                both = lambda d: d[c, p, 0] + d[c, p, 1]
                g_ql, g_qu, g_qt, g_kl, g_ku, g_kh = both(d_ql), both(d_qu), both(d_qt), both(d_kl), both(d_ku), both(d_kh)
                dq = (g_ql * pair(c, p, "e1") + g_qu * pair(c, p, "e2") + g_qt * pair(c, p, "eb")) * (B_DK ** -0.5)
                dk = g_kl * pair(c, p, "e2") + g_ku * pair(c, p, "e1") + g_kh * pair(c, p, "ek")
                dkh_kh = g_kh * pair(c, p, "kh")
                db = (g_ql * pair(c, p, "ql") - g_qu * pair(c, p, "qu") + g_qt * pair(c, p, "qt")
                      - g_kl * pair(c, p, "kl") + g_ku * pair(c, p, "ku") - dkh_kh)
                db_last = (pair(c, p, "el") * jnp.sum(leaving[c, p] * st_ref[c, p], axis=0, keepdims=True)
                           + jnp.sum(dkh_kh, axis=0, keepdims=True))
                db = jnp.where(rowi == CH - 1, db + db_last, db)
                dh_ref[rs, p * 128:(p + 1) * 128] = dq.astype(BF16)
                dh_ref[rs, B_KW + p * 128:B_KW + (p + 1) * 128] = dk.astype(BF16)
                db_scr[rs, p * 128:(p + 1) * 128] = db

        row = lax.broadcasted_iota(jnp.int32, (GB, GB), 0)
        col = lax.broadcasted_iota(jnp.int32, (GB, GB), 1)
        trit = jnp.where((row // CH == col // CH) & (col >= row), 1.0, 0.0).astype(F32)
        dlg = _dot(trit, db_scr[...], 1, 0, precision=lax.Precision.HIGHEST)
        dlogit = dlg * (_sigmoid(-logit) / GATE_TAU)
        dlb = dlogit.astype(BF16)
        dgw_ref[...] += _dot(lr, dlb, 0, 0)
        dgb_ref[...] += jnp.sum(dlogit, axis=0, keepdims=True)
        dh_ref[:, 1536:1664] = _dot(dlb, gw_ref[...], 1, 1).astype(BF16)
        dh_ref[:, 1664:2048] = jnp.zeros((GB, 384), BF16)

    rev = lambda i: nb - 1 - i
    (dh, dgw, dgb, dng), c_out = _hosted(
        body, comm, name="gla_bwd",
        out_shape=(jax.ShapeDtypeStruct(dh_in.shape, BF16), jax.ShapeDtypeStruct((128, B_KW), F32),
                   jax.ShapeDtypeStruct((1, B_KW), F32), jax.ShapeDtypeStruct((1, B_DV), F32)),
        grid=(nb,),
        in_specs=[pl.BlockSpec((GB, 2048), lambda i: (rev(i), B_BASE // 2048)),
                  pl.BlockSpec((GB, B_W), lambda i: (rev(i), 1024 // B_W)),
                  pl.BlockSpec((GB, B_W), lambda i: (rev(i), 0)),
                  pl.BlockSpec((cpb, N_PAIR, 128, 128), lambda i: (rev(i), 0, 0, 0)),
                  pl.BlockSpec((128, B_KW), lambda i: (0, 0)), pl.BlockSpec((1, B_KW), lambda i: (0, 0)),
                  pl.BlockSpec((1, B_DV), lambda i: (0, 0)), pl.BlockSpec(memory_space=pl.ANY)],
        out_specs=(pl.BlockSpec((GB, 2048), lambda i: (rev(i), B_BASE // 2048)),
                   pl.BlockSpec((128, B_KW), lambda i: (0, 0)), pl.BlockSpec((1, B_KW), lambda i: (0, 0)),
                   pl.BlockSpec((1, B_DV), lambda i: (0, 0))),
        scratch_shapes=[pltpu.VMEM((N_PAIR, 128, 128), F32), pltpu.VMEM((GB, B_KW), F32), pltpu.VMEM((GB, B_W), F32)],
        input_output_aliases={7: 0},
        compiler_params=_params(("arbitrary",)),
        inputs=(h, dy, opre, states, gw, gb, ng, dh_in))
    return dh, dgw, dgb, dng, c_out


MB = 512


def _mem_probs(q, mk, scale):
    sc = _dot(q, mk, 1, 1) * scale
    p = jnp.exp(sc - jnp.max(sc, axis=1, keepdims=True))
    return p / jnp.sum(p, axis=1, keepdims=True)


def _mem_fwd(h, r_in, mkv):
    s = h.shape[0]
    scale = M_DH ** -0.5

    def body(q_ref, mkv_ref, r_any, o_ref):
        del r_any
        for hd in range(M_HEADS):
            cs = slice(hd * M_DH, (hd + 1) * M_DH)
            p = _mem_probs(q_ref[:, cs], mkv_ref[:, cs], scale)
            o_ref[:, cs] = _dot(p.astype(BF16), mkv_ref[:, M_W + hd * M_DH:M_W + (hd + 1) * M_DH], 1, 0).astype(BF16)

    return pl.pallas_call(
        body, name="mem_fwd",
        out_shape=jax.ShapeDtypeStruct(r_in.shape, BF16),
        grid=(s // MB,),
        in_specs=[pl.BlockSpec((MB, M_W), lambda i: (i, M_BASE // M_W)),
                  pl.BlockSpec((N_MEM, 2 * M_W), lambda i: (0, 0)), pl.BlockSpec(memory_space=pl.ANY)],
        out_specs=pl.BlockSpec((MB, M_W), lambda i: (i, 1536 // M_W)),
        input_output_aliases={2: 0},
        compiler_params=_params(("arbitrary",)),
    )(h, mkv, r_in)


def _mem_bwd(h, r, dy, mkv, dh_in):
    s = h.shape[0]
    scale = M_DH ** -0.5

    def body(q_ref, z_ref, r_ref, dy_ref, mkv_ref, dh_any, dh_ref, dmkv_ref):
        del dh_any

        @pl.when(pl.program_id(0) == 0)
        def _():
            dmkv_ref[...] = jnp.zeros_like(dmkv_ref)

        z = z_ref[...].astype(F32)
        sg = _sigmoid(z)
        dyv = dy_ref[...].astype(F32)
        do_all = dyv * (z * sg)
        dh_ref[:, M_W:2 * M_W] = (dyv * r_ref[...].astype(F32) * (sg * (1.0 + z * (1.0 - sg)))).astype(BF16)
        for hd in range(M_HEADS):
            cs = slice(hd * M_DH, (hd + 1) * M_DH)
            vs = slice(M_W + hd * M_DH, M_W + (hd + 1) * M_DH)
            q = q_ref[:, cs]
            p = _mem_probs(q, mkv_ref[:, cs], scale)
            dob = do_all[:, cs].astype(BF16)
            dp = _dot(dob, mkv_ref[:, vs], 1, 1)
            ds = p * (dp - jnp.sum(p * dp, axis=1, keepdims=True))
            dsb = ds.astype(BF16)
            dh_ref[:, cs] = (_dot(dsb, mkv_ref[:, cs], 1, 0) * scale).astype(BF16)
            dmkv_ref[:, cs] += _dot(dsb, q, 0, 0) * scale
            dmkv_ref[:, vs] += _dot(p.astype(BF16), dob, 0, 0)

    return pl.pallas_call(
        body, name="mem_bwd",
        out_shape=(jax.ShapeDtypeStruct(dh_in.shape, BF16), jax.ShapeDtypeStruct((N_MEM, 2 * M_W), F32)),
        grid=(s // MB,),
        in_specs=[pl.BlockSpec((MB, M_W), lambda i: (i, M_BASE // M_W)),
                  pl.BlockSpec((MB, M_W), lambda i: (i, M_BASE // M_W + 1)),
                  pl.BlockSpec((MB, M_W), lambda i: (i, 1536 // M_W)),
                  pl.BlockSpec((MB, M_W), lambda i: (i, 1536 // M_W)),
                  pl.BlockSpec((N_MEM, 2 * M_W), lambda i: (0, 0)), pl.BlockSpec(memory_space=pl.ANY)],
        out_specs=(pl.BlockSpec((MB, 2 * M_W), lambda i: (i, M_BASE // (2 * M_W))),
                   pl.BlockSpec((N_MEM, 2 * M_W), lambda i: (0, 0))),
        input_output_aliases={5: 0},
        compiler_params=_params(("arbitrary",)),
    )(h, h, r, dy, mkv, dh_in)


OB = 256


def _outproj_ln(h, r, w_out, x, ln_g, ln_b):
    s = h.shape[0]

    def body(za_ref, zb_ref, zm_ref, r_ref, w_ref, x_ref, g_ref, b_ref, xn_ref, xh_ref, rstd_ref, y_ref):
        z = jnp.concatenate([za_ref[...], zb_ref[...], zm_ref[...]], axis=1).astype(F32)
        y = (r_ref[...].astype(F32) * (z * _sigmoid(z))).astype(BF16)
        y_ref[...] = y
        u = ALPHA * x_ref[...] + _dot(y, w_ref[...], 1, 0)
        mu = jnp.mean(u, axis=1, keepdims=True)
        uc = u - mu
        rstd = lax.rsqrt(jnp.mean(uc * uc, axis=1, keepdims=True) + LN_EPS)
        xh = uc * rstd
        xh_ref[...] = xh
        rstd_ref[...] = rstd
        xn_ref[...] = xh * g_ref[...] + b_ref[...]

    row = lambda w, c: pl.BlockSpec((OB, w), lambda i: (i, c))
    vec = pl.BlockSpec((1, D), lambda i: (0, 0))
    return pl.pallas_call(
        body, name="outproj_ln",
        out_shape=(jax.ShapeDtypeStruct((s, D), F32), jax.ShapeDtypeStruct((s, D), F32),
                   jax.ShapeDtypeStruct((s, 1), F32), jax.ShapeDtypeStruct((s, D), BF16)),
        grid=(s // OB,),
        in_specs=[row(A_W, A_Z // A_W), row(B_W, (B_BASE + 1024) // B_W), row(M_W, (M_BASE + M_W) // M_W), row(D, 0),
                  pl.BlockSpec((D, D), lambda i: (0, 0)), row(D, 0), vec, vec],
        out_specs=(row(D, 0), row(D, 0), pl.BlockSpec((OB, 1), lambda i: (i, 0)), row(D, 0)),
        compiler_params=_params(("arbitrary",), 56),
    )(h, h, h, r, w_out, x, ln_g, ln_b)


def _ln_bwd(g, xh, rstd, ln_g):
    s = g.shape[0]

    def body(g_ref, xh_ref, rstd_ref, lg_ref, du_ref, dg_ref, db_ref):
        @pl.when(pl.program_id(0) == 0)
        def _():
            dg_ref[...] = jnp.zeros_like(dg_ref)
            db_ref[...] = jnp.zeros_like(db_ref)

        gv, xh = g_ref[...], xh_ref[...]
        dg_ref[...] += jnp.sum(gv * xh, axis=0, keepdims=True)
        db_ref[...] += jnp.sum(gv, axis=0, keepdims=True)
        dxh = gv * lg_ref[...]
        du_ref[...] = rstd_ref[...] * (dxh - jnp.mean(dxh, axis=1, keepdims=True)
                                       - xh * jnp.mean(dxh * xh, axis=1, keepdims=True))

    row = pl.BlockSpec((OB, D), lambda i: (i, 0))
    vec = pl.BlockSpec((1, D), lambda i: (0, 0))
    return pl.pallas_call(
        body, name="ln_bwd",
        out_shape=(jax.ShapeDtypeStruct((s, D), F32), jax.ShapeDtypeStruct((1, D), F32), jax.ShapeDtypeStruct((1, D), F32)),
        grid=(s // OB,),
        in_specs=[row, row, pl.BlockSpec((OB, 1), lambda i: (i, 0)), vec],
        out_specs=(row, vec, vec),
        compiler_params=_params(("arbitrary",)),
    )(g, xh, rstd, ln_g)


def _loss_grad(y, target):
    s = y.shape[0]

    def body(y_ref, t_ref, l_ref, dy_ref):
        @pl.when(pl.program_id(0) == 0)
        def _():
            l_ref[...] = jnp.zeros_like(l_ref)

        e = y_ref[...] - t_ref[...]
        dy_ref[...] = e / D
        l_ref[...] += 0.5 * jnp.sum(jnp.mean(e * e, axis=1, keepdims=True))

    row = pl.BlockSpec((OB, D), lambda i: (i, 0))
    return pl.pallas_call(
        body, name="loss_grad",
        out_shape=(jax.ShapeDtypeStruct((1, 128), F32), jax.ShapeDtypeStruct((s, D), F32)),
        grid=(s // OB,),
        in_specs=[row, row],
        out_specs=(pl.BlockSpec((1, 128), lambda i: (0, 0)), row),
        compiler_params=_params(("arbitrary",)),
    )(y, target)


class _LocalWeights:
    def __init__(self, w_in_p, w_out_f, w_kv_f):
        self.w = list(zip(w_in_p, w_out_f, w_kv_f))
        self.depth = len(self.w)
        self.grads = [dict() for _ in self.w]

    def weights(self, l):
        return self.w[l]

    def host(self, where, l, payload=None):
        if payload is not None:
            self.grads[l][where] = payload
        return None

    def landed(self, where, l, outs):
        pass


def _shards_to_padded(raw):
    sh = raw.shape[2]
    parts = []
    for j in range(N_DEV):
        lo, hi = j * sh, (j + 1) * sh
        if lo < NAT_SPLIT < hi:
            parts += [raw[j][:, :NAT_SPLIT - lo], jnp.zeros((D, H_PAD), raw.dtype), raw[j][:, NAT_SPLIT - lo:]]
        else:
            parts.append(raw[j])
            if hi == NAT_SPLIT:
                parts.append(jnp.zeros((D, H_PAD), raw.dtype))
    return jnp.concatenate(parts, axis=1)


def _padded_to_shards(w):
    sh = IN_W // N_DEV
    pos = lambda c: c if c < NAT_SPLIT else c + H_PAD
    blocks = []
    for j in range(N_DEV):
        lo, hi = j * sh, (j + 1) * sh
        if lo < NAT_SPLIT < hi:
            blocks.append(jnp.concatenate([w[:, lo:NAT_SPLIT], w[:, pos(NAT_SPLIT):pos(hi - 1) + 1]], axis=1))
        else:
            blocks.append(w[:, pos(lo):pos(lo) + sh])
    return jnp.stack(blocks)


_PART_A = (0, 1, 2, 4, 7)
_PART_B = (3, 5, 6)


class _Fsdp:
    def __init__(self, w_in, w_out, w_kv, extra):
        self.sh = (w_in, w_out, w_kv)
        self.depth = w_in.shape[0]
        self.raw = [dict() for _ in range(self.depth)]
        self.recv = [dict() for _ in range(self.depth)]
        g_in, g_out, g_kv, *self.extra = _exchange([w_in[0], w_out[0], w_kv[0]] + list(extra), False, "gather_layer0")
        self.raw[0] = dict(w_in=g_in, w_out=g_out, w_kv=g_kv)

    def weights(self, l):
        raw = self.raw[l]
        return _shards_to_padded(raw["w_in"]), raw["w_out"].reshape(D, D), raw["w_kv"].reshape(D, 2 * M_W)

    def host(self, where, l, payload=None):
        w_in, w_out, w_kv = self.sh
        if where == "in_proj" and l + 1 < self.depth:
            return _Comm([w_in[l + 1]], False)
        if where == "attn_fwd" and l + 1 < self.depth:
            return _Comm([w_out[l + 1]], False)
        if where == "gla_fwd" and l + 1 < self.depth:
            return _Comm([w_kv[l + 1]], False)
        if where == "gla_bwd":
            return _Comm([payload.reshape(N_DEV, D // N_DEV, D)], True)
        if where == "d_w_in":
            return _Comm([payload.reshape(N_DEV, D // N_DEV, 2 * M_W)], True)
        if where == "d_x":
            self.blocks = _padded_to_shards(payload)
            return _Comm([self.blocks], True, relations=_PART_A if l > 0 else tuple(range(N_DEV)))
        if where == "attn_bwd" and l + 1 < self.depth:
            return _Comm([self.blocks], True, relations=_PART_B, into=[self.recv[l + 1]["w_in"]])
        return None

    def landed(self, where, l, outs):
        if where == "in_proj" and outs:
            self.raw[l + 1]["w_in"] = outs[0]
        elif where == "attn_fwd" and outs:
            self.raw[l + 1]["w_out"] = outs[0]
        elif where == "gla_fwd" and outs:
            self.raw[l + 1]["w_kv"] = outs[0]
        elif where == "gla_bwd":
            self.recv[l]["w_out"] = outs[0]
        elif where == "d_w_in":
            self.recv[l]["w_kv"] = outs[0]
        elif where == "d_x":
            self.recv[l]["w_in"] = outs[0]
        elif where == "attn_bwd" and outs:
            self.recv[l + 1]["w_in"] = outs[0]


def _local_step(x, mem, target, pipe, rel, gate_w, gate_b, norm_g, ln_g, ln_b):
    depth = pipe.depth
    s = x.shape[0]
    saved = []
    xl = x
    for l in range(depth):
        w_in_p, w_out_f, w_kv_f = pipe.weights(l)
        hmat, landed = _mm(xl, w_in_p, out_dtype=BF16, tm=1024, tn=512, tk=D, name="in_proj", comm=pipe.host("in_proj", l))
        pipe.landed("in_proj", l, landed)
        mkv, _ = _mm(mem, w_kv_f, out_dtype=BF16, tm=N_MEM, tn=1024, tk=D, name="mem_kv")
        bias = _band_bias(rel[l])
        gw = jnp.zeros((128, B_KW), F32).at[:GATE_RANK].set(gate_w[l]).astype(BF16)
        gb, ng = gate_b[l][None, :], norm_g[l][None, :]
        r, lse, landed = _attn_fwd(hmat, lax.empty((s, D), BF16), bias, pipe.host("attn_fwd", l))
        pipe.landed("attn_fwd", l, landed)
        r, opre, states, landed = _gla_fwd(hmat, r, gw, gb, ng, pipe.host("gla_fwd", l))
        pipe.landed("gla_fwd", l, landed)
        r = _mem_fwd(hmat, r, mkv)
        xn, xh, rstd, y = _outproj_ln(hmat, r, w_out_f, xl, ln_g[l][None, :], ln_b[l][None, :])
        saved.append(dict(x=xl, h=hmat, mkv=mkv, bias=bias, gw=gw, gb=gb, ng=ng, r=r, lse=lse, opre=opre,
                          states=states, xh=xh, rstd=rstd, y=y, w_in_p=w_in_p, w_out_f=w_out_f))
        xl = xn
    loss, g = _loss_grad(xl, target)

    grads = [None] * depth
    for l in reversed(range(depth)):
        sv = saved[l]
        du, d_lng, d_lnb = _ln_bwd(g, sv["xh"], sv["rstd"], ln_g[l][None, :])
        d_wout, _ = _mm(sv["y"], du, ta=True, out_dtype=BF16, tm=1024, tn=1024, tk=1024, name="d_w_out")
        dy, _ = _mm(du, sv["w_out_f"], tb=True, out_dtype=BF16, tm=1024, tn=1024, tk=D, name="d_y")
        dh, dbias, landed = _attn_bwd(sv["h"], sv["r"], dy, sv["lse"], sv["bias"], lax.empty((s, H_W), BF16),
                                      pipe.host("attn_bwd", l))
        pipe.landed("attn_bwd", l, landed)
        dh, d_gw, d_gb, d_ng, landed = _gla_bwd(sv["h"], dy, sv["opre"], sv["states"], sv["gw"], sv["gb"], sv["ng"], dh,
                                                pipe.host("gla_bwd", l, d_wout))
        pipe.landed("gla_bwd", l, landed)
        dh, d_mkv = _mem_bwd(sv["h"], sv["r"], dy, sv["mkv"], dh)
        d_wkv, _ = _mm(mem, d_mkv, ta=True, out_dtype=BF16, tm=1024, tn=1024, tk=N_MEM, name="d_w_kv")
        d_win, landed = _mm(sv["x"], dh, ta=True, out_dtype=BF16, tm=1024, tn=1792, tk=1024, name="d_w_in",
                            comm=pipe.host("d_w_in", l, d_wkv))
        pipe.landed("d_w_in", l, landed)
        g, landed = _mm(dh, sv["w_in_p"], tb=True, out_dtype=F32, tm=1024, tn=1024, tk=1792, name="d_x",
                        adds=((du, ALPHA),), comm=pipe.host("d_x", l, d_win))
        pipe.landed("d_x", l, landed)
        grads[l] = dict(rel=_bias_grad(dbias), gate_w=d_gw[:GATE_RANK], gate_b=d_gb[0], norm_g=d_ng[0],
                        ln_g=d_lng[0], ln_b=d_lnb[0])
    return loss, g, grads


def _adamw(parts, w, m, v, rows_per_step, name):
    depth, rows, cols = w.shape
    n = parts[0].shape[0]
    tr = min(rows_per_step, rows)
    assert rows % tr == 0 and len(parts) == depth

    def body(*refs):
        p_refs = refs[:depth]
        w_ref, m_ref, v_ref, g_ref, d_ref, nm_ref, nv_ref = refs[depth:]
        for l in range(depth):
            @pl.when(pl.program_id(0) == l)
            def _(p_ref=p_refs[l]):
                g = p_ref[0].astype(F32)
                for j in range(1, n):
                    g = g + p_ref[j].astype(F32)
                nm = ADAM_B1 * m_ref[...] + (1.0 - ADAM_B1) * g
                nv = ADAM_B2 * v_ref[...] + (1.0 - ADAM_B2) * (g * g)
                m_hat = nm / (1.0 - ADAM_B1 ** ADAM_STEP)
                v_hat = nv / (1.0 - ADAM_B2 ** ADAM_STEP)
                g_ref[...] = g
                nm_ref[...] = nm
                nv_ref[...] = nv
                d_ref[...] = -ADAM_LR * (m_hat / (jnp.sqrt(v_hat) + ADAM_EPS) + ADAM_WD * w_ref[...])

    def part_spec(l):
        return pl.BlockSpec((n, tr, cols), lambda ll, i: (0, jnp.where(ll == l, i, 0), 0))

    blk = pl.BlockSpec((None, tr, cols), lambda ll, i: (ll, i, 0))
    shape = jax.ShapeDtypeStruct((depth, rows, cols), F32)
    return pl.pallas_call(
        body, name=name,
        out_shape=(shape, shape, shape, shape),
        grid=(depth, rows // tr),
        in_specs=[part_spec(l) for l in range(depth)] + [blk, blk, blk],
        out_specs=(blk, blk, blk, blk),
        compiler_params=_params(("arbitrary", "arbitrary")),
    )(*parts, w, m, v)


SMALL = (("rel", A_HEADS * (2 * MAX_REL + 1)), ("gate_w", GATE_RANK * B_KW), ("gate_b", B_KW), ("norm_g", B_DV),
         ("ln_g", D), ("ln_b", D))


def _pack_small(parts, depth):
    rows = []
    for name, size in SMALL:
        flat = parts[name].reshape(depth * size).astype(F32)
        rows.append(jnp.pad(flat, (0, -(depth * size) % 128)).reshape(-1, 128))
    packed = jnp.concatenate(rows, axis=0)
    return jnp.pad(packed, ((0, -packed.shape[0] % 8), (0, 0)))


def _unpack_small(packed, depth, shapes):
    out, row = {}, 0
    for name, size in SMALL:
        nrow = -(-(depth * size) // 128)
        out[name] = packed[row:row + nrow].reshape(-1)[:depth * size].reshape(shapes[name])
        row += nrow
    return out


def kernel(x, mem, w_in, a_rel_bias, b_gate_w, b_gate_b, b_norm_g, w_mem_kv, w_out, ln_g, ln_b, loss_target, m_w_in, m_a_rel_bias, m_b_gate_w, m_b_gate_b, m_b_norm_g, m_w_mem_kv, m_w_out, m_ln_g, m_ln_b, v_w_in, v_a_rel_bias, v_b_gate_w, v_b_gate_b, v_b_norm_g, v_w_mem_kv, v_w_out, v_ln_g, v_ln_b):
    depth = w_in.shape[0]
    sh_in = w_in.shape[2]
    sh_gw = b_gate_w.shape[2]
    me = 4 * lax.axis_index("x") + 2 * lax.axis_index("y") + lax.axis_index("c")

    pipe = _Fsdp(w_in.astype(BF16), w_out.astype(BF16), w_mem_kv.astype(BF16), [b_gate_w])
    gate_w_full = jnp.transpose(pipe.extra[0], (1, 2, 0, 3)).reshape(depth, GATE_RANK, N_DEV * sh_gw)
    loss_dev, dx, grads = _local_step(x[0], mem[0], loss_target[0], pipe,
                                      a_rel_bias, gate_w_full, b_gate_b, b_norm_g, ln_g, ln_b)
    loss = lax.psum(loss_dev[0, 0], ("x", "y", "c"))

    recv = lambda n: [pipe.recv[l][n] for l in range(depth)]
    big = {"w_in": _adamw(recv("w_in"), w_in, m_w_in, v_w_in, 128, "adamw_w_in"),
           "w_out": _adamw(recv("w_out"), w_out, m_w_out, v_w_out, 64, "adamw_w_out"),
           "w_kv": _adamw(recv("w_kv"), w_mem_kv, m_w_mem_kv, v_w_mem_kv, 128, "adamw_w_kv")}

    shapes = {"rel": a_rel_bias.shape, "gate_w": (depth, GATE_RANK, N_DEV * sh_gw), "gate_b": b_gate_b.shape,
              "norm_g": b_norm_g.shape, "ln_g": ln_g.shape, "ln_b": ln_b.shape}
    part = _pack_small({n: jnp.stack([grads[l][n] for l in range(depth)]) for n, _ in SMALL}, depth)
    (all_parts,) = _exchange([part], False, "gather_small")
    zeros_gw = jnp.zeros(shapes["gate_w"], F32)
    w_s = _pack_small(dict(rel=a_rel_bias, gate_w=zeros_gw, gate_b=b_gate_b, norm_g=b_norm_g, ln_g=ln_g, ln_b=ln_b), depth)
    m_s = _pack_small(dict(rel=m_a_rel_bias, gate_w=zeros_gw, gate_b=m_b_gate_b, norm_g=m_b_norm_g, ln_g=m_ln_g, ln_b=m_ln_b), depth)
    v_s = _pack_small(dict(rel=v_a_rel_bias, gate_w=zeros_gw, gate_b=v_b_gate_b, norm_g=v_b_norm_g, ln_g=v_ln_g, ln_b=v_ln_b), depth)
    small = [_unpack_small(t[0], depth, shapes)
             for t in _adamw([all_parts], w_s[None], m_s[None], v_s[None], all_parts.shape[1], "adamw_small")]
    gw_grad = lax.dynamic_slice_in_dim(small[0]["gate_w"], me * sh_gw, sh_gw, axis=2).reshape(1, depth * GATE_RANK, sh_gw)
    flat = lambda t: t.reshape(1, depth * GATE_RANK, sh_gw)
    gw_res = [t.reshape(depth, GATE_RANK, sh_gw)
              for t in _adamw([gw_grad], flat(b_gate_w), flat(m_b_gate_w), flat(v_b_gate_w), depth * GATE_RANK, "adamw_gate_w")]

    def leaves(t):
        return (big["w_in"][t], small[t]["rel"], gw_res[t], small[t]["gate_b"], small[t]["norm_g"],
                big["w_kv"][t], big["w_out"][t], small[t]["ln_g"], small[t]["ln_b"])

    return (loss, dx[None]) + leaves(0) + leaves(1) + leaves(2) + leaves(3)
```

```python
import functools
import math

import numpy as np
import jax
import jax.numpy as jnp
from jax import lax
from jax.experimental import pallas as pl
from jax.experimental.pallas import tpu as pltpu

F32 = jnp.float32
BF16 = jnp.bfloat16

N_DEV = 8
D = 2048
CH = 64
LEFT = 8
MAX_REL = 128
N_MEM = 256
A_HEADS, A_DH, A_W = 8, 128, 1024
B_HEADS, B_DK, B_DV, B_KW, B_W = 4, 64, 128, 256, 512
GATE_RANK, GATE_TAU = 16, 16.0
M_HEADS, M_DH, M_W = 4, 128, 512
IN_W = 6672
NAT_SPLIT = 5648
H_W = 7168
H_PAD = H_W - IN_W
A_Q, A_K, A_V, A_Z = 0, 1024, 2048, 3072
B_BASE = 4096
M_BASE = 6144
ALPHA = (2.0 * 4) ** 0.25
LN_EPS = 1e-5
RMS_EPS = 1e-6
NEG = -1e30
QB = 256
KB = 3 * QB
ADAM_LR, ADAM_B1, ADAM_B2, ADAM_EPS, ADAM_WD, ADAM_STEP = 0.001, 0.9, 0.999, 1e-08, 0.01, 10
VMEM_MB = 1024 * 1024


def _params(sem, vmem_mb=48):
    return pltpu.CompilerParams(dimension_semantics=sem, vmem_limit_bytes=vmem_mb * VMEM_MB)


def _sigmoid(x):
    return 1.0 / (1.0 + jnp.exp(-x))


def _dot(a, b, ca, cb, precision=None):
    return lax.dot_general(a, b, (((ca,), (cb,)), ((), ())), preferred_element_type=F32, precision=precision)


MESH = pl.DeviceIdType.MESH


class _Comm:
    def __init__(self, xs, scatter, relations=tuple(range(N_DEV)), into=None):
        self.xs, self.scatter, self.n = list(xs), scatter, len(xs)
        self.relations, self.into = tuple(relations), list(into or [])
        self.out_shape = [jax.ShapeDtypeStruct((N_DEV,) + (x.shape[1:] if scatter else x.shape), x.dtype) for x in xs]
        self.specs = [pl.BlockSpec(memory_space=pltpu.HBM)] * self.n
        self.scratch = [pltpu.SemaphoreType.DMA((self.n, N_DEV)), pltpu.SemaphoreType.DMA((self.n, N_DEV)),
                        pltpu.SemaphoreType.DMA((self.n,))]

    def _copies(self, x_refs, o_refs, sems):
        send_sems, recv_sems, local_sems = sems
        mx, my, mc = lax.axis_index("x"), lax.axis_index("y"), lax.axis_index("c")
        me = 4 * mx + 2 * my + mc
        own, sends, arrivals = [], [], []
        for k in self.relations:
            if k == 0:
                own = [pltpu.make_async_copy(x_refs[a].at[me] if self.scatter else x_refs[a], o_refs[a].at[me],
                                             local_sems.at[a]) for a in range(self.n)]
                continue
            px = 1 - mx if k & 4 else mx
            py = 1 - my if k & 2 else my
            pc = 1 - mc if k & 1 else mc
            idx = 4 * px + 2 * py + pc
            for a in range(self.n):
                src = x_refs[a].at[idx] if self.scatter else x_refs[a]
                for dst, group in ((o_refs[a].at[me], sends), (o_refs[a].at[idx], arrivals)):
                    group.append(pltpu.make_async_remote_copy(
                        src_ref=src, dst_ref=dst, send_sem=send_sems.at[a, k], recv_sem=recv_sems.at[a, k],
                        device_id=(px, py, pc), device_id_type=MESH))
        return own, sends, arrivals

    def start(self, x_refs, o_refs, sems):
        own, sends, _ = self._copies(x_refs, o_refs, sems)
        for cp in own + sends:
            cp.start()

    def finish(self, x_refs, o_refs, sems):
        own, sends, arrivals = self._copies(x_refs, o_refs, sems)
        for cp in sends:
            cp.wait_send()
        for cp in arrivals:
            cp.wait_recv()
        for cp in own:
            cp.wait()


class _Gather:
    into = ()

    def __init__(self, xs):
        self.xs, self.n = list(xs), len(xs)
        self.out_shape = [jax.ShapeDtypeStruct((N_DEV,) + x.shape, x.dtype) for x in xs]
        self.specs = [pl.BlockSpec(memory_space=pltpu.HBM)] * self.n
        self.scratch = [pltpu.SemaphoreType.DMA((self.n, N_DEV - 1)), pltpu.SemaphoreType.DMA((self.n, N_DEV - 1)),
                        pltpu.SemaphoreType.DMA((self.n,))]

    def _copies(self, x_refs, o_refs, sems):
        send_sems, recv_sems, local_sems = sems
        mx, my, mc = lax.axis_index("x"), lax.axis_index("y"), lax.axis_index("c")
        idx = lambda px, py, pc: 4 * px + 2 * py + pc
        me, sibling = (mx, my, mc), (mx, my, 1 - mc)
        chips = [(mx, 1 - my), (1 - mx, my), (1 - mx, 1 - my)]

        def copy(a, k, src, slot, to):
            return pltpu.make_async_remote_copy(
                src_ref=src, dst_ref=o_refs[a].at[idx(*slot)], send_sem=send_sems.at[a, k], recv_sem=recv_sems.at[a, k],
                device_id=to, device_id_type=MESH)

        c = dict(own=[], first=[], passed=[], ici_in=[], late_in=[])
        for a in range(self.n):
            x = x_refs[a]
            c["own"].append(pltpu.make_async_copy(x, o_refs[a].at[idx(*me)], local_sems.at[a]))
            c["first"].append(copy(a, 0, x, me, sibling))
            c["late_in"].append(copy(a, 0, x, sibling, sibling))
            for j, chip in enumerate(chips):
                c["first"].append(copy(a, 1 + j, x, me, (*chip, mc)))
                c["ici_in"].append(copy(a, 1 + j, x, (*chip, mc), (*chip, mc)))
                c["passed"].append(copy(a, 4 + j, o_refs[a].at[idx(*chip, mc)], (*chip, mc), sibling))
                c["late_in"].append(copy(a, 4 + j, x, (*chip, 1 - mc), sibling))
        return c

    def start(self, x_refs, o_refs, sems):
        c = self._copies(x_refs, o_refs, sems)
        for cp in c["own"] + c["first"]:
            cp.start()

    def finish(self, x_refs, o_refs, sems):
        c = self._copies(x_refs, o_refs, sems)
        for arrived, onward in zip(c["ici_in"], c["passed"]):
            arrived.wait_recv()
            onward.start()
        for cp in c["first"] + c["passed"]:
            cp.wait_send()
        for cp in c["late_in"]:
            cp.wait_recv()
        for cp in c["own"]:
            cp.wait()


def _exchange(xs, scatter, name):
    comm = _Comm(xs, scatter) if scatter else _Gather(xs)

    def body(*refs):
        x_refs, o_refs, sems = refs[:comm.n], refs[comm.n:2 * comm.n], refs[2 * comm.n:]
        comm.start(x_refs, o_refs, sems)
        comm.finish(x_refs, o_refs, sems)

    return pl.pallas_call(body, name=name, out_shape=tuple(comm.out_shape), in_specs=comm.specs,
                          out_specs=tuple(comm.specs), scratch_shapes=comm.scratch)(*comm.xs)


def _hosted(body, comm, *, name, grid, in_specs, out_specs, out_shape, scratch_shapes, compiler_params, inputs,
            input_output_aliases=None):
    out_specs, out_shape = tuple(out_specs), tuple(out_shape)
    aliases = input_output_aliases or {}
    if comm is None:
        outs = pl.pallas_call(body, name=name, grid=grid, in_specs=list(in_specs), out_specs=out_specs, out_shape=out_shape,
                              scratch_shapes=list(scratch_shapes), compiler_params=compiler_params,
                              input_output_aliases=aliases)(*inputs)
        return tuple(outs), ()
    ni, no, ns, nc = len(in_specs), len(out_specs), len(scratch_shapes), comm.n
    n_into = len(comm.into)
    aliases = {**aliases, **{ni + nc + a: no + a for a in range(n_into)}}

    def wrapped(*refs):
        ins, c_in = refs[:ni], refs[ni:ni + nc]
        refs = refs[ni + nc + n_into:]
        outs, c_out = refs[:no], refs[no:no + nc]
        scr, sems = refs[no + nc:no + nc + ns], refs[no + nc + ns:]
        first = functools.reduce(jnp.logical_and, [pl.program_id(d) == 0 for d in range(len(grid))])
        last = functools.reduce(jnp.logical_and, [pl.program_id(d) == grid[d] - 1 for d in range(len(grid))])

        @pl.when(first)
        def _():
            comm.start(c_in, c_out, sems)

        body(*ins, *outs, *scr)

        @pl.when(last)
        def _():
            comm.finish(c_in, c_out, sems)

    params = pltpu.CompilerParams(dimension_semantics=("arbitrary",) * len(grid),
                                  vmem_limit_bytes=compiler_params.vmem_limit_bytes)
    outs = pl.pallas_call(wrapped, name=name, grid=grid, in_specs=list(in_specs) + comm.specs + comm.specs[:n_into],
                          out_specs=out_specs + tuple(comm.specs), out_shape=out_shape + tuple(comm.out_shape),
                          scratch_shapes=list(scratch_shapes) + comm.scratch, compiler_params=params,
                          input_output_aliases=aliases)(*inputs, *comm.xs, *comm.into)
    return tuple(outs[:no]), tuple(outs[no:])


def _mm(a, b, *, ta=False, tb=False, out_dtype, tm, tn, tk, name, adds=(), vmem_mb=48, comm=None):
    m = a.shape[1] if ta else a.shape[0]
    k = a.shape[0] if ta else a.shape[1]
    n = b.shape[0] if tb else b.shape[1]
    assert k == (b.shape[1] if tb else b.shape[0])
    tm, tn, tk = min(tm, m), min(tn, n), min(tk, k)
    assert m % tm == 0 and n % tn == 0 and k % tk == 0, (name, m, n, k)
    nk = k // tk
    n_add = len(adds)
    scales = [s for _, s in adds]

    def body(a_ref, b_ref, *rest):
        add_refs, o_ref = rest[:n_add], rest[n_add]
        kk = pl.program_id(2)

        def product():
            return _dot(a_ref[...].astype(BF16), b_ref[...].astype(BF16), 0 if ta else 1, 1 if tb else 0)

        def finish(r):
            for ref, s in zip(add_refs, scales):
                r = r + s * ref[...].astype(F32)
            o_ref[...] = r.astype(out_dtype)

        if nk == 1:
            finish(product())
            return
        acc_ref = rest[n_add + 1]

        @pl.when(kk == 0)
        def _():
            acc_ref[...] = product()

        if nk > 2:
            @pl.when((kk > 0) & (kk < nk - 1))
            def _():
                acc_ref[...] += product()

        @pl.when(kk == nk - 1)
        def _():
            finish(acc_ref[...] + product())

    a_spec = pl.BlockSpec((tk, tm), lambda i, j, kk: (kk, i)) if ta else pl.BlockSpec((tm, tk), lambda i, j, kk: (i, kk))
    b_spec = pl.BlockSpec((tn, tk), lambda i, j, kk: (j, kk)) if tb else pl.BlockSpec((tk, tn), lambda i, j, kk: (kk, j))
    add_specs = [pl.BlockSpec((tm, tn), lambda i, j, kk: (i, j)) for _ in adds]
    (out,), c_out = _hosted(
        body, comm, name=name,
        out_shape=[jax.ShapeDtypeStruct((m, n), out_dtype)],
        grid=(m // tm, n // tn, nk),
        in_specs=[a_spec, b_spec] + add_specs,
        out_specs=[pl.BlockSpec((tm, tn), lambda i, j, kk: (i, j))],
        scratch_shapes=[pltpu.VMEM((tm, tn), F32)] if nk > 1 else [],
        compiler_params=_params(("parallel", "parallel", "arbitrary"), vmem_mb),
        inputs=(a, b, *[x for x, _ in adds]))
    return out, c_out


NKB = KB // QB
LEAD = NKB - 1


def _band_bias(table):
    i = np.arange(QB)[:, None]
    j = np.arange(KB)[None, :]
    qc = i // CH + LEAD * QB // CH
    kc = j // CH
    valid = (kc <= qc) & (kc >= qc - LEFT)
    n = QB + KB
    c = np.arange(n)
    onehot = np.zeros((2 * MAX_REL + 1, n), np.float32)
    onehot[np.clip(LEAD * QB - (c - (QB - 1)), -MAX_REL, MAX_REL) + MAX_REL, c] = 1.0
    row = jnp.dot(table.astype(F32), jnp.asarray(onehot), precision=lax.Precision.HIGHEST)
    flow = jnp.tile(row, (1, QB))[:, :QB * (n - 1)].reshape(table.shape[0], QB, n - 1)
    return jnp.where(valid[None], flow[:, :, QB - 1:], NEG)


def _bias_grad(dbias):
    h, n = dbias.shape[0], KB + 1
    flat = jnp.pad(dbias.reshape(h, QB * KB), ((0, 0), (0, -(QB * KB) % n)))
    diag = flat.reshape(h, -1, n).sum(axis=1)
    c = np.arange(n)
    jm = np.where(c < LEAD * QB + CH, c, c - n)
    didx = np.clip(LEAD * QB - jm, -MAX_REL, MAX_REL) + MAX_REL
    onehot = np.zeros((n, 2 * MAX_REL + 1), np.float32)
    onehot[c, didx] = 1.0
    return jnp.dot(diag, jnp.asarray(onehot), precision=lax.Precision.HIGHEST)


def _attn_scores(q, k_refs, cs, bias_h, m, masked, scale):
    parts = []
    for t in range(NKB):
        sc = _dot(q, k_refs[t][:, cs], 1, 1) * scale
        parts.append(jnp.where(m + t - LEAD >= 0, sc, NEG) if masked else sc)
    return jnp.concatenate(parts, axis=1) + bias_h


def _attn_fwd(h, r_in, bias, comm=None):
    s = h.shape[0]
    nq = s // QB
    scale = A_DH ** -0.5

    def body(q_ref, *rest):
        k_refs, v_refs = rest[:NKB], rest[NKB:2 * NKB]
        bias_hbm, r_any, o_ref, lse_ref, bias_ref = rest[2 * NKB:]
        del r_any
        m = pl.program_id(0)

        @pl.when(m == 0)
        def _():
            pltpu.sync_copy(bias_hbm, bias_ref)

        def scores(hd, masked):
            cs = slice(hd * A_DH, (hd + 1) * A_DH)
            return _attn_scores(q_ref[:, cs], k_refs, cs, bias_ref[hd], m, masked, scale)

        def step(masked):
            sc_next = scores(0, masked)
            for hd in range(A_HEADS):
                cs = slice(hd * A_DH, (hd + 1) * A_DH)
                sc = sc_next
                if hd + 1 < A_HEADS:
                    sc_next = scores(hd + 1, masked)
                mx = jnp.max(sc, axis=1, keepdims=True)
                p = jnp.exp(sc - mx)
                l = jnp.sum(p, axis=1, keepdims=True)
                pb = p.astype(BF16)
                o = _dot(pb[:, :QB], v_refs[0][:, cs], 1, 0)
                for t in range(1, NKB):
                    o += _dot(pb[:, t * QB:(t + 1) * QB], v_refs[t][:, cs], 1, 0)
                o_ref[:, cs] = (o / l).astype(BF16)
                lse_ref[:, hd:hd + 1] = mx + jnp.log(l)

        pl.when(m < LEAD)(functools.partial(step, True))
        pl.when(m >= LEAD)(functools.partial(step, False))

    def kv_spec(col, t):
        return pl.BlockSpec((QB, A_W), lambda m: (jnp.maximum(m + t - LEAD, 0), col))

    (r, lse), c_out = _hosted(
        body, comm, name="attn_fwd",
        out_shape=(jax.ShapeDtypeStruct(r_in.shape, BF16), jax.ShapeDtypeStruct((s, A_HEADS), F32)),
        grid=(nq,),
        in_specs=[pl.BlockSpec((QB, A_W), lambda m: (m, 0))]
        + [kv_spec(1, t) for t in range(NKB)] + [kv_spec(2, t) for t in range(NKB)]
        + [pl.BlockSpec(memory_space=pl.ANY), pl.BlockSpec(memory_space=pl.ANY)],
        out_specs=(pl.BlockSpec((QB, A_W), lambda m: (m, 0)), pl.BlockSpec((QB, A_HEADS), lambda m: (m, 0))),
        scratch_shapes=[pltpu.VMEM((A_HEADS, QB, KB), F32)],
        input_output_aliases={2 * NKB + 2: 0},
        compiler_params=_params(("arbitrary",)),
        inputs=(h,) * (2 * NKB + 1) + (bias, r_in))
    return r, lse, c_out


def _attn_bwd(h, r, dy, lse, bias, dh_in, comm=None):
    s = h.shape[0]
    nq = s // QB
    scale = A_DH ** -0.5

    def body(q_ref, *rest):
        k_refs, v_refs = rest[:NKB], rest[NKB:2 * NKB]
        (z_ref, r_ref, dy_ref, lse_ref, zl_ref, rl_ref, dyl_ref, bias_hbm, dh_any,
         dh_ref, dbias_hbm, dk_acc, dv_acc, dq_ring, bias_ref, dbias_ref) = rest[2 * NKB:]
        del dh_any
        m = pl.program_id(0)

        def slot(b):
            return pl.multiple_of(lax.rem(b + NKB, NKB) * QB, QB)

        @pl.when(m == 0)
        def _():
            dk_acc[...] = jnp.zeros_like(dk_acc)
            dv_acc[...] = jnp.zeros_like(dv_acc)
            dq_ring[...] = jnp.zeros_like(dq_ring)
            dbias_ref[...] = jnp.zeros_like(dbias_ref)
            pltpu.sync_copy(bias_hbm, bias_ref)

        def step(masked):
            z = z_ref[...].astype(F32)
            do_all = dy_ref[...].astype(F32) * (z * _sigmoid(z))
            o_all = r_ref[...].astype(F32)

            def products(hd):
                cs = slice(hd * A_DH, (hd + 1) * A_DH)
                dob = do_all[:, cs].astype(BF16)
                sc = _attn_scores(q_ref[:, cs], k_refs, cs, bias_ref[hd], m, masked, scale)
                return sc, jnp.concatenate([_dot(dob, v_refs[t][:, cs], 1, 1) for t in range(NKB)], axis=1)

            ahead = products(0)
            for hd in range(A_HEADS):
                cs = slice(hd * A_DH, (hd + 1) * A_DH)
                q = q_ref[:, cs]
                do = do_all[:, cs]
                dob = do.astype(BF16)
                delta = jnp.sum(do * o_all[:, cs], axis=1, keepdims=True)
                sc, dp = ahead
                if hd + 1 < A_HEADS:
                    ahead = products(hd + 1)
                p = jnp.exp(sc - lse_ref[:, hd:hd + 1])
                ds = p * (dp - delta)
                dbias_ref[hd] += ds
                pb, dsb = p.astype(BF16), ds.astype(BF16)
                dq = jnp.zeros((QB, A_DH), F32)
                for t in range(NKB):
                    ts = slice(t * QB, (t + 1) * QB)
                    rows = pl.ds(slot(m - LEAD + t), QB)
                    dq += _dot(dsb[:, ts], k_refs[t][:, cs], 1, 0)
                    dk_acc[rows, cs] += _dot(dsb[:, ts], q, 0, 0) * scale
                    dv_acc[rows, cs] += _dot(pb[:, ts], dob, 0, 0)
                dq_ring[pl.ds(slot(m), QB), cs] = dq * scale

        pl.when(m < LEAD)(functools.partial(step, True))
        pl.when((m >= LEAD) & (m < nq))(functools.partial(step, False))

        done = pl.ds(slot(m - LEAD), QB)
        zl = zl_ref[...].astype(F32)
        sg = _sigmoid(zl)
        dz = dyl_ref[...].astype(F32) * rl_ref[...].astype(F32) * (sg * (1.0 + zl * (1.0 - sg)))
        dh_ref[:, A_Q:A_Q + A_W] = dq_ring[done, :].astype(BF16)
        dh_ref[:, A_K:A_K + A_W] = dk_acc[done, :].astype(BF16)
        dh_ref[:, A_V:A_V + A_W] = dv_acc[done, :].astype(BF16)
        dh_ref[:, A_Z:A_Z + A_W] = dz.astype(BF16)
        dk_acc[done, :] = jnp.zeros((QB, A_W), F32)
        dv_acc[done, :] = jnp.zeros((QB, A_W), F32)

        @pl.when(m == nq + LEAD - 1)
        def _():
            pltpu.sync_copy(dbias_ref, dbias_hbm)

    last = nq - 1

    def cur(col):
        return pl.BlockSpec((QB, A_W), lambda m: (jnp.minimum(m, last), col))

    def kv_spec(col, t):
        return pl.BlockSpec((QB, A_W), lambda m: (jnp.clip(m + t - LEAD, 0, last), col))

    def lag(col):
        return pl.BlockSpec((QB, A_W), lambda m: (jnp.clip(m - LEAD, 0, last), col))

    (dh, dbias), c_out = _hosted(
        body, comm, name="attn_bwd",
        out_shape=(jax.ShapeDtypeStruct(dh_in.shape, BF16), jax.ShapeDtypeStruct((A_HEADS, QB, KB), F32)),
        grid=(nq + LEAD,),
        in_specs=[cur(0)] + [kv_spec(1, t) for t in range(NKB)] + [kv_spec(2, t) for t in range(NKB)]
        + [cur(3), cur(0), cur(0), pl.BlockSpec((QB, A_HEADS), lambda m: (jnp.minimum(m, last), 0)),
           lag(3), lag(0), lag(0),
           pl.BlockSpec(memory_space=pl.ANY), pl.BlockSpec(memory_space=pl.ANY)],
        out_specs=(pl.BlockSpec((QB, 4 * A_W), lambda m: (jnp.clip(m - LEAD, 0, last), 0)),
                   pl.BlockSpec(memory_space=pl.ANY)),
        scratch_shapes=[pltpu.VMEM((KB, A_W), F32), pltpu.VMEM((KB, A_W), F32), pltpu.VMEM((KB, A_W), F32),
                        pltpu.VMEM((A_HEADS, QB, KB), F32), pltpu.VMEM((A_HEADS, QB, KB), F32)],
        input_output_aliases={2 * NKB + 9: 0},
        compiler_params=_params(("arbitrary",), 56),
        inputs=(h,) * (2 * NKB + 2) + (r, dy, lse, h, r, dy, bias, dh_in))
    return dh, dbias, c_out


GB = 256
N_PAIR = B_HEADS // 2


def _gla_gates(lr, gw_ref, gb_ref):
    logit = _dot(lr, gw_ref[...], 1, 0) + gb_ref[...]
    lg = (jnp.minimum(logit, 0.0) - jnp.log(1.0 + jnp.exp(-jnp.abs(logit)))) / GATE_TAU
    row = lax.broadcasted_iota(jnp.int32, (GB, GB), 0)
    col = lax.broadcasted_iota(jnp.int32, (GB, GB), 1)
    tri = jnp.where((row // CH == col // CH) & (col <= row), 1.0, 0.0).astype(F32)
    return logit, _dot(tri, lg, 1, 0, precision=lax.Precision.HIGHEST)


def _gla_factors(hb_ref, b_all, c):
    rs = slice(c * CH, (c + 1) * CH)
    q = hb_ref[rs, 0:B_KW].astype(F32) * (B_DK ** -0.5)
    k = hb_ref[rs, B_KW:2 * B_KW].astype(F32)
    b = b_all[rs]
    bm, bl = b[CH // 2:CH // 2 + 1, :], b[CH - 1:CH, :]
    e1, e2, eb, ek = jnp.exp(b - bm), jnp.exp(bm - b), jnp.exp(b), jnp.exp(bl - b)
    el = jnp.exp(bl)
    return dict(ql=q * e1, kl=k * e2, qu=q * e2, ku=k * e1, qt=q * eb, kh=k * ek, e1=e1, e2=e2, eb=eb, ek=ek, el=el)


class _GlaPairTools:
    def __init__(self, hb_ref, fs):
        self.hb_ref, self.fs = hb_ref, fs
        row = lax.broadcasted_iota(jnp.int32, (2 * CH, 2 * CH), 0)
        col = lax.broadcasted_iota(jnp.int32, (2 * CH, 2 * CH), 1)
        self.same = (row // CH) == (col // CH)
        self.lower = self.same & ((row % CH) >= (col % CH))
        self.upper = self.same & ((row % CH) < (col % CH))
        self.lower_t = self.same & ((col % CH) >= (row % CH))
        self.upper_t = self.same & ((col % CH) < (row % CH))

    def lanes(self, c, p, name):
        return self.fs[c][name][:, p * 128:(p + 1) * 128]

    def heads(self, c, p, name):
        x = self.lanes(c, p, name)
        return jnp.where(self.same, jnp.concatenate([x, x], axis=0), 0.0).astype(BF16)

    def twice(self, c, p, name):
        x = self.lanes(c, p, name).astype(BF16)
        return jnp.concatenate([x, x], axis=0)

    def vals(self, c, p):
        rows = slice(c * CH, (c + 1) * CH)
        return jnp.concatenate([self.hb_ref[rows, 512 + (2 * p + sh) * B_DV:512 + (2 * p + sh + 1) * B_DV]
                                for sh in range(2)], axis=0)

    def intra(self, c, p):
        lo = _dot(self.heads(c, p, "ql"), self.twice(c, p, "kl"), 1, 1)
        up = _dot(self.heads(c, p, "qu"), self.twice(c, p, "ku"), 1, 1)
        return jnp.where(self.lower, lo, jnp.where(self.upper, up, 0.0)).astype(BF16)

    def fold(self, x):
        x = jnp.where(self.same, x, 0.0)
        return x[:CH] + x[CH:]


def _gla_fwd(h, r_in, gw, gb, ng, comm=None):
    s = h.shape[0]
    nb = s // GB
    cpb = GB // CH

    def body(hb_ref, gw_ref, gb_ref, ng_ref, r_any, o_ref, opre_ref, st_ref, state):
        del r_any

        @pl.when(pl.program_id(0) == 0)
        def _():
            state[...] = jnp.zeros_like(state)

        _, b_all = _gla_gates(hb_ref[:, 1536:1664], gw_ref, gb_ref)
        fs = [_gla_factors(hb_ref, b_all, c) for c in range(cpb)]
        pairs = [(c, p) for c in range(cpb) for p in range(N_PAIR)]
        tools = _GlaPairTools(hb_ref, fs)
        a = {k: tools.intra(*k) for k in pairs}
        o_intra = {k: _dot(a[k], tools.vals(*k), 1, 0) for k in pairs}
        upd = {k: _dot(tools.vals(*k), tools.heads(*k, "kh"), 0, 0) for k in pairs}
        st = [state[p] for p in range(N_PAIR)]
        entering = {}
        for c, p in pairs:
            entering[c, p] = st[p]
            st_ref[c, p] = st[p]
            st[p] = st[p] * fs[c]["el"][:, p * 128:(p + 1) * 128] + upd[c, p]
        for p in range(N_PAIR):
            state[p] = st[p]
        for c, p in pairs:
            o2 = o_intra[c, p] + _dot(tools.heads(c, p, "qt"), entering[c, p].astype(BF16), 1, 1)
            for sh in range(2):
                o = o2[sh * CH:(sh + 1) * CH]
                rs, hs = slice(c * CH, (c + 1) * CH), slice((2 * p + sh) * B_DV, (2 * p + sh + 1) * B_DV)
                opre_ref[rs, hs] = o
                rinv = lax.rsqrt(jnp.mean(o * o, axis=1, keepdims=True) + RMS_EPS)
                o_ref[rs, hs] = (o * rinv * ng_ref[...]).astype(BF16)

    (r, opre, states), c_out = _hosted(
        body, comm, name="gla_fwd",
        out_shape=(jax.ShapeDtypeStruct(r_in.shape, BF16), jax.ShapeDtypeStruct((s, B_W), F32),
                   jax.ShapeDtypeStruct((s // CH, N_PAIR, 128, 128), F32)),
        grid=(nb,),
        in_specs=[pl.BlockSpec((GB, 2048), lambda i: (i, B_BASE // 2048)),
                  pl.BlockSpec((128, B_KW), lambda i: (0, 0)), pl.BlockSpec((1, B_KW), lambda i: (0, 0)),
                  pl.BlockSpec((1, B_DV), lambda i: (0, 0)), pl.BlockSpec(memory_space=pl.ANY)],
        out_specs=(pl.BlockSpec((GB, B_W), lambda i: (i, 1024 // B_W)), pl.BlockSpec((GB, B_W), lambda i: (i, 0)),
                   pl.BlockSpec((cpb, N_PAIR, 128, 128), lambda i: (i, 0, 0, 0))),
        scratch_shapes=[pltpu.VMEM((N_PAIR, 128, 128), F32)],
        input_output_aliases={4: 0},
        compiler_params=_params(("arbitrary",)),
        inputs=(h, gw, gb, ng, r_in))
    return r, opre, states, c_out


def _gla_bwd(h, dy, opre, states, gw, gb, ng, dh_in, comm=None):
    s = h.shape[0]
    nb = s // GB
    cpb = GB // CH

    def body(hb_ref, dy_ref, opre_ref, st_ref, gw_ref, gb_ref, ng_ref, dh_any,
             dh_ref, dgw_ref, dgb_ref, dng_ref, dstate, db_scr, do_scr):
        del dh_any

        @pl.when(pl.program_id(0) == 0)
        def _():
            dstate[...] = jnp.zeros_like(dstate)
            dgw_ref[...] = jnp.zeros_like(dgw_ref)
            dgb_ref[...] = jnp.zeros_like(dgb_ref)
            dng_ref[...] = jnp.zeros_like(dng_ref)

        lr = hb_ref[:, 1536:1664]
        logit, b_all = _gla_gates(lr, gw_ref, gb_ref)
        z = hb_ref[:, 1024:1536].astype(F32)
        sg = _sigmoid(z)
        dyb = dy_ref[...].astype(F32)
        dng = jnp.zeros((1, B_DV), F32)
        for hd in range(B_HEADS):
            hs = slice(hd * B_DV, (hd + 1) * B_DV)
            o = opre_ref[:, hs]
            rinv = lax.rsqrt(jnp.mean(o * o, axis=1, keepdims=True) + RMS_EPS)
            on = o * rinv
            dr = dyb[:, hs] * (z[:, hs] * sg[:, hs])
            dh_ref[:, 1024 + hd * B_DV:1024 + (hd + 1) * B_DV] = (
                dyb[:, hs] * (on * ng_ref[...]) * (sg[:, hs] * (1.0 + z[:, hs] * (1.0 - sg[:, hs])))).astype(BF16)
            dng += jnp.sum(dr * on, axis=0, keepdims=True)
            dn = dr * ng_ref[...]
            do_scr[:, hs] = rinv * (dn - on * jnp.mean(dn * on, axis=1, keepdims=True))
        dng_ref[...] += dng

        rowi = lax.broadcasted_iota(jnp.int32, (CH, 128), 0)
        fs = [_gla_factors(hb_ref, b_all, c) for c in range(cpb)]
        pairs = [(c, p) for c in reversed(range(cpb)) for p in range(N_PAIR)]
        tools = _GlaPairTools(hb_ref, fs)
        pair = tools.lanes

        def douts(c, p):
            rows = slice(c * CH, (c + 1) * CH)
            return jnp.concatenate([do_scr[rows, (2 * p + sh) * B_DV:(2 * p + sh + 1) * B_DV] for sh in range(2)],
                                   axis=0).astype(BF16)

        at, da, dat, dst_own, g_qt = {}, {}, {}, {}, {}
        for k in pairs:
            c, p = k
            lo_t = _dot(tools.twice(*k, "kl"), tools.heads(*k, "ql"), 1, 1)
            up_t = _dot(tools.twice(*k, "ku"), tools.heads(*k, "qu"), 1, 1)
            at[k] = jnp.where(tools.lower_t, lo_t, jnp.where(tools.upper_t, up_t, 0.0)).astype(BF16)
            da[k] = _dot(douts(*k), tools.vals(*k), 1, 1)
            dat[k] = _dot(tools.vals(*k), douts(*k), 1, 1)
            dst_own[k] = _dot(douts(*k), tools.heads(*k, "qt"), 0, 0)
            g_qt[k] = tools.fold(_dot(douts(*k), st_ref[c, p].astype(BF16), 1, 0))
        dv, g_ql, g_qu, g_kl, g_ku = {}, {}, {}, {}, {}
        for k in pairs:
            dv[k] = _dot(at[k], douts(*k), 1, 0)
            g_ql[k] = tools.fold(_dot(jnp.where(tools.lower, da[k], 0.0).astype(BF16), tools.twice(*k, "kl"), 1, 0))
            g_qu[k] = tools.fold(_dot(jnp.where(tools.upper, da[k], 0.0).astype(BF16), tools.twice(*k, "ku"), 1, 0))
            g_kl[k] = tools.fold(_dot(jnp.where(tools.lower_t, dat[k], 0.0).astype(BF16), tools.heads(*k, "ql"), 1, 0))
            g_ku[k] = tools.fold(_dot(jnp.where(tools.upper_t, dat[k], 0.0).astype(BF16), tools.heads(*k, "qu"), 1, 0))
        dst = [dstate[p] for p in range(N_PAIR)]
        leaving = {}
        for c, p in pairs:
            leaving[c, p] = dst[p]
            dst[p] = dst[p] * pair(c, p, "el") + dst_own[c, p]
        for p in range(N_PAIR):
            dstate[p] = dst[p]
        g_kh = {}
        for k in pairs:
            c, p = k
            dstb = leaving[k].astype(BF16)
            dv2 = dv[k] + _dot(tools.heads(*k, "kh"), dstb, 1, 1)
            for sh in range(2):
                hd = 2 * p + sh
                dh_ref[c * CH:(c + 1) * CH, 512 + hd * B_DV:512 + (hd + 1) * B_DV] = dv2[sh * CH:(sh + 1) * CH].astype(BF16)
            g_kh[k] = tools.fold(_dot(tools.vals(*k), dstb, 1, 0))
        for c in reversed(range(cpb)):
            rs = slice(c * CH, (c + 1) * CH)
            for p in range(N_PAIR):
                k = (c, p)
                dq = (g_ql[k] * pair(c, p, "e1") + g_qu[k] * pair(c, p, "e2") + g_qt[k] * pair(c, p, "eb")) * (B_DK ** -0.5)
                dk = g_kl[k] * pair(c, p, "e2") + g_ku[k] * pair(c, p, "e1") + g_kh[k] * pair(c, p, "ek")
                dkh_kh = g_kh[k] * pair(c, p, "kh")
                db = (g_ql[k] * pair(c, p, "ql") - g_qu[k] * pair(c, p, "qu") + g_qt[k] * pair(c, p, "qt")
                      - g_kl[k] * pair(c, p, "kl") + g_ku[k] * pair(c, p, "ku") - dkh_kh)
                db_last = (pair(c, p, "el") * jnp.sum(leaving[c, p] * st_ref[c, p], axis=0, keepdims=True)
                           + jnp.sum(dkh_kh, axis=0, keepdims=True))
                db = jnp.where(rowi == CH - 1, db + db_last, db)
                dh_ref[rs, p * 128:(p + 1) * 128] = dq.astype(BF16)
                dh_ref[rs, B_KW + p * 128:B_KW + (p + 1) * 128] = dk.astype(BF16)
                db_scr[rs, p * 128:(p + 1) * 128] = db

        row = lax.broadcasted_iota(jnp.int32, (GB, GB), 0)
        col = lax.broadcasted_iota(jnp.int32, (GB, GB), 1)
        trit = jnp.where((row // CH == col // CH) & (col >= row), 1.0, 0.0).astype(F32)
        dlg = _dot(trit, db_scr[...], 1, 0, precision=lax.Precision.HIGHEST)
        dlogit = dlg * (_sigmoid(-logit) / GATE_TAU)
        dlb = dlogit.astype(BF16)
        dgw_ref[...] += _dot(lr, dlb, 0, 0)
        dgb_ref[...] += jnp.sum(dlogit, axis=0, keepdims=True)
        dh_ref[:, 1536:1664] = _dot(dlb, gw_ref[...], 1, 1).astype(BF16)
        dh_ref[:, 1664:2048] = jnp.zeros((GB, 384), BF16)

    rev = lambda i: nb - 1 - i
    (dh, dgw, dgb, dng), c_out = _hosted(
        body, comm, name="gla_bwd",
        out_shape=(jax.ShapeDtypeStruct(dh_in.shape, BF16), jax.ShapeDtypeStruct((128, B_KW), F32),
                   jax.ShapeDtypeStruct((1, B_KW), F32), jax.ShapeDtypeStruct((1, B_DV), F32)),
        grid=(nb,),
        in_specs=[pl.BlockSpec((GB, 2048), lambda i: (rev(i), B_BASE // 2048)),
                  pl.BlockSpec((GB, B_W), lambda i: (rev(i), 1024 // B_W)),
                  pl.BlockSpec((GB, B_W), lambda i: (rev(i), 0)),
                  pl.BlockSpec((cpb, N_PAIR, 128, 128), lambda i: (rev(i), 0, 0, 0)),
                  pl.BlockSpec((128, B_KW), lambda i: (0, 0)), pl.BlockSpec((1, B_KW), lambda i: (0, 0)),
                  pl.BlockSpec((1, B_DV), lambda i: (0, 0)), pl.BlockSpec(memory_space=pl.ANY)],
        out_specs=(pl.BlockSpec((GB, 2048), lambda i: (rev(i), B_BASE // 2048)),
                   pl.BlockSpec((128, B_KW), lambda i: (0, 0)), pl.BlockSpec((1, B_KW), lambda i: (0, 0)),
                   pl.BlockSpec((1, B_DV), lambda i: (0, 0))),
        scratch_shapes=[pltpu.VMEM((N_PAIR, 128, 128), F32), pltpu.VMEM((GB, B_KW), F32), pltpu.VMEM((GB, B_W), F32)],
        input_output_aliases={7: 0},
        compiler_params=_params(("arbitrary",)),
        inputs=(h, dy, opre, states, gw, gb, ng, dh_in))
    return dh, dgw, dgb, dng, c_out


MB = 512


def _mem_probs(q, mk, scale):
    sc = _dot(q, mk, 1, 1) * scale
    p = jnp.exp(sc - jnp.max(sc, axis=1, keepdims=True))
    return p / jnp.sum(p, axis=1, keepdims=True)


def _mem_fwd(h, r_in, mkv):
    s = h.shape[0]
    scale = M_DH ** -0.5

    def body(q_ref, mkv_ref, r_any, o_ref):
        del r_any
        for hd in range(M_HEADS):
            cs = slice(hd * M_DH, (hd + 1) * M_DH)
            p = _mem_probs(q_ref[:, cs], mkv_ref[:, cs], scale)
            o_ref[:, cs] = _dot(p.astype(BF16), mkv_ref[:, M_W + hd * M_DH:M_W + (hd + 1) * M_DH], 1, 0).astype(BF16)

    return pl.pallas_call(
        body, name="mem_fwd",
        out_shape=jax.ShapeDtypeStruct(r_in.shape, BF16),
        grid=(s // MB,),
        in_specs=[pl.BlockSpec((MB, M_W), lambda i: (i, M_BASE // M_W)),
                  pl.BlockSpec((N_MEM, 2 * M_W), lambda i: (0, 0)), pl.BlockSpec(memory_space=pl.ANY)],
        out_specs=pl.BlockSpec((MB, M_W), lambda i: (i, 1536 // M_W)),
        input_output_aliases={2: 0},
        compiler_params=_params(("arbitrary",)),
    )(h, mkv, r_in)


def _mem_bwd(h, r, dy, mkv, dh_in):
    s = h.shape[0]
    scale = M_DH ** -0.5

    def body(q_ref, z_ref, r_ref, dy_ref, mkv_ref, dh_any, dh_ref, dmkv_ref):
        del dh_any

        @pl.when(pl.program_id(0) == 0)
        def _():
            dmkv_ref[...] = jnp.zeros_like(dmkv_ref)

        z = z_ref[...].astype(F32)
        sg = _sigmoid(z)
        dyv = dy_ref[...].astype(F32)
        do_all = dyv * (z * sg)
        dh_ref[:, M_W:2 * M_W] = (dyv * r_ref[...].astype(F32) * (sg * (1.0 + z * (1.0 - sg)))).astype(BF16)
        for hd in range(M_HEADS):
            cs = slice(hd * M_DH, (hd + 1) * M_DH)
            vs = slice(M_W + hd * M_DH, M_W + (hd + 1) * M_DH)
            q = q_ref[:, cs]
            p = _mem_probs(q, mkv_ref[:, cs], scale)
            dob = do_all[:, cs].astype(BF16)
            dp = _dot(dob, mkv_ref[:, vs], 1, 1)
            ds = p * (dp - jnp.sum(p * dp, axis=1, keepdims=True))
            dsb = ds.astype(BF16)
            dh_ref[:, cs] = (_dot(dsb, mkv_ref[:, cs], 1, 0) * scale).astype(BF16)
            dmkv_ref[:, cs] += _dot(dsb, q, 0, 0) * scale
            dmkv_ref[:, vs] += _dot(p.astype(BF16), dob, 0, 0)

    return pl.pallas_call(
        body, name="mem_bwd",
        out_shape=(jax.ShapeDtypeStruct(dh_in.shape, BF16), jax.ShapeDtypeStruct((N_MEM, 2 * M_W), F32)),
        grid=(s // MB,),
        in_specs=[pl.BlockSpec((MB, M_W), lambda i: (i, M_BASE // M_W)),
                  pl.BlockSpec((MB, M_W), lambda i: (i, M_BASE // M_W + 1)),
                  pl.BlockSpec((MB, M_W), lambda i: (i, 1536 // M_W)),
                  pl.BlockSpec((MB, M_W), lambda i: (i, 1536 // M_W)),
                  pl.BlockSpec((N_MEM, 2 * M_W), lambda i: (0, 0)), pl.BlockSpec(memory_space=pl.ANY)],
        out_specs=(pl.BlockSpec((MB, 2 * M_W), lambda i: (i, M_BASE // (2 * M_W))),
                   pl.BlockSpec((N_MEM, 2 * M_W), lambda i: (0, 0))),
        input_output_aliases={5: 0},
        compiler_params=_params(("arbitrary",)),
    )(h, h, r, dy, mkv, dh_in)


OB = 256


def _outproj_ln(h, r, w_out, x, ln_g, ln_b):
    s = h.shape[0]

    def body(za_ref, zb_ref, zm_ref, r_ref, w_ref, x_ref, g_ref, b_ref, xn_ref, xh_ref, rstd_ref, y_ref):
        z = jnp.concatenate([za_ref[...], zb_ref[...], zm_ref[...]], axis=1).astype(F32)
        y = (r_ref[...].astype(F32) * (z * _sigmoid(z))).astype(BF16)
        y_ref[...] = y
        u = ALPHA * x_ref[...] + _dot(y, w_ref[...], 1, 0)
        mu = jnp.mean(u, axis=1, keepdims=True)
        uc = u - mu
        rstd = lax.rsqrt(jnp.mean(uc * uc, axis=1, keepdims=True) + LN_EPS)
        xh = uc * rstd
        xh_ref[...] = xh
        rstd_ref[...] = rstd
        xn_ref[...] = xh * g_ref[...] + b_ref[...]

    row = lambda w, c: pl.BlockSpec((OB, w), lambda i: (i, c))
    vec = pl.BlockSpec((1, D), lambda i: (0, 0))
    return pl.pallas_call(
        body, name="outproj_ln",
        out_shape=(jax.ShapeDtypeStruct((s, D), F32), jax.ShapeDtypeStruct((s, D), F32),
                   jax.ShapeDtypeStruct((s, 1), F32), jax.ShapeDtypeStruct((s, D), BF16)),
        grid=(s // OB,),
        in_specs=[row(A_W, A_Z // A_W), row(B_W, (B_BASE + 1024) // B_W), row(M_W, (M_BASE + M_W) // M_W), row(D, 0),
                  pl.BlockSpec((D, D), lambda i: (0, 0)), row(D, 0), vec, vec],
        out_specs=(row(D, 0), row(D, 0), pl.BlockSpec((OB, 1), lambda i: (i, 0)), row(D, 0)),
        compiler_params=_params(("arbitrary",), 56),
    )(h, h, h, r, w_out, x, ln_g, ln_b)


def _ln_bwd(g, xh, rstd, ln_g):
    s = g.shape[0]

    def body(g_ref, xh_ref, rstd_ref, lg_ref, du_ref, dg_ref, db_ref):
        @pl.when(pl.program_id(0) == 0)
        def _():
            dg_ref[...] = jnp.zeros_like(dg_ref)
            db_ref[...] = jnp.zeros_like(db_ref)

        gv, xh = g_ref[...], xh_ref[...]
        dg_ref[...] += jnp.sum(gv * xh, axis=0, keepdims=True)
        db_ref[...] += jnp.sum(gv, axis=0, keepdims=True)
        dxh = gv * lg_ref[...]
        du_ref[...] = rstd_ref[...] * (dxh - jnp.mean(dxh, axis=1, keepdims=True)
                                       - xh * jnp.mean(dxh * xh, axis=1, keepdims=True))

    row = pl.BlockSpec((OB, D), lambda i: (i, 0))
    vec = pl.BlockSpec((1, D), lambda i: (0, 0))
    return pl.pallas_call(
        body, name="ln_bwd",
        out_shape=(jax.ShapeDtypeStruct((s, D), F32), jax.ShapeDtypeStruct((1, D), F32), jax.ShapeDtypeStruct((1, D), F32)),
        grid=(s // OB,),
        in_specs=[row, row, pl.BlockSpec((OB, 1), lambda i: (i, 0)), vec],
        out_specs=(row, vec, vec),
        compiler_params=_params(("arbitrary",)),
    )(g, xh, rstd, ln_g)


def _loss_grad(y, target):
    s = y.shape[0]

    def body(y_ref, t_ref, l_ref, dy_ref):
        @pl.when(pl.program_id(0) == 0)
        def _():
            l_ref[...] = jnp.zeros_like(l_ref)

        e = y_ref[...] - t_ref[...]
        dy_ref[...] = e / D
        l_ref[...] += 0.5 * jnp.sum(jnp.mean(e * e, axis=1, keepdims=True))

    row = pl.BlockSpec((OB, D), lambda i: (i, 0))
    return pl.pallas_call(
        body, name="loss_grad",
        out_shape=(jax.ShapeDtypeStruct((1, 128), F32), jax.ShapeDtypeStruct((s, D), F32)),
        grid=(s // OB,),
        in_specs=[row, row],
        out_specs=(pl.BlockSpec((1, 128), lambda i: (0, 0)), row),
        compiler_params=_params(("arbitrary",)),
    )(y, target)


class _LocalWeights:
    def __init__(self, w_in_p, w_out_f, w_kv_f):
        self.w = list(zip(w_in_p, w_out_f, w_kv_f))
        self.depth = len(self.w)
        self.grads = [dict() for _ in self.w]

    def weights(self, l):
        return self.w[l]

    def host(self, where, l, payload=None):
        if payload is not None:
            self.grads[l][where] = payload
        return None

    def landed(self, where, l, outs):
        pass


def _shards_to_padded(raw):
    sh = raw.shape[2]
    parts = []
    for j in range(N_DEV):
        lo, hi = j * sh, (j + 1) * sh
        if lo < NAT_SPLIT < hi:
            parts += [raw[j][:, :NAT_SPLIT - lo], jnp.zeros((D, H_PAD), raw.dtype), raw[j][:, NAT_SPLIT - lo:]]
        else:
            parts.append(raw[j])
            if hi == NAT_SPLIT:
                parts.append(jnp.zeros((D, H_PAD), raw.dtype))
    return jnp.concatenate(parts, axis=1)


def _padded_to_shards(w):
    sh = IN_W // N_DEV
    pos = lambda c: c if c < NAT_SPLIT else c + H_PAD
    blocks = []
    for j in range(N_DEV):
        lo, hi = j * sh, (j + 1) * sh
        if lo < NAT_SPLIT < hi:
            blocks.append(jnp.concatenate([w[:, lo:NAT_SPLIT], w[:, pos(NAT_SPLIT):pos(hi - 1) + 1]], axis=1))
        else:
            blocks.append(w[:, pos(lo):pos(lo) + sh])
    return jnp.stack(blocks)


_PART_A = (0, 1, 2, 4, 7)
_PART_B = (3, 5, 6)


class _Fsdp:
    def __init__(self, w_in, w_out, w_kv, extra):
        self.sh = (w_in, w_out, w_kv)
        self.depth = w_in.shape[0]
        self.raw = [dict() for _ in range(self.depth)]
        self.recv = [dict() for _ in range(self.depth)]
        g_in, g_out, g_kv, *self.extra = _exchange([w_in[0], w_out[0], w_kv[0]] + list(extra), False, "gather_layer0")
        self.raw[0] = dict(w_in=g_in, w_out=g_out, w_kv=g_kv)

    def weights(self, l):
        raw = self.raw[l]
        return _shards_to_padded(raw["w_in"]), raw["w_out"].reshape(D, D), raw["w_kv"].reshape(D, 2 * M_W)

    def host(self, where, l, payload=None):
        w_in, w_out, w_kv = self.sh
        if where == "in_proj" and l + 1 < self.depth:
            return _Gather([w_in[l + 1]])
        if where == "attn_fwd" and l + 1 < self.depth:
            return _Gather([w_out[l + 1]])
        if where == "gla_fwd" and l + 1 < self.depth:
            return _Gather([w_kv[l + 1]])
        if where == "gla_bwd":
            return _Comm([payload.reshape(N_DEV, D // N_DEV, D)], True)
        if where == "d_w_in":
            return _Comm([payload.reshape(N_DEV, D // N_DEV, 2 * M_W)], True)
        if where == "d_x":
            self.blocks = _padded_to_shards(payload)
            return _Comm([self.blocks], True, relations=_PART_A if l > 0 else tuple(range(N_DEV)))
        if where == "attn_bwd" and l + 1 < self.depth:
            return _Comm([self.blocks], True, relations=_PART_B, into=[self.recv[l + 1]["w_in"]])
        return None

    def landed(self, where, l, outs):
        if where == "in_proj" and outs:
            self.raw[l + 1]["w_in"] = outs[0]
        elif where == "attn_fwd" and outs:
            self.raw[l + 1]["w_out"] = outs[0]
        elif where == "gla_fwd" and outs:
            self.raw[l + 1]["w_kv"] = outs[0]
        elif where == "gla_bwd":
            self.recv[l]["w_out"] = outs[0]
        elif where == "d_w_in":
            self.recv[l]["w_kv"] = outs[0]
        elif where == "d_x":
            self.recv[l]["w_in"] = outs[0]
        elif where == "attn_bwd" and outs:
            self.recv[l + 1]["w_in"] = outs[0]


def _local_step(x, mem, target, pipe, rel, gate_w, gate_b, norm_g, ln_g, ln_b):
    depth = pipe.depth
    s = x.shape[0]
    saved = []
    xl = x
    for l in range(depth):
        w_in_p, w_out_f, w_kv_f = pipe.weights(l)
        hmat, landed = _mm(xl, w_in_p, out_dtype=BF16, tm=1024, tn=512, tk=D, name="in_proj", comm=pipe.host("in_proj", l))
        pipe.landed("in_proj", l, landed)
        mkv, _ = _mm(mem, w_kv_f, out_dtype=BF16, tm=N_MEM, tn=1024, tk=D, name="mem_kv")
        bias = _band_bias(rel[l])
        gw = jnp.zeros((128, B_KW), F32).at[:GATE_RANK].set(gate_w[l]).astype(BF16)
        gb, ng = gate_b[l][None, :], norm_g[l][None, :]
        r, lse, landed = _attn_fwd(hmat, lax.empty((s, D), BF16), bias, pipe.host("attn_fwd", l))
        pipe.landed("attn_fwd", l, landed)
        r, opre, states, landed = _gla_fwd(hmat, r, gw, gb, ng, pipe.host("gla_fwd", l))
        pipe.landed("gla_fwd", l, landed)
        r = _mem_fwd(hmat, r, mkv)
        xn, xh, rstd, y = _outproj_ln(hmat, r, w_out_f, xl, ln_g[l][None, :], ln_b[l][None, :])
        saved.append(dict(x=xl, h=hmat, mkv=mkv, bias=bias, gw=gw, gb=gb, ng=ng, r=r, lse=lse, opre=opre,
                          states=states, xh=xh, rstd=rstd, y=y, w_in_p=w_in_p, w_out_f=w_out_f))
        xl = xn
    loss, g = _loss_grad(xl, target)

    grads = [None] * depth
    for l in reversed(range(depth)):
        sv = saved[l]
        du, d_lng, d_lnb = _ln_bwd(g, sv["xh"], sv["rstd"], ln_g[l][None, :])
        d_wout, _ = _mm(sv["y"], du, ta=True, out_dtype=BF16, tm=1024, tn=1024, tk=1024, name="d_w_out")
        dy, _ = _mm(du, sv["w_out_f"], tb=True, out_dtype=BF16, tm=1024, tn=1024, tk=D, name="d_y")
        dh, dbias, landed = _attn_bwd(sv["h"], sv["r"], dy, sv["lse"], sv["bias"], lax.empty((s, H_W), BF16),
                                      pipe.host("attn_bwd", l))
        pipe.landed("attn_bwd", l, landed)
        dh, d_gw, d_gb, d_ng, landed = _gla_bwd(sv["h"], dy, sv["opre"], sv["states"], sv["gw"], sv["gb"], sv["ng"], dh,
                                                pipe.host("gla_bwd", l, d_wout))
        pipe.landed("gla_bwd", l, landed)
        dh, d_mkv = _mem_bwd(sv["h"], sv["r"], dy, sv["mkv"], dh)
        d_wkv, _ = _mm(mem, d_mkv, ta=True, out_dtype=BF16, tm=1024, tn=1024, tk=N_MEM, name="d_w_kv")
        d_win, landed = _mm(sv["x"], dh, ta=True, out_dtype=BF16, tm=1024, tn=1792, tk=1024, name="d_w_in",
                            comm=pipe.host("d_w_in", l, d_wkv))
        pipe.landed("d_w_in", l, landed)
        g, landed = _mm(dh, sv["w_in_p"], tb=True, out_dtype=F32, tm=1024, tn=1024, tk=1792, name="d_x",
                        adds=((du, ALPHA),), comm=pipe.host("d_x", l, d_win))
        pipe.landed("d_x", l, landed)
        grads[l] = dict(rel=_bias_grad(dbias), gate_w=d_gw[:GATE_RANK], gate_b=d_gb[0], norm_g=d_ng[0],
                        ln_g=d_lng[0], ln_b=d_lnb[0])
    return loss, g, grads


def _adamw(parts, w, m, v, rows_per_step, name):
    depth, rows, cols = w.shape
    n = parts[0].shape[0]
    tr = min(rows_per_step, rows)
    assert rows % tr == 0 and len(parts) == depth

    def body(*refs):
        p_refs = refs[:depth]
        w_ref, m_ref, v_ref, g_ref, d_ref, nm_ref, nv_ref = refs[depth:]
        for l in range(depth):
            @pl.when(pl.program_id(0) == l)
            def _(p_ref=p_refs[l]):
                g = p_ref[0].astype(F32)
                for j in range(1, n):
                    g = g + p_ref[j].astype(F32)
                nm = ADAM_B1 * m_ref[...] + (1.0 - ADAM_B1) * g
                nv = ADAM_B2 * v_ref[...] + (1.0 - ADAM_B2) * (g * g)
                m_hat = nm / (1.0 - ADAM_B1 ** ADAM_STEP)
                v_hat = nv / (1.0 - ADAM_B2 ** ADAM_STEP)
                g_ref[...] = g
                nm_ref[...] = nm
                nv_ref[...] = nv
                d_ref[...] = -ADAM_LR * (m_hat / (jnp.sqrt(v_hat) + ADAM_EPS) + ADAM_WD * w_ref[...])

    def part_spec(l):
        return pl.BlockSpec((n, tr, cols), lambda ll, i: (0, jnp.where(ll == l, i, 0), 0))

    blk = pl.BlockSpec((None, tr, cols), lambda ll, i: (ll, i, 0))
    shape = jax.ShapeDtypeStruct((depth, rows, cols), F32)
    return pl.pallas_call(
        body, name=name,
        out_shape=(shape, shape, shape, shape),
        grid=(depth, rows // tr),
        in_specs=[part_spec(l) for l in range(depth)] + [blk, blk, blk],
        out_specs=(blk, blk, blk, blk),
        compiler_params=_params(("arbitrary", "arbitrary")),
    )(*parts, w, m, v)


SMALL = (("rel", A_HEADS * (2 * MAX_REL + 1)), ("gate_w", GATE_RANK * B_KW), ("gate_b", B_KW), ("norm_g", B_DV),
         ("ln_g", D), ("ln_b", D))


def _pack_small(parts, depth):
    rows = []
    for name, size in SMALL:
        flat = parts[name].reshape(depth * size).astype(F32)
        rows.append(jnp.pad(flat, (0, -(depth * size) % 128)).reshape(-1, 128))
    packed = jnp.concatenate(rows, axis=0)
    return jnp.pad(packed, ((0, -packed.shape[0] % 8), (0, 0)))


def _unpack_small(packed, depth, shapes):
    out, row = {}, 0
    for name, size in SMALL:
        nrow = -(-(depth * size) // 128)
        out[name] = packed[row:row + nrow].reshape(-1)[:depth * size].reshape(shapes[name])
        row += nrow
    return out


def kernel(x, mem, w_in, a_rel_bias, b_gate_w, b_gate_b, b_norm_g, w_mem_kv, w_out, ln_g, ln_b, loss_target, m_w_in, m_a_rel_bias, m_b_gate_w, m_b_gate_b, m_b_norm_g, m_w_mem_kv, m_w_out, m_ln_g, m_ln_b, v_w_in, v_a_rel_bias, v_b_gate_w, v_b_gate_b, v_b_norm_g, v_w_mem_kv, v_w_out, v_ln_g, v_ln_b):
    depth = w_in.shape[0]
    sh_in = w_in.shape[2]
    sh_gw = b_gate_w.shape[2]
    me = 4 * lax.axis_index("x") + 2 * lax.axis_index("y") + lax.axis_index("c")

    pipe = _Fsdp(w_in.astype(BF16), w_out.astype(BF16), w_mem_kv.astype(BF16), [b_gate_w])
    gate_w_full = jnp.transpose(pipe.extra[0], (1, 2, 0, 3)).reshape(depth, GATE_RANK, N_DEV * sh_gw)
    loss_dev, dx, grads = _local_step(x[0], mem[0], loss_target[0], pipe,
                                      a_rel_bias, gate_w_full, b_gate_b, b_norm_g, ln_g, ln_b)
    loss = lax.psum(loss_dev[0, 0], ("x", "y", "c"))

    recv = lambda n: [pipe.recv[l][n] for l in range(depth)]
    big = {"w_in": _adamw(recv("w_in"), w_in, m_w_in, v_w_in, 128, "adamw_w_in"),
           "w_out": _adamw(recv("w_out"), w_out, m_w_out, v_w_out, 64, "adamw_w_out"),
           "w_kv": _adamw(recv("w_kv"), w_mem_kv, m_w_mem_kv, v_w_mem_kv, 128, "adamw_w_kv")}

    shapes = {"rel": a_rel_bias.shape, "gate_w": (depth, GATE_RANK, N_DEV * sh_gw), "gate_b": b_gate_b.shape,
              "norm_g": b_norm_g.shape, "ln_g": ln_g.shape, "ln_b": ln_b.shape}
    part = _pack_small({n: jnp.stack([grads[l][n] for l in range(depth)]) for n, _ in SMALL}, depth)
    (all_parts,) = _exchange([part], False, "gather_small")
    zeros_gw = jnp.zeros(shapes["gate_w"], F32)
    w_s = _pack_small(dict(rel=a_rel_bias, gate_w=zeros_gw, gate_b=b_gate_b, norm_g=b_norm_g, ln_g=ln_g, ln_b=ln_b), depth)
    m_s = _pack_small(dict(rel=m_a_rel_bias, gate_w=zeros_gw, gate_b=m_b_gate_b, norm_g=m_b_norm_g, ln_g=m_ln_g, ln_b=m_ln_b), depth)
    v_s = _pack_small(dict(rel=v_a_rel_bias, gate_w=zeros_gw, gate_b=v_b_gate_b, norm_g=v_b_norm_g, ln_g=v_ln_g, ln_b=v_ln_b), depth)
    small = [_unpack_small(t[0], depth, shapes)
             for t in _adamw([all_parts], w_s[None], m_s[None], v_s[None], all_parts.shape[1], "adamw_small")]
    gw_grad = lax.dynamic_slice_in_dim(small[0]["gate_w"], me * sh_gw, sh_gw, axis=2).reshape(1, depth * GATE_RANK, sh_gw)
    flat = lambda t: t.reshape(1, depth * GATE_RANK, sh_gw)
    gw_res = [t.reshape(depth, GATE_RANK, sh_gw)
              for t in _adamw([gw_grad], flat(b_gate_w), flat(m_b_gate_w), flat(v_b_gate_w), depth * GATE_RANK, "adamw_gate_w")]

    def leaves(t):
        return (big["w_in"][t], small[t]["rel"], gw_res[t], small[t]["gate_b"], small[t]["norm_g"],
                big["w_kv"][t], big["w_out"][t], small[t]["ln_g"], small[t]["ln_b"])

    return (loss, dx[None]) + leaves(0) + leaves(1) + leaves(2) + leaves(3)
```

```python
import functools
import math

import numpy as np
import jax
import jax.numpy as jnp
from jax import lax
from jax.experimental import pallas as pl
from jax.experimental.pallas import tpu as pltpu

F32 = jnp.float32
BF16 = jnp.bfloat16

N_DEV = 8
D = 2048
CH = 64
LEFT = 8
MAX_REL = 128
N_MEM = 256
A_HEADS, A_DH, A_W = 8, 128, 1024
B_HEADS, B_DK, B_DV, B_KW, B_W = 4, 64, 128, 256, 512
GATE_RANK, GATE_TAU = 16, 16.0
M_HEADS, M_DH, M_W = 4, 128, 512
IN_W = 6672
NAT_SPLIT = 5648
H_W = 7168
H_PAD = H_W - IN_W
A_Q, A_K, A_V, A_Z = 0, 1024, 2048, 3072
B_BASE = 4096
M_BASE = 6144
ALPHA = (2.0 * 4) ** 0.25
LN_EPS = 1e-5
RMS_EPS = 1e-6
NEG = -1e30
QB = 256
KB = 3 * QB
ADAM_LR, ADAM_B1, ADAM_B2, ADAM_EPS, ADAM_WD, ADAM_STEP = 0.001, 0.9, 0.999, 1e-08, 0.01, 10
VMEM_MB = 1024 * 1024


def _params(sem, vmem_mb=48):
    return pltpu.CompilerParams(dimension_semantics=sem, vmem_limit_bytes=vmem_mb * VMEM_MB)


def _sigmoid(x):
    return 1.0 / (1.0 + jnp.exp(-x))


def _dot(a, b, ca, cb, precision=None):
    return lax.dot_general(a, b, (((ca,), (cb,)), ((), ())), preferred_element_type=F32, precision=precision)


MESH = pl.DeviceIdType.MESH


class _Comm:
    def __init__(self, xs, scatter, relations=tuple(range(N_DEV)), into=None):
        self.xs, self.scatter, self.n = list(xs), scatter, len(xs)
        self.relations, self.into = tuple(relations), list(into or [])
        self.out_shape = [jax.ShapeDtypeStruct((N_DEV,) + (x.shape[1:] if scatter else x.shape), x.dtype) for x in xs]
        self.specs = [pl.BlockSpec(memory_space=pltpu.HBM)] * self.n
        self.scratch = [pltpu.SemaphoreType.DMA((self.n, N_DEV)), pltpu.SemaphoreType.DMA((self.n, N_DEV)),
                        pltpu.SemaphoreType.DMA((self.n,))]

    def _copies(self, x_refs, o_refs, sems):
        send_sems, recv_sems, local_sems = sems
        mx, my, mc = lax.axis_index("x"), lax.axis_index("y"), lax.axis_index("c")
        me = 4 * mx + 2 * my + mc
        own, sends, arrivals = [], [], []
        for k in self.relations:
            if k == 0:
                own = [pltpu.make_async_copy(x_refs[a].at[me] if self.scatter else x_refs[a], o_refs[a].at[me],
                                             local_sems.at[a]) for a in range(self.n)]
                continue
            px = 1 - mx if k & 4 else mx
            py = 1 - my if k & 2 else my
            pc = 1 - mc if k & 1 else mc
            idx = 4 * px + 2 * py + pc
            for a in range(self.n):
                src = x_refs[a].at[idx] if self.scatter else x_refs[a]
                for dst, group in ((o_refs[a].at[me], sends), (o_refs[a].at[idx], arrivals)):
                    group.append(pltpu.make_async_remote_copy(
                        src_ref=src, dst_ref=dst, send_sem=send_sems.at[a, k], recv_sem=recv_sems.at[a, k],
                        device_id=(px, py, pc), device_id_type=MESH))
        return own, sends, arrivals

    def start(self, x_refs, o_refs, sems):
        own, sends, _ = self._copies(x_refs, o_refs, sems)
        for cp in own + sends:
            cp.start()

    def middle(self, x_refs, o_refs, sems):
        pass

    def finish(self, x_refs, o_refs, sems):
        own, sends, arrivals = self._copies(x_refs, o_refs, sems)
        for cp in sends:
            cp.wait_send()
        for cp in arrivals:
            cp.wait_recv()
        for cp in own:
            cp.wait()


class _Gather:
    into = ()

    def __init__(self, xs):
        self.xs, self.n = list(xs), len(xs)
        self.out_shape = [jax.ShapeDtypeStruct((N_DEV,) + x.shape, x.dtype) for x in xs]
        self.specs = [pl.BlockSpec(memory_space=pltpu.HBM)] * self.n
        self.scratch = [pltpu.SemaphoreType.DMA((self.n, N_DEV - 1)), pltpu.SemaphoreType.DMA((self.n, N_DEV - 1)),
                        pltpu.SemaphoreType.DMA((self.n,))]

    def _copies(self, x_refs, o_refs, sems):
        send_sems, recv_sems, local_sems = sems
        mx, my, mc = lax.axis_index("x"), lax.axis_index("y"), lax.axis_index("c")
        idx = lambda px, py, pc: 4 * px + 2 * py + pc
        me, sibling = (mx, my, mc), (mx, my, 1 - mc)
        chips = [(mx, 1 - my), (1 - mx, my), (1 - mx, 1 - my)]

        def copy(a, k, src, slot, to):
            return pltpu.make_async_remote_copy(
                src_ref=src, dst_ref=o_refs[a].at[idx(*slot)], send_sem=send_sems.at[a, k], recv_sem=recv_sems.at[a, k],
                device_id=to, device_id_type=MESH)

        c = dict(own=[], first=[], passed=[], ici_in=[], late_in=[])
        for a in range(self.n):
            x = x_refs[a]
            c["own"].append(pltpu.make_async_copy(x, o_refs[a].at[idx(*me)], local_sems.at[a]))
            c["first"].append(copy(a, 0, x, me, sibling))
            c["late_in"].append(copy(a, 0, x, sibling, sibling))
            for j, chip in enumerate(chips):
                c["first"].append(copy(a, 1 + j, x, me, (*chip, mc)))
                c["ici_in"].append(copy(a, 1 + j, x, (*chip, mc), (*chip, mc)))
                c["passed"].append(copy(a, 4 + j, o_refs[a].at[idx(*chip, mc)], (*chip, mc), sibling))
                c["late_in"].append(copy(a, 4 + j, x, (*chip, 1 - mc), sibling))
        return c

    def start(self, x_refs, o_refs, sems):
        c = self._copies(x_refs, o_refs, sems)
        for cp in c["own"] + c["first"]:
            cp.start()

    def middle(self, x_refs, o_refs, sems):
        c = self._copies(x_refs, o_refs, sems)
        for arrived, onward in zip(c["ici_in"], c["passed"]):
            arrived.wait_recv()
            onward.start()

    def finish(self, x_refs, o_refs, sems):
        c = self._copies(x_refs, o_refs, sems)
        for cp in c["first"] + c["passed"]:
            cp.wait_send()
        for cp in c["late_in"]:
            cp.wait_recv()
        for cp in c["own"]:
            cp.wait()


def _exchange(xs, scatter, name):
    comm = _Comm(xs, scatter) if scatter else _Gather(xs)

    def body(*refs):
        x_refs, o_refs, sems = refs[:comm.n], refs[comm.n:2 * comm.n], refs[2 * comm.n:]
        comm.start(x_refs, o_refs, sems)
        comm.middle(x_refs, o_refs, sems)
        comm.finish(x_refs, o_refs, sems)

    return pl.pallas_call(body, name=name, out_shape=tuple(comm.out_shape), in_specs=comm.specs,
                          out_specs=tuple(comm.specs), scratch_shapes=comm.scratch)(*comm.xs)


def _hosted(body, comm, *, name, grid, in_specs, out_specs, out_shape, scratch_shapes, compiler_params, inputs,
            input_output_aliases=None):
    out_specs, out_shape = tuple(out_specs), tuple(out_shape)
    aliases = input_output_aliases or {}
    if comm is None:
        outs = pl.pallas_call(body, name=name, grid=grid, in_specs=list(in_specs), out_specs=out_specs, out_shape=out_shape,
                              scratch_shapes=list(scratch_shapes), compiler_params=compiler_params,
                              input_output_aliases=aliases)(*inputs)
        return tuple(outs), ()
    ni, no, ns, nc = len(in_specs), len(out_specs), len(scratch_shapes), comm.n
    n_into = len(comm.into)
    aliases = {**aliases, **{ni + nc + a: no + a for a in range(n_into)}}

    def wrapped(*refs):
        ins, c_in = refs[:ni], refs[ni:ni + nc]
        refs = refs[ni + nc + n_into:]
        outs, c_out = refs[:no], refs[no:no + nc]
        scr, sems = refs[no + nc:no + nc + ns], refs[no + nc + ns:]
        first = functools.reduce(jnp.logical_and, [pl.program_id(d) == 0 for d in range(len(grid))])
        last = functools.reduce(jnp.logical_and, [pl.program_id(d) == grid[d] - 1 for d in range(len(grid))])

        step = functools.reduce(lambda acc, d: acc * grid[d] + pl.program_id(d), range(len(grid)), 0)
        n_steps = math.prod(grid)

        @pl.when(first)
        def _():
            comm.start(c_in, c_out, sems)

        body(*ins, *outs, *scr)

        @pl.when(step == max(3 * n_steps // 4, 1) - 1)
        def _():
            comm.middle(c_in, c_out, sems)

        @pl.when(last)
        def _():
            comm.finish(c_in, c_out, sems)

    params = pltpu.CompilerParams(dimension_semantics=("arbitrary",) * len(grid),
                                  vmem_limit_bytes=compiler_params.vmem_limit_bytes)
    outs = pl.pallas_call(wrapped, name=name, grid=grid, in_specs=list(in_specs) + comm.specs + comm.specs[:n_into],
                          out_specs=out_specs + tuple(comm.specs), out_shape=out_shape + tuple(comm.out_shape),
                          scratch_shapes=list(scratch_shapes) + comm.scratch, compiler_params=params,
                          input_output_aliases=aliases)(*inputs, *comm.xs, *comm.into)
    return tuple(outs[:no]), tuple(outs[no:])


def _mm(a, b, *, ta=False, tb=False, out_dtype, tm, tn, tk, name, adds=(), vmem_mb=48, comm=None):
    m = a.shape[1] if ta else a.shape[0]
    k = a.shape[0] if ta else a.shape[1]
    n = b.shape[0] if tb else b.shape[1]
    assert k == (b.shape[1] if tb else b.shape[0])
    tm, tn, tk = min(tm, m), min(tn, n), min(tk, k)
    assert m % tm == 0 and n % tn == 0 and k % tk == 0, (name, m, n, k)
    nk = k // tk
    n_add = len(adds)
    scales = [s for _, s in adds]

    def body(a_ref, b_ref, *rest):
        add_refs, o_ref = rest[:n_add], rest[n_add]
        kk = pl.program_id(2)

        def product():
            return _dot(a_ref[...].astype(BF16), b_ref[...].astype(BF16), 0 if ta else 1, 1 if tb else 0)

        def finish(r):
            for ref, s in zip(add_refs, scales):
                r = r + s * ref[...].astype(F32)
            o_ref[...] = r.astype(out_dtype)

        if nk == 1:
            finish(product())
            return
        acc_ref = rest[n_add + 1]

        @pl.when(kk == 0)
        def _():
            acc_ref[...] = product()

        if nk > 2:
            @pl.when((kk > 0) & (kk < nk - 1))
            def _():
                acc_ref[...] += product()

        @pl.when(kk == nk - 1)
        def _():
            finish(acc_ref[...] + product())

    a_spec = pl.BlockSpec((tk, tm), lambda i, j, kk: (kk, i)) if ta else pl.BlockSpec((tm, tk), lambda i, j, kk: (i, kk))
    b_spec = pl.BlockSpec((tn, tk), lambda i, j, kk: (j, kk)) if tb else pl.BlockSpec((tk, tn), lambda i, j, kk: (kk, j))
    add_specs = [pl.BlockSpec((tm, tn), lambda i, j, kk: (i, j)) for _ in adds]
    (out,), c_out = _hosted(
        body, comm, name=name,
        out_shape=[jax.ShapeDtypeStruct((m, n), out_dtype)],
        grid=(m // tm, n // tn, nk),
        in_specs=[a_spec, b_spec] + add_specs,
        out_specs=[pl.BlockSpec((tm, tn), lambda i, j, kk: (i, j))],
        scratch_shapes=[pltpu.VMEM((tm, tn), F32)] if nk > 1 else [],
        compiler_params=_params(("parallel", "parallel", "arbitrary"), vmem_mb),
        inputs=(a, b, *[x for x, _ in adds]))
    return out, c_out


NKB = KB // QB
LEAD = NKB - 1


def _band_bias(table):
    i = np.arange(QB)[:, None]
    j = np.arange(KB)[None, :]
    qc = i // CH + LEAD * QB // CH
    kc = j // CH
    valid = (kc <= qc) & (kc >= qc - LEFT)
    n = QB + KB
    c = np.arange(n)
    onehot = np.zeros((2 * MAX_REL + 1, n), np.float32)
    onehot[np.clip(LEAD * QB - (c - (QB - 1)), -MAX_REL, MAX_REL) + MAX_REL, c] = 1.0
    row = jnp.dot(table.astype(F32), jnp.asarray(onehot), precision=lax.Precision.HIGHEST)
    flow = jnp.tile(row, (1, QB))[:, :QB * (n - 1)].reshape(table.shape[0], QB, n - 1)
    return jnp.where(valid[None], flow[:, :, QB - 1:], NEG)


def _bias_grad(dbias):
    h, n = dbias.shape[0], KB + 1
    flat = jnp.pad(dbias.reshape(h, QB * KB), ((0, 0), (0, -(QB * KB) % n)))
    diag = flat.reshape(h, -1, n).sum(axis=1)
    c = np.arange(n)
    jm = np.where(c < LEAD * QB + CH, c, c - n)
    didx = np.clip(LEAD * QB - jm, -MAX_REL, MAX_REL) + MAX_REL
    onehot = np.zeros((n, 2 * MAX_REL + 1), np.float32)
    onehot[c, didx] = 1.0
    return jnp.dot(diag, jnp.asarray(onehot), precision=lax.Precision.HIGHEST)


def _attn_scores(q, k_refs, cs, bias_h, m, masked, scale):
    parts = []
    for t in range(NKB):
        sc = _dot(q, k_refs[t][:, cs], 1, 1) * scale
        parts.append(jnp.where(m + t - LEAD >= 0, sc, NEG) if masked else sc)
    return jnp.concatenate(parts, axis=1) + bias_h


def _attn_fwd(h, r_in, bias, comm=None):
    s = h.shape[0]
    nq = s // QB
    scale = A_DH ** -0.5

    def body(q_ref, *rest):
        k_refs, v_refs = rest[:NKB], rest[NKB:2 * NKB]
        bias_hbm, r_any, o_ref, lse_ref, bias_ref = rest[2 * NKB:]
        del r_any
        m = pl.program_id(0)

        @pl.when(m == 0)
        def _():
            pltpu.sync_copy(bias_hbm, bias_ref)

        def scores(hd, masked):
            cs = slice(hd * A_DH, (hd + 1) * A_DH)
            return _attn_scores(q_ref[:, cs], k_refs, cs, bias_ref[hd], m, masked, scale)

        def step(masked):
            sc_next = scores(0, masked)
            for hd in range(A_HEADS):
                cs = slice(hd * A_DH, (hd + 1) * A_DH)
                sc = sc_next
                if hd + 1 < A_HEADS:
                    sc_next = scores(hd + 1, masked)
                mx = jnp.max(sc, axis=1, keepdims=True)
                p = jnp.exp(sc - mx)
                l = jnp.sum(p, axis=1, keepdims=True)
                pb = p.astype(BF16)
                o = _dot(pb[:, :QB], v_refs[0][:, cs], 1, 0)
                for t in range(1, NKB):
                    o += _dot(pb[:, t * QB:(t + 1) * QB], v_refs[t][:, cs], 1, 0)
                o_ref[:, cs] = (o / l).astype(BF16)
                lse_ref[:, hd:hd + 1] = mx + jnp.log(l)

        pl.when(m < LEAD)(functools.partial(step, True))
        pl.when(m >= LEAD)(functools.partial(step, False))

    def kv_spec(col, t):
        return pl.BlockSpec((QB, A_W), lambda m: (jnp.maximum(m + t - LEAD, 0), col))

    (r, lse), c_out = _hosted(
        body, comm, name="attn_fwd",
        out_shape=(jax.ShapeDtypeStruct(r_in.shape, BF16), jax.ShapeDtypeStruct((s, A_HEADS), F32)),
        grid=(nq,),
        in_specs=[pl.BlockSpec((QB, A_W), lambda m: (m, 0))]
        + [kv_spec(1, t) for t in range(NKB)] + [kv_spec(2, t) for t in range(NKB)]
        + [pl.BlockSpec(memory_space=pl.ANY), pl.BlockSpec(memory_space=pl.ANY)],
        out_specs=(pl.BlockSpec((QB, A_W), lambda m: (m, 0)), pl.BlockSpec((QB, A_HEADS), lambda m: (m, 0))),
        scratch_shapes=[pltpu.VMEM((A_HEADS, QB, KB), F32)],
        input_output_aliases={2 * NKB + 2: 0},
        compiler_params=_params(("arbitrary",)),
        inputs=(h,) * (2 * NKB + 1) + (bias, r_in))
    return r, lse, c_out


def _attn_bwd(h, r, dy, lse, bias, dh_in, comm=None):
    s = h.shape[0]
    nq = s // QB
    scale = A_DH ** -0.5

    def body(q_ref, *rest):
        k_refs, v_refs = rest[:NKB], rest[NKB:2 * NKB]
        (z_ref, r_ref, dy_ref, lse_ref, zl_ref, rl_ref, dyl_ref, bias_hbm, dh_any,
         dh_ref, dbias_hbm, dk_acc, dv_acc, dq_ring, bias_ref, dbias_ref) = rest[2 * NKB:]
        del dh_any
        m = pl.program_id(0)

        def slot(b):
            return pl.multiple_of(lax.rem(b + NKB, NKB) * QB, QB)

        @pl.when(m == 0)
        def _():
            dk_acc[...] = jnp.zeros_like(dk_acc)
            dv_acc[...] = jnp.zeros_like(dv_acc)
            dq_ring[...] = jnp.zeros_like(dq_ring)
            dbias_ref[...] = jnp.zeros_like(dbias_ref)
            pltpu.sync_copy(bias_hbm, bias_ref)

        def step(masked):
            z = z_ref[...].astype(F32)
            do_all = dy_ref[...].astype(F32) * (z * _sigmoid(z))
            o_all = r_ref[...].astype(F32)

            def products(hd):
                cs = slice(hd * A_DH, (hd + 1) * A_DH)
                dob = do_all[:, cs].astype(BF16)
                sc = _attn_scores(q_ref[:, cs], k_refs, cs, bias_ref[hd], m, masked, scale)
                return sc, jnp.concatenate([_dot(dob, v_refs[t][:, cs], 1, 1) for t in range(NKB)], axis=1)

            ahead = products(0)
            for hd in range(A_HEADS):
                cs = slice(hd * A_DH, (hd + 1) * A_DH)
                q = q_ref[:, cs]
                do = do_all[:, cs]
                dob = do.astype(BF16)
                delta = jnp.sum(do * o_all[:, cs], axis=1, keepdims=True)
                sc, dp = ahead
                if hd + 1 < A_HEADS:
                    ahead = products(hd + 1)
                p = jnp.exp(sc - lse_ref[:, hd:hd + 1])
                ds = p * (dp - delta)
                dbias_ref[hd] += ds
                pb, dsb = p.astype(BF16), ds.astype(BF16)
                dq = jnp.zeros((QB, A_DH), F32)
                for t in range(NKB):
                    ts = slice(t * QB, (t + 1) * QB)
                    rows = pl.ds(slot(m - LEAD + t), QB)
                    dq += _dot(dsb[:, ts], k_refs[t][:, cs], 1, 0)
                    dk_acc[rows, cs] += _dot(dsb[:, ts], q, 0, 0) * scale
                    dv_acc[rows, cs] += _dot(pb[:, ts], dob, 0, 0)
                dq_ring[pl.ds(slot(m), QB), cs] = dq * scale

        pl.when(m < LEAD)(functools.partial(step, True))
        pl.when((m >= LEAD) & (m < nq))(functools.partial(step, False))

        done = pl.ds(slot(m - LEAD), QB)
        zl = zl_ref[...].astype(F32)
        sg = _sigmoid(zl)
        dz = dyl_ref[...].astype(F32) * rl_ref[...].astype(F32) * (sg * (1.0 + zl * (1.0 - sg)))
        dh_ref[:, A_Q:A_Q + A_W] = dq_ring[done, :].astype(BF16)
        dh_ref[:, A_K:A_K + A_W] = dk_acc[done, :].astype(BF16)
        dh_ref[:, A_V:A_V + A_W] = dv_acc[done, :].astype(BF16)
        dh_ref[:, A_Z:A_Z + A_W] = dz.astype(BF16)
        dk_acc[done, :] = jnp.zeros((QB, A_W), F32)
        dv_acc[done, :] = jnp.zeros((QB, A_W), F32)

        @pl.when(m == nq + LEAD - 1)
        def _():
            pltpu.sync_copy(dbias_ref, dbias_hbm)

    last = nq - 1

    def cur(col):
        return pl.BlockSpec((QB, A_W), lambda m: (jnp.minimum(m, last), col))

    def kv_spec(col, t):
        return pl.BlockSpec((QB, A_W), lambda m: (jnp.clip(m + t - LEAD, 0, last), col))

    def lag(col):
        return pl.BlockSpec((QB, A_W), lambda m: (jnp.clip(m - LEAD, 0, last), col))

    (dh, dbias), c_out = _hosted(
        body, comm, name="attn_bwd",
        out_shape=(jax.ShapeDtypeStruct(dh_in.shape, BF16), jax.ShapeDtypeStruct((A_HEADS, QB, KB), F32)),
        grid=(nq + LEAD,),
        in_specs=[cur(0)] + [kv_spec(1, t) for t in range(NKB)] + [kv_spec(2, t) for t in range(NKB)]
        + [cur(3), cur(0), cur(0), pl.BlockSpec((QB, A_HEADS), lambda m: (jnp.minimum(m, last), 0)),
           lag(3), lag(0), lag(0),
           pl.BlockSpec(memory_space=pl.ANY), pl.BlockSpec(memory_space=pl.ANY)],
        out_specs=(pl.BlockSpec((QB, 4 * A_W), lambda m: (jnp.clip(m - LEAD, 0, last), 0)),
                   pl.BlockSpec(memory_space=pl.ANY)),
        scratch_shapes=[pltpu.VMEM((KB, A_W), F32), pltpu.VMEM((KB, A_W), F32), pltpu.VMEM((KB, A_W), F32),
                        pltpu.VMEM((A_HEADS, QB, KB), F32), pltpu.VMEM((A_HEADS, QB, KB), F32)],
        input_output_aliases={2 * NKB + 9: 0},
        compiler_params=_params(("arbitrary",), 56),
        inputs=(h,) * (2 * NKB + 2) + (r, dy, lse, h, r, dy, bias, dh_in))
    return dh, dbias, c_out


GB = 256
N_PAIR = B_HEADS // 2


def _gla_gates(lr, gw_ref, gb_ref):
    logit = _dot(lr, gw_ref[...], 1, 0) + gb_ref[...]
    lg = (jnp.minimum(logit, 0.0) - jnp.log(1.0 + jnp.exp(-jnp.abs(logit)))) / GATE_TAU
    row = lax.broadcasted_iota(jnp.int32, (GB, GB), 0)
    col = lax.broadcasted_iota(jnp.int32, (GB, GB), 1)
    tri = jnp.where((row // CH == col // CH) & (col <= row), 1.0, 0.0).astype(F32)
    return logit, _dot(tri, lg, 1, 0, precision=lax.Precision.HIGHEST)


def _gla_factors(hb_ref, b_all, c):
    rs = slice(c * CH, (c + 1) * CH)
    q = hb_ref[rs, 0:B_KW].astype(F32) * (B_DK ** -0.5)
    k = hb_ref[rs, B_KW:2 * B_KW].astype(F32)
    b = b_all[rs]
    bm, bl = b[CH // 2:CH // 2 + 1, :], b[CH - 1:CH, :]
    e1, e2, eb, ek = jnp.exp(b - bm), jnp.exp(bm - b), jnp.exp(b), jnp.exp(bl - b)
    el = jnp.exp(bl)
    return dict(ql=q * e1, kl=k * e2, qu=q * e2, ku=k * e1, qt=q * eb, kh=k * ek, e1=e1, e2=e2, eb=eb, ek=ek, el=el)


class _GlaPairTools:
    def __init__(self, hb_ref, fs):
        self.hb_ref, self.fs = hb_ref, fs
        row = lax.broadcasted_iota(jnp.int32, (2 * CH, 2 * CH), 0)
        col = lax.broadcasted_iota(jnp.int32, (2 * CH, 2 * CH), 1)
        self.same = (row // CH) == (col // CH)
        self.lower = self.same & ((row % CH) >= (col % CH))
        self.upper = self.same & ((row % CH) < (col % CH))
        self.lower_t = self.same & ((col % CH) >= (row % CH))
        self.upper_t = self.same & ((col % CH) < (row % CH))

    def lanes(self, c, p, name):
        return self.fs[c][name][:, p * 128:(p + 1) * 128]

    def heads(self, c, p, name):
        x = self.lanes(c, p, name)
        return jnp.where(self.same, jnp.concatenate([x, x], axis=0), 0.0).astype(BF16)

    def twice(self, c, p, name):
        x = self.lanes(c, p, name).astype(BF16)
        return jnp.concatenate([x, x], axis=0)

    def vals(self, c, p):
        rows = slice(c * CH, (c + 1) * CH)
        return jnp.concatenate([self.hb_ref[rows, 512 + (2 * p + sh) * B_DV:512 + (2 * p + sh + 1) * B_DV]
                                for sh in range(2)], axis=0)

    def intra(self, c, p):
        lo = _dot(self.heads(c, p, "ql"), self.twice(c, p, "kl"), 1, 1)
        up = _dot(self.heads(c, p, "qu"), self.twice(c, p, "ku"), 1, 1)
        return jnp.where(self.lower, lo, jnp.where(self.upper, up, 0.0)).astype(BF16)

    def fold(self, x):
        x = jnp.where(self.same, x, 0.0)
        return x[:CH] + x[CH:]


def _gla_fwd(h, r_in, gw, gb, ng, comm=None):
    s = h.shape[0]
    nb = s // GB
    cpb = GB // CH

    def body(hb_ref, gw_ref, gb_ref, ng_ref, r_any, o_ref, opre_ref, st_ref, state):
        del r_any

        @pl.when(pl.program_id(0) == 0)
        def _():
            state[...] = jnp.zeros_like(state)

        _, b_all = _gla_gates(hb_ref[:, 1536:1664], gw_ref, gb_ref)
        fs = [_gla_factors(hb_ref, b_all, c) for c in range(cpb)]
        pairs = [(c, p) for c in range(cpb) for p in range(N_PAIR)]
        tools = _GlaPairTools(hb_ref, fs)
        a = {k: tools.intra(*k) for k in pairs}
        o_intra = {k: _dot(a[k], tools.vals(*k), 1, 0) for k in pairs}
        upd = {k: _dot(tools.vals(*k), tools.heads(*k, "kh"), 0, 0) for k in pairs}
        st = [state[p] for p in range(N_PAIR)]
        entering = {}
        for c, p in pairs:
            entering[c, p] = st[p]
            st_ref[c, p] = st[p]
            st[p] = st[p] * fs[c]["el"][:, p * 128:(p + 1) * 128] + upd[c, p]
        for p in range(N_PAIR):
            state[p] = st[p]
        for c, p in pairs:
            o2 = o_intra[c, p] + _dot(tools.heads(c, p, "qt"), entering[c, p].astype(BF16), 1, 1)
            for sh in range(2):
                o = o2[sh * CH:(sh + 1) * CH]
                rs, hs = slice(c * CH, (c + 1) * CH), slice((2 * p + sh) * B_DV, (2 * p + sh + 1) * B_DV)
                opre_ref[rs, hs] = o
                rinv = lax.rsqrt(jnp.mean(o * o, axis=1, keepdims=True) + RMS_EPS)
                o_ref[rs, hs] = (o * rinv * ng_ref[...]).astype(BF16)

    (r, opre, states), c_out = _hosted(
        body, comm, name="gla_fwd",
        out_shape=(jax.ShapeDtypeStruct(r_in.shape, BF16), jax.ShapeDtypeStruct((s, B_W), F32),
                   jax.ShapeDtypeStruct((s // CH, N_PAIR, 128, 128), F32)),
        grid=(nb,),
        in_specs=[pl.BlockSpec((GB, 2048), lambda i: (i, B_BASE // 2048)),
                  pl.BlockSpec((128, B_KW), lambda i: (0, 0)), pl.BlockSpec((1, B_KW), lambda i: (0, 0)),
                  pl.BlockSpec((1, B_DV), lambda i: (0, 0)), pl.BlockSpec(memory_space=pl.ANY)],
        out_specs=(pl.BlockSpec((GB, B_W), lambda i: (i, 1024 // B_W)), pl.BlockSpec((GB, B_W), lambda i: (i, 0)),
                   pl.BlockSpec((cpb, N_PAIR, 128, 128), lambda i: (i, 0, 0, 0))),
        scratch_shapes=[pltpu.VMEM((N_PAIR, 128, 128), F32)],
        input_output_aliases={4: 0},
        compiler_params=_params(("arbitrary",)),
        inputs=(h, gw, gb, ng, r_in))
    return r, opre, states, c_out


def _gla_bwd(h, dy, opre, states, gw, gb, ng, dh_in, comm=None):
    s = h.shape[0]
    nb = s // GB
    cpb = GB // CH

    def body(hb_ref, dy_ref, opre_ref, st_ref, gw_ref, gb_ref, ng_ref, dh_any,
             dh_ref, dgw_ref, dgb_ref, dng_ref, dstate, db_scr, do_scr):
        del dh_any

        @pl.when(pl.program_id(0) == 0)
        def _():
            dstate[...] = jnp.zeros_like(dstate)
            dgw_ref[...] = jnp.zeros_like(dgw_ref)
            dgb_ref[...] = jnp.zeros_like(dgb_ref)
            dng_ref[...] = jnp.zeros_like(dng_ref)

        lr = hb_ref[:, 1536:1664]
        logit, b_all = _gla_gates(lr, gw_ref, gb_ref)
        z = hb_ref[:, 1024:1536].astype(F32)
        sg = _sigmoid(z)
        dyb = dy_ref[...].astype(F32)
        dng = jnp.zeros((1, B_DV), F32)
        for hd in range(B_HEADS):
            hs = slice(hd * B_DV, (hd + 1) * B_DV)
            o = opre_ref[:, hs]
            rinv = lax.rsqrt(jnp.mean(o * o, axis=1, keepdims=True) + RMS_EPS)
            on = o * rinv
            dr = dyb[:, hs] * (z[:, hs] * sg[:, hs])
            dh_ref[:, 1024 + hd * B_DV:1024 + (hd + 1) * B_DV] = (
                dyb[:, hs] * (on * ng_ref[...]) * (sg[:, hs] * (1.0 + z[:, hs] * (1.0 - sg[:, hs])))).astype(BF16)
            dng += jnp.sum(dr * on, axis=0, keepdims=True)
            dn = dr * ng_ref[...]
            do_scr[:, hs] = rinv * (dn - on * jnp.mean(dn * on, axis=1, keepdims=True))
        dng_ref[...] += dng

        rowi = lax.broadcasted_iota(jnp.int32, (CH, 128), 0)
        fs = [_gla_factors(hb_ref, b_all, c) for c in range(cpb)]
        pairs = [(c, p) for c in reversed(range(cpb)) for p in range(N_PAIR)]
        tools = _GlaPairTools(hb_ref, fs)
        pair = tools.lanes

        def douts(c, p):
            rows = slice(c * CH, (c + 1) * CH)
            return jnp.concatenate([do_scr[rows, (2 * p + sh) * B_DV:(2 * p + sh + 1) * B_DV] for sh in range(2)],
                                   axis=0).astype(BF16)

        at, da, dat, dst_own, g_qt = {}, {}, {}, {}, {}
        for k in pairs:
            c, p = k
            lo_t = _dot(tools.twice(*k, "kl"), tools.heads(*k, "ql"), 1, 1)
            up_t = _dot(tools.twice(*k, "ku"), tools.heads(*k, "qu"), 1, 1)
            at[k] = jnp.where(tools.lower_t, lo_t, jnp.where(tools.upper_t, up_t, 0.0)).astype(BF16)
            da[k] = _dot(douts(*k), tools.vals(*k), 1, 1)
            dat[k] = _dot(tools.vals(*k), douts(*k), 1, 1)
            dst_own[k] = _dot(douts(*k), tools.heads(*k, "qt"), 0, 0)
            g_qt[k] = tools.fold(_dot(douts(*k), st_ref[c, p].astype(BF16), 1, 0))
        dv, g_ql, g_qu, g_kl, g_ku = {}, {}, {}, {}, {}
        for k in pairs:
            dv[k] = _dot(at[k], douts(*k), 1, 0)
            g_ql[k] = tools.fold(_dot(jnp.where(tools.lower, da[k], 0.0).astype(BF16), tools.twice(*k, "kl"), 1, 0))
            g_qu[k] = tools.fold(_dot(jnp.where(tools.upper, da[k], 0.0).astype(BF16), tools.twice(*k, "ku"), 1, 0))
            g_kl[k] = tools.fold(_dot(jnp.where(tools.lower_t, dat[k], 0.0).astype(BF16), tools.heads(*k, "ql"), 1, 0))
            g_ku[k] = tools.fold(_dot(jnp.where(tools.upper_t, dat[k], 0.0).astype(BF16), tools.heads(*k, "qu"), 1, 0))
        dst = [dstate[p] for p in range(N_PAIR)]
        leaving = {}
        for c, p in pairs:
            leaving[c, p] = dst[p]
            dst[p] = dst[p] * pair(c, p, "el") + dst_own[c, p]
        for p in range(N_PAIR):
            dstate[p] = dst[p]
        g_kh = {}
        for k in pairs:
            c, p = k
            dstb = leaving[k].astype(BF16)
            dv2 = dv[k] + _dot(tools.heads(*k, "kh"), dstb, 1, 1)
            for sh in range(2):
                hd = 2 * p + sh
                dh_ref[c * CH:(c + 1) * CH, 512 + hd * B_DV:512 + (hd + 1) * B_DV] = dv2[sh * CH:(sh + 1) * CH].astype(BF16)
            g_kh[k] = tools.fold(_dot(tools.vals(*k), dstb, 1, 0))
        for c in reversed(range(cpb)):
            rs = slice(c * CH, (c + 1) * CH)
            for p in range(N_PAIR):
                k = (c, p)
                dq = (g_ql[k] * pair(c, p, "e1") + g_qu[k] * pair(c, p, "e2") + g_qt[k] * pair(c, p, "eb")) * (B_DK ** -0.5)
                dk = g_kl[k] * pair(c, p, "e2") + g_ku[k] * pair(c, p, "e1") + g_kh[k] * pair(c, p, "ek")
                dkh_kh = g_kh[k] * pair(c, p, "kh")
                db = (g_ql[k] * pair(c, p, "ql") - g_qu[k] * pair(c, p, "qu") + g_qt[k] * pair(c, p, "qt")
                      - g_kl[k] * pair(c, p, "kl") + g_ku[k] * pair(c, p, "ku") - dkh_kh)
                db_last = (pair(c, p, "el") * jnp.sum(leaving[c, p] * st_ref[c, p], axis=0, keepdims=True)
                           + jnp.sum(dkh_kh, axis=0, keepdims=True))
                db = jnp.where(rowi == CH - 1, db + db_last, db)
                dh_ref[rs, p * 128:(p + 1) * 128] = dq.astype(BF16)
                dh_ref[rs, B_KW + p * 128:B_KW + (p + 1) * 128] = dk.astype(BF16)
                db_scr[rs, p * 128:(p + 1) * 128] = db

        row = lax.broadcasted_iota(jnp.int32, (GB, GB), 0)
        col = lax.broadcasted_iota(jnp.int32, (GB, GB), 1)
        trit = jnp.where((row // CH == col // CH) & (col >= row), 1.0, 0.0).astype(F32)
        dlg = _dot(trit, db_scr[...], 1, 0, precision=lax.Precision.HIGHEST)
        dlogit = dlg * (_sigmoid(-logit) / GATE_TAU)
        dlb = dlogit.astype(BF16)
        dgw_ref[...] += _dot(lr, dlb, 0, 0)
        dgb_ref[...] += jnp.sum(dlogit, axis=0, keepdims=True)
        dh_ref[:, 1536:1664] = _dot(dlb, gw_ref[...], 1, 1).astype(BF16)
        dh_ref[:, 1664:2048] = jnp.zeros((GB, 384), BF16)

    rev = lambda i: nb - 1 - i
    (dh, dgw, dgb, dng), c_out = _hosted(
        body, comm, name="gla_bwd",
        out_shape=(jax.ShapeDtypeStruct(dh_in.shape, BF16), jax.ShapeDtypeStruct((128, B_KW), F32),
                   jax.ShapeDtypeStruct((1, B_KW), F32), jax.ShapeDtypeStruct((1, B_DV), F32)),
        grid=(nb,),
        in_specs=[pl.BlockSpec((GB, 2048), lambda i: (rev(i), B_BASE // 2048)),
                  pl.BlockSpec((GB, B_W), lambda i: (rev(i), 1024 // B_W)),
                  pl.BlockSpec((GB, B_W), lambda i: (rev(i), 0)),
                  pl.BlockSpec((cpb, N_PAIR, 128, 128), lambda i: (rev(i), 0, 0, 0)),
                  pl.BlockSpec((128, B_KW), lambda i: (0, 0)), pl.BlockSpec((1, B_KW), lambda i: (0, 0)),
                  pl.BlockSpec((1, B_DV), lambda i: (0, 0)), pl.BlockSpec(memory_space=pl.ANY)],
        out_specs=(pl.BlockSpec((GB, 2048), lambda i: (rev(i), B_BASE // 2048)),
                   pl.BlockSpec((128, B_KW), lambda i: (0, 0)), pl.BlockSpec((1, B_KW), lambda i: (0, 0)),
                   pl.BlockSpec((1, B_DV), lambda i: (0, 0))),
        scratch_shapes=[pltpu.VMEM((N_PAIR, 128, 128), F32), pltpu.VMEM((GB, B_KW), F32), pltpu.VMEM((GB, B_W), F32)],
        input_output_aliases={7: 0},
        compiler_params=_params(("arbitrary",)),
        inputs=(h, dy, opre, states, gw, gb, ng, dh_in))
    return dh, dgw, dgb, dng, c_out


MB = 512


def _mem_probs(q, mk, scale):
    sc = _dot(q, mk, 1, 1) * scale
    p = jnp.exp(sc - jnp.max(sc, axis=1, keepdims=True))
    return p / jnp.sum(p, axis=1, keepdims=True)


def _mem_fwd(h, r_in, mkv):
    s = h.shape[0]
    scale = M_DH ** -0.5

    def body(q_ref, mkv_ref, r_any, o_ref):
        del r_any
        for hd in range(M_HEADS):
            cs = slice(hd * M_DH, (hd + 1) * M_DH)
            p = _mem_probs(q_ref[:, cs], mkv_ref[:, cs], scale)
            o_ref[:, cs] = _dot(p.astype(BF16), mkv_ref[:, M_W + hd * M_DH:M_W + (hd + 1) * M_DH], 1, 0).astype(BF16)

    return pl.pallas_call(
        body, name="mem_fwd",
        out_shape=jax.ShapeDtypeStruct(r_in.shape, BF16),
        grid=(s // MB,),
        in_specs=[pl.BlockSpec((MB, M_W), lambda i: (i, M_BASE // M_W)),
                  pl.BlockSpec((N_MEM, 2 * M_W), lambda i: (0, 0)), pl.BlockSpec(memory_space=pl.ANY)],
        out_specs=pl.BlockSpec((MB, M_W), lambda i: (i, 1536 // M_W)),
        input_output_aliases={2: 0},
        compiler_params=_params(("arbitrary",)),
    )(h, mkv, r_in)


def _mem_bwd(h, r, dy, mkv, dh_in):
    s = h.shape[0]
    scale = M_DH ** -0.5

    def body(q_ref, z_ref, r_ref, dy_ref, mkv_ref, dh_any, dh_ref, dmkv_ref):
        del dh_any

        @pl.when(pl.program_id(0) == 0)
        def _():
            dmkv_ref[...] = jnp.zeros_like(dmkv_ref)

        z = z_ref[...].astype(F32)
        sg = _sigmoid(z)
        dyv = dy_ref[...].astype(F32)
        do_all = dyv * (z * sg)
        dh_ref[:, M_W:2 * M_W] = (dyv * r_ref[...].astype(F32) * (sg * (1.0 + z * (1.0 - sg)))).astype(BF16)
        for hd in range(M_HEADS):
            cs = slice(hd * M_DH, (hd + 1) * M_DH)
            vs = slice(M_W + hd * M_DH, M_W + (hd + 1) * M_DH)
            q = q_ref[:, cs]
            p = _mem_probs(q, mkv_ref[:, cs], scale)
            dob = do_all[:, cs].astype(BF16)
            dp = _dot(dob, mkv_ref[:, vs], 1, 1)
            ds = p * (dp - jnp.sum(p * dp, axis=1, keepdims=True))
            dsb = ds.astype(BF16)
            dh_ref[:, cs] = (_dot(dsb, mkv_ref[:, cs], 1, 0) * scale).astype(BF16)
            dmkv_ref[:, cs] += _dot(dsb, q, 0, 0) * scale
            dmkv_ref[:, vs] += _dot(p.astype(BF16), dob, 0, 0)

    return pl.pallas_call(
        body, name="mem_bwd",
        out_shape=(jax.ShapeDtypeStruct(dh_in.shape, BF16), jax.ShapeDtypeStruct((N_MEM, 2 * M_W), F32)),
        grid=(s // MB,),
        in_specs=[pl.BlockSpec((MB, M_W), lambda i: (i, M_BASE // M_W)),
                  pl.BlockSpec((MB, M_W), lambda i: (i, M_BASE // M_W + 1)),
                  pl.BlockSpec((MB, M_W), lambda i: (i, 1536 // M_W)),
                  pl.BlockSpec((MB, M_W), lambda i: (i, 1536 // M_W)),
                  pl.BlockSpec((N_MEM, 2 * M_W), lambda i: (0, 0)), pl.BlockSpec(memory_space=pl.ANY)],
        out_specs=(pl.BlockSpec((MB, 2 * M_W), lambda i: (i, M_BASE // (2 * M_W))),
                   pl.BlockSpec((N_MEM, 2 * M_W), lambda i: (0, 0))),
        input_output_aliases={5: 0},
        compiler_params=_params(("arbitrary",)),
    )(h, h, r, dy, mkv, dh_in)


OB = 256


def _outproj_ln(h, r, w_out, x, ln_g, ln_b):
    s = h.shape[0]

    def body(za_ref, zb_ref, zm_ref, r_ref, w_ref, x_ref, g_ref, b_ref, xn_ref, xh_ref, rstd_ref, y_ref):
        z = jnp.concatenate([za_ref[...], zb_ref[...], zm_ref[...]], axis=1).astype(F32)
        y = (r_ref[...].astype(F32) * (z * _sigmoid(z))).astype(BF16)
        y_ref[...] = y
        u = ALPHA * x_ref[...] + _dot(y, w_ref[...], 1, 0)
        mu = jnp.mean(u, axis=1, keepdims=True)
        uc = u - mu
        rstd = lax.rsqrt(jnp.mean(uc * uc, axis=1, keepdims=True) + LN_EPS)
        xh = uc * rstd
        xh_ref[...] = xh
        rstd_ref[...] = rstd
        xn_ref[...] = xh * g_ref[...] + b_ref[...]

    row = lambda w, c: pl.BlockSpec((OB, w), lambda i: (i, c))
    vec = pl.BlockSpec((1, D), lambda i: (0, 0))
    return pl.pallas_call(
        body, name="outproj_ln",
        out_shape=(jax.ShapeDtypeStruct((s, D), F32), jax.ShapeDtypeStruct((s, D), F32),
                   jax.ShapeDtypeStruct((s, 1), F32), jax.ShapeDtypeStruct((s, D), BF16)),
        grid=(s // OB,),
        in_specs=[row(A_W, A_Z // A_W), row(B_W, (B_BASE + 1024) // B_W), row(M_W, (M_BASE + M_W) // M_W), row(D, 0),
                  pl.BlockSpec((D, D), lambda i: (0, 0)), row(D, 0), vec, vec],
        out_specs=(row(D, 0), row(D, 0), pl.BlockSpec((OB, 1), lambda i: (i, 0)), row(D, 0)),
        compiler_params=_params(("arbitrary",), 56),
    )(h, h, h, r, w_out, x, ln_g, ln_b)


def _ln_bwd(g, xh, rstd, ln_g):
    s = g.shape[0]

    def body(g_ref, xh_ref, rstd_ref, lg_ref, du_ref, dg_ref, db_ref):
        @pl.when(pl.program_id(0) == 0)
        def _():
            dg_ref[...] = jnp.zeros_like(dg_ref)
            db_ref[...] = jnp.zeros_like(db_ref)

        gv, xh = g_ref[...], xh_ref[...]
        dg_ref[...] += jnp.sum(gv * xh, axis=0, keepdims=True)
        db_ref[...] += jnp.sum(gv, axis=0, keepdims=True)
        dxh = gv * lg_ref[...]
        du_ref[...] = rstd_ref[...] * (dxh - jnp.mean(dxh, axis=1, keepdims=True)
                                       - xh * jnp.mean(dxh * xh, axis=1, keepdims=True))

    row = pl.BlockSpec((OB, D), lambda i: (i, 0))
    vec = pl.BlockSpec((1, D), lambda i: (0, 0))
    return pl.pallas_call(
        body, name="ln_bwd",
        out_shape=(jax.ShapeDtypeStruct((s, D), F32), jax.ShapeDtypeStruct((1, D), F32), jax.ShapeDtypeStruct((1, D), F32)),
        grid=(s // OB,),
        in_specs=[row, row, pl.BlockSpec((OB, 1), lambda i: (i, 0)), vec],
        out_specs=(row, vec, vec),
        compiler_params=_params(("arbitrary",)),
    )(g, xh, rstd, ln_g)


def _loss_grad(y, target):
    s = y.shape[0]

    def body(y_ref, t_ref, l_ref, dy_ref):
        @pl.when(pl.program_id(0) == 0)
        def _():
            l_ref[...] = jnp.zeros_like(l_ref)

        e = y_ref[...] - t_ref[...]
        dy_ref[...] = e / D
        l_ref[...] += 0.5 * jnp.sum(jnp.mean(e * e, axis=1, keepdims=True))

    row = pl.BlockSpec((OB, D), lambda i: (i, 0))
    return pl.pallas_call(
        body, name="loss_grad",
        out_shape=(jax.ShapeDtypeStruct((1, 128), F32), jax.ShapeDtypeStruct((s, D), F32)),
        grid=(s // OB,),
        in_specs=[row, row],
        out_specs=(pl.BlockSpec((1, 128), lambda i: (0, 0)), row),
        compiler_params=_params(("arbitrary",)),
    )(y, target)


class _LocalWeights:
    def __init__(self, w_in_p, w_out_f, w_kv_f):
        self.w = list(zip(w_in_p, w_out_f, w_kv_f))
        self.depth = len(self.w)
        self.grads = [dict() for _ in self.w]

    def weights(self, l):
        return self.w[l]

    def host(self, where, l, payload=None):
        if payload is not None:
            self.grads[l][where] = payload
        return None

    def landed(self, where, l, outs):
        pass


def _shards_to_padded(raw):
    sh = raw.shape[2]
    parts = []
    for j in range(N_DEV):
        lo, hi = j * sh, (j + 1) * sh
        if lo < NAT_SPLIT < hi:
            parts += [raw[j][:, :NAT_SPLIT - lo], jnp.zeros((D, H_PAD), raw.dtype), raw[j][:, NAT_SPLIT - lo:]]
        else:
            parts.append(raw[j])
            if hi == NAT_SPLIT:
                parts.append(jnp.zeros((D, H_PAD), raw.dtype))
    return jnp.concatenate(parts, axis=1)


def _padded_to_shards(w):
    sh = IN_W // N_DEV
    pos = lambda c: c if c < NAT_SPLIT else c + H_PAD
    blocks = []
    for j in range(N_DEV):
        lo, hi = j * sh, (j + 1) * sh
        if lo < NAT_SPLIT < hi:
            blocks.append(jnp.concatenate([w[:, lo:NAT_SPLIT], w[:, pos(NAT_SPLIT):pos(hi - 1) + 1]], axis=1))
        else:
            blocks.append(w[:, pos(lo):pos(lo) + sh])
    return jnp.stack(blocks)


_PART_A = (0, 1, 2, 4, 7)
_PART_B = (3, 5, 6)


class _Fsdp:
    def __init__(self, w_in, w_out, w_kv, extra):
        self.sh = (w_in, w_out, w_kv)
        self.depth = w_in.shape[0]
        self.raw = [dict() for _ in range(self.depth)]
        self.recv = [dict() for _ in range(self.depth)]
        g_in, g_out, g_kv, *self.extra = _exchange([w_in[0], w_out[0], w_kv[0]] + list(extra), False, "gather_layer0")
        self.raw[0] = dict(w_in=g_in, w_out=g_out, w_kv=g_kv)

    def weights(self, l):
        raw = self.raw[l]
        return _shards_to_padded(raw["w_in"]), raw["w_out"].reshape(D, D), raw["w_kv"].reshape(D, 2 * M_W)

    def host(self, where, l, payload=None):
        w_in, w_out, w_kv = self.sh
        if where == "in_proj" and l + 1 < self.depth:
            return _Gather([w_in[l + 1], w_out[l + 1], w_kv[l + 1]])
        if where == "d_w_in":
            d_wout, d_wkv = payload
            return _Comm([d_wout.reshape(N_DEV, D // N_DEV, D), d_wkv.reshape(N_DEV, D // N_DEV, 2 * M_W)], True)
        if where == "d_x":
            self.blocks = _padded_to_shards(payload)
            return _Comm([self.blocks], True, relations=_PART_A if l > 0 else tuple(range(N_DEV)))
        if where == "attn_bwd" and l + 1 < self.depth:
            return _Comm([self.blocks], True, relations=_PART_B, into=[self.recv[l + 1]["w_in"]])
        return None

    def landed(self, where, l, outs):
        if where == "in_proj" and outs:
            self.raw[l + 1] = dict(zip(("w_in", "w_out", "w_kv"), outs))
        elif where == "d_w_in":
            self.recv[l]["w_out"], self.recv[l]["w_kv"] = outs
        elif where == "d_x":
            self.recv[l]["w_in"] = outs[0]
        elif where == "attn_bwd" and outs:
            self.recv[l + 1]["w_in"] = outs[0]


def _local_step(x, mem, target, pipe, rel, gate_w, gate_b, norm_g, ln_g, ln_b):
    depth = pipe.depth
    s = x.shape[0]
    saved = []
    xl = x
    for l in range(depth):
        w_in_p, w_out_f, w_kv_f = pipe.weights(l)
        hmat, landed = _mm(xl, w_in_p, out_dtype=BF16, tm=1024, tn=512, tk=D, name="in_proj", comm=pipe.host("in_proj", l))
        pipe.landed("in_proj", l, landed)
        mkv, _ = _mm(mem, w_kv_f, out_dtype=BF16, tm=N_MEM, tn=1024, tk=D, name="mem_kv")
        bias = _band_bias(rel[l])
        gw = jnp.zeros((128, B_KW), F32).at[:GATE_RANK].set(gate_w[l]).astype(BF16)
        gb, ng = gate_b[l][None, :], norm_g[l][None, :]
        r, lse, _ = _attn_fwd(hmat, lax.empty((s, D), BF16), bias)
        r, opre, states, _ = _gla_fwd(hmat, r, gw, gb, ng)
        r = _mem_fwd(hmat, r, mkv)
        xn, xh, rstd, y = _outproj_ln(hmat, r, w_out_f, xl, ln_g[l][None, :], ln_b[l][None, :])
        saved.append(dict(x=xl, h=hmat, mkv=mkv, bias=bias, gw=gw, gb=gb, ng=ng, r=r, lse=lse, opre=opre,
                          states=states, xh=xh, rstd=rstd, y=y, w_in_p=w_in_p, w_out_f=w_out_f))
        xl = xn
    loss, g = _loss_grad(xl, target)

    grads = [None] * depth
    for l in reversed(range(depth)):
        sv = saved[l]
        du, d_lng, d_lnb = _ln_bwd(g, sv["xh"], sv["rstd"], ln_g[l][None, :])
        d_wout, _ = _mm(sv["y"], du, ta=True, out_dtype=BF16, tm=1024, tn=1024, tk=1024, name="d_w_out")
        dy, _ = _mm(du, sv["w_out_f"], tb=True, out_dtype=BF16, tm=1024, tn=1024, tk=D, name="d_y")
        dh, dbias, landed = _attn_bwd(sv["h"], sv["r"], dy, sv["lse"], sv["bias"], lax.empty((s, H_W), BF16),
                                      pipe.host("attn_bwd", l))
        pipe.landed("attn_bwd", l, landed)
        dh, d_gw, d_gb, d_ng, _ = _gla_bwd(sv["h"], dy, sv["opre"], sv["states"], sv["gw"], sv["gb"], sv["ng"], dh)
        dh, d_mkv = _mem_bwd(sv["h"], sv["r"], dy, sv["mkv"], dh)
        d_wkv, _ = _mm(mem, d_mkv, ta=True, out_dtype=BF16, tm=1024, tn=1024, tk=N_MEM, name="d_w_kv")
        d_win, landed = _mm(sv["x"], dh, ta=True, out_dtype=BF16, tm=1024, tn=1792, tk=1024, name="d_w_in",
                            comm=pipe.host("d_w_in", l, (d_wout, d_wkv)))
        pipe.landed("d_w_in", l, landed)
        g, landed = _mm(dh, sv["w_in_p"], tb=True, out_dtype=F32, tm=1024, tn=1024, tk=1792, name="d_x",
                        adds=((du, ALPHA),), comm=pipe.host("d_x", l, d_win))
        pipe.landed("d_x", l, landed)
        grads[l] = dict(rel=_bias_grad(dbias), gate_w=d_gw[:GATE_RANK], gate_b=d_gb[0], norm_g=d_ng[0],
                        ln_g=d_lng[0], ln_b=d_lnb[0])
    return loss, g, grads


def _adamw(parts, w, m, v, rows_per_step, name):
    depth, rows, cols = w.shape
    n = parts[0].shape[0]
    tr = min(rows_per_step, rows)
    assert rows % tr == 0 and len(parts) == depth

    def body(*refs):
        p_refs = refs[:depth]
        w_ref, m_ref, v_ref, g_ref, d_ref, nm_ref, nv_ref = refs[depth:]
        for l in range(depth):
            @pl.when(pl.program_id(0) == l)
            def _(p_ref=p_refs[l]):
                g = p_ref[0].astype(F32)
                for j in range(1, n):
                    g = g + p_ref[j].astype(F32)
                nm = ADAM_B1 * m_ref[...] + (1.0 - ADAM_B1) * g
                nv = ADAM_B2 * v_ref[...] + (1.0 - ADAM_B2) * (g * g)
                m_hat = nm / (1.0 - ADAM_B1 ** ADAM_STEP)
                v_hat = nv / (1.0 - ADAM_B2 ** ADAM_STEP)
                g_ref[...] = g
                nm_ref[...] = nm
                nv_ref[...] = nv
                d_ref[...] = -ADAM_LR * (m_hat / (jnp.sqrt(v_hat) + ADAM_EPS) + ADAM_WD * w_ref[...])

    def part_spec(l):
        return pl.BlockSpec((n, tr, cols), lambda ll, i: (0, jnp.where(ll == l, i, 0), 0))

    blk = pl.BlockSpec((None, tr, cols), lambda ll, i: (ll, i, 0))
    shape = jax.ShapeDtypeStruct((depth, rows, cols), F32)
    return pl.pallas_call(
        body, name=name,
        out_shape=(shape, shape, shape, shape),
        grid=(depth, rows // tr),
        in_specs=[part_spec(l) for l in range(depth)] + [blk, blk, blk],
        out_specs=(blk, blk, blk, blk),
        compiler_params=_params(("arbitrary", "arbitrary")),
    )(*parts, w, m, v)


SMALL = (("rel", A_HEADS * (2 * MAX_REL + 1)), ("gate_w", GATE_RANK * B_KW), ("gate_b", B_KW), ("norm_g", B_DV),
         ("ln_g", D), ("ln_b", D))


def _pack_small(parts, depth):
    rows = []
    for name, size in SMALL:
        flat = parts[name].reshape(depth * size).astype(F32)
        rows.append(jnp.pad(flat, (0, -(depth * size) % 128)).reshape(-1, 128))
    packed = jnp.concatenate(rows, axis=0)
    return jnp.pad(packed, ((0, -packed.shape[0] % 8), (0, 0)))


def _unpack_small(packed, depth, shapes):
    out, row = {}, 0
    for name, size in SMALL:
        nrow = -(-(depth * size) // 128)
        out[name] = packed[row:row + nrow].reshape(-1)[:depth * size].reshape(shapes[name])
        row += nrow
    return out


def kernel(x, mem, w_in, a_rel_bias, b_gate_w, b_gate_b, b_norm_g, w_mem_kv, w_out, ln_g, ln_b, loss_target, m_w_in, m_a_rel_bias, m_b_gate_w, m_b_gate_b, m_b_norm_g, m_w_mem_kv, m_w_out, m_ln_g, m_ln_b, v_w_in, v_a_rel_bias, v_b_gate_w, v_b_gate_b, v_b_norm_g, v_w_mem_kv, v_w_out, v_ln_g, v_ln_b):
    depth = w_in.shape[0]
    sh_in = w_in.shape[2]
    sh_gw = b_gate_w.shape[2]
    me = 4 * lax.axis_index("x") + 2 * lax.axis_index("y") + lax.axis_index("c")

    pipe = _Fsdp(w_in.astype(BF16), w_out.astype(BF16), w_mem_kv.astype(BF16), [b_gate_w])
    gate_w_full = jnp.transpose(pipe.extra[0], (1, 2, 0, 3)).reshape(depth, GATE_RANK, N_DEV * sh_gw)
    loss_dev, dx, grads = _local_step(x[0], mem[0], loss_target[0], pipe,
                                      a_rel_bias, gate_w_full, b_gate_b, b_norm_g, ln_g, ln_b)
    loss = lax.psum(loss_dev[0, 0], ("x", "y", "c"))

    recv = lambda n: [pipe.recv[l][n] for l in range(depth)]
    big = {"w_in": _adamw(recv("w_in"), w_in, m_w_in, v_w_in, 128, "adamw_w_in"),
           "w_out": _adamw(recv("w_out"), w_out, m_w_out, v_w_out, 64, "adamw_w_out"),
           "w_kv": _adamw(recv("w_kv"), w_mem_kv, m_w_mem_kv, v_w_mem_kv, 128, "adamw_w_kv")}

    shapes = {"rel": a_rel_bias.shape, "gate_w": (depth, GATE_RANK, N_DEV * sh_gw), "gate_b": b_gate_b.shape,
              "norm_g": b_norm_g.shape, "ln_g": ln_g.shape, "ln_b": ln_b.shape}
    part = _pack_small({n: jnp.stack([grads[l][n] for l in range(depth)]) for n, _ in SMALL}, depth)
    (all_parts,) = _exchange([part], False, "gather_small")
    zeros_gw = jnp.zeros(shapes["gate_w"], F32)
    w_s = _pack_small(dict(rel=a_rel_bias, gate_w=zeros_gw, gate_b=b_gate_b, norm_g=b_norm_g, ln_g=ln_g, ln_b=ln_b), depth)
    m_s = _pack_small(dict(rel=m_a_rel_bias, gate_w=zeros_gw, gate_b=m_b_gate_b, norm_g=m_b_norm_g, ln_g=m_ln_g, ln_b=m_ln_b), depth)
    v_s = _pack_small(dict(rel=v_a_rel_bias, gate_w=zeros_gw, gate_b=v_b_gate_b, norm_g=v_b_norm_g, ln_g=v_ln_g, ln_b=v_ln_b), depth)
    small = [_unpack_small(t[0], depth, shapes)
             for t in _adamw([all_parts], w_s[None], m_s[None], v_s[None], all_parts.shape[1], "adamw_small")]
    gw_grad = lax.dynamic_slice_in_dim(small[0]["gate_w"], me * sh_gw, sh_gw, axis=2).reshape(1, depth * GATE_RANK, sh_gw)
    flat = lambda t: t.reshape(1, depth * GATE_RANK, sh_gw)
    gw_res = [t.reshape(depth, GATE_RANK, sh_gw)
              for t in _adamw([gw_grad], flat(b_gate_w), flat(m_b_gate_w), flat(v_b_gate_w), depth * GATE_RANK, "adamw_gate_w")]

    def leaves(t):
        return (big["w_in"][t], small[t]["rel"], gw_res[t], small[t]["gate_b"], small[t]["norm_g"],
                big["w_kv"][t], big["w_out"][t], small[t]["ln_g"], small[t]["ln_b"])

    return (loss, dx[None]) + leaves(0) + leaves(1) + leaves(2) + leaves(3)
```

```python
import functools
import math

import numpy as np
import jax
import jax.numpy as jnp
from jax import lax
from jax.experimental import pallas as pl
from jax.experimental.pallas import tpu as pltpu

F32 = jnp.float32
BF16 = jnp.bfloat16

N_DEV = 8
D = 2048
CH = 64
LEFT = 8
MAX_REL = 128
N_MEM = 256
A_HEADS, A_DH, A_W = 8, 128, 1024
B_HEADS, B_DK, B_DV, B_KW, B_W = 4, 64, 128, 256, 512
GATE_RANK, GATE_TAU = 16, 16.0
M_HEADS, M_DH, M_W = 4, 128, 512
IN_W = 6672
NAT_SPLIT = 5648
H_W = 7168
H_PAD = H_W - IN_W
A_Q, A_K, A_V, A_Z = 0, 1024, 2048, 3072
B_BASE = 4096
M_BASE = 6144
ALPHA = (2.0 * 4) ** 0.25
LN_EPS = 1e-5
RMS_EPS = 1e-6
NEG = -1e30
QB = 256
KB = 3 * QB
ADAM_LR, ADAM_B1, ADAM_B2, ADAM_EPS, ADAM_WD, ADAM_STEP = 0.001, 0.9, 0.999, 1e-08, 0.01, 10
VMEM_MB = 1024 * 1024


def _params(sem, vmem_mb=48):
    return pltpu.CompilerParams(dimension_semantics=sem, vmem_limit_bytes=vmem_mb * VMEM_MB)


def _sigmoid(x):
    return 1.0 / (1.0 + jnp.exp(-x))


def _dot(a, b, ca, cb, precision=None):
    return lax.dot_general(a, b, (((ca,), (cb,)), ((), ())), preferred_element_type=F32, precision=precision)


MESH = pl.DeviceIdType.MESH


class _Comm:
    def __init__(self, xs, scatter, relations=tuple(range(N_DEV)), into=None):
        self.xs, self.scatter, self.n = list(xs), scatter, len(xs)
        self.relations, self.into = tuple(relations), list(into or [])
        self.out_shape = [jax.ShapeDtypeStruct((N_DEV,) + (x.shape[1:] if scatter else x.shape), x.dtype) for x in xs]
        self.specs = [pl.BlockSpec(memory_space=pltpu.HBM)] * self.n
        self.scratch = [pltpu.SemaphoreType.DMA((self.n, N_DEV)), pltpu.SemaphoreType.DMA((self.n, N_DEV)),
                        pltpu.SemaphoreType.DMA((self.n,))]

    def _copies(self, x_refs, o_refs, sems):
        send_sems, recv_sems, local_sems = sems
        mx, my, mc = lax.axis_index("x"), lax.axis_index("y"), lax.axis_index("c")
        me = 4 * mx + 2 * my + mc
        own, sends, arrivals = [], [], []
        for k in self.relations:
            if k == 0:
                own = [pltpu.make_async_copy(x_refs[a].at[me] if self.scatter else x_refs[a], o_refs[a].at[me],
                                             local_sems.at[a]) for a in range(self.n)]
                continue
            px = 1 - mx if k & 4 else mx
            py = 1 - my if k & 2 else my
            pc = 1 - mc if k & 1 else mc
            idx = 4 * px + 2 * py + pc
            for a in range(self.n):
                src = x_refs[a].at[idx] if self.scatter else x_refs[a]
                for dst, group in ((o_refs[a].at[me], sends), (o_refs[a].at[idx], arrivals)):
                    group.append(pltpu.make_async_remote_copy(
                        src_ref=src, dst_ref=dst, send_sem=send_sems.at[a, k], recv_sem=recv_sems.at[a, k],
                        device_id=(px, py, pc), device_id_type=MESH))
        return own, sends, arrivals

    def start(self, x_refs, o_refs, sems):
        own, sends, _ = self._copies(x_refs, o_refs, sems)
        for cp in own + sends:
            cp.start()

    def middle(self, x_refs, o_refs, sems):
        pass

    def finish(self, x_refs, o_refs, sems):
        own, sends, arrivals = self._copies(x_refs, o_refs, sems)
        for cp in sends:
            cp.wait_send()
        for cp in arrivals:
            cp.wait_recv()
        for cp in own:
            cp.wait()


class _Gather:
    into = ()

    def __init__(self, xs):
        self.xs, self.n = list(xs), len(xs)
        self.out_shape = [jax.ShapeDtypeStruct((N_DEV,) + x.shape, x.dtype) for x in xs]
        self.specs = [pl.BlockSpec(memory_space=pltpu.HBM)] * self.n
        self.scratch = [pltpu.SemaphoreType.DMA((self.n, N_DEV - 1)), pltpu.SemaphoreType.DMA((self.n, N_DEV - 1)),
                        pltpu.SemaphoreType.DMA((self.n,))]

    def _copies(self, x_refs, o_refs, sems):
        send_sems, recv_sems, local_sems = sems
        mx, my, mc = lax.axis_index("x"), lax.axis_index("y"), lax.axis_index("c")
        idx = lambda px, py, pc: 4 * px + 2 * py + pc
        me, sibling = (mx, my, mc), (mx, my, 1 - mc)
        chips = [(mx, 1 - my), (1 - mx, my), (1 - mx, 1 - my)]

        def copy(a, k, src, slot, to):
            return pltpu.make_async_remote_copy(
                src_ref=src, dst_ref=o_refs[a].at[idx(*slot)], send_sem=send_sems.at[a, k], recv_sem=recv_sems.at[a, k],
                device_id=to, device_id_type=MESH)

        c = dict(own=[], first=[], passed=[], ici_in=[], late_in=[])
        for a in range(self.n):
            x = x_refs[a]
            c["own"].append(pltpu.make_async_copy(x, o_refs[a].at[idx(*me)], local_sems.at[a]))
            c["first"].append(copy(a, 0, x, me, sibling))
            c["late_in"].append(copy(a, 0, x, sibling, sibling))
            for j, chip in enumerate(chips):
                c["first"].append(copy(a, 1 + j, x, me, (*chip, mc)))
                c["ici_in"].append(copy(a, 1 + j, x, (*chip, mc), (*chip, mc)))
                c["passed"].append(copy(a, 4 + j, o_refs[a].at[idx(*chip, mc)], (*chip, mc), sibling))
                c["late_in"].append(copy(a, 4 + j, x, (*chip, 1 - mc), sibling))
        return c

    def start(self, x_refs, o_refs, sems):
        c = self._copies(x_refs, o_refs, sems)
        for cp in c["own"] + c["first"]:
            cp.start()

    def middle(self, x_refs, o_refs, sems):
        c = self._copies(x_refs, o_refs, sems)
        for arrived, onward in zip(c["ici_in"], c["passed"]):
            arrived.wait_recv()
            onward.start()

    def finish(self, x_refs, o_refs, sems):
        c = self._copies(x_refs, o_refs, sems)
        for cp in c["first"] + c["passed"]:
            cp.wait_send()
        for cp in c["late_in"]:
            cp.wait_recv()
        for cp in c["own"]:
            cp.wait()


def _exchange(xs, scatter, name):
    comm = _Comm(xs, scatter) if scatter else _Gather(xs)

    def body(*refs):
        x_refs, o_refs, sems = refs[:comm.n], refs[comm.n:2 * comm.n], refs[2 * comm.n:]
        comm.start(x_refs, o_refs, sems)
        comm.middle(x_refs, o_refs, sems)
        comm.finish(x_refs, o_refs, sems)

    return pl.pallas_call(body, name=name, out_shape=tuple(comm.out_shape), in_specs=comm.specs,
                          out_specs=tuple(comm.specs), scratch_shapes=comm.scratch)(*comm.xs)


def _hosted(body, comm, *, name, grid, in_specs, out_specs, out_shape, scratch_shapes, compiler_params, inputs,
            input_output_aliases=None):
    out_specs, out_shape = tuple(out_specs), tuple(out_shape)
    aliases = input_output_aliases or {}
    if comm is None:
        outs = pl.pallas_call(body, name=name, grid=grid, in_specs=list(in_specs), out_specs=out_specs, out_shape=out_shape,
                              scratch_shapes=list(scratch_shapes), compiler_params=compiler_params,
                              input_output_aliases=aliases)(*inputs)
        return tuple(outs), ()
    ni, no, ns, nc = len(in_specs), len(out_specs), len(scratch_shapes), comm.n
    n_into = len(comm.into)
    aliases = {**aliases, **{ni + nc + a: no + a for a in range(n_into)}}

    def wrapped(*refs):
        ins, c_in = refs[:ni], refs[ni:ni + nc]
        refs = refs[ni + nc + n_into:]
        outs, c_out = refs[:no], refs[no:no + nc]
        scr, sems = refs[no + nc:no + nc + ns], refs[no + nc + ns:]
        first = functools.reduce(jnp.logical_and, [pl.program_id(d) == 0 for d in range(len(grid))])
        last = functools.reduce(jnp.logical_and, [pl.program_id(d) == grid[d] - 1 for d in range(len(grid))])

        step = functools.reduce(lambda acc, d: acc * grid[d] + pl.program_id(d), range(len(grid)), 0)
        n_steps = math.prod(grid)

        @pl.when(first)
        def _():
            comm.start(c_in, c_out, sems)

        body(*ins, *outs, *scr)

        @pl.when(step == max(3 * n_steps // 4, 1) - 1)
        def _():
            comm.middle(c_in, c_out, sems)

        @pl.when(last)
        def _():
            comm.finish(c_in, c_out, sems)

    params = pltpu.CompilerParams(dimension_semantics=("arbitrary",) * len(grid),
                                  vmem_limit_bytes=compiler_params.vmem_limit_bytes)
    outs = pl.pallas_call(wrapped, name=name, grid=grid, in_specs=list(in_specs) + comm.specs + comm.specs[:n_into],
                          out_specs=out_specs + tuple(comm.specs), out_shape=out_shape + tuple(comm.out_shape),
                          scratch_shapes=list(scratch_shapes) + comm.scratch, compiler_params=params,
                          input_output_aliases=aliases)(*inputs, *comm.xs, *comm.into)
    return tuple(outs[:no]), tuple(outs[no:])


def _mm(a, b, *, ta=False, tb=False, out_dtype, tm, tn, tk, name, adds=(), vmem_mb=48, comm=None):
    m = a.shape[1] if ta else a.shape[0]
    k = a.shape[0] if ta else a.shape[1]
    n = b.shape[0] if tb else b.shape[1]
    assert k == (b.shape[1] if tb else b.shape[0])
    tm, tn, tk = min(tm, m), min(tn, n), min(tk, k)
    assert m % tm == 0 and n % tn == 0 and k % tk == 0, (name, m, n, k)
    nk = k // tk
    n_add = len(adds)
    scales = [s for _, s in adds]

    def body(a_ref, b_ref, *rest):
        add_refs, o_ref = rest[:n_add], rest[n_add]
        kk = pl.program_id(2)

        def product():
            return _dot(a_ref[...].astype(BF16), b_ref[...].astype(BF16), 0 if ta else 1, 1 if tb else 0)

        def finish(r):
            for ref, s in zip(add_refs, scales):
                r = r + s * ref[...].astype(F32)
            o_ref[...] = r.astype(out_dtype)

        if nk == 1:
            finish(product())
            return
        acc_ref = rest[n_add + 1]

        @pl.when(kk == 0)
        def _():
            acc_ref[...] = product()

        if nk > 2:
            @pl.when((kk > 0) & (kk < nk - 1))
            def _():
                acc_ref[...] += product()

        @pl.when(kk == nk - 1)
        def _():
            finish(acc_ref[...] + product())

    a_spec = pl.BlockSpec((tk, tm), lambda i, j, kk: (kk, i)) if ta else pl.BlockSpec((tm, tk), lambda i, j, kk: (i, kk))
    b_spec = pl.BlockSpec((tn, tk), lambda i, j, kk: (j, kk)) if tb else pl.BlockSpec((tk, tn), lambda i, j, kk: (kk, j))
    add_specs = [pl.BlockSpec((tm, tn), lambda i, j, kk: (i, j)) for _ in adds]
    (out,), c_out = _hosted(
        body, comm, name=name,
        out_shape=[jax.ShapeDtypeStruct((m, n), out_dtype)],
        grid=(m // tm, n // tn, nk),
        in_specs=[a_spec, b_spec] + add_specs,
        out_specs=[pl.BlockSpec((tm, tn), lambda i, j, kk: (i, j))],
        scratch_shapes=[pltpu.VMEM((tm, tn), F32)] if nk > 1 else [],
        compiler_params=_params(("parallel", "parallel", "arbitrary"), vmem_mb),
        inputs=(a, b, *[x for x, _ in adds]))
    return out, c_out


NKB = KB // QB
LEAD = NKB - 1


def _band_bias(table):
    i = np.arange(QB)[:, None]
    j = np.arange(KB)[None, :]
    qc = i // CH + LEAD * QB // CH
    kc = j // CH
    valid = (kc <= qc) & (kc >= qc - LEFT)
    n = QB + KB
    c = np.arange(n)
    onehot = np.zeros((2 * MAX_REL + 1, n), np.float32)
    onehot[np.clip(LEAD * QB - (c - (QB - 1)), -MAX_REL, MAX_REL) + MAX_REL, c] = 1.0
    row = jnp.dot(table.astype(F32), jnp.asarray(onehot), precision=lax.Precision.HIGHEST)
    flow = jnp.tile(row, (1, QB))[:, :QB * (n - 1)].reshape(table.shape[0], QB, n - 1)
    return jnp.where(valid[None], flow[:, :, QB - 1:], NEG)


def _bias_grad(dbias):
    h, n = dbias.shape[0], KB + 1
    flat = jnp.pad(dbias.reshape(h, QB * KB), ((0, 0), (0, -(QB * KB) % n)))
    diag = flat.reshape(h, -1, n).sum(axis=1)
    c = np.arange(n)
    jm = np.where(c < LEAD * QB + CH, c, c - n)
    didx = np.clip(LEAD * QB - jm, -MAX_REL, MAX_REL) + MAX_REL
    onehot = np.zeros((n, 2 * MAX_REL + 1), np.float32)
    onehot[c, didx] = 1.0
    return jnp.dot(diag, jnp.asarray(onehot), precision=lax.Precision.HIGHEST)


def _attn_scores(q, k_refs, cs, bias_h, m, masked, scale):
    parts = []
    for t in range(NKB):
        sc = _dot(q, k_refs[t][:, cs], 1, 1) * scale
        parts.append(jnp.where(m + t - LEAD >= 0, sc, NEG) if masked else sc)
    return jnp.concatenate(parts, axis=1) + bias_h


def _attn_fwd(h, r_in, bias, comm=None):
    s = h.shape[0]
    nq = s // QB
    scale = A_DH ** -0.5

    def body(q_ref, *rest):
        k_refs, v_refs = rest[:NKB], rest[NKB:2 * NKB]
        bias_hbm, r_any, o_ref, lse_ref, bias_ref = rest[2 * NKB:]
        del r_any
        m = pl.program_id(0)

        @pl.when(m == 0)
        def _():
            pltpu.sync_copy(bias_hbm, bias_ref)

        def scores(hd, masked):
            cs = slice(hd * A_DH, (hd + 1) * A_DH)
            return _attn_scores(q_ref[:, cs], k_refs, cs, bias_ref[hd], m, masked, scale)

        def step(masked):
            sc_next = scores(0, masked)
            for hd in range(A_HEADS):
                cs = slice(hd * A_DH, (hd + 1) * A_DH)
                sc = sc_next
                if hd + 1 < A_HEADS:
                    sc_next = scores(hd + 1, masked)
                mx = jnp.max(sc, axis=1, keepdims=True)
                p = jnp.exp(sc - mx)
                l = jnp.sum(p, axis=1, keepdims=True)
                pb = p.astype(BF16)
                o = _dot(pb[:, :QB], v_refs[0][:, cs], 1, 0)
                for t in range(1, NKB):
                    o += _dot(pb[:, t * QB:(t + 1) * QB], v_refs[t][:, cs], 1, 0)
                o_ref[:, cs] = (o / l).astype(BF16)
                lse_ref[:, hd:hd + 1] = mx + jnp.log(l)

        pl.when(m < LEAD)(functools.partial(step, True))
        pl.when(m >= LEAD)(functools.partial(step, False))

    def kv_spec(col, t):
        return pl.BlockSpec((QB, A_W), lambda m: (jnp.maximum(m + t - LEAD, 0), col))

    (r, lse), c_out = _hosted(
        body, comm, name="attn_fwd",
        out_shape=(jax.ShapeDtypeStruct(r_in.shape, BF16), jax.ShapeDtypeStruct((s, A_HEADS), F32)),
        grid=(nq,),
        in_specs=[pl.BlockSpec((QB, A_W), lambda m: (m, 0))]
        + [kv_spec(1, t) for t in range(NKB)] + [kv_spec(2, t) for t in range(NKB)]
        + [pl.BlockSpec(memory_space=pl.ANY), pl.BlockSpec(memory_space=pl.ANY)],
        out_specs=(pl.BlockSpec((QB, A_W), lambda m: (m, 0)), pl.BlockSpec((QB, A_HEADS), lambda m: (m, 0))),
        scratch_shapes=[pltpu.VMEM((A_HEADS, QB, KB), F32)],
        input_output_aliases={2 * NKB + 2: 0},
        compiler_params=_params(("arbitrary",)),
        inputs=(h,) * (2 * NKB + 1) + (bias, r_in))
    return r, lse, c_out


def _attn_bwd(h, r, dy, lse, bias, dh_in, comm=None):
    s = h.shape[0]
    nq = s // QB
    scale = A_DH ** -0.5

    def body(q_ref, *rest):
        k_refs, v_refs = rest[:NKB], rest[NKB:2 * NKB]
        (z_ref, r_ref, dy_ref, lse_ref, zl_ref, rl_ref, dyl_ref, bias_hbm, dh_any,
         dh_ref, dbias_hbm, dk_acc, dv_acc, dq_ring, bias_ref, dbias_ref) = rest[2 * NKB:]
        del dh_any
        m = pl.program_id(0)

        def slot(b):
            return pl.multiple_of(lax.rem(b + NKB, NKB) * QB, QB)

        @pl.when(m == 0)
        def _():
            dk_acc[...] = jnp.zeros_like(dk_acc)
            dv_acc[...] = jnp.zeros_like(dv_acc)
            dq_ring[...] = jnp.zeros_like(dq_ring)
            dbias_ref[...] = jnp.zeros_like(dbias_ref)
            pltpu.sync_copy(bias_hbm, bias_ref)

        def step(masked):
            z = z_ref[...].astype(F32)
            do_all = dy_ref[...].astype(F32) * (z * _sigmoid(z))
            o_all = r_ref[...].astype(F32)

            def products(hd):
                cs = slice(hd * A_DH, (hd + 1) * A_DH)
                dob = do_all[:, cs].astype(BF16)
                sc = _attn_scores(q_ref[:, cs], k_refs, cs, bias_ref[hd], m, masked, scale)
                return sc, jnp.concatenate([_dot(dob, v_refs[t][:, cs], 1, 1) for t in range(NKB)], axis=1)

            ahead = products(0)
            for hd in range(A_HEADS):
                cs = slice(hd * A_DH, (hd + 1) * A_DH)
                q = q_ref[:, cs]
                do = do_all[:, cs]
                dob = do.astype(BF16)
                delta = jnp.sum(do * o_all[:, cs], axis=1, keepdims=True)
                sc, dp = ahead
                if hd + 1 < A_HEADS:
                    ahead = products(hd + 1)
                p = jnp.exp(sc - lse_ref[:, hd:hd + 1])
                ds = p * (dp - delta)
                dbias_ref[hd] += ds
                pb, dsb = p.astype(BF16), ds.astype(BF16)
                dq = jnp.zeros((QB, A_DH), F32)
                for t in range(NKB):
                    ts = slice(t * QB, (t + 1) * QB)
                    rows = pl.ds(slot(m - LEAD + t), QB)
                    dq += _dot(dsb[:, ts], k_refs[t][:, cs], 1, 0)
                    dk_acc[rows, cs] += _dot(dsb[:, ts], q, 0, 0) * scale
                    dv_acc[rows, cs] += _dot(pb[:, ts], dob, 0, 0)
                dq_ring[pl.ds(slot(m), QB), cs] = dq * scale

        pl.when(m < LEAD)(functools.partial(step, True))
        pl.when((m >= LEAD) & (m < nq))(functools.partial(step, False))

        done = pl.ds(slot(m - LEAD), QB)
        zl = zl_ref[...].astype(F32)
        sg = _sigmoid(zl)
        dz = dyl_ref[...].astype(F32) * rl_ref[...].astype(F32) * (sg * (1.0 + zl * (1.0 - sg)))
        dh_ref[:, A_Q:A_Q + A_W] = dq_ring[done, :].astype(BF16)
        dh_ref[:, A_K:A_K + A_W] = dk_acc[done, :].astype(BF16)
        dh_ref[:, A_V:A_V + A_W] = dv_acc[done, :].astype(BF16)
        dh_ref[:, A_Z:A_Z + A_W] = dz.astype(BF16)
        dk_acc[done, :] = jnp.zeros((QB, A_W), F32)
        dv_acc[done, :] = jnp.zeros((QB, A_W), F32)

        @pl.when(m == nq + LEAD - 1)
        def _():
            pltpu.sync_copy(dbias_ref, dbias_hbm)

    last = nq - 1

    def cur(col):
        return pl.BlockSpec((QB, A_W), lambda m: (jnp.minimum(m, last), col))

    def kv_spec(col, t):
        return pl.BlockSpec((QB, A_W), lambda m: (jnp.clip(m + t - LEAD, 0, last), col))

    def lag(col):
        return pl.BlockSpec((QB, A_W), lambda m: (jnp.clip(m - LEAD, 0, last), col))

    (dh, dbias), c_out = _hosted(
        body, comm, name="attn_bwd",
        out_shape=(jax.ShapeDtypeStruct(dh_in.shape, BF16), jax.ShapeDtypeStruct((A_HEADS, QB, KB), F32)),
        grid=(nq + LEAD,),
        in_specs=[cur(0)] + [kv_spec(1, t) for t in range(NKB)] + [kv_spec(2, t) for t in range(NKB)]
        + [cur(3), cur(0), cur(0), pl.BlockSpec((QB, A_HEADS), lambda m: (jnp.minimum(m, last), 0)),
           lag(3), lag(0), lag(0),
           pl.BlockSpec(memory_space=pl.ANY), pl.BlockSpec(memory_space=pl.ANY)],
        out_specs=(pl.BlockSpec((QB, 4 * A_W), lambda m: (jnp.clip(m - LEAD, 0, last), 0)),
                   pl.BlockSpec(memory_space=pl.ANY)),
        scratch_shapes=[pltpu.VMEM((KB, A_W), F32), pltpu.VMEM((KB, A_W), F32), pltpu.VMEM((KB, A_W), F32),
                        pltpu.VMEM((A_HEADS, QB, KB), F32), pltpu.VMEM((A_HEADS, QB, KB), F32)],
        input_output_aliases={2 * NKB + 9: 0},
        compiler_params=_params(("arbitrary",), 56),
        inputs=(h,) * (2 * NKB + 2) + (r, dy, lse, h, r, dy, bias, dh_in))
    return dh, dbias, c_out


GB = 256
N_PAIR = B_HEADS // 2


def _gla_gates(lr, gw_ref, gb_ref):
    logit = _dot(lr, gw_ref[...], 1, 0) + gb_ref[...]
    lg = (jnp.minimum(logit, 0.0) - jnp.log(1.0 + jnp.exp(-jnp.abs(logit)))) / GATE_TAU
    row = lax.broadcasted_iota(jnp.int32, (GB, GB), 0)
    col = lax.broadcasted_iota(jnp.int32, (GB, GB), 1)
    tri = jnp.where((row // CH == col // CH) & (col <= row), 1.0, 0.0).astype(F32)
    return logit, _dot(tri, lg, 1, 0, precision=lax.Precision.HIGHEST)


def _gla_factors(hb_ref, b_all, c):
    rs = slice(c * CH, (c + 1) * CH)
    q = hb_ref[rs, 0:B_KW].astype(F32) * (B_DK ** -0.5)
    k = hb_ref[rs, B_KW:2 * B_KW].astype(F32)
    b = b_all[rs]
    bm, bl = b[CH // 2:CH // 2 + 1, :], b[CH - 1:CH, :]
    e1, e2, eb, ek = jnp.exp(b - bm), jnp.exp(bm - b), jnp.exp(b), jnp.exp(bl - b)
    el = jnp.exp(bl)
    return dict(ql=q * e1, kl=k * e2, qu=q * e2, ku=k * e1, qt=q * eb, kh=k * ek, e1=e1, e2=e2, eb=eb, ek=ek, el=el)


class _GlaPairTools:
    def __init__(self, hb_ref, fs):
        self.hb_ref, self.fs = hb_ref, fs
        row = lax.broadcasted_iota(jnp.int32, (2 * CH, 2 * CH), 0)
        col = lax.broadcasted_iota(jnp.int32, (2 * CH, 2 * CH), 1)
        self.same = (row // CH) == (col // CH)
        self.lower = self.same & ((row % CH) >= (col % CH))
        self.upper = self.same & ((row % CH) < (col % CH))
        self.lower_t = self.same & ((col % CH) >= (row % CH))
        self.upper_t = self.same & ((col % CH) < (row % CH))

    def lanes(self, c, p, name):
        return self.fs[c][name][:, p * 128:(p + 1) * 128]

    def heads(self, c, p, name):
        x = self.lanes(c, p, name)
        return jnp.where(self.same, jnp.concatenate([x, x], axis=0), 0.0).astype(BF16)

    def twice(self, c, p, name):
        x = self.lanes(c, p, name).astype(BF16)
        return jnp.concatenate([x, x], axis=0)

    def vals(self, c, p):
        rows = slice(c * CH, (c + 1) * CH)
        return jnp.concatenate([self.hb_ref[rows, 512 + (2 * p + sh) * B_DV:512 + (2 * p + sh + 1) * B_DV]
                                for sh in range(2)], axis=0)

    def intra(self, c, p):
        lo = _dot(self.heads(c, p, "ql"), self.twice(c, p, "kl"), 1, 1)
        up = _dot(self.heads(c, p, "qu"), self.twice(c, p, "ku"), 1, 1)
        return jnp.where(self.lower, lo, jnp.where(self.upper, up, 0.0)).astype(BF16)

    def fold(self, x):
        x = jnp.where(self.same, x, 0.0)
        return x[:CH] + x[CH:]


def _gla_fwd(h, r_in, gw, gb, ng, comm=None):
    s = h.shape[0]
    nb = s // GB
    cpb = GB // CH

    def body(hb_ref, gw_ref, gb_ref, ng_ref, r_any, o_ref, opre_ref, st_ref, state):
        del r_any

        @pl.when(pl.program_id(0) == 0)
        def _():
            state[...] = jnp.zeros_like(state)

        _, b_all = _gla_gates(hb_ref[:, 1536:1664], gw_ref, gb_ref)
        fs = [_gla_factors(hb_ref, b_all, c) for c in range(cpb)]
        pairs = [(c, p) for c in range(cpb) for p in range(N_PAIR)]
        tools = _GlaPairTools(hb_ref, fs)
        a = {k: tools.intra(*k) for k in pairs}
        o_intra = {k: _dot(a[k], tools.vals(*k), 1, 0) for k in pairs}
        upd = {k: _dot(tools.vals(*k), tools.heads(*k, "kh"), 0, 0) for k in pairs}
        st = [state[p] for p in range(N_PAIR)]
        entering = {}
        for c, p in pairs:
            entering[c, p] = st[p]
            st_ref[c, p] = st[p]
            st[p] = st[p] * fs[c]["el"][:, p * 128:(p + 1) * 128] + upd[c, p]
        for p in range(N_PAIR):
            state[p] = st[p]
        for c, p in pairs:
            o2 = o_intra[c, p] + _dot(tools.heads(c, p, "qt"), entering[c, p].astype(BF16), 1, 1)
            for sh in range(2):
                o = o2[sh * CH:(sh + 1) * CH]
                rs, hs = slice(c * CH, (c + 1) * CH), slice((2 * p + sh) * B_DV, (2 * p + sh + 1) * B_DV)
                opre_ref[rs, hs] = o
                rinv = lax.rsqrt(jnp.mean(o * o, axis=1, keepdims=True) + RMS_EPS)
                o_ref[rs, hs] = (o * rinv * ng_ref[...]).astype(BF16)

    (r, opre, states), c_out = _hosted(
        body, comm, name="gla_fwd",
        out_shape=(jax.ShapeDtypeStruct(r_in.shape, BF16), jax.ShapeDtypeStruct((s, B_W), F32),
                   jax.ShapeDtypeStruct((s // CH, N_PAIR, 128, 128), F32)),
        grid=(nb,),
        in_specs=[pl.BlockSpec((GB, 2048), lambda i: (i, B_BASE // 2048)),
                  pl.BlockSpec((128, B_KW), lambda i: (0, 0)), pl.BlockSpec((1, B_KW), lambda i: (0, 0)),
                  pl.BlockSpec((1, B_DV), lambda i: (0, 0)), pl.BlockSpec(memory_space=pl.ANY)],
        out_specs=(pl.BlockSpec((GB, B_W), lambda i: (i, 1024 // B_W)), pl.BlockSpec((GB, B_W), lambda i: (i, 0)),
                   pl.BlockSpec((cpb, N_PAIR, 128, 128), lambda i: (i, 0, 0, 0))),
        scratch_shapes=[pltpu.VMEM((N_PAIR, 128, 128), F32)],
        input_output_aliases={4: 0},
        compiler_params=_params(("arbitrary",)),
        inputs=(h, gw, gb, ng, r_in))
    return r, opre, states, c_out


def _gla_bwd(h, dy, opre, states, gw, gb, ng, dh_in, comm=None):
    s = h.shape[0]
    nb = s // GB
    cpb = GB // CH

    def body(hb_ref, dy_ref, opre_ref, st_ref, gw_ref, gb_ref, ng_ref, dh_any,
             dh_ref, dgw_ref, dgb_ref, dng_ref, dstate, db_scr, do_scr):
        del dh_any

        @pl.when(pl.program_id(0) == 0)
        def _():
            dstate[...] = jnp.zeros_like(dstate)
            dgw_ref[...] = jnp.zeros_like(dgw_ref)
            dgb_ref[...] = jnp.zeros_like(dgb_ref)
            dng_ref[...] = jnp.zeros_like(dng_ref)

        lr = hb_ref[:, 1536:1664]
        logit, b_all = _gla_gates(lr, gw_ref, gb_ref)
        z = hb_ref[:, 1024:1536].astype(F32)
        sg = _sigmoid(z)
        dyb = dy_ref[...].astype(F32)
        dng = jnp.zeros((1, B_DV), F32)
        for hd in range(B_HEADS):
            hs = slice(hd * B_DV, (hd + 1) * B_DV)
            o = opre_ref[:, hs]
            rinv = lax.rsqrt(jnp.mean(o * o, axis=1, keepdims=True) + RMS_EPS)
            on = o * rinv
            dr = dyb[:, hs] * (z[:, hs] * sg[:, hs])
            dh_ref[:, 1024 + hd * B_DV:1024 + (hd + 1) * B_DV] = (
                dyb[:, hs] * (on * ng_ref[...]) * (sg[:, hs] * (1.0 + z[:, hs] * (1.0 - sg[:, hs])))).astype(BF16)
            dng += jnp.sum(dr * on, axis=0, keepdims=True)
            dn = dr * ng_ref[...]
            do_scr[:, hs] = rinv * (dn - on * jnp.mean(dn * on, axis=1, keepdims=True))
        dng_ref[...] += dng

        rowi = lax.broadcasted_iota(jnp.int32, (CH, 128), 0)
        fs = [_gla_factors(hb_ref, b_all, c) for c in range(cpb)]
        pairs = [(c, p) for c in reversed(range(cpb)) for p in range(N_PAIR)]
        tools = _GlaPairTools(hb_ref, fs)
        pair = tools.lanes

        def douts(c, p):
            rows = slice(c * CH, (c + 1) * CH)
            return jnp.concatenate([do_scr[rows, (2 * p + sh) * B_DV:(2 * p + sh + 1) * B_DV] for sh in range(2)],
                                   axis=0).astype(BF16)

        at, da, dat, dst_own, g_qt = {}, {}, {}, {}, {}
        for k in pairs:
            c, p = k
            lo_t = _dot(tools.twice(*k, "kl"), tools.heads(*k, "ql"), 1, 1)
            up_t = _dot(tools.twice(*k, "ku"), tools.heads(*k, "qu"), 1, 1)
            at[k] = jnp.where(tools.lower_t, lo_t, jnp.where(tools.upper_t, up_t, 0.0)).astype(BF16)
            da[k] = _dot(douts(*k), tools.vals(*k), 1, 1)
            dat[k] = _dot(tools.vals(*k), douts(*k), 1, 1)
            dst_own[k] = _dot(douts(*k), tools.heads(*k, "qt"), 0, 0)
            g_qt[k] = tools.fold(_dot(douts(*k), st_ref[c, p].astype(BF16), 1, 0))
        dv, g_ql, g_qu, g_kl, g_ku = {}, {}, {}, {}, {}
        for k in pairs:
            dv[k] = _dot(at[k], douts(*k), 1, 0)
            g_ql[k] = tools.fold(_dot(jnp.where(tools.lower, da[k], 0.0).astype(BF16), tools.twice(*k, "kl"), 1, 0))
            g_qu[k] = tools.fold(_dot(jnp.where(tools.upper, da[k], 0.0).astype(BF16), tools.twice(*k, "ku"), 1, 0))
            g_kl[k] = tools.fold(_dot(jnp.where(tools.lower_t, dat[k], 0.0).astype(BF16), tools.heads(*k, "ql"), 1, 0))
            g_ku[k] = tools.fold(_dot(jnp.where(tools.upper_t, dat[k], 0.0).astype(BF16), tools.heads(*k, "qu"), 1, 0))
        dst = [dstate[p] for p in range(N_PAIR)]
        leaving = {}
        for c, p in pairs:
            leaving[c, p] = dst[p]
            dst[p] = dst[p] * pair(c, p, "el") + dst_own[c, p]
        for p in range(N_PAIR):
            dstate[p] = dst[p]
        g_kh = {}
        for k in pairs:
            c, p = k
            dstb = leaving[k].astype(BF16)
            dv2 = dv[k] + _dot(tools.heads(*k, "kh"), dstb, 1, 1)
            for sh in range(2):
                hd = 2 * p + sh
                dh_ref[c * CH:(c + 1) * CH, 512 + hd * B_DV:512 + (hd + 1) * B_DV] = dv2[sh * CH:(sh + 1) * CH].astype(BF16)
            g_kh[k] = tools.fold(_dot(tools.vals(*k), dstb, 1, 0))
        for c in reversed(range(cpb)):
            rs = slice(c * CH, (c + 1) * CH)
            for p in range(N_PAIR):
                k = (c, p)
                dq = (g_ql[k] * pair(c, p, "e1") + g_qu[k] * pair(c, p, "e2") + g_qt[k] * pair(c, p, "eb")) * (B_DK ** -0.5)
                dk = g_kl[k] * pair(c, p, "e2") + g_ku[k] * pair(c, p, "e1") + g_kh[k] * pair(c, p, "ek")
                dkh_kh = g_kh[k] * pair(c, p, "kh")
                db = (g_ql[k] * pair(c, p, "ql") - g_qu[k] * pair(c, p, "qu") + g_qt[k] * pair(c, p, "qt")
                      - g_kl[k] * pair(c, p, "kl") + g_ku[k] * pair(c, p, "ku") - dkh_kh)
                db_last = (pair(c, p, "el") * jnp.sum(leaving[c, p] * st_ref[c, p], axis=0, keepdims=True)
                           + jnp.sum(dkh_kh, axis=0, keepdims=True))
                db = jnp.where(rowi == CH - 1, db + db_last, db)
                dh_ref[rs, p * 128:(p + 1) * 128] = dq.astype(BF16)
                dh_ref[rs, B_KW + p * 128:B_KW + (p + 1) * 128] = dk.astype(BF16)
                db_scr[rs, p * 128:(p + 1) * 128] = db

        row = lax.broadcasted_iota(jnp.int32, (GB, GB), 0)
        col = lax.broadcasted_iota(jnp.int32, (GB, GB), 1)
        trit = jnp.where((row // CH == col // CH) & (col >= row), 1.0, 0.0).astype(F32)
        dlg = _dot(trit, db_scr[...], 1, 0, precision=lax.Precision.HIGHEST)
        dlogit = dlg * (_sigmoid(-logit) / GATE_TAU)
        dlb = dlogit.astype(BF16)
        dgw_ref[...] += _dot(lr, dlb, 0, 0)
        dgb_ref[...] += jnp.sum(dlogit, axis=0, keepdims=True)
        dh_ref[:, 1536:1664] = _dot(dlb, gw_ref[...], 1, 1).astype(BF16)
        dh_ref[:, 1664:2048] = jnp.zeros((GB, 384), BF16)

    rev = lambda i: nb - 1 - i
    (dh, dgw, dgb, dng), c_out = _hosted(
        body, comm, name="gla_bwd",
        out_shape=(jax.ShapeDtypeStruct(dh_in.shape, BF16), jax.ShapeDtypeStruct((128, B_KW), F32),
                   jax.ShapeDtypeStruct((1, B_KW), F32), jax.ShapeDtypeStruct((1, B_DV), F32)),
        grid=(nb,),
        in_specs=[pl.BlockSpec((GB, 2048), lambda i: (rev(i), B_BASE // 2048)),
                  pl.BlockSpec((GB, B_W), lambda i: (rev(i), 1024 // B_W)),
                  pl.BlockSpec((GB, B_W), lambda i: (rev(i), 0)),
                  pl.BlockSpec((cpb, N_PAIR, 128, 128), lambda i: (rev(i), 0, 0, 0)),
                  pl.BlockSpec((128, B_KW), lambda i: (0, 0)), pl.BlockSpec((1, B_KW), lambda i: (0, 0)),
                  pl.BlockSpec((1, B_DV), lambda i: (0, 0)), pl.BlockSpec(memory_space=pl.ANY)],
        out_specs=(pl.BlockSpec((GB, 2048), lambda i: (rev(i), B_BASE // 2048)),
                   pl.BlockSpec((128, B_KW), lambda i: (0, 0)), pl.BlockSpec((1, B_KW), lambda i: (0, 0)),
                   pl.BlockSpec((1, B_DV), lambda i: (0, 0))),
        scratch_shapes=[pltpu.VMEM((N_PAIR, 128, 128), F32), pltpu.VMEM((GB, B_KW), F32), pltpu.VMEM((GB, B_W), F32)],
        input_output_aliases={7: 0},
        compiler_params=_params(("arbitrary",)),
        inputs=(h, dy, opre, states, gw, gb, ng, dh_in))
    return dh, dgw, dgb, dng, c_out


MB = 512


def _mem_probs(q, mk, scale):
    sc = _dot(q, mk, 1, 1) * scale
    p = jnp.exp(sc - jnp.max(sc, axis=1, keepdims=True))
    return p / jnp.sum(p, axis=1, keepdims=True)


def _mem_fwd(h, r_in, mkv):
    s = h.shape[0]
    scale = M_DH ** -0.5

    def body(q_ref, mkv_ref, r_any, o_ref):
        del r_any
        for hd in range(M_HEADS):
            cs = slice(hd * M_DH, (hd + 1) * M_DH)
            p = _mem_probs(q_ref[:, cs], mkv_ref[:, cs], scale)
            o_ref[:, cs] = _dot(p.astype(BF16), mkv_ref[:, M_W + hd * M_DH:M_W + (hd + 1) * M_DH], 1, 0).astype(BF16)

    return pl.pallas_call(
        body, name="mem_fwd",
        out_shape=jax.ShapeDtypeStruct(r_in.shape, BF16),
        grid=(s // MB,),
        in_specs=[pl.BlockSpec((MB, M_W), lambda i: (i, M_BASE // M_W)),
                  pl.BlockSpec((N_MEM, 2 * M_W), lambda i: (0, 0)), pl.BlockSpec(memory_space=pl.ANY)],
        out_specs=pl.BlockSpec((MB, M_W), lambda i: (i, 1536 // M_W)),
        input_output_aliases={2: 0},
        compiler_params=_params(("arbitrary",)),
    )(h, mkv, r_in)


def _mem_bwd(h, r, dy, mkv, dh_in):
    s = h.shape[0]
    scale = M_DH ** -0.5

    def body(q_ref, z_ref, r_ref, dy_ref, mkv_ref, dh_any, dh_ref, dmkv_ref):
        del dh_any

        @pl.when(pl.program_id(0) == 0)
        def _():
            dmkv_ref[...] = jnp.zeros_like(dmkv_ref)

        z = z_ref[...].astype(F32)
        sg = _sigmoid(z)
        dyv = dy_ref[...].astype(F32)
        do_all = dyv * (z * sg)
        dh_ref[:, M_W:2 * M_W] = (dyv * r_ref[...].astype(F32) * (sg * (1.0 + z * (1.0 - sg)))).astype(BF16)
        for hd in range(M_HEADS):
            cs = slice(hd * M_DH, (hd + 1) * M_DH)
            vs = slice(M_W + hd * M_DH, M_W + (hd + 1) * M_DH)
            q = q_ref[:, cs]
            p = _mem_probs(q, mkv_ref[:, cs], scale)
            dob = do_all[:, cs].astype(BF16)
            dp = _dot(dob, mkv_ref[:, vs], 1, 1)
            ds = p * (dp - jnp.sum(p * dp, axis=1, keepdims=True))
            dsb = ds.astype(BF16)
            dh_ref[:, cs] = (_dot(dsb, mkv_ref[:, cs], 1, 0) * scale).astype(BF16)
            dmkv_ref[:, cs] += _dot(dsb, q, 0, 0) * scale
            dmkv_ref[:, vs] += _dot(p.astype(BF16), dob, 0, 0)

    return pl.pallas_call(
        body, name="mem_bwd",
        out_shape=(jax.ShapeDtypeStruct(dh_in.shape, BF16), jax.ShapeDtypeStruct((N_MEM, 2 * M_W), F32)),
        grid=(s // MB,),
        in_specs=[pl.BlockSpec((MB, M_W), lambda i: (i, M_BASE // M_W)),
                  pl.BlockSpec((MB, M_W), lambda i: (i, M_BASE // M_W + 1)),
                  pl.BlockSpec((MB, M_W), lambda i: (i, 1536 // M_W)),
                  pl.BlockSpec((MB, M_W), lambda i: (i, 1536 // M_W)),
                  pl.BlockSpec((N_MEM, 2 * M_W), lambda i: (0, 0)), pl.BlockSpec(memory_space=pl.ANY)],
        out_specs=(pl.BlockSpec((MB, 2 * M_W), lambda i: (i, M_BASE // (2 * M_W))),
                   pl.BlockSpec((N_MEM, 2 * M_W), lambda i: (0, 0))),
        input_output_aliases={5: 0},
        compiler_params=_params(("arbitrary",)),
    )(h, h, r, dy, mkv, dh_in)


OB = 256


def _outproj_ln(h, r, w_out, x, ln_g, ln_b, target=None):
    s = h.shape[0]
    last = target is not None

    def body(za_ref, zb_ref, zm_ref, r_ref, w_ref, x_ref, g_ref, b_ref, *rest):
        if last:
            t_ref, xn_ref, xh_ref, rstd_ref, y_ref, l_ref = rest
        else:
            xn_ref, xh_ref, rstd_ref, y_ref = rest
        z = jnp.concatenate([za_ref[...], zb_ref[...], zm_ref[...]], axis=1).astype(F32)
        y = (r_ref[...].astype(F32) * (z * _sigmoid(z))).astype(BF16)
        y_ref[...] = y
        u = ALPHA * x_ref[...] + _dot(y, w_ref[...], 1, 0)
        mu = jnp.mean(u, axis=1, keepdims=True)
        uc = u - mu
        rstd = lax.rsqrt(jnp.mean(uc * uc, axis=1, keepdims=True) + LN_EPS)
        xh = uc * rstd
        xh_ref[...] = xh
        rstd_ref[...] = rstd
        xn = xh * g_ref[...] + b_ref[...]
        if last:
            @pl.when(pl.program_id(0) == 0)
            def _():
                l_ref[...] = jnp.zeros_like(l_ref)

            e = xn - t_ref[...]
            xn_ref[...] = e / D
            l_ref[...] += 0.5 * jnp.sum(jnp.mean(e * e, axis=1, keepdims=True))
        else:
            xn_ref[...] = xn

    row = lambda w, c: pl.BlockSpec((OB, w), lambda i: (i, c))
    vec = pl.BlockSpec((1, D), lambda i: (0, 0))
    full = jax.ShapeDtypeStruct((s, D), F32)
    return pl.pallas_call(
        body, name="outproj_ln_loss" if last else "outproj_ln",
        out_shape=(full, full, jax.ShapeDtypeStruct((s, 1), F32), jax.ShapeDtypeStruct((s, D), BF16))
        + ((jax.ShapeDtypeStruct((1, 128), F32),) if last else ()),
        grid=(s // OB,),
        in_specs=[row(A_W, A_Z // A_W), row(B_W, (B_BASE + 1024) // B_W), row(M_W, (M_BASE + M_W) // M_W), row(D, 0),
                  pl.BlockSpec((D, D), lambda i: (0, 0)), row(D, 0), vec, vec] + ([row(D, 0)] if last else []),
        out_specs=(row(D, 0), row(D, 0), pl.BlockSpec((OB, 1), lambda i: (i, 0)), row(D, 0))
        + ((pl.BlockSpec((1, 128), lambda i: (0, 0)),) if last else ()),
        compiler_params=_params(("arbitrary",), 56),
    )(h, h, h, r, w_out, x, ln_g, ln_b, *([target] if last else []))


def _ln_bwd(g, xh, rstd, ln_g):
    s = g.shape[0]

    def body(g_ref, xh_ref, rstd_ref, lg_ref, du_ref, dg_ref, db_ref):
        @pl.when(pl.program_id(0) == 0)
        def _():
            dg_ref[...] = jnp.zeros_like(dg_ref)
            db_ref[...] = jnp.zeros_like(db_ref)

        gv, xh = g_ref[...], xh_ref[...]
        dg_ref[...] += jnp.sum(gv * xh, axis=0, keepdims=True)
        db_ref[...] += jnp.sum(gv, axis=0, keepdims=True)
        dxh = gv * lg_ref[...]
        du_ref[...] = rstd_ref[...] * (dxh - jnp.mean(dxh, axis=1, keepdims=True)
                                       - xh * jnp.mean(dxh * xh, axis=1, keepdims=True))

    row = pl.BlockSpec((OB, D), lambda i: (i, 0))
    vec = pl.BlockSpec((1, D), lambda i: (0, 0))
    return pl.pallas_call(
        body, name="ln_bwd",
        out_shape=(jax.ShapeDtypeStruct((s, D), F32), jax.ShapeDtypeStruct((1, D), F32), jax.ShapeDtypeStruct((1, D), F32)),
        grid=(s // OB,),
        in_specs=[row, row, pl.BlockSpec((OB, 1), lambda i: (i, 0)), vec],
        out_specs=(row, vec, vec),
        compiler_params=_params(("arbitrary",)),
    )(g, xh, rstd, ln_g)


LB = 512


def _ln_bwd_dy(g, xh, rstd, ln_g, w_out):
    s = g.shape[0]
    tn = 1024
    nj = D // tn

    def body(g_ref, xh_ref, rstd_ref, lg_ref, w_ref, dy_ref, du_ref, dg_ref, db_ref, dub):
        i, j = pl.program_id(0), pl.program_id(1)

        @pl.when((i == 0) & (j == 0))
        def _():
            dg_ref[...] = jnp.zeros_like(dg_ref)
            db_ref[...] = jnp.zeros_like(db_ref)

        @pl.when(j == 0)
        def _():
            gv, xh = g_ref[...], xh_ref[...]
            dg_ref[...] += jnp.sum(gv * xh, axis=0, keepdims=True)
            db_ref[...] += jnp.sum(gv, axis=0, keepdims=True)
            dxh = gv * lg_ref[...]
            du = rstd_ref[...] * (dxh - jnp.mean(dxh, axis=1, keepdims=True)
                                  - xh * jnp.mean(dxh * xh, axis=1, keepdims=True))
            du_ref[...] = du
            dub[...] = du.astype(BF16)

        dy_ref[...] = _dot(dub[...], w_ref[...], 1, 1).astype(BF16)

    row = pl.BlockSpec((LB, D), lambda i, j: (i, 0))
    vec = pl.BlockSpec((1, D), lambda i, j: (0, 0))
    return pl.pallas_call(
        body, name="ln_bwd_dy",
        out_shape=(jax.ShapeDtypeStruct((s, D), BF16), jax.ShapeDtypeStruct((s, D), F32),
                   jax.ShapeDtypeStruct((1, D), F32), jax.ShapeDtypeStruct((1, D), F32)),
        grid=(s // LB, nj),
        in_specs=[row, row, pl.BlockSpec((LB, 1), lambda i, j: (i, 0)), vec, pl.BlockSpec((tn, D), lambda i, j: (j, 0))],
        out_specs=(pl.BlockSpec((LB, tn), lambda i, j: (i, j)), row, vec, vec),
        scratch_shapes=[pltpu.VMEM((LB, D), BF16)],
        compiler_params=_params(("arbitrary", "arbitrary")),
    )(g, xh, rstd, ln_g, w_out)


def _loss_grad(y, target):
    s = y.shape[0]

    def body(y_ref, t_ref, l_ref, dy_ref):
        @pl.when(pl.program_id(0) == 0)
        def _():
            l_ref[...] = jnp.zeros_like(l_ref)

        e = y_ref[...] - t_ref[...]
        dy_ref[...] = e / D
        l_ref[...] += 0.5 * jnp.sum(jnp.mean(e * e, axis=1, keepdims=True))

    row = pl.BlockSpec((OB, D), lambda i: (i, 0))
    return pl.pallas_call(
        body, name="loss_grad",
        out_shape=(jax.ShapeDtypeStruct((1, 128), F32), jax.ShapeDtypeStruct((s, D), F32)),
        grid=(s // OB,),
        in_specs=[row, row],
        out_specs=(pl.BlockSpec((1, 128), lambda i: (0, 0)), row),
        compiler_params=_params(("arbitrary",)),
    )(y, target)


class _LocalWeights:
    def __init__(self, w_in_p, w_out_f, w_kv_f):
        self.w = list(zip(w_in_p, w_out_f, w_kv_f))
        self.depth = len(self.w)
        self.grads = [dict() for _ in self.w]

    def weights(self, l):
        return self.w[l]

    def host(self, where, l, payload=None):
        if payload is not None:
            self.grads[l][where] = payload
        return None

    def landed(self, where, l, outs):
        pass


def _shards_to_padded(raw):
    sh = raw.shape[2]
    parts = []
    for j in range(N_DEV):
        lo, hi = j * sh, (j + 1) * sh
        if lo < NAT_SPLIT < hi:
            parts += [raw[j][:, :NAT_SPLIT - lo], jnp.zeros((D, H_PAD), raw.dtype), raw[j][:, NAT_SPLIT - lo:]]
        else:
            parts.append(raw[j])
            if hi == NAT_SPLIT:
                parts.append(jnp.zeros((D, H_PAD), raw.dtype))
    return jnp.concatenate(parts, axis=1)


def _padded_to_shards(w):
    sh = IN_W // N_DEV
    pos = lambda c: c if c < NAT_SPLIT else c + H_PAD
    blocks = []
    for j in range(N_DEV):
        lo, hi = j * sh, (j + 1) * sh
        if lo < NAT_SPLIT < hi:
            blocks.append(jnp.concatenate([w[:, lo:NAT_SPLIT], w[:, pos(NAT_SPLIT):pos(hi - 1) + 1]], axis=1))
        else:
            blocks.append(w[:, pos(lo):pos(lo) + sh])
    return jnp.stack(blocks)


_PART_A = (0, 1, 2, 4, 7)
_PART_B = (3, 5, 6)


class _Fsdp:
    def __init__(self, w_in, w_out, w_kv, extra):
        self.sh = (w_in, w_out, w_kv)
        self.depth = w_in.shape[0]
        self.raw = [dict() for _ in range(self.depth)]
        self.recv = [dict() for _ in range(self.depth)]
        g_in, g_out, g_kv, *self.extra = _exchange([w_in[0], w_out[0], w_kv[0]] + list(extra), False, "gather_layer0")
        self.raw[0] = dict(w_in=g_in, w_out=g_out, w_kv=g_kv)

    def weights(self, l):
        raw = self.raw[l]
        return _shards_to_padded(raw["w_in"]), raw["w_out"].reshape(D, D), raw["w_kv"].reshape(D, 2 * M_W)

    def host(self, where, l, payload=None):
        w_in, w_out, w_kv = self.sh
        if where == "in_proj" and l + 1 < self.depth:
            return _Gather([w_in[l + 1], w_out[l + 1], w_kv[l + 1]])
        if where == "d_w_in":
            d_wout, d_wkv = payload
            return _Comm([d_wout.reshape(N_DEV, D // N_DEV, D), d_wkv.reshape(N_DEV, D // N_DEV, 2 * M_W)], True)
        if where == "d_x":
            self.blocks = _padded_to_shards(payload)
            return _Comm([self.blocks], True, relations=_PART_A if l > 0 else tuple(range(N_DEV)))
        if where == "attn_bwd" and l + 1 < self.depth:
            return _Comm([self.blocks], True, relations=_PART_B, into=[self.recv[l + 1]["w_in"]])
        return None

    def landed(self, where, l, outs):
        if where == "in_proj" and outs:
            self.raw[l + 1] = dict(zip(("w_in", "w_out", "w_kv"), outs))
        elif where == "d_w_in":
            self.recv[l]["w_out"], self.recv[l]["w_kv"] = outs
        elif where == "d_x":
            self.recv[l]["w_in"] = outs[0]
        elif where == "attn_bwd" and outs:
            self.recv[l + 1]["w_in"] = outs[0]


def _local_step(x, mem, target, pipe, rel, gate_w, gate_b, norm_g, ln_g, ln_b):
    depth = pipe.depth
    s = x.shape[0]
    saved = []
    xl = x
    for l in range(depth):
        w_in_p, w_out_f, w_kv_f = pipe.weights(l)
        hmat, landed = _mm(xl, w_in_p, out_dtype=BF16, tm=1024, tn=512, tk=D, name="in_proj", comm=pipe.host("in_proj", l))
        pipe.landed("in_proj", l, landed)
        mkv, _ = _mm(mem, w_kv_f, out_dtype=BF16, tm=N_MEM, tn=1024, tk=D, name="mem_kv")
        bias = _band_bias(rel[l])
        gw = jnp.zeros((128, B_KW), F32).at[:GATE_RANK].set(gate_w[l]).astype(BF16)
        gb, ng = gate_b[l][None, :], norm_g[l][None, :]
        r, lse, _ = _attn_fwd(hmat, lax.empty((s, D), BF16), bias)
        r, opre, states, _ = _gla_fwd(hmat, r, gw, gb, ng)
        r = _mem_fwd(hmat, r, mkv)
        xn, xh, rstd, y, *loss = _outproj_ln(hmat, r, w_out_f, xl, ln_g[l][None, :], ln_b[l][None, :],
                                             target if l == depth - 1 else None)
        saved.append(dict(x=xl, h=hmat, mkv=mkv, bias=bias, gw=gw, gb=gb, ng=ng, r=r, lse=lse, opre=opre,
                          states=states, xh=xh, rstd=rstd, y=y, w_in_p=w_in_p, w_out_f=w_out_f))
        xl = xn
    (loss,), g = loss, xl

    grads = [None] * depth
    for l in reversed(range(depth)):
        sv = saved[l]
        dy, du, d_lng, d_lnb = _ln_bwd_dy(g, sv["xh"], sv["rstd"], ln_g[l][None, :], sv["w_out_f"])
        d_wout, _ = _mm(sv["y"], du, ta=True, out_dtype=BF16, tm=1024, tn=1024, tk=1024, name="d_w_out")
        dh, dbias, landed = _attn_bwd(sv["h"], sv["r"], dy, sv["lse"], sv["bias"], lax.empty((s, H_W), BF16),
                                      pipe.host("attn_bwd", l))
        pipe.landed("attn_bwd", l, landed)
        dh, d_gw, d_gb, d_ng, _ = _gla_bwd(sv["h"], dy, sv["opre"], sv["states"], sv["gw"], sv["gb"], sv["ng"], dh)
        dh, d_mkv = _mem_bwd(sv["h"], sv["r"], dy, sv["mkv"], dh)
        d_wkv, _ = _mm(mem, d_mkv, ta=True, out_dtype=BF16, tm=1024, tn=1024, tk=N_MEM, name="d_w_kv")
        d_win, landed = _mm(sv["x"], dh, ta=True, out_dtype=BF16, tm=1024, tn=1792, tk=1024, name="d_w_in",
                            comm=pipe.host("d_w_in", l, (d_wout, d_wkv)))
        pipe.landed("d_w_in", l, landed)
        g, landed = _mm(dh, sv["w_in_p"], tb=True, out_dtype=F32, tm=1024, tn=1024, tk=1792, name="d_x",
                        adds=((du, ALPHA),), comm=pipe.host("d_x", l, d_win))
        pipe.landed("d_x", l, landed)
        grads[l] = dict(rel=_bias_grad(dbias), gate_w=d_gw[:GATE_RANK], gate_b=d_gb[0], norm_g=d_ng[0],
                        ln_g=d_lng[0], ln_b=d_lnb[0])
    return loss, g, grads


def _adamw(parts, w, m, v, rows_per_step, name):
    depth, rows, cols = w.shape
    n = parts[0].shape[0]
    tr = min(rows_per_step, rows)
    assert rows % tr == 0 and len(parts) == depth

    def body(*refs):
        p_refs = refs[:depth]
        w_ref, m_ref, v_ref, g_ref, d_ref, nm_ref, nv_ref = refs[depth:]
        for l in range(depth):
            @pl.when(pl.program_id(0) == l)
            def _(p_ref=p_refs[l]):
                g = p_ref[0].astype(F32)
                for j in range(1, n):
                    g = g + p_ref[j].astype(F32)
                nm = ADAM_B1 * m_ref[...] + (1.0 - ADAM_B1) * g
                nv = ADAM_B2 * v_ref[...] + (1.0 - ADAM_B2) * (g * g)
                m_hat = nm / (1.0 - ADAM_B1 ** ADAM_STEP)
                v_hat = nv / (1.0 - ADAM_B2 ** ADAM_STEP)
                g_ref[...] = g
                nm_ref[...] = nm
                nv_ref[...] = nv
                d_ref[...] = -ADAM_LR * (m_hat / (jnp.sqrt(v_hat) + ADAM_EPS) + ADAM_WD * w_ref[...])

    def part_spec(l):
        return pl.BlockSpec((n, tr, cols), lambda ll, i: (0, jnp.where(ll == l, i, 0), 0))

    blk = pl.BlockSpec((None, tr, cols), lambda ll, i: (ll, i, 0))
    shape = jax.ShapeDtypeStruct((depth, rows, cols), F32)
    return pl.pallas_call(
        body, name=name,
        out_shape=(shape, shape, shape, shape),
        grid=(depth, rows // tr),
        in_specs=[part_spec(l) for l in range(depth)] + [blk, blk, blk],
        out_specs=(blk, blk, blk, blk),
        compiler_params=_params(("arbitrary", "arbitrary")),
    )(*parts, w, m, v)


SMALL = (("rel", A_HEADS * (2 * MAX_REL + 1)), ("gate_w", GATE_RANK * B_KW), ("gate_b", B_KW), ("norm_g", B_DV),
         ("ln_g", D), ("ln_b", D))


def _pack_small(parts, depth):
    rows = []
    for name, size in SMALL:
        flat = parts[name].reshape(depth * size).astype(F32)
        rows.append(jnp.pad(flat, (0, -(depth * size) % 128)).reshape(-1, 128))
    packed = jnp.concatenate(rows, axis=0)
    return jnp.pad(packed, ((0, -packed.shape[0] % 8), (0, 0)))


def _unpack_small(packed, depth, shapes):
    out, row = {}, 0
    for name, size in SMALL:
        nrow = -(-(depth * size) // 128)
        out[name] = packed[row:row + nrow].reshape(-1)[:depth * size].reshape(shapes[name])
        row += nrow
    return out


def kernel(x, mem, w_in, a_rel_bias, b_gate_w, b_gate_b, b_norm_g, w_mem_kv, w_out, ln_g, ln_b, loss_target, m_w_in, m_a_rel_bias, m_b_gate_w, m_b_gate_b, m_b_norm_g, m_w_mem_kv, m_w_out, m_ln_g, m_ln_b, v_w_in, v_a_rel_bias, v_b_gate_w, v_b_gate_b, v_b_norm_g, v_w_mem_kv, v_w_out, v_ln_g, v_ln_b):
    depth = w_in.shape[0]
    sh_in = w_in.shape[2]
    sh_gw = b_gate_w.shape[2]
    me = 4 * lax.axis_index("x") + 2 * lax.axis_index("y") + lax.axis_index("c")

    pipe = _Fsdp(w_in.astype(BF16), w_out.astype(BF16), w_mem_kv.astype(BF16), [b_gate_w])
    gate_w_full = jnp.transpose(pipe.extra[0], (1, 2, 0, 3)).reshape(depth, GATE_RANK, N_DEV * sh_gw)
    loss_dev, dx, grads = _local_step(x[0], mem[0], loss_target[0], pipe,
                                      a_rel_bias, gate_w_full, b_gate_b, b_norm_g, ln_g, ln_b)
    loss = lax.psum(loss_dev[0, 0], ("x", "y", "c"))

    recv = lambda n: [pipe.recv[l][n] for l in range(depth)]
    big = {"w_in": _adamw(recv("w_in"), w_in, m_w_in, v_w_in, 128, "adamw_w_in"),
           "w_out": _adamw(recv("w_out"), w_out, m_w_out, v_w_out, 64, "adamw_w_out"),
           "w_kv": _adamw(recv("w_kv"), w_mem_kv, m_w_mem_kv, v_w_mem_kv, 128, "adamw_w_kv")}

    shapes = {"rel": a_rel_bias.shape, "gate_w": (depth, GATE_RANK, N_DEV * sh_gw), "gate_b": b_gate_b.shape,
              "norm_g": b_norm_g.shape, "ln_g": ln_g.shape, "ln_b": ln_b.shape}
    part = _pack_small({n: jnp.stack([grads[l][n] for l in range(depth)]) for n, _ in SMALL}, depth)
    (all_parts,) = _exchange([part], False, "gather_small")
    zeros_gw = jnp.zeros(shapes["gate_w"], F32)
    w_s = _pack_small(dict(rel=a_rel_bias, gate_w=zeros_gw, gate_b=b_gate_b, norm_g=b_norm_g, ln_g=ln_g, ln_b=ln_b), depth)
    m_s = _pack_small(dict(rel=m_a_rel_bias, gate_w=zeros_gw, gate_b=m_b_gate_b, norm_g=m_b_norm_g, ln_g=m_ln_g, ln_b=m_ln_b), depth)
    v_s = _pack_small(dict(rel=v_a_rel_bias, gate_w=zeros_gw, gate_b=v_b_gate_b, norm_g=v_b_norm_g, ln_g=v_ln_g, ln_b=v_ln_b), depth)
    small = [_unpack_small(t[0], depth, shapes)
             for t in _adamw([all_parts], w_s[None], m_s[None], v_s[None], all_parts.shape[1], "adamw_small")]
    gw_grad = lax.dynamic_slice_in_dim(small[0]["gate_w"], me * sh_gw, sh_gw, axis=2).reshape(1, depth * GATE_RANK, sh_gw)
    flat = lambda t: t.reshape(1, depth * GATE_RANK, sh_gw)
    gw_res = [t.reshape(depth, GATE_RANK, sh_gw)
              for t in _adamw([gw_grad], flat(b_gate_w), flat(m_b_gate_w), flat(v_b_gate_w), depth * GATE_RANK, "adamw_gate_w")]

    def leaves(t):
        return (big["w_in"][t], small[t]["rel"], gw_res[t], small[t]["gate_b"], small[t]["norm_g"],
                big["w_kv"][t], big["w_out"][t], small[t]["ln_g"], small[t]["ln_b"])

    return (loss, dx[None]) + leaves(0) + leaves(1) + leaves(2) + leaves(3)
```

```python
import functools
import math

import numpy as np
import jax
import jax.numpy as jnp
from jax import lax
from jax.experimental import pallas as pl
from jax.experimental.pallas import tpu as pltpu

F32 = jnp.float32
BF16 = jnp.bfloat16

N_DEV = 8
D = 2048
CH = 64
LEFT = 8
MAX_REL = 128
N_MEM = 256
A_HEADS, A_DH, A_W = 8, 128, 1024
B_HEADS, B_DK, B_DV, B_KW, B_W = 4, 64, 128, 256, 512
GATE_RANK, GATE_TAU = 16, 16.0
M_HEADS, M_DH, M_W = 4, 128, 512
IN_W = 6672
NAT_SPLIT = 5648
H_W = 7168
H_PAD = H_W - IN_W
A_Q, A_K, A_V, A_Z = 0, 1024, 2048, 3072
B_BASE = 4096
M_BASE = 6144
ALPHA = (2.0 * 4) ** 0.25
LN_EPS = 1e-5
RMS_EPS = 1e-6
NEG = -1e30
QB = 256
KB = 3 * QB
ADAM_LR, ADAM_B1, ADAM_B2, ADAM_EPS, ADAM_WD, ADAM_STEP = 0.001, 0.9, 0.999, 1e-08, 0.01, 10
VMEM_MB = 1024 * 1024


def _params(sem, vmem_mb=48):
    return pltpu.CompilerParams(dimension_semantics=sem, vmem_limit_bytes=vmem_mb * VMEM_MB)


def _sigmoid(x):
    return 1.0 / (1.0 + jnp.exp(-x))


def _dot(a, b, ca, cb, precision=None):
    return lax.dot_general(a, b, (((ca,), (cb,)), ((), ())), preferred_element_type=F32, precision=precision)


MESH = pl.DeviceIdType.MESH


class _Comm:
    def __init__(self, xs, scatter, relations=tuple(range(N_DEV)), into=None):
        self.xs, self.scatter, self.n = list(xs), scatter, len(xs)
        self.relations, self.into = tuple(relations), list(into or [])
        self.out_shape = [jax.ShapeDtypeStruct((N_DEV,) + (x.shape[1:] if scatter else x.shape), x.dtype) for x in xs]
        self.specs = [pl.BlockSpec(memory_space=pltpu.HBM)] * self.n
        self.scratch = [pltpu.SemaphoreType.DMA((self.n, N_DEV)), pltpu.SemaphoreType.DMA((self.n, N_DEV)),
                        pltpu.SemaphoreType.DMA((self.n,))]

    def _copies(self, x_refs, o_refs, sems):
        send_sems, recv_sems, local_sems = sems
        mx, my, mc = lax.axis_index("x"), lax.axis_index("y"), lax.axis_index("c")
        me = 4 * mx + 2 * my + mc
        own, sends, arrivals = [], [], []
        for k in self.relations:
            if k == 0:
                own = [pltpu.make_async_copy(x_refs[a].at[me] if self.scatter else x_refs[a], o_refs[a].at[me],
                                             local_sems.at[a]) for a in range(self.n)]
                continue
            px = 1 - mx if k & 4 else mx
            py = 1 - my if k & 2 else my
            pc = 1 - mc if k & 1 else mc
            idx = 4 * px + 2 * py + pc
            for a in range(self.n):
                src = x_refs[a].at[idx] if self.scatter else x_refs[a]
                for dst, group in ((o_refs[a].at[me], sends), (o_refs[a].at[idx], arrivals)):
                    group.append(pltpu.make_async_remote_copy(
                        src_ref=src, dst_ref=dst, send_sem=send_sems.at[a, k], recv_sem=recv_sems.at[a, k],
                        device_id=(px, py, pc), device_id_type=MESH))
        return own, sends, arrivals

    def start(self, x_refs, o_refs, sems):
        own, sends, _ = self._copies(x_refs, o_refs, sems)
        for cp in own + sends:
            cp.start()

    def middle(self, x_refs, o_refs, sems):
        pass

    def finish(self, x_refs, o_refs, sems):
        own, sends, arrivals = self._copies(x_refs, o_refs, sems)
        for cp in sends:
            cp.wait_send()
        for cp in arrivals:
            cp.wait_recv()
        for cp in own:
            cp.wait()


class _Gather:
    into = ()

    def __init__(self, xs):
        self.xs, self.n = list(xs), len(xs)
        self.out_shape = [jax.ShapeDtypeStruct((N_DEV,) + x.shape, x.dtype) for x in xs]
        self.specs = [pl.BlockSpec(memory_space=pltpu.HBM)] * self.n
        self.scratch = [pltpu.SemaphoreType.DMA((self.n, N_DEV - 1)), pltpu.SemaphoreType.DMA((self.n, N_DEV - 1)),
                        pltpu.SemaphoreType.DMA((self.n,))]

    def _copies(self, x_refs, o_refs, sems):
        send_sems, recv_sems, local_sems = sems
        mx, my, mc = lax.axis_index("x"), lax.axis_index("y"), lax.axis_index("c")
        idx = lambda px, py, pc: 4 * px + 2 * py + pc
        me, sibling = (mx, my, mc), (mx, my, 1 - mc)
        chips = [(mx, 1 - my), (1 - mx, my), (1 - mx, 1 - my)]

        def copy(a, k, src, slot, to):
            return pltpu.make_async_remote_copy(
                src_ref=src, dst_ref=o_refs[a].at[idx(*slot)], send_sem=send_sems.at[a, k], recv_sem=recv_sems.at[a, k],
                device_id=to, device_id_type=MESH)

        c = dict(own=[], first=[], passed=[], ici_in=[], late_in=[])
        for a in range(self.n):
            x = x_refs[a]
            c["own"].append(pltpu.make_async_copy(x, o_refs[a].at[idx(*me)], local_sems.at[a]))
            c["first"].append(copy(a, 0, x, me, sibling))
            c["late_in"].append(copy(a, 0, x, sibling, sibling))
            for j, chip in enumerate(chips):
                c["first"].append(copy(a, 1 + j, x, me, (*chip, mc)))
                c["ici_in"].append(copy(a, 1 + j, x, (*chip, mc), (*chip, mc)))
                c["passed"].append(copy(a, 4 + j, o_refs[a].at[idx(*chip, mc)], (*chip, mc), sibling))
                c["late_in"].append(copy(a, 4 + j, x, (*chip, 1 - mc), sibling))
        return c

    def start(self, x_refs, o_refs, sems):
        c = self._copies(x_refs, o_refs, sems)
        for cp in c["own"] + c["first"]:
            cp.start()

    def middle(self, x_refs, o_refs, sems):
        c = self._copies(x_refs, o_refs, sems)
        for arrived, onward in zip(c["ici_in"], c["passed"]):
            arrived.wait_recv()
            onward.start()

    def finish(self, x_refs, o_refs, sems):
        c = self._copies(x_refs, o_refs, sems)
        for cp in c["first"] + c["passed"]:
            cp.wait_send()
        for cp in c["late_in"]:
            cp.wait_recv()
        for cp in c["own"]:
            cp.wait()


def _exchange(xs, scatter, name):
    comm = _Comm(xs, scatter) if scatter else _Gather(xs)

    def body(*refs):
        x_refs, o_refs, sems = refs[:comm.n], refs[comm.n:2 * comm.n], refs[2 * comm.n:]
        comm.start(x_refs, o_refs, sems)
        comm.middle(x_refs, o_refs, sems)
        comm.finish(x_refs, o_refs, sems)

    return pl.pallas_call(body, name=name, out_shape=tuple(comm.out_shape), in_specs=comm.specs,
                          out_specs=tuple(comm.specs), scratch_shapes=comm.scratch)(*comm.xs)


def _hosted(body, comm, *, name, grid, in_specs, out_specs, out_shape, scratch_shapes, compiler_params, inputs,
            input_output_aliases=None):
    out_specs, out_shape = tuple(out_specs), tuple(out_shape)
    aliases = input_output_aliases or {}
    if comm is None:
        outs = pl.pallas_call(body, name=name, grid=grid, in_specs=list(in_specs), out_specs=out_specs, out_shape=out_shape,
                              scratch_shapes=list(scratch_shapes), compiler_params=compiler_params,
                              input_output_aliases=aliases)(*inputs)
        return tuple(outs), ()
    ni, no, ns, nc = len(in_specs), len(out_specs), len(scratch_shapes), comm.n
    n_into = len(comm.into)
    aliases = {**aliases, **{ni + nc + a: no + a for a in range(n_into)}}

    def wrapped(*refs):
        ins, c_in = refs[:ni], refs[ni:ni + nc]
        refs = refs[ni + nc + n_into:]
        outs, c_out = refs[:no], refs[no:no + nc]
        scr, sems = refs[no + nc:no + nc + ns], refs[no + nc + ns:]
        first = functools.reduce(jnp.logical_and, [pl.program_id(d) == 0 for d in range(len(grid))])
        last = functools.reduce(jnp.logical_and, [pl.program_id(d) == grid[d] - 1 for d in range(len(grid))])

        step = functools.reduce(lambda acc, d: acc * grid[d] + pl.program_id(d), range(len(grid)), 0)
        n_steps = math.prod(grid)

        @pl.when(first)
        def _():
            comm.start(c_in, c_out, sems)

        body(*ins, *outs, *scr)

        @pl.when(step == max(3 * n_steps // 4, 1) - 1)
        def _():
            comm.middle(c_in, c_out, sems)

        @pl.when(last)
        def _():
            comm.finish(c_in, c_out, sems)

    params = pltpu.CompilerParams(dimension_semantics=("arbitrary",) * len(grid),
                                  vmem_limit_bytes=compiler_params.vmem_limit_bytes)
    outs = pl.pallas_call(wrapped, name=name, grid=grid, in_specs=list(in_specs) + comm.specs + comm.specs[:n_into],
                          out_specs=out_specs + tuple(comm.specs), out_shape=out_shape + tuple(comm.out_shape),
                          scratch_shapes=list(scratch_shapes) + comm.scratch, compiler_params=params,
                          input_output_aliases=aliases)(*inputs, *comm.xs, *comm.into)
    return tuple(outs[:no]), tuple(outs[no:])


def _mm(a, b, *, ta=False, tb=False, out_dtype, tm, tn, tk, name, adds=(), vmem_mb=48, comm=None):
    m = a.shape[1] if ta else a.shape[0]
    k = a.shape[0] if ta else a.shape[1]
    n = b.shape[0] if tb else b.shape[1]
    assert k == (b.shape[1] if tb else b.shape[0])
    tm, tn, tk = min(tm, m), min(tn, n), min(tk, k)
    assert m % tm == 0 and n % tn == 0 and k % tk == 0, (name, m, n, k)
    nk = k // tk
    n_add = len(adds)
    scales = [s for _, s in adds]

    def body(a_ref, b_ref, *rest):
        add_refs, o_ref = rest[:n_add], rest[n_add]
        kk = pl.program_id(2)

        def product():
            return _dot(a_ref[...].astype(BF16), b_ref[...].astype(BF16), 0 if ta else 1, 1 if tb else 0)

        def finish(r):
            for ref, s in zip(add_refs, scales):
                r = r + s * ref[...].astype(F32)
            o_ref[...] = r.astype(out_dtype)

        if nk == 1:
            finish(product())
            return
        acc_ref = rest[n_add + 1]

        @pl.when(kk == 0)
        def _():
            acc_ref[...] = product()

        if nk > 2:
            @pl.when((kk > 0) & (kk < nk - 1))
            def _():
                acc_ref[...] += product()

        @pl.when(kk == nk - 1)
        def _():
            finish(acc_ref[...] + product())

    a_spec = pl.BlockSpec((tk, tm), lambda i, j, kk: (kk, i)) if ta else pl.BlockSpec((tm, tk), lambda i, j, kk: (i, kk))
    b_spec = pl.BlockSpec((tn, tk), lambda i, j, kk: (j, kk)) if tb else pl.BlockSpec((tk, tn), lambda i, j, kk: (kk, j))
    add_specs = [pl.BlockSpec((tm, tn), lambda i, j, kk: (i, j)) for _ in adds]
    (out,), c_out = _hosted(
        body, comm, name=name,
        out_shape=[jax.ShapeDtypeStruct((m, n), out_dtype)],
        grid=(m // tm, n // tn, nk),
        in_specs=[a_spec, b_spec] + add_specs,
        out_specs=[pl.BlockSpec((tm, tn), lambda i, j, kk: (i, j))],
        scratch_shapes=[pltpu.VMEM((tm, tn), F32)] if nk > 1 else [],
        compiler_params=_params(("parallel", "parallel", "arbitrary"), vmem_mb),
        inputs=(a, b, *[x for x, _ in adds]))
    return out, c_out


NKB = KB // QB
LEAD = NKB - 1


def _band_bias(table):
    i = np.arange(QB)[:, None]
    j = np.arange(KB)[None, :]
    qc = i // CH + LEAD * QB // CH
    kc = j // CH
    valid = (kc <= qc) & (kc >= qc - LEFT)
    n = QB + KB
    c = np.arange(n)
    onehot = np.zeros((2 * MAX_REL + 1, n), np.float32)
    onehot[np.clip(LEAD * QB - (c - (QB - 1)), -MAX_REL, MAX_REL) + MAX_REL, c] = 1.0
    row = jnp.dot(table.astype(F32), jnp.asarray(onehot), precision=lax.Precision.HIGHEST)
    flow = jnp.tile(row, (1, QB))[:, :QB * (n - 1)].reshape(table.shape[0], QB, n - 1)
    return jnp.where(valid[None], flow[:, :, QB - 1:], NEG)


def _bias_grad(dbias):
    h, n = dbias.shape[0], KB + 1
    flat = jnp.pad(dbias.reshape(h, QB * KB), ((0, 0), (0, -(QB * KB) % n)))
    diag = flat.reshape(h, -1, n).sum(axis=1)
    c = np.arange(n)
    jm = np.where(c < LEAD * QB + CH, c, c - n)
    didx = np.clip(LEAD * QB - jm, -MAX_REL, MAX_REL) + MAX_REL
    onehot = np.zeros((n, 2 * MAX_REL + 1), np.float32)
    onehot[c, didx] = 1.0
    return jnp.dot(diag, jnp.asarray(onehot), precision=lax.Precision.HIGHEST)


def _attn_scores(q, k_refs, cs, bias_h, m, masked, scale):
    parts = []
    for t in range(NKB):
        sc = _dot(q, k_refs[t][:, cs], 1, 1) * scale
        parts.append(jnp.where(m + t - LEAD >= 0, sc, NEG) if masked else sc)
    return jnp.concatenate(parts, axis=1) + bias_h


def _attn_fwd(h, r_in, bias, comm=None):
    s = h.shape[0]
    nq = s // QB
    scale = A_DH ** -0.5

    def body(q_ref, *rest):
        k_refs, v_refs = rest[:NKB], rest[NKB:2 * NKB]
        bias_hbm, r_any, o_ref, lse_ref, bias_ref = rest[2 * NKB:]
        del r_any
        m = pl.program_id(0)

        @pl.when(m == 0)
        def _():
            pltpu.sync_copy(bias_hbm, bias_ref)

        def scores(hd, masked):
            cs = slice(hd * A_DH, (hd + 1) * A_DH)
            return _attn_scores(q_ref[:, cs], k_refs, cs, bias_ref[hd], m, masked, scale)

        def step(masked):
            sc_next = scores(0, masked)
            for hd in range(A_HEADS):
                cs = slice(hd * A_DH, (hd + 1) * A_DH)
                sc = sc_next
                if hd + 1 < A_HEADS:
                    sc_next = scores(hd + 1, masked)
                mx = jnp.max(sc, axis=1, keepdims=True)
                p = jnp.exp(sc - mx)
                l = jnp.sum(p, axis=1, keepdims=True)
                pb = p.astype(BF16)
                o = _dot(pb[:, :QB], v_refs[0][:, cs], 1, 0)
                for t in range(1, NKB):
                    o += _dot(pb[:, t * QB:(t + 1) * QB], v_refs[t][:, cs], 1, 0)
                o_ref[:, cs] = (o / l).astype(BF16)
                lse_ref[:, hd:hd + 1] = mx + jnp.log(l)

        pl.when(m < LEAD)(functools.partial(step, True))
        pl.when(m >= LEAD)(functools.partial(step, False))

    def kv_spec(col, t):
        return pl.BlockSpec((QB, A_W), lambda m: (jnp.maximum(m + t - LEAD, 0), col))

    (r, lse), c_out = _hosted(
        body, comm, name="attn_fwd",
        out_shape=(jax.ShapeDtypeStruct(r_in.shape, BF16), jax.ShapeDtypeStruct((s, A_HEADS), F32)),
        grid=(nq,),
        in_specs=[pl.BlockSpec((QB, A_W), lambda m: (m, 0))]
        + [kv_spec(1, t) for t in range(NKB)] + [kv_spec(2, t) for t in range(NKB)]
        + [pl.BlockSpec(memory_space=pl.ANY), pl.BlockSpec(memory_space=pl.ANY)],
        out_specs=(pl.BlockSpec((QB, A_W), lambda m: (m, 0)), pl.BlockSpec((QB, A_HEADS), lambda m: (m, 0))),
        scratch_shapes=[pltpu.VMEM((A_HEADS, QB, KB), F32)],
        input_output_aliases={2 * NKB + 2: 0},
        compiler_params=_params(("arbitrary",)),
        inputs=(h,) * (2 * NKB + 1) + (bias, r_in))
    return r, lse, c_out


def _attn_bwd(h, r, dy, lse, bias, dh_in, comm=None):
    s = h.shape[0]
    nq = s // QB
    scale = A_DH ** -0.5

    def body(q_ref, *rest):
        k_refs, v_refs = rest[:NKB], rest[NKB:2 * NKB]
        (z_ref, r_ref, dy_ref, lse_ref, zl_ref, rl_ref, dyl_ref, bias_hbm, dh_any,
         dh_ref, dbias_hbm, dk_acc, dv_acc, dq_ring, bias_ref, dbias_ref) = rest[2 * NKB:]
        del dh_any
        m = pl.program_id(0)

        def slot(b):
            return pl.multiple_of(lax.rem(b + NKB, NKB) * QB, QB)

        @pl.when(m == 0)
        def _():
            dk_acc[...] = jnp.zeros_like(dk_acc)
            dv_acc[...] = jnp.zeros_like(dv_acc)
            dq_ring[...] = jnp.zeros_like(dq_ring)
            dbias_ref[...] = jnp.zeros_like(dbias_ref)
            pltpu.sync_copy(bias_hbm, bias_ref)

        def step(masked):
            z = z_ref[...].astype(F32)
            do_all = dy_ref[...].astype(F32) * (z * _sigmoid(z))
            o_all = r_ref[...].astype(F32)

            def products(hd):
                cs = slice(hd * A_DH, (hd + 1) * A_DH)
                dob = do_all[:, cs].astype(BF16)
                sc = _attn_scores(q_ref[:, cs], k_refs, cs, bias_ref[hd], m, masked, scale)
                return sc, jnp.concatenate([_dot(dob, v_refs[t][:, cs], 1, 1) for t in range(NKB)], axis=1)

            ahead = products(0)
            for hd in range(A_HEADS):
                cs = slice(hd * A_DH, (hd + 1) * A_DH)
                q = q_ref[:, cs]
                do = do_all[:, cs]
                dob = do.astype(BF16)
                delta = jnp.sum(do * o_all[:, cs], axis=1, keepdims=True)
                sc, dp = ahead
                if hd + 1 < A_HEADS:
                    ahead = products(hd + 1)
                p = jnp.exp(sc - lse_ref[:, hd:hd + 1])
                ds = p * (dp - delta)
                dbias_ref[hd] += ds
                pb, dsb = p.astype(BF16), ds.astype(BF16)
                dq = jnp.zeros((QB, A_DH), F32)
                for t in range(NKB):
                    ts = slice(t * QB, (t + 1) * QB)
                    rows = pl.ds(slot(m - LEAD + t), QB)
                    dq += _dot(dsb[:, ts], k_refs[t][:, cs], 1, 0)
                    dk_acc[rows, cs] += _dot(dsb[:, ts], q, 0, 0) * scale
                    dv_acc[rows, cs] += _dot(pb[:, ts], dob, 0, 0)
                dq_ring[pl.ds(slot(m), QB), cs] = dq * scale

        pl.when(m < LEAD)(functools.partial(step, True))
        pl.when((m >= LEAD) & (m < nq))(functools.partial(step, False))

        done = pl.ds(slot(m - LEAD), QB)
        zl = zl_ref[...].astype(F32)
        sg = _sigmoid(zl)
        dz = dyl_ref[...].astype(F32) * rl_ref[...].astype(F32) * (sg * (1.0 + zl * (1.0 - sg)))
        dh_ref[:, A_Q:A_Q + A_W] = dq_ring[done, :].astype(BF16)
        dh_ref[:, A_K:A_K + A_W] = dk_acc[done, :].astype(BF16)
        dh_ref[:, A_V:A_V + A_W] = dv_acc[done, :].astype(BF16)
        dh_ref[:, A_Z:A_Z + A_W] = dz.astype(BF16)
        dk_acc[done, :] = jnp.zeros((QB, A_W), F32)
        dv_acc[done, :] = jnp.zeros((QB, A_W), F32)

        @pl.when(m == nq + LEAD - 1)
        def _():
            pltpu.sync_copy(dbias_ref, dbias_hbm)

    last = nq - 1

    def cur(col):
        return pl.BlockSpec((QB, A_W), lambda m: (jnp.minimum(m, last), col))

    def kv_spec(col, t):
        return pl.BlockSpec((QB, A_W), lambda m: (jnp.clip(m + t - LEAD, 0, last), col))

    def lag(col):
        return pl.BlockSpec((QB, A_W), lambda m: (jnp.clip(m - LEAD, 0, last), col))

    (dh, dbias), c_out = _hosted(
        body, comm, name="attn_bwd",
        out_shape=(jax.ShapeDtypeStruct(dh_in.shape, BF16), jax.ShapeDtypeStruct((A_HEADS, QB, KB), F32)),
        grid=(nq + LEAD,),
        in_specs=[cur(0)] + [kv_spec(1, t) for t in range(NKB)] + [kv_spec(2, t) for t in range(NKB)]
        + [cur(3), cur(0), cur(0), pl.BlockSpec((QB, A_HEADS), lambda m: (jnp.minimum(m, last), 0)),
           lag(3), lag(0), lag(0),
           pl.BlockSpec(memory_space=pl.ANY), pl.BlockSpec(memory_space=pl.ANY)],
        out_specs=(pl.BlockSpec((QB, 4 * A_W), lambda m: (jnp.clip(m - LEAD, 0, last), 0)),
                   pl.BlockSpec(memory_space=pl.ANY)),
        scratch_shapes=[pltpu.VMEM((KB, A_W), F32), pltpu.VMEM((KB, A_W), F32), pltpu.VMEM((KB, A_W), F32),
                        pltpu.VMEM((A_HEADS, QB, KB), F32), pltpu.VMEM((A_HEADS, QB, KB), F32)],
        input_output_aliases={2 * NKB + 9: 0},
        compiler_params=_params(("arbitrary",), 56),
        inputs=(h,) * (2 * NKB + 2) + (r, dy, lse, h, r, dy, bias, dh_in))
    return dh, dbias, c_out


GB = 256
N_PAIR = B_HEADS // 2


def _gla_gates(lr, gw_ref, gb_ref):
    logit = _dot(lr, gw_ref[...], 1, 0) + gb_ref[...]
    lg = (jnp.minimum(logit, 0.0) - jnp.log(1.0 + jnp.exp(-jnp.abs(logit)))) / GATE_TAU
    row = lax.broadcasted_iota(jnp.int32, (GB, GB), 0)
    col = lax.broadcasted_iota(jnp.int32, (GB, GB), 1)
    tri = jnp.where((row // CH == col // CH) & (col <= row), 1.0, 0.0).astype(F32)
    return logit, _dot(tri, lg, 1, 0, precision=lax.Precision.HIGHEST)


def _gla_factors(hb_ref, b_all, c):
    rs = slice(c * CH, (c + 1) * CH)
    q = hb_ref[rs, 0:B_KW].astype(F32) * (B_DK ** -0.5)
    k = hb_ref[rs, B_KW:2 * B_KW].astype(F32)
    b = b_all[rs]
    bm, bl = b[CH // 2:CH // 2 + 1, :], b[CH - 1:CH, :]
    e1, e2, eb, ek = jnp.exp(b - bm), jnp.exp(bm - b), jnp.exp(b), jnp.exp(bl - b)
    el = jnp.exp(bl)
    return dict(ql=q * e1, kl=k * e2, qu=q * e2, ku=k * e1, qt=q * eb, kh=k * ek, e1=e1, e2=e2, eb=eb, ek=ek, el=el)


class _GlaPairTools:
    def __init__(self, hb_ref, fs):
        self.hb_ref, self.fs = hb_ref, fs
        row = lax.broadcasted_iota(jnp.int32, (2 * CH, 2 * CH), 0)
        col = lax.broadcasted_iota(jnp.int32, (2 * CH, 2 * CH), 1)
        self.same = (row // CH) == (col // CH)
        self.lower = self.same & ((row % CH) >= (col % CH))
        self.upper = self.same & ((row % CH) < (col % CH))
        self.lower_t = self.same & ((col % CH) >= (row % CH))
        self.upper_t = self.same & ((col % CH) < (row % CH))

    def lanes(self, c, p, name):
        return self.fs[c][name][:, p * 128:(p + 1) * 128]

    def heads(self, c, p, name):
        x = self.lanes(c, p, name)
        return jnp.where(self.same, jnp.concatenate([x, x], axis=0), 0.0).astype(BF16)

    def twice(self, c, p, name):
        x = self.lanes(c, p, name).astype(BF16)
        return jnp.concatenate([x, x], axis=0)

    def vals(self, c, p):
        rows = slice(c * CH, (c + 1) * CH)
        return jnp.concatenate([self.hb_ref[rows, 512 + (2 * p + sh) * B_DV:512 + (2 * p + sh + 1) * B_DV]
                                for sh in range(2)], axis=0)

    def intra(self, c, p):
        lo = _dot(self.heads(c, p, "ql"), self.twice(c, p, "kl"), 1, 1)
        up = _dot(self.heads(c, p, "qu"), self.twice(c, p, "ku"), 1, 1)
        return jnp.where(self.lower, lo, jnp.where(self.upper, up, 0.0)).astype(BF16)

    def fold(self, x):
        x = jnp.where(self.same, x, 0.0)
        return x[:CH] + x[CH:]


def _gla_fwd(h, r_in, gw, gb, ng, comm=None):
    s = h.shape[0]
    nb = s // GB
    cpb = GB // CH

    def body(hb_ref, gw_ref, gb_ref, ng_ref, r_any, o_ref, opre_ref, st_ref, state):
        del r_any

        @pl.when(pl.program_id(0) == 0)
        def _():
            state[...] = jnp.zeros_like(state)

        _, b_all = _gla_gates(hb_ref[:, 1536:1664], gw_ref, gb_ref)
        fs = [_gla_factors(hb_ref, b_all, c) for c in range(cpb)]
        pairs = [(c, p) for c in range(cpb) for p in range(N_PAIR)]
        tools = _GlaPairTools(hb_ref, fs)
        a = {k: tools.intra(*k) for k in pairs}
        o_intra = {k: _dot(a[k], tools.vals(*k), 1, 0) for k in pairs}
        upd = {k: _dot(tools.vals(*k), tools.heads(*k, "kh"), 0, 0) for k in pairs}
        st = [state[p] for p in range(N_PAIR)]
        entering = {}
        for c, p in pairs:
            entering[c, p] = st[p]
            st_ref[c, p] = st[p]
            st[p] = st[p] * fs[c]["el"][:, p * 128:(p + 1) * 128] + upd[c, p]
        for p in range(N_PAIR):
            state[p] = st[p]
        for c, p in pairs:
            o2 = o_intra[c, p] + _dot(tools.heads(c, p, "qt"), entering[c, p].astype(BF16), 1, 1)
            for sh in range(2):
                o = o2[sh * CH:(sh + 1) * CH]
                rs, hs = slice(c * CH, (c + 1) * CH), slice((2 * p + sh) * B_DV, (2 * p + sh + 1) * B_DV)
                opre_ref[rs, hs] = o
                rinv = lax.rsqrt(jnp.mean(o * o, axis=1, keepdims=True) + RMS_EPS)
                o_ref[rs, hs] = (o * rinv * ng_ref[...]).astype(BF16)

    (r, opre, states), c_out = _hosted(
        body, comm, name="gla_fwd",
        out_shape=(jax.ShapeDtypeStruct(r_in.shape, BF16), jax.ShapeDtypeStruct((s, B_W), F32),
                   jax.ShapeDtypeStruct((s // CH, N_PAIR, 128, 128), F32)),
        grid=(nb,),
        in_specs=[pl.BlockSpec((GB, 2048), lambda i: (i, B_BASE // 2048)),
                  pl.BlockSpec((128, B_KW), lambda i: (0, 0)), pl.BlockSpec((1, B_KW), lambda i: (0, 0)),
                  pl.BlockSpec((1, B_DV), lambda i: (0, 0)), pl.BlockSpec(memory_space=pl.ANY)],
        out_specs=(pl.BlockSpec((GB, B_W), lambda i: (i, 1024 // B_W)), pl.BlockSpec((GB, B_W), lambda i: (i, 0)),
                   pl.BlockSpec((cpb, N_PAIR, 128, 128), lambda i: (i, 0, 0, 0))),
        scratch_shapes=[pltpu.VMEM((N_PAIR, 128, 128), F32)],
        input_output_aliases={4: 0},
        compiler_params=_params(("arbitrary",)),
        inputs=(h, gw, gb, ng, r_in))
    return r, opre, states, c_out


def _gla_bwd(h, dy, opre, states, gw, gb, ng, dh_in, comm=None):
    s = h.shape[0]
    nb = s // GB
    cpb = GB // CH

    def body(hb_ref, dy_ref, opre_ref, st_ref, gw_ref, gb_ref, ng_ref, dh_any,
             dh_ref, dgw_ref, dgb_ref, dng_ref, dstate, db_scr, do_scr):
        del dh_any

        @pl.when(pl.program_id(0) == 0)
        def _():
            dstate[...] = jnp.zeros_like(dstate)
            dgw_ref[...] = jnp.zeros_like(dgw_ref)
            dgb_ref[...] = jnp.zeros_like(dgb_ref)
            dng_ref[...] = jnp.zeros_like(dng_ref)

        lr = hb_ref[:, 1536:1664]
        logit, b_all = _gla_gates(lr, gw_ref, gb_ref)
        z = hb_ref[:, 1024:1536].astype(F32)
        sg = _sigmoid(z)
        dyb = dy_ref[...].astype(F32)
        dng = jnp.zeros((1, B_DV), F32)
        for hd in range(B_HEADS):
            hs = slice(hd * B_DV, (hd + 1) * B_DV)
            o = opre_ref[:, hs]
            rinv = lax.rsqrt(jnp.mean(o * o, axis=1, keepdims=True) + RMS_EPS)
            on = o * rinv
            dr = dyb[:, hs] * (z[:, hs] * sg[:, hs])
            dh_ref[:, 1024 + hd * B_DV:1024 + (hd + 1) * B_DV] = (
                dyb[:, hs] * (on * ng_ref[...]) * (sg[:, hs] * (1.0 + z[:, hs] * (1.0 - sg[:, hs])))).astype(BF16)
            dng += jnp.sum(dr * on, axis=0, keepdims=True)
            dn = dr * ng_ref[...]
            do_scr[:, hs] = rinv * (dn - on * jnp.mean(dn * on, axis=1, keepdims=True))
        dng_ref[...] += dng

        rowi = lax.broadcasted_iota(jnp.int32, (CH, 128), 0)
        fs = [_gla_factors(hb_ref, b_all, c) for c in range(cpb)]
        pairs = [(c, p) for c in reversed(range(cpb)) for p in range(N_PAIR)]
        tools = _GlaPairTools(hb_ref, fs)
        pair = tools.lanes

        def douts(c, p):
            rows = slice(c * CH, (c + 1) * CH)
            return jnp.concatenate([do_scr[rows, (2 * p + sh) * B_DV:(2 * p + sh + 1) * B_DV] for sh in range(2)],
                                   axis=0).astype(BF16)

        at, da, dat, dst_own, g_qt = {}, {}, {}, {}, {}
        for k in pairs:
            c, p = k
            lo_t = _dot(tools.twice(*k, "kl"), tools.heads(*k, "ql"), 1, 1)
            up_t = _dot(tools.twice(*k, "ku"), tools.heads(*k, "qu"), 1, 1)
            at[k] = jnp.where(tools.lower_t, lo_t, jnp.where(tools.upper_t, up_t, 0.0)).astype(BF16)
            da[k] = _dot(douts(*k), tools.vals(*k), 1, 1)
            dat[k] = _dot(tools.vals(*k), douts(*k), 1, 1)
            dst_own[k] = _dot(douts(*k), tools.heads(*k, "qt"), 0, 0)
            g_qt[k] = tools.fold(_dot(douts(*k), st_ref[c, p].astype(BF16), 1, 0))
        dv, g_ql, g_qu, g_kl, g_ku = {}, {}, {}, {}, {}
        for k in pairs:
            dv[k] = _dot(at[k], douts(*k), 1, 0)
            g_ql[k] = tools.fold(_dot(jnp.where(tools.lower, da[k], 0.0).astype(BF16), tools.twice(*k, "kl"), 1, 0))
            g_qu[k] = tools.fold(_dot(jnp.where(tools.upper, da[k], 0.0).astype(BF16), tools.twice(*k, "ku"), 1, 0))
            g_kl[k] = tools.fold(_dot(jnp.where(tools.lower_t, dat[k], 0.0).astype(BF16), tools.heads(*k, "ql"), 1, 0))
            g_ku[k] = tools.fold(_dot(jnp.where(tools.upper_t, dat[k], 0.0).astype(BF16), tools.heads(*k, "qu"), 1, 0))
        dst = [dstate[p] for p in range(N_PAIR)]
        leaving = {}
        for c, p in pairs:
            leaving[c, p] = dst[p]
            dst[p] = dst[p] * pair(c, p, "el") + dst_own[c, p]
        for p in range(N_PAIR):
            dstate[p] = dst[p]
        g_kh = {}
        for k in pairs:
            c, p = k
            dstb = leaving[k].astype(BF16)
            dv2 = dv[k] + _dot(tools.heads(*k, "kh"), dstb, 1, 1)
            for sh in range(2):
                hd = 2 * p + sh
                dh_ref[c * CH:(c + 1) * CH, 512 + hd * B_DV:512 + (hd + 1) * B_DV] = dv2[sh * CH:(sh + 1) * CH].astype(BF16)
            g_kh[k] = tools.fold(_dot(tools.vals(*k), dstb, 1, 0))
        for c in reversed(range(cpb)):
            rs = slice(c * CH, (c + 1) * CH)
            for p in range(N_PAIR):
                k = (c, p)
                dq = (g_ql[k] * pair(c, p, "e1") + g_qu[k] * pair(c, p, "e2") + g_qt[k] * pair(c, p, "eb")) * (B_DK ** -0.5)
                dk = g_kl[k] * pair(c, p, "e2") + g_ku[k] * pair(c, p, "e1") + g_kh[k] * pair(c, p, "ek")
                dkh_kh = g_kh[k] * pair(c, p, "kh")
                db = (g_ql[k] * pair(c, p, "ql") - g_qu[k] * pair(c, p, "qu") + g_qt[k] * pair(c, p, "qt")
                      - g_kl[k] * pair(c, p, "kl") + g_ku[k] * pair(c, p, "ku") - dkh_kh)
                db_last = (pair(c, p, "el") * jnp.sum(leaving[c, p] * st_ref[c, p], axis=0, keepdims=True)
                           + jnp.sum(dkh_kh, axis=0, keepdims=True))
                db = jnp.where(rowi == CH - 1, db + db_last, db)
                dh_ref[rs, p * 128:(p + 1) * 128] = dq.astype(BF16)
                dh_ref[rs, B_KW + p * 128:B_KW + (p + 1) * 128] = dk.astype(BF16)
                db_scr[rs, p * 128:(p + 1) * 128] = db

        row = lax.broadcasted_iota(jnp.int32, (GB, GB), 0)
        col = lax.broadcasted_iota(jnp.int32, (GB, GB), 1)
        trit = jnp.where((row // CH == col // CH) & (col >= row), 1.0, 0.0).astype(F32)
        dlg = _dot(trit, db_scr[...], 1, 0, precision=lax.Precision.HIGHEST)
        dlogit = dlg * (_sigmoid(-logit) / GATE_TAU)
        dlb = dlogit.astype(BF16)
        dgw_ref[...] += _dot(lr, dlb, 0, 0)
        dgb_ref[...] += jnp.sum(dlogit, axis=0, keepdims=True)
        dh_ref[:, 1536:1664] = _dot(dlb, gw_ref[...], 1, 1).astype(BF16)
        dh_ref[:, 1664:2048] = jnp.zeros((GB, 384), BF16)

    rev = lambda i: nb - 1 - i
    (dh, dgw, dgb, dng), c_out = _hosted(
        body, comm, name="gla_bwd",
        out_shape=(jax.ShapeDtypeStruct(dh_in.shape, BF16), jax.ShapeDtypeStruct((128, B_KW), F32),
                   jax.ShapeDtypeStruct((1, B_KW), F32), jax.ShapeDtypeStruct((1, B_DV), F32)),
        grid=(nb,),
        in_specs=[pl.BlockSpec((GB, 2048), lambda i: (rev(i), B_BASE // 2048)),
                  pl.BlockSpec((GB, B_W), lambda i: (rev(i), 1024 // B_W)),
                  pl.BlockSpec((GB, B_W), lambda i: (rev(i), 0)),
                  pl.BlockSpec((cpb, N_PAIR, 128, 128), lambda i: (rev(i), 0, 0, 0)),
                  pl.BlockSpec((128, B_KW), lambda i: (0, 0)), pl.BlockSpec((1, B_KW), lambda i: (0, 0)),
                  pl.BlockSpec((1, B_DV), lambda i: (0, 0)), pl.BlockSpec(memory_space=pl.ANY)],
        out_specs=(pl.BlockSpec((GB, 2048), lambda i: (rev(i), B_BASE // 2048)),
                   pl.BlockSpec((128, B_KW), lambda i: (0, 0)), pl.BlockSpec((1, B_KW), lambda i: (0, 0)),
                   pl.BlockSpec((1, B_DV), lambda i: (0, 0))),
        scratch_shapes=[pltpu.VMEM((N_PAIR, 128, 128), F32), pltpu.VMEM((GB, B_KW), F32), pltpu.VMEM((GB, B_W), F32)],
        input_output_aliases={7: 0},
        compiler_params=_params(("arbitrary",)),
        inputs=(h, dy, opre, states, gw, gb, ng, dh_in))
    return dh, dgw, dgb, dng, c_out


MB = 512


def _mem_probs(q, mk, scale):
    sc = _dot(q, mk, 1, 1) * scale
    p = jnp.exp(sc - jnp.max(sc, axis=1, keepdims=True))
    return p / jnp.sum(p, axis=1, keepdims=True)


def _mem_fwd(h, r_in, mkv):
    s = h.shape[0]
    scale = M_DH ** -0.5

    def body(q_ref, mkv_ref, r_any, o_ref):
        del r_any
        for hd in range(M_HEADS):
            cs = slice(hd * M_DH, (hd + 1) * M_DH)
            p = _mem_probs(q_ref[:, cs], mkv_ref[:, cs], scale)
            o_ref[:, cs] = _dot(p.astype(BF16), mkv_ref[:, M_W + hd * M_DH:M_W + (hd + 1) * M_DH], 1, 0).astype(BF16)

    return pl.pallas_call(
        body, name="mem_fwd",
        out_shape=jax.ShapeDtypeStruct(r_in.shape, BF16),
        grid=(s // MB,),
        in_specs=[pl.BlockSpec((MB, M_W), lambda i: (i, M_BASE // M_W)),
                  pl.BlockSpec((N_MEM, 2 * M_W), lambda i: (0, 0)), pl.BlockSpec(memory_space=pl.ANY)],
        out_specs=pl.BlockSpec((MB, M_W), lambda i: (i, 1536 // M_W)),
        input_output_aliases={2: 0},
        compiler_params=_params(("arbitrary",)),
    )(h, mkv, r_in)


def _mem_bwd(h, r, dy, mkv, dh_in):
    s = h.shape[0]
    scale = M_DH ** -0.5

    def body(q_ref, z_ref, r_ref, dy_ref, mkv_ref, dh_any, dh_ref, dmkv_ref):
        del dh_any

        @pl.when(pl.program_id(0) == 0)
        def _():
            dmkv_ref[...] = jnp.zeros_like(dmkv_ref)

        z = z_ref[...].astype(F32)
        sg = _sigmoid(z)
        dyv = dy_ref[...].astype(F32)
        do_all = dyv * (z * sg)
        dh_ref[:, M_W:2 * M_W] = (dyv * r_ref[...].astype(F32) * (sg * (1.0 + z * (1.0 - sg)))).astype(BF16)
        for hd in range(M_HEADS):
            cs = slice(hd * M_DH, (hd + 1) * M_DH)
            vs = slice(M_W + hd * M_DH, M_W + (hd + 1) * M_DH)
            q = q_ref[:, cs]
            p = _mem_probs(q, mkv_ref[:, cs], scale)
            dob = do_all[:, cs].astype(BF16)
            dp = _dot(dob, mkv_ref[:, vs], 1, 1)
            ds = p * (dp - jnp.sum(p * dp, axis=1, keepdims=True))
            dsb = ds.astype(BF16)
            dh_ref[:, cs] = (_dot(dsb, mkv_ref[:, cs], 1, 0) * scale).astype(BF16)
            dmkv_ref[:, cs] += _dot(dsb, q, 0, 0) * scale
            dmkv_ref[:, vs] += _dot(p.astype(BF16), dob, 0, 0)

    return pl.pallas_call(
        body, name="mem_bwd",
        out_shape=(jax.ShapeDtypeStruct(dh_in.shape, BF16), jax.ShapeDtypeStruct((N_MEM, 2 * M_W), F32)),
        grid=(s // MB,),
        in_specs=[pl.BlockSpec((MB, M_W), lambda i: (i, M_BASE // M_W)),
                  pl.BlockSpec((MB, M_W), lambda i: (i, M_BASE // M_W + 1)),
                  pl.BlockSpec((MB, M_W), lambda i: (i, 1536 // M_W)),
                  pl.BlockSpec((MB, M_W), lambda i: (i, 1536 // M_W)),
                  pl.BlockSpec((N_MEM, 2 * M_W), lambda i: (0, 0)), pl.BlockSpec(memory_space=pl.ANY)],
        out_specs=(pl.BlockSpec((MB, 2 * M_W), lambda i: (i, M_BASE // (2 * M_W))),
                   pl.BlockSpec((N_MEM, 2 * M_W), lambda i: (0, 0))),
        input_output_aliases={5: 0},
        compiler_params=_params(("arbitrary",)),
    )(h, h, r, dy, mkv, dh_in)


OB = 256


def _outproj_ln(h, r, w_out, x, ln_g, ln_b, target=None):
    s = h.shape[0]
    last = target is not None

    def body(za_ref, zb_ref, zm_ref, r_ref, w_ref, x_ref, g_ref, b_ref, *rest):
        if last:
            t_ref, xn_ref, xh_ref, rstd_ref, y_ref, l_ref = rest
        else:
            xn_ref, xh_ref, rstd_ref, y_ref = rest
        z = jnp.concatenate([za_ref[...], zb_ref[...], zm_ref[...]], axis=1).astype(F32)
        y = (r_ref[...].astype(F32) * (z * _sigmoid(z))).astype(BF16)
        y_ref[...] = y
        u = ALPHA * x_ref[...] + _dot(y, w_ref[...], 1, 0)
        mu = jnp.mean(u, axis=1, keepdims=True)
        uc = u - mu
        rstd = lax.rsqrt(jnp.mean(uc * uc, axis=1, keepdims=True) + LN_EPS)
        xh = uc * rstd
        xh_ref[...] = xh
        rstd_ref[...] = rstd
        xn = xh * g_ref[...] + b_ref[...]
        if last:
            @pl.when(pl.program_id(0) == 0)
            def _():
                l_ref[...] = jnp.zeros_like(l_ref)

            e = xn - t_ref[...]
            xn_ref[...] = e / D
            l_ref[...] += 0.5 * jnp.sum(jnp.mean(e * e, axis=1, keepdims=True))
        else:
            xn_ref[...] = xn

    row = lambda w, c: pl.BlockSpec((OB, w), lambda i: (i, c))
    vec = pl.BlockSpec((1, D), lambda i: (0, 0))
    full = jax.ShapeDtypeStruct((s, D), F32)
    return pl.pallas_call(
        body, name="outproj_ln_loss" if last else "outproj_ln",
        out_shape=(full, full, jax.ShapeDtypeStruct((s, 1), F32), jax.ShapeDtypeStruct((s, D), BF16))
        + ((jax.ShapeDtypeStruct((1, 128), F32),) if last else ()),
        grid=(s // OB,),
        in_specs=[row(A_W, A_Z // A_W), row(B_W, (B_BASE + 1024) // B_W), row(M_W, (M_BASE + M_W) // M_W), row(D, 0),
                  pl.BlockSpec((D, D), lambda i: (0, 0)), row(D, 0), vec, vec] + ([row(D, 0)] if last else []),
        out_specs=(row(D, 0), row(D, 0), pl.BlockSpec((OB, 1), lambda i: (i, 0)), row(D, 0))
        + ((pl.BlockSpec((1, 128), lambda i: (0, 0)),) if last else ()),
        compiler_params=_params(("arbitrary",), 56),
    )(h, h, h, r, w_out, x, ln_g, ln_b, *([target] if last else []))


def _ln_bwd(g, xh, rstd, ln_g):
    s = g.shape[0]

    def body(g_ref, xh_ref, rstd_ref, lg_ref, du_ref, dg_ref, db_ref):
        @pl.when(pl.program_id(0) == 0)
        def _():
            dg_ref[...] = jnp.zeros_like(dg_ref)
            db_ref[...] = jnp.zeros_like(db_ref)

        gv, xh = g_ref[...], xh_ref[...]
        dg_ref[...] += jnp.sum(gv * xh, axis=0, keepdims=True)
        db_ref[...] += jnp.sum(gv, axis=0, keepdims=True)
        dxh = gv * lg_ref[...]
        du_ref[...] = rstd_ref[...] * (dxh - jnp.mean(dxh, axis=1, keepdims=True)
                                       - xh * jnp.mean(dxh * xh, axis=1, keepdims=True))

    row = pl.BlockSpec((OB, D), lambda i: (i, 0))
    vec = pl.BlockSpec((1, D), lambda i: (0, 0))
    return pl.pallas_call(
        body, name="ln_bwd",
        out_shape=(jax.ShapeDtypeStruct((s, D), F32), jax.ShapeDtypeStruct((1, D), F32), jax.ShapeDtypeStruct((1, D), F32)),
        grid=(s // OB,),
        in_specs=[row, row, pl.BlockSpec((OB, 1), lambda i: (i, 0)), vec],
        out_specs=(row, vec, vec),
        compiler_params=_params(("arbitrary",)),
    )(g, xh, rstd, ln_g)


def _ln_bwd_dy(g, xh, rstd, ln_g, w_out):
    s = g.shape[0]

    def body(g_ref, xh_ref, rstd_ref, lg_ref, w_ref, dy_ref, du_ref, dg_ref, db_ref):
        @pl.when(pl.program_id(0) == 0)
        def _():
            dg_ref[...] = jnp.zeros_like(dg_ref)
            db_ref[...] = jnp.zeros_like(db_ref)

        gv, xh = g_ref[...], xh_ref[...]
        dg_ref[...] += jnp.sum(gv * xh, axis=0, keepdims=True)
        db_ref[...] += jnp.sum(gv, axis=0, keepdims=True)
        dxh = gv * lg_ref[...]
        du = rstd_ref[...] * (dxh - jnp.mean(dxh, axis=1, keepdims=True) - xh * jnp.mean(dxh * xh, axis=1, keepdims=True))
        du_ref[...] = du
        dy_ref[...] = _dot(du.astype(BF16), w_ref[...], 1, 1).astype(BF16)

    row = pl.BlockSpec((OB, D), lambda i: (i, 0))
    vec = pl.BlockSpec((1, D), lambda i: (0, 0))
    return pl.pallas_call(
        body, name="ln_bwd_dy",
        out_shape=(jax.ShapeDtypeStruct((s, D), BF16), jax.ShapeDtypeStruct((s, D), F32),
                   jax.ShapeDtypeStruct((1, D), F32), jax.ShapeDtypeStruct((1, D), F32)),
        grid=(s // OB,),
        in_specs=[row, row, pl.BlockSpec((OB, 1), lambda i: (i, 0)), vec, pl.BlockSpec((D, D), lambda i: (0, 0))],
        out_specs=(row, row, vec, vec),
        compiler_params=_params(("arbitrary",), 56),
    )(g, xh, rstd, ln_g, w_out)


def _loss_grad(y, target):
    s = y.shape[0]

    def body(y_ref, t_ref, l_ref, dy_ref):
        @pl.when(pl.program_id(0) == 0)
        def _():
            l_ref[...] = jnp.zeros_like(l_ref)

        e = y_ref[...] - t_ref[...]
        dy_ref[...] = e / D
        l_ref[...] += 0.5 * jnp.sum(jnp.mean(e * e, axis=1, keepdims=True))

    row = pl.BlockSpec((OB, D), lambda i: (i, 0))
    return pl.pallas_call(
        body, name="loss_grad",
        out_shape=(jax.ShapeDtypeStruct((1, 128), F32), jax.ShapeDtypeStruct((s, D), F32)),
        grid=(s // OB,),
        in_specs=[row, row],
        out_specs=(pl.BlockSpec((1, 128), lambda i: (0, 0)), row),
        compiler_params=_params(("arbitrary",)),
    )(y, target)


class _LocalWeights:
    def __init__(self, w_in_p, w_out_f, w_kv_f):
        self.w = list(zip(w_in_p, w_out_f, w_kv_f))
        self.depth = len(self.w)
        self.grads = [dict() for _ in self.w]

    def weights(self, l):
        return self.w[l]

    def host(self, where, l, payload=None):
        if payload is not None:
            self.grads[l][where] = payload
        return None

    def landed(self, where, l, outs):
        pass


def _shards_to_padded(raw):
    sh = raw.shape[2]
    parts = []
    for j in range(N_DEV):
        lo, hi = j * sh, (j + 1) * sh
        if lo < NAT_SPLIT < hi:
            parts += [raw[j][:, :NAT_SPLIT - lo], jnp.zeros((D, H_PAD), raw.dtype), raw[j][:, NAT_SPLIT - lo:]]
        else:
            parts.append(raw[j])
            if hi == NAT_SPLIT:
                parts.append(jnp.zeros((D, H_PAD), raw.dtype))
    return jnp.concatenate(parts, axis=1)


def _padded_to_shards(w):
    sh = IN_W // N_DEV
    pos = lambda c: c if c < NAT_SPLIT else c + H_PAD
    blocks = []
    for j in range(N_DEV):
        lo, hi = j * sh, (j + 1) * sh
        if lo < NAT_SPLIT < hi:
            blocks.append(jnp.concatenate([w[:, lo:NAT_SPLIT], w[:, pos(NAT_SPLIT):pos(hi - 1) + 1]], axis=1))
        else:
            blocks.append(w[:, pos(lo):pos(lo) + sh])
    return jnp.stack(blocks)


_PART_A = (0, 1, 2, 4, 7)
_PART_B = (3, 5, 6)


class _Fsdp:
    def __init__(self, w_in, w_out, w_kv, extra):
        self.sh = (w_in, w_out, w_kv)
        self.depth = w_in.shape[0]
        self.raw = [dict() for _ in range(self.depth)]
        self.recv = [dict() for _ in range(self.depth)]
        g_in, g_out, g_kv, *self.extra = _exchange([w_in[0], w_out[0], w_kv[0]] + list(extra), False, "gather_layer0")
        self.raw[0] = dict(w_in=g_in, w_out=g_out, w_kv=g_kv)

    def weights(self, l):
        raw = self.raw[l]
        return _shards_to_padded(raw["w_in"]), raw["w_out"].reshape(D, D), raw["w_kv"].reshape(D, 2 * M_W)

    def host(self, where, l, payload=None):
        w_in, w_out, w_kv = self.sh
        if where == "in_proj" and l + 1 < self.depth:
            return _Gather([w_in[l + 1], w_out[l + 1], w_kv[l + 1]])
        if where == "d_w_in":
            d_wout, d_wkv = payload
            return _Comm([d_wout.reshape(N_DEV, D // N_DEV, D), d_wkv.reshape(N_DEV, D // N_DEV, 2 * M_W)], True)
        if where == "d_x":
            self.blocks = _padded_to_shards(payload)
            return _Comm([self.blocks], True, relations=_PART_A if l > 0 else tuple(range(N_DEV)))
        if where == "attn_bwd" and l + 1 < self.depth:
            return _Comm([self.blocks], True, relations=_PART_B, into=[self.recv[l + 1]["w_in"]])
        return None

    def landed(self, where, l, outs):
        if where == "in_proj" and outs:
            self.raw[l + 1] = dict(zip(("w_in", "w_out", "w_kv"), outs))
        elif where == "d_w_in":
            self.recv[l]["w_out"], self.recv[l]["w_kv"] = outs
        elif where == "d_x":
            self.recv[l]["w_in"] = outs[0]
        elif where == "attn_bwd" and outs:
            self.recv[l + 1]["w_in"] = outs[0]


def _local_step(x, mem, target, pipe, rel, gate_w, gate_b, norm_g, ln_g, ln_b):
    depth = pipe.depth
    s = x.shape[0]
    saved = []
    xl = x
    for l in range(depth):
        w_in_p, w_out_f, w_kv_f = pipe.weights(l)
        hmat, landed = _mm(xl, w_in_p, out_dtype=BF16, tm=1024, tn=512, tk=D, name="in_proj", comm=pipe.host("in_proj", l))
        pipe.landed("in_proj", l, landed)
        mkv, _ = _mm(mem, w_kv_f, out_dtype=BF16, tm=N_MEM, tn=1024, tk=D, name="mem_kv")
        bias = _band_bias(rel[l])
        gw = jnp.zeros((128, B_KW), F32).at[:GATE_RANK].set(gate_w[l]).astype(BF16)
        gb, ng = gate_b[l][None, :], norm_g[l][None, :]
        r, lse, _ = _attn_fwd(hmat, lax.empty((s, D), BF16), bias)
        r, opre, states, _ = _gla_fwd(hmat, r, gw, gb, ng)
        r = _mem_fwd(hmat, r, mkv)
        xn, xh, rstd, y, *loss = _outproj_ln(hmat, r, w_out_f, xl, ln_g[l][None, :], ln_b[l][None, :],
                                             target if l == depth - 1 else None)
        saved.append(dict(x=xl, h=hmat, mkv=mkv, bias=bias, gw=gw, gb=gb, ng=ng, r=r, lse=lse, opre=opre,
                          states=states, xh=xh, rstd=rstd, y=y, w_in_p=w_in_p, w_out_f=w_out_f))
        xl = xn
    (loss,), g = loss, xl

    grads = [None] * depth
    for l in reversed(range(depth)):
        sv = saved[l]
        dy, du, d_lng, d_lnb = _ln_bwd_dy(g, sv["xh"], sv["rstd"], ln_g[l][None, :], sv["w_out_f"])
        d_wout, _ = _mm(sv["y"], du, ta=True, out_dtype=BF16, tm=1024, tn=1024, tk=1024, name="d_w_out")
        dh, dbias, landed = _attn_bwd(sv["h"], sv["r"], dy, sv["lse"], sv["bias"], lax.empty((s, H_W), BF16),
                                      pipe.host("attn_bwd", l))
        pipe.landed("attn_bwd", l, landed)
        dh, d_gw, d_gb, d_ng, _ = _gla_bwd(sv["h"], dy, sv["opre"], sv["states"], sv["gw"], sv["gb"], sv["ng"], dh)
        dh, d_mkv = _mem_bwd(sv["h"], sv["r"], dy, sv["mkv"], dh)
        d_wkv, _ = _mm(mem, d_mkv, ta=True, out_dtype=BF16, tm=1024, tn=1024, tk=N_MEM, name="d_w_kv")
        d_win, landed = _mm(sv["x"], dh, ta=True, out_dtype=BF16, tm=1024, tn=1792, tk=1024, name="d_w_in",
                            comm=pipe.host("d_w_in", l, (d_wout, d_wkv)))
        pipe.landed("d_w_in", l, landed)
        g, landed = _mm(dh, sv["w_in_p"], tb=True, out_dtype=F32, tm=1024, tn=1024, tk=1792, name="d_x",
                        adds=((du, ALPHA),), comm=pipe.host("d_x", l, d_win))
        pipe.landed("d_x", l, landed)
        grads[l] = dict(rel=_bias_grad(dbias), gate_w=d_gw[:GATE_RANK], gate_b=d_gb[0], norm_g=d_ng[0],
                        ln_g=d_lng[0], ln_b=d_lnb[0])
    return loss, g, grads


def _adamw(parts, w, m, v, rows_per_step, name):
    depth, rows, cols = w.shape
    n = parts[0].shape[0]
    tr = min(rows_per_step, rows)
    assert rows % tr == 0 and len(parts) == depth

    def body(*refs):
        p_refs = refs[:depth]
        w_ref, m_ref, v_ref, g_ref, d_ref, nm_ref, nv_ref = refs[depth:]
        for l in range(depth):
            @pl.when(pl.program_id(0) == l)
            def _(p_ref=p_refs[l]):
                g = p_ref[0].astype(F32)
                for j in range(1, n):
                    g = g + p_ref[j].astype(F32)
                nm = ADAM_B1 * m_ref[...] + (1.0 - ADAM_B1) * g
                nv = ADAM_B2 * v_ref[...] + (1.0 - ADAM_B2) * (g * g)
                m_hat = nm / (1.0 - ADAM_B1 ** ADAM_STEP)
                v_hat = nv / (1.0 - ADAM_B2 ** ADAM_STEP)
                g_ref[...] = g
                nm_ref[...] = nm
                nv_ref[...] = nv
                d_ref[...] = -ADAM_LR * (m_hat / (jnp.sqrt(v_hat) + ADAM_EPS) + ADAM_WD * w_ref[...])

    def part_spec(l):
        return pl.BlockSpec((n, tr, cols), lambda ll, i: (0, jnp.where(ll == l, i, 0), 0))

    blk = pl.BlockSpec((None, tr, cols), lambda ll, i: (ll, i, 0))
    shape = jax.ShapeDtypeStruct((depth, rows, cols), F32)
    return pl.pallas_call(
        body, name=name,
        out_shape=(shape, shape, shape, shape),
        grid=(depth, rows // tr),
        in_specs=[part_spec(l) for l in range(depth)] + [blk, blk, blk],
        out_specs=(blk, blk, blk, blk),
        compiler_params=_params(("arbitrary", "arbitrary")),
    )(*parts, w, m, v)


SMALL = (("rel", A_HEADS * (2 * MAX_REL + 1)), ("gate_w", GATE_RANK * B_KW), ("gate_b", B_KW), ("norm_g", B_DV),
         ("ln_g", D), ("ln_b", D))


def _pack_small(parts, depth):
    rows = []
    for name, size in SMALL:
        flat = parts[name].reshape(depth * size).astype(F32)
        rows.append(jnp.pad(flat, (0, -(depth * size) % 128)).reshape(-1, 128))
    packed = jnp.concatenate(rows, axis=0)
    return jnp.pad(packed, ((0, -packed.shape[0] % 8), (0, 0)))


def _unpack_small(packed, depth, shapes):
    out, row = {}, 0
    for name, size in SMALL:
        nrow = -(-(depth * size) // 128)
        out[name] = packed[row:row + nrow].reshape(-1)[:depth * size].reshape(shapes[name])
        row += nrow
    return out


def kernel(x, mem, w_in, a_rel_bias, b_gate_w, b_gate_b, b_norm_g, w_mem_kv, w_out, ln_g, ln_b, loss_target, m_w_in, m_a_rel_bias, m_b_gate_w, m_b_gate_b, m_b_norm_g, m_w_mem_kv, m_w_out, m_ln_g, m_ln_b, v_w_in, v_a_rel_bias, v_b_gate_w, v_b_gate_b, v_b_norm_g, v_w_mem_kv, v_w_out, v_ln_g, v_ln_b):
    depth = w_in.shape[0]
    sh_in = w_in.shape[2]
    sh_gw = b_gate_w.shape[2]
    me = 4 * lax.axis_index("x") + 2 * lax.axis_index("y") + lax.axis_index("c")

    pipe = _Fsdp(w_in.astype(BF16), w_out.astype(BF16), w_mem_kv.astype(BF16), [b_gate_w])
    gate_w_full = jnp.transpose(pipe.extra[0], (1, 2, 0, 3)).reshape(depth, GATE_RANK, N_DEV * sh_gw)
    loss_dev, dx, grads = _local_step(x[0], mem[0], loss_target[0], pipe,
                                      a_rel_bias, gate_w_full, b_gate_b, b_norm_g, ln_g, ln_b)
    loss = lax.psum(loss_dev[0, 0], ("x", "y", "c"))

    recv = lambda n: [pipe.recv[l][n] for l in range(depth)]
    big = {"w_in": _adamw(recv("w_in"), w_in, m_w_in, v_w_in, 128, "adamw_w_in"),
           "w_out": _adamw(recv("w_out"), w_out, m_w_out, v_w_out, 64, "adamw_w_out"),
           "w_kv": _adamw(recv("w_kv"), w_mem_kv, m_w_mem_kv, v_w_mem_kv, 128, "adamw_w_kv")}

    shapes = {"rel": a_rel_bias.shape, "gate_w": (depth, GATE_RANK, N_DEV * sh_gw), "gate_b": b_gate_b.shape,
              "norm_g": b_norm_g.shape, "ln_g": ln_g.shape, "ln_b": ln_b.shape}
    part = _pack_small({n: jnp.stack([grads[l][n] for l in range(depth)]) for n, _ in SMALL}, depth)
    (all_parts,) = _exchange([part], False, "gather_small")
    zeros_gw = jnp.zeros(shapes["gate_w"], F32)
    w_s = _pack_small(dict(rel=a_rel_bias, gate_w=zeros_gw, gate_b=b_gate_b, norm_g=b_norm_g, ln_g=ln_g, ln_b=ln_b), depth)
    m_s = _pack_small(dict(rel=m_a_rel_bias, gate_w=zeros_gw, gate_b=m_b_gate_b, norm_g=m_b_norm_g, ln_g=m_ln_g, ln_b=m_ln_b), depth)
    v_s = _pack_small(dict(rel=v_a_rel_bias, gate_w=zeros_gw, gate_b=v_b_gate_b, norm_g=v_b_norm_g, ln_g=v_ln_g, ln_b=v_ln_b), depth)
    small = [_unpack_small(t[0], depth, shapes)
             for t in _adamw([all_parts], w_s[None], m_s[None], v_s[None], all_parts.shape[1], "adamw_small")]
    gw_grad = lax.dynamic_slice_in_dim(small[0]["gate_w"], me * sh_gw, sh_gw, axis=2).reshape(1, depth * GATE_RANK, sh_gw)
    flat = lambda t: t.reshape(1, depth * GATE_RANK, sh_gw)
    gw_res = [t.reshape(depth, GATE_RANK, sh_gw)
              for t in _adamw([gw_grad], flat(b_gate_w), flat(m_b_gate_w), flat(v_b_gate_w), depth * GATE_RANK, "adamw_gate_w")]

    def leaves(t):
        return (big["w_in"][t], small[t]["rel"], gw_res[t], small[t]["gate_b"], small[t]["norm_g"],
                big["w_kv"][t], big["w_out"][t], small[t]["ln_g"], small[t]["ln_b"])

    return (loss, dx[None]) + leaves(0) + leaves(1) + leaves(2) + leaves(3)
```

```python
import functools
import math

import numpy as np
import jax
import jax.numpy as jnp
from jax import lax
from jax.experimental import pallas as pl
from jax.experimental.pallas import tpu as pltpu

F32 = jnp.float32
BF16 = jnp.bfloat16

N_DEV = 8
D = 2048
CH = 64
LEFT = 8
MAX_REL = 128
N_MEM = 256
A_HEADS, A_DH, A_W = 8, 128, 1024
B_HEADS, B_DK, B_DV, B_KW, B_W = 4, 64, 128, 256, 512
GATE_RANK, GATE_TAU = 16, 16.0
M_HEADS, M_DH, M_W = 4, 128, 512
IN_W = 6672
NAT_SPLIT = 5648
H_W = 7168
H_PAD = H_W - IN_W
A_Q, A_K, A_V, A_Z = 0, 1024, 2048, 3072
B_BASE = 4096
M_BASE = 6144
ALPHA = (2.0 * 4) ** 0.25
LN_EPS = 1e-5
RMS_EPS = 1e-6
NEG = -1e30
QB = 256
KB = 3 * QB
ADAM_LR, ADAM_B1, ADAM_B2, ADAM_EPS, ADAM_WD, ADAM_STEP = 0.001, 0.9, 0.999, 1e-08, 0.01, 10
VMEM_MB = 1024 * 1024


def _params(sem, vmem_mb=48):
    return pltpu.CompilerParams(dimension_semantics=sem, vmem_limit_bytes=vmem_mb * VMEM_MB)


def _sigmoid(x):
    return 1.0 / (1.0 + jnp.exp(-x))


def _dot(a, b, ca, cb, precision=None):
    return lax.dot_general(a, b, (((ca,), (cb,)), ((), ())), preferred_element_type=F32, precision=precision)


MESH = pl.DeviceIdType.MESH


class _Comm:
    def __init__(self, xs, scatter, relations=tuple(range(N_DEV)), into=None):
        self.xs, self.scatter, self.n = list(xs), scatter, len(xs)
        self.relations, self.into = tuple(relations), list(into or [])
        self.out_shape = [jax.ShapeDtypeStruct((N_DEV,) + (x.shape[1:] if scatter else x.shape), x.dtype) for x in xs]
        self.specs = [pl.BlockSpec(memory_space=pltpu.HBM)] * self.n
        self.scratch = [pltpu.SemaphoreType.DMA((self.n, N_DEV)), pltpu.SemaphoreType.DMA((self.n, N_DEV)),
                        pltpu.SemaphoreType.DMA((self.n,))]

    def _copies(self, x_refs, o_refs, sems):
        send_sems, recv_sems, local_sems = sems
        mx, my, mc = lax.axis_index("x"), lax.axis_index("y"), lax.axis_index("c")
        me = 4 * mx + 2 * my + mc
        own, sends, arrivals = [], [], []
        for k in self.relations:
            if k == 0:
                own = [pltpu.make_async_copy(x_refs[a].at[me] if self.scatter else x_refs[a], o_refs[a].at[me],
                                             local_sems.at[a]) for a in range(self.n)]
                continue
            px = 1 - mx if k & 4 else mx
            py = 1 - my if k & 2 else my
            pc = 1 - mc if k & 1 else mc
            idx = 4 * px + 2 * py + pc
            for a in range(self.n):
                src = x_refs[a].at[idx] if self.scatter else x_refs[a]
                for dst, group in ((o_refs[a].at[me], sends), (o_refs[a].at[idx], arrivals)):
                    group.append(pltpu.make_async_remote_copy(
                        src_ref=src, dst_ref=dst, send_sem=send_sems.at[a, k], recv_sem=recv_sems.at[a, k],
                        device_id=(px, py, pc), device_id_type=MESH))
        return own, sends, arrivals

    def start(self, x_refs, o_refs, sems):
        own, sends, _ = self._copies(x_refs, o_refs, sems)
        for cp in own + sends:
            cp.start()

    def middle(self, x_refs, o_refs, sems):
        pass

    def finish(self, x_refs, o_refs, sems):
        own, sends, arrivals = self._copies(x_refs, o_refs, sems)
        for cp in sends:
            cp.wait_send()
        for cp in arrivals:
            cp.wait_recv()
        for cp in own:
            cp.wait()


class _Gather:
    into = ()

    def __init__(self, xs):
        self.xs, self.n = list(xs), len(xs)
        self.out_shape = [jax.ShapeDtypeStruct((N_DEV,) + x.shape, x.dtype) for x in xs]
        self.specs = [pl.BlockSpec(memory_space=pltpu.HBM)] * self.n
        self.scratch = [pltpu.SemaphoreType.DMA((self.n, N_DEV - 1)), pltpu.SemaphoreType.DMA((self.n, N_DEV - 1)),
                        pltpu.SemaphoreType.DMA((self.n,))]

    def _copies(self, x_refs, o_refs, sems):
        send_sems, recv_sems, local_sems = sems
        mx, my, mc = lax.axis_index("x"), lax.axis_index("y"), lax.axis_index("c")
        idx = lambda px, py, pc: 4 * px + 2 * py + pc
        me, sibling = (mx, my, mc), (mx, my, 1 - mc)
        chips = [(mx, 1 - my), (1 - mx, my), (1 - mx, 1 - my)]

        def copy(a, k, src, slot, to):
            return pltpu.make_async_remote_copy(
                src_ref=src, dst_ref=o_refs[a].at[idx(*slot)], send_sem=send_sems.at[a, k], recv_sem=recv_sems.at[a, k],
                device_id=to, device_id_type=MESH)

        c = dict(own=[], first=[], passed=[], ici_in=[], late_in=[])
        for a in range(self.n):
            x = x_refs[a]
            c["own"].append(pltpu.make_async_copy(x, o_refs[a].at[idx(*me)], local_sems.at[a]))
            c["first"].append(copy(a, 0, x, me, sibling))
            c["late_in"].append(copy(a, 0, x, sibling, sibling))
            for j, chip in enumerate(chips):
                c["first"].append(copy(a, 1 + j, x, me, (*chip, mc)))
                c["ici_in"].append(copy(a, 1 + j, x, (*chip, mc), (*chip, mc)))
                c["passed"].append(copy(a, 4 + j, o_refs[a].at[idx(*chip, mc)], (*chip, mc), sibling))
                c["late_in"].append(copy(a, 4 + j, x, (*chip, 1 - mc), sibling))
        return c

    def start(self, x_refs, o_refs, sems):
        c = self._copies(x_refs, o_refs, sems)
        for cp in c["own"] + c["first"]:
            cp.start()

    def middle(self, x_refs, o_refs, sems):
        c = self._copies(x_refs, o_refs, sems)
        for arrived, onward in zip(c["ici_in"], c["passed"]):
            arrived.wait_recv()
            onward.start()

    def finish(self, x_refs, o_refs, sems):
        c = self._copies(x_refs, o_refs, sems)
        for cp in c["first"] + c["passed"]:
            cp.wait_send()
        for cp in c["late_in"]:
            cp.wait_recv()
        for cp in c["own"]:
            cp.wait()


def _exchange(xs, scatter, name):
    comm = _Comm(xs, scatter) if scatter else _Gather(xs)

    def body(*refs):
        x_refs, o_refs, sems = refs[:comm.n], refs[comm.n:2 * comm.n], refs[2 * comm.n:]
        comm.start(x_refs, o_refs, sems)
        comm.middle(x_refs, o_refs, sems)
        comm.finish(x_refs, o_refs, sems)

    return pl.pallas_call(body, name=name, out_shape=tuple(comm.out_shape), in_specs=comm.specs,
                          out_specs=tuple(comm.specs), scratch_shapes=comm.scratch)(*comm.xs)


def _hosted(body, comm, *, name, grid, in_specs, out_specs, out_shape, scratch_shapes, compiler_params, inputs,
            input_output_aliases=None):
    out_specs, out_shape = tuple(out_specs), tuple(out_shape)
    aliases = input_output_aliases or {}
    if comm is None:
        outs = pl.pallas_call(body, name=name, grid=grid, in_specs=list(in_specs), out_specs=out_specs, out_shape=out_shape,
                              scratch_shapes=list(scratch_shapes), compiler_params=compiler_params,
                              input_output_aliases=aliases)(*inputs)
        return tuple(outs), ()
    ni, no, ns, nc = len(in_specs), len(out_specs), len(scratch_shapes), comm.n
    n_into = len(comm.into)
    aliases = {**aliases, **{ni + nc + a: no + a for a in range(n_into)}}

    def wrapped(*refs):
        ins, c_in = refs[:ni], refs[ni:ni + nc]
        refs = refs[ni + nc + n_into:]
        outs, c_out = refs[:no], refs[no:no + nc]
        scr, sems = refs[no + nc:no + nc + ns], refs[no + nc + ns:]
        first = functools.reduce(jnp.logical_and, [pl.program_id(d) == 0 for d in range(len(grid))])
        last = functools.reduce(jnp.logical_and, [pl.program_id(d) == grid[d] - 1 for d in range(len(grid))])

        step = functools.reduce(lambda acc, d: acc * grid[d] + pl.program_id(d), range(len(grid)), 0)
        n_steps = math.prod(grid)

        @pl.when(first)
        def _():
            comm.start(c_in, c_out, sems)

        body(*ins, *outs, *scr)

        @pl.when(step == max(3 * n_steps // 4, 1) - 1)
        def _():
            comm.middle(c_in, c_out, sems)

        @pl.when(last)
        def _():
            comm.finish(c_in, c_out, sems)

    params = pltpu.CompilerParams(dimension_semantics=("arbitrary",) * len(grid),
                                  vmem_limit_bytes=compiler_params.vmem_limit_bytes)
    outs = pl.pallas_call(wrapped, name=name, grid=grid, in_specs=list(in_specs) + comm.specs + comm.specs[:n_into],
                          out_specs=out_specs + tuple(comm.specs), out_shape=out_shape + tuple(comm.out_shape),
                          scratch_shapes=list(scratch_shapes) + comm.scratch, compiler_params=params,
                          input_output_aliases=aliases)(*inputs, *comm.xs, *comm.into)
    return tuple(outs[:no]), tuple(outs[no:])


def _mm(a, b, *, ta=False, tb=False, out_dtype, tm, tn, tk, name, adds=(), vmem_mb=48, comm=None):
    m = a.shape[1] if ta else a.shape[0]
    k = a.shape[0] if ta else a.shape[1]
    n = b.shape[0] if tb else b.shape[1]
    assert k == (b.shape[1] if tb else b.shape[0])
    tm, tn, tk = min(tm, m), min(tn, n), min(tk, k)
    assert m % tm == 0 and n % tn == 0 and k % tk == 0, (name, m, n, k)
    nk = k // tk
    n_add = len(adds)
    scales = [s for _, s in adds]

    def body(a_ref, b_ref, *rest):
        add_refs, o_ref = rest[:n_add], rest[n_add]
        kk = pl.program_id(2)

        def product():
            return _dot(a_ref[...].astype(BF16), b_ref[...].astype(BF16), 0 if ta else 1, 1 if tb else 0)

        def finish(r):
            for ref, s in zip(add_refs, scales):
                r = r + s * ref[...].astype(F32)
            o_ref[...] = r.astype(out_dtype)

        if nk == 1:
            finish(product())
            return
        acc_ref = rest[n_add + 1]

        @pl.when(kk == 0)
        def _():
            acc_ref[...] = product()

        if nk > 2:
            @pl.when((kk > 0) & (kk < nk - 1))
            def _():
                acc_ref[...] += product()

        @pl.when(kk == nk - 1)
        def _():
            finish(acc_ref[...] + product())

    a_spec = pl.BlockSpec((tk, tm), lambda i, j, kk: (kk, i)) if ta else pl.BlockSpec((tm, tk), lambda i, j, kk: (i, kk))
    b_spec = pl.BlockSpec((tn, tk), lambda i, j, kk: (j, kk)) if tb else pl.BlockSpec((tk, tn), lambda i, j, kk: (kk, j))
    add_specs = [pl.BlockSpec((tm, tn), lambda i, j, kk: (i, j)) for _ in adds]
    (out,), c_out = _hosted(
        body, comm, name=name,
        out_shape=[jax.ShapeDtypeStruct((m, n), out_dtype)],
        grid=(m // tm, n // tn, nk),
        in_specs=[a_spec, b_spec] + add_specs,
        out_specs=[pl.BlockSpec((tm, tn), lambda i, j, kk: (i, j))],
        scratch_shapes=[pltpu.VMEM((tm, tn), F32)] if nk > 1 else [],
        compiler_params=_params(("parallel", "parallel", "arbitrary"), vmem_mb),
        inputs=(a, b, *[x for x, _ in adds]))
    return out, c_out


NKB = KB // QB
LEAD = NKB - 1


def _band_bias(table):
    i = np.arange(QB)[:, None]
    j = np.arange(KB)[None, :]
    qc = i // CH + LEAD * QB // CH
    kc = j // CH
    valid = (kc <= qc) & (kc >= qc - LEFT)
    n = QB + KB
    c = np.arange(n)
    onehot = np.zeros((2 * MAX_REL + 1, n), np.float32)
    onehot[np.clip(LEAD * QB - (c - (QB - 1)), -MAX_REL, MAX_REL) + MAX_REL, c] = 1.0
    row = jnp.dot(table.astype(F32), jnp.asarray(onehot), precision=lax.Precision.HIGHEST)
    flow = jnp.tile(row, (1, QB))[:, :QB * (n - 1)].reshape(table.shape[0], QB, n - 1)
    return jnp.where(valid[None], flow[:, :, QB - 1:], NEG)


def _bias_grad(dbias):
    h, n = dbias.shape[0], KB + 1
    flat = jnp.pad(dbias.reshape(h, QB * KB), ((0, 0), (0, -(QB * KB) % n)))
    diag = flat.reshape(h, -1, n).sum(axis=1)
    c = np.arange(n)
    jm = np.where(c < LEAD * QB + CH, c, c - n)
    didx = np.clip(LEAD * QB - jm, -MAX_REL, MAX_REL) + MAX_REL
    onehot = np.zeros((n, 2 * MAX_REL + 1), np.float32)
    onehot[c, didx] = 1.0
    return jnp.dot(diag, jnp.asarray(onehot), precision=lax.Precision.HIGHEST)


def _attn_scores(q, k_refs, cs, bias_h, m, masked, scale):
    parts = []
    for t in range(NKB):
        sc = _dot(q, k_refs[t][:, cs], 1, 1) * scale
        parts.append(jnp.where(m + t - LEAD >= 0, sc, NEG) if masked else sc)
    return jnp.concatenate(parts, axis=1) + bias_h


def _attn_fwd(h, r_in, bias, comm=None):
    s = h.shape[0]
    nq = s // QB
    scale = A_DH ** -0.5

    def body(q_ref, *rest):
        k_refs, v_refs = rest[:NKB], rest[NKB:2 * NKB]
        bias_hbm, r_any, o_ref, lse_ref, bias_ref = rest[2 * NKB:]
        del r_any
        m = pl.program_id(0)

        @pl.when(m == 0)
        def _():
            pltpu.sync_copy(bias_hbm, bias_ref)

        def scores(hd, masked):
            cs = slice(hd * A_DH, (hd + 1) * A_DH)
            return _attn_scores(q_ref[:, cs], k_refs, cs, bias_ref[hd], m, masked, scale)

        def step(masked):
            sc_next = scores(0, masked)
            for hd in range(A_HEADS):
                cs = slice(hd * A_DH, (hd + 1) * A_DH)
                sc = sc_next
                if hd + 1 < A_HEADS:
                    sc_next = scores(hd + 1, masked)
                mx = jnp.max(sc, axis=1, keepdims=True)
                p = jnp.exp(sc - mx)
                l = jnp.sum(p, axis=1, keepdims=True)
                pb = p.astype(BF16)
                o = _dot(pb[:, :QB], v_refs[0][:, cs], 1, 0)
                for t in range(1, NKB):
                    o += _dot(pb[:, t * QB:(t + 1) * QB], v_refs[t][:, cs], 1, 0)
                o_ref[:, cs] = (o / l).astype(BF16)
                lse_ref[:, hd:hd + 1] = mx + jnp.log(l)

        pl.when(m < LEAD)(functools.partial(step, True))
        pl.when(m >= LEAD)(functools.partial(step, False))

    def kv_spec(col, t):
        return pl.BlockSpec((QB, A_W), lambda m: (jnp.maximum(m + t - LEAD, 0), col))

    (r, lse), c_out = _hosted(
        body, comm, name="attn_fwd",
        out_shape=(jax.ShapeDtypeStruct(r_in.shape, BF16), jax.ShapeDtypeStruct((s, A_HEADS), F32)),
        grid=(nq,),
        in_specs=[pl.BlockSpec((QB, A_W), lambda m: (m, 0))]
        + [kv_spec(1, t) for t in range(NKB)] + [kv_spec(2, t) for t in range(NKB)]
        + [pl.BlockSpec(memory_space=pl.ANY), pl.BlockSpec(memory_space=pl.ANY)],
        out_specs=(pl.BlockSpec((QB, A_W), lambda m: (m, 0)), pl.BlockSpec((QB, A_HEADS), lambda m: (m, 0))),
        scratch_shapes=[pltpu.VMEM((A_HEADS, QB, KB), F32)],
        input_output_aliases={2 * NKB + 2: 0},
        compiler_params=_params(("arbitrary",)),
        inputs=(h,) * (2 * NKB + 1) + (bias, r_in))
    return r, lse, c_out


def _attn_bwd(h, r, dy, lse, bias, dh_in, comm=None):
    s = h.shape[0]
    nq = s // QB
    scale = A_DH ** -0.5

    def body(q_ref, *rest):
        k_refs, v_refs = rest[:NKB], rest[NKB:2 * NKB]
        (z_ref, r_ref, dy_ref, lse_ref, zl_ref, rl_ref, dyl_ref, bias_hbm, dh_any,
         dh_ref, dbias_hbm, dk_acc, dv_acc, dq_ring, bias_ref, dbias_ref) = rest[2 * NKB:]
        del dh_any
        m = pl.program_id(0)

        def slot(b):
            return pl.multiple_of(lax.rem(b + NKB, NKB) * QB, QB)

        @pl.when(m == 0)
        def _():
            dk_acc[...] = jnp.zeros_like(dk_acc)
            dv_acc[...] = jnp.zeros_like(dv_acc)
            dq_ring[...] = jnp.zeros_like(dq_ring)
            dbias_ref[...] = jnp.zeros_like(dbias_ref)
            pltpu.sync_copy(bias_hbm, bias_ref)

        def step(masked):
            z = z_ref[...].astype(F32)
            do_all = dy_ref[...].astype(F32) * (z * _sigmoid(z))
            o_all = r_ref[...].astype(F32)

            def products(hd):
                cs = slice(hd * A_DH, (hd + 1) * A_DH)
                dob = do_all[:, cs].astype(BF16)
                sc = _attn_scores(q_ref[:, cs], k_refs, cs, bias_ref[hd], m, masked, scale)
                return sc, jnp.concatenate([_dot(dob, v_refs[t][:, cs], 1, 1) for t in range(NKB)], axis=1)

            ahead = products(0)
            for hd in range(A_HEADS):
                cs = slice(hd * A_DH, (hd + 1) * A_DH)
                q = q_ref[:, cs]
                do = do_all[:, cs]
                dob = do.astype(BF16)
                delta = jnp.sum(do * o_all[:, cs], axis=1, keepdims=True)
                sc, dp = ahead
                if hd + 1 < A_HEADS:
                    ahead = products(hd + 1)
                p = jnp.exp(sc - lse_ref[:, hd:hd + 1])
                ds = p * (dp - delta)
                dbias_ref[hd] += ds
                pb, dsb = p.astype(BF16), ds.astype(BF16)
                dq = jnp.zeros((QB, A_DH), F32)
                for t in range(NKB):
                    ts = slice(t * QB, (t + 1) * QB)
                    rows = pl.ds(slot(m - LEAD + t), QB)
                    dq += _dot(dsb[:, ts], k_refs[t][:, cs], 1, 0)
                    dk_acc[rows, cs] += _dot(dsb[:, ts], q, 0, 0) * scale
                    dv_acc[rows, cs] += _dot(pb[:, ts], dob, 0, 0)
                dq_ring[pl.ds(slot(m), QB), cs] = dq * scale

        pl.when(m < LEAD)(functools.partial(step, True))
        pl.when((m >= LEAD) & (m < nq))(functools.partial(step, False))

        done = pl.ds(slot(m - LEAD), QB)
        zl = zl_ref[...].astype(F32)
        sg = _sigmoid(zl)
        dz = dyl_ref[...].astype(F32) * rl_ref[...].astype(F32) * (sg * (1.0 + zl * (1.0 - sg)))
        dh_ref[:, A_Q:A_Q + A_W] = dq_ring[done, :].astype(BF16)
        dh_ref[:, A_K:A_K + A_W] = dk_acc[done, :].astype(BF16)
        dh_ref[:, A_V:A_V + A_W] = dv_acc[done, :].astype(BF16)
        dh_ref[:, A_Z:A_Z + A_W] = dz.astype(BF16)
        dk_acc[done, :] = jnp.zeros((QB, A_W), F32)
        dv_acc[done, :] = jnp.zeros((QB, A_W), F32)

        @pl.when(m == nq + LEAD - 1)
        def _():
            pltpu.sync_copy(dbias_ref, dbias_hbm)

    last = nq - 1

    def cur(col):
        return pl.BlockSpec((QB, A_W), lambda m: (jnp.minimum(m, last), col))

    def kv_spec(col, t):
        return pl.BlockSpec((QB, A_W), lambda m: (jnp.clip(m + t - LEAD, 0, last), col))

    def lag(col):
        return pl.BlockSpec((QB, A_W), lambda m: (jnp.clip(m - LEAD, 0, last), col))

    (dh, dbias), c_out = _hosted(
        body, comm, name="attn_bwd",
        out_shape=(jax.ShapeDtypeStruct(dh_in.shape, BF16), jax.ShapeDtypeStruct((A_HEADS, QB, KB), F32)),
        grid=(nq + LEAD,),
        in_specs=[cur(0)] + [kv_spec(1, t) for t in range(NKB)] + [kv_spec(2, t) for t in range(NKB)]
        + [cur(3), cur(0), cur(0), pl.BlockSpec((QB, A_HEADS), lambda m: (jnp.minimum(m, last), 0)),
           lag(3), lag(0), lag(0),
           pl.BlockSpec(memory_space=pl.ANY), pl.BlockSpec(memory_space=pl.ANY)],
        out_specs=(pl.BlockSpec((QB, 4 * A_W), lambda m: (jnp.clip(m - LEAD, 0, last), 0)),
                   pl.BlockSpec(memory_space=pl.ANY)),
        scratch_shapes=[pltpu.VMEM((KB, A_W), F32), pltpu.VMEM((KB, A_W), F32), pltpu.VMEM((KB, A_W), F32),
                        pltpu.VMEM((A_HEADS, QB, KB), F32), pltpu.VMEM((A_HEADS, QB, KB), F32)],
        input_output_aliases={2 * NKB + 9: 0},
        compiler_params=_params(("arbitrary",), 56),
        inputs=(h,) * (2 * NKB + 2) + (r, dy, lse, h, r, dy, bias, dh_in))
    return dh, dbias, c_out


GB = 256
N_PAIR = B_HEADS // 2


def _gla_gates(lr, gw_ref, gb_ref):
    logit = _dot(lr, gw_ref[...], 1, 0) + gb_ref[...]
    lg = (jnp.minimum(logit, 0.0) - jnp.log(1.0 + jnp.exp(-jnp.abs(logit)))) / GATE_TAU
    row = lax.broadcasted_iota(jnp.int32, (GB, GB), 0)
    col = lax.broadcasted_iota(jnp.int32, (GB, GB), 1)
    tri = jnp.where((row // CH == col // CH) & (col <= row), 1.0, 0.0).astype(F32)
    return logit, _dot(tri, lg, 1, 0, precision=lax.Precision.HIGHEST)


def _gla_factors(hb_ref, b_all, c):
    rs = slice(c * CH, (c + 1) * CH)
    q = hb_ref[rs, 0:B_KW].astype(F32) * (B_DK ** -0.5)
    k = hb_ref[rs, B_KW:2 * B_KW].astype(F32)
    b = b_all[rs]
    bm, bl = b[CH // 2:CH // 2 + 1, :], b[CH - 1:CH, :]
    e1, e2, eb, ek = jnp.exp(b - bm), jnp.exp(bm - b), jnp.exp(b), jnp.exp(bl - b)
    el = jnp.exp(bl)
    return dict(ql=q * e1, kl=k * e2, qu=q * e2, ku=k * e1, qt=q * eb, kh=k * ek, e1=e1, e2=e2, eb=eb, ek=ek, el=el)


class _GlaPairTools:
    def __init__(self, hb_ref, fs):
        self.hb_ref, self.fs = hb_ref, fs
        row = lax.broadcasted_iota(jnp.int32, (2 * CH, 2 * CH), 0)
        col = lax.broadcasted_iota(jnp.int32, (2 * CH, 2 * CH), 1)
        self.same = (row // CH) == (col // CH)
        self.lower = self.same & ((row % CH) >= (col % CH))
        self.upper = self.same & ((row % CH) < (col % CH))
        self.lower_t = self.same & ((col % CH) >= (row % CH))
        self.upper_t = self.same & ((col % CH) < (row % CH))

    def lanes(self, c, p, name):
        return self.fs[c][name][:, p * 128:(p + 1) * 128]

    def heads(self, c, p, name):
        x = self.lanes(c, p, name)
        return jnp.where(self.same, jnp.concatenate([x, x], axis=0), 0.0).astype(BF16)

    def twice(self, c, p, name):
        x = self.lanes(c, p, name).astype(BF16)
        return jnp.concatenate([x, x], axis=0)

    def vals(self, c, p):
        rows = slice(c * CH, (c + 1) * CH)
        return jnp.concatenate([self.hb_ref[rows, 512 + (2 * p + sh) * B_DV:512 + (2 * p + sh + 1) * B_DV]
                                for sh in range(2)], axis=0)

    def intra(self, c, p):
        lo = _dot(self.heads(c, p, "ql"), self.twice(c, p, "kl"), 1, 1)
        up = _dot(self.heads(c, p, "qu"), self.twice(c, p, "ku"), 1, 1)
        return jnp.where(self.lower, lo, jnp.where(self.upper, up, 0.0)).astype(BF16)

    def fold(self, x):
        x = jnp.where(self.same, x, 0.0)
        return x[:CH] + x[CH:]


def _gla_fwd(h, r_in, gw, gb, ng, comm=None):
    s = h.shape[0]
    nb = s // GB
    cpb = GB // CH

    def body(hb_ref, gw_ref, gb_ref, ng_ref, r_any, o_ref, opre_ref, st_ref, state):
        del r_any

        @pl.when(pl.program_id(0) == 0)
        def _():
            state[...] = jnp.zeros_like(state)

        _, b_all = _gla_gates(hb_ref[:, 1536:1664], gw_ref, gb_ref)
        fs = [_gla_factors(hb_ref, b_all, c) for c in range(cpb)]
        pairs = [(c, p) for c in range(cpb) for p in range(N_PAIR)]
        tools = _GlaPairTools(hb_ref, fs)
        a = {k: tools.intra(*k) for k in pairs}
        o_intra = {k: _dot(a[k], tools.vals(*k), 1, 0) for k in pairs}
        upd = {k: _dot(tools.vals(*k), tools.heads(*k, "kh"), 0, 0) for k in pairs}
        st = [state[p] for p in range(N_PAIR)]
        entering = {}
        for c, p in pairs:
            entering[c, p] = st[p]
            st_ref[c, p] = st[p]
            st[p] = st[p] * fs[c]["el"][:, p * 128:(p + 1) * 128] + upd[c, p]
        for p in range(N_PAIR):
            state[p] = st[p]
        for c, p in pairs:
            o2 = o_intra[c, p] + _dot(tools.heads(c, p, "qt"), entering[c, p].astype(BF16), 1, 1)
            for sh in range(2):
                o = o2[sh * CH:(sh + 1) * CH]
                rs, hs = slice(c * CH, (c + 1) * CH), slice((2 * p + sh) * B_DV, (2 * p + sh + 1) * B_DV)
                opre_ref[rs, hs] = o
                rinv = lax.rsqrt(jnp.mean(o * o, axis=1, keepdims=True) + RMS_EPS)
                o_ref[rs, hs] = (o * rinv * ng_ref[...]).astype(BF16)

    (r, opre, states), c_out = _hosted(
        body, comm, name="gla_fwd",
        out_shape=(jax.ShapeDtypeStruct(r_in.shape, BF16), jax.ShapeDtypeStruct((s, B_W), F32),
                   jax.ShapeDtypeStruct((s // CH, N_PAIR, 128, 128), F32)),
        grid=(nb,),
        in_specs=[pl.BlockSpec((GB, 2048), lambda i: (i, B_BASE // 2048)),
                  pl.BlockSpec((128, B_KW), lambda i: (0, 0)), pl.BlockSpec((1, B_KW), lambda i: (0, 0)),
                  pl.BlockSpec((1, B_DV), lambda i: (0, 0)), pl.BlockSpec(memory_space=pl.ANY)],
        out_specs=(pl.BlockSpec((GB, B_W), lambda i: (i, 1024 // B_W)), pl.BlockSpec((GB, B_W), lambda i: (i, 0)),
                   pl.BlockSpec((cpb, N_PAIR, 128, 128), lambda i: (i, 0, 0, 0))),
        scratch_shapes=[pltpu.VMEM((N_PAIR, 128, 128), F32)],
        input_output_aliases={4: 0},
        compiler_params=_params(("arbitrary",)),
        inputs=(h, gw, gb, ng, r_in))
    return r, opre, states, c_out


def _gla_bwd(h, dy, opre, states, gw, gb, ng, dh_in, comm=None):
    s = h.shape[0]
    nb = s // GB
    cpb = GB // CH

    def body(hb_ref, dy_ref, opre_ref, st_ref, gw_ref, gb_ref, ng_ref, dh_any,
             dh_ref, dgw_ref, dgb_ref, dng_ref, dstate, db_scr, do_scr):
        del dh_any

        @pl.when(pl.program_id(0) == 0)
        def _():
            dstate[...] = jnp.zeros_like(dstate)
            dgw_ref[...] = jnp.zeros_like(dgw_ref)
            dgb_ref[...] = jnp.zeros_like(dgb_ref)
            dng_ref[...] = jnp.zeros_like(dng_ref)

        lr = hb_ref[:, 1536:1664]
        logit, b_all = _gla_gates(lr, gw_ref, gb_ref)
        z = hb_ref[:, 1024:1536].astype(F32)
        sg = _sigmoid(z)
        dyb = dy_ref[...].astype(F32)
        dng = jnp.zeros((1, B_DV), F32)
        for hd in range(B_HEADS):
            hs = slice(hd * B_DV, (hd + 1) * B_DV)
            o = opre_ref[:, hs]
            rinv = lax.rsqrt(jnp.mean(o * o, axis=1, keepdims=True) + RMS_EPS)
            on = o * rinv
            dr = dyb[:, hs] * (z[:, hs] * sg[:, hs])
            dh_ref[:, 1024 + hd * B_DV:1024 + (hd + 1) * B_DV] = (
                dyb[:, hs] * (on * ng_ref[...]) * (sg[:, hs] * (1.0 + z[:, hs] * (1.0 - sg[:, hs])))).astype(BF16)
            dng += jnp.sum(dr * on, axis=0, keepdims=True)
            dn = dr * ng_ref[...]
            do_scr[:, hs] = rinv * (dn - on * jnp.mean(dn * on, axis=1, keepdims=True))
        dng_ref[...] += dng

        rowi = lax.broadcasted_iota(jnp.int32, (CH, 128), 0)
        fs = [_gla_factors(hb_ref, b_all, c) for c in range(cpb)]
        pairs = [(c, p) for c in reversed(range(cpb)) for p in range(N_PAIR)]
        tools = _GlaPairTools(hb_ref, fs)
        pair = tools.lanes

        def douts(c, p):
            rows = slice(c * CH, (c + 1) * CH)
            return jnp.concatenate([do_scr[rows, (2 * p + sh) * B_DV:(2 * p + sh + 1) * B_DV] for sh in range(2)],
                                   axis=0).astype(BF16)

        at, da, dat, dst_own, g_qt = {}, {}, {}, {}, {}
        for k in pairs:
            c, p = k
            lo_t = _dot(tools.twice(*k, "kl"), tools.heads(*k, "ql"), 1, 1)
            up_t = _dot(tools.twice(*k, "ku"), tools.heads(*k, "qu"), 1, 1)
            at[k] = jnp.where(tools.lower_t, lo_t, jnp.where(tools.upper_t, up_t, 0.0)).astype(BF16)
            da[k] = _dot(douts(*k), tools.vals(*k), 1, 1)
            dat[k] = _dot(tools.vals(*k), douts(*k), 1, 1)
            dst_own[k] = _dot(douts(*k), tools.heads(*k, "qt"), 0, 0)
            g_qt[k] = tools.fold(_dot(douts(*k), st_ref[c, p].astype(BF16), 1, 0))
        dv, g_ql, g_qu, g_kl, g_ku = {}, {}, {}, {}, {}
        for k in pairs:
            dv[k] = _dot(at[k], douts(*k), 1, 0)
            g_ql[k] = tools.fold(_dot(jnp.where(tools.lower, da[k], 0.0).astype(BF16), tools.twice(*k, "kl"), 1, 0))
            g_qu[k] = tools.fold(_dot(jnp.where(tools.upper, da[k], 0.0).astype(BF16), tools.twice(*k, "ku"), 1, 0))
            g_kl[k] = tools.fold(_dot(jnp.where(tools.lower_t, dat[k], 0.0).astype(BF16), tools.heads(*k, "ql"), 1, 0))
            g_ku[k] = tools.fold(_dot(jnp.where(tools.upper_t, dat[k], 0.0).astype(BF16), tools.heads(*k, "qu"), 1, 0))
        dst = [dstate[p] for p in range(N_PAIR)]
        leaving = {}
        for c, p in pairs:
            leaving[c, p] = dst[p]
            dst[p] = dst[p] * pair(c, p, "el") + dst_own[c, p]
        for p in range(N_PAIR):
            dstate[p] = dst[p]
        g_kh = {}
        for k in pairs:
            c, p = k
            dstb = leaving[k].astype(BF16)
            dv2 = dv[k] + _dot(tools.heads(*k, "kh"), dstb, 1, 1)
            for sh in range(2):
                hd = 2 * p + sh
                dh_ref[c * CH:(c + 1) * CH, 512 + hd * B_DV:512 + (hd + 1) * B_DV] = dv2[sh * CH:(sh + 1) * CH].astype(BF16)
            g_kh[k] = tools.fold(_dot(tools.vals(*k), dstb, 1, 0))
        for c in reversed(range(cpb)):
            rs = slice(c * CH, (c + 1) * CH)
            for p in range(N_PAIR):
                k = (c, p)
                dq = (g_ql[k] * pair(c, p, "e1") + g_qu[k] * pair(c, p, "e2") + g_qt[k] * pair(c, p, "eb")) * (B_DK ** -0.5)
                dk = g_kl[k] * pair(c, p, "e2") + g_ku[k] * pair(c, p, "e1") + g_kh[k] * pair(c, p, "ek")
                dkh_kh = g_kh[k] * pair(c, p, "kh")
                db = (g_ql[k] * pair(c, p, "ql") - g_qu[k] * pair(c, p, "qu") + g_qt[k] * pair(c, p, "qt")
                      - g_kl[k] * pair(c, p, "kl") + g_ku[k] * pair(c, p, "ku") - dkh_kh)
                db_last = (pair(c, p, "el") * jnp.sum(leaving[c, p] * st_ref[c, p], axis=0, keepdims=True)
                           + jnp.sum(dkh_kh, axis=0, keepdims=True))
                db = jnp.where(rowi == CH - 1, db + db_last, db)
                dh_ref[rs, p * 128:(p + 1) * 128] = dq.astype(BF16)
                dh_ref[rs, B_KW + p * 128:B_KW + (p + 1) * 128] = dk.astype(BF16)
                db_scr[rs, p * 128:(p + 1) * 128] = db

        row = lax.broadcasted_iota(jnp.int32, (GB, GB), 0)
        col = lax.broadcasted_iota(jnp.int32, (GB, GB), 1)
        trit = jnp.where((row // CH == col // CH) & (col >= row), 1.0, 0.0).astype(F32)
        dlg = _dot(trit, db_scr[...], 1, 0, precision=lax.Precision.HIGHEST)
        dlogit = dlg * (_sigmoid(-logit) / GATE_TAU)
        dlb = dlogit.astype(BF16)
        dgw_ref[...] += _dot(lr, dlb, 0, 0)
        dgb_ref[...] += jnp.sum(dlogit, axis=0, keepdims=True)
        dh_ref[:, 1536:1664] = _dot(dlb, gw_ref[...], 1, 1).astype(BF16)
        dh_ref[:, 1664:2048] = jnp.zeros((GB, 384), BF16)

    rev = lambda i: nb - 1 - i
    (dh, dgw, dgb, dng), c_out = _hosted(
        body, comm, name="gla_bwd",
        out_shape=(jax.ShapeDtypeStruct(dh_in.shape, BF16), jax.ShapeDtypeStruct((128, B_KW), F32),
                   jax.ShapeDtypeStruct((1, B_KW), F32), jax.ShapeDtypeStruct((1, B_DV), F32)),
        grid=(nb,),
        in_specs=[pl.BlockSpec((GB, 2048), lambda i: (rev(i), B_BASE // 2048)),
                  pl.BlockSpec((GB, B_W), lambda i: (rev(i), 1024 // B_W)),
                  pl.BlockSpec((GB, B_W), lambda i: (rev(i), 0)),
                  pl.BlockSpec((cpb, N_PAIR, 128, 128), lambda i: (rev(i), 0, 0, 0)),
                  pl.BlockSpec((128, B_KW), lambda i: (0, 0)), pl.BlockSpec((1, B_KW), lambda i: (0, 0)),
                  pl.BlockSpec((1, B_DV), lambda i: (0, 0)), pl.BlockSpec(memory_space=pl.ANY)],
        out_specs=(pl.BlockSpec((GB, 2048), lambda i: (rev(i), B_BASE // 2048)),
                   pl.BlockSpec((128, B_KW), lambda i: (0, 0)), pl.BlockSpec((1, B_KW), lambda i: (0, 0)),
                   pl.BlockSpec((1, B_DV), lambda i: (0, 0))),
        scratch_shapes=[pltpu.VMEM((N_PAIR, 128, 128), F32), pltpu.VMEM((GB, B_KW), F32), pltpu.VMEM((GB, B_W), F32)],
        input_output_aliases={7: 0},
        compiler_params=_params(("arbitrary",)),
        inputs=(h, dy, opre, states, gw, gb, ng, dh_in))
    return dh, dgw, dgb, dng, c_out


MB = 512


def _mem_probs(q, mk, scale):
    sc = _dot(q, mk, 1, 1) * scale
    p = jnp.exp(sc - jnp.max(sc, axis=1, keepdims=True))
    return p / jnp.sum(p, axis=1, keepdims=True)


def _mem_fwd(h, r_in, mkv):
    s = h.shape[0]
    scale = M_DH ** -0.5

    def body(q_ref, mkv_ref, r_any, o_ref):
        del r_any
        for hd in range(M_HEADS):
            cs = slice(hd * M_DH, (hd + 1) * M_DH)
            p = _mem_probs(q_ref[:, cs], mkv_ref[:, cs], scale)
            o_ref[:, cs] = _dot(p.astype(BF16), mkv_ref[:, M_W + hd * M_DH:M_W + (hd + 1) * M_DH], 1, 0).astype(BF16)

    return pl.pallas_call(
        body, name="mem_fwd",
        out_shape=jax.ShapeDtypeStruct(r_in.shape, BF16),
        grid=(s // MB,),
        in_specs=[pl.BlockSpec((MB, M_W), lambda i: (i, M_BASE // M_W)),
                  pl.BlockSpec((N_MEM, 2 * M_W), lambda i: (0, 0)), pl.BlockSpec(memory_space=pl.ANY)],
        out_specs=pl.BlockSpec((MB, M_W), lambda i: (i, 1536 // M_W)),
        input_output_aliases={2: 0},
        compiler_params=_params(("arbitrary",)),
    )(h, mkv, r_in)


def _mem_bwd(h, r, dy, mkv, dh_in):
    s = h.shape[0]
    scale = M_DH ** -0.5

    def body(q_ref, z_ref, r_ref, dy_ref, mkv_ref, dh_any, dh_ref, dmkv_ref):
        del dh_any

        @pl.when(pl.program_id(0) == 0)
        def _():
            dmkv_ref[...] = jnp.zeros_like(dmkv_ref)

        z = z_ref[...].astype(F32)
        sg = _sigmoid(z)
        dyv = dy_ref[...].astype(F32)
        do_all = dyv * (z * sg)
        dh_ref[:, M_W:2 * M_W] = (dyv * r_ref[...].astype(F32) * (sg * (1.0 + z * (1.0 - sg)))).astype(BF16)
        for hd in range(M_HEADS):
            cs = slice(hd * M_DH, (hd + 1) * M_DH)
            vs = slice(M_W + hd * M_DH, M_W + (hd + 1) * M_DH)
            q = q_ref[:, cs]
            p = _mem_probs(q, mkv_ref[:, cs], scale)
            dob = do_all[:, cs].astype(BF16)
            dp = _dot(dob, mkv_ref[:, vs], 1, 1)
            ds = p * (dp - jnp.sum(p * dp, axis=1, keepdims=True))
            dsb = ds.astype(BF16)
            dh_ref[:, cs] = (_dot(dsb, mkv_ref[:, cs], 1, 0) * scale).astype(BF16)
            dmkv_ref[:, cs] += _dot(dsb, q, 0, 0) * scale
            dmkv_ref[:, vs] += _dot(p.astype(BF16), dob, 0, 0)

    return pl.pallas_call(
        body, name="mem_bwd",
        out_shape=(jax.ShapeDtypeStruct(dh_in.shape, BF16), jax.ShapeDtypeStruct((N_MEM, 2 * M_W), F32)),
        grid=(s // MB,),
        in_specs=[pl.BlockSpec((MB, M_W), lambda i: (i, M_BASE // M_W)),
                  pl.BlockSpec((MB, M_W), lambda i: (i, M_BASE // M_W + 1)),
                  pl.BlockSpec((MB, M_W), lambda i: (i, 1536 // M_W)),
                  pl.BlockSpec((MB, M_W), lambda i: (i, 1536 // M_W)),
                  pl.BlockSpec((N_MEM, 2 * M_W), lambda i: (0, 0)), pl.BlockSpec(memory_space=pl.ANY)],
        out_specs=(pl.BlockSpec((MB, 2 * M_W), lambda i: (i, M_BASE // (2 * M_W))),
                   pl.BlockSpec((N_MEM, 2 * M_W), lambda i: (0, 0))),
        input_output_aliases={5: 0},
        compiler_params=_params(("arbitrary",)),
    )(h, h, r, dy, mkv, dh_in)


OB = 256


def _outproj_ln(h, r, w_out, x, ln_g, ln_b, target=None):
    s = h.shape[0]
    last = target is not None

    def body(za_ref, zb_ref, zm_ref, r_ref, w_ref, x_ref, g_ref, b_ref, *rest):
        if last:
            t_ref, xn_ref, xh_ref, rstd_ref, y_ref, l_ref = rest
        else:
            xn_ref, xh_ref, rstd_ref, y_ref = rest
        z = jnp.concatenate([za_ref[...], zb_ref[...], zm_ref[...]], axis=1).astype(F32)
        y = (r_ref[...].astype(F32) * (z * _sigmoid(z))).astype(BF16)
        y_ref[...] = y
        u = ALPHA * x_ref[...] + _dot(y, w_ref[...], 1, 0)
        mu = jnp.mean(u, axis=1, keepdims=True)
        uc = u - mu
        rstd = lax.rsqrt(jnp.mean(uc * uc, axis=1, keepdims=True) + LN_EPS)
        xh = uc * rstd
        xh_ref[...] = xh
        rstd_ref[...] = rstd
        xn = xh * g_ref[...] + b_ref[...]
        if last:
            @pl.when(pl.program_id(0) == 0)
            def _():
                l_ref[...] = jnp.zeros_like(l_ref)

            e = xn - t_ref[...]
            xn_ref[...] = e / D
            l_ref[...] += 0.5 * jnp.sum(jnp.mean(e * e, axis=1, keepdims=True))
        else:
            xn_ref[...] = xn

    row = lambda w, c: pl.BlockSpec((OB, w), lambda i: (i, c))
    vec = pl.BlockSpec((1, D), lambda i: (0, 0))
    full = jax.ShapeDtypeStruct((s, D), F32)
    return pl.pallas_call(
        body, name="outproj_ln_loss" if last else "outproj_ln",
        out_shape=(full, full, jax.ShapeDtypeStruct((s, 1), F32), jax.ShapeDtypeStruct((s, D), BF16))
        + ((jax.ShapeDtypeStruct((1, 128), F32),) if last else ()),
        grid=(s // OB,),
        in_specs=[row(A_W, A_Z // A_W), row(B_W, (B_BASE + 1024) // B_W), row(M_W, (M_BASE + M_W) // M_W), row(D, 0),
                  pl.BlockSpec((D, D), lambda i: (0, 0)), row(D, 0), vec, vec] + ([row(D, 0)] if last else []),
        out_specs=(row(D, 0), row(D, 0), pl.BlockSpec((OB, 1), lambda i: (i, 0)), row(D, 0))
        + ((pl.BlockSpec((1, 128), lambda i: (0, 0)),) if last else ()),
        compiler_params=_params(("arbitrary",), 56),
    )(h, h, h, r, w_out, x, ln_g, ln_b, *([target] if last else []))


def _ln_bwd(g, xh, rstd, ln_g):
    s = g.shape[0]

    def body(g_ref, xh_ref, rstd_ref, lg_ref, du_ref, dg_ref, db_ref):
        @pl.when(pl.program_id(0) == 0)
        def _():
            dg_ref[...] = jnp.zeros_like(dg_ref)
            db_ref[...] = jnp.zeros_like(db_ref)

        gv, xh = g_ref[...], xh_ref[...]
        dg_ref[...] += jnp.sum(gv * xh, axis=0, keepdims=True)
        db_ref[...] += jnp.sum(gv, axis=0, keepdims=True)
        dxh = gv * lg_ref[...]
        du_ref[...] = rstd_ref[...] * (dxh - jnp.mean(dxh, axis=1, keepdims=True)
                                       - xh * jnp.mean(dxh * xh, axis=1, keepdims=True))

    row = pl.BlockSpec((OB, D), lambda i: (i, 0))
    vec = pl.BlockSpec((1, D), lambda i: (0, 0))
    return pl.pallas_call(
        body, name="ln_bwd",
        out_shape=(jax.ShapeDtypeStruct((s, D), F32), jax.ShapeDtypeStruct((1, D), F32), jax.ShapeDtypeStruct((1, D), F32)),
        grid=(s // OB,),
        in_specs=[row, row, pl.BlockSpec((OB, 1), lambda i: (i, 0)), vec],
        out_specs=(row, vec, vec),
        compiler_params=_params(("arbitrary",)),
    )(g, xh, rstd, ln_g)


def _ln_bwd_dy(g, xh, rstd, ln_g, w_out, y):
    s = g.shape[0]
    n = s // OB

    def body(g_ref, xh_ref, rstd_ref, lg_ref, w_ref, y_ref, dy_ref, du_ref, dg_ref, db_ref, dw_hbm, dw_acc, stage):
        i = pl.program_id(0)

        @pl.when(i == 0)
        def _():
            dg_ref[...] = jnp.zeros_like(dg_ref)
            db_ref[...] = jnp.zeros_like(db_ref)
            dw_acc[...] = jnp.zeros_like(dw_acc)

        gv, xh = g_ref[...], xh_ref[...]
        dg_ref[...] += jnp.sum(gv * xh, axis=0, keepdims=True)
        db_ref[...] += jnp.sum(gv, axis=0, keepdims=True)
        dxh = gv * lg_ref[...]
        du = rstd_ref[...] * (dxh - jnp.mean(dxh, axis=1, keepdims=True) - xh * jnp.mean(dxh * xh, axis=1, keepdims=True))
        du_ref[...] = du
        dub = du.astype(BF16)
        dy_ref[...] = _dot(dub, w_ref[...], 1, 1).astype(BF16)
        dw_acc[...] += _dot(y_ref[...], dub, 0, 0)

        @pl.when(i == n - 1)
        def _():
            for c in range(D // OB):
                stage[...] = dw_acc[c * OB:(c + 1) * OB, :].astype(BF16)
                pltpu.sync_copy(stage, dw_hbm.at[c * OB:(c + 1) * OB, :])

    row = pl.BlockSpec((OB, D), lambda i: (i, 0))
    vec = pl.BlockSpec((1, D), lambda i: (0, 0))
    return pl.pallas_call(
        body, name="ln_bwd_dy",
        out_shape=(jax.ShapeDtypeStruct((s, D), BF16), jax.ShapeDtypeStruct((s, D), F32),
                   jax.ShapeDtypeStruct((1, D), F32), jax.ShapeDtypeStruct((1, D), F32), jax.ShapeDtypeStruct((D, D), BF16)),
        grid=(n,),
        in_specs=[row, row, pl.BlockSpec((OB, 1), lambda i: (i, 0)), vec, pl.BlockSpec((D, D), lambda i: (0, 0)), row],
        out_specs=(row, row, vec, vec, pl.BlockSpec(memory_space=pl.ANY)),
        scratch_shapes=[pltpu.VMEM((D, D), F32), pltpu.VMEM((OB, D), BF16)],
        compiler_params=_params(("arbitrary",), 60),
    )(g, xh, rstd, ln_g, w_out, y)


def _loss_grad(y, target):
    s = y.shape[0]

    def body(y_ref, t_ref, l_ref, dy_ref):
        @pl.when(pl.program_id(0) == 0)
        def _():
            l_ref[...] = jnp.zeros_like(l_ref)

        e = y_ref[...] - t_ref[...]
        dy_ref[...] = e / D
        l_ref[...] += 0.5 * jnp.sum(jnp.mean(e * e, axis=1, keepdims=True))

    row = pl.BlockSpec((OB, D), lambda i: (i, 0))
    return pl.pallas_call(
        body, name="loss_grad",
        out_shape=(jax.ShapeDtypeStruct((1, 128), F32), jax.ShapeDtypeStruct((s, D), F32)),
        grid=(s // OB,),
        in_specs=[row, row],
        out_specs=(pl.BlockSpec((1, 128), lambda i: (0, 0)), row),
        compiler_params=_params(("arbitrary",)),
    )(y, target)


class _LocalWeights:
    def __init__(self, w_in_p, w_out_f, w_kv_f):
        self.w = list(zip(w_in_p, w_out_f, w_kv_f))
        self.depth = len(self.w)
        self.grads = [dict() for _ in self.w]

    def weights(self, l):
        return self.w[l]

    def host(self, where, l, payload=None):
        if payload is not None:
            self.grads[l][where] = payload
        return None

    def landed(self, where, l, outs):
        pass


def _shard_pieces():
    sh = IN_W // N_DEV
    pieces = []
    for j in range(N_DEV):
        lo, hi = j * sh, (j + 1) * sh
        cuts = [lo, NAT_SPLIT, hi] if lo < NAT_SPLIT < hi else [lo, hi]
        for a, b in zip(cuts[:-1], cuts[1:]):
            pieces.append((j, a - lo, a if a < NAT_SPLIT else a + H_PAD, b - a))
    return pieces


RB = 256


def _shards_to_padded(raw):
    def body(x_ref, o_ref):
        for j, src, dst, width in _shard_pieces():
            o_ref[:, dst:dst + width] = x_ref[j, :, src:src + width]
        o_ref[:, NAT_SPLIT:NAT_SPLIT + H_PAD] = jnp.zeros((RB, H_PAD), o_ref.dtype)

    return pl.pallas_call(
        body, name="place_w_in", out_shape=jax.ShapeDtypeStruct((D, H_W), raw.dtype), grid=(D // RB,),
        in_specs=[pl.BlockSpec((N_DEV, RB, raw.shape[2]), lambda i: (0, i, 0))],
        out_specs=pl.BlockSpec((RB, H_W), lambda i: (i, 0)),
        compiler_params=_params(("parallel",)),
    )(raw)


def _padded_to_shards(w):
    sh = IN_W // N_DEV

    def body(x_ref, o_ref):
        for j, src, dst, width in _shard_pieces():
            o_ref[j, :, src:src + width] = x_ref[:, dst:dst + width]

    return pl.pallas_call(
        body, name="split_d_w_in", out_shape=jax.ShapeDtypeStruct((N_DEV, D, sh), w.dtype), grid=(D // RB,),
        in_specs=[pl.BlockSpec((RB, H_W), lambda i: (i, 0))],
        out_specs=pl.BlockSpec((N_DEV, RB, sh), lambda i: (0, i, 0)),
        compiler_params=_params(("parallel",)),
    )(w)


_PART_A = (0, 1, 2, 4, 7)
_PART_B = (3, 5, 6)


class _Fsdp:
    def __init__(self, w_in, w_out, w_kv, extra):
        self.sh = (w_in, w_out, w_kv)
        self.depth = w_in.shape[0]
        self.raw = [dict() for _ in range(self.depth)]
        self.recv = [dict() for _ in range(self.depth)]
        g_in, g_out, g_kv, *self.extra = _exchange([w_in[0], w_out[0], w_kv[0]] + list(extra), False, "gather_layer0")
        self.raw[0] = dict(w_in=g_in, w_out=g_out, w_kv=g_kv)

    def weights(self, l):
        raw = self.raw[l]
        return _shards_to_padded(raw["w_in"]), raw["w_out"].reshape(D, D), raw["w_kv"].reshape(D, 2 * M_W)

    def host(self, where, l, payload=None):
        w_in, w_out, w_kv = self.sh
        if where == "in_proj" and l + 1 < self.depth:
            return _Gather([w_in[l + 1], w_out[l + 1], w_kv[l + 1]])
        if where == "d_w_in":
            d_wout, d_wkv = payload
            return _Comm([d_wout.reshape(N_DEV, D // N_DEV, D), d_wkv.reshape(N_DEV, D // N_DEV, 2 * M_W)], True)
        if where == "d_x":
            self.blocks = _padded_to_shards(payload)
            return _Comm([self.blocks], True, relations=_PART_A if l > 0 else tuple(range(N_DEV)))
        if where == "attn_bwd" and l + 1 < self.depth:
            return _Comm([self.blocks], True, relations=_PART_B, into=[self.recv[l + 1]["w_in"]])
        return None

    def landed(self, where, l, outs):
        if where == "in_proj" and outs:
            self.raw[l + 1] = dict(zip(("w_in", "w_out", "w_kv"), outs))
        elif where == "d_w_in":
            self.recv[l]["w_out"], self.recv[l]["w_kv"] = outs
        elif where == "d_x":
            self.recv[l]["w_in"] = outs[0]
        elif where == "attn_bwd" and outs:
            self.recv[l + 1]["w_in"] = outs[0]


def _local_step(x, mem, target, pipe, rel, gate_w, gate_b, norm_g, ln_g, ln_b):
    depth = pipe.depth
    s = x.shape[0]
    saved = []
    xl = x
    for l in range(depth):
        w_in_p, w_out_f, w_kv_f = pipe.weights(l)
        hmat, landed = _mm(xl, w_in_p, out_dtype=BF16, tm=1024, tn=512, tk=D, name="in_proj", comm=pipe.host("in_proj", l))
        pipe.landed("in_proj", l, landed)
        mkv, _ = _mm(mem, w_kv_f, out_dtype=BF16, tm=N_MEM, tn=1024, tk=D, name="mem_kv")
        bias = _band_bias(rel[l])
        gw = jnp.zeros((128, B_KW), F32).at[:GATE_RANK].set(gate_w[l]).astype(BF16)
        gb, ng = gate_b[l][None, :], norm_g[l][None, :]
        r, lse, _ = _attn_fwd(hmat, lax.empty((s, D), BF16), bias)
        r, opre, states, _ = _gla_fwd(hmat, r, gw, gb, ng)
        r = _mem_fwd(hmat, r, mkv)
        xn, xh, rstd, y, *loss = _outproj_ln(hmat, r, w_out_f, xl, ln_g[l][None, :], ln_b[l][None, :],
                                             target if l == depth - 1 else None)
        saved.append(dict(x=xl, h=hmat, mkv=mkv, bias=bias, gw=gw, gb=gb, ng=ng, r=r, lse=lse, opre=opre,
                          states=states, xh=xh, rstd=rstd, y=y, w_in_p=w_in_p, w_out_f=w_out_f))
        xl = xn
    (loss,), g = loss, xl

    grads = [None] * depth
    for l in reversed(range(depth)):
        sv = saved[l]
        dy, du, d_lng, d_lnb, d_wout = _ln_bwd_dy(g, sv["xh"], sv["rstd"], ln_g[l][None, :], sv["w_out_f"], sv["y"])
        dh, dbias, landed = _attn_bwd(sv["h"], sv["r"], dy, sv["lse"], sv["bias"], lax.empty((s, H_W), BF16),
                                      pipe.host("attn_bwd", l))
        pipe.landed("attn_bwd", l, landed)
        dh, d_gw, d_gb, d_ng, _ = _gla_bwd(sv["h"], dy, sv["opre"], sv["states"], sv["gw"], sv["gb"], sv["ng"], dh)
        dh, d_mkv = _mem_bwd(sv["h"], sv["r"], dy, sv["mkv"], dh)
        d_wkv, _ = _mm(mem, d_mkv, ta=True, out_dtype=BF16, tm=1024, tn=1024, tk=N_MEM, name="d_w_kv")
        d_win, landed = _mm(sv["x"], dh, ta=True, out_dtype=BF16, tm=1024, tn=1792, tk=1024, name="d_w_in",
                            comm=pipe.host("d_w_in", l, (d_wout, d_wkv)))
        pipe.landed("d_w_in", l, landed)
        g, landed = _mm(dh, sv["w_in_p"], tb=True, out_dtype=F32, tm=1024, tn=1024, tk=1792, name="d_x",
                        adds=((du, ALPHA),), comm=pipe.host("d_x", l, d_win))
        pipe.landed("d_x", l, landed)
        grads[l] = dict(rel=_bias_grad(dbias), gate_w=d_gw[:GATE_RANK], gate_b=d_gb[0], norm_g=d_ng[0],
                        ln_g=d_lng[0], ln_b=d_lnb[0])
    return loss, g, grads


def _adamw(parts, w, m, v, rows_per_step, name):
    depth, rows, cols = w.shape
    n = parts[0].shape[0]
    tr = min(rows_per_step, rows)
    assert rows % tr == 0 and len(parts) == depth

    def body(*refs):
        p_refs = refs[:depth]
        w_ref, m_ref, v_ref, g_ref, d_ref, nm_ref, nv_ref = refs[depth:]
        for l in range(depth):
            @pl.when(pl.program_id(0) == l)
            def _(p_ref=p_refs[l]):
                g = p_ref[0].astype(F32)
                for j in range(1, n):
                    g = g + p_ref[j].astype(F32)
                nm = ADAM_B1 * m_ref[...] + (1.0 - ADAM_B1) * g
                nv = ADAM_B2 * v_ref[...] + (1.0 - ADAM_B2) * (g * g)
                m_hat = nm / (1.0 - ADAM_B1 ** ADAM_STEP)
                v_hat = nv / (1.0 - ADAM_B2 ** ADAM_STEP)
                g_ref[...] = g
                nm_ref[...] = nm
                nv_ref[...] = nv
                d_ref[...] = -ADAM_LR * (m_hat / (jnp.sqrt(v_hat) + ADAM_EPS) + ADAM_WD * w_ref[...])

    def part_spec(l):
        return pl.BlockSpec((n, tr, cols), lambda ll, i: (0, jnp.where(ll == l, i, 0), 0))

    blk = pl.BlockSpec((None, tr, cols), lambda ll, i: (ll, i, 0))
    shape = jax.ShapeDtypeStruct((depth, rows, cols), F32)
    return pl.pallas_call(
        body, name=name,
        out_shape=(shape, shape, shape, shape),
        grid=(depth, rows // tr),
        in_specs=[part_spec(l) for l in range(depth)] + [blk, blk, blk],
        out_specs=(blk, blk, blk, blk),
        compiler_params=_params(("arbitrary", "arbitrary")),
    )(*parts, w, m, v)


SMALL = (("rel", A_HEADS * (2 * MAX_REL + 1)), ("gate_w", GATE_RANK * B_KW), ("gate_b", B_KW), ("norm_g", B_DV),
         ("ln_g", D), ("ln_b", D))


def _pack_small(parts, depth):
    rows = []
    for name, size in SMALL:
        flat = parts[name].reshape(depth * size).astype(F32)
        rows.append(jnp.pad(flat, (0, -(depth * size) % 128)).reshape(-1, 128))
    packed = jnp.concatenate(rows, axis=0)
    return jnp.pad(packed, ((0, -packed.shape[0] % 8), (0, 0)))


def _unpack_small(packed, depth, shapes):
    out, row = {}, 0
    for name, size in SMALL:
        nrow = -(-(depth * size) // 128)
        out[name] = packed[row:row + nrow].reshape(-1)[:depth * size].reshape(shapes[name])
        row += nrow
    return out


def kernel(x, mem, w_in, a_rel_bias, b_gate_w, b_gate_b, b_norm_g, w_mem_kv, w_out, ln_g, ln_b, loss_target, m_w_in, m_a_rel_bias, m_b_gate_w, m_b_gate_b, m_b_norm_g, m_w_mem_kv, m_w_out, m_ln_g, m_ln_b, v_w_in, v_a_rel_bias, v_b_gate_w, v_b_gate_b, v_b_norm_g, v_w_mem_kv, v_w_out, v_ln_g, v_ln_b):
    depth = w_in.shape[0]
    sh_in = w_in.shape[2]
    sh_gw = b_gate_w.shape[2]
    me = 4 * lax.axis_index("x") + 2 * lax.axis_index("y") + lax.axis_index("c")

    pipe = _Fsdp(w_in.astype(BF16), w_out.astype(BF16), w_mem_kv.astype(BF16), [b_gate_w])
    gate_w_full = jnp.transpose(pipe.extra[0], (1, 2, 0, 3)).reshape(depth, GATE_RANK, N_DEV * sh_gw)
    loss_dev, dx, grads = _local_step(x[0], mem[0], loss_target[0], pipe,
                                      a_rel_bias, gate_w_full, b_gate_b, b_norm_g, ln_g, ln_b)
    loss = lax.psum(loss_dev[0, 0], ("x", "y", "c"))

    recv = lambda n: [pipe.recv[l][n] for l in range(depth)]
    big = {"w_in": _adamw(recv("w_in"), w_in, m_w_in, v_w_in, 128, "adamw_w_in"),
           "w_out": _adamw(recv("w_out"), w_out, m_w_out, v_w_out, 64, "adamw_w_out"),
           "w_kv": _adamw(recv("w_kv"), w_mem_kv, m_w_mem_kv, v_w_mem_kv, 128, "adamw_w_kv")}

    shapes = {"rel": a_rel_bias.shape, "gate_w": (depth, GATE_RANK, N_DEV * sh_gw), "gate_b": b_gate_b.shape,
              "norm_g": b_norm_g.shape, "ln_g": ln_g.shape, "ln_b": ln_b.shape}
    part = _pack_small({n: jnp.stack([grads[l][n] for l in range(depth)]) for n, _ in SMALL}, depth)
    (all_parts,) = _exchange([part], False, "gather_small")
    zeros_gw = jnp.zeros(shapes["gate_w"], F32)
    w_s = _pack_small(dict(rel=a_rel_bias, gate_w=zeros_gw, gate_b=b_gate_b, norm_g=b_norm_g, ln_g=ln_g, ln_b=ln_b), depth)
    m_s = _pack_small(dict(rel=m_a_rel_bias, gate_w=zeros_gw, gate_b=m_b_gate_b, norm_g=m_b_norm_g, ln_g=m_ln_g, ln_b=m_ln_b), depth)
    v_s = _pack_small(dict(rel=v_a_rel_bias, gate_w=zeros_gw, gate_b=v_b_gate_b, norm_g=v_b_norm_g, ln_g=v_ln_g, ln_b=v_ln_b), depth)
    small = [_unpack_small(t[0], depth, shapes)
             for t in _adamw([all_parts], w_s[None], m_s[None], v_s[None], all_parts.shape[1], "adamw_small")]
    gw_grad = lax.dynamic_slice_in_dim(small[0]["gate_w"], me * sh_gw, sh_gw, axis=2).reshape(1, depth * GATE_RANK, sh_gw)
    flat = lambda t: t.reshape(1, depth * GATE_RANK, sh_gw)
    gw_res = [t.reshape(depth, GATE_RANK, sh_gw)
              for t in _adamw([gw_grad], flat(b_gate_w), flat(m_b_gate_w), flat(v_b_gate_w), depth * GATE_RANK, "adamw_gate_w")]

    def leaves(t):
        return (big["w_in"][t], small[t]["rel"], gw_res[t], small[t]["gate_b"], small[t]["norm_g"],
                big["w_kv"][t], big["w_out"][t], small[t]["ln_g"], small[t]["ln_b"])

    return (loss, dx[None]) + leaves(0) + leaves(1) + leaves(2) + leaves(3)
```

```python
import functools
import math

import numpy as np
import jax
import jax.numpy as jnp
from jax import lax
from jax.experimental import pallas as pl
from jax.experimental.pallas import tpu as pltpu

F32 = jnp.float32
BF16 = jnp.bfloat16

N_DEV = 8
D = 2048
CH = 64
LEFT = 8
MAX_REL = 128
N_MEM = 256
A_HEADS, A_DH, A_W = 8, 128, 1024
B_HEADS, B_DK, B_DV, B_KW, B_W = 4, 64, 128, 256, 512
GATE_RANK, GATE_TAU = 16, 16.0
M_HEADS, M_DH, M_W = 4, 128, 512
IN_W = 6672
NAT_SPLIT = 5648
H_W = 7168
H_PAD = H_W - IN_W
A_Q, A_K, A_V, A_Z = 0, 1024, 2048, 3072
B_BASE = 4096
M_BASE = 6144
H_DEAD = (B_BASE + 1664, M_BASE)
ALPHA = (2.0 * 4) ** 0.25
LN_EPS = 1e-5
RMS_EPS = 1e-6
NEG = -1e30
QB = 256
KB = 3 * QB
ADAM_LR, ADAM_B1, ADAM_B2, ADAM_EPS, ADAM_WD, ADAM_STEP = 0.001, 0.9, 0.999, 1e-08, 0.01, 10
VMEM_MB = 1024 * 1024


def _params(sem, vmem_mb=48):
    return pltpu.CompilerParams(dimension_semantics=sem, vmem_limit_bytes=vmem_mb * VMEM_MB)


def _sigmoid(x):
    return 1.0 / (1.0 + jnp.exp(-x))


def _dot(a, b, ca, cb, precision=None):
    return lax.dot_general(a, b, (((ca,), (cb,)), ((), ())), preferred_element_type=F32, precision=precision)


MESH = pl.DeviceIdType.MESH


class _Comm:
    def __init__(self, xs, scatter, relations=tuple(range(N_DEV)), into=None):
        self.xs, self.scatter, self.n = list(xs), scatter, len(xs)
        self.relations, self.into = tuple(relations), list(into or [])
        self.out_shape = [jax.ShapeDtypeStruct((N_DEV,) + (x.shape[1:] if scatter else x.shape), x.dtype) for x in xs]
        self.specs = [pl.BlockSpec(memory_space=pltpu.HBM)] * self.n
        self.scratch = [pltpu.SemaphoreType.DMA((self.n, N_DEV)), pltpu.SemaphoreType.DMA((self.n, N_DEV)),
                        pltpu.SemaphoreType.DMA((self.n,))]

    def _copies(self, x_refs, o_refs, sems):
        send_sems, recv_sems, local_sems = sems
        mx, my, mc = lax.axis_index("x"), lax.axis_index("y"), lax.axis_index("c")
        me = 4 * mx + 2 * my + mc
        own, sends, arrivals = [], [], []
        for k in self.relations:
            if k == 0:
                own = [pltpu.make_async_copy(x_refs[a].at[me] if self.scatter else x_refs[a], o_refs[a].at[me],
                                             local_sems.at[a]) for a in range(self.n)]
                continue
            px = 1 - mx if k & 4 else mx
            py = 1 - my if k & 2 else my
            pc = 1 - mc if k & 1 else mc
            idx = 4 * px + 2 * py + pc
            for a in range(self.n):
                src = x_refs[a].at[idx] if self.scatter else x_refs[a]
                for dst, group in ((o_refs[a].at[me], sends), (o_refs[a].at[idx], arrivals)):
                    group.append(pltpu.make_async_remote_copy(
                        src_ref=src, dst_ref=dst, send_sem=send_sems.at[a, k], recv_sem=recv_sems.at[a, k],
                        device_id=(px, py, pc), device_id_type=MESH))
        return own, sends, arrivals

    def start(self, x_refs, o_refs, sems):
        own, sends, _ = self._copies(x_refs, o_refs, sems)
        for cp in own + sends:
            cp.start()

    def middle(self, x_refs, o_refs, sems):
        pass

    def finish(self, x_refs, o_refs, sems):
        own, sends, arrivals = self._copies(x_refs, o_refs, sems)
        for cp in sends:
            cp.wait_send()
        for cp in arrivals:
            cp.wait_recv()
        for cp in own:
            cp.wait()


class _Gather:
    into = ()

    def __init__(self, xs):
        self.xs, self.n = list(xs), len(xs)
        self.out_shape = [jax.ShapeDtypeStruct((N_DEV,) + x.shape, x.dtype) for x in xs]
        self.specs = [pl.BlockSpec(memory_space=pltpu.HBM)] * self.n
        self.scratch = [pltpu.SemaphoreType.DMA((self.n, N_DEV - 1)), pltpu.SemaphoreType.DMA((self.n, N_DEV - 1)),
                        pltpu.SemaphoreType.DMA((self.n,))]

    def _copies(self, x_refs, o_refs, sems):
        send_sems, recv_sems, local_sems = sems
        mx, my, mc = lax.axis_index("x"), lax.axis_index("y"), lax.axis_index("c")
        idx = lambda px, py, pc: 4 * px + 2 * py + pc
        me, sibling = (mx, my, mc), (mx, my, 1 - mc)
        chips = [(mx, 1 - my), (1 - mx, my), (1 - mx, 1 - my)]

        def copy(a, k, src, slot, to):
            return pltpu.make_async_remote_copy(
                src_ref=src, dst_ref=o_refs[a].at[idx(*slot)], send_sem=send_sems.at[a, k], recv_sem=recv_sems.at[a, k],
                device_id=to, device_id_type=MESH)

        c = dict(own=[], first=[], passed=[], ici_in=[], late_in=[])
        for a in range(self.n):
            x = x_refs[a]
            c["own"].append(pltpu.make_async_copy(x, o_refs[a].at[idx(*me)], local_sems.at[a]))
            c["first"].append(copy(a, 0, x, me, sibling))
            c["late_in"].append(copy(a, 0, x, sibling, sibling))
            for j, chip in enumerate(chips):
                c["first"].append(copy(a, 1 + j, x, me, (*chip, mc)))
                c["ici_in"].append(copy(a, 1 + j, x, (*chip, mc), (*chip, mc)))
                c["passed"].append(copy(a, 4 + j, o_refs[a].at[idx(*chip, mc)], (*chip, mc), sibling))
                c["late_in"].append(copy(a, 4 + j, x, (*chip, 1 - mc), sibling))
        return c

    def start(self, x_refs, o_refs, sems):
        c = self._copies(x_refs, o_refs, sems)
        for cp in c["own"] + c["first"]:
            cp.start()

    def middle(self, x_refs, o_refs, sems):
        c = self._copies(x_refs, o_refs, sems)
        for arrived, onward in zip(c["ici_in"], c["passed"]):
            arrived.wait_recv()
            onward.start()

    def finish(self, x_refs, o_refs, sems):
        c = self._copies(x_refs, o_refs, sems)
        for cp in c["first"] + c["passed"]:
            cp.wait_send()
        for cp in c["late_in"]:
            cp.wait_recv()
        for cp in c["own"]:
            cp.wait()


def _exchange(xs, scatter, name):
    comm = _Comm(xs, scatter) if scatter else _Gather(xs)

    def body(*refs):
        x_refs, o_refs, sems = refs[:comm.n], refs[comm.n:2 * comm.n], refs[2 * comm.n:]
        comm.start(x_refs, o_refs, sems)
        comm.middle(x_refs, o_refs, sems)
        comm.finish(x_refs, o_refs, sems)

    return pl.pallas_call(body, name=name, out_shape=tuple(comm.out_shape), in_specs=comm.specs,
                          out_specs=tuple(comm.specs), scratch_shapes=comm.scratch)(*comm.xs)


def _hosted(body, comm, *, name, grid, in_specs, out_specs, out_shape, scratch_shapes, compiler_params, inputs,
            input_output_aliases=None):
    out_specs, out_shape = tuple(out_specs), tuple(out_shape)
    aliases = input_output_aliases or {}
    if comm is None:
        outs = pl.pallas_call(body, name=name, grid=grid, in_specs=list(in_specs), out_specs=out_specs, out_shape=out_shape,
                              scratch_shapes=list(scratch_shapes), compiler_params=compiler_params,
                              input_output_aliases=aliases)(*inputs)
        return tuple(outs), ()
    ni, no, ns, nc = len(in_specs), len(out_specs), len(scratch_shapes), comm.n
    n_into = len(comm.into)
    aliases = {**aliases, **{ni + nc + a: no + a for a in range(n_into)}}

    def wrapped(*refs):
        ins, c_in = refs[:ni], refs[ni:ni + nc]
        refs = refs[ni + nc + n_into:]
        outs, c_out = refs[:no], refs[no:no + nc]
        scr, sems = refs[no + nc:no + nc + ns], refs[no + nc + ns:]
        first = functools.reduce(jnp.logical_and, [pl.program_id(d) == 0 for d in range(len(grid))])
        last = functools.reduce(jnp.logical_and, [pl.program_id(d) == grid[d] - 1 for d in range(len(grid))])

        step = functools.reduce(lambda acc, d: acc * grid[d] + pl.program_id(d), range(len(grid)), 0)
        n_steps = math.prod(grid)

        @pl.when(first)
        def _():
            comm.start(c_in, c_out, sems)

        body(*ins, *outs, *scr)

        @pl.when(step == max(3 * n_steps // 4, 1) - 1)
        def _():
            comm.middle(c_in, c_out, sems)

        @pl.when(last)
        def _():
            comm.finish(c_in, c_out, sems)

    params = pltpu.CompilerParams(dimension_semantics=("arbitrary",) * len(grid),
                                  vmem_limit_bytes=compiler_params.vmem_limit_bytes)
    outs = pl.pallas_call(wrapped, name=name, grid=grid, in_specs=list(in_specs) + comm.specs + comm.specs[:n_into],
                          out_specs=out_specs + tuple(comm.specs), out_shape=out_shape + tuple(comm.out_shape),
                          scratch_shapes=list(scratch_shapes) + comm.scratch, compiler_params=params,
                          input_output_aliases=aliases)(*inputs, *comm.xs, *comm.into)
    return tuple(outs[:no]), tuple(outs[no:])


def _live_pieces(dead, tile):
    lo, hi = dead
    t = lo // tile
    assert (hi - 1) // tile == t and lo % 128 == 0 and hi % 128 == 0
    return t, [(x, y) for x, y in ((0, lo - t * tile), (hi - t * tile, tile)) if y > x]


def _mm(a, b, *, ta=False, tb=False, out_dtype, tm, tn, tk, name, adds=(), vmem_mb=48, comm=None, dead_n=None, dead_k=None):
    m = a.shape[1] if ta else a.shape[0]
    k = a.shape[0] if ta else a.shape[1]
    n = b.shape[0] if tb else b.shape[1]
    assert k == (b.shape[1] if tb else b.shape[0])
    tm, tn, tk = min(tm, m), min(tn, n), min(tk, k)
    assert m % tm == 0 and n % tn == 0 and k % tk == 0, (name, m, n, k)
    nk = k // tk
    n_add = len(adds)
    scales = [s for _, s in adds]
    jd, cols_d = _live_pieces(dead_n, tn) if dead_n else (None, None)
    kd, ks_d = _live_pieces(dead_k, tk) if dead_k else (None, None)

    def body(a_ref, b_ref, *rest):
        add_refs, o_ref = rest[:n_add], rest[n_add]
        acc_ref = rest[n_add + 1] if nk > 1 else None
        jj, kk = pl.program_id(1), pl.program_id(2)

        def product(c0, c1, ks):
            p = None
            for k0, k1 in ks:
                a_blk = a_ref[k0:k1, :] if ta else a_ref[:, k0:k1]
                b_blk = b_ref[c0:c1, k0:k1] if tb else b_ref[k0:k1, c0:c1]
                q = _dot(a_blk.astype(BF16), b_blk.astype(BF16), 0 if ta else 1, 1 if tb else 0)
                p = q if p is None else p + q
            return p

        def step(mode, cols, ks):
            for c0, c1 in cols:
                p = product(c0, c1, ks)
                if mode == "first":
                    acc_ref[:, c0:c1] = p
                elif mode == "mid":
                    acc_ref[:, c0:c1] += p
                else:
                    r = p if mode == "only" else acc_ref[:, c0:c1] + p
                    for ref, s in zip(add_refs, scales):
                        r = r + s * ref[:, c0:c1].astype(F32)
                    o_ref[:, c0:c1] = r.astype(out_dtype)

        modes = [("only", None, range(1))] if nk == 1 else (
            [("first", kk == 0, range(1))] + ([("mid", (kk > 0) & (kk < nk - 1), range(1, nk - 1))] if nk > 2 else [])
            + [("last", kk == nk - 1, range(nk - 1, nk))])
        for mode, when_k, k_range in modes:
            k_cases = [(when_k, [(0, tk)])]
            if kd is not None and kd in k_range:
                is_dead = kk == kd
                k_cases = [(is_dead, ks_d)] + ([(when_k & ~is_dead, [(0, tk)])] if len(k_range) > 1 else [])
            for cond_k, ks in k_cases:
                n_cases = [(None, [(0, tn)])] if jd is None else [(jj == jd, cols_d), (jj != jd, [(0, tn)])]
                for cond_n, cols in n_cases:
                    conds = [c for c in (cond_k, cond_n) if c is not None]
                    run = functools.partial(step, mode, cols, ks)
                    pl.when(functools.reduce(jnp.logical_and, conds))(run) if conds else run()

    a_spec = pl.BlockSpec((tk, tm), lambda i, j, kk: (kk, i)) if ta else pl.BlockSpec((tm, tk), lambda i, j, kk: (i, kk))
    b_spec = pl.BlockSpec((tn, tk), lambda i, j, kk: (j, kk)) if tb else pl.BlockSpec((tk, tn), lambda i, j, kk: (kk, j))
    add_specs = [pl.BlockSpec((tm, tn), lambda i, j, kk: (i, j)) for _ in adds]
    (out,), c_out = _hosted(
        body, comm, name=name,
        out_shape=[jax.ShapeDtypeStruct((m, n), out_dtype)],
        grid=(m // tm, n // tn, nk),
        in_specs=[a_spec, b_spec] + add_specs,
        out_specs=[pl.BlockSpec((tm, tn), lambda i, j, kk: (i, j))],
        scratch_shapes=[pltpu.VMEM((tm, tn), F32)] if nk > 1 else [],
        compiler_params=_params(("parallel", "parallel", "arbitrary"), vmem_mb),
        inputs=(a, b, *[x for x, _ in adds]))
    return out, c_out


NKB = KB // QB
LEAD = NKB - 1


def _band_bias(table):
    i = np.arange(QB)[:, None]
    j = np.arange(KB)[None, :]
    qc = i // CH + LEAD * QB // CH
    kc = j // CH
    valid = (kc <= qc) & (kc >= qc - LEFT)
    n = QB + KB
    c = np.arange(n)
    onehot = np.zeros((2 * MAX_REL + 1, n), np.float32)
    onehot[np.clip(LEAD * QB - (c - (QB - 1)), -MAX_REL, MAX_REL) + MAX_REL, c] = 1.0
    row = jnp.dot(table.astype(F32), jnp.asarray(onehot), precision=lax.Precision.HIGHEST)
    flow = jnp.tile(row, (1, QB))[:, :QB * (n - 1)].reshape(table.shape[0], QB, n - 1)
    return jnp.where(valid[None], flow[:, :, QB - 1:], NEG)


def _bias_grad(dbias):
    h, n = dbias.shape[0], KB + 1
    flat = jnp.pad(dbias.reshape(h, QB * KB), ((0, 0), (0, -(QB * KB) % n)))
    diag = flat.reshape(h, -1, n).sum(axis=1)
    c = np.arange(n)
    jm = np.where(c < LEAD * QB + CH, c, c - n)
    didx = np.clip(LEAD * QB - jm, -MAX_REL, MAX_REL) + MAX_REL
    onehot = np.zeros((n, 2 * MAX_REL + 1), np.float32)
    onehot[c, didx] = 1.0
    return jnp.dot(diag, jnp.asarray(onehot), precision=lax.Precision.HIGHEST)


def _attn_scores(q, k_refs, cs, bias_h, m, masked, scale):
    parts = []
    for t in range(NKB):
        sc = _dot(q, k_refs[t][:, cs], 1, 1) * scale
        parts.append(jnp.where(m + t - LEAD >= 0, sc, NEG) if masked else sc)
    return jnp.concatenate(parts, axis=1) + bias_h


def _attn_fwd(h, r_in, bias, comm=None):
    s = h.shape[0]
    nq = s // QB
    scale = A_DH ** -0.5

    def body(q_ref, *rest):
        k_refs, v_refs = rest[:NKB], rest[NKB:2 * NKB]
        bias_hbm, r_any, o_ref, lse_ref, bias_ref = rest[2 * NKB:]
        del r_any
        m = pl.program_id(0)

        @pl.when(m == 0)
        def _():
            pltpu.sync_copy(bias_hbm, bias_ref)

        def scores(hd, masked):
            cs = slice(hd * A_DH, (hd + 1) * A_DH)
            return _attn_scores(q_ref[:, cs], k_refs, cs, bias_ref[hd], m, masked, scale)

        def step(masked):
            sc_next = scores(0, masked)
            for hd in range(A_HEADS):
                cs = slice(hd * A_DH, (hd + 1) * A_DH)
                sc = sc_next
                if hd + 1 < A_HEADS:
                    sc_next = scores(hd + 1, masked)
                mx = jnp.max(sc, axis=1, keepdims=True)
                p = jnp.exp(sc - mx)
                l = jnp.sum(p, axis=1, keepdims=True)
                pb = p.astype(BF16)
                o = _dot(pb[:, :QB], v_refs[0][:, cs], 1, 0)
                for t in range(1, NKB):
                    o += _dot(pb[:, t * QB:(t + 1) * QB], v_refs[t][:, cs], 1, 0)
                o_ref[:, cs] = (o / l).astype(BF16)
                lse_ref[:, hd:hd + 1] = mx + jnp.log(l)

        pl.when(m < LEAD)(functools.partial(step, True))
        pl.when(m >= LEAD)(functools.partial(step, False))

    def kv_spec(col, t):
        return pl.BlockSpec((QB, A_W), lambda m: (jnp.maximum(m + t - LEAD, 0), col))

    (r, lse), c_out = _hosted(
        body, comm, name="attn_fwd",
        out_shape=(jax.ShapeDtypeStruct(r_in.shape, BF16), jax.ShapeDtypeStruct((s, A_HEADS), F32)),
        grid=(nq,),
        in_specs=[pl.BlockSpec((QB, A_W), lambda m: (m, 0))]
        + [kv_spec(1, t) for t in range(NKB)] + [kv_spec(2, t) for t in range(NKB)]
        + [pl.BlockSpec(memory_space=pl.ANY), pl.BlockSpec(memory_space=pl.ANY)],
        out_specs=(pl.BlockSpec((QB, A_W), lambda m: (m, 0)), pl.BlockSpec((QB, A_HEADS), lambda m: (m, 0))),
        scratch_shapes=[pltpu.VMEM((A_HEADS, QB, KB), F32)],
        input_output_aliases={2 * NKB + 2: 0},
        compiler_params=_params(("arbitrary",)),
        inputs=(h,) * (2 * NKB + 1) + (bias, r_in))
    return r, lse, c_out


def _attn_bwd(h, r, dy, lse, bias, dh_in, comm=None):
    s = h.shape[0]
    nq = s // QB
    scale = A_DH ** -0.5

    def body(q_ref, *rest):
        k_refs, v_refs = rest[:NKB], rest[NKB:2 * NKB]
        (z_ref, r_ref, dy_ref, lse_ref, zl_ref, rl_ref, dyl_ref, bias_hbm, dh_any,
         dh_ref, dbias_hbm, dk_acc, dv_acc, dq_ring, bias_ref, dbias_ref) = rest[2 * NKB:]
        del dh_any
        m = pl.program_id(0)

        def slot(b):
            return pl.multiple_of(lax.rem(b + NKB, NKB) * QB, QB)

        @pl.when(m == 0)
        def _():
            dk_acc[...] = jnp.zeros_like(dk_acc)
            dv_acc[...] = jnp.zeros_like(dv_acc)
            dq_ring[...] = jnp.zeros_like(dq_ring)
            dbias_ref[...] = jnp.zeros_like(dbias_ref)
            pltpu.sync_copy(bias_hbm, bias_ref)

        def step(masked):
            z = z_ref[...].astype(F32)
            do_all = dy_ref[...].astype(F32) * (z * _sigmoid(z))
            o_all = r_ref[...].astype(F32)

            def products(hd):
                cs = slice(hd * A_DH, (hd + 1) * A_DH)
                dob = do_all[:, cs].astype(BF16)
                sc = _attn_scores(q_ref[:, cs], k_refs, cs, bias_ref[hd], m, masked, scale)
                return sc, jnp.concatenate([_dot(dob, v_refs[t][:, cs], 1, 1) for t in range(NKB)], axis=1)

            ahead = products(0)
            for hd in range(A_HEADS):
                cs = slice(hd * A_DH, (hd + 1) * A_DH)
                q = q_ref[:, cs]
                do = do_all[:, cs]
                dob = do.astype(BF16)
                delta = jnp.sum(do * o_all[:, cs], axis=1, keepdims=True)
                sc, dp = ahead
                if hd + 1 < A_HEADS:
                    ahead = products(hd + 1)
                p = jnp.exp(sc - lse_ref[:, hd:hd + 1])
                ds = p * (dp - delta)
                dbias_ref[hd] += ds
                pb, dsb = p.astype(BF16), ds.astype(BF16)
                dq = jnp.zeros((QB, A_DH), F32)
                for t in range(NKB):
                    ts = slice(t * QB, (t + 1) * QB)
                    rows = pl.ds(slot(m - LEAD + t), QB)
                    dq += _dot(dsb[:, ts], k_refs[t][:, cs], 1, 0)
                    dk_acc[rows, cs] += _dot(dsb[:, ts], q, 0, 0) * scale
                    dv_acc[rows, cs] += _dot(pb[:, ts], dob, 0, 0)
                dq_ring[pl.ds(slot(m), QB), cs] = dq * scale

        pl.when(m < LEAD)(functools.partial(step, True))
        pl.when((m >= LEAD) & (m < nq))(functools.partial(step, False))

        done = pl.ds(slot(m - LEAD), QB)
        zl = zl_ref[...].astype(F32)
        sg = _sigmoid(zl)
        dz = dyl_ref[...].astype(F32) * rl_ref[...].astype(F32) * (sg * (1.0 + zl * (1.0 - sg)))
        dh_ref[:, A_Q:A_Q + A_W] = dq_ring[done, :].astype(BF16)
        dh_ref[:, A_K:A_K + A_W] = dk_acc[done, :].astype(BF16)
        dh_ref[:, A_V:A_V + A_W] = dv_acc[done, :].astype(BF16)
        dh_ref[:, A_Z:A_Z + A_W] = dz.astype(BF16)
        dk_acc[done, :] = jnp.zeros((QB, A_W), F32)
        dv_acc[done, :] = jnp.zeros((QB, A_W), F32)

        @pl.when(m == nq + LEAD - 1)
        def _():
            pltpu.sync_copy(dbias_ref, dbias_hbm)

    last = nq - 1

    def cur(col):
        return pl.BlockSpec((QB, A_W), lambda m: (jnp.minimum(m, last), col))

    def kv_spec(col, t):
        return pl.BlockSpec((QB, A_W), lambda m: (jnp.clip(m + t - LEAD, 0, last), col))

    def lag(col):
        return pl.BlockSpec((QB, A_W), lambda m: (jnp.clip(m - LEAD, 0, last), col))

    (dh, dbias), c_out = _hosted(
        body, comm, name="attn_bwd",
        out_shape=(jax.ShapeDtypeStruct(dh_in.shape, BF16), jax.ShapeDtypeStruct((A_HEADS, QB, KB), F32)),
        grid=(nq + LEAD,),
        in_specs=[cur(0)] + [kv_spec(1, t) for t in range(NKB)] + [kv_spec(2, t) for t in range(NKB)]
        + [cur(3), cur(0), cur(0), pl.BlockSpec((QB, A_HEADS), lambda m: (jnp.minimum(m, last), 0)),
           lag(3), lag(0), lag(0),
           pl.BlockSpec(memory_space=pl.ANY), pl.BlockSpec(memory_space=pl.ANY)],
        out_specs=(pl.BlockSpec((QB, 4 * A_W), lambda m: (jnp.clip(m - LEAD, 0, last), 0)),
                   pl.BlockSpec(memory_space=pl.ANY)),
        scratch_shapes=[pltpu.VMEM((KB, A_W), F32), pltpu.VMEM((KB, A_W), F32), pltpu.VMEM((KB, A_W), F32),
                        pltpu.VMEM((A_HEADS, QB, KB), F32), pltpu.VMEM((A_HEADS, QB, KB), F32)],
        input_output_aliases={2 * NKB + 9: 0},
        compiler_params=_params(("arbitrary",), 56),
        inputs=(h,) * (2 * NKB + 2) + (r, dy, lse, h, r, dy, bias, dh_in))
    return dh, dbias, c_out


GB = 256
N_PAIR = B_HEADS // 2


def _gla_gates(lr, gw_ref, gb_ref):
    logit = _dot(lr, gw_ref[...], 1, 0) + gb_ref[...]
    lg = (jnp.minimum(logit, 0.0) - jnp.log(1.0 + jnp.exp(-jnp.abs(logit)))) / GATE_TAU
    row = lax.broadcasted_iota(jnp.int32, (GB, GB), 0)
    col = lax.broadcasted_iota(jnp.int32, (GB, GB), 1)
    tri = jnp.where((row // CH == col // CH) & (col <= row), 1.0, 0.0).astype(F32)
    return logit, _dot(tri, lg, 1, 0, precision=lax.Precision.HIGHEST)


def _gla_factors(hb_ref, b_all, c):
    rs = slice(c * CH, (c + 1) * CH)
    q = hb_ref[rs, 0:B_KW].astype(F32) * (B_DK ** -0.5)
    k = hb_ref[rs, B_KW:2 * B_KW].astype(F32)
    b = b_all[rs]
    bm, bl = b[CH // 2:CH // 2 + 1, :], b[CH - 1:CH, :]
    e1, e2, eb, ek = jnp.exp(b - bm), jnp.exp(bm - b), jnp.exp(b), jnp.exp(bl - b)
    el = jnp.exp(bl)
    return dict(ql=q * e1, kl=k * e2, qu=q * e2, ku=k * e1, qt=q * eb, kh=k * ek, e1=e1, e2=e2, eb=eb, ek=ek, el=el)


class _GlaPairTools:
    def __init__(self, hb_ref, fs):
        self.hb_ref, self.fs = hb_ref, fs
        row = lax.broadcasted_iota(jnp.int32, (2 * CH, 2 * CH), 0)
        col = lax.broadcasted_iota(jnp.int32, (2 * CH, 2 * CH), 1)
        self.same = (row // CH) == (col // CH)
        self.lower = self.same & ((row % CH) >= (col % CH))
        self.upper = self.same & ((row % CH) < (col % CH))
        self.lower_t = self.same & ((col % CH) >= (row % CH))
        self.upper_t = self.same & ((col % CH) < (row % CH))

    def lanes(self, c, p, name):
        return self.fs[c][name][:, p * 128:(p + 1) * 128]

    def heads(self, c, p, name):
        x = self.lanes(c, p, name)
        return jnp.where(self.same, jnp.concatenate([x, x], axis=0), 0.0).astype(BF16)

    def twice(self, c, p, name):
        x = self.lanes(c, p, name).astype(BF16)
        return jnp.concatenate([x, x], axis=0)

    def vals(self, c, p):
        rows = slice(c * CH, (c + 1) * CH)
        return jnp.concatenate([self.hb_ref[rows, 512 + (2 * p + sh) * B_DV:512 + (2 * p + sh + 1) * B_DV]
                                for sh in range(2)], axis=0)

    def intra(self, c, p):
        lo = _dot(self.heads(c, p, "ql"), self.twice(c, p, "kl"), 1, 1)
        up = _dot(self.heads(c, p, "qu"), self.twice(c, p, "ku"), 1, 1)
        return jnp.where(self.lower, lo, jnp.where(self.upper, up, 0.0)).astype(BF16)

    def fold(self, x):
        x = jnp.where(self.same, x, 0.0)
        return x[:CH] + x[CH:]


def _gla_fwd(h, r_in, gw, gb, ng, comm=None):
    s = h.shape[0]
    nb = s // GB
    cpb = GB // CH

    def body(hb_ref, gw_ref, gb_ref, ng_ref, r_any, o_ref, opre_ref, st_ref, state):
        del r_any

        @pl.when(pl.program_id(0) == 0)
        def _():
            state[...] = jnp.zeros_like(state)

        _, b_all = _gla_gates(hb_ref[:, 1536:1664], gw_ref, gb_ref)
        fs = [_gla_factors(hb_ref, b_all, c) for c in range(cpb)]
        pairs = [(c, p) for c in range(cpb) for p in range(N_PAIR)]
        tools = _GlaPairTools(hb_ref, fs)
        a = {k: tools.intra(*k) for k in pairs}
        o_intra = {k: _dot(a[k], tools.vals(*k), 1, 0) for k in pairs}
        upd = {k: _dot(tools.vals(*k), tools.heads(*k, "kh"), 0, 0) for k in pairs}
        st = [state[p] for p in range(N_PAIR)]
        entering = {}
        for c, p in pairs:
            entering[c, p] = st[p]
            st_ref[c, p] = st[p]
            st[p] = st[p] * fs[c]["el"][:, p * 128:(p + 1) * 128] + upd[c, p]
        for p in range(N_PAIR):
            state[p] = st[p]
        for c, p in pairs:
            o2 = o_intra[c, p] + _dot(tools.heads(c, p, "qt"), entering[c, p].astype(BF16), 1, 1)
            for sh in range(2):
                o = o2[sh * CH:(sh + 1) * CH]
                rs, hs = slice(c * CH, (c + 1) * CH), slice((2 * p + sh) * B_DV, (2 * p + sh + 1) * B_DV)
                opre_ref[rs, hs] = o
                rinv = lax.rsqrt(jnp.mean(o * o, axis=1, keepdims=True) + RMS_EPS)
                o_ref[rs, hs] = (o * rinv * ng_ref[...]).astype(BF16)

    (r, opre, states), c_out = _hosted(
        body, comm, name="gla_fwd",
        out_shape=(jax.ShapeDtypeStruct(r_in.shape, BF16), jax.ShapeDtypeStruct((s, B_W), F32),
                   jax.ShapeDtypeStruct((s // CH, N_PAIR, 128, 128), F32)),
        grid=(nb,),
        in_specs=[pl.BlockSpec((GB, 2048), lambda i: (i, B_BASE // 2048)),
                  pl.BlockSpec((128, B_KW), lambda i: (0, 0)), pl.BlockSpec((1, B_KW), lambda i: (0, 0)),
                  pl.BlockSpec((1, B_DV), lambda i: (0, 0)), pl.BlockSpec(memory_space=pl.ANY)],
        out_specs=(pl.BlockSpec((GB, B_W), lambda i: (i, 1024 // B_W)), pl.BlockSpec((GB, B_W), lambda i: (i, 0)),
                   pl.BlockSpec((cpb, N_PAIR, 128, 128), lambda i: (i, 0, 0, 0))),
        scratch_shapes=[pltpu.VMEM((N_PAIR, 128, 128), F32)],
        input_output_aliases={4: 0},
        compiler_params=_params(("arbitrary",)),
        inputs=(h, gw, gb, ng, r_in))
    return r, opre, states, c_out


def _gla_bwd(h, dy, opre, states, gw, gb, ng, dh_in, comm=None):
    s = h.shape[0]
    nb = s // GB
    cpb = GB // CH

    def body(hb_ref, dy_ref, opre_ref, st_ref, gw_ref, gb_ref, ng_ref, dh_any,
             dh_ref, dgw_ref, dgb_ref, dng_ref, dstate, db_scr, do_scr):
        del dh_any

        @pl.when(pl.program_id(0) == 0)
        def _():
            dstate[...] = jnp.zeros_like(dstate)
            dgw_ref[...] = jnp.zeros_like(dgw_ref)
            dgb_ref[...] = jnp.zeros_like(dgb_ref)
            dng_ref[...] = jnp.zeros_like(dng_ref)

        lr = hb_ref[:, 1536:1664]
        logit, b_all = _gla_gates(lr, gw_ref, gb_ref)
        z = hb_ref[:, 1024:1536].astype(F32)
        sg = _sigmoid(z)
        dyb = dy_ref[...].astype(F32)
        dng = jnp.zeros((1, B_DV), F32)
        for hd in range(B_HEADS):
            hs = slice(hd * B_DV, (hd + 1) * B_DV)
            o = opre_ref[:, hs]
            rinv = lax.rsqrt(jnp.mean(o * o, axis=1, keepdims=True) + RMS_EPS)
            on = o * rinv
            dr = dyb[:, hs] * (z[:, hs] * sg[:, hs])
            dh_ref[:, 1024 + hd * B_DV:1024 + (hd + 1) * B_DV] = (
                dyb[:, hs] * (on * ng_ref[...]) * (sg[:, hs] * (1.0 + z[:, hs] * (1.0 - sg[:, hs])))).astype(BF16)
            dng += jnp.sum(dr * on, axis=0, keepdims=True)
            dn = dr * ng_ref[...]
            do_scr[:, hs] = rinv * (dn - on * jnp.mean(dn * on, axis=1, keepdims=True))
        dng_ref[...] += dng

        rowi = lax.broadcasted_iota(jnp.int32, (CH, 128), 0)
        fs = [_gla_factors(hb_ref, b_all, c) for c in range(cpb)]
        pairs = [(c, p) for c in reversed(range(cpb)) for p in range(N_PAIR)]
        tools = _GlaPairTools(hb_ref, fs)
        pair = tools.lanes

        def douts(c, p):
            rows = slice(c * CH, (c + 1) * CH)
            return jnp.concatenate([do_scr[rows, (2 * p + sh) * B_DV:(2 * p + sh + 1) * B_DV] for sh in range(2)],
                                   axis=0).astype(BF16)

        at, da, dat, dst_own, g_qt = {}, {}, {}, {}, {}
        for k in pairs:
            c, p = k
            lo_t = _dot(tools.twice(*k, "kl"), tools.heads(*k, "ql"), 1, 1)
            up_t = _dot(tools.twice(*k, "ku"), tools.heads(*k, "qu"), 1, 1)
            at[k] = jnp.where(tools.lower_t, lo_t, jnp.where(tools.upper_t, up_t, 0.0)).astype(BF16)
            da[k] = _dot(douts(*k), tools.vals(*k), 1, 1)
            dat[k] = _dot(tools.vals(*k), douts(*k), 1, 1)
            dst_own[k] = _dot(douts(*k), tools.heads(*k, "qt"), 0, 0)
            g_qt[k] = tools.fold(_dot(douts(*k), st_ref[c, p].astype(BF16), 1, 0))
        dv, g_ql, g_qu, g_kl, g_ku = {}, {}, {}, {}, {}
        for k in pairs:
            dv[k] = _dot(at[k], douts(*k), 1, 0)
            g_ql[k] = tools.fold(_dot(jnp.where(tools.lower, da[k], 0.0).astype(BF16), tools.twice(*k, "kl"), 1, 0))
            g_qu[k] = tools.fold(_dot(jnp.where(tools.upper, da[k], 0.0).astype(BF16), tools.twice(*k, "ku"), 1, 0))
            g_kl[k] = tools.fold(_dot(jnp.where(tools.lower_t, dat[k], 0.0).astype(BF16), tools.heads(*k, "ql"), 1, 0))
            g_ku[k] = tools.fold(_dot(jnp.where(tools.upper_t, dat[k], 0.0).astype(BF16), tools.heads(*k, "qu"), 1, 0))
        dst = [dstate[p] for p in range(N_PAIR)]
        leaving = {}
        for c, p in pairs:
            leaving[c, p] = dst[p]
            dst[p] = dst[p] * pair(c, p, "el") + dst_own[c, p]
        for p in range(N_PAIR):
            dstate[p] = dst[p]
        g_kh = {}
        for k in pairs:
            c, p = k
            dstb = leaving[k].astype(BF16)
            dv2 = dv[k] + _dot(tools.heads(*k, "kh"), dstb, 1, 1)
            for sh in range(2):
                hd = 2 * p + sh
                dh_ref[c * CH:(c + 1) * CH, 512 + hd * B_DV:512 + (hd + 1) * B_DV] = dv2[sh * CH:(sh + 1) * CH].astype(BF16)
            g_kh[k] = tools.fold(_dot(tools.vals(*k), dstb, 1, 0))
        for c in reversed(range(cpb)):
            rs = slice(c * CH, (c + 1) * CH)
            for p in range(N_PAIR):
                k = (c, p)
                dq = (g_ql[k] * pair(c, p, "e1") + g_qu[k] * pair(c, p, "e2") + g_qt[k] * pair(c, p, "eb")) * (B_DK ** -0.5)
                dk = g_kl[k] * pair(c, p, "e2") + g_ku[k] * pair(c, p, "e1") + g_kh[k] * pair(c, p, "ek")
                dkh_kh = g_kh[k] * pair(c, p, "kh")
                db = (g_ql[k] * pair(c, p, "ql") - g_qu[k] * pair(c, p, "qu") + g_qt[k] * pair(c, p, "qt")
                      - g_kl[k] * pair(c, p, "kl") + g_ku[k] * pair(c, p, "ku") - dkh_kh)
                db_last = (pair(c, p, "el") * jnp.sum(leaving[c, p] * st_ref[c, p], axis=0, keepdims=True)
                           + jnp.sum(dkh_kh, axis=0, keepdims=True))
                db = jnp.where(rowi == CH - 1, db + db_last, db)
                dh_ref[rs, p * 128:(p + 1) * 128] = dq.astype(BF16)
                dh_ref[rs, B_KW + p * 128:B_KW + (p + 1) * 128] = dk.astype(BF16)
                db_scr[rs, p * 128:(p + 1) * 128] = db

        row = lax.broadcasted_iota(jnp.int32, (GB, GB), 0)
        col = lax.broadcasted_iota(jnp.int32, (GB, GB), 1)
        trit = jnp.where((row // CH == col // CH) & (col >= row), 1.0, 0.0).astype(F32)
        dlg = _dot(trit, db_scr[...], 1, 0, precision=lax.Precision.HIGHEST)
        dlogit = dlg * (_sigmoid(-logit) / GATE_TAU)
        dlb = dlogit.astype(BF16)
        dgw_ref[...] += _dot(lr, dlb, 0, 0)
        dgb_ref[...] += jnp.sum(dlogit, axis=0, keepdims=True)
        dh_ref[:, 1536:1664] = _dot(dlb, gw_ref[...], 1, 1).astype(BF16)
        dh_ref[:, 1664:2048] = jnp.zeros((GB, 384), BF16)

    rev = lambda i: nb - 1 - i
    (dh, dgw, dgb, dng), c_out = _hosted(
        body, comm, name="gla_bwd",
        out_shape=(jax.ShapeDtypeStruct(dh_in.shape, BF16), jax.ShapeDtypeStruct((128, B_KW), F32),
                   jax.ShapeDtypeStruct((1, B_KW), F32), jax.ShapeDtypeStruct((1, B_DV), F32)),
        grid=(nb,),
        in_specs=[pl.BlockSpec((GB, 2048), lambda i: (rev(i), B_BASE // 2048)),
                  pl.BlockSpec((GB, B_W), lambda i: (rev(i), 1024 // B_W)),
                  pl.BlockSpec((GB, B_W), lambda i: (rev(i), 0)),
                  pl.BlockSpec((cpb, N_PAIR, 128, 128), lambda i: (rev(i), 0, 0, 0)),
                  pl.BlockSpec((128, B_KW), lambda i: (0, 0)), pl.BlockSpec((1, B_KW), lambda i: (0, 0)),
                  pl.BlockSpec((1, B_DV), lambda i: (0, 0)), pl.BlockSpec(memory_space=pl.ANY)],
        out_specs=(pl.BlockSpec((GB, 2048), lambda i: (rev(i), B_BASE // 2048)),
                   pl.BlockSpec((128, B_KW), lambda i: (0, 0)), pl.BlockSpec((1, B_KW), lambda i: (0, 0)),
                   pl.BlockSpec((1, B_DV), lambda i: (0, 0))),
        scratch_shapes=[pltpu.VMEM((N_PAIR, 128, 128), F32), pltpu.VMEM((GB, B_KW), F32), pltpu.VMEM((GB, B_W), F32)],
        input_output_aliases={7: 0},
        compiler_params=_params(("arbitrary",)),
        inputs=(h, dy, opre, states, gw, gb, ng, dh_in))
    return dh, dgw, dgb, dng, c_out


MB = 512


def _mem_probs(q, mk, scale):
    sc = _dot(q, mk, 1, 1) * scale
    p = jnp.exp(sc - jnp.max(sc, axis=1, keepdims=True))
    return p / jnp.sum(p, axis=1, keepdims=True)


def _mem_fwd(h, r_in, mkv):
    s = h.shape[0]
    scale = M_DH ** -0.5

    def body(q_ref, mkv_ref, r_any, o_ref):
        del r_any
        for hd in range(M_HEADS):
            cs = slice(hd * M_DH, (hd + 1) * M_DH)
            p = _mem_probs(q_ref[:, cs], mkv_ref[:, cs], scale)
            o_ref[:, cs] = _dot(p.astype(BF16), mkv_ref[:, M_W + hd * M_DH:M_W + (hd + 1) * M_DH], 1, 0).astype(BF16)

    return pl.pallas_call(
        body, name="mem_fwd",
        out_shape=jax.ShapeDtypeStruct(r_in.shape, BF16),
        grid=(s // MB,),
        in_specs=[pl.BlockSpec((MB, M_W), lambda i: (i, M_BASE // M_W)),
                  pl.BlockSpec((N_MEM, 2 * M_W), lambda i: (0, 0)), pl.BlockSpec(memory_space=pl.ANY)],
        out_specs=pl.BlockSpec((MB, M_W), lambda i: (i, 1536 // M_W)),
        input_output_aliases={2: 0},
        compiler_params=_params(("arbitrary",)),
    )(h, mkv, r_in)


def _mem_bwd(h, r, dy, mkv, dh_in):
    s = h.shape[0]
    scale = M_DH ** -0.5

    def body(q_ref, z_ref, r_ref, dy_ref, mkv_ref, dh_any, dh_ref, dmkv_ref):
        del dh_any

        @pl.when(pl.program_id(0) == 0)
        def _():
            dmkv_ref[...] = jnp.zeros_like(dmkv_ref)

        z = z_ref[...].astype(F32)
        sg = _sigmoid(z)
        dyv = dy_ref[...].astype(F32)
        do_all = dyv * (z * sg)
        dh_ref[:, M_W:2 * M_W] = (dyv * r_ref[...].astype(F32) * (sg * (1.0 + z * (1.0 - sg)))).astype(BF16)
        for hd in range(M_HEADS):
            cs = slice(hd * M_DH, (hd + 1) * M_DH)
            vs = slice(M_W + hd * M_DH, M_W + (hd + 1) * M_DH)
            q = q_ref[:, cs]
            p = _mem_probs(q, mkv_ref[:, cs], scale)
            dob = do_all[:, cs].astype(BF16)
            dp = _dot(dob, mkv_ref[:, vs], 1, 1)
            ds = p * (dp - jnp.sum(p * dp, axis=1, keepdims=True))
            dsb = ds.astype(BF16)
            dh_ref[:, cs] = (_dot(dsb, mkv_ref[:, cs], 1, 0) * scale).astype(BF16)
            dmkv_ref[:, cs] += _dot(dsb, q, 0, 0) * scale
            dmkv_ref[:, vs] += _dot(p.astype(BF16), dob, 0, 0)

    return pl.pallas_call(
        body, name="mem_bwd",
        out_shape=(jax.ShapeDtypeStruct(dh_in.shape, BF16), jax.ShapeDtypeStruct((N_MEM, 2 * M_W), F32)),
        grid=(s // MB,),
        in_specs=[pl.BlockSpec((MB, M_W), lambda i: (i, M_BASE // M_W)),
                  pl.BlockSpec((MB, M_W), lambda i: (i, M_BASE // M_W + 1)),
                  pl.BlockSpec((MB, M_W), lambda i: (i, 1536 // M_W)),
                  pl.BlockSpec((MB, M_W), lambda i: (i, 1536 // M_W)),
                  pl.BlockSpec((N_MEM, 2 * M_W), lambda i: (0, 0)), pl.BlockSpec(memory_space=pl.ANY)],
        out_specs=(pl.BlockSpec((MB, 2 * M_W), lambda i: (i, M_BASE // (2 * M_W))),
                   pl.BlockSpec((N_MEM, 2 * M_W), lambda i: (0, 0))),
        input_output_aliases={5: 0},
        compiler_params=_params(("arbitrary",)),
    )(h, h, r, dy, mkv, dh_in)


OB = 256


def _outproj_ln(h, r, w_out, x, ln_g, ln_b, target=None):
    s = h.shape[0]
    last = target is not None

    def body(za_ref, zb_ref, zm_ref, r_ref, w_ref, x_ref, g_ref, b_ref, *rest):
        if last:
            t_ref, xn_ref, xh_ref, rstd_ref, y_ref, l_ref = rest
        else:
            xn_ref, xh_ref, rstd_ref, y_ref = rest
        z = jnp.concatenate([za_ref[...], zb_ref[...], zm_ref[...]], axis=1).astype(F32)
        y = (r_ref[...].astype(F32) * (z * _sigmoid(z))).astype(BF16)
        y_ref[...] = y
        u = ALPHA * x_ref[...] + _dot(y, w_ref[...], 1, 0)
        mu = jnp.mean(u, axis=1, keepdims=True)
        uc = u - mu
        rstd = lax.rsqrt(jnp.mean(uc * uc, axis=1, keepdims=True) + LN_EPS)
        xh = uc * rstd
        xh_ref[...] = xh
        rstd_ref[...] = rstd
        xn = xh * g_ref[...] + b_ref[...]
        if last:
            @pl.when(pl.program_id(0) == 0)
            def _():
                l_ref[...] = jnp.zeros_like(l_ref)

            e = xn - t_ref[...]
            xn_ref[...] = e / D
            l_ref[...] += 0.5 * jnp.sum(jnp.mean(e * e, axis=1, keepdims=True))
        else:
            xn_ref[...] = xn

    row = lambda w, c: pl.BlockSpec((OB, w), lambda i: (i, c))
    vec = pl.BlockSpec((1, D), lambda i: (0, 0))
    full = jax.ShapeDtypeStruct((s, D), F32)
    return pl.pallas_call(
        body, name="outproj_ln_loss" if last else "outproj_ln",
        out_shape=(full, full, jax.ShapeDtypeStruct((s, 1), F32), jax.ShapeDtypeStruct((s, D), BF16))
        + ((jax.ShapeDtypeStruct((1, 128), F32),) if last else ()),
        grid=(s // OB,),
        in_specs=[row(A_W, A_Z // A_W), row(B_W, (B_BASE + 1024) // B_W), row(M_W, (M_BASE + M_W) // M_W), row(D, 0),
                  pl.BlockSpec((D, D), lambda i: (0, 0)), row(D, 0), vec, vec] + ([row(D, 0)] if last else []),
        out_specs=(row(D, 0), row(D, 0), pl.BlockSpec((OB, 1), lambda i: (i, 0)), row(D, 0))
        + ((pl.BlockSpec((1, 128), lambda i: (0, 0)),) if last else ()),
        compiler_params=_params(("arbitrary",), 56),
    )(h, h, h, r, w_out, x, ln_g, ln_b, *([target] if last else []))


def _ln_bwd(g, xh, rstd, ln_g):
    s = g.shape[0]

    def body(g_ref, xh_ref, rstd_ref, lg_ref, du_ref, dg_ref, db_ref):
        @pl.when(pl.program_id(0) == 0)
        def _():
            dg_ref[...] = jnp.zeros_like(dg_ref)
            db_ref[...] = jnp.zeros_like(db_ref)

        gv, xh = g_ref[...], xh_ref[...]
        dg_ref[...] += jnp.sum(gv * xh, axis=0, keepdims=True)
        db_ref[...] += jnp.sum(gv, axis=0, keepdims=True)
        dxh = gv * lg_ref[...]
        du_ref[...] = rstd_ref[...] * (dxh - jnp.mean(dxh, axis=1, keepdims=True)
                                       - xh * jnp.mean(dxh * xh, axis=1, keepdims=True))

    row = pl.BlockSpec((OB, D), lambda i: (i, 0))
    vec = pl.BlockSpec((1, D), lambda i: (0, 0))
    return pl.pallas_call(
        body, name="ln_bwd",
        out_shape=(jax.ShapeDtypeStruct((s, D), F32), jax.ShapeDtypeStruct((1, D), F32), jax.ShapeDtypeStruct((1, D), F32)),
        grid=(s // OB,),
        in_specs=[row, row, pl.BlockSpec((OB, 1), lambda i: (i, 0)), vec],
        out_specs=(row, vec, vec),
        compiler_params=_params(("arbitrary",)),
    )(g, xh, rstd, ln_g)


def _ln_bwd_dy(g, xh, rstd, ln_g, w_out, y):
    s = g.shape[0]
    n = s // OB

    def body(g_ref, xh_ref, rstd_ref, lg_ref, w_ref, y_ref, dy_ref, du_ref, dg_ref, db_ref, dw_hbm, dw_acc, stage):
        i = pl.program_id(0)

        @pl.when(i == 0)
        def _():
            dg_ref[...] = jnp.zeros_like(dg_ref)
            db_ref[...] = jnp.zeros_like(db_ref)
            dw_acc[...] = jnp.zeros_like(dw_acc)

        gv, xh = g_ref[...], xh_ref[...]
        dg_ref[...] += jnp.sum(gv * xh, axis=0, keepdims=True)
        db_ref[...] += jnp.sum(gv, axis=0, keepdims=True)
        dxh = gv * lg_ref[...]
        du = rstd_ref[...] * (dxh - jnp.mean(dxh, axis=1, keepdims=True) - xh * jnp.mean(dxh * xh, axis=1, keepdims=True))
        du_ref[...] = du
        dub = du.astype(BF16)
        dy_ref[...] = _dot(dub, w_ref[...], 1, 1).astype(BF16)
        dw_acc[...] += _dot(y_ref[...], dub, 0, 0)

        @pl.when(i == n - 1)
        def _():
            for c in range(D // OB):
                stage[...] = dw_acc[c * OB:(c + 1) * OB, :].astype(BF16)
                pltpu.sync_copy(stage, dw_hbm.at[c * OB:(c + 1) * OB, :])

    row = pl.BlockSpec((OB, D), lambda i: (i, 0))
    vec = pl.BlockSpec((1, D), lambda i: (0, 0))
    return pl.pallas_call(
        body, name="ln_bwd_dy",
        out_shape=(jax.ShapeDtypeStruct((s, D), BF16), jax.ShapeDtypeStruct((s, D), F32),
                   jax.ShapeDtypeStruct((1, D), F32), jax.ShapeDtypeStruct((1, D), F32), jax.ShapeDtypeStruct((D, D), BF16)),
        grid=(n,),
        in_specs=[row, row, pl.BlockSpec((OB, 1), lambda i: (i, 0)), vec, pl.BlockSpec((D, D), lambda i: (0, 0)), row],
        out_specs=(row, row, vec, vec, pl.BlockSpec(memory_space=pl.ANY)),
        scratch_shapes=[pltpu.VMEM((D, D), F32), pltpu.VMEM((OB, D), BF16)],
        compiler_params=_params(("arbitrary",), 60),
    )(g, xh, rstd, ln_g, w_out, y)


def _loss_grad(y, target):
    s = y.shape[0]

    def body(y_ref, t_ref, l_ref, dy_ref):
        @pl.when(pl.program_id(0) == 0)
        def _():
            l_ref[...] = jnp.zeros_like(l_ref)

        e = y_ref[...] - t_ref[...]
        dy_ref[...] = e / D
        l_ref[...] += 0.5 * jnp.sum(jnp.mean(e * e, axis=1, keepdims=True))

    row = pl.BlockSpec((OB, D), lambda i: (i, 0))
    return pl.pallas_call(
        body, name="loss_grad",
        out_shape=(jax.ShapeDtypeStruct((1, 128), F32), jax.ShapeDtypeStruct((s, D), F32)),
        grid=(s // OB,),
        in_specs=[row, row],
        out_specs=(pl.BlockSpec((1, 128), lambda i: (0, 0)), row),
        compiler_params=_params(("arbitrary",)),
    )(y, target)


class _LocalWeights:
    def __init__(self, w_in_p, w_out_f, w_kv_f):
        self.w = list(zip(w_in_p, w_out_f, w_kv_f))
        self.depth = len(self.w)
        self.grads = [dict() for _ in self.w]

    def weights(self, l):
        return self.w[l]

    def host(self, where, l, payload=None):
        if payload is not None:
            self.grads[l][where] = payload
        return None

    def landed(self, where, l, outs):
        pass


def _shard_pieces():
    sh = IN_W // N_DEV
    pieces = []
    for j in range(N_DEV):
        lo, hi = j * sh, (j + 1) * sh
        cuts = [lo, NAT_SPLIT, hi] if lo < NAT_SPLIT < hi else [lo, hi]
        for a, b in zip(cuts[:-1], cuts[1:]):
            pieces.append((j, a - lo, a if a < NAT_SPLIT else a + H_PAD, b - a))
    return pieces


RB = 256


def _shards_to_padded(raw):
    def body(x_ref, o_ref):
        for j, src, dst, width in _shard_pieces():
            o_ref[:, dst:dst + width] = x_ref[j, :, src:src + width]
        o_ref[:, NAT_SPLIT:NAT_SPLIT + H_PAD] = jnp.zeros((RB, H_PAD), o_ref.dtype)

    return pl.pallas_call(
        body, name="place_w_in", out_shape=jax.ShapeDtypeStruct((D, H_W), raw.dtype), grid=(D // RB,),
        in_specs=[pl.BlockSpec((N_DEV, RB, raw.shape[2]), lambda i: (0, i, 0))],
        out_specs=pl.BlockSpec((RB, H_W), lambda i: (i, 0)),
        compiler_params=_params(("parallel",)),
    )(raw)


def _padded_to_shards(w):
    sh = IN_W // N_DEV

    def body(x_ref, o_ref):
        for j, src, dst, width in _shard_pieces():
            o_ref[j, :, src:src + width] = x_ref[:, dst:dst + width]

    return pl.pallas_call(
        body, name="split_d_w_in", out_shape=jax.ShapeDtypeStruct((N_DEV, D, sh), w.dtype), grid=(D // RB,),
        in_specs=[pl.BlockSpec((RB, H_W), lambda i: (i, 0))],
        out_specs=pl.BlockSpec((N_DEV, RB, sh), lambda i: (0, i, 0)),
        compiler_params=_params(("parallel",)),
    )(w)


_PART_A = (0, 1, 2, 4, 7)
_PART_B = (3, 5, 6)


class _Fsdp:
    def __init__(self, w_in, w_out, w_kv, extra):
        self.sh = (w_in, w_out, w_kv)
        self.depth = w_in.shape[0]
        self.raw = [dict() for _ in range(self.depth)]
        self.recv = [dict() for _ in range(self.depth)]
        g_in, g_out, g_kv, *self.extra = _exchange([w_in[0], w_out[0], w_kv[0]] + list(extra), False, "gather_layer0")
        self.raw[0] = dict(w_in=g_in, w_out=g_out, w_kv=g_kv)

    def weights(self, l):
        raw = self.raw[l]
        return _shards_to_padded(raw["w_in"]), raw["w_out"].reshape(D, D), raw["w_kv"].reshape(D, 2 * M_W)

    def host(self, where, l, payload=None):
        w_in, w_out, w_kv = self.sh
        if where == "in_proj" and l + 1 < self.depth:
            return _Gather([w_in[l + 1], w_out[l + 1], w_kv[l + 1]])
        if where == "d_w_in":
            d_wout, d_wkv = payload
            return _Comm([d_wout.reshape(N_DEV, D // N_DEV, D), d_wkv.reshape(N_DEV, D // N_DEV, 2 * M_W)], True)
        if where == "d_x":
            self.blocks = _padded_to_shards(payload)
            return _Comm([self.blocks], True, relations=_PART_A if l > 0 else tuple(range(N_DEV)))
        if where == "attn_bwd" and l + 1 < self.depth:
            return _Comm([self.blocks], True, relations=_PART_B, into=[self.recv[l + 1]["w_in"]])
        return None

    def landed(self, where, l, outs):
        if where == "in_proj" and outs:
            self.raw[l + 1] = dict(zip(("w_in", "w_out", "w_kv"), outs))
        elif where == "d_w_in":
            self.recv[l]["w_out"], self.recv[l]["w_kv"] = outs
        elif where == "d_x":
            self.recv[l]["w_in"] = outs[0]
        elif where == "attn_bwd" and outs:
            self.recv[l + 1]["w_in"] = outs[0]


def _local_step(x, mem, target, pipe, rel, gate_w, gate_b, norm_g, ln_g, ln_b):
    depth = pipe.depth
    s = x.shape[0]
    saved = []
    xl = x
    for l in range(depth):
        w_in_p, w_out_f, w_kv_f = pipe.weights(l)
        hmat, landed = _mm(xl, w_in_p, out_dtype=BF16, tm=1024, tn=1024, tk=D, name="in_proj",
                           comm=pipe.host("in_proj", l), dead_n=H_DEAD)
        pipe.landed("in_proj", l, landed)
        mkv, _ = _mm(mem, w_kv_f, out_dtype=BF16, tm=N_MEM, tn=1024, tk=D, name="mem_kv")
        bias = _band_bias(rel[l])
        gw = jnp.zeros((128, B_KW), F32).at[:GATE_RANK].set(gate_w[l]).astype(BF16)
        gb, ng = gate_b[l][None, :], norm_g[l][None, :]
        r, lse, _ = _attn_fwd(hmat, lax.empty((s, D), BF16), bias)
        r, opre, states, _ = _gla_fwd(hmat, r, gw, gb, ng)
        r = _mem_fwd(hmat, r, mkv)
        xn, xh, rstd, y, *loss = _outproj_ln(hmat, r, w_out_f, xl, ln_g[l][None, :], ln_b[l][None, :],
                                             target if l == depth - 1 else None)
        saved.append(dict(x=xl, h=hmat, mkv=mkv, bias=bias, gw=gw, gb=gb, ng=ng, r=r, lse=lse, opre=opre,
                          states=states, xh=xh, rstd=rstd, y=y, w_in_p=w_in_p, w_out_f=w_out_f))
        xl = xn
    (loss,), g = loss, xl

    grads = [None] * depth
    for l in reversed(range(depth)):
        sv = saved[l]
        dy, du, d_lng, d_lnb, d_wout = _ln_bwd_dy(g, sv["xh"], sv["rstd"], ln_g[l][None, :], sv["w_out_f"], sv["y"])
        dh, dbias, landed = _attn_bwd(sv["h"], sv["r"], dy, sv["lse"], sv["bias"], lax.empty((s, H_W), BF16),
                                      pipe.host("attn_bwd", l))
        pipe.landed("attn_bwd", l, landed)
        dh, d_gw, d_gb, d_ng, _ = _gla_bwd(sv["h"], dy, sv["opre"], sv["states"], sv["gw"], sv["gb"], sv["ng"], dh)
        dh, d_mkv = _mem_bwd(sv["h"], sv["r"], dy, sv["mkv"], dh)
        d_wkv, _ = _mm(mem, d_mkv, ta=True, out_dtype=BF16, tm=1024, tn=1024, tk=N_MEM, name="d_w_kv")
        d_win, landed = _mm(sv["x"], dh, ta=True, out_dtype=BF16, tm=1024, tn=1792, tk=1024, name="d_w_in",
                            comm=pipe.host("d_w_in", l, (d_wout, d_wkv)), dead_n=H_DEAD)
        pipe.landed("d_w_in", l, landed)
        g, landed = _mm(dh, sv["w_in_p"], tb=True, out_dtype=F32, tm=1024, tn=1024, tk=1792, name="d_x",
                        adds=((du, ALPHA),), comm=pipe.host("d_x", l, d_win), dead_k=H_DEAD)
        pipe.landed("d_x", l, landed)
        grads[l] = dict(rel=_bias_grad(dbias), gate_w=d_gw[:GATE_RANK], gate_b=d_gb[0], norm_g=d_ng[0],
                        ln_g=d_lng[0], ln_b=d_lnb[0])
    return loss, g, grads


def _adamw(parts, w, m, v, rows_per_step, name):
    depth, rows, cols = w.shape
    n = parts[0].shape[0]
    tr = min(rows_per_step, rows)
    assert rows % tr == 0 and len(parts) == depth

    def body(*refs):
        p_refs = refs[:depth]
        w_ref, m_ref, v_ref, g_ref, d_ref, nm_ref, nv_ref = refs[depth:]
        for l in range(depth):
            @pl.when(pl.program_id(0) == l)
            def _(p_ref=p_refs[l]):
                g = p_ref[0].astype(F32)
                for j in range(1, n):
                    g = g + p_ref[j].astype(F32)
                nm = ADAM_B1 * m_ref[...] + (1.0 - ADAM_B1) * g
                nv = ADAM_B2 * v_ref[...] + (1.0 - ADAM_B2) * (g * g)
                m_hat = nm / (1.0 - ADAM_B1 ** ADAM_STEP)
                v_hat = nv / (1.0 - ADAM_B2 ** ADAM_STEP)
                g_ref[...] = g
                nm_ref[...] = nm
                nv_ref[...] = nv
                d_ref[...] = -ADAM_LR * (m_hat / (jnp.sqrt(v_hat) + ADAM_EPS) + ADAM_WD * w_ref[...])

    def part_spec(l):
        return pl.BlockSpec((n, tr, cols), lambda ll, i: (0, jnp.where(ll == l, i, 0), 0))

    blk = pl.BlockSpec((None, tr, cols), lambda ll, i: (ll, i, 0))
    shape = jax.ShapeDtypeStruct((depth, rows, cols), F32)
    return pl.pallas_call(
        body, name=name,
        out_shape=(shape, shape, shape, shape),
        grid=(depth, rows // tr),
        in_specs=[part_spec(l) for l in range(depth)] + [blk, blk, blk],
        out_specs=(blk, blk, blk, blk),
        compiler_params=_params(("arbitrary", "arbitrary")),
    )(*parts, w, m, v)


SMALL = (("rel", A_HEADS * (2 * MAX_REL + 1)), ("gate_w", GATE_RANK * B_KW), ("gate_b", B_KW), ("norm_g", B_DV),
         ("ln_g", D), ("ln_b", D))


def _pack_small(parts, depth):
    rows = []
    for name, size in SMALL:
        flat = parts[name].reshape(depth * size).astype(F32)
        rows.append(jnp.pad(flat, (0, -(depth * size) % 128)).reshape(-1, 128))
    packed = jnp.concatenate(rows, axis=0)
    return jnp.pad(packed, ((0, -packed.shape[0] % 8), (0, 0)))


def _unpack_small(packed, depth, shapes):
    out, row = {}, 0
    for name, size in SMALL:
        nrow = -(-(depth * size) // 128)
        out[name] = packed[row:row + nrow].reshape(-1)[:depth * size].reshape(shapes[name])
        row += nrow
    return out


def kernel(x, mem, w_in, a_rel_bias, b_gate_w, b_gate_b, b_norm_g, w_mem_kv, w_out, ln_g, ln_b, loss_target, m_w_in, m_a_rel_bias, m_b_gate_w, m_b_gate_b, m_b_norm_g, m_w_mem_kv, m_w_out, m_ln_g, m_ln_b, v_w_in, v_a_rel_bias, v_b_gate_w, v_b_gate_b, v_b_norm_g, v_w_mem_kv, v_w_out, v_ln_g, v_ln_b):
    depth = w_in.shape[0]
    sh_in = w_in.shape[2]
    sh_gw = b_gate_w.shape[2]
    me = 4 * lax.axis_index("x") + 2 * lax.axis_index("y") + lax.axis_index("c")

    pipe = _Fsdp(w_in.astype(BF16), w_out.astype(BF16), w_mem_kv.astype(BF16), [b_gate_w])
    gate_w_full = jnp.transpose(pipe.extra[0], (1, 2, 0, 3)).reshape(depth, GATE_RANK, N_DEV * sh_gw)
    loss_dev, dx, grads = _local_step(x[0], mem[0], loss_target[0], pipe,
                                      a_rel_bias, gate_w_full, b_gate_b, b_norm_g, ln_g, ln_b)
    loss = lax.psum(loss_dev[0, 0], ("x", "y", "c"))

    recv = lambda n: [pipe.recv[l][n] for l in range(depth)]
    big = {"w_in": _adamw(recv("w_in"), w_in, m_w_in, v_w_in, 128, "adamw_w_in"),
           "w_out": _adamw(recv("w_out"), w_out, m_w_out, v_w_out, 64, "adamw_w_out"),
           "w_kv": _adamw(recv("w_kv"), w_mem_kv, m_w_mem_kv, v_w_mem_kv, 128, "adamw_w_kv")}

    shapes = {"rel": a_rel_bias.shape, "gate_w": (depth, GATE_RANK, N_DEV * sh_gw), "gate_b": b_gate_b.shape,
              "norm_g": b_norm_g.shape, "ln_g": ln_g.shape, "ln_b": ln_b.shape}
    part = _pack_small({n: jnp.stack([grads[l][n] for l in range(depth)]) for n, _ in SMALL}, depth)
    (all_parts,) = _exchange([part], False, "gather_small")
    zeros_gw = jnp.zeros(shapes["gate_w"], F32)
    w_s = _pack_small(dict(rel=a_rel_bias, gate_w=zeros_gw, gate_b=b_gate_b, norm_g=b_norm_g, ln_g=ln_g, ln_b=ln_b), depth)
    m_s = _pack_small(dict(rel=m_a_rel_bias, gate_w=zeros_gw, gate_b=m_b_gate_b, norm_g=m_b_norm_g, ln_g=m_ln_g, ln_b=m_ln_b), depth)
    v_s = _pack_small(dict(rel=v_a_rel_bias, gate_w=zeros_gw, gate_b=v_b_gate_b, norm_g=v_b_norm_g, ln_g=v_ln_g, ln_b=v_ln_b), depth)
    small = [_unpack_small(t[0], depth, shapes)
             for t in _adamw([all_parts], w_s[None], m_s[None], v_s[None], all_parts.shape[1], "adamw_small")]
    gw_grad = lax.dynamic_slice_in_dim(small[0]["gate_w"], me * sh_gw, sh_gw, axis=2).reshape(1, depth * GATE_RANK, sh_gw)
    flat = lambda t: t.reshape(1, depth * GATE_RANK, sh_gw)
    gw_res = [t.reshape(depth, GATE_RANK, sh_gw)
              for t in _adamw([gw_grad], flat(b_gate_w), flat(m_b_gate_w), flat(v_b_gate_w), depth * GATE_RANK, "adamw_gate_w")]

    def leaves(t):
        return (big["w_in"][t], small[t]["rel"], gw_res[t], small[t]["gate_b"], small[t]["norm_g"],
                big["w_kv"][t], big["w_out"][t], small[t]["ln_g"], small[t]["ln_b"])

    return (loss, dx[None]) + leaves(0) + leaves(1) + leaves(2) + leaves(3)
```

```python
import functools
import math

import numpy as np
import jax
import jax.numpy as jnp
from jax import lax
from jax.experimental import pallas as pl
from jax.experimental.pallas import tpu as pltpu

F32 = jnp.float32
BF16 = jnp.bfloat16

N_DEV = 8
D = 2048
CH = 64
LEFT = 8
MAX_REL = 128
N_MEM = 256
A_HEADS, A_DH, A_W = 8, 128, 1024
B_HEADS, B_DK, B_DV, B_KW, B_W = 4, 64, 128, 256, 512
GATE_RANK, GATE_TAU = 16, 16.0
M_HEADS, M_DH, M_W = 4, 128, 512
IN_W = 6672
NAT_SPLIT = 5648
H_W = 7168
H_PAD = H_W - IN_W
A_Q, A_K, A_V, A_Z = 0, 1024, 2048, 3072
B_BASE = 4096
M_BASE = 6144
H_DEAD = (B_BASE + 1664, M_BASE)
ALPHA = (2.0 * 4) ** 0.25
LOG2E = math.log2(math.e)
LN_EPS = 1e-5
RMS_EPS = 1e-6
NEG = -1e30
QB = 256
KB = 3 * QB
ADAM_LR, ADAM_B1, ADAM_B2, ADAM_EPS, ADAM_WD, ADAM_STEP = 0.001, 0.9, 0.999, 1e-08, 0.01, 10
VMEM_MB = 1024 * 1024


def _params(sem, vmem_mb=48):
    return pltpu.CompilerParams(dimension_semantics=sem, vmem_limit_bytes=vmem_mb * VMEM_MB)


def _sigmoid(x):
    return 1.0 / (1.0 + jnp.exp(-x))


def _dot(a, b, ca, cb, precision=None):
    return lax.dot_general(a, b, (((ca,), (cb,)), ((), ())), preferred_element_type=F32, precision=precision)


MESH = pl.DeviceIdType.MESH


class _Scatter:
    def __init__(self, xs, relations=tuple(range(N_DEV)), into=None):
        self.xs, self.n = list(xs), len(xs)
        self.relations, self.into = tuple(relations), list(into or [])
        self.out_shape = [jax.ShapeDtypeStruct(x.shape, x.dtype) for x in xs]
        self.specs = [pl.BlockSpec(memory_space=pltpu.HBM)] * self.n
        self.scratch = [pltpu.SemaphoreType.DMA((self.n, N_DEV)), pltpu.SemaphoreType.DMA((self.n, N_DEV)),
                        pltpu.SemaphoreType.DMA((self.n,))]

    def _copies(self, x_refs, o_refs, sems):
        send_sems, recv_sems, local_sems = sems
        mx, my, mc = lax.axis_index("x"), lax.axis_index("y"), lax.axis_index("c")
        me = 4 * mx + 2 * my + mc
        own, sends, arrivals = [], [], []
        for k in self.relations:
            if k == 0:
                own = [pltpu.make_async_copy(x_refs[a].at[me], o_refs[a].at[me], local_sems.at[a]) for a in range(self.n)]
                continue
            px = 1 - mx if k & 4 else mx
            py = 1 - my if k & 2 else my
            pc = 1 - mc if k & 1 else mc
            idx = 4 * px + 2 * py + pc
            for a in range(self.n):
                for dst, group in ((o_refs[a].at[me], sends), (o_refs[a].at[idx], arrivals)):
                    group.append(pltpu.make_async_remote_copy(
                        src_ref=x_refs[a].at[idx], dst_ref=dst, send_sem=send_sems.at[a, k], recv_sem=recv_sems.at[a, k],
                        device_id=(px, py, pc), device_id_type=MESH))
        return own, sends, arrivals

    def start(self, x_refs, o_refs, sems):
        own, sends, _ = self._copies(x_refs, o_refs, sems)
        for cp in own + sends:
            cp.start()

    def middle(self, x_refs, o_refs, sems):
        pass

    def finish(self, x_refs, o_refs, sems):
        own, sends, arrivals = self._copies(x_refs, o_refs, sems)
        for cp in sends:
            cp.wait_send()
        for cp in arrivals:
            cp.wait_recv()
        for cp in own:
            cp.wait()


class _Gather:
    into = ()

    def __init__(self, xs):
        self.xs, self.n = list(xs), len(xs)
        self.out_shape = [jax.ShapeDtypeStruct((N_DEV,) + x.shape, x.dtype) for x in xs]
        self.specs = [pl.BlockSpec(memory_space=pltpu.HBM)] * self.n
        self.scratch = [pltpu.SemaphoreType.DMA((self.n, N_DEV - 1)), pltpu.SemaphoreType.DMA((self.n, N_DEV - 1)),
                        pltpu.SemaphoreType.DMA((self.n,))]

    def _copies(self, x_refs, o_refs, sems):
        send_sems, recv_sems, local_sems = sems
        mx, my, mc = lax.axis_index("x"), lax.axis_index("y"), lax.axis_index("c")
        idx = lambda px, py, pc: 4 * px + 2 * py + pc
        me, sibling = (mx, my, mc), (mx, my, 1 - mc)
        chips = [(mx, 1 - my), (1 - mx, my), (1 - mx, 1 - my)]

        def copy(a, k, src, slot, to):
            return pltpu.make_async_remote_copy(
                src_ref=src, dst_ref=o_refs[a].at[idx(*slot)], send_sem=send_sems.at[a, k], recv_sem=recv_sems.at[a, k],
                device_id=to, device_id_type=MESH)

        c = dict(own=[], first=[], passed=[], ici_in=[], late_in=[])
        for a in range(self.n):
            x = x_refs[a]
            c["own"].append(pltpu.make_async_copy(x, o_refs[a].at[idx(*me)], local_sems.at[a]))
            c["first"].append(copy(a, 0, x, me, sibling))
            c["late_in"].append(copy(a, 0, x, sibling, sibling))
            for j, chip in enumerate(chips):
                c["first"].append(copy(a, 1 + j, x, me, (*chip, mc)))
                c["ici_in"].append(copy(a, 1 + j, x, (*chip, mc), (*chip, mc)))
                c["passed"].append(copy(a, 4 + j, o_refs[a].at[idx(*chip, mc)], (*chip, mc), sibling))
                c["late_in"].append(copy(a, 4 + j, x, (*chip, 1 - mc), sibling))
        return c

    def start(self, x_refs, o_refs, sems):
        c = self._copies(x_refs, o_refs, sems)
        for cp in c["own"] + c["first"]:
            cp.start()

    def middle(self, x_refs, o_refs, sems):
        c = self._copies(x_refs, o_refs, sems)
        for arrived, onward in zip(c["ici_in"], c["passed"]):
            arrived.wait_recv()
            onward.start()

    def finish(self, x_refs, o_refs, sems):
        c = self._copies(x_refs, o_refs, sems)
        for cp in c["first"] + c["passed"]:
            cp.wait_send()
        for cp in c["late_in"]:
            cp.wait_recv()
        for cp in c["own"]:
            cp.wait()


def _gather_now(xs, name):
    comm = _Gather(xs)

    def body(*refs):
        x_refs, o_refs, sems = refs[:comm.n], refs[comm.n:2 * comm.n], refs[2 * comm.n:]
        comm.start(x_refs, o_refs, sems)
        comm.middle(x_refs, o_refs, sems)
        comm.finish(x_refs, o_refs, sems)

    return pl.pallas_call(body, name=name, out_shape=tuple(comm.out_shape), in_specs=comm.specs,
                          out_specs=tuple(comm.specs), scratch_shapes=comm.scratch)(*comm.xs)


def _hosted(body, comm, *, name, grid, in_specs, out_specs, out_shape, scratch_shapes, compiler_params, inputs,
            input_output_aliases=None):
    out_specs, out_shape = tuple(out_specs), tuple(out_shape)
    aliases = input_output_aliases or {}
    if comm is None:
        outs = pl.pallas_call(body, name=name, grid=grid, in_specs=list(in_specs), out_specs=out_specs, out_shape=out_shape,
                              scratch_shapes=list(scratch_shapes), compiler_params=compiler_params,
                              input_output_aliases=aliases)(*inputs)
        return tuple(outs), ()
    ni, no, ns, nc = len(in_specs), len(out_specs), len(scratch_shapes), comm.n
    n_into = len(comm.into)
    aliases = {**aliases, **{ni + nc + a: no + a for a in range(n_into)}}

    def wrapped(*refs):
        ins, c_in = refs[:ni], refs[ni:ni + nc]
        refs = refs[ni + nc + n_into:]
        outs, c_out = refs[:no], refs[no:no + nc]
        scr, sems = refs[no + nc:no + nc + ns], refs[no + nc + ns:]
        first = functools.reduce(jnp.logical_and, [pl.program_id(d) == 0 for d in range(len(grid))])
        last = functools.reduce(jnp.logical_and, [pl.program_id(d) == grid[d] - 1 for d in range(len(grid))])

        step = functools.reduce(lambda acc, d: acc * grid[d] + pl.program_id(d), range(len(grid)), 0)
        n_steps = math.prod(grid)

        @pl.when(first)
        def _():
            comm.start(c_in, c_out, sems)

        body(*ins, *outs, *scr)

        @pl.when(step == max(3 * n_steps // 4, 1) - 1)
        def _():
            comm.middle(c_in, c_out, sems)

        @pl.when(last)
        def _():
            comm.finish(c_in, c_out, sems)

    params = pltpu.CompilerParams(dimension_semantics=("arbitrary",) * len(grid),
                                  vmem_limit_bytes=compiler_params.vmem_limit_bytes)
    outs = pl.pallas_call(wrapped, name=name, grid=grid, in_specs=list(in_specs) + comm.specs + comm.specs[:n_into],
                          out_specs=out_specs + tuple(comm.specs), out_shape=out_shape + tuple(comm.out_shape),
                          scratch_shapes=list(scratch_shapes) + comm.scratch, compiler_params=params,
                          input_output_aliases=aliases)(*inputs, *comm.xs, *comm.into)
    return tuple(outs[:no]), tuple(outs[no:])


def _live_pieces(dead, tile):
    lo, hi = dead
    t = lo // tile
    assert (hi - 1) // tile == t and lo % 128 == 0 and hi % 128 == 0
    return t, [(x, y) for x, y in ((0, lo - t * tile), (hi - t * tile, tile)) if y > x]


def _mm(a, b, *, ta=False, tb=False, out_dtype, tm, tn, tk, name, adds=(), vmem_mb=48, comm=None, dead_n=None, dead_k=None):
    m = a.shape[1] if ta else a.shape[0]
    k = a.shape[0] if ta else a.shape[1]
    n = b.shape[0] if tb else b.shape[1]
    assert k == (b.shape[1] if tb else b.shape[0])
    tm, tn, tk = min(tm, m), min(tn, n), min(tk, k)
    assert m % tm == 0 and n % tn == 0 and k % tk == 0, (name, m, n, k)
    nk = k // tk
    n_add = len(adds)
    scales = [s for _, s in adds]
    jd, cols_d = _live_pieces(dead_n, tn) if dead_n else (None, None)
    kd, ks_d = _live_pieces(dead_k, tk) if dead_k else (None, None)

    def body(a_ref, b_ref, *rest):
        add_refs, o_ref = rest[:n_add], rest[n_add]
        acc_ref = rest[n_add + 1] if nk > 1 else None
        jj, kk = pl.program_id(1), pl.program_id(2)

        def product(c0, c1, ks):
            p = None
            for k0, k1 in ks:
                a_blk = a_ref[k0:k1, :] if ta else a_ref[:, k0:k1]
                b_blk = b_ref[c0:c1, k0:k1] if tb else b_ref[k0:k1, c0:c1]
                q = _dot(a_blk.astype(BF16), b_blk.astype(BF16), 0 if ta else 1, 1 if tb else 0)
                p = q if p is None else p + q
            return p

        def step(mode, cols, ks):
            for c0, c1 in cols:
                p = product(c0, c1, ks)
                if mode == "first":
                    acc_ref[:, c0:c1] = p
                elif mode == "mid":
                    acc_ref[:, c0:c1] += p
                else:
                    r = p if mode == "only" else acc_ref[:, c0:c1] + p
                    for ref, s in zip(add_refs, scales):
                        r = r + s * ref[:, c0:c1].astype(F32)
                    o_ref[:, c0:c1] = r.astype(out_dtype)

        modes = [("only", None, range(1))] if nk == 1 else (
            [("first", kk == 0, range(1))] + ([("mid", (kk > 0) & (kk < nk - 1), range(1, nk - 1))] if nk > 2 else [])
            + [("last", kk == nk - 1, range(nk - 1, nk))])
        for mode, when_k, k_range in modes:
            k_cases = [(when_k, [(0, tk)])]
            if kd is not None and kd in k_range:
                is_dead = kk == kd
                k_cases = [(is_dead, ks_d)] + ([(when_k & ~is_dead, [(0, tk)])] if len(k_range) > 1 else [])
            for cond_k, ks in k_cases:
                n_cases = [(None, [(0, tn)])] if jd is None else [(jj == jd, cols_d), (jj != jd, [(0, tn)])]
                for cond_n, cols in n_cases:
                    conds = [c for c in (cond_k, cond_n) if c is not None]
                    run = functools.partial(step, mode, cols, ks)
                    pl.when(functools.reduce(jnp.logical_and, conds))(run) if conds else run()

    a_spec = pl.BlockSpec((tk, tm), lambda i, j, kk: (kk, i)) if ta else pl.BlockSpec((tm, tk), lambda i, j, kk: (i, kk))
    b_spec = pl.BlockSpec((tn, tk), lambda i, j, kk: (j, kk)) if tb else pl.BlockSpec((tk, tn), lambda i, j, kk: (kk, j))
    add_specs = [pl.BlockSpec((tm, tn), lambda i, j, kk: (i, j)) for _ in adds]
    (out,), c_out = _hosted(
        body, comm, name=name,
        out_shape=[jax.ShapeDtypeStruct((m, n), out_dtype)],
        grid=(m // tm, n // tn, nk),
        in_specs=[a_spec, b_spec] + add_specs,
        out_specs=[pl.BlockSpec((tm, tn), lambda i, j, kk: (i, j))],
        scratch_shapes=[pltpu.VMEM((tm, tn), F32)] if nk > 1 else [],
        compiler_params=_params(("parallel", "parallel", "arbitrary"), vmem_mb),
        inputs=(a, b, *[x for x, _ in adds]))
    return out, c_out


NKB = KB // QB
LEAD = NKB - 1


def _band_bias(table):
    i = np.arange(QB)[:, None]
    j = np.arange(KB)[None, :]
    qc = i // CH + LEAD * QB // CH
    kc = j // CH
    valid = (kc <= qc) & (kc >= qc - LEFT)
    n = QB + KB
    c = np.arange(n)
    onehot = np.zeros((2 * MAX_REL + 1, n), np.float32)
    onehot[np.clip(LEAD * QB - (c - (QB - 1)), -MAX_REL, MAX_REL) + MAX_REL, c] = 1.0
    row = jnp.dot(table.astype(F32), jnp.asarray(onehot), precision=lax.Precision.HIGHEST)
    flow = jnp.tile(row, (1, QB))[:, :QB * (n - 1)].reshape(table.shape[0], QB, n - 1)
    return jnp.where(valid[None], flow[:, :, QB - 1:] * LOG2E, NEG)


def _bias_grad(dbias):
    h, n = dbias.shape[0], KB + 1
    flat = jnp.pad(dbias.reshape(h, QB * KB), ((0, 0), (0, -(QB * KB) % n)))
    diag = flat.reshape(h, -1, n).sum(axis=1)
    c = np.arange(n)
    jm = np.where(c < LEAD * QB + CH, c, c - n)
    didx = np.clip(LEAD * QB - jm, -MAX_REL, MAX_REL) + MAX_REL
    onehot = np.zeros((n, 2 * MAX_REL + 1), np.float32)
    onehot[c, didx] = 1.0
    return jnp.dot(diag, jnp.asarray(onehot), precision=lax.Precision.HIGHEST)


def _attn_scores(q, k_refs, cs, bias_h, m, masked, scale):
    parts = []
    for t in range(NKB):
        sc = _dot(q, k_refs[t][:, cs], 1, 1) * scale
        parts.append(jnp.where(m + t - LEAD >= 0, sc, NEG) if masked else sc)
    return jnp.concatenate(parts, axis=1) + bias_h


def _attn_fwd(h, r_in, bias):
    s = h.shape[0]
    nq = s // QB
    scale = A_DH ** -0.5

    def body(q_ref, *rest):
        k_refs, v_refs = rest[:NKB], rest[NKB:2 * NKB]
        bias_hbm, r_any, o_ref, lse_ref, bias_ref = rest[2 * NKB:]
        del r_any
        m = pl.program_id(0)

        @pl.when(m == 0)
        def _():
            pltpu.sync_copy(bias_hbm, bias_ref)

        def scores(hd, masked):
            cs = slice(hd * A_DH, (hd + 1) * A_DH)
            return _attn_scores(q_ref[:, cs], k_refs, cs, bias_ref[hd], m, masked, scale * LOG2E)

        def step(masked):
            sc_next = scores(0, masked)
            for hd in range(A_HEADS):
                cs = slice(hd * A_DH, (hd + 1) * A_DH)
                sc = sc_next
                if hd + 1 < A_HEADS:
                    sc_next = scores(hd + 1, masked)
                mx = jnp.max(sc, axis=1, keepdims=True)
                p = jnp.exp2(sc - mx)
                l = jnp.sum(p, axis=1, keepdims=True)
                pb = p.astype(BF16)
                o = _dot(pb[:, :QB], v_refs[0][:, cs], 1, 0)
                for t in range(1, NKB):
                    o += _dot(pb[:, t * QB:(t + 1) * QB], v_refs[t][:, cs], 1, 0)
                o_ref[:, cs] = (o / l).astype(BF16)
                lse_ref[:, hd:hd + 1] = mx + jnp.log2(l)

        pl.when(m < LEAD)(functools.partial(step, True))
        pl.when(m >= LEAD)(functools.partial(step, False))

    def kv_spec(col, t):
        return pl.BlockSpec((QB, A_W), lambda m: (jnp.maximum(m + t - LEAD, 0), col))

    (r, lse), _ = _hosted(
        body, None, name="attn_fwd",
        out_shape=(jax.ShapeDtypeStruct(r_in.shape, BF16), jax.ShapeDtypeStruct((s, A_HEADS), F32)),
        grid=(nq,),
        in_specs=[pl.BlockSpec((QB, A_W), lambda m: (m, 0))]
        + [kv_spec(1, t) for t in range(NKB)] + [kv_spec(2, t) for t in range(NKB)]
        + [pl.BlockSpec(memory_space=pl.ANY), pl.BlockSpec(memory_space=pl.ANY)],
        out_specs=(pl.BlockSpec((QB, A_W), lambda m: (m, 0)), pl.BlockSpec((QB, A_HEADS), lambda m: (m, 0))),
        scratch_shapes=[pltpu.VMEM((A_HEADS, QB, KB), F32)],
        input_output_aliases={2 * NKB + 2: 0},
        compiler_params=_params(("arbitrary",)),
        inputs=(h,) * (2 * NKB + 1) + (bias, r_in))
    return r, lse


def _attn_bwd(h, r, dy, lse, bias, dh_in, comm=None):
    s = h.shape[0]
    nq = s // QB
    scale = A_DH ** -0.5

    def body(q_ref, *rest):
        k_refs, v_refs = rest[:NKB], rest[NKB:2 * NKB]
        (z_ref, r_ref, dy_ref, lse_ref, zl_ref, rl_ref, dyl_ref, bias_hbm, dh_any,
         dh_ref, dbias_hbm, dk_acc, dv_acc, dq_ring, bias_ref, dbias_ref) = rest[2 * NKB:]
        del dh_any
        m = pl.program_id(0)

        def slot(b):
            return pl.multiple_of(lax.rem(b + NKB, NKB) * QB, QB)

        @pl.when(m == 0)
        def _():
            dk_acc[...] = jnp.zeros_like(dk_acc)
            dv_acc[...] = jnp.zeros_like(dv_acc)
            dq_ring[...] = jnp.zeros_like(dq_ring)
            dbias_ref[...] = jnp.zeros_like(dbias_ref)
            pltpu.sync_copy(bias_hbm, bias_ref)

        def step(masked):
            z = z_ref[...].astype(F32)
            do_all = dy_ref[...].astype(F32) * (z * _sigmoid(z))
            o_all = r_ref[...].astype(F32)

            def products(hd):
                cs = slice(hd * A_DH, (hd + 1) * A_DH)
                dob = do_all[:, cs].astype(BF16)
                sc = _attn_scores(q_ref[:, cs], k_refs, cs, bias_ref[hd], m, masked, scale * LOG2E)
                return sc, jnp.concatenate([_dot(dob, v_refs[t][:, cs], 1, 1) for t in range(NKB)], axis=1)

            ahead = products(0)
            for hd in range(A_HEADS):
                cs = slice(hd * A_DH, (hd + 1) * A_DH)
                q = q_ref[:, cs]
                do = do_all[:, cs]
                dob = do.astype(BF16)
                delta = jnp.sum(do * o_all[:, cs], axis=1, keepdims=True)
                sc, dp = ahead
                if hd + 1 < A_HEADS:
                    ahead = products(hd + 1)
                p = jnp.exp2(sc - lse_ref[:, hd:hd + 1])
                ds = p * (dp - delta)
                dbias_ref[hd] += ds
                pb, dsb = p.astype(BF16), ds.astype(BF16)
                dq = jnp.zeros((QB, A_DH), F32)
                for t in range(NKB):
                    ts = slice(t * QB, (t + 1) * QB)
                    rows = pl.ds(slot(m - LEAD + t), QB)
                    dq += _dot(dsb[:, ts], k_refs[t][:, cs], 1, 0)
                    dk_acc[rows, cs] += _dot(dsb[:, ts], q, 0, 0) * scale
                    dv_acc[rows, cs] += _dot(pb[:, ts], dob, 0, 0)
                dq_ring[pl.ds(slot(m), QB), cs] = dq * scale

        pl.when(m < LEAD)(functools.partial(step, True))
        pl.when((m >= LEAD) & (m < nq))(functools.partial(step, False))

        done = pl.ds(slot(m - LEAD), QB)
        zl = zl_ref[...].astype(F32)
        sg = _sigmoid(zl)
        dz = dyl_ref[...].astype(F32) * rl_ref[...].astype(F32) * (sg * (1.0 + zl * (1.0 - sg)))
        dh_ref[:, A_Q:A_Q + A_W] = dq_ring[done, :].astype(BF16)
        dh_ref[:, A_K:A_K + A_W] = dk_acc[done, :].astype(BF16)
        dh_ref[:, A_V:A_V + A_W] = dv_acc[done, :].astype(BF16)
        dh_ref[:, A_Z:A_Z + A_W] = dz.astype(BF16)
        dk_acc[done, :] = jnp.zeros((QB, A_W), F32)
        dv_acc[done, :] = jnp.zeros((QB, A_W), F32)

        @pl.when(m == nq + LEAD - 1)
        def _():
            pltpu.sync_copy(dbias_ref, dbias_hbm)

    last = nq - 1

    def cur(col):
        return pl.BlockSpec((QB, A_W), lambda m: (jnp.minimum(m, last), col))

    def kv_spec(col, t):
        return pl.BlockSpec((QB, A_W), lambda m: (jnp.clip(m + t - LEAD, 0, last), col))

    def lag(col):
        return pl.BlockSpec((QB, A_W), lambda m: (jnp.clip(m - LEAD, 0, last), col))

    (dh, dbias), c_out = _hosted(
        body, comm, name="attn_bwd",
        out_shape=(jax.ShapeDtypeStruct(dh_in.shape, BF16), jax.ShapeDtypeStruct((A_HEADS, QB, KB), F32)),
        grid=(nq + LEAD,),
        in_specs=[cur(0)] + [kv_spec(1, t) for t in range(NKB)] + [kv_spec(2, t) for t in range(NKB)]
        + [cur(3), cur(0), cur(0), pl.BlockSpec((QB, A_HEADS), lambda m: (jnp.minimum(m, last), 0)),
           lag(3), lag(0), lag(0),
           pl.BlockSpec(memory_space=pl.ANY), pl.BlockSpec(memory_space=pl.ANY)],
        out_specs=(pl.BlockSpec((QB, 4 * A_W), lambda m: (jnp.clip(m - LEAD, 0, last), 0)),
                   pl.BlockSpec(memory_space=pl.ANY)),
        scratch_shapes=[pltpu.VMEM((KB, A_W), F32), pltpu.VMEM((KB, A_W), F32), pltpu.VMEM((KB, A_W), F32),
                        pltpu.VMEM((A_HEADS, QB, KB), F32), pltpu.VMEM((A_HEADS, QB, KB), F32)],
        input_output_aliases={2 * NKB + 9: 0},
        compiler_params=_params(("arbitrary",), 56),
        inputs=(h,) * (2 * NKB + 2) + (r, dy, lse, h, r, dy, bias, dh_in))
    return dh, dbias, c_out


GB = 256
N_PAIR = B_HEADS // 2


def _gla_gates(lr, gw_ref, gb_ref):
    logit = _dot(lr, gw_ref[...], 1, 0) + gb_ref[...]
    lg = (jnp.minimum(logit, 0.0) - jnp.log(1.0 + jnp.exp(-jnp.abs(logit)))) / GATE_TAU
    row = lax.broadcasted_iota(jnp.int32, (GB, GB), 0)
    col = lax.broadcasted_iota(jnp.int32, (GB, GB), 1)
    tri = jnp.where((row // CH == col // CH) & (col <= row), 1.0, 0.0).astype(F32)
    return logit, _dot(tri, lg, 1, 0, precision=lax.Precision.HIGHEST)


def _gla_factors(hb_ref, b_all, c):
    rs = slice(c * CH, (c + 1) * CH)
    q = hb_ref[rs, 0:B_KW].astype(F32) * (B_DK ** -0.5)
    k = hb_ref[rs, B_KW:2 * B_KW].astype(F32)
    b = b_all[rs]
    bm, bl = b[CH // 2:CH // 2 + 1, :], b[CH - 1:CH, :]
    e1, e2, eb, ek = jnp.exp(b - bm), jnp.exp(bm - b), jnp.exp(b), jnp.exp(bl - b)
    el = jnp.exp(bl)
    return dict(ql=q * e1, kl=k * e2, qu=q * e2, ku=k * e1, qt=q * eb, kh=k * ek, e1=e1, e2=e2, eb=eb, ek=ek, el=el)


class _GlaPairTools:
    def __init__(self, hb_ref, fs):
        self.hb_ref, self.fs = hb_ref, fs
        row = lax.broadcasted_iota(jnp.int32, (2 * CH, 2 * CH), 0)
        col = lax.broadcasted_iota(jnp.int32, (2 * CH, 2 * CH), 1)
        self.same = (row // CH) == (col // CH)
        self.lower = self.same & ((row % CH) >= (col % CH))
        self.upper = self.same & ((row % CH) < (col % CH))
        self.lower_t = self.same & ((col % CH) >= (row % CH))
        self.upper_t = self.same & ((col % CH) < (row % CH))

    def lanes(self, c, p, name):
        return self.fs[c][name][:, p * 128:(p + 1) * 128]

    def heads(self, c, p, name):
        x = self.lanes(c, p, name)
        return jnp.where(self.same, jnp.concatenate([x, x], axis=0), 0.0).astype(BF16)

    def twice(self, c, p, name):
        x = self.lanes(c, p, name).astype(BF16)
        return jnp.concatenate([x, x], axis=0)

    def vals(self, c, p):
        rows = slice(c * CH, (c + 1) * CH)
        return jnp.concatenate([self.hb_ref[rows, 512 + (2 * p + sh) * B_DV:512 + (2 * p + sh + 1) * B_DV]
                                for sh in range(2)], axis=0)

    def intra(self, c, p):
        lo = _dot(self.heads(c, p, "ql"), self.twice(c, p, "kl"), 1, 1)
        up = _dot(self.heads(c, p, "qu"), self.twice(c, p, "ku"), 1, 1)
        return jnp.where(self.lower, lo, jnp.where(self.upper, up, 0.0)).astype(BF16)

    def fold(self, x):
        x = jnp.where(self.same, x, 0.0)
        return x[:CH] + x[CH:]


def _gla_fwd(h, r_in, gw, gb, ng):
    s = h.shape[0]
    nb = s // GB
    cpb = GB // CH

    def body(hb_ref, gw_ref, gb_ref, ng_ref, r_any, o_ref, opre_ref, st_ref, state):
        del r_any

        @pl.when(pl.program_id(0) == 0)
        def _():
            state[...] = jnp.zeros_like(state)

        _, b_all = _gla_gates(hb_ref[:, 1536:1664], gw_ref, gb_ref)
        fs = [_gla_factors(hb_ref, b_all, c) for c in range(cpb)]
        pairs = [(c, p) for c in range(cpb) for p in range(N_PAIR)]
        tools = _GlaPairTools(hb_ref, fs)
        a = {k: tools.intra(*k) for k in pairs}
        o_intra = {k: _dot(a[k], tools.vals(*k), 1, 0) for k in pairs}
        upd = {k: _dot(tools.vals(*k), tools.heads(*k, "kh"), 0, 0) for k in pairs}
        st = [state[p] for p in range(N_PAIR)]
        entering = {}
        for c, p in pairs:
            entering[c, p] = st[p]
            st_ref[c, p] = st[p]
            st[p] = st[p] * fs[c]["el"][:, p * 128:(p + 1) * 128] + upd[c, p]
        for p in range(N_PAIR):
            state[p] = st[p]
        for c, p in pairs:
            o2 = o_intra[c, p] + _dot(tools.heads(c, p, "qt"), entering[c, p].astype(BF16), 1, 1)
            for sh in range(2):
                o = o2[sh * CH:(sh + 1) * CH]
                rs, hs = slice(c * CH, (c + 1) * CH), slice((2 * p + sh) * B_DV, (2 * p + sh + 1) * B_DV)
                opre_ref[rs, hs] = o
                rinv = lax.rsqrt(jnp.mean(o * o, axis=1, keepdims=True) + RMS_EPS)
                o_ref[rs, hs] = (o * rinv * ng_ref[...]).astype(BF16)

    (r, opre, states), _ = _hosted(
        body, None, name="gla_fwd",
        out_shape=(jax.ShapeDtypeStruct(r_in.shape, BF16), jax.ShapeDtypeStruct((s, B_W), F32),
                   jax.ShapeDtypeStruct((s // CH, N_PAIR, 128, 128), F32)),
        grid=(nb,),
        in_specs=[pl.BlockSpec((GB, 2048), lambda i: (i, B_BASE // 2048)),
                  pl.BlockSpec((128, B_KW), lambda i: (0, 0)), pl.BlockSpec((1, B_KW), lambda i: (0, 0)),
                  pl.BlockSpec((1, B_DV), lambda i: (0, 0)), pl.BlockSpec(memory_space=pl.ANY)],
        out_specs=(pl.BlockSpec((GB, B_W), lambda i: (i, 1024 // B_W)), pl.BlockSpec((GB, B_W), lambda i: (i, 0)),
                   pl.BlockSpec((cpb, N_PAIR, 128, 128), lambda i: (i, 0, 0, 0))),
        scratch_shapes=[pltpu.VMEM((N_PAIR, 128, 128), F32)],
        input_output_aliases={4: 0},
        compiler_params=_params(("arbitrary",)),
        inputs=(h, gw, gb, ng, r_in))
    return r, opre, states


def _gla_bwd(h, dy, opre, states, gw, gb, ng, dh_in):
    s = h.shape[0]
    nb = s // GB
    cpb = GB // CH

    def body(hb_ref, dy_ref, opre_ref, st_ref, gw_ref, gb_ref, ng_ref, dh_any,
             dh_ref, dgw_ref, dgb_ref, dng_ref, dstate, db_scr, do_scr):
        del dh_any

        @pl.when(pl.program_id(0) == 0)
        def _():
            dstate[...] = jnp.zeros_like(dstate)
            dgw_ref[...] = jnp.zeros_like(dgw_ref)
            dgb_ref[...] = jnp.zeros_like(dgb_ref)
            dng_ref[...] = jnp.zeros_like(dng_ref)

        lr = hb_ref[:, 1536:1664]
        logit, b_all = _gla_gates(lr, gw_ref, gb_ref)
        z = hb_ref[:, 1024:1536].astype(F32)
        sg = _sigmoid(z)
        dyb = dy_ref[...].astype(F32)
        dng = jnp.zeros((1, B_DV), F32)
        for hd in range(B_HEADS):
            hs = slice(hd * B_DV, (hd + 1) * B_DV)
            o = opre_ref[:, hs]
            rinv = lax.rsqrt(jnp.mean(o * o, axis=1, keepdims=True) + RMS_EPS)
            on = o * rinv
            dr = dyb[:, hs] * (z[:, hs] * sg[:, hs])
            dh_ref[:, 1024 + hd * B_DV:1024 + (hd + 1) * B_DV] = (
                dyb[:, hs] * (on * ng_ref[...]) * (sg[:, hs] * (1.0 + z[:, hs] * (1.0 - sg[:, hs])))).astype(BF16)
            dng += jnp.sum(dr * on, axis=0, keepdims=True)
            dn = dr * ng_ref[...]
            do_scr[:, hs] = rinv * (dn - on * jnp.mean(dn * on, axis=1, keepdims=True))
        dng_ref[...] += dng

        rowi = lax.broadcasted_iota(jnp.int32, (CH, 128), 0)
        fs = [_gla_factors(hb_ref, b_all, c) for c in range(cpb)]
        pairs = [(c, p) for c in reversed(range(cpb)) for p in range(N_PAIR)]
        tools = _GlaPairTools(hb_ref, fs)
        pair = tools.lanes

        def douts(c, p):
            rows = slice(c * CH, (c + 1) * CH)
            return jnp.concatenate([do_scr[rows, (2 * p + sh) * B_DV:(2 * p + sh + 1) * B_DV] for sh in range(2)],
                                   axis=0).astype(BF16)

        at, da, dat, dst_own, g_qt = {}, {}, {}, {}, {}
        for k in pairs:
            c, p = k
            lo_t = _dot(tools.twice(*k, "kl"), tools.heads(*k, "ql"), 1, 1)
            up_t = _dot(tools.twice(*k, "ku"), tools.heads(*k, "qu"), 1, 1)
            at[k] = jnp.where(tools.lower_t, lo_t, jnp.where(tools.upper_t, up_t, 0.0)).astype(BF16)
            da[k] = _dot(douts(*k), tools.vals(*k), 1, 1)
            dat[k] = _dot(tools.vals(*k), douts(*k), 1, 1)
            dst_own[k] = _dot(douts(*k), tools.heads(*k, "qt"), 0, 0)
            g_qt[k] = tools.fold(_dot(douts(*k), st_ref[c, p].astype(BF16), 1, 0))
        dv, g_ql, g_qu, g_kl, g_ku = {}, {}, {}, {}, {}
        for k in pairs:
            dv[k] = _dot(at[k], douts(*k), 1, 0)
            g_ql[k] = tools.fold(_dot(jnp.where(tools.lower, da[k], 0.0).astype(BF16), tools.twice(*k, "kl"), 1, 0))
            g_qu[k] = tools.fold(_dot(jnp.where(tools.upper, da[k], 0.0).astype(BF16), tools.twice(*k, "ku"), 1, 0))
            g_kl[k] = tools.fold(_dot(jnp.where(tools.lower_t, dat[k], 0.0).astype(BF16), tools.heads(*k, "ql"), 1, 0))
            g_ku[k] = tools.fold(_dot(jnp.where(tools.upper_t, dat[k], 0.0).astype(BF16), tools.heads(*k, "qu"), 1, 0))
        dst = [dstate[p] for p in range(N_PAIR)]
        leaving = {}
        for c, p in pairs:
            leaving[c, p] = dst[p]
            dst[p] = dst[p] * pair(c, p, "el") + dst_own[c, p]
        for p in range(N_PAIR):
            dstate[p] = dst[p]
        g_kh = {}
        for k in pairs:
            c, p = k
            dstb = leaving[k].astype(BF16)
            dv2 = dv[k] + _dot(tools.heads(*k, "kh"), dstb, 1, 1)
            for sh in range(2):
                hd = 2 * p + sh
                dh_ref[c * CH:(c + 1) * CH, 512 + hd * B_DV:512 + (hd + 1) * B_DV] = dv2[sh * CH:(sh + 1) * CH].astype(BF16)
            g_kh[k] = tools.fold(_dot(tools.vals(*k), dstb, 1, 0))
        for c in reversed(range(cpb)):
            rs = slice(c * CH, (c + 1) * CH)
            for p in range(N_PAIR):
                k = (c, p)
                dq = (g_ql[k] * pair(c, p, "e1") + g_qu[k] * pair(c, p, "e2") + g_qt[k] * pair(c, p, "eb")) * (B_DK ** -0.5)
                dk = g_kl[k] * pair(c, p, "e2") + g_ku[k] * pair(c, p, "e1") + g_kh[k] * pair(c, p, "ek")
                dkh_kh = g_kh[k] * pair(c, p, "kh")
                db = (g_ql[k] * pair(c, p, "ql") - g_qu[k] * pair(c, p, "qu") + g_qt[k] * pair(c, p, "qt")
                      - g_kl[k] * pair(c, p, "kl") + g_ku[k] * pair(c, p, "ku") - dkh_kh)
                db_last = (pair(c, p, "el") * jnp.sum(leaving[c, p] * st_ref[c, p], axis=0, keepdims=True)
                           + jnp.sum(dkh_kh, axis=0, keepdims=True))
                db = jnp.where(rowi == CH - 1, db + db_last, db)
                dh_ref[rs, p * 128:(p + 1) * 128] = dq.astype(BF16)
                dh_ref[rs, B_KW + p * 128:B_KW + (p + 1) * 128] = dk.astype(BF16)
                db_scr[rs, p * 128:(p + 1) * 128] = db

        row = lax.broadcasted_iota(jnp.int32, (GB, GB), 0)
        col = lax.broadcasted_iota(jnp.int32, (GB, GB), 1)
        trit = jnp.where((row // CH == col // CH) & (col >= row), 1.0, 0.0).astype(F32)
        dlg = _dot(trit, db_scr[...], 1, 0, precision=lax.Precision.HIGHEST)
        dlogit = dlg * (_sigmoid(-logit) / GATE_TAU)
        dlb = dlogit.astype(BF16)
        dgw_ref[...] += _dot(lr, dlb, 0, 0)
        dgb_ref[...] += jnp.sum(dlogit, axis=0, keepdims=True)
        dh_ref[:, 1536:1664] = _dot(dlb, gw_ref[...], 1, 1).astype(BF16)
        dh_ref[:, 1664:2048] = jnp.zeros((GB, 384), BF16)

    rev = lambda i: nb - 1 - i
    (dh, dgw, dgb, dng), _ = _hosted(
        body, None, name="gla_bwd",
        out_shape=(jax.ShapeDtypeStruct(dh_in.shape, BF16), jax.ShapeDtypeStruct((128, B_KW), F32),
                   jax.ShapeDtypeStruct((1, B_KW), F32), jax.ShapeDtypeStruct((1, B_DV), F32)),
        grid=(nb,),
        in_specs=[pl.BlockSpec((GB, 2048), lambda i: (rev(i), B_BASE // 2048)),
                  pl.BlockSpec((GB, B_W), lambda i: (rev(i), 1024 // B_W)),
                  pl.BlockSpec((GB, B_W), lambda i: (rev(i), 0)),
                  pl.BlockSpec((cpb, N_PAIR, 128, 128), lambda i: (rev(i), 0, 0, 0)),
                  pl.BlockSpec((128, B_KW), lambda i: (0, 0)), pl.BlockSpec((1, B_KW), lambda i: (0, 0)),
                  pl.BlockSpec((1, B_DV), lambda i: (0, 0)), pl.BlockSpec(memory_space=pl.ANY)],
        out_specs=(pl.BlockSpec((GB, 2048), lambda i: (rev(i), B_BASE // 2048)),
                   pl.BlockSpec((128, B_KW), lambda i: (0, 0)), pl.BlockSpec((1, B_KW), lambda i: (0, 0)),
                   pl.BlockSpec((1, B_DV), lambda i: (0, 0))),
        scratch_shapes=[pltpu.VMEM((N_PAIR, 128, 128), F32), pltpu.VMEM((GB, B_KW), F32), pltpu.VMEM((GB, B_W), F32)],
        input_output_aliases={7: 0},
        compiler_params=_params(("arbitrary",)),
        inputs=(h, dy, opre, states, gw, gb, ng, dh_in))
    return dh, dgw, dgb, dng


MB = 512


def _mem_probs(q, mk, scale):
    sc = _dot(q, mk, 1, 1) * (scale * LOG2E)
    p = jnp.exp2(sc - jnp.max(sc, axis=1, keepdims=True))
    return p / jnp.sum(p, axis=1, keepdims=True)


def _mem_fwd(h, r_in, mkv):
    s = h.shape[0]
    scale = M_DH ** -0.5

    def body(q_ref, mkv_ref, r_any, o_ref):
        del r_any
        for hd in range(M_HEADS):
            cs = slice(hd * M_DH, (hd + 1) * M_DH)
            p = _mem_probs(q_ref[:, cs], mkv_ref[:, cs], scale)
            o_ref[:, cs] = _dot(p.astype(BF16), mkv_ref[:, M_W + hd * M_DH:M_W + (hd + 1) * M_DH], 1, 0).astype(BF16)

    return pl.pallas_call(
        body, name="mem_fwd",
        out_shape=jax.ShapeDtypeStruct(r_in.shape, BF16),
        grid=(s // MB,),
        in_specs=[pl.BlockSpec((MB, M_W), lambda i: (i, M_BASE // M_W)),
                  pl.BlockSpec((N_MEM, 2 * M_W), lambda i: (0, 0)), pl.BlockSpec(memory_space=pl.ANY)],
        out_specs=pl.BlockSpec((MB, M_W), lambda i: (i, 1536 // M_W)),
        input_output_aliases={2: 0},
        compiler_params=_params(("arbitrary",)),
    )(h, mkv, r_in)


def _mem_bwd(h, r, dy, mkv, dh_in):
    s = h.shape[0]
    scale = M_DH ** -0.5

    def body(q_ref, z_ref, r_ref, dy_ref, mkv_ref, dh_any, dh_ref, dmkv_ref):
        del dh_any

        @pl.when(pl.program_id(0) == 0)
        def _():
            dmkv_ref[...] = jnp.zeros_like(dmkv_ref)

        z = z_ref[...].astype(F32)
        sg = _sigmoid(z)
        dyv = dy_ref[...].astype(F32)
        do_all = dyv * (z * sg)
        dh_ref[:, M_W:2 * M_W] = (dyv * r_ref[...].astype(F32) * (sg * (1.0 + z * (1.0 - sg)))).astype(BF16)
        for hd in range(M_HEADS):
            cs = slice(hd * M_DH, (hd + 1) * M_DH)
            vs = slice(M_W + hd * M_DH, M_W + (hd + 1) * M_DH)
            q = q_ref[:, cs]
            p = _mem_probs(q, mkv_ref[:, cs], scale)
            dob = do_all[:, cs].astype(BF16)
            dp = _dot(dob, mkv_ref[:, vs], 1, 1)
            ds = p * (dp - jnp.sum(p * dp, axis=1, keepdims=True))
            dsb = ds.astype(BF16)
            dh_ref[:, cs] = (_dot(dsb, mkv_ref[:, cs], 1, 0) * scale).astype(BF16)
            dmkv_ref[:, cs] += _dot(dsb, q, 0, 0) * scale
            dmkv_ref[:, vs] += _dot(p.astype(BF16), dob, 0, 0)

    return pl.pallas_call(
        body, name="mem_bwd",
        out_shape=(jax.ShapeDtypeStruct(dh_in.shape, BF16), jax.ShapeDtypeStruct((N_MEM, 2 * M_W), F32)),
        grid=(s // MB,),
        in_specs=[pl.BlockSpec((MB, M_W), lambda i: (i, M_BASE // M_W)),
                  pl.BlockSpec((MB, M_W), lambda i: (i, M_BASE // M_W + 1)),
                  pl.BlockSpec((MB, M_W), lambda i: (i, 1536 // M_W)),
                  pl.BlockSpec((MB, M_W), lambda i: (i, 1536 // M_W)),
                  pl.BlockSpec((N_MEM, 2 * M_W), lambda i: (0, 0)), pl.BlockSpec(memory_space=pl.ANY)],
        out_specs=(pl.BlockSpec((MB, 2 * M_W), lambda i: (i, M_BASE // (2 * M_W))),
                   pl.BlockSpec((N_MEM, 2 * M_W), lambda i: (0, 0))),
        input_output_aliases={5: 0},
        compiler_params=_params(("arbitrary",)),
    )(h, h, r, dy, mkv, dh_in)


OB = 256


def _outproj_ln(h, r, w_out, x, ln_g, ln_b, target=None):
    s = h.shape[0]
    last = target is not None

    def body(za_ref, zb_ref, zm_ref, r_ref, w_ref, x_ref, g_ref, b_ref, *rest):
        if last:
            t_ref, xn_ref, xh_ref, rstd_ref, y_ref, l_ref = rest
        else:
            xn_ref, xh_ref, rstd_ref, y_ref = rest
        z = jnp.concatenate([za_ref[...], zb_ref[...], zm_ref[...]], axis=1).astype(F32)
        y = (r_ref[...].astype(F32) * (z * _sigmoid(z))).astype(BF16)
        y_ref[...] = y
        u = ALPHA * x_ref[...] + _dot(y, w_ref[...], 1, 0)
        mu = jnp.mean(u, axis=1, keepdims=True)
        uc = u - mu
        rstd = lax.rsqrt(jnp.mean(uc * uc, axis=1, keepdims=True) + LN_EPS)
        xh = uc * rstd
        xh_ref[...] = xh
        rstd_ref[...] = rstd
        xn = xh * g_ref[...] + b_ref[...]
        if last:
            @pl.when(pl.program_id(0) == 0)
            def _():
                l_ref[...] = jnp.zeros_like(l_ref)

            e = xn - t_ref[...]
            xn_ref[...] = e / D
            l_ref[...] += 0.5 * jnp.sum(jnp.mean(e * e, axis=1, keepdims=True))
        else:
            xn_ref[...] = xn

    row = lambda w, c: pl.BlockSpec((OB, w), lambda i: (i, c))
    vec = pl.BlockSpec((1, D), lambda i: (0, 0))
    full = jax.ShapeDtypeStruct((s, D), F32)
    return pl.pallas_call(
        body, name="outproj_ln_loss" if last else "outproj_ln",
        out_shape=(full, full, jax.ShapeDtypeStruct((s, 1), F32), jax.ShapeDtypeStruct((s, D), BF16))
        + ((jax.ShapeDtypeStruct((1, 128), F32),) if last else ()),
        grid=(s // OB,),
        in_specs=[row(A_W, A_Z // A_W), row(B_W, (B_BASE + 1024) // B_W), row(M_W, (M_BASE + M_W) // M_W), row(D, 0),
                  pl.BlockSpec((D, D), lambda i: (0, 0)), row(D, 0), vec, vec] + ([row(D, 0)] if last else []),
        out_specs=(row(D, 0), row(D, 0), pl.BlockSpec((OB, 1), lambda i: (i, 0)), row(D, 0))
        + ((pl.BlockSpec((1, 128), lambda i: (0, 0)),) if last else ()),
        compiler_params=_params(("arbitrary",), 56),
    )(h, h, h, r, w_out, x, ln_g, ln_b, *([target] if last else []))


def _ln_bwd_dy(g, xh, rstd, ln_g, w_out, y):
    s = g.shape[0]
    n = s // OB

    def body(g_ref, xh_ref, rstd_ref, lg_ref, w_ref, y_ref, dy_ref, du_ref, dg_ref, db_ref, dw_hbm, dw_acc, stage):
        i = pl.program_id(0)

        @pl.when(i == 0)
        def _():
            dg_ref[...] = jnp.zeros_like(dg_ref)
            db_ref[...] = jnp.zeros_like(db_ref)
            dw_acc[...] = jnp.zeros_like(dw_acc)

        gv, xh = g_ref[...], xh_ref[...]
        dg_ref[...] += jnp.sum(gv * xh, axis=0, keepdims=True)
        db_ref[...] += jnp.sum(gv, axis=0, keepdims=True)
        dxh = gv * lg_ref[...]
        du = rstd_ref[...] * (dxh - jnp.mean(dxh, axis=1, keepdims=True) - xh * jnp.mean(dxh * xh, axis=1, keepdims=True))
        du_ref[...] = du
        dub = du.astype(BF16)
        dy_ref[...] = _dot(dub, w_ref[...], 1, 1).astype(BF16)
        dw_acc[...] += _dot(y_ref[...], dub, 0, 0)

        @pl.when(i == n - 1)
        def _():
            for c in range(D // OB):
                stage[...] = dw_acc[c * OB:(c + 1) * OB, :].astype(BF16)
                pltpu.sync_copy(stage, dw_hbm.at[c * OB:(c + 1) * OB, :])

    row = pl.BlockSpec((OB, D), lambda i: (i, 0))
    vec = pl.BlockSpec((1, D), lambda i: (0, 0))
    return pl.pallas_call(
        body, name="ln_bwd_dy",
        out_shape=(jax.ShapeDtypeStruct((s, D), BF16), jax.ShapeDtypeStruct((s, D), F32),
                   jax.ShapeDtypeStruct((1, D), F32), jax.ShapeDtypeStruct((1, D), F32), jax.ShapeDtypeStruct((D, D), BF16)),
        grid=(n,),
        in_specs=[row, row, pl.BlockSpec((OB, 1), lambda i: (i, 0)), vec, pl.BlockSpec((D, D), lambda i: (0, 0)), row],
        out_specs=(row, row, vec, vec, pl.BlockSpec(memory_space=pl.ANY)),
        scratch_shapes=[pltpu.VMEM((D, D), F32), pltpu.VMEM((OB, D), BF16)],
        compiler_params=_params(("arbitrary",), 60),
    )(g, xh, rstd, ln_g, w_out, y)


class _LocalWeights:
    def __init__(self, w_in_p, w_out_f, w_kv_f):
        self.w = list(zip(w_in_p, w_out_f, w_kv_f))
        self.depth = len(self.w)
        self.grads = [dict() for _ in self.w]

    def weights(self, l):
        return self.w[l]

    def host(self, where, l, payload=None):
        if payload is not None:
            self.grads[l][where] = payload
        return None

    def landed(self, where, l, outs):
        pass


def _shard_pieces():
    sh = IN_W // N_DEV
    pieces = []
    for j in range(N_DEV):
        lo, hi = j * sh, (j + 1) * sh
        cuts = [lo, NAT_SPLIT, hi] if lo < NAT_SPLIT < hi else [lo, hi]
        for a, b in zip(cuts[:-1], cuts[1:]):
            pieces.append((j, a - lo, a if a < NAT_SPLIT else a + H_PAD, b - a))
    return pieces


RB = 256


def _shards_to_padded(raw):
    def body(x_ref, o_ref):
        for j, src, dst, width in _shard_pieces():
            o_ref[:, dst:dst + width] = x_ref[j, :, src:src + width]
        o_ref[:, NAT_SPLIT:NAT_SPLIT + H_PAD] = jnp.zeros((RB, H_PAD), o_ref.dtype)

    return pl.pallas_call(
        body, name="place_w_in", out_shape=jax.ShapeDtypeStruct((D, H_W), raw.dtype), grid=(D // RB,),
        in_specs=[pl.BlockSpec((N_DEV, RB, raw.shape[2]), lambda i: (0, i, 0))],
        out_specs=pl.BlockSpec((RB, H_W), lambda i: (i, 0)),
        compiler_params=_params(("parallel",)),
    )(raw)


def _padded_to_shards(w):
    sh = IN_W // N_DEV

    def body(x_ref, o_ref):
        for j, src, dst, width in _shard_pieces():
            o_ref[j, :, src:src + width] = x_ref[:, dst:dst + width]

    return pl.pallas_call(
        body, name="split_d_w_in", out_shape=jax.ShapeDtypeStruct((N_DEV, D, sh), w.dtype), grid=(D // RB,),
        in_specs=[pl.BlockSpec((RB, H_W), lambda i: (i, 0))],
        out_specs=pl.BlockSpec((N_DEV, RB, sh), lambda i: (0, i, 0)),
        compiler_params=_params(("parallel",)),
    )(w)


_PART_A = (0, 1, 2, 4, 7)
_PART_B = (3, 5, 6)


class _Fsdp:
    def __init__(self, w_in, w_out, w_kv, extra):
        self.sh = (w_in, w_out, w_kv)
        self.depth = w_in.shape[0]
        self.raw = [dict() for _ in range(self.depth)]
        self.recv = [dict() for _ in range(self.depth)]
        g_in, g_out, g_kv, *self.extra = _gather_now([w_in[0], w_out[0], w_kv[0]] + list(extra), "gather_layer0")
        self.raw[0] = dict(w_in=g_in, w_out=g_out, w_kv=g_kv)

    def weights(self, l):
        raw = self.raw[l]
        return _shards_to_padded(raw["w_in"]), raw["w_out"].reshape(D, D), raw["w_kv"].reshape(D, 2 * M_W)

    def host(self, where, l, payload=None):
        w_in, w_out, w_kv = self.sh
        if where == "in_proj" and l + 1 < self.depth:
            return _Gather([w_in[l + 1], w_out[l + 1], w_kv[l + 1]])
        if where == "d_w_in":
            d_wout, d_wkv = payload
            return _Scatter([d_wout.reshape(N_DEV, D // N_DEV, D), d_wkv.reshape(N_DEV, D // N_DEV, 2 * M_W)])
        if where == "d_x":
            self.blocks = _padded_to_shards(payload)
            return _Scatter([self.blocks], relations=_PART_A if l > 0 else tuple(range(N_DEV)))
        if where == "attn_bwd" and l + 1 < self.depth:
            return _Scatter([self.blocks], relations=_PART_B, into=[self.recv[l + 1]["w_in"]])
        return None

    def landed(self, where, l, outs):
        if where == "in_proj" and outs:
            self.raw[l + 1] = dict(zip(("w_in", "w_out", "w_kv"), outs))
        elif where == "d_w_in":
            self.recv[l]["w_out"], self.recv[l]["w_kv"] = outs
        elif where == "d_x":
            self.recv[l]["w_in"] = outs[0]
        elif where == "attn_bwd" and outs:
            self.recv[l + 1]["w_in"] = outs[0]


def _local_step(x, mem, target, pipe, rel, gate_w, gate_b, norm_g, ln_g, ln_b):
    depth = pipe.depth
    s = x.shape[0]
    saved = []
    xl = x
    for l in range(depth):
        w_in_p, w_out_f, w_kv_f = pipe.weights(l)
        hmat, landed = _mm(xl, w_in_p, out_dtype=BF16, tm=1024, tn=1024, tk=D, name="in_proj",
                           comm=pipe.host("in_proj", l), dead_n=H_DEAD)
        pipe.landed("in_proj", l, landed)
        mkv, _ = _mm(mem, w_kv_f, out_dtype=BF16, tm=N_MEM, tn=1024, tk=D, name="mem_kv")
        bias = _band_bias(rel[l])
        gw = jnp.zeros((128, B_KW), F32).at[:GATE_RANK].set(gate_w[l]).astype(BF16)
        gb, ng = gate_b[l][None, :], norm_g[l][None, :]
        r, lse = _attn_fwd(hmat, lax.empty((s, D), BF16), bias)
        r, opre, states = _gla_fwd(hmat, r, gw, gb, ng)
        r = _mem_fwd(hmat, r, mkv)
        xn, xh, rstd, y, *loss = _outproj_ln(hmat, r, w_out_f, xl, ln_g[l][None, :], ln_b[l][None, :],
                                             target if l == depth - 1 else None)
        saved.append(dict(x=xl, h=hmat, mkv=mkv, bias=bias, gw=gw, gb=gb, ng=ng, r=r, lse=lse, opre=opre,
                          states=states, xh=xh, rstd=rstd, y=y, w_in_p=w_in_p, w_out_f=w_out_f))
        xl = xn
    (loss,), g = loss, xl

    grads = [None] * depth
    for l in reversed(range(depth)):
        sv = saved[l]
        dy, du, d_lng, d_lnb, d_wout = _ln_bwd_dy(g, sv["xh"], sv["rstd"], ln_g[l][None, :], sv["w_out_f"], sv["y"])
        dh, dbias, landed = _attn_bwd(sv["h"], sv["r"], dy, sv["lse"], sv["bias"], lax.empty((s, H_W), BF16),
                                      pipe.host("attn_bwd", l))
        pipe.landed("attn_bwd", l, landed)
        dh, d_gw, d_gb, d_ng = _gla_bwd(sv["h"], dy, sv["opre"], sv["states"], sv["gw"], sv["gb"], sv["ng"], dh)
        dh, d_mkv = _mem_bwd(sv["h"], sv["r"], dy, sv["mkv"], dh)
        d_wkv, _ = _mm(mem, d_mkv, ta=True, out_dtype=BF16, tm=1024, tn=1024, tk=N_MEM, name="d_w_kv")
        d_win, landed = _mm(sv["x"], dh, ta=True, out_dtype=BF16, tm=1024, tn=1792, tk=1024, name="d_w_in",
                            comm=pipe.host("d_w_in", l, (d_wout, d_wkv)), dead_n=H_DEAD)
        pipe.landed("d_w_in", l, landed)
        g, landed = _mm(dh, sv["w_in_p"], tb=True, out_dtype=F32, tm=1024, tn=1024, tk=1792, name="d_x",
                        adds=((du, ALPHA),), comm=pipe.host("d_x", l, d_win), dead_k=H_DEAD)
        pipe.landed("d_x", l, landed)
        grads[l] = dict(rel=_bias_grad(dbias), gate_w=d_gw[:GATE_RANK], gate_b=d_gb[0], norm_g=d_ng[0],
                        ln_g=d_lng[0], ln_b=d_lnb[0])
    return loss, g, grads


def _adamw(parts, w, m, v, rows_per_step, name):
    depth, rows, cols = w.shape
    n = parts[0].shape[0]
    tr = min(rows_per_step, rows)
    assert rows % tr == 0 and len(parts) == depth

    def body(*refs):
        p_refs = refs[:depth]
        w_ref, m_ref, v_ref, g_ref, d_ref, nm_ref, nv_ref = refs[depth:]
        for l in range(depth):
            @pl.when(pl.program_id(0) == l)
            def _(p_ref=p_refs[l]):
                g = p_ref[0].astype(F32)
                for j in range(1, n):
                    g = g + p_ref[j].astype(F32)
                nm = ADAM_B1 * m_ref[...] + (1.0 - ADAM_B1) * g
                nv = ADAM_B2 * v_ref[...] + (1.0 - ADAM_B2) * (g * g)
                m_hat = nm / (1.0 - ADAM_B1 ** ADAM_STEP)
                v_hat = nv / (1.0 - ADAM_B2 ** ADAM_STEP)
                g_ref[...] = g
                nm_ref[...] = nm
                nv_ref[...] = nv
                d_ref[...] = -ADAM_LR * (m_hat / (jnp.sqrt(v_hat) + ADAM_EPS) + ADAM_WD * w_ref[...])

    def part_spec(l):
        return pl.BlockSpec((n, tr, cols), lambda ll, i: (0, jnp.where(ll == l, i, 0), 0))

    blk = pl.BlockSpec((None, tr, cols), lambda ll, i: (ll, i, 0))
    shape = jax.ShapeDtypeStruct((depth, rows, cols), F32)
    return pl.pallas_call(
        body, name=name,
        out_shape=(shape, shape, shape, shape),
        grid=(depth, rows // tr),
        in_specs=[part_spec(l) for l in range(depth)] + [blk, blk, blk],
        out_specs=(blk, blk, blk, blk),
        compiler_params=_params(("arbitrary", "arbitrary")),
    )(*parts, w, m, v)


SMALL = (("rel", A_HEADS * (2 * MAX_REL + 1)), ("gate_w", GATE_RANK * B_KW), ("gate_b", B_KW), ("norm_g", B_DV),
         ("ln_g", D), ("ln_b", D))


def _pack_small(parts, depth):
    rows = []
    for name, size in SMALL:
        flat = parts[name].reshape(depth * size).astype(F32)
        rows.append(jnp.pad(flat, (0, -(depth * size) % 128)).reshape(-1, 128))
    packed = jnp.concatenate(rows, axis=0)
    return jnp.pad(packed, ((0, -packed.shape[0] % 8), (0, 0)))


def _unpack_small(packed, depth, shapes):
    out, row = {}, 0
    for name, size in SMALL:
        nrow = -(-(depth * size) // 128)
        out[name] = packed[row:row + nrow].reshape(-1)[:depth * size].reshape(shapes[name])
        row += nrow
    return out


def kernel(x, mem, w_in, a_rel_bias, b_gate_w, b_gate_b, b_norm_g, w_mem_kv, w_out, ln_g, ln_b, loss_target, m_w_in, m_a_rel_bias, m_b_gate_w, m_b_gate_b, m_b_norm_g, m_w_mem_kv, m_w_out, m_ln_g, m_ln_b, v_w_in, v_a_rel_bias, v_b_gate_w, v_b_gate_b, v_b_norm_g, v_w_mem_kv, v_w_out, v_ln_g, v_ln_b):
    depth = w_in.shape[0]
    sh_in = w_in.shape[2]
    sh_gw = b_gate_w.shape[2]
    me = 4 * lax.axis_index("x") + 2 * lax.axis_index("y") + lax.axis_index("c")

    pipe = _Fsdp(w_in.astype(BF16), w_out.astype(BF16), w_mem_kv.astype(BF16), [b_gate_w])
    gate_w_full = jnp.transpose(pipe.extra[0], (1, 2, 0, 3)).reshape(depth, GATE_RANK, N_DEV * sh_gw)
    loss_dev, dx, grads = _local_step(x[0], mem[0], loss_target[0], pipe,
                                      a_rel_bias, gate_w_full, b_gate_b, b_norm_g, ln_g, ln_b)
    loss = lax.psum(loss_dev[0, 0], ("x", "y", "c"))

    recv = lambda n: [pipe.recv[l][n] for l in range(depth)]
    big = {"w_in": _adamw(recv("w_in"), w_in, m_w_in, v_w_in, 128, "adamw_w_in"),
           "w_out": _adamw(recv("w_out"), w_out, m_w_out, v_w_out, 64, "adamw_w_out"),
           "w_kv": _adamw(recv("w_kv"), w_mem_kv, m_w_mem_kv, v_w_mem_kv, 128, "adamw_w_kv")}

    shapes = {"rel": a_rel_bias.shape, "gate_w": (depth, GATE_RANK, N_DEV * sh_gw), "gate_b": b_gate_b.shape,
              "norm_g": b_norm_g.shape, "ln_g": ln_g.shape, "ln_b": ln_b.shape}
    part = _pack_small({n: jnp.stack([grads[l][n] for l in range(depth)]) for n, _ in SMALL}, depth)
    (all_parts,) = _gather_now([part], "gather_small")
    zeros_gw = jnp.zeros(shapes["gate_w"], F32)
    w_s = _pack_small(dict(rel=a_rel_bias, gate_w=zeros_gw, gate_b=b_gate_b, norm_g=b_norm_g, ln_g=ln_g, ln_b=ln_b), depth)
    m_s = _pack_small(dict(rel=m_a_rel_bias, gate_w=zeros_gw, gate_b=m_b_gate_b, norm_g=m_b_norm_g, ln_g=m_ln_g, ln_b=m_ln_b), depth)
    v_s = _pack_small(dict(rel=v_a_rel_bias, gate_w=zeros_gw, gate_b=v_b_gate_b, norm_g=v_b_norm_g, ln_g=v_ln_g, ln_b=v_ln_b), depth)
    small = [_unpack_small(t[0], depth, shapes)
             for t in _adamw([all_parts], w_s[None], m_s[None], v_s[None], all_parts.shape[1], "adamw_small")]
    gw_grad = lax.dynamic_slice_in_dim(small[0]["gate_w"], me * sh_gw, sh_gw, axis=2).reshape(1, depth * GATE_RANK, sh_gw)
    flat = lambda t: t.reshape(1, depth * GATE_RANK, sh_gw)
    gw_res = [t.reshape(depth, GATE_RANK, sh_gw)
              for t in _adamw([gw_grad], flat(b_gate_w), flat(m_b_gate_w), flat(v_b_gate_w), depth * GATE_RANK, "adamw_gate_w")]

    def leaves(t):
        return (big["w_in"][t], small[t]["rel"], gw_res[t], small[t]["gate_b"], small[t]["norm_g"],
                big["w_kv"][t], big["w_out"][t], small[t]["ln_g"], small[t]["ln_b"])

    return (loss, dx[None]) + leaves(0) + leaves(1) + leaves(2) + leaves(3)
```

```python
import functools
import math

import numpy as np
import jax
import jax.numpy as jnp
from jax import lax
from jax.experimental import pallas as pl
from jax.experimental.pallas import tpu as pltpu

F32 = jnp.float32
BF16 = jnp.bfloat16

N_DEV = 8
D = 2048
CH = 64
LEFT = 8
MAX_REL = 128
N_MEM = 256
A_HEADS, A_DH, A_W = 8, 128, 1024
B_HEADS, B_DK, B_DV, B_KW, B_W = 4, 64, 128, 256, 512
GATE_RANK, GATE_TAU = 16, 16.0
M_HEADS, M_DH, M_W = 4, 128, 512
IN_W = 6672
NAT_SPLIT = 5648
H_W = 7168
H_PAD = H_W - IN_W
A_Q, A_K, A_V, A_Z = 0, 1024, 2048, 3072
B_BASE = 4096
M_BASE = 6144
H_DEAD = (B_BASE + 1664, M_BASE)
ALPHA = (2.0 * 4) ** 0.25
LOG2E = math.log2(math.e)
LN_EPS = 1e-5
RMS_EPS = 1e-6
NEG = -1e30
QB = 256
KB = 3 * QB
ADAM_LR, ADAM_B1, ADAM_B2, ADAM_EPS, ADAM_WD, ADAM_STEP = 0.001, 0.9, 0.999, 1e-08, 0.01, 10
VMEM_MB = 1024 * 1024


def _params(sem, vmem_mb=48):
    return pltpu.CompilerParams(dimension_semantics=sem, vmem_limit_bytes=vmem_mb * VMEM_MB)


def _sigmoid(x):
    return 1.0 / (1.0 + jnp.exp(-x))


def _dot(a, b, ca, cb, precision=None):
    return lax.dot_general(a, b, (((ca,), (cb,)), ((), ())), preferred_element_type=F32, precision=precision)


MESH = pl.DeviceIdType.MESH


class _Scatter:
    def __init__(self, xs, relations=tuple(range(N_DEV)), into=None):
        self.xs, self.n = list(xs), len(xs)
        self.relations, self.into = tuple(relations), list(into or [])
        self.out_shape = [jax.ShapeDtypeStruct(x.shape, x.dtype) for x in xs]
        self.specs = [pl.BlockSpec(memory_space=pltpu.HBM)] * self.n
        self.scratch = [pltpu.SemaphoreType.DMA((self.n, N_DEV)), pltpu.SemaphoreType.DMA((self.n, N_DEV)),
                        pltpu.SemaphoreType.DMA((self.n,))]

    def _copies(self, x_refs, o_refs, sems):
        send_sems, recv_sems, local_sems = sems
        mx, my, mc = lax.axis_index("x"), lax.axis_index("y"), lax.axis_index("c")
        me = 4 * mx + 2 * my + mc
        own, sends, arrivals = [], [], []
        for k in self.relations:
            if k == 0:
                own = [pltpu.make_async_copy(x_refs[a].at[me], o_refs[a].at[me], local_sems.at[a]) for a in range(self.n)]
                continue
            px = 1 - mx if k & 4 else mx
            py = 1 - my if k & 2 else my
            pc = 1 - mc if k & 1 else mc
            idx = 4 * px + 2 * py + pc
            for a in range(self.n):
                for dst, group in ((o_refs[a].at[me], sends), (o_refs[a].at[idx], arrivals)):
                    group.append(pltpu.make_async_remote_copy(
                        src_ref=x_refs[a].at[idx], dst_ref=dst, send_sem=send_sems.at[a, k], recv_sem=recv_sems.at[a, k],
                        device_id=(px, py, pc), device_id_type=MESH))
        return own, sends, arrivals

    def start(self, x_refs, o_refs, sems):
        own, sends, _ = self._copies(x_refs, o_refs, sems)
        for cp in own + sends:
            cp.start()

    def middle(self, x_refs, o_refs, sems):
        pass

    def finish(self, x_refs, o_refs, sems):
        own, sends, arrivals = self._copies(x_refs, o_refs, sems)
        for cp in sends:
            cp.wait_send()
        for cp in arrivals:
            cp.wait_recv()
        for cp in own:
            cp.wait()


class _Gather:
    into = ()

    def __init__(self, xs):
        self.xs, self.n = list(xs), len(xs)
        self.out_shape = [jax.ShapeDtypeStruct((N_DEV,) + x.shape, x.dtype) for x in xs]
        self.specs = [pl.BlockSpec(memory_space=pltpu.HBM)] * self.n
        self.scratch = [pltpu.SemaphoreType.DMA((self.n, N_DEV - 1)), pltpu.SemaphoreType.DMA((self.n, N_DEV - 1)),
                        pltpu.SemaphoreType.DMA((self.n,))]

    def _copies(self, x_refs, o_refs, sems):
        send_sems, recv_sems, local_sems = sems
        mx, my, mc = lax.axis_index("x"), lax.axis_index("y"), lax.axis_index("c")
        idx = lambda px, py, pc: 4 * px + 2 * py + pc
        me, sibling = (mx, my, mc), (mx, my, 1 - mc)
        chips = [(mx, 1 - my), (1 - mx, my), (1 - mx, 1 - my)]

        def copy(a, k, src, slot, to):
            return pltpu.make_async_remote_copy(
                src_ref=src, dst_ref=o_refs[a].at[idx(*slot)], send_sem=send_sems.at[a, k], recv_sem=recv_sems.at[a, k],
                device_id=to, device_id_type=MESH)

        c = dict(own=[], first=[], passed=[], ici_in=[], late_in=[])
        for a in range(self.n):
            x = x_refs[a]
            c["own"].append(pltpu.make_async_copy(x, o_refs[a].at[idx(*me)], local_sems.at[a]))
            c["first"].append(copy(a, 0, x, me, sibling))
            c["late_in"].append(copy(a, 0, x, sibling, sibling))
            for j, chip in enumerate(chips):
                c["first"].append(copy(a, 1 + j, x, me, (*chip, mc)))
                c["ici_in"].append(copy(a, 1 + j, x, (*chip, mc), (*chip, mc)))
                c["passed"].append(copy(a, 4 + j, o_refs[a].at[idx(*chip, mc)], (*chip, mc), sibling))
                c["late_in"].append(copy(a, 4 + j, x, (*chip, 1 - mc), sibling))
        return c

    def start(self, x_refs, o_refs, sems):
        c = self._copies(x_refs, o_refs, sems)
        for cp in c["own"] + c["first"]:
            cp.start()

    def middle(self, x_refs, o_refs, sems):
        c = self._copies(x_refs, o_refs, sems)
        for arrived, onward in zip(c["ici_in"], c["passed"]):
            arrived.wait_recv()
            onward.start()

    def finish(self, x_refs, o_refs, sems):
        c = self._copies(x_refs, o_refs, sems)
        for cp in c["first"] + c["passed"]:
            cp.wait_send()
        for cp in c["late_in"]:
            cp.wait_recv()
        for cp in c["own"]:
            cp.wait()


def _gather_now(xs, name):
    comm = _Gather(xs)

    def body(*refs):
        x_refs, o_refs, sems = refs[:comm.n], refs[comm.n:2 * comm.n], refs[2 * comm.n:]
        comm.start(x_refs, o_refs, sems)
        comm.middle(x_refs, o_refs, sems)
        comm.finish(x_refs, o_refs, sems)

    return pl.pallas_call(body, name=name, out_shape=tuple(comm.out_shape), in_specs=comm.specs,
                          out_specs=tuple(comm.specs), scratch_shapes=comm.scratch)(*comm.xs)


def _hosted(body, comm, *, name, grid, in_specs, out_specs, out_shape, scratch_shapes, compiler_params, inputs,
            input_output_aliases=None):
    out_specs, out_shape = tuple(out_specs), tuple(out_shape)
    aliases = input_output_aliases or {}
    if comm is None:
        outs = pl.pallas_call(body, name=name, grid=grid, in_specs=list(in_specs), out_specs=out_specs, out_shape=out_shape,
                              scratch_shapes=list(scratch_shapes), compiler_params=compiler_params,
                              input_output_aliases=aliases)(*inputs)
        return tuple(outs), ()
    ni, no, ns, nc = len(in_specs), len(out_specs), len(scratch_shapes), comm.n
    n_into = len(comm.into)
    aliases = {**aliases, **{ni + nc + a: no + a for a in range(n_into)}}

    def wrapped(*refs):
        ins, c_in = refs[:ni], refs[ni:ni + nc]
        refs = refs[ni + nc + n_into:]
        outs, c_out = refs[:no], refs[no:no + nc]
        scr, sems = refs[no + nc:no + nc + ns], refs[no + nc + ns:]
        first = functools.reduce(jnp.logical_and, [pl.program_id(d) == 0 for d in range(len(grid))])
        last = functools.reduce(jnp.logical_and, [pl.program_id(d) == grid[d] - 1 for d in range(len(grid))])

        step = functools.reduce(lambda acc, d: acc * grid[d] + pl.program_id(d), range(len(grid)), 0)
        n_steps = math.prod(grid)

        @pl.when(first)
        def _():
            comm.start(c_in, c_out, sems)

        body(*ins, *outs, *scr)

        @pl.when(step == max(7 * n_steps // 8, 1) - 1)
        def _():
            comm.middle(c_in, c_out, sems)

        @pl.when(last)
        def _():
            comm.finish(c_in, c_out, sems)

    params = pltpu.CompilerParams(dimension_semantics=("arbitrary",) * len(grid),
                                  vmem_limit_bytes=compiler_params.vmem_limit_bytes)
    outs = pl.pallas_call(wrapped, name=name, grid=grid, in_specs=list(in_specs) + comm.specs + comm.specs[:n_into],
                          out_specs=out_specs + tuple(comm.specs), out_shape=out_shape + tuple(comm.out_shape),
                          scratch_shapes=list(scratch_shapes) + comm.scratch, compiler_params=params,
                          input_output_aliases=aliases)(*inputs, *comm.xs, *comm.into)
    return tuple(outs[:no]), tuple(outs[no:])


def _live_pieces(dead, tile):
    lo, hi = dead
    t = lo // tile
    assert (hi - 1) // tile == t and lo % 128 == 0 and hi % 128 == 0
    return t, [(x, y) for x, y in ((0, lo - t * tile), (hi - t * tile, tile)) if y > x]


def _mm(a, b, *, ta=False, tb=False, out_dtype, tm, tn, tk, name, adds=(), vmem_mb=48, comm=None, dead_n=None, dead_k=None):
    m = a.shape[1] if ta else a.shape[0]
    k = a.shape[0] if ta else a.shape[1]
    n = b.shape[0] if tb else b.shape[1]
    assert k == (b.shape[1] if tb else b.shape[0])
    tm, tn, tk = min(tm, m), min(tn, n), min(tk, k)
    assert m % tm == 0 and n % tn == 0 and k % tk == 0, (name, m, n, k)
    nk = k // tk
    n_add = len(adds)
    scales = [s for _, s in adds]
    jd, cols_d = _live_pieces(dead_n, tn) if dead_n else (None, None)
    kd, ks_d = _live_pieces(dead_k, tk) if dead_k else (None, None)

    def body(a_ref, b_ref, *rest):
        add_refs, o_ref = rest[:n_add], rest[n_add]
        acc_ref = rest[n_add + 1] if nk > 1 else None
        jj, kk = pl.program_id(1), pl.program_id(2)

        def product(c0, c1, ks):
            p = None
            for k0, k1 in ks:
                a_blk = a_ref[k0:k1, :] if ta else a_ref[:, k0:k1]
                b_blk = b_ref[c0:c1, k0:k1] if tb else b_ref[k0:k1, c0:c1]
                q = _dot(a_blk.astype(BF16), b_blk.astype(BF16), 0 if ta else 1, 1 if tb else 0)
                p = q if p is None else p + q
            return p

        def step(mode, cols, ks):
            for c0, c1 in cols:
                p = product(c0, c1, ks)
                if mode == "first":
                    acc_ref[:, c0:c1] = p
                elif mode == "mid":
                    acc_ref[:, c0:c1] += p
                else:
                    r = p if mode == "only" else acc_ref[:, c0:c1] + p
                    for ref, s in zip(add_refs, scales):
                        r = r + s * ref[:, c0:c1].astype(F32)
                    o_ref[:, c0:c1] = r.astype(out_dtype)

        modes = [("only", None, range(1))] if nk == 1 else (
            [("first", kk == 0, range(1))] + ([("mid", (kk > 0) & (kk < nk - 1), range(1, nk - 1))] if nk > 2 else [])
            + [("last", kk == nk - 1, range(nk - 1, nk))])
        for mode, when_k, k_range in modes:
            k_cases = [(when_k, [(0, tk)])]
            if kd is not None and kd in k_range:
                is_dead = kk == kd
                k_cases = [(is_dead, ks_d)] + ([(when_k & ~is_dead, [(0, tk)])] if len(k_range) > 1 else [])
            for cond_k, ks in k_cases:
                n_cases = [(None, [(0, tn)])] if jd is None else [(jj == jd, cols_d), (jj != jd, [(0, tn)])]
                for cond_n, cols in n_cases:
                    conds = [c for c in (cond_k, cond_n) if c is not None]
                    run = functools.partial(step, mode, cols, ks)
                    pl.when(functools.reduce(jnp.logical_and, conds))(run) if conds else run()

    a_spec = pl.BlockSpec((tk, tm), lambda i, j, kk: (kk, i)) if ta else pl.BlockSpec((tm, tk), lambda i, j, kk: (i, kk))
    b_spec = pl.BlockSpec((tn, tk), lambda i, j, kk: (j, kk)) if tb else pl.BlockSpec((tk, tn), lambda i, j, kk: (kk, j))
    add_specs = [pl.BlockSpec((tm, tn), lambda i, j, kk: (i, j)) for _ in adds]
    (out,), c_out = _hosted(
        body, comm, name=name,
        out_shape=[jax.ShapeDtypeStruct((m, n), out_dtype)],
        grid=(m // tm, n // tn, nk),
        in_specs=[a_spec, b_spec] + add_specs,
        out_specs=[pl.BlockSpec((tm, tn), lambda i, j, kk: (i, j))],
        scratch_shapes=[pltpu.VMEM((tm, tn), F32)] if nk > 1 else [],
        compiler_params=_params(("parallel", "parallel", "arbitrary"), vmem_mb),
        inputs=(a, b, *[x for x, _ in adds]))
    return out, c_out


NKB = KB // QB
LEAD = NKB - 1


def _band_bias(table):
    i = np.arange(QB)[:, None]
    j = np.arange(KB)[None, :]
    qc = i // CH + LEAD * QB // CH
    kc = j // CH
    valid = (kc <= qc) & (kc >= qc - LEFT)
    n = QB + KB
    c = np.arange(n)
    onehot = np.zeros((2 * MAX_REL + 1, n), np.float32)
    onehot[np.clip(LEAD * QB - (c - (QB - 1)), -MAX_REL, MAX_REL) + MAX_REL, c] = 1.0
    row = jnp.dot(table.astype(F32), jnp.asarray(onehot), precision=lax.Precision.HIGHEST)
    flow = jnp.tile(row, (1, QB))[:, :QB * (n - 1)].reshape(table.shape[0], QB, n - 1)
    return jnp.where(valid[None], flow[:, :, QB - 1:] * LOG2E, NEG)


def _bias_grad(dbias):
    h, n = dbias.shape[0], KB + 1
    flat = jnp.pad(dbias.reshape(h, QB * KB), ((0, 0), (0, -(QB * KB) % n)))
    diag = flat.reshape(h, -1, n).sum(axis=1)
    c = np.arange(n)
    jm = np.where(c < LEAD * QB + CH, c, c - n)
    didx = np.clip(LEAD * QB - jm, -MAX_REL, MAX_REL) + MAX_REL
    onehot = np.zeros((n, 2 * MAX_REL + 1), np.float32)
    onehot[c, didx] = 1.0
    return jnp.dot(diag, jnp.asarray(onehot), precision=lax.Precision.HIGHEST)


def _attn_scores(q, k_refs, cs, bias_h, m, masked, scale):
    parts = []
    for t in range(NKB):
        sc = _dot(q, k_refs[t][:, cs], 1, 1) * scale
        parts.append(jnp.where(m + t - LEAD >= 0, sc, NEG) if masked else sc)
    return jnp.concatenate(parts, axis=1) + bias_h


def _attn_fwd(h, r_in, bias):
    s = h.shape[0]
    nq = s // QB
    scale = A_DH ** -0.5

    def body(q_ref, *rest):
        k_refs, v_refs = rest[:NKB], rest[NKB:2 * NKB]
        bias_hbm, r_any, o_ref, linv_ref, p_ref, bias_ref = rest[2 * NKB:]
        del r_any
        m = pl.program_id(0)

        @pl.when(m == 0)
        def _():
            pltpu.sync_copy(bias_hbm, bias_ref)

        def scores(hd, masked):
            cs = slice(hd * A_DH, (hd + 1) * A_DH)
            return _attn_scores(q_ref[:, cs], k_refs, cs, bias_ref[hd], m, masked, scale * LOG2E)

        def step(masked):
            sc_next = scores(0, masked)
            for hd in range(A_HEADS):
                cs = slice(hd * A_DH, (hd + 1) * A_DH)
                sc = sc_next
                if hd + 1 < A_HEADS:
                    sc_next = scores(hd + 1, masked)
                mx = jnp.max(sc, axis=1, keepdims=True)
                p = jnp.exp2(sc - mx)
                l = jnp.sum(p, axis=1, keepdims=True)
                pb = p.astype(BF16)
                p_ref[hd] = pb
                o = _dot(pb[:, :QB], v_refs[0][:, cs], 1, 0)
                for t in range(1, NKB):
                    o += _dot(pb[:, t * QB:(t + 1) * QB], v_refs[t][:, cs], 1, 0)
                linv = 1.0 / l
                o_ref[:, cs] = (o * linv).astype(BF16)
                linv_ref[:, hd:hd + 1] = linv

        pl.when(m < LEAD)(functools.partial(step, True))
        pl.when(m >= LEAD)(functools.partial(step, False))

    def kv_spec(col, t):
        return pl.BlockSpec((QB, A_W), lambda m: (jnp.maximum(m + t - LEAD, 0), col))

    (r, linv, probs), _ = _hosted(
        body, None, name="attn_fwd",
        out_shape=(jax.ShapeDtypeStruct(r_in.shape, BF16), jax.ShapeDtypeStruct((s, A_HEADS), F32),
                   jax.ShapeDtypeStruct((nq, A_HEADS, QB, KB), BF16)),
        grid=(nq,),
        in_specs=[pl.BlockSpec((QB, A_W), lambda m: (m, 0))]
        + [kv_spec(1, t) for t in range(NKB)] + [kv_spec(2, t) for t in range(NKB)]
        + [pl.BlockSpec(memory_space=pl.ANY), pl.BlockSpec(memory_space=pl.ANY)],
        out_specs=(pl.BlockSpec((QB, A_W), lambda m: (m, 0)), pl.BlockSpec((QB, A_HEADS), lambda m: (m, 0)),
                   pl.BlockSpec((None, A_HEADS, QB, KB), lambda m: (m, 0, 0, 0))),
        scratch_shapes=[pltpu.VMEM((A_HEADS, QB, KB), F32)],
        input_output_aliases={2 * NKB + 2: 0},
        compiler_params=_params(("arbitrary",)),
        inputs=(h,) * (2 * NKB + 1) + (bias, r_in))
    return r, linv, probs


def _attn_bwd(h, r, dy, linv, probs, dh_in, comm=None):
    s = h.shape[0]
    nq = s // QB
    scale = A_DH ** -0.5

    def body(q_ref, *rest):
        k_refs, v_refs = rest[:NKB], rest[NKB:2 * NKB]
        (z_ref, r_ref, dy_ref, linv_ref, p_ref, zl_ref, rl_ref, dyl_ref, dh_any,
         dh_ref, dbias_hbm, dk_acc, dv_acc, dq_ring, dbias_ref) = rest[2 * NKB:]
        del dh_any
        m = pl.program_id(0)

        def slot(b):
            return pl.multiple_of(lax.rem(b + NKB, NKB) * QB, QB)

        @pl.when(m == 0)
        def _():
            dk_acc[...] = jnp.zeros_like(dk_acc)
            dv_acc[...] = jnp.zeros_like(dv_acc)
            dq_ring[...] = jnp.zeros_like(dq_ring)
            dbias_ref[...] = jnp.zeros_like(dbias_ref)

        @pl.when(m < nq)
        def _():
            z = z_ref[...].astype(F32)
            do_all = dy_ref[...].astype(F32) * (z * _sigmoid(z))
            o_all = r_ref[...].astype(F32)

            def product(hd):
                cs = slice(hd * A_DH, (hd + 1) * A_DH)
                dob = do_all[:, cs].astype(BF16)
                return jnp.concatenate([_dot(dob, v_refs[t][:, cs], 1, 1) for t in range(NKB)], axis=1)

            ahead = product(0)
            for hd in range(A_HEADS):
                cs = slice(hd * A_DH, (hd + 1) * A_DH)
                q = q_ref[:, cs]
                do = do_all[:, cs]
                dob = do.astype(BF16)
                delta = jnp.sum(do * o_all[:, cs], axis=1, keepdims=True)
                dp = ahead
                if hd + 1 < A_HEADS:
                    ahead = product(hd + 1)
                p = p_ref[hd].astype(F32) * linv_ref[:, hd:hd + 1]
                ds = p * (dp - delta)
                dbias_ref[hd] += ds
                pb, dsb = p.astype(BF16), ds.astype(BF16)
                dq = jnp.zeros((QB, A_DH), F32)
                for t in range(NKB):
                    ts = slice(t * QB, (t + 1) * QB)
                    rows = pl.ds(slot(m - LEAD + t), QB)
                    dq += _dot(dsb[:, ts], k_refs[t][:, cs], 1, 0)
                    dk_acc[rows, cs] += _dot(dsb[:, ts], q, 0, 0) * scale
                    dv_acc[rows, cs] += _dot(pb[:, ts], dob, 0, 0)
                dq_ring[pl.ds(slot(m), QB), cs] = dq * scale

        done = pl.ds(slot(m - LEAD), QB)
        zl = zl_ref[...].astype(F32)
        sg = _sigmoid(zl)
        dz = dyl_ref[...].astype(F32) * rl_ref[...].astype(F32) * (sg * (1.0 + zl * (1.0 - sg)))
        dh_ref[:, A_Q:A_Q + A_W] = dq_ring[done, :].astype(BF16)
        dh_ref[:, A_K:A_K + A_W] = dk_acc[done, :].astype(BF16)
        dh_ref[:, A_V:A_V + A_W] = dv_acc[done, :].astype(BF16)
        dh_ref[:, A_Z:A_Z + A_W] = dz.astype(BF16)
        dk_acc[done, :] = jnp.zeros((QB, A_W), F32)
        dv_acc[done, :] = jnp.zeros((QB, A_W), F32)

        @pl.when(m == nq + LEAD - 1)
        def _():
            pltpu.sync_copy(dbias_ref, dbias_hbm)

    last = nq - 1

    def cur(col):
        return pl.BlockSpec((QB, A_W), lambda m: (jnp.minimum(m, last), col))

    def kv_spec(col, t):
        return pl.BlockSpec((QB, A_W), lambda m: (jnp.clip(m + t - LEAD, 0, last), col))

    def lag(col):
        return pl.BlockSpec((QB, A_W), lambda m: (jnp.clip(m - LEAD, 0, last), col))

    (dh, dbias), c_out = _hosted(
        body, comm, name="attn_bwd",
        out_shape=(jax.ShapeDtypeStruct(dh_in.shape, BF16), jax.ShapeDtypeStruct((A_HEADS, QB, KB), F32)),
        grid=(nq + LEAD,),
        in_specs=[cur(0)] + [kv_spec(1, t) for t in range(NKB)] + [kv_spec(2, t) for t in range(NKB)]
        + [cur(3), cur(0), cur(0), pl.BlockSpec((QB, A_HEADS), lambda m: (jnp.minimum(m, last), 0)),
           pl.BlockSpec((None, A_HEADS, QB, KB), lambda m: (jnp.minimum(m, last), 0, 0, 0)),
           lag(3), lag(0), lag(0), pl.BlockSpec(memory_space=pl.ANY)],
        out_specs=(pl.BlockSpec((QB, 4 * A_W), lambda m: (jnp.clip(m - LEAD, 0, last), 0)),
                   pl.BlockSpec(memory_space=pl.ANY)),
        scratch_shapes=[pltpu.VMEM((KB, A_W), F32), pltpu.VMEM((KB, A_W), F32), pltpu.VMEM((KB, A_W), F32),
                        pltpu.VMEM((A_HEADS, QB, KB), F32)],
        input_output_aliases={2 * NKB + 9: 0},
        compiler_params=_params(("arbitrary",), 56),
        inputs=(h,) * (2 * NKB + 2) + (r, dy, linv, probs, h, r, dy, dh_in))
    return dh, dbias, c_out


GB = 256
N_PAIR = B_HEADS // 2


def _gla_gates(lr, gw_ref, gb_ref):
    logit = _dot(lr, gw_ref[...], 1, 0) + gb_ref[...]
    lg = (jnp.minimum(logit, 0.0) - jnp.log(1.0 + jnp.exp(-jnp.abs(logit)))) / GATE_TAU
    row = lax.broadcasted_iota(jnp.int32, (GB, GB), 0)
    col = lax.broadcasted_iota(jnp.int32, (GB, GB), 1)
    tri = jnp.where((row // CH == col // CH) & (col <= row), 1.0, 0.0).astype(F32)
    return logit, _dot(tri, lg, 1, 0, precision=lax.Precision.HIGHEST)


def _gla_factors(hb_ref, b_all, c):
    rs = slice(c * CH, (c + 1) * CH)
    q = hb_ref[rs, 0:B_KW].astype(F32) * (B_DK ** -0.5)
    k = hb_ref[rs, B_KW:2 * B_KW].astype(F32)
    b = b_all[rs]
    bm, bl = b[CH // 2:CH // 2 + 1, :], b[CH - 1:CH, :]
    e1, e2, eb, ek = jnp.exp(b - bm), jnp.exp(bm - b), jnp.exp(b), jnp.exp(bl - b)
    el = jnp.exp(bl)
    return dict(ql=q * e1, kl=k * e2, qu=q * e2, ku=k * e1, qt=q * eb, kh=k * ek, e1=e1, e2=e2, eb=eb, ek=ek, el=el)


class _GlaPairTools:
    def __init__(self, hb_ref, fs):
        self.hb_ref, self.fs, self.made = hb_ref, fs, {}
        row = lax.broadcasted_iota(jnp.int32, (2 * CH, 2 * CH), 0)
        col = lax.broadcasted_iota(jnp.int32, (2 * CH, 2 * CH), 1)
        self.same = (row // CH) == (col // CH)
        self.lower = self.same & ((row % CH) >= (col % CH))
        self.upper = self.same & ((row % CH) < (col % CH))
        self.lower_t = self.same & ((col % CH) >= (row % CH))
        self.upper_t = self.same & ((col % CH) < (row % CH))

    def _once(self, key, make):
        if key not in self.made:
            self.made[key] = make()
        return self.made[key]

    def lanes(self, c, p, name):
        return self.fs[c][name][:, p * 128:(p + 1) * 128]

    def heads(self, c, p, name):
        def make():
            x = self.lanes(c, p, name)
            return jnp.where(self.same, jnp.concatenate([x, x], axis=0), 0.0).astype(BF16)
        return self._once(("heads", c, p, name), make)

    def twice(self, c, p, name):
        def make():
            x = self.lanes(c, p, name).astype(BF16)
            return jnp.concatenate([x, x], axis=0)
        return self._once(("twice", c, p, name), make)

    def vals(self, c, p):
        rows = slice(c * CH, (c + 1) * CH)
        return self._once(("vals", c, p), lambda: jnp.concatenate(
            [self.hb_ref[rows, 512 + (2 * p + sh) * B_DV:512 + (2 * p + sh + 1) * B_DV] for sh in range(2)], axis=0))

    def intra(self, c, p):
        lo = _dot(self.heads(c, p, "ql"), self.twice(c, p, "kl"), 1, 1)
        up = _dot(self.heads(c, p, "qu"), self.twice(c, p, "ku"), 1, 1)
        return jnp.where(self.lower, lo, jnp.where(self.upper, up, 0.0)).astype(BF16)

    def fold(self, x):
        x = jnp.where(self.same, x, 0.0)
        return x[:CH] + x[CH:]


def _gla_fwd(h, r_in, gw, gb, ng):
    s = h.shape[0]
    nb = s // GB
    cpb = GB // CH

    def body(hb_ref, gw_ref, gb_ref, ng_ref, r_any, o_ref, opre_ref, st_ref, state):
        del r_any

        @pl.when(pl.program_id(0) == 0)
        def _():
            state[...] = jnp.zeros_like(state)

        _, b_all = _gla_gates(hb_ref[:, 1536:1664], gw_ref, gb_ref)
        fs = [_gla_factors(hb_ref, b_all, c) for c in range(cpb)]
        pairs = [(c, p) for c in range(cpb) for p in range(N_PAIR)]
        tools = _GlaPairTools(hb_ref, fs)
        a = {k: tools.intra(*k) for k in pairs}
        o_intra = {k: _dot(a[k], tools.vals(*k), 1, 0) for k in pairs}
        upd = {k: _dot(tools.vals(*k), tools.heads(*k, "kh"), 0, 0) for k in pairs}
        st = [state[p] for p in range(N_PAIR)]
        entering = {}
        for c, p in pairs:
            entering[c, p] = st[p]
            st_ref[c, p] = st[p]
            st[p] = st[p] * fs[c]["el"][:, p * 128:(p + 1) * 128] + upd[c, p]
        for p in range(N_PAIR):
            state[p] = st[p]
        for c, p in pairs:
            o2 = o_intra[c, p] + _dot(tools.heads(c, p, "qt"), entering[c, p].astype(BF16), 1, 1)
            for sh in range(2):
                o = o2[sh * CH:(sh + 1) * CH]
                rs, hs = slice(c * CH, (c + 1) * CH), slice((2 * p + sh) * B_DV, (2 * p + sh + 1) * B_DV)
                opre_ref[rs, hs] = o
                rinv = lax.rsqrt(jnp.mean(o * o, axis=1, keepdims=True) + RMS_EPS)
                o_ref[rs, hs] = (o * rinv * ng_ref[...]).astype(BF16)

    (r, opre, states), _ = _hosted(
        body, None, name="gla_fwd",
        out_shape=(jax.ShapeDtypeStruct(r_in.shape, BF16), jax.ShapeDtypeStruct((s, B_W), F32),
                   jax.ShapeDtypeStruct((s // CH, N_PAIR, 128, 128), F32)),
        grid=(nb,),
        in_specs=[pl.BlockSpec((GB, 2048), lambda i: (i, B_BASE // 2048)),
                  pl.BlockSpec((128, B_KW), lambda i: (0, 0)), pl.BlockSpec((1, B_KW), lambda i: (0, 0)),
                  pl.BlockSpec((1, B_DV), lambda i: (0, 0)), pl.BlockSpec(memory_space=pl.ANY)],
        out_specs=(pl.BlockSpec((GB, B_W), lambda i: (i, 1024 // B_W)), pl.BlockSpec((GB, B_W), lambda i: (i, 0)),
                   pl.BlockSpec((cpb, N_PAIR, 128, 128), lambda i: (i, 0, 0, 0))),
        scratch_shapes=[pltpu.VMEM((N_PAIR, 128, 128), F32)],
        input_output_aliases={4: 0},
        compiler_params=_params(("arbitrary",)),
        inputs=(h, gw, gb, ng, r_in))
    return r, opre, states


def _gla_bwd(h, dy, opre, states, gw, gb, ng, dh_in):
    s = h.shape[0]
    nb = s // GB
    cpb = GB // CH

    def body(hb_ref, dy_ref, opre_ref, st_ref, gw_ref, gb_ref, ng_ref, dh_any,
             dh_ref, dgw_ref, dgb_ref, dng_ref, dstate, db_scr, do_scr):
        del dh_any

        @pl.when(pl.program_id(0) == 0)
        def _():
            dstate[...] = jnp.zeros_like(dstate)
            dgw_ref[...] = jnp.zeros_like(dgw_ref)
            dgb_ref[...] = jnp.zeros_like(dgb_ref)
            dng_ref[...] = jnp.zeros_like(dng_ref)

        lr = hb_ref[:, 1536:1664]
        logit, b_all = _gla_gates(lr, gw_ref, gb_ref)
        z = hb_ref[:, 1024:1536].astype(F32)
        sg = _sigmoid(z)
        dyb = dy_ref[...].astype(F32)
        dng = jnp.zeros((1, B_DV), F32)
        for hd in range(B_HEADS):
            hs = slice(hd * B_DV, (hd + 1) * B_DV)
            o = opre_ref[:, hs]
            rinv = lax.rsqrt(jnp.mean(o * o, axis=1, keepdims=True) + RMS_EPS)
            on = o * rinv
            dr = dyb[:, hs] * (z[:, hs] * sg[:, hs])
            dh_ref[:, 1024 + hd * B_DV:1024 + (hd + 1) * B_DV] = (
                dyb[:, hs] * (on * ng_ref[...]) * (sg[:, hs] * (1.0 + z[:, hs] * (1.0 - sg[:, hs])))).astype(BF16)
            dng += jnp.sum(dr * on, axis=0, keepdims=True)
            dn = dr * ng_ref[...]
            do_scr[:, hs] = rinv * (dn - on * jnp.mean(dn * on, axis=1, keepdims=True))
        dng_ref[...] += dng

        rowi = lax.broadcasted_iota(jnp.int32, (CH, 128), 0)
        fs = [_gla_factors(hb_ref, b_all, c) for c in range(cpb)]
        pairs = [(c, p) for c in reversed(range(cpb)) for p in range(N_PAIR)]
        tools = _GlaPairTools(hb_ref, fs)
        pair = tools.lanes

        def douts(c, p):
            rows = slice(c * CH, (c + 1) * CH)
            return tools._once(("douts", c, p), lambda: jnp.concatenate(
                [do_scr[rows, (2 * p + sh) * B_DV:(2 * p + sh + 1) * B_DV] for sh in range(2)], axis=0).astype(BF16))

        at, da, dat, dst_own, g_qt = {}, {}, {}, {}, {}
        for k in pairs:
            c, p = k
            lo_t = _dot(tools.twice(*k, "kl"), tools.heads(*k, "ql"), 1, 1)
            up_t = _dot(tools.twice(*k, "ku"), tools.heads(*k, "qu"), 1, 1)
            at[k] = jnp.where(tools.lower_t, lo_t, jnp.where(tools.upper_t, up_t, 0.0)).astype(BF16)
            da[k] = _dot(douts(*k), tools.vals(*k), 1, 1)
            dat[k] = _dot(tools.vals(*k), douts(*k), 1, 1)
            dst_own[k] = _dot(douts(*k), tools.heads(*k, "qt"), 0, 0)
            g_qt[k] = tools.fold(_dot(douts(*k), st_ref[c, p].astype(BF16), 1, 0))
        dv, g_ql, g_qu, g_kl, g_ku = {}, {}, {}, {}, {}
        for k in pairs:
            dv[k] = _dot(at[k], douts(*k), 1, 0)
            g_ql[k] = tools.fold(_dot(jnp.where(tools.lower, da[k], 0.0).astype(BF16), tools.twice(*k, "kl"), 1, 0))
            g_qu[k] = tools.fold(_dot(jnp.where(tools.upper, da[k], 0.0).astype(BF16), tools.twice(*k, "ku"), 1, 0))
            g_kl[k] = tools.fold(_dot(jnp.where(tools.lower_t, dat[k], 0.0).astype(BF16), tools.heads(*k, "ql"), 1, 0))
            g_ku[k] = tools.fold(_dot(jnp.where(tools.upper_t, dat[k], 0.0).astype(BF16), tools.heads(*k, "qu"), 1, 0))
        dst = [dstate[p] for p in range(N_PAIR)]
        leaving = {}
        for c, p in pairs:
            leaving[c, p] = dst[p]
            dst[p] = dst[p] * pair(c, p, "el") + dst_own[c, p]
        for p in range(N_PAIR):
            dstate[p] = dst[p]
        g_kh = {}
        for k in pairs:
            c, p = k
            dstb = leaving[k].astype(BF16)
            dv2 = dv[k] + _dot(tools.heads(*k, "kh"), dstb, 1, 1)
            for sh in range(2):
                hd = 2 * p + sh
                dh_ref[c * CH:(c + 1) * CH, 512 + hd * B_DV:512 + (hd + 1) * B_DV] = dv2[sh * CH:(sh + 1) * CH].astype(BF16)
            g_kh[k] = tools.fold(_dot(tools.vals(*k), dstb, 1, 0))
        for c in reversed(range(cpb)):
            rs = slice(c * CH, (c + 1) * CH)
            for p in range(N_PAIR):
                k = (c, p)
                dq = (g_ql[k] * pair(c, p, "e1") + g_qu[k] * pair(c, p, "e2") + g_qt[k] * pair(c, p, "eb")) * (B_DK ** -0.5)
                dk = g_kl[k] * pair(c, p, "e2") + g_ku[k] * pair(c, p, "e1") + g_kh[k] * pair(c, p, "ek")
                dkh_kh = g_kh[k] * pair(c, p, "kh")
                db = (g_ql[k] * pair(c, p, "ql") - g_qu[k] * pair(c, p, "qu") + g_qt[k] * pair(c, p, "qt")
                      - g_kl[k] * pair(c, p, "kl") + g_ku[k] * pair(c, p, "ku") - dkh_kh)
                db_last = (pair(c, p, "el") * jnp.sum(leaving[c, p] * st_ref[c, p], axis=0, keepdims=True)
                           + jnp.sum(dkh_kh, axis=0, keepdims=True))
                db = jnp.where(rowi == CH - 1, db + db_last, db)
                dh_ref[rs, p * 128:(p + 1) * 128] = dq.astype(BF16)
                dh_ref[rs, B_KW + p * 128:B_KW + (p + 1) * 128] = dk.astype(BF16)
                db_scr[rs, p * 128:(p + 1) * 128] = db

        row = lax.broadcasted_iota(jnp.int32, (GB, GB), 0)
        col = lax.broadcasted_iota(jnp.int32, (GB, GB), 1)
        trit = jnp.where((row // CH == col // CH) & (col >= row), 1.0, 0.0).astype(F32)
        dlg = _dot(trit, db_scr[...], 1, 0, precision=lax.Precision.HIGHEST)
        dlogit = dlg * (_sigmoid(-logit) / GATE_TAU)
        dlb = dlogit.astype(BF16)
        dgw_ref[...] += _dot(lr, dlb, 0, 0)
        dgb_ref[...] += jnp.sum(dlogit, axis=0, keepdims=True)
        dh_ref[:, 1536:1664] = _dot(dlb, gw_ref[...], 1, 1).astype(BF16)
        dh_ref[:, 1664:2048] = jnp.zeros((GB, 384), BF16)

    rev = lambda i: nb - 1 - i
    (dh, dgw, dgb, dng), _ = _hosted(
        body, None, name="gla_bwd",
        out_shape=(jax.ShapeDtypeStruct(dh_in.shape, BF16), jax.ShapeDtypeStruct((128, B_KW), F32),
                   jax.ShapeDtypeStruct((1, B_KW), F32), jax.ShapeDtypeStruct((1, B_DV), F32)),
        grid=(nb,),
        in_specs=[pl.BlockSpec((GB, 2048), lambda i: (rev(i), B_BASE // 2048)),
                  pl.BlockSpec((GB, B_W), lambda i: (rev(i), 1024 // B_W)),
                  pl.BlockSpec((GB, B_W), lambda i: (rev(i), 0)),
                  pl.BlockSpec((cpb, N_PAIR, 128, 128), lambda i: (rev(i), 0, 0, 0)),
                  pl.BlockSpec((128, B_KW), lambda i: (0, 0)), pl.BlockSpec((1, B_KW), lambda i: (0, 0)),
                  pl.BlockSpec((1, B_DV), lambda i: (0, 0)), pl.BlockSpec(memory_space=pl.ANY)],
        out_specs=(pl.BlockSpec((GB, 2048), lambda i: (rev(i), B_BASE // 2048)),
                   pl.BlockSpec((128, B_KW), lambda i: (0, 0)), pl.BlockSpec((1, B_KW), lambda i: (0, 0)),
                   pl.BlockSpec((1, B_DV), lambda i: (0, 0))),
        scratch_shapes=[pltpu.VMEM((N_PAIR, 128, 128), F32), pltpu.VMEM((GB, B_KW), F32), pltpu.VMEM((GB, B_W), F32)],
        input_output_aliases={7: 0},
        compiler_params=_params(("arbitrary",)),
        inputs=(h, dy, opre, states, gw, gb, ng, dh_in))
    return dh, dgw, dgb, dng


MB = 512


def _mem_probs(q, mk, scale):
    sc = _dot(q, mk, 1, 1) * (scale * LOG2E)
    p = jnp.exp2(sc - jnp.max(sc, axis=1, keepdims=True))
    return p / jnp.sum(p, axis=1, keepdims=True)


def _mem_fwd(h, r_in, mkv):
    s = h.shape[0]
    scale = M_DH ** -0.5

    def body(q_ref, mkv_ref, r_any, o_ref):
        del r_any
        for hd in range(M_HEADS):
            cs = slice(hd * M_DH, (hd + 1) * M_DH)
            p = _mem_probs(q_ref[:, cs], mkv_ref[:, cs], scale)
            o_ref[:, cs] = _dot(p.astype(BF16), mkv_ref[:, M_W + hd * M_DH:M_W + (hd + 1) * M_DH], 1, 0).astype(BF16)

    return pl.pallas_call(
        body, name="mem_fwd",
        out_shape=jax.ShapeDtypeStruct(r_in.shape, BF16),
        grid=(s // MB,),
        in_specs=[pl.BlockSpec((MB, M_W), lambda i: (i, M_BASE // M_W)),
                  pl.BlockSpec((N_MEM, 2 * M_W), lambda i: (0, 0)), pl.BlockSpec(memory_space=pl.ANY)],
        out_specs=pl.BlockSpec((MB, M_W), lambda i: (i, 1536 // M_W)),
        input_output_aliases={2: 0},
        compiler_params=_params(("arbitrary",)),
    )(h, mkv, r_in)


def _mem_bwd(h, r, dy, mkv, dh_in):
    s = h.shape[0]
    scale = M_DH ** -0.5

    def body(q_ref, z_ref, r_ref, dy_ref, mkv_ref, dh_any, dh_ref, dmkv_ref):
        del dh_any

        @pl.when(pl.program_id(0) == 0)
        def _():
            dmkv_ref[...] = jnp.zeros_like(dmkv_ref)

        z = z_ref[...].astype(F32)
        sg = _sigmoid(z)
        dyv = dy_ref[...].astype(F32)
        do_all = dyv * (z * sg)
        dh_ref[:, M_W:2 * M_W] = (dyv * r_ref[...].astype(F32) * (sg * (1.0 + z * (1.0 - sg)))).astype(BF16)
        for hd in range(M_HEADS):
            cs = slice(hd * M_DH, (hd + 1) * M_DH)
            vs = slice(M_W + hd * M_DH, M_W + (hd + 1) * M_DH)
            q = q_ref[:, cs]
            p = _mem_probs(q, mkv_ref[:, cs], scale)
            dob = do_all[:, cs].astype(BF16)
            dp = _dot(dob, mkv_ref[:, vs], 1, 1)
            ds = p * (dp - jnp.sum(p * dp, axis=1, keepdims=True))
            dsb = ds.astype(BF16)
            dh_ref[:, cs] = (_dot(dsb, mkv_ref[:, cs], 1, 0) * scale).astype(BF16)
            dmkv_ref[:, cs] += _dot(dsb, q, 0, 0) * scale
            dmkv_ref[:, vs] += _dot(p.astype(BF16), dob, 0, 0)

    return pl.pallas_call(
        body, name="mem_bwd",
        out_shape=(jax.ShapeDtypeStruct(dh_in.shape, BF16), jax.ShapeDtypeStruct((N_MEM, 2 * M_W), F32)),
        grid=(s // MB,),
        in_specs=[pl.BlockSpec((MB, M_W), lambda i: (i, M_BASE // M_W)),
                  pl.BlockSpec((MB, M_W), lambda i: (i, M_BASE // M_W + 1)),
                  pl.BlockSpec((MB, M_W), lambda i: (i, 1536 // M_W)),
                  pl.BlockSpec((MB, M_W), lambda i: (i, 1536 // M_W)),
                  pl.BlockSpec((N_MEM, 2 * M_W), lambda i: (0, 0)), pl.BlockSpec(memory_space=pl.ANY)],
        out_specs=(pl.BlockSpec((MB, 2 * M_W), lambda i: (i, M_BASE // (2 * M_W))),
                   pl.BlockSpec((N_MEM, 2 * M_W), lambda i: (0, 0))),
        input_output_aliases={5: 0},
        compiler_params=_params(("arbitrary",)),
    )(h, h, r, dy, mkv, dh_in)


OB = 256


def _outproj_ln(h, r, w_out, x, ln_g, ln_b, target=None):
    s = h.shape[0]
    last = target is not None

    def body(za_ref, zb_ref, zm_ref, r_ref, w_ref, x_ref, g_ref, b_ref, *rest):
        if last:
            t_ref, xn_ref, xh_ref, rstd_ref, y_ref, l_ref = rest
        else:
            xn_ref, xh_ref, rstd_ref, y_ref = rest
        z = jnp.concatenate([za_ref[...], zb_ref[...], zm_ref[...]], axis=1).astype(F32)
        y = (r_ref[...].astype(F32) * (z * _sigmoid(z))).astype(BF16)
        y_ref[...] = y
        u = ALPHA * x_ref[...] + _dot(y, w_ref[...], 1, 0)
        mu = jnp.mean(u, axis=1, keepdims=True)
        uc = u - mu
        rstd = lax.rsqrt(jnp.mean(uc * uc, axis=1, keepdims=True) + LN_EPS)
        xh = uc * rstd
        xh_ref[...] = xh
        rstd_ref[...] = rstd
        xn = xh * g_ref[...] + b_ref[...]
        if last:
            @pl.when(pl.program_id(0) == 0)
            def _():
                l_ref[...] = jnp.zeros_like(l_ref)

            e = xn - t_ref[...]
            xn_ref[...] = e / D
            l_ref[...] += 0.5 * jnp.sum(jnp.mean(e * e, axis=1, keepdims=True))
        else:
            xn_ref[...] = xn

    row = lambda w, c: pl.BlockSpec((OB, w), lambda i: (i, c))
    vec = pl.BlockSpec((1, D), lambda i: (0, 0))
    full = jax.ShapeDtypeStruct((s, D), F32)
    return pl.pallas_call(
        body, name="outproj_ln_loss" if last else "outproj_ln",
        out_shape=(full, full, jax.ShapeDtypeStruct((s, 1), F32), jax.ShapeDtypeStruct((s, D), BF16))
        + ((jax.ShapeDtypeStruct((1, 128), F32),) if last else ()),
        grid=(s // OB,),
        in_specs=[row(A_W, A_Z // A_W), row(B_W, (B_BASE + 1024) // B_W), row(M_W, (M_BASE + M_W) // M_W), row(D, 0),
                  pl.BlockSpec((D, D), lambda i: (0, 0)), row(D, 0), vec, vec] + ([row(D, 0)] if last else []),
        out_specs=(row(D, 0), row(D, 0), pl.BlockSpec((OB, 1), lambda i: (i, 0)), row(D, 0))
        + ((pl.BlockSpec((1, 128), lambda i: (0, 0)),) if last else ()),
        compiler_params=_params(("arbitrary",), 56),
    )(h, h, h, r, w_out, x, ln_g, ln_b, *([target] if last else []))


def _ln_bwd_dy(g, xh, rstd, ln_g, w_out, y):
    s = g.shape[0]
    n = s // OB

    def body(g_ref, xh_ref, rstd_ref, lg_ref, w_ref, y_ref, dy_ref, du_ref, dg_ref, db_ref, dw_hbm, dw_acc, stage):
        i = pl.program_id(0)

        @pl.when(i == 0)
        def _():
            dg_ref[...] = jnp.zeros_like(dg_ref)
            db_ref[...] = jnp.zeros_like(db_ref)
            dw_acc[...] = jnp.zeros_like(dw_acc)

        gv, xh = g_ref[...], xh_ref[...]
        dg_ref[...] += jnp.sum(gv * xh, axis=0, keepdims=True)
        db_ref[...] += jnp.sum(gv, axis=0, keepdims=True)
        dxh = gv * lg_ref[...]
        du = rstd_ref[...] * (dxh - jnp.mean(dxh, axis=1, keepdims=True) - xh * jnp.mean(dxh * xh, axis=1, keepdims=True))
        du_ref[...] = du
        dub = du.astype(BF16)
        dy_ref[...] = _dot(dub, w_ref[...], 1, 1).astype(BF16)
        dw_acc[...] += _dot(y_ref[...], dub, 0, 0)

        @pl.when(i == n - 1)
        def _():
            for c in range(D // OB):
                stage[...] = dw_acc[c * OB:(c + 1) * OB, :].astype(BF16)
                pltpu.sync_copy(stage, dw_hbm.at[c * OB:(c + 1) * OB, :])

    row = pl.BlockSpec((OB, D), lambda i: (i, 0))
    vec = pl.BlockSpec((1, D), lambda i: (0, 0))
    return pl.pallas_call(
        body, name="ln_bwd_dy",
        out_shape=(jax.ShapeDtypeStruct((s, D), BF16), jax.ShapeDtypeStruct((s, D), F32),
                   jax.ShapeDtypeStruct((1, D), F32), jax.ShapeDtypeStruct((1, D), F32), jax.ShapeDtypeStruct((D, D), BF16)),
        grid=(n,),
        in_specs=[row, row, pl.BlockSpec((OB, 1), lambda i: (i, 0)), vec, pl.BlockSpec((D, D), lambda i: (0, 0)), row],
        out_specs=(row, row, vec, vec, pl.BlockSpec(memory_space=pl.ANY)),
        scratch_shapes=[pltpu.VMEM((D, D), F32), pltpu.VMEM((OB, D), BF16)],
        compiler_params=_params(("arbitrary",), 60),
    )(g, xh, rstd, ln_g, w_out, y)


class _LocalWeights:
    def __init__(self, w_in_p, w_out_f, w_kv_f):
        self.w = list(zip(w_in_p, w_out_f, w_kv_f))
        self.depth = len(self.w)
        self.grads = [dict() for _ in self.w]

    def weights(self, l):
        return self.w[l]

    def host(self, where, l, payload=None):
        if payload is not None:
            self.grads[l][where] = payload
        return None

    def landed(self, where, l, outs):
        pass


def _shard_pieces():
    sh = IN_W // N_DEV
    pieces = []
    for j in range(N_DEV):
        lo, hi = j * sh, (j + 1) * sh
        cuts = [lo, NAT_SPLIT, hi] if lo < NAT_SPLIT < hi else [lo, hi]
        for a, b in zip(cuts[:-1], cuts[1:]):
            pieces.append((j, a - lo, a if a < NAT_SPLIT else a + H_PAD, b - a))
    return pieces


RB = 256


def _shards_to_padded(raw):
    def body(x_ref, o_ref):
        for j, src, dst, width in _shard_pieces():
            o_ref[:, dst:dst + width] = x_ref[j, :, src:src + width]
        o_ref[:, NAT_SPLIT:NAT_SPLIT + H_PAD] = jnp.zeros((RB, H_PAD), o_ref.dtype)

    return pl.pallas_call(
        body, name="place_w_in", out_shape=jax.ShapeDtypeStruct((D, H_W), raw.dtype), grid=(D // RB,),
        in_specs=[pl.BlockSpec((N_DEV, RB, raw.shape[2]), lambda i: (0, i, 0))],
        out_specs=pl.BlockSpec((RB, H_W), lambda i: (i, 0)),
        compiler_params=_params(("parallel",)),
    )(raw)


def _padded_to_shards(w):
    sh = IN_W // N_DEV

    def body(x_ref, o_ref):
        for j, src, dst, width in _shard_pieces():
            o_ref[j, :, src:src + width] = x_ref[:, dst:dst + width]

    return pl.pallas_call(
        body, name="split_d_w_in", out_shape=jax.ShapeDtypeStruct((N_DEV, D, sh), w.dtype), grid=(D // RB,),
        in_specs=[pl.BlockSpec((RB, H_W), lambda i: (i, 0))],
        out_specs=pl.BlockSpec((N_DEV, RB, sh), lambda i: (0, i, 0)),
        compiler_params=_params(("parallel",)),
    )(w)


_PART_A = (0, 1, 2, 4, 7)
_PART_B = (3, 5, 6)


class _Fsdp:
    def __init__(self, w_in, w_out, w_kv, extra):
        self.sh = (w_in, w_out, w_kv)
        self.depth = w_in.shape[0]
        self.raw = [dict() for _ in range(self.depth)]
        self.recv = [dict() for _ in range(self.depth)]
        g_in, g_out, g_kv, *self.extra = _gather_now([w_in[0], w_out[0], w_kv[0]] + list(extra), "gather_layer0")
        self.raw[0] = dict(w_in=g_in, w_out=g_out, w_kv=g_kv)

    def weights(self, l):
        raw = self.raw[l]
        return _shards_to_padded(raw["w_in"]), raw["w_out"].reshape(D, D), raw["w_kv"].reshape(D, 2 * M_W)

    def host(self, where, l, payload=None):
        w_in, w_out, w_kv = self.sh
        if where == "in_proj" and l + 1 < self.depth:
            return _Gather([w_in[l + 1], w_out[l + 1], w_kv[l + 1]])
        if where == "d_w_in":
            d_wout, d_wkv = payload
            return _Scatter([d_wout.reshape(N_DEV, D // N_DEV, D), d_wkv.reshape(N_DEV, D // N_DEV, 2 * M_W)])
        if where == "d_x":
            self.blocks = _padded_to_shards(payload)
            return _Scatter([self.blocks], relations=_PART_A if l > 0 else tuple(range(N_DEV)))
        if where == "attn_bwd" and l + 1 < self.depth:
            return _Scatter([self.blocks], relations=_PART_B, into=[self.recv[l + 1]["w_in"]])
        return None

    def landed(self, where, l, outs):
        if where == "in_proj" and outs:
            self.raw[l + 1] = dict(zip(("w_in", "w_out", "w_kv"), outs))
        elif where == "d_w_in":
            self.recv[l]["w_out"], self.recv[l]["w_kv"] = outs
        elif where == "d_x":
            self.recv[l]["w_in"] = outs[0]
        elif where == "attn_bwd" and outs:
            self.recv[l + 1]["w_in"] = outs[0]


def _local_step(x, mem, target, pipe, rel, gate_w, gate_b, norm_g, ln_g, ln_b):
    depth = pipe.depth
    s = x.shape[0]
    saved = []
    xl = x
    for l in range(depth):
        w_in_p, w_out_f, w_kv_f = pipe.weights(l)
        hmat, landed = _mm(xl, w_in_p, out_dtype=BF16, tm=1024, tn=1024, tk=D, name="in_proj",
                           comm=pipe.host("in_proj", l), dead_n=H_DEAD)
        pipe.landed("in_proj", l, landed)
        mkv, _ = _mm(mem, w_kv_f, out_dtype=BF16, tm=N_MEM, tn=1024, tk=D, name="mem_kv")
        bias = _band_bias(rel[l])
        gw = jnp.zeros((128, B_KW), F32).at[:GATE_RANK].set(gate_w[l]).astype(BF16)
        gb, ng = gate_b[l][None, :], norm_g[l][None, :]
        r, linv, probs = _attn_fwd(hmat, lax.empty((s, D), BF16), bias)
        r, opre, states = _gla_fwd(hmat, r, gw, gb, ng)
        r = _mem_fwd(hmat, r, mkv)
        xn, xh, rstd, y, *loss = _outproj_ln(hmat, r, w_out_f, xl, ln_g[l][None, :], ln_b[l][None, :],
                                             target if l == depth - 1 else None)
        saved.append(dict(x=xl, h=hmat, mkv=mkv, gw=gw, gb=gb, ng=ng, r=r, linv=linv, probs=probs, opre=opre,
                          states=states, xh=xh, rstd=rstd, y=y, w_in_p=w_in_p, w_out_f=w_out_f))
        xl = xn
    (loss,), g = loss, xl

    grads = [None] * depth
    for l in reversed(range(depth)):
        sv = saved[l]
        dy, du, d_lng, d_lnb, d_wout = _ln_bwd_dy(g, sv["xh"], sv["rstd"], ln_g[l][None, :], sv["w_out_f"], sv["y"])
        dh, dbias, landed = _attn_bwd(sv["h"], sv["r"], dy, sv["linv"], sv["probs"], lax.empty((s, H_W), BF16),
                                      pipe.host("attn_bwd", l))
        pipe.landed("attn_bwd", l, landed)
        dh, d_gw, d_gb, d_ng = _gla_bwd(sv["h"], dy, sv["opre"], sv["states"], sv["gw"], sv["gb"], sv["ng"], dh)
        dh, d_mkv = _mem_bwd(sv["h"], sv["r"], dy, sv["mkv"], dh)
        d_wkv, _ = _mm(mem, d_mkv, ta=True, out_dtype=BF16, tm=1024, tn=1024, tk=N_MEM, name="d_w_kv")
        d_win, landed = _mm(sv["x"], dh, ta=True, out_dtype=BF16, tm=1024, tn=1792, tk=1024, name="d_w_in",
                            comm=pipe.host("d_w_in", l, (d_wout, d_wkv)), dead_n=H_DEAD)
        pipe.landed("d_w_in", l, landed)
        g, landed = _mm(dh, sv["w_in_p"], tb=True, out_dtype=F32, tm=1024, tn=1024, tk=1792, name="d_x",
                        adds=((du, ALPHA),), comm=pipe.host("d_x", l, d_win), dead_k=H_DEAD)
        pipe.landed("d_x", l, landed)
        grads[l] = dict(rel=_bias_grad(dbias), gate_w=d_gw[:GATE_RANK], gate_b=d_gb[0], norm_g=d_ng[0],
                        ln_g=d_lng[0], ln_b=d_lnb[0])
    return loss, g, grads


def _adamw(parts, w, m, v, rows_per_step, name):
    depth, rows, cols = w.shape
    n = parts[0].shape[0]
    tr = min(rows_per_step, rows)
    assert rows % tr == 0 and len(parts) == depth

    def body(*refs):
        p_refs = refs[:depth]
        w_ref, m_ref, v_ref, g_ref, d_ref, nm_ref, nv_ref = refs[depth:]
        for l in range(depth):
            @pl.when(pl.program_id(0) == l)
            def _(p_ref=p_refs[l]):
                g = p_ref[0].astype(F32)
                for j in range(1, n):
                    g = g + p_ref[j].astype(F32)
                nm = ADAM_B1 * m_ref[...] + (1.0 - ADAM_B1) * g
                nv = ADAM_B2 * v_ref[...] + (1.0 - ADAM_B2) * (g * g)
                m_hat = nm / (1.0 - ADAM_B1 ** ADAM_STEP)
                v_hat = nv / (1.0 - ADAM_B2 ** ADAM_STEP)
                g_ref[...] = g
                nm_ref[...] = nm
                nv_ref[...] = nv
                d_ref[...] = -ADAM_LR * (m_hat / (jnp.sqrt(v_hat) + ADAM_EPS) + ADAM_WD * w_ref[...])

    def part_spec(l):
        return pl.BlockSpec((n, tr, cols), lambda ll, i: (0, jnp.where(ll == l, i, 0), 0))

    blk = pl.BlockSpec((None, tr, cols), lambda ll, i: (ll, i, 0))
    shape = jax.ShapeDtypeStruct((depth, rows, cols), F32)
    return pl.pallas_call(
        body, name=name,
        out_shape=(shape, shape, shape, shape),
        grid=(depth, rows // tr),
        in_specs=[part_spec(l) for l in range(depth)] + [blk, blk, blk],
        out_specs=(blk, blk, blk, blk),
        compiler_params=_params(("arbitrary", "arbitrary")),
    )(*parts, w, m, v)


SMALL = (("rel", A_HEADS * (2 * MAX_REL + 1)), ("gate_w", GATE_RANK * B_KW), ("gate_b", B_KW), ("norm_g", B_DV),
         ("ln_g", D), ("ln_b", D))


def _pack_small(parts, depth):
    rows = []
    for name, size in SMALL:
        flat = parts[name].reshape(depth * size).astype(F32)
        rows.append(jnp.pad(flat, (0, -(depth * size) % 128)).reshape(-1, 128))
    packed = jnp.concatenate(rows, axis=0)
    return jnp.pad(packed, ((0, -packed.shape[0] % 8), (0, 0)))


def _unpack_small(packed, depth, shapes):
    out, row = {}, 0
    for name, size in SMALL:
        nrow = -(-(depth * size) // 128)
        out[name] = packed[row:row + nrow].reshape(-1)[:depth * size].reshape(shapes[name])
        row += nrow
    return out


def kernel(x, mem, w_in, a_rel_bias, b_gate_w, b_gate_b, b_norm_g, w_mem_kv, w_out, ln_g, ln_b, loss_target, m_w_in, m_a_rel_bias, m_b_gate_w, m_b_gate_b, m_b_norm_g, m_w_mem_kv, m_w_out, m_ln_g, m_ln_b, v_w_in, v_a_rel_bias, v_b_gate_w, v_b_gate_b, v_b_norm_g, v_w_mem_kv, v_w_out, v_ln_g, v_ln_b):
    depth = w_in.shape[0]
    sh_in = w_in.shape[2]
    sh_gw = b_gate_w.shape[2]
    me = 4 * lax.axis_index("x") + 2 * lax.axis_index("y") + lax.axis_index("c")

    pipe = _Fsdp(w_in.astype(BF16), w_out.astype(BF16), w_mem_kv.astype(BF16), [b_gate_w])
    gate_w_full = jnp.transpose(pipe.extra[0], (1, 2, 0, 3)).reshape(depth, GATE_RANK, N_DEV * sh_gw)
    loss_dev, dx, grads = _local_step(x[0], mem[0], loss_target[0], pipe,
                                      a_rel_bias, gate_w_full, b_gate_b, b_norm_g, ln_g, ln_b)
    loss = lax.psum(loss_dev[0, 0], ("x", "y", "c"))

    recv = lambda n: [pipe.recv[l][n] for l in range(depth)]
    big = {"w_in": _adamw(recv("w_in"), w_in, m_w_in, v_w_in, 128, "adamw_w_in"),
           "w_out": _adamw(recv("w_out"), w_out, m_w_out, v_w_out, 64, "adamw_w_out"),
           "w_kv": _adamw(recv("w_kv"), w_mem_kv, m_w_mem_kv, v_w_mem_kv, 128, "adamw_w_kv")}

    shapes = {"rel": a_rel_bias.shape, "gate_w": (depth, GATE_RANK, N_DEV * sh_gw), "gate_b": b_gate_b.shape,
              "norm_g": b_norm_g.shape, "ln_g": ln_g.shape, "ln_b": ln_b.shape}
    part = _pack_small({n: jnp.stack([grads[l][n] for l in range(depth)]) for n, _ in SMALL}, depth)
    (all_parts,) = _gather_now([part], "gather_small")
    zeros_gw = jnp.zeros(shapes["gate_w"], F32)
    w_s = _pack_small(dict(rel=a_rel_bias, gate_w=zeros_gw, gate_b=b_gate_b, norm_g=b_norm_g, ln_g=ln_g, ln_b=ln_b), depth)
    m_s = _pack_small(dict(rel=m_a_rel_bias, gate_w=zeros_gw, gate_b=m_b_gate_b, norm_g=m_b_norm_g, ln_g=m_ln_g, ln_b=m_ln_b), depth)
    v_s = _pack_small(dict(rel=v_a_rel_bias, gate_w=zeros_gw, gate_b=v_b_gate_b, norm_g=v_b_norm_g, ln_g=v_ln_g, ln_b=v_ln_b), depth)
    small = [_unpack_small(t[0], depth, shapes)
             for t in _adamw([all_parts], w_s[None], m_s[None], v_s[None], all_parts.shape[1], "adamw_small")]
    gw_grad = lax.dynamic_slice_in_dim(small[0]["gate_w"], me * sh_gw, sh_gw, axis=2).reshape(1, depth * GATE_RANK, sh_gw)
    flat = lambda t: t.reshape(1, depth * GATE_RANK, sh_gw)
    gw_res = [t.reshape(depth, GATE_RANK, sh_gw)
              for t in _adamw([gw_grad], flat(b_gate_w), flat(m_b_gate_w), flat(v_b_gate_w), depth * GATE_RANK, "adamw_gate_w")]

    def leaves(t):
        return (big["w_in"][t], small[t]["rel"], gw_res[t], small[t]["gate_b"], small[t]["norm_g"],
                big["w_kv"][t], big["w_out"][t], small[t]["ln_g"], small[t]["ln_b"])

    return (loss, dx[None]) + leaves(0) + leaves(1) + leaves(2) + leaves(3)
```

```python
import functools
import math

import numpy as np
import jax
import jax.numpy as jnp
from jax import lax
from jax.experimental import pallas as pl
from jax.experimental.pallas import tpu as pltpu

F32 = jnp.float32
BF16 = jnp.bfloat16

N_DEV = 8
D = 2048
CH = 64
LEFT = 8
MAX_REL = 128
N_MEM = 256
A_HEADS, A_DH, A_W = 8, 128, 1024
B_HEADS, B_DK, B_DV, B_KW, B_W = 4, 64, 128, 256, 512
GATE_RANK, GATE_TAU = 16, 16.0
M_HEADS, M_DH, M_W = 4, 128, 512
IN_W = 6672
NAT_SPLIT = 5648
H_W = 7168
H_PAD = H_W - IN_W
A_Q, A_K, A_V, A_Z = 0, 1024, 2048, 3072
B_BASE = 4096
M_BASE = 6144
H_DEAD = (B_BASE + 1664, M_BASE)
ALPHA = (2.0 * 4) ** 0.25
LOG2E = math.log2(math.e)
LN_EPS = 1e-5
RMS_EPS = 1e-6
NEG = -1e30
QB = 256
KB = 3 * QB
ADAM_LR, ADAM_B1, ADAM_B2, ADAM_EPS, ADAM_WD, ADAM_STEP = 0.001, 0.9, 0.999, 1e-08, 0.01, 10
VMEM_MB = 1024 * 1024


def _params(sem, vmem_mb=48):
    return pltpu.CompilerParams(dimension_semantics=sem, vmem_limit_bytes=vmem_mb * VMEM_MB)


def _sigmoid(x):
    return 1.0 / (1.0 + jnp.exp(-x))


def _dot(a, b, ca, cb, precision=None):
    return lax.dot_general(a, b, (((ca,), (cb,)), ((), ())), preferred_element_type=F32, precision=precision)


MESH = pl.DeviceIdType.MESH


class _Scatter:
    def __init__(self, xs, relations=tuple(range(N_DEV)), into=None):
        self.xs, self.n = list(xs), len(xs)
        self.relations, self.into = tuple(relations), list(into or [])
        self.out_shape = [jax.ShapeDtypeStruct(x.shape, x.dtype) for x in xs]
        self.specs = [pl.BlockSpec(memory_space=pltpu.HBM)] * self.n
        self.scratch = [pltpu.SemaphoreType.DMA((self.n, N_DEV)), pltpu.SemaphoreType.DMA((self.n, N_DEV)),
                        pltpu.SemaphoreType.DMA((self.n,))]

    def _copies(self, x_refs, o_refs, sems):
        send_sems, recv_sems, local_sems = sems
        mx, my, mc = lax.axis_index("x"), lax.axis_index("y"), lax.axis_index("c")
        me = 4 * mx + 2 * my + mc
        own, sends, arrivals = [], [], []
        for k in self.relations:
            if k == 0:
                own = [pltpu.make_async_copy(x_refs[a].at[me], o_refs[a].at[me], local_sems.at[a]) for a in range(self.n)]
                continue
            px = 1 - mx if k & 4 else mx
            py = 1 - my if k & 2 else my
            pc = 1 - mc if k & 1 else mc
            idx = 4 * px + 2 * py + pc
            for a in range(self.n):
                for dst, group in ((o_refs[a].at[me], sends), (o_refs[a].at[idx], arrivals)):
                    group.append(pltpu.make_async_remote_copy(
                        src_ref=x_refs[a].at[idx], dst_ref=dst, send_sem=send_sems.at[a, k], recv_sem=recv_sems.at[a, k],
                        device_id=(px, py, pc), device_id_type=MESH))
        return own, sends, arrivals

    def start(self, x_refs, o_refs, sems):
        own, sends, _ = self._copies(x_refs, o_refs, sems)
        for cp in own + sends:
            cp.start()

    def middle(self, x_refs, o_refs, sems):
        pass

    def finish(self, x_refs, o_refs, sems):
        own, sends, arrivals = self._copies(x_refs, o_refs, sems)
        for cp in sends:
            cp.wait_send()
        for cp in arrivals:
            cp.wait_recv()
        for cp in own:
            cp.wait()


class _Gather:
    into = ()

    def __init__(self, xs):
        self.xs, self.n = list(xs), len(xs)
        self.out_shape = [jax.ShapeDtypeStruct((N_DEV,) + x.shape, x.dtype) for x in xs]
        self.specs = [pl.BlockSpec(memory_space=pltpu.HBM)] * self.n
        self.scratch = [pltpu.SemaphoreType.DMA((self.n, N_DEV - 1)), pltpu.SemaphoreType.DMA((self.n, N_DEV - 1)),
                        pltpu.SemaphoreType.DMA((self.n,))]

    def _copies(self, x_refs, o_refs, sems):
        send_sems, recv_sems, local_sems = sems
        mx, my, mc = lax.axis_index("x"), lax.axis_index("y"), lax.axis_index("c")
        idx = lambda px, py, pc: 4 * px + 2 * py + pc
        me, sibling = (mx, my, mc), (mx, my, 1 - mc)
        chips = [(mx, 1 - my), (1 - mx, my), (1 - mx, 1 - my)]

        def copy(a, k, src, slot, to):
            return pltpu.make_async_remote_copy(
                src_ref=src, dst_ref=o_refs[a].at[idx(*slot)], send_sem=send_sems.at[a, k], recv_sem=recv_sems.at[a, k],
                device_id=to, device_id_type=MESH)

        c = dict(own=[], first=[], passed=[], ici_in=[], late_in=[])
        for a in range(self.n):
            x = x_refs[a]
            c["own"].append(pltpu.make_async_copy(x, o_refs[a].at[idx(*me)], local_sems.at[a]))
            c["first"].append(copy(a, 0, x, me, sibling))
            c["late_in"].append(copy(a, 0, x, sibling, sibling))
            for j, chip in enumerate(chips):
                c["first"].append(copy(a, 1 + j, x, me, (*chip, mc)))
                c["ici_in"].append(copy(a, 1 + j, x, (*chip, mc), (*chip, mc)))
                c["passed"].append(copy(a, 4 + j, o_refs[a].at[idx(*chip, mc)], (*chip, mc), sibling))
                c["late_in"].append(copy(a, 4 + j, x, (*chip, 1 - mc), sibling))
        return c

    def start(self, x_refs, o_refs, sems):
        c = self._copies(x_refs, o_refs, sems)
        for cp in c["own"] + c["first"]:
            cp.start()

    def middle(self, x_refs, o_refs, sems):
        c = self._copies(x_refs, o_refs, sems)
        for arrived, onward in zip(c["ici_in"], c["passed"]):
            arrived.wait_recv()
            onward.start()

    def finish(self, x_refs, o_refs, sems):
        c = self._copies(x_refs, o_refs, sems)
        for cp in c["first"] + c["passed"]:
            cp.wait_send()
        for cp in c["late_in"]:
            cp.wait_recv()
        for cp in c["own"]:
            cp.wait()


def _gather_now(xs, name):
    comm = _Gather(xs)

    def body(*refs):
        x_refs, o_refs, sems = refs[:comm.n], refs[comm.n:2 * comm.n], refs[2 * comm.n:]
        comm.start(x_refs, o_refs, sems)
        comm.middle(x_refs, o_refs, sems)
        comm.finish(x_refs, o_refs, sems)

    return pl.pallas_call(body, name=name, out_shape=tuple(comm.out_shape), in_specs=comm.specs,
                          out_specs=tuple(comm.specs), scratch_shapes=comm.scratch)(*comm.xs)


def _hosted(body, comm, *, name, grid, in_specs, out_specs, out_shape, scratch_shapes, compiler_params, inputs,
            input_output_aliases=None):
    out_specs, out_shape = tuple(out_specs), tuple(out_shape)
    aliases = input_output_aliases or {}
    if comm is None:
        outs = pl.pallas_call(body, name=name, grid=grid, in_specs=list(in_specs), out_specs=out_specs, out_shape=out_shape,
                              scratch_shapes=list(scratch_shapes), compiler_params=compiler_params,
                              input_output_aliases=aliases)(*inputs)
        return tuple(outs), ()
    ni, no, ns, nc = len(in_specs), len(out_specs), len(scratch_shapes), comm.n
    n_into = len(comm.into)
    aliases = {**aliases, **{ni + nc + a: no + a for a in range(n_into)}}

    def wrapped(*refs):
        ins, c_in = refs[:ni], refs[ni:ni + nc]
        refs = refs[ni + nc + n_into:]
        outs, c_out = refs[:no], refs[no:no + nc]
        scr, sems = refs[no + nc:no + nc + ns], refs[no + nc + ns:]
        first = functools.reduce(jnp.logical_and, [pl.program_id(d) == 0 for d in range(len(grid))])
        last = functools.reduce(jnp.logical_and, [pl.program_id(d) == grid[d] - 1 for d in range(len(grid))])

        step = functools.reduce(lambda acc, d: acc * grid[d] + pl.program_id(d), range(len(grid)), 0)
        n_steps = math.prod(grid)

        @pl.when(first)
        def _():
            comm.start(c_in, c_out, sems)

        body(*ins, *outs, *scr)

        @pl.when(step == max(7 * n_steps // 8, 1) - 1)
        def _():
            comm.middle(c_in, c_out, sems)

        @pl.when(last)
        def _():
            comm.finish(c_in, c_out, sems)

    params = pltpu.CompilerParams(dimension_semantics=("arbitrary",) * len(grid),
                                  vmem_limit_bytes=compiler_params.vmem_limit_bytes)
    outs = pl.pallas_call(wrapped, name=name, grid=grid, in_specs=list(in_specs) + comm.specs + comm.specs[:n_into],
                          out_specs=out_specs + tuple(comm.specs), out_shape=out_shape + tuple(comm.out_shape),
                          scratch_shapes=list(scratch_shapes) + comm.scratch, compiler_params=params,
                          input_output_aliases=aliases)(*inputs, *comm.xs, *comm.into)
    return tuple(outs[:no]), tuple(outs[no:])


def _live_pieces(dead, tile):
    lo, hi = dead
    t = lo // tile
    assert (hi - 1) // tile == t and lo % 128 == 0 and hi % 128 == 0
    return t, [(x, y) for x, y in ((0, lo - t * tile), (hi - t * tile, tile)) if y > x]


def _mm(a, b, *, ta=False, tb=False, out_dtype, tm, tn, tk, name, adds=(), vmem_mb=48, comm=None, dead_n=None, dead_k=None):
    m = a.shape[1] if ta else a.shape[0]
    k = a.shape[0] if ta else a.shape[1]
    n = b.shape[0] if tb else b.shape[1]
    assert k == (b.shape[1] if tb else b.shape[0])
    tm, tn, tk = min(tm, m), min(tn, n), min(tk, k)
    assert m % tm == 0 and n % tn == 0 and k % tk == 0, (name, m, n, k)
    nk = k // tk
    n_add = len(adds)
    scales = [s for _, s in adds]
    jd, cols_d = _live_pieces(dead_n, tn) if dead_n else (None, None)
    kd, ks_d = _live_pieces(dead_k, tk) if dead_k else (None, None)

    def body(a_ref, b_ref, *rest):
        add_refs, o_ref = rest[:n_add], rest[n_add]
        acc_ref = rest[n_add + 1] if nk > 1 else None
        jj, kk = pl.program_id(1), pl.program_id(2)

        def product(c0, c1, ks):
            p = None
            for k0, k1 in ks:
                a_blk = a_ref[k0:k1, :] if ta else a_ref[:, k0:k1]
                b_blk = b_ref[c0:c1, k0:k1] if tb else b_ref[k0:k1, c0:c1]
                q = _dot(a_blk.astype(BF16), b_blk.astype(BF16), 0 if ta else 1, 1 if tb else 0)
                p = q if p is None else p + q
            return p

        def step(mode, cols, ks):
            for c0, c1 in cols:
                p = product(c0, c1, ks)
                if mode == "first":
                    acc_ref[:, c0:c1] = p
                elif mode == "mid":
                    acc_ref[:, c0:c1] += p
                else:
                    r = p if mode == "only" else acc_ref[:, c0:c1] + p
                    for ref, s in zip(add_refs, scales):
                        r = r + s * ref[:, c0:c1].astype(F32)
                    o_ref[:, c0:c1] = r.astype(out_dtype)

        modes = [("only", None, range(1))] if nk == 1 else (
            [("first", kk == 0, range(1))] + ([("mid", (kk > 0) & (kk < nk - 1), range(1, nk - 1))] if nk > 2 else [])
            + [("last", kk == nk - 1, range(nk - 1, nk))])
        for mode, when_k, k_range in modes:
            k_cases = [(when_k, [(0, tk)])]
            if kd is not None and kd in k_range:
                is_dead = kk == kd
                k_cases = [(is_dead, ks_d)] + ([(when_k & ~is_dead, [(0, tk)])] if len(k_range) > 1 else [])
            for cond_k, ks in k_cases:
                n_cases = [(None, [(0, tn)])] if jd is None else [(jj == jd, cols_d), (jj != jd, [(0, tn)])]
                for cond_n, cols in n_cases:
                    conds = [c for c in (cond_k, cond_n) if c is not None]
                    run = functools.partial(step, mode, cols, ks)
                    pl.when(functools.reduce(jnp.logical_and, conds))(run) if conds else run()

    a_spec = pl.BlockSpec((tk, tm), lambda i, j, kk: (kk, i)) if ta else pl.BlockSpec((tm, tk), lambda i, j, kk: (i, kk))
    b_spec = pl.BlockSpec((tn, tk), lambda i, j, kk: (j, kk)) if tb else pl.BlockSpec((tk, tn), lambda i, j, kk: (kk, j))
    add_specs = [pl.BlockSpec((tm, tn), lambda i, j, kk: (i, j)) for _ in adds]
    (out,), c_out = _hosted(
        body, comm, name=name,
        out_shape=[jax.ShapeDtypeStruct((m, n), out_dtype)],
        grid=(m // tm, n // tn, nk),
        in_specs=[a_spec, b_spec] + add_specs,
        out_specs=[pl.BlockSpec((tm, tn), lambda i, j, kk: (i, j))],
        scratch_shapes=[pltpu.VMEM((tm, tn), F32)] if nk > 1 else [],
        compiler_params=_params(("parallel", "parallel", "arbitrary"), vmem_mb),
        inputs=(a, b, *[x for x, _ in adds]))
    return out, c_out


NKB = KB // QB
LEAD = NKB - 1


def _band_bias(table):
    i = np.arange(QB)[:, None]
    j = np.arange(KB)[None, :]
    qc = i // CH + LEAD * QB // CH
    kc = j // CH
    valid = (kc <= qc) & (kc >= qc - LEFT)
    n = QB + KB
    c = np.arange(n)
    onehot = np.zeros((2 * MAX_REL + 1, n), np.float32)
    onehot[np.clip(LEAD * QB - (c - (QB - 1)), -MAX_REL, MAX_REL) + MAX_REL, c] = 1.0
    row = jnp.dot(table.astype(F32), jnp.asarray(onehot), precision=lax.Precision.HIGHEST)
    flow = jnp.tile(row, (1, QB))[:, :QB * (n - 1)].reshape(table.shape[0], QB, n - 1)
    return jnp.where(valid[None], flow[:, :, QB - 1:] * LOG2E, NEG)


def _bias_grad(dbias):
    h, n = dbias.shape[0], KB + 1
    flat = jnp.pad(dbias.reshape(h, QB * KB), ((0, 0), (0, -(QB * KB) % n)))
    diag = flat.reshape(h, -1, n).sum(axis=1)
    c = np.arange(n)
    jm = np.where(c < LEAD * QB + CH, c, c - n)
    didx = np.clip(LEAD * QB - jm, -MAX_REL, MAX_REL) + MAX_REL
    onehot = np.zeros((n, 2 * MAX_REL + 1), np.float32)
    onehot[c, didx] = 1.0
    return jnp.dot(diag, jnp.asarray(onehot), precision=lax.Precision.HIGHEST)


def _attn_scores(q, k_refs, cs, bias_h, m, masked, scale):
    parts = []
    for t in range(NKB):
        sc = _dot(q, k_refs[t][:, cs], 1, 1) * scale
        parts.append(jnp.where(m + t - LEAD >= 0, sc, NEG) if masked else sc)
    return jnp.concatenate(parts, axis=1) + bias_h


def _attn_fwd(h, r_in, bias):
    s = h.shape[0]
    nq = s // QB
    scale = A_DH ** -0.5

    def body(q_ref, *rest):
        k_refs, v_refs = rest[:NKB], rest[NKB:2 * NKB]
        bias_hbm, r_any, o_ref, linv_ref, p_ref, bias_ref = rest[2 * NKB:]
        del r_any
        m = pl.program_id(0)

        @pl.when(m == 0)
        def _():
            pltpu.sync_copy(bias_hbm, bias_ref)

        def scores(hd, masked):
            cs = slice(hd * A_DH, (hd + 1) * A_DH)
            return _attn_scores(q_ref[:, cs], k_refs, cs, bias_ref[hd], m, masked, scale * LOG2E)

        def step(masked):
            sc_next = scores(0, masked)
            for hd in range(A_HEADS):
                cs = slice(hd * A_DH, (hd + 1) * A_DH)
                sc = sc_next
                if hd + 1 < A_HEADS:
                    sc_next = scores(hd + 1, masked)
                mx = jnp.max(sc, axis=1, keepdims=True)
                p = jnp.exp2(sc - mx)
                l = jnp.sum(p, axis=1, keepdims=True)
                pb = p.astype(BF16)
                p_ref[hd] = pb
                o = _dot(pb[:, :QB], v_refs[0][:, cs], 1, 0)
                for t in range(1, NKB):
                    o += _dot(pb[:, t * QB:(t + 1) * QB], v_refs[t][:, cs], 1, 0)
                linv = 1.0 / l
                o_ref[:, cs] = (o * linv).astype(BF16)
                linv_ref[:, hd:hd + 1] = linv

        pl.when(m < LEAD)(functools.partial(step, True))
        pl.when(m >= LEAD)(functools.partial(step, False))

    def kv_spec(col, t):
        return pl.BlockSpec((QB, A_W), lambda m: (jnp.maximum(m + t - LEAD, 0), col))

    (r, linv, probs), _ = _hosted(
        body, None, name="attn_fwd",
        out_shape=(jax.ShapeDtypeStruct(r_in.shape, BF16), jax.ShapeDtypeStruct((s, A_HEADS), F32),
                   jax.ShapeDtypeStruct((nq, A_HEADS, QB, KB), BF16)),
        grid=(nq,),
        in_specs=[pl.BlockSpec((QB, A_W), lambda m: (m, 0))]
        + [kv_spec(1, t) for t in range(NKB)] + [kv_spec(2, t) for t in range(NKB)]
        + [pl.BlockSpec(memory_space=pl.ANY), pl.BlockSpec(memory_space=pl.ANY)],
        out_specs=(pl.BlockSpec((QB, A_W), lambda m: (m, 0)), pl.BlockSpec((QB, A_HEADS), lambda m: (m, 0)),
                   pl.BlockSpec((None, A_HEADS, QB, KB), lambda m: (m, 0, 0, 0))),
        scratch_shapes=[pltpu.VMEM((A_HEADS, QB, KB), F32)],
        input_output_aliases={2 * NKB + 2: 0},
        compiler_params=_params(("arbitrary",)),
        inputs=(h,) * (2 * NKB + 1) + (bias, r_in))
    return r, linv, probs


def _attn_bwd(h, r, dy, linv, probs, dh_in, comm=None):
    s = h.shape[0]
    nq = s // QB
    scale = A_DH ** -0.5

    def body(q_ref, *rest):
        k_refs, v_refs = rest[:NKB], rest[NKB:2 * NKB]
        (z_ref, r_ref, dy_ref, linv_ref, p_ref, zl_ref, rl_ref, dyl_ref, dh_any,
         dh_ref, dbias_hbm, dk_acc, dv_acc, dq_ring, dbias_ref) = rest[2 * NKB:]
        del dh_any
        m = pl.program_id(0)

        def slot(b):
            return pl.multiple_of(lax.rem(b + NKB, NKB) * QB, QB)

        @pl.when(m == 0)
        def _():
            dk_acc[...] = jnp.zeros_like(dk_acc)
            dv_acc[...] = jnp.zeros_like(dv_acc)
            dq_ring[...] = jnp.zeros_like(dq_ring)
            dbias_ref[...] = jnp.zeros_like(dbias_ref)

        @pl.when(m < nq)
        def _():
            z = z_ref[...].astype(F32)
            do_all = dy_ref[...].astype(F32) * (z * _sigmoid(z))
            o_all = r_ref[...].astype(F32)

            def product(hd):
                cs = slice(hd * A_DH, (hd + 1) * A_DH)
                dob = do_all[:, cs].astype(BF16)
                return jnp.concatenate([_dot(dob, v_refs[t][:, cs], 1, 1) for t in range(NKB)], axis=1)

            ahead = product(0)
            for hd in range(A_HEADS):
                cs = slice(hd * A_DH, (hd + 1) * A_DH)
                q = q_ref[:, cs]
                do = do_all[:, cs]
                dob = do.astype(BF16)
                delta = jnp.sum(do * o_all[:, cs], axis=1, keepdims=True)
                dp = ahead
                if hd + 1 < A_HEADS:
                    ahead = product(hd + 1)
                p = p_ref[hd].astype(F32) * linv_ref[:, hd:hd + 1]
                ds = p * (dp - delta)
                dbias_ref[hd] += ds
                pb, dsb = p.astype(BF16), ds.astype(BF16)
                dq = jnp.zeros((QB, A_DH), F32)
                for t in range(NKB):
                    ts = slice(t * QB, (t + 1) * QB)
                    rows = pl.ds(slot(m - LEAD + t), QB)
                    dq += _dot(dsb[:, ts], k_refs[t][:, cs], 1, 0)
                    dk_acc[rows, cs] += _dot(dsb[:, ts], q, 0, 0) * scale
                    dv_acc[rows, cs] += _dot(pb[:, ts], dob, 0, 0)
                dq_ring[pl.ds(slot(m), QB), cs] = dq * scale

        done = pl.ds(slot(m - LEAD), QB)
        zl = zl_ref[...].astype(F32)
        sg = _sigmoid(zl)
        dz = dyl_ref[...].astype(F32) * rl_ref[...].astype(F32) * (sg * (1.0 + zl * (1.0 - sg)))
        dh_ref[:, A_Q:A_Q + A_W] = dq_ring[done, :].astype(BF16)
        dh_ref[:, A_K:A_K + A_W] = dk_acc[done, :].astype(BF16)
        dh_ref[:, A_V:A_V + A_W] = dv_acc[done, :].astype(BF16)
        dh_ref[:, A_Z:A_Z + A_W] = dz.astype(BF16)
        dk_acc[done, :] = jnp.zeros((QB, A_W), F32)
        dv_acc[done, :] = jnp.zeros((QB, A_W), F32)

        @pl.when(m == nq + LEAD - 1)
        def _():
            pltpu.sync_copy(dbias_ref, dbias_hbm)

    last = nq - 1

    def cur(col):
        return pl.BlockSpec((QB, A_W), lambda m: (jnp.minimum(m, last), col))

    def kv_spec(col, t):
        return pl.BlockSpec((QB, A_W), lambda m: (jnp.clip(m + t - LEAD, 0, last), col))

    def lag(col):
        return pl.BlockSpec((QB, A_W), lambda m: (jnp.clip(m - LEAD, 0, last), col))

    (dh, dbias), c_out = _hosted(
        body, comm, name="attn_bwd",
        out_shape=(jax.ShapeDtypeStruct(dh_in.shape, BF16), jax.ShapeDtypeStruct((A_HEADS, QB, KB), F32)),
        grid=(nq + LEAD,),
        in_specs=[cur(0)] + [kv_spec(1, t) for t in range(NKB)] + [kv_spec(2, t) for t in range(NKB)]
        + [cur(3), cur(0), cur(0), pl.BlockSpec((QB, A_HEADS), lambda m: (jnp.minimum(m, last), 0)),
           pl.BlockSpec((None, A_HEADS, QB, KB), lambda m: (jnp.minimum(m, last), 0, 0, 0)),
           lag(3), lag(0), lag(0), pl.BlockSpec(memory_space=pl.ANY)],
        out_specs=(pl.BlockSpec((QB, 4 * A_W), lambda m: (jnp.clip(m - LEAD, 0, last), 0)),
                   pl.BlockSpec(memory_space=pl.ANY)),
        scratch_shapes=[pltpu.VMEM((KB, A_W), F32), pltpu.VMEM((KB, A_W), F32), pltpu.VMEM((KB, A_W), F32),
                        pltpu.VMEM((A_HEADS, QB, KB), F32)],
        input_output_aliases={2 * NKB + 9: 0},
        compiler_params=_params(("arbitrary",), 56),
        inputs=(h,) * (2 * NKB + 2) + (r, dy, linv, probs, h, r, dy, dh_in))
    return dh, dbias, c_out


GB = 256
N_PAIR = B_HEADS // 2


def _gla_gates(lr, gw_ref, gb_ref):
    logit = _dot(lr, gw_ref[...], 1, 0) + gb_ref[...]
    lg = (jnp.minimum(logit, 0.0) - jnp.log(1.0 + jnp.exp(-jnp.abs(logit)))) / GATE_TAU
    row = lax.broadcasted_iota(jnp.int32, (GB, GB), 0)
    col = lax.broadcasted_iota(jnp.int32, (GB, GB), 1)
    tri = jnp.where((row // CH == col // CH) & (col <= row), 1.0, 0.0).astype(F32)
    return logit, _dot(tri, lg, 1, 0, precision=lax.Precision.HIGHEST)


def _gla_factors(hb_ref, b_all, c):
    rs = slice(c * CH, (c + 1) * CH)
    q = hb_ref[rs, 0:B_KW].astype(F32) * (B_DK ** -0.5)
    k = hb_ref[rs, B_KW:2 * B_KW].astype(F32)
    b = b_all[rs]
    bm, bl = b[CH // 2:CH // 2 + 1, :], b[CH - 1:CH, :]
    e1, e2, eb, ek = jnp.exp(b - bm), jnp.exp(bm - b), jnp.exp(b), jnp.exp(bl - b)
    el = jnp.exp(bl)
    return dict(ql=q * e1, kl=k * e2, qu=q * e2, ku=k * e1, qt=q * eb, kh=k * ek, e1=e1, e2=e2, eb=eb, ek=ek, el=el)


class _GlaPairTools:
    def __init__(self, hb_ref, fs):
        self.hb_ref, self.fs, self.made = hb_ref, fs, {}
        row = lax.broadcasted_iota(jnp.int32, (2 * CH, 2 * CH), 0)
        col = lax.broadcasted_iota(jnp.int32, (2 * CH, 2 * CH), 1)
        self.same = (row // CH) == (col // CH)
        self.lower = self.same & ((row % CH) >= (col % CH))
        self.upper = self.same & ((row % CH) < (col % CH))
        self.lower_t = self.same & ((col % CH) >= (row % CH))
        self.upper_t = self.same & ((col % CH) < (row % CH))

    def _once(self, key, make):
        if key not in self.made:
            self.made[key] = make()
        return self.made[key]

    def lanes(self, c, p, name):
        return self.fs[c][name][:, p * 128:(p + 1) * 128]

    def heads(self, c, p, name):
        def make():
            x = self.lanes(c, p, name)
            return jnp.where(self.same, jnp.concatenate([x, x], axis=0), 0.0).astype(BF16)
        return self._once(("heads", c, p, name), make)

    def twice(self, c, p, name):
        def make():
            x = self.lanes(c, p, name).astype(BF16)
            return jnp.concatenate([x, x], axis=0)
        return self._once(("twice", c, p, name), make)

    def vals(self, c, p):
        rows = slice(c * CH, (c + 1) * CH)
        return self._once(("vals", c, p), lambda: jnp.concatenate(
            [self.hb_ref[rows, 512 + (2 * p + sh) * B_DV:512 + (2 * p + sh + 1) * B_DV] for sh in range(2)], axis=0))

    def intra(self, c, p):
        lo = _dot(self.heads(c, p, "ql"), self.twice(c, p, "kl"), 1, 1)
        up = _dot(self.heads(c, p, "qu"), self.twice(c, p, "ku"), 1, 1)
        return jnp.where(self.lower, lo, jnp.where(self.upper, up, 0.0)).astype(BF16)

    def fold(self, x):
        x = jnp.where(self.same, x, 0.0)
        return x[:CH] + x[CH:]


def _gla_fwd(h, r_in, gw, gb, ng):
    s = h.shape[0]
    nb = s // GB
    cpb = GB // CH

    def body(hb_ref, gw_ref, gb_ref, ng_ref, r_any, o_ref, opre_ref, st_ref, state):
        del r_any

        @pl.when(pl.program_id(0) == 0)
        def _():
            state[...] = jnp.zeros_like(state)

        _, b_all = _gla_gates(hb_ref[:, 1536:1664], gw_ref, gb_ref)
        fs = [_gla_factors(hb_ref, b_all, c) for c in range(cpb)]
        pairs = [(c, p) for c in range(cpb) for p in range(N_PAIR)]
        tools = _GlaPairTools(hb_ref, fs)
        a = {k: tools.intra(*k) for k in pairs}
        o_intra = {k: _dot(a[k], tools.vals(*k), 1, 0) for k in pairs}
        upd = {k: _dot(tools.vals(*k), tools.heads(*k, "kh"), 0, 0) for k in pairs}
        st = [state[p] for p in range(N_PAIR)]
        entering = {}
        for c, p in pairs:
            entering[c, p] = st[p]
            st_ref[c, p] = st[p]
            st[p] = st[p] * fs[c]["el"][:, p * 128:(p + 1) * 128] + upd[c, p]
        for p in range(N_PAIR):
            state[p] = st[p]
        for c, p in pairs:
            o2 = o_intra[c, p] + _dot(tools.heads(c, p, "qt"), entering[c, p].astype(BF16), 1, 1)
            for sh in range(2):
                o = o2[sh * CH:(sh + 1) * CH]
                rs, hs = slice(c * CH, (c + 1) * CH), slice((2 * p + sh) * B_DV, (2 * p + sh + 1) * B_DV)
                opre_ref[rs, hs] = o
                rinv = lax.rsqrt(jnp.mean(o * o, axis=1, keepdims=True) + RMS_EPS)
                o_ref[rs, hs] = (o * rinv * ng_ref[...]).astype(BF16)

    (r, opre, states), _ = _hosted(
        body, None, name="gla_fwd",
        out_shape=(jax.ShapeDtypeStruct(r_in.shape, BF16), jax.ShapeDtypeStruct((s, B_W), F32),
                   jax.ShapeDtypeStruct((s // CH, N_PAIR, 128, 128), F32)),
        grid=(nb,),
        in_specs=[pl.BlockSpec((GB, 2048), lambda i: (i, B_BASE // 2048)),
                  pl.BlockSpec((128, B_KW), lambda i: (0, 0)), pl.BlockSpec((1, B_KW), lambda i: (0, 0)),
                  pl.BlockSpec((1, B_DV), lambda i: (0, 0)), pl.BlockSpec(memory_space=pl.ANY)],
        out_specs=(pl.BlockSpec((GB, B_W), lambda i: (i, 1024 // B_W)), pl.BlockSpec((GB, B_W), lambda i: (i, 0)),
                   pl.BlockSpec((cpb, N_PAIR, 128, 128), lambda i: (i, 0, 0, 0))),
        scratch_shapes=[pltpu.VMEM((N_PAIR, 128, 128), F32)],
        input_output_aliases={4: 0},
        compiler_params=_params(("arbitrary",)),
        inputs=(h, gw, gb, ng, r_in))
    return r, opre, states


def _gla_bwd(h, dy, opre, states, gw, gb, ng, dh_in):
    s = h.shape[0]
    nb = s // GB
    cpb = GB // CH

    def body(hb_ref, dy_ref, opre_ref, st_ref, gw_ref, gb_ref, ng_ref, dh_any,
             dh_ref, dgw_ref, dgb_ref, dng_ref, dstate, db_scr, do_scr):
        del dh_any

        @pl.when(pl.program_id(0) == 0)
        def _():
            dstate[...] = jnp.zeros_like(dstate)
            dgw_ref[...] = jnp.zeros_like(dgw_ref)
            dgb_ref[...] = jnp.zeros_like(dgb_ref)
            dng_ref[...] = jnp.zeros_like(dng_ref)

        lr = hb_ref[:, 1536:1664]
        logit, b_all = _gla_gates(lr, gw_ref, gb_ref)
        z = hb_ref[:, 1024:1536].astype(F32)
        sg = _sigmoid(z)
        dyb = dy_ref[...].astype(F32)
        dng = jnp.zeros((1, B_DV), F32)
        for hd in range(B_HEADS):
            hs = slice(hd * B_DV, (hd + 1) * B_DV)
            o = opre_ref[:, hs]
            rinv = lax.rsqrt(jnp.mean(o * o, axis=1, keepdims=True) + RMS_EPS)
            on = o * rinv
            dr = dyb[:, hs] * (z[:, hs] * sg[:, hs])
            dh_ref[:, 1024 + hd * B_DV:1024 + (hd + 1) * B_DV] = (
                dyb[:, hs] * (on * ng_ref[...]) * (sg[:, hs] * (1.0 + z[:, hs] * (1.0 - sg[:, hs])))).astype(BF16)
            dng += jnp.sum(dr * on, axis=0, keepdims=True)
            dn = dr * ng_ref[...]
            do_scr[:, hs] = rinv * (dn - on * jnp.mean(dn * on, axis=1, keepdims=True))
        dng_ref[...] += dng

        rowi = lax.broadcasted_iota(jnp.int32, (CH, 128), 0)
        fs = [_gla_factors(hb_ref, b_all, c) for c in range(cpb)]
        pairs = [(c, p) for c in reversed(range(cpb)) for p in range(N_PAIR)]
        tools = _GlaPairTools(hb_ref, fs)
        pair = tools.lanes

        def douts(c, p):
            rows = slice(c * CH, (c + 1) * CH)
            return tools._once(("douts", c, p), lambda: jnp.concatenate(
                [do_scr[rows, (2 * p + sh) * B_DV:(2 * p + sh + 1) * B_DV] for sh in range(2)], axis=0).astype(BF16))

        at, da, dat, dst_own, g_qt = {}, {}, {}, {}, {}
        for k in pairs:
            c, p = k
            lo_t = _dot(tools.twice(*k, "kl"), tools.heads(*k, "ql"), 1, 1)
            up_t = _dot(tools.twice(*k, "ku"), tools.heads(*k, "qu"), 1, 1)
            at[k] = jnp.where(tools.lower_t, lo_t, jnp.where(tools.upper_t, up_t, 0.0)).astype(BF16)
            da[k] = _dot(douts(*k), tools.vals(*k), 1, 1)
            dat[k] = _dot(tools.vals(*k), douts(*k), 1, 1)
            dst_own[k] = _dot(douts(*k), tools.heads(*k, "qt"), 0, 0)
            g_qt[k] = tools.fold(_dot(douts(*k), st_ref[c, p].astype(BF16), 1, 0))
        dv, g_ql, g_qu, g_kl, g_ku = {}, {}, {}, {}, {}
        for k in pairs:
            dv[k] = _dot(at[k], douts(*k), 1, 0)
            g_ql[k] = tools.fold(_dot(jnp.where(tools.lower, da[k], 0.0).astype(BF16), tools.twice(*k, "kl"), 1, 0))
            g_qu[k] = tools.fold(_dot(jnp.where(tools.upper, da[k], 0.0).astype(BF16), tools.twice(*k, "ku"), 1, 0))
            g_kl[k] = tools.fold(_dot(jnp.where(tools.lower_t, dat[k], 0.0).astype(BF16), tools.heads(*k, "ql"), 1, 0))
            g_ku[k] = tools.fold(_dot(jnp.where(tools.upper_t, dat[k], 0.0).astype(BF16), tools.heads(*k, "qu"), 1, 0))
        dst = [dstate[p] for p in range(N_PAIR)]
        leaving = {}
        for c, p in pairs:
            leaving[c, p] = dst[p]
            dst[p] = dst[p] * pair(c, p, "el") + dst_own[c, p]
        for p in range(N_PAIR):
            dstate[p] = dst[p]
        g_kh = {}
        for k in pairs:
            c, p = k
            dstb = leaving[k].astype(BF16)
            dv2 = dv[k] + _dot(tools.heads(*k, "kh"), dstb, 1, 1)
            for sh in range(2):
                hd = 2 * p + sh
                dh_ref[c * CH:(c + 1) * CH, 512 + hd * B_DV:512 + (hd + 1) * B_DV] = dv2[sh * CH:(sh + 1) * CH].astype(BF16)
            g_kh[k] = tools.fold(_dot(tools.vals(*k), dstb, 1, 0))
        for c in reversed(range(cpb)):
            rs = slice(c * CH, (c + 1) * CH)
            for p in range(N_PAIR):
                k = (c, p)
                dq = (g_ql[k] * pair(c, p, "e1") + g_qu[k] * pair(c, p, "e2") + g_qt[k] * pair(c, p, "eb")) * (B_DK ** -0.5)
                dk = g_kl[k] * pair(c, p, "e2") + g_ku[k] * pair(c, p, "e1") + g_kh[k] * pair(c, p, "ek")
                dkh_kh = g_kh[k] * pair(c, p, "kh")
                db = (g_ql[k] * pair(c, p, "ql") - g_qu[k] * pair(c, p, "qu") + g_qt[k] * pair(c, p, "qt")
                      - g_kl[k] * pair(c, p, "kl") + g_ku[k] * pair(c, p, "ku") - dkh_kh)
                db_last = (pair(c, p, "el") * jnp.sum(leaving[c, p] * st_ref[c, p], axis=0, keepdims=True)
                           + jnp.sum(dkh_kh, axis=0, keepdims=True))
                db = jnp.where(rowi == CH - 1, db + db_last, db)
                dh_ref[rs, p * 128:(p + 1) * 128] = dq.astype(BF16)
                dh_ref[rs, B_KW + p * 128:B_KW + (p + 1) * 128] = dk.astype(BF16)
                db_scr[rs, p * 128:(p + 1) * 128] = db

        row = lax.broadcasted_iota(jnp.int32, (GB, GB), 0)
        col = lax.broadcasted_iota(jnp.int32, (GB, GB), 1)
        trit = jnp.where((row // CH == col // CH) & (col >= row), 1.0, 0.0).astype(F32)
        dlg = _dot(trit, db_scr[...], 1, 0, precision=lax.Precision.HIGHEST)
        dlogit = dlg * (_sigmoid(-logit) / GATE_TAU)
        dlb = dlogit.astype(BF16)
        dgw_ref[...] += _dot(lr, dlb, 0, 0)
        dgb_ref[...] += jnp.sum(dlogit, axis=0, keepdims=True)
        dh_ref[:, 1536:1664] = _dot(dlb, gw_ref[...], 1, 1).astype(BF16)
        dh_ref[:, 1664:2048] = jnp.zeros((GB, 384), BF16)

    rev = lambda i: nb - 1 - i
    (dh, dgw, dgb, dng), _ = _hosted(
        body, None, name="gla_bwd",
        out_shape=(jax.ShapeDtypeStruct(dh_in.shape, BF16), jax.ShapeDtypeStruct((128, B_KW), F32),
                   jax.ShapeDtypeStruct((1, B_KW), F32), jax.ShapeDtypeStruct((1, B_DV), F32)),
        grid=(nb,),
        in_specs=[pl.BlockSpec((GB, 2048), lambda i: (rev(i), B_BASE // 2048)),
                  pl.BlockSpec((GB, B_W), lambda i: (rev(i), 1024 // B_W)),
                  pl.BlockSpec((GB, B_W), lambda i: (rev(i), 0)),
                  pl.BlockSpec((cpb, N_PAIR, 128, 128), lambda i: (rev(i), 0, 0, 0)),
                  pl.BlockSpec((128, B_KW), lambda i: (0, 0)), pl.BlockSpec((1, B_KW), lambda i: (0, 0)),
                  pl.BlockSpec((1, B_DV), lambda i: (0, 0)), pl.BlockSpec(memory_space=pl.ANY)],
        out_specs=(pl.BlockSpec((GB, 2048), lambda i: (rev(i), B_BASE // 2048)),
                   pl.BlockSpec((128, B_KW), lambda i: (0, 0)), pl.BlockSpec((1, B_KW), lambda i: (0, 0)),
                   pl.BlockSpec((1, B_DV), lambda i: (0, 0))),
        scratch_shapes=[pltpu.VMEM((N_PAIR, 128, 128), F32), pltpu.VMEM((GB, B_KW), F32), pltpu.VMEM((GB, B_W), F32)],
        input_output_aliases={7: 0},
        compiler_params=_params(("arbitrary",)),
        inputs=(h, dy, opre, states, gw, gb, ng, dh_in))
    return dh, dgw, dgb, dng


MB = 512


def _mem_probs(q, mk, scale):
    sc = _dot(q, mk, 1, 1) * (scale * LOG2E)
    p = jnp.exp2(sc - jnp.max(sc, axis=1, keepdims=True))
    return p / jnp.sum(p, axis=1, keepdims=True)


def _mem_fwd(h, r_in, mkv):
    s = h.shape[0]
    scale = M_DH ** -0.5

    def body(q_ref, mkv_ref, r_any, o_ref):
        del r_any
        for hd in range(M_HEADS):
            cs = slice(hd * M_DH, (hd + 1) * M_DH)
            p = _mem_probs(q_ref[:, cs], mkv_ref[:, cs], scale)
            o_ref[:, cs] = _dot(p.astype(BF16), mkv_ref[:, M_W + hd * M_DH:M_W + (hd + 1) * M_DH], 1, 0).astype(BF16)

    return pl.pallas_call(
        body, name="mem_fwd",
        out_shape=jax.ShapeDtypeStruct(r_in.shape, BF16),
        grid=(s // MB,),
        in_specs=[pl.BlockSpec((MB, M_W), lambda i: (i, M_BASE // M_W)),
                  pl.BlockSpec((N_MEM, 2 * M_W), lambda i: (0, 0)), pl.BlockSpec(memory_space=pl.ANY)],
        out_specs=pl.BlockSpec((MB, M_W), lambda i: (i, 1536 // M_W)),
        input_output_aliases={2: 0},
        compiler_params=_params(("arbitrary",)),
    )(h, mkv, r_in)


def _mem_bwd(h, r, dy, mkv, dh_in):
    s = h.shape[0]
    scale = M_DH ** -0.5

    def body(q_ref, z_ref, r_ref, dy_ref, mkv_ref, dh_any, dh_ref, dmkv_ref):
        del dh_any

        @pl.when(pl.program_id(0) == 0)
        def _():
            dmkv_ref[...] = jnp.zeros_like(dmkv_ref)

        z = z_ref[...].astype(F32)
        sg = _sigmoid(z)
        dyv = dy_ref[...].astype(F32)
        do_all = dyv * (z * sg)
        dh_ref[:, M_W:2 * M_W] = (dyv * r_ref[...].astype(F32) * (sg * (1.0 + z * (1.0 - sg)))).astype(BF16)
        for hd in range(M_HEADS):
            cs = slice(hd * M_DH, (hd + 1) * M_DH)
            vs = slice(M_W + hd * M_DH, M_W + (hd + 1) * M_DH)
            q = q_ref[:, cs]
            p = _mem_probs(q, mkv_ref[:, cs], scale)
            dob = do_all[:, cs].astype(BF16)
            dp = _dot(dob, mkv_ref[:, vs], 1, 1)
            ds = p * (dp - jnp.sum(p * dp, axis=1, keepdims=True))
            dsb = ds.astype(BF16)
            dh_ref[:, cs] = (_dot(dsb, mkv_ref[:, cs], 1, 0) * scale).astype(BF16)
            dmkv_ref[:, cs] += _dot(dsb, q, 0, 0) * scale
            dmkv_ref[:, vs] += _dot(p.astype(BF16), dob, 0, 0)

    return pl.pallas_call(
        body, name="mem_bwd",
        out_shape=(jax.ShapeDtypeStruct(dh_in.shape, BF16), jax.ShapeDtypeStruct((N_MEM, 2 * M_W), F32)),
        grid=(s // MB,),
        in_specs=[pl.BlockSpec((MB, M_W), lambda i: (i, M_BASE // M_W)),
                  pl.BlockSpec((MB, M_W), lambda i: (i, M_BASE // M_W + 1)),
                  pl.BlockSpec((MB, M_W), lambda i: (i, 1536 // M_W)),
                  pl.BlockSpec((MB, M_W), lambda i: (i, 1536 // M_W)),
                  pl.BlockSpec((N_MEM, 2 * M_W), lambda i: (0, 0)), pl.BlockSpec(memory_space=pl.ANY)],
        out_specs=(pl.BlockSpec((MB, 2 * M_W), lambda i: (i, M_BASE // (2 * M_W))),
                   pl.BlockSpec((N_MEM, 2 * M_W), lambda i: (0, 0))),
        input_output_aliases={5: 0},
        compiler_params=_params(("arbitrary",)),
    )(h, h, r, dy, mkv, dh_in)


OB = 256


def _outproj_ln(h, r, w_out, x, ln_g, ln_b, target=None):
    s = h.shape[0]
    last = target is not None

    def body(za_ref, zb_ref, zm_ref, r_ref, w_ref, x_ref, g_ref, b_ref, *rest):
        if last:
            t_ref, xn_ref, xh_ref, rstd_ref, y_ref, l_ref = rest
        else:
            xn_ref, xh_ref, rstd_ref, y_ref = rest
        z = jnp.concatenate([za_ref[...], zb_ref[...], zm_ref[...]], axis=1).astype(F32)
        y = (r_ref[...].astype(F32) * (z * _sigmoid(z))).astype(BF16)
        y_ref[...] = y
        u = ALPHA * x_ref[...] + _dot(y, w_ref[...], 1, 0)
        mu = jnp.mean(u, axis=1, keepdims=True)
        uc = u - mu
        rstd = lax.rsqrt(jnp.mean(uc * uc, axis=1, keepdims=True) + LN_EPS)
        xh = uc * rstd
        xh_ref[...] = xh
        rstd_ref[...] = rstd
        xn = xh * g_ref[...] + b_ref[...]
        if last:
            @pl.when(pl.program_id(0) == 0)
            def _():
                l_ref[...] = jnp.zeros_like(l_ref)

            e = xn - t_ref[...]
            xn_ref[...] = e / D
            l_ref[...] += 0.5 * jnp.sum(jnp.mean(e * e, axis=1, keepdims=True))
        else:
            xn_ref[...] = xn

    row = lambda w, c: pl.BlockSpec((OB, w), lambda i: (i, c))
    vec = pl.BlockSpec((1, D), lambda i: (0, 0))
    full = jax.ShapeDtypeStruct((s, D), F32)
    return pl.pallas_call(
        body, name="outproj_ln_loss" if last else "outproj_ln",
        out_shape=(full, full, jax.ShapeDtypeStruct((s, 1), F32), jax.ShapeDtypeStruct((s, D), BF16))
        + ((jax.ShapeDtypeStruct((1, 128), F32),) if last else ()),
        grid=(s // OB,),
        in_specs=[row(A_W, A_Z // A_W), row(B_W, (B_BASE + 1024) // B_W), row(M_W, (M_BASE + M_W) // M_W), row(D, 0),
                  pl.BlockSpec((D, D), lambda i: (0, 0)), row(D, 0), vec, vec] + ([row(D, 0)] if last else []),
        out_specs=(row(D, 0), row(D, 0), pl.BlockSpec((OB, 1), lambda i: (i, 0)), row(D, 0))
        + ((pl.BlockSpec((1, 128), lambda i: (0, 0)),) if last else ()),
        compiler_params=_params(("arbitrary",), 56),
    )(h, h, h, r, w_out, x, ln_g, ln_b, *([target] if last else []))


def _ln_bwd_dy(g, xh, rstd, ln_g, w_out, y):
    s = g.shape[0]
    n = s // OB

    def body(g_ref, xh_ref, rstd_ref, lg_ref, w_ref, y_ref, dy_ref, du_ref, dg_ref, db_ref, dw_hbm, dw_acc, stage):
        i = pl.program_id(0)

        @pl.when(i == 0)
        def _():
            dg_ref[...] = jnp.zeros_like(dg_ref)
            db_ref[...] = jnp.zeros_like(db_ref)
            dw_acc[...] = jnp.zeros_like(dw_acc)

        gv, xh = g_ref[...], xh_ref[...]
        dg_ref[...] += jnp.sum(gv * xh, axis=0, keepdims=True)
        db_ref[...] += jnp.sum(gv, axis=0, keepdims=True)
        dxh = gv * lg_ref[...]
        du = rstd_ref[...] * (dxh - jnp.mean(dxh, axis=1, keepdims=True) - xh * jnp.mean(dxh * xh, axis=1, keepdims=True))
        du_ref[...] = du
        dub = du.astype(BF16)
        dy_ref[...] = _dot(dub, w_ref[...], 1, 1).astype(BF16)
        dw_acc[...] += _dot(y_ref[...], dub, 0, 0)

        @pl.when(i == n - 1)
        def _():
            for c in range(D // OB):
                stage[...] = dw_acc[c * OB:(c + 1) * OB, :].astype(BF16)
                pltpu.sync_copy(stage, dw_hbm.at[c * OB:(c + 1) * OB, :])

    row = pl.BlockSpec((OB, D), lambda i: (i, 0))
    vec = pl.BlockSpec((1, D), lambda i: (0, 0))
    return pl.pallas_call(
        body, name="ln_bwd_dy",
        out_shape=(jax.ShapeDtypeStruct((s, D), BF16), jax.ShapeDtypeStruct((s, D), F32),
                   jax.ShapeDtypeStruct((1, D), F32), jax.ShapeDtypeStruct((1, D), F32), jax.ShapeDtypeStruct((D, D), BF16)),
        grid=(n,),
        in_specs=[row, row, pl.BlockSpec((OB, 1), lambda i: (i, 0)), vec, pl.BlockSpec((D, D), lambda i: (0, 0)), row],
        out_specs=(row, row, vec, vec, pl.BlockSpec(memory_space=pl.ANY)),
        scratch_shapes=[pltpu.VMEM((D, D), F32), pltpu.VMEM((OB, D), BF16)],
        compiler_params=_params(("arbitrary",), 60),
    )(g, xh, rstd, ln_g, w_out, y)


class _LocalWeights:
    def __init__(self, w_in_p, w_out_f, w_kv_f, gate_w):
        self.w, self.gate = list(zip(w_in_p, w_out_f, w_kv_f)), gate_w
        self.depth = len(self.w)
        self.grads = [dict() for _ in self.w]

    def w_in(self, l):
        return self.w[l][0]

    def w_rest(self, l):
        return self.w[l][1:]

    def gate_w(self, l):
        return self.gate[l]

    def host(self, where, l, payload=None):
        if payload is not None:
            self.grads[l][where] = payload
        return None

    def landed(self, where, l, outs):
        pass


def _shard_pieces():
    sh = IN_W // N_DEV
    pieces = []
    for j in range(N_DEV):
        lo, hi = j * sh, (j + 1) * sh
        cuts = [lo, NAT_SPLIT, hi] if lo < NAT_SPLIT < hi else [lo, hi]
        for a, b in zip(cuts[:-1], cuts[1:]):
            pieces.append((j, a - lo, a if a < NAT_SPLIT else a + H_PAD, b - a))
    return pieces


RB = 256


def _shards_to_padded(raw):
    def body(x_ref, o_ref):
        for j, src, dst, width in _shard_pieces():
            o_ref[:, dst:dst + width] = x_ref[j, :, src:src + width]
        o_ref[:, NAT_SPLIT:NAT_SPLIT + H_PAD] = jnp.zeros((RB, H_PAD), o_ref.dtype)

    return pl.pallas_call(
        body, name="place_w_in", out_shape=jax.ShapeDtypeStruct((D, H_W), raw.dtype), grid=(D // RB,),
        in_specs=[pl.BlockSpec((N_DEV, RB, raw.shape[2]), lambda i: (0, i, 0))],
        out_specs=pl.BlockSpec((RB, H_W), lambda i: (i, 0)),
        compiler_params=_params(("parallel",)),
    )(raw)


def _padded_to_shards(w):
    sh = IN_W // N_DEV

    def body(x_ref, o_ref):
        for j, src, dst, width in _shard_pieces():
            o_ref[j, :, src:src + width] = x_ref[:, dst:dst + width]

    return pl.pallas_call(
        body, name="split_d_w_in", out_shape=jax.ShapeDtypeStruct((N_DEV, D, sh), w.dtype), grid=(D // RB,),
        in_specs=[pl.BlockSpec((RB, H_W), lambda i: (i, 0))],
        out_specs=pl.BlockSpec((N_DEV, RB, sh), lambda i: (0, i, 0)),
        compiler_params=_params(("parallel",)),
    )(w)


_PART_A = (0, 1, 2, 4, 7)
_PART_B = (3, 5, 6)


class _Fsdp:
    def __init__(self, w_in, w_out, w_kv, gate_w):
        self.sh = (w_in, w_out, w_kv)
        self.gate_sh = gate_w
        self.depth = w_in.shape[0]
        self.raw = [dict() for _ in range(self.depth)]
        self.recv = [dict() for _ in range(self.depth)]
        (self.raw[0]["w_in"],) = _gather_now([w_in[0]], "gather_layer0")

    def w_in(self, l):
        return _shards_to_padded(self.raw[l]["w_in"])

    def w_rest(self, l):
        return self.raw[l]["w_out"].reshape(D, D), self.raw[l]["w_kv"].reshape(D, 2 * M_W)

    def gate_w(self, l):
        return jnp.transpose(self.gate_all[:, l], (1, 0, 2)).reshape(GATE_RANK, -1)

    def host(self, where, l, payload=None):
        w_in, w_out, w_kv = self.sh
        if where == "in_proj":
            xs = [w_out[0], w_kv[0], self.gate_sh] if l == 0 else []
            xs += [w_in[l + 1], w_out[l + 1], w_kv[l + 1]] if l + 1 < self.depth else []
            return _Gather(xs) if xs else None
        if where == "d_w_in":
            d_wout, d_wkv = payload
            return _Scatter([d_wout.reshape(N_DEV, D // N_DEV, D), d_wkv.reshape(N_DEV, D // N_DEV, 2 * M_W)])
        if where == "d_x":
            self.blocks = _padded_to_shards(payload)
            return _Scatter([self.blocks], relations=_PART_A if l > 0 else tuple(range(N_DEV)))
        if where == "attn_bwd" and l + 1 < self.depth:
            return _Scatter([self.blocks], relations=_PART_B, into=[self.recv[l + 1]["w_in"]])
        return None

    def landed(self, where, l, outs):
        if where == "in_proj" and outs:
            outs = list(outs)
            if l == 0:
                self.raw[0]["w_out"], self.raw[0]["w_kv"], self.gate_all = outs[:3]
                outs = outs[3:]
            if outs:
                self.raw[l + 1] = dict(zip(("w_in", "w_out", "w_kv"), outs))
        elif where == "d_w_in":
            self.recv[l]["w_out"], self.recv[l]["w_kv"] = outs
        elif where == "d_x":
            self.recv[l]["w_in"] = outs[0]
        elif where == "attn_bwd" and outs:
            self.recv[l + 1]["w_in"] = outs[0]


def _local_step(x, mem, target, pipe, rel, gate_b, norm_g, ln_g, ln_b):
    depth = pipe.depth
    s = x.shape[0]
    saved = []
    xl = x
    for l in range(depth):
        w_in_p = pipe.w_in(l)
        hmat, landed = _mm(xl, w_in_p, out_dtype=BF16, tm=1024, tn=1024, tk=D, name="in_proj",
                           comm=pipe.host("in_proj", l), dead_n=H_DEAD)
        pipe.landed("in_proj", l, landed)
        w_out_f, w_kv_f = pipe.w_rest(l)
        mkv, _ = _mm(mem, w_kv_f, out_dtype=BF16, tm=N_MEM, tn=1024, tk=D, name="mem_kv")
        bias = _band_bias(rel[l])
        gw = jnp.zeros((128, B_KW), F32).at[:GATE_RANK].set(pipe.gate_w(l)).astype(BF16)
        gb, ng = gate_b[l][None, :], norm_g[l][None, :]
        r, linv, probs = _attn_fwd(hmat, lax.empty((s, D), BF16), bias)
        r, opre, states = _gla_fwd(hmat, r, gw, gb, ng)
        r = _mem_fwd(hmat, r, mkv)
        xn, xh, rstd, y, *loss = _outproj_ln(hmat, r, w_out_f, xl, ln_g[l][None, :], ln_b[l][None, :],
                                             target if l == depth - 1 else None)
        saved.append(dict(x=xl, h=hmat, mkv=mkv, gw=gw, gb=gb, ng=ng, r=r, linv=linv, probs=probs, opre=opre,
                          states=states, xh=xh, rstd=rstd, y=y, w_in_p=w_in_p, w_out_f=w_out_f))
        xl = xn
    (loss,), g = loss, xl

    grads = [None] * depth
    for l in reversed(range(depth)):
        sv = saved[l]
        dy, du, d_lng, d_lnb, d_wout = _ln_bwd_dy(g, sv["xh"], sv["rstd"], ln_g[l][None, :], sv["w_out_f"], sv["y"])
        dh, dbias, landed = _attn_bwd(sv["h"], sv["r"], dy, sv["linv"], sv["probs"], lax.empty((s, H_W), BF16),
                                      pipe.host("attn_bwd", l))
        pipe.landed("attn_bwd", l, landed)
        dh, d_gw, d_gb, d_ng = _gla_bwd(sv["h"], dy, sv["opre"], sv["states"], sv["gw"], sv["gb"], sv["ng"], dh)
        dh, d_mkv = _mem_bwd(sv["h"], sv["r"], dy, sv["mkv"], dh)
        d_wkv, _ = _mm(mem, d_mkv, ta=True, out_dtype=BF16, tm=1024, tn=1024, tk=N_MEM, name="d_w_kv")
        d_win, landed = _mm(sv["x"], dh, ta=True, out_dtype=BF16, tm=1024, tn=1792, tk=1024, name="d_w_in",
                            comm=pipe.host("d_w_in", l, (d_wout, d_wkv)), dead_n=H_DEAD)
        pipe.landed("d_w_in", l, landed)
        g, landed = _mm(dh, sv["w_in_p"], tb=True, out_dtype=F32, tm=1024, tn=1024, tk=1792, name="d_x",
                        adds=((du, ALPHA),), comm=pipe.host("d_x", l, d_win), dead_k=H_DEAD)
        pipe.landed("d_x", l, landed)
        grads[l] = dict(rel=_bias_grad(dbias), gate_w=d_gw[:GATE_RANK], gate_b=d_gb[0], norm_g=d_ng[0],
                        ln_g=d_lng[0], ln_b=d_lnb[0])
    return loss, g, grads


def _adamw(parts, w, m, v, rows_per_step, name):
    depth, rows, cols = w.shape
    n = parts[0].shape[0]
    tr = min(rows_per_step, rows)
    assert rows % tr == 0 and len(parts) == depth

    def body(*refs):
        p_refs = refs[:depth]
        w_ref, m_ref, v_ref, g_ref, d_ref, nm_ref, nv_ref = refs[depth:]
        for l in range(depth):
            @pl.when(pl.program_id(0) == l)
            def _(p_ref=p_refs[l]):
                g = p_ref[0].astype(F32)
                for j in range(1, n):
                    g = g + p_ref[j].astype(F32)
                nm = ADAM_B1 * m_ref[...] + (1.0 - ADAM_B1) * g
                nv = ADAM_B2 * v_ref[...] + (1.0 - ADAM_B2) * (g * g)
                m_hat = nm / (1.0 - ADAM_B1 ** ADAM_STEP)
                v_hat = nv / (1.0 - ADAM_B2 ** ADAM_STEP)
                g_ref[...] = g
                nm_ref[...] = nm
                nv_ref[...] = nv
                d_ref[...] = -ADAM_LR * (m_hat / (jnp.sqrt(v_hat) + ADAM_EPS) + ADAM_WD * w_ref[...])

    def part_spec(l):
        return pl.BlockSpec((n, tr, cols), lambda ll, i: (0, jnp.where(ll == l, i, 0), 0))

    blk = pl.BlockSpec((None, tr, cols), lambda ll, i: (ll, i, 0))
    shape = jax.ShapeDtypeStruct((depth, rows, cols), F32)
    return pl.pallas_call(
        body, name=name,
        out_shape=(shape, shape, shape, shape),
        grid=(depth, rows // tr),
        in_specs=[part_spec(l) for l in range(depth)] + [blk, blk, blk],
        out_specs=(blk, blk, blk, blk),
        compiler_params=_params(("arbitrary", "arbitrary")),
    )(*parts, w, m, v)


SMALL = (("rel", A_HEADS * (2 * MAX_REL + 1)), ("gate_w", GATE_RANK * B_KW), ("gate_b", B_KW), ("norm_g", B_DV),
         ("ln_g", D), ("ln_b", D))


def _pack_small(parts, depth):
    rows = []
    for name, size in SMALL:
        flat = parts[name].reshape(depth * size).astype(F32)
        rows.append(jnp.pad(flat, (0, -(depth * size) % 128)).reshape(-1, 128))
    packed = jnp.concatenate(rows, axis=0)
    return jnp.pad(packed, ((0, -packed.shape[0] % 8), (0, 0)))


def _unpack_small(packed, depth, shapes):
    out, row = {}, 0
    for name, size in SMALL:
        nrow = -(-(depth * size) // 128)
        out[name] = packed[row:row + nrow].reshape(-1)[:depth * size].reshape(shapes[name])
        row += nrow
    return out


def kernel(x, mem, w_in, a_rel_bias, b_gate_w, b_gate_b, b_norm_g, w_mem_kv, w_out, ln_g, ln_b, loss_target, m_w_in, m_a_rel_bias, m_b_gate_w, m_b_gate_b, m_b_norm_g, m_w_mem_kv, m_w_out, m_ln_g, m_ln_b, v_w_in, v_a_rel_bias, v_b_gate_w, v_b_gate_b, v_b_norm_g, v_w_mem_kv, v_w_out, v_ln_g, v_ln_b):
    depth = w_in.shape[0]
    sh_in = w_in.shape[2]
    sh_gw = b_gate_w.shape[2]
    me = 4 * lax.axis_index("x") + 2 * lax.axis_index("y") + lax.axis_index("c")

    pipe = _Fsdp(w_in.astype(BF16), w_out.astype(BF16), w_mem_kv.astype(BF16), b_gate_w)
    loss_dev, dx, grads = _local_step(x[0], mem[0], loss_target[0], pipe, a_rel_bias, b_gate_b, b_norm_g, ln_g, ln_b)
    loss = lax.psum(loss_dev[0, 0], ("x", "y", "c"))

    recv = lambda n: [pipe.recv[l][n] for l in range(depth)]
    big = {"w_in": _adamw(recv("w_in"), w_in, m_w_in, v_w_in, 128, "adamw_w_in"),
           "w_out": _adamw(recv("w_out"), w_out, m_w_out, v_w_out, 64, "adamw_w_out"),
           "w_kv": _adamw(recv("w_kv"), w_mem_kv, m_w_mem_kv, v_w_mem_kv, 128, "adamw_w_kv")}

    shapes = {"rel": a_rel_bias.shape, "gate_w": (depth, GATE_RANK, N_DEV * sh_gw), "gate_b": b_gate_b.shape,
              "norm_g": b_norm_g.shape, "ln_g": ln_g.shape, "ln_b": ln_b.shape}
    part = _pack_small({n: jnp.stack([grads[l][n] for l in range(depth)]) for n, _ in SMALL}, depth)
    (all_parts,) = _gather_now([part], "gather_small")
    zeros_gw = jnp.zeros(shapes["gate_w"], F32)
    w_s = _pack_small(dict(rel=a_rel_bias, gate_w=zeros_gw, gate_b=b_gate_b, norm_g=b_norm_g, ln_g=ln_g, ln_b=ln_b), depth)
    m_s = _pack_small(dict(rel=m_a_rel_bias, gate_w=zeros_gw, gate_b=m_b_gate_b, norm_g=m_b_norm_g, ln_g=m_ln_g, ln_b=m_ln_b), depth)
    v_s = _pack_small(dict(rel=v_a_rel_bias, gate_w=zeros_gw, gate_b=v_b_gate_b, norm_g=v_b_norm_g, ln_g=v_ln_g, ln_b=v_ln_b), depth)
    small = [_unpack_small(t[0], depth, shapes)
             for t in _adamw([all_parts], w_s[None], m_s[None], v_s[None], all_parts.shape[1], "adamw_small")]
    gw_grad = lax.dynamic_slice_in_dim(small[0]["gate_w"], me * sh_gw, sh_gw, axis=2).reshape(1, depth * GATE_RANK, sh_gw)
    flat = lambda t: t.reshape(1, depth * GATE_RANK, sh_gw)
    gw_res = [t.reshape(depth, GATE_RANK, sh_gw)
              for t in _adamw([gw_grad], flat(b_gate_w), flat(m_b_gate_w), flat(v_b_gate_w), depth * GATE_RANK, "adamw_gate_w")]

    def leaves(t):
        return (big["w_in"][t], small[t]["rel"], gw_res[t], small[t]["gate_b"], small[t]["norm_g"],
                big["w_kv"][t], big["w_out"][t], small[t]["ln_g"], small[t]["ln_b"])

    return (loss, dx[None]) + leaves(0) + leaves(1) + leaves(2) + leaves(3)
```

```python
import functools
import math

import numpy as np
import jax
import jax.numpy as jnp
from jax import lax
from jax.experimental import pallas as pl
from jax.experimental.pallas import tpu as pltpu

F32 = jnp.float32
BF16 = jnp.bfloat16

N_DEV = 8
D = 2048
CH = 64
LEFT = 8
MAX_REL = 128
N_MEM = 256
A_HEADS, A_DH, A_W = 8, 128, 1024
B_HEADS, B_DK, B_DV, B_KW, B_W = 4, 64, 128, 256, 512
GATE_RANK, GATE_TAU = 16, 16.0
M_HEADS, M_DH, M_W = 4, 128, 512
IN_W = 6672
NAT_SPLIT = 5648
H_W = 7168
H_PAD = H_W - IN_W
A_Q, A_K, A_V, A_Z = 0, 1024, 2048, 3072
B_BASE = 4096
M_BASE = 6144
H_DEAD = (B_BASE + 1664, M_BASE)
ALPHA = (2.0 * 4) ** 0.25
LOG2E = math.log2(math.e)
LN_EPS = 1e-5
RMS_EPS = 1e-6
NEG = -1e30
QB = 256
KB = 3 * QB
ADAM_LR, ADAM_B1, ADAM_B2, ADAM_EPS, ADAM_WD, ADAM_STEP = 0.001, 0.9, 0.999, 1e-08, 0.01, 10
VMEM_MB = 1024 * 1024


def _params(sem, vmem_mb=48):
    return pltpu.CompilerParams(dimension_semantics=sem, vmem_limit_bytes=vmem_mb * VMEM_MB)


def _sigmoid(x):
    return 1.0 / (1.0 + jnp.exp(-x))


def _dot(a, b, ca, cb, precision=None):
    return lax.dot_general(a, b, (((ca,), (cb,)), ((), ())), preferred_element_type=F32, precision=precision)


MESH = pl.DeviceIdType.MESH


class _Scatter:
    def __init__(self, xs, relations=tuple(range(N_DEV)), into=None):
        self.xs, self.n = list(xs), len(xs)
        self.relations, self.into = tuple(relations), list(into or [])
        self.out_shape = [jax.ShapeDtypeStruct(x.shape, x.dtype) for x in xs]
        self.specs = [pl.BlockSpec(memory_space=pltpu.HBM)] * self.n
        self.scratch = [pltpu.SemaphoreType.DMA((self.n, N_DEV)), pltpu.SemaphoreType.DMA((self.n, N_DEV)),
                        pltpu.SemaphoreType.DMA((self.n,))]

    def _copies(self, x_refs, o_refs, sems):
        send_sems, recv_sems, local_sems = sems
        mx, my, mc = lax.axis_index("x"), lax.axis_index("y"), lax.axis_index("c")
        me = 4 * mx + 2 * my + mc
        own, sends, arrivals = [], [], []
        for k in self.relations:
            if k == 0:
                own = [pltpu.make_async_copy(x_refs[a].at[me], o_refs[a].at[me], local_sems.at[a]) for a in range(self.n)]
                continue
            px = 1 - mx if k & 4 else mx
            py = 1 - my if k & 2 else my
            pc = 1 - mc if k & 1 else mc
            idx = 4 * px + 2 * py + pc
            for a in range(self.n):
                for dst, group in ((o_refs[a].at[me], sends), (o_refs[a].at[idx], arrivals)):
                    group.append(pltpu.make_async_remote_copy(
                        src_ref=x_refs[a].at[idx], dst_ref=dst, send_sem=send_sems.at[a, k], recv_sem=recv_sems.at[a, k],
                        device_id=(px, py, pc), device_id_type=MESH))
        return own, sends, arrivals

    def start(self, x_refs, o_refs, sems):
        own, sends, _ = self._copies(x_refs, o_refs, sems)
        for cp in own + sends:
            cp.start()

    def middle(self, x_refs, o_refs, sems):
        pass

    def finish(self, x_refs, o_refs, sems):
        own, sends, arrivals = self._copies(x_refs, o_refs, sems)
        for cp in sends:
            cp.wait_send()
        for cp in arrivals:
            cp.wait_recv()
        for cp in own:
            cp.wait()


class _Gather:
    into = ()

    def __init__(self, xs):
        self.xs, self.n = list(xs), len(xs)
        self.out_shape = [jax.ShapeDtypeStruct((N_DEV,) + x.shape, x.dtype) for x in xs]
        self.specs = [pl.BlockSpec(memory_space=pltpu.HBM)] * self.n
        self.scratch = [pltpu.SemaphoreType.DMA((self.n, N_DEV - 1)), pltpu.SemaphoreType.DMA((self.n, N_DEV - 1)),
                        pltpu.SemaphoreType.DMA((self.n,))]

    def _copies(self, x_refs, o_refs, sems):
        send_sems, recv_sems, local_sems = sems
        mx, my, mc = lax.axis_index("x"), lax.axis_index("y"), lax.axis_index("c")
        idx = lambda px, py, pc: 4 * px + 2 * py + pc
        me, sibling = (mx, my, mc), (mx, my, 1 - mc)
        chips = [(mx, 1 - my), (1 - mx, my), (1 - mx, 1 - my)]

        def copy(a, k, src, slot, to):
            return pltpu.make_async_remote_copy(
                src_ref=src, dst_ref=o_refs[a].at[idx(*slot)], send_sem=send_sems.at[a, k], recv_sem=recv_sems.at[a, k],
                device_id=to, device_id_type=MESH)

        c = dict(own=[], first=[], passed=[], ici_in=[], late_in=[])
        for a in range(self.n):
            x = x_refs[a]
            c["own"].append(pltpu.make_async_copy(x, o_refs[a].at[idx(*me)], local_sems.at[a]))
            c["first"].append(copy(a, 0, x, me, sibling))
            c["late_in"].append(copy(a, 0, x, sibling, sibling))
            for j, chip in enumerate(chips):
                c["first"].append(copy(a, 1 + j, x, me, (*chip, mc)))
                c["ici_in"].append(copy(a, 1 + j, x, (*chip, mc), (*chip, mc)))
                c["passed"].append(copy(a, 4 + j, o_refs[a].at[idx(*chip, mc)], (*chip, mc), sibling))
                c["late_in"].append(copy(a, 4 + j, x, (*chip, 1 - mc), sibling))
        return c

    def start(self, x_refs, o_refs, sems):
        c = self._copies(x_refs, o_refs, sems)
        for cp in c["own"] + c["first"]:
            cp.start()

    def middle(self, x_refs, o_refs, sems):
        c = self._copies(x_refs, o_refs, sems)
        for arrived, onward in zip(c["ici_in"], c["passed"]):
            arrived.wait_recv()
            onward.start()

    def finish(self, x_refs, o_refs, sems):
        c = self._copies(x_refs, o_refs, sems)
        for cp in c["first"] + c["passed"]:
            cp.wait_send()
        for cp in c["late_in"]:
            cp.wait_recv()
        for cp in c["own"]:
            cp.wait()


def _gather_now(xs, name):
    comm = _Gather(xs)

    def body(*refs):
        x_refs, o_refs, sems = refs[:comm.n], refs[comm.n:2 * comm.n], refs[2 * comm.n:]
        comm.start(x_refs, o_refs, sems)
        comm.middle(x_refs, o_refs, sems)
        comm.finish(x_refs, o_refs, sems)

    return pl.pallas_call(body, name=name, out_shape=tuple(comm.out_shape), in_specs=comm.specs,
                          out_specs=tuple(comm.specs), scratch_shapes=comm.scratch)(*comm.xs)


def _hosted(body, comm, *, name, grid, in_specs, out_specs, out_shape, scratch_shapes, compiler_params, inputs,
            input_output_aliases=None):
    out_specs, out_shape = tuple(out_specs), tuple(out_shape)
    aliases = input_output_aliases or {}
    if comm is None:
        outs = pl.pallas_call(body, name=name, grid=grid, in_specs=list(in_specs), out_specs=out_specs, out_shape=out_shape,
                              scratch_shapes=list(scratch_shapes), compiler_params=compiler_params,
                              input_output_aliases=aliases)(*inputs)
        return tuple(outs), ()
    ni, no, ns, nc = len(in_specs), len(out_specs), len(scratch_shapes), comm.n
    n_into = len(comm.into)
    aliases = {**aliases, **{ni + nc + a: no + a for a in range(n_into)}}

    def wrapped(*refs):
        ins, c_in = refs[:ni], refs[ni:ni + nc]
        refs = refs[ni + nc + n_into:]
        outs, c_out = refs[:no], refs[no:no + nc]
        scr, sems = refs[no + nc:no + nc + ns], refs[no + nc + ns:]
        first = functools.reduce(jnp.logical_and, [pl.program_id(d) == 0 for d in range(len(grid))])
        last = functools.reduce(jnp.logical_and, [pl.program_id(d) == grid[d] - 1 for d in range(len(grid))])

        step = functools.reduce(lambda acc, d: acc * grid[d] + pl.program_id(d), range(len(grid)), 0)
        n_steps = math.prod(grid)

        @pl.when(first)
        def _():
            comm.start(c_in, c_out, sems)

        body(*ins, *outs, *scr)

        @pl.when(step == max(7 * n_steps // 8, 1) - 1)
        def _():
            comm.middle(c_in, c_out, sems)

        @pl.when(last)
        def _():
            comm.finish(c_in, c_out, sems)

    params = pltpu.CompilerParams(dimension_semantics=("arbitrary",) * len(grid),
                                  vmem_limit_bytes=compiler_params.vmem_limit_bytes)
    outs = pl.pallas_call(wrapped, name=name, grid=grid, in_specs=list(in_specs) + comm.specs + comm.specs[:n_into],
                          out_specs=out_specs + tuple(comm.specs), out_shape=out_shape + tuple(comm.out_shape),
                          scratch_shapes=list(scratch_shapes) + comm.scratch, compiler_params=params,
                          input_output_aliases=aliases)(*inputs, *comm.xs, *comm.into)
    return tuple(outs[:no]), tuple(outs[no:])


def _live_pieces(dead, tile):
    lo, hi = dead
    t = lo // tile
    assert (hi - 1) // tile == t and lo % 128 == 0 and hi % 128 == 0
    return t, [(x, y) for x, y in ((0, lo - t * tile), (hi - t * tile, tile)) if y > x]


def _mm(a, b, *, ta=False, tb=False, out_dtype, tm, tn, tk, name, adds=(), vmem_mb=48, comm=None, dead_n=None, dead_k=None):
    m = a.shape[1] if ta else a.shape[0]
    k = a.shape[0] if ta else a.shape[1]
    n = b.shape[0] if tb else b.shape[1]
    assert k == (b.shape[1] if tb else b.shape[0])
    tm, tn, tk = min(tm, m), min(tn, n), min(tk, k)
    assert m % tm == 0 and n % tn == 0 and k % tk == 0, (name, m, n, k)
    nk = k // tk
    n_add = len(adds)
    scales = [s for _, s in adds]
    jd, cols_d = _live_pieces(dead_n, tn) if dead_n else (None, None)
    kd, ks_d = _live_pieces(dead_k, tk) if dead_k else (None, None)

    def body(a_ref, b_ref, *rest):
        add_refs, o_ref = rest[:n_add], rest[n_add]
        acc_ref = rest[n_add + 1] if nk > 1 else None
        jj, kk = pl.program_id(1), pl.program_id(2)

        def product(c0, c1, ks):
            p = None
            for k0, k1 in ks:
                a_blk = a_ref[k0:k1, :] if ta else a_ref[:, k0:k1]
                b_blk = b_ref[c0:c1, k0:k1] if tb else b_ref[k0:k1, c0:c1]
                q = _dot(a_blk.astype(BF16), b_blk.astype(BF16), 0 if ta else 1, 1 if tb else 0)
                p = q if p is None else p + q
            return p

        def step(mode, cols, ks):
            for c0, c1 in cols:
                p = product(c0, c1, ks)
                if mode == "first":
                    acc_ref[:, c0:c1] = p
                elif mode == "mid":
                    acc_ref[:, c0:c1] += p
                else:
                    r = p if mode == "only" else acc_ref[:, c0:c1] + p
                    for ref, s in zip(add_refs, scales):
                        r = r + s * ref[:, c0:c1].astype(F32)
                    o_ref[:, c0:c1] = r.astype(out_dtype)

        modes = [("only", None, range(1))] if nk == 1 else (
            [("first", kk == 0, range(1))] + ([("mid", (kk > 0) & (kk < nk - 1), range(1, nk - 1))] if nk > 2 else [])
            + [("last", kk == nk - 1, range(nk - 1, nk))])
        for mode, when_k, k_range in modes:
            k_cases = [(when_k, [(0, tk)])]
            if kd is not None and kd in k_range:
                is_dead = kk == kd
                k_cases = [(is_dead, ks_d)] + ([(when_k & ~is_dead, [(0, tk)])] if len(k_range) > 1 else [])
            for cond_k, ks in k_cases:
                n_cases = [(None, [(0, tn)])] if jd is None else [(jj == jd, cols_d), (jj != jd, [(0, tn)])]
                for cond_n, cols in n_cases:
                    conds = [c for c in (cond_k, cond_n) if c is not None]
                    run = functools.partial(step, mode, cols, ks)
                    pl.when(functools.reduce(jnp.logical_and, conds))(run) if conds else run()

    a_spec = pl.BlockSpec((tk, tm), lambda i, j, kk: (kk, i)) if ta else pl.BlockSpec((tm, tk), lambda i, j, kk: (i, kk))
    b_spec = pl.BlockSpec((tn, tk), lambda i, j, kk: (j, kk)) if tb else pl.BlockSpec((tk, tn), lambda i, j, kk: (kk, j))
    add_specs = [pl.BlockSpec((tm, tn), lambda i, j, kk: (i, j)) for _ in adds]
    (out,), c_out = _hosted(
        body, comm, name=name,
        out_shape=[jax.ShapeDtypeStruct((m, n), out_dtype)],
        grid=(m // tm, n // tn, nk),
        in_specs=[a_spec, b_spec] + add_specs,
        out_specs=[pl.BlockSpec((tm, tn), lambda i, j, kk: (i, j))],
        scratch_shapes=[pltpu.VMEM((tm, tn), F32)] if nk > 1 else [],
        compiler_params=_params(("parallel", "parallel", "arbitrary"), vmem_mb),
        inputs=(a, b, *[x for x, _ in adds]))
    return out, c_out


NKB = KB // QB
LEAD = NKB - 1


def _band_bias(table):
    i = np.arange(QB)[:, None]
    j = np.arange(KB)[None, :]
    qc = i // CH + LEAD * QB // CH
    kc = j // CH
    valid = (kc <= qc) & (kc >= qc - LEFT)
    n = QB + KB
    c = np.arange(n)
    onehot = np.zeros((2 * MAX_REL + 1, n), np.float32)
    onehot[np.clip(LEAD * QB - (c - (QB - 1)), -MAX_REL, MAX_REL) + MAX_REL, c] = 1.0
    row = jnp.dot(table.astype(F32), jnp.asarray(onehot), precision=lax.Precision.HIGHEST)
    flow = jnp.tile(row, (1, QB))[:, :QB * (n - 1)].reshape(table.shape[0], QB, n - 1)
    return jnp.where(valid[None], flow[:, :, QB - 1:] * LOG2E, NEG)


def _bias_grad(dbias):
    h, n = dbias.shape[0], KB + 1
    flat = jnp.pad(dbias.reshape(h, QB * KB), ((0, 0), (0, -(QB * KB) % n)))
    diag = flat.reshape(h, -1, n).sum(axis=1)
    c = np.arange(n)
    jm = np.where(c < LEAD * QB + CH, c, c - n)
    didx = np.clip(LEAD * QB - jm, -MAX_REL, MAX_REL) + MAX_REL
    onehot = np.zeros((n, 2 * MAX_REL + 1), np.float32)
    onehot[c, didx] = 1.0
    return jnp.dot(diag, jnp.asarray(onehot), precision=lax.Precision.HIGHEST)


def _attn_scores(q, k_refs, cs, bias_h, m, masked, scale):
    parts = []
    for t in range(NKB):
        sc = _dot(q, k_refs[t][:, cs], 1, 1) * scale
        parts.append(jnp.where(m + t - LEAD >= 0, sc, NEG) if masked else sc)
    return jnp.concatenate(parts, axis=1) + bias_h


def _attn_fwd(h, r_in, bias, comm=None):
    s = h.shape[0]
    nq = s // QB
    scale = A_DH ** -0.5

    def body(q_ref, *rest):
        k_refs, v_refs = rest[:NKB], rest[NKB:2 * NKB]
        bias_hbm, r_any, o_ref, linv_ref, p_ref, bias_ref = rest[2 * NKB:]
        del r_any
        m = pl.program_id(0)

        @pl.when(m == 0)
        def _():
            pltpu.sync_copy(bias_hbm, bias_ref)

        def scores(hd, masked):
            cs = slice(hd * A_DH, (hd + 1) * A_DH)
            return _attn_scores(q_ref[:, cs], k_refs, cs, bias_ref[hd], m, masked, scale * LOG2E)

        def step(masked):
            sc_next = scores(0, masked)
            for hd in range(A_HEADS):
                cs = slice(hd * A_DH, (hd + 1) * A_DH)
                sc = sc_next
                if hd + 1 < A_HEADS:
                    sc_next = scores(hd + 1, masked)
                mx = jnp.max(sc, axis=1, keepdims=True)
                p = jnp.exp2(sc - mx)
                l = jnp.sum(p, axis=1, keepdims=True)
                pb = p.astype(BF16)
                p_ref[hd] = pb
                o = _dot(pb[:, :QB], v_refs[0][:, cs], 1, 0)
                for t in range(1, NKB):
                    o += _dot(pb[:, t * QB:(t + 1) * QB], v_refs[t][:, cs], 1, 0)
                linv = 1.0 / l
                o_ref[:, cs] = (o * linv).astype(BF16)
                linv_ref[:, hd:hd + 1] = linv

        pl.when(m < LEAD)(functools.partial(step, True))
        pl.when(m >= LEAD)(functools.partial(step, False))

    def kv_spec(col, t):
        return pl.BlockSpec((QB, A_W), lambda m: (jnp.maximum(m + t - LEAD, 0), col))

    (r, linv, probs), c_out = _hosted(
        body, comm, name="attn_fwd",
        out_shape=(jax.ShapeDtypeStruct(r_in.shape, BF16), jax.ShapeDtypeStruct((s, A_HEADS), F32),
                   jax.ShapeDtypeStruct((nq, A_HEADS, QB, KB), BF16)),
        grid=(nq,),
        in_specs=[pl.BlockSpec((QB, A_W), lambda m: (m, 0))]
        + [kv_spec(1, t) for t in range(NKB)] + [kv_spec(2, t) for t in range(NKB)]
        + [pl.BlockSpec(memory_space=pl.ANY), pl.BlockSpec(memory_space=pl.ANY)],
        out_specs=(pl.BlockSpec((QB, A_W), lambda m: (m, 0)), pl.BlockSpec((QB, A_HEADS), lambda m: (m, 0)),
                   pl.BlockSpec((None, A_HEADS, QB, KB), lambda m: (m, 0, 0, 0))),
        scratch_shapes=[pltpu.VMEM((A_HEADS, QB, KB), F32)],
        input_output_aliases={2 * NKB + 2: 0},
        compiler_params=_params(("arbitrary",)),
        inputs=(h,) * (2 * NKB + 1) + (bias, r_in))
    return r, linv, probs, c_out


def _attn_bwd(h, r, dy, linv, probs, dh_in, comm=None):
    s = h.shape[0]
    nq = s // QB
    scale = A_DH ** -0.5

    def body(q_ref, *rest):
        k_refs, v_refs = rest[:NKB], rest[NKB:2 * NKB]
        (z_ref, r_ref, dy_ref, linv_ref, p_ref, zl_ref, rl_ref, dyl_ref, dh_any,
         dh_ref, dbias_hbm, dk_acc, dv_acc, dq_ring, dbias_ref) = rest[2 * NKB:]
        del dh_any
        m = pl.program_id(0)

        def slot(b):
            return pl.multiple_of(lax.rem(b + NKB, NKB) * QB, QB)

        @pl.when(m == 0)
        def _():
            dk_acc[...] = jnp.zeros_like(dk_acc)
            dv_acc[...] = jnp.zeros_like(dv_acc)
            dq_ring[...] = jnp.zeros_like(dq_ring)
            dbias_ref[...] = jnp.zeros_like(dbias_ref)

        @pl.when(m < nq)
        def _():
            z = z_ref[...].astype(F32)
            do_all = dy_ref[...].astype(F32) * (z * _sigmoid(z))
            o_all = r_ref[...].astype(F32)

            def product(hd):
                cs = slice(hd * A_DH, (hd + 1) * A_DH)
                dob = do_all[:, cs].astype(BF16)
                return jnp.concatenate([_dot(dob, v_refs[t][:, cs], 1, 1) for t in range(NKB)], axis=1)

            ahead = product(0)
            for hd in range(A_HEADS):
                cs = slice(hd * A_DH, (hd + 1) * A_DH)
                q = q_ref[:, cs]
                do = do_all[:, cs]
                dob = do.astype(BF16)
                delta = jnp.sum(do * o_all[:, cs], axis=1, keepdims=True)
                dp = ahead
                if hd + 1 < A_HEADS:
                    ahead = product(hd + 1)
                p = p_ref[hd].astype(F32) * linv_ref[:, hd:hd + 1]
                ds = p * (dp - delta)
                dbias_ref[hd] += ds
                pb, dsb = p.astype(BF16), ds.astype(BF16)
                dq = jnp.zeros((QB, A_DH), F32)
                for t in range(NKB):
                    ts = slice(t * QB, (t + 1) * QB)
                    rows = pl.ds(slot(m - LEAD + t), QB)
                    dq += _dot(dsb[:, ts], k_refs[t][:, cs], 1, 0)
                    dk_acc[rows, cs] += _dot(dsb[:, ts], q, 0, 0) * scale
                    dv_acc[rows, cs] += _dot(pb[:, ts], dob, 0, 0)
                dq_ring[pl.ds(slot(m), QB), cs] = dq * scale

        done = pl.ds(slot(m - LEAD), QB)
        zl = zl_ref[...].astype(F32)
        sg = _sigmoid(zl)
        dz = dyl_ref[...].astype(F32) * rl_ref[...].astype(F32) * (sg * (1.0 + zl * (1.0 - sg)))
        dh_ref[:, A_Q:A_Q + A_W] = dq_ring[done, :].astype(BF16)
        dh_ref[:, A_K:A_K + A_W] = dk_acc[done, :].astype(BF16)
        dh_ref[:, A_V:A_V + A_W] = dv_acc[done, :].astype(BF16)
        dh_ref[:, A_Z:A_Z + A_W] = dz.astype(BF16)
        dk_acc[done, :] = jnp.zeros((QB, A_W), F32)
        dv_acc[done, :] = jnp.zeros((QB, A_W), F32)

        @pl.when(m == nq + LEAD - 1)
        def _():
            pltpu.sync_copy(dbias_ref, dbias_hbm)

    last = nq - 1

    def cur(col):
        return pl.BlockSpec((QB, A_W), lambda m: (jnp.minimum(m, last), col))

    def kv_spec(col, t):
        return pl.BlockSpec((QB, A_W), lambda m: (jnp.clip(m + t - LEAD, 0, last), col))

    def lag(col):
        return pl.BlockSpec((QB, A_W), lambda m: (jnp.clip(m - LEAD, 0, last), col))

    (dh, dbias), c_out = _hosted(
        body, comm, name="attn_bwd",
        out_shape=(jax.ShapeDtypeStruct(dh_in.shape, BF16), jax.ShapeDtypeStruct((A_HEADS, QB, KB), F32)),
        grid=(nq + LEAD,),
        in_specs=[cur(0)] + [kv_spec(1, t) for t in range(NKB)] + [kv_spec(2, t) for t in range(NKB)]
        + [cur(3), cur(0), cur(0), pl.BlockSpec((QB, A_HEADS), lambda m: (jnp.minimum(m, last), 0)),
           pl.BlockSpec((None, A_HEADS, QB, KB), lambda m: (jnp.minimum(m, last), 0, 0, 0)),
           lag(3), lag(0), lag(0), pl.BlockSpec(memory_space=pl.ANY)],
        out_specs=(pl.BlockSpec((QB, 4 * A_W), lambda m: (jnp.clip(m - LEAD, 0, last), 0)),
                   pl.BlockSpec(memory_space=pl.ANY)),
        scratch_shapes=[pltpu.VMEM((KB, A_W), F32), pltpu.VMEM((KB, A_W), F32), pltpu.VMEM((KB, A_W), F32),
                        pltpu.VMEM((A_HEADS, QB, KB), F32)],
        input_output_aliases={2 * NKB + 9: 0},
        compiler_params=_params(("arbitrary",), 56),
        inputs=(h,) * (2 * NKB + 2) + (r, dy, linv, probs, h, r, dy, dh_in))
    return dh, dbias, c_out


GB = 256
N_PAIR = B_HEADS // 2


def _gla_gates(lr, gw_ref, gb_ref):
    logit = _dot(lr, gw_ref[...], 1, 0) + gb_ref[...]
    lg = (jnp.minimum(logit, 0.0) - jnp.log(1.0 + jnp.exp(-jnp.abs(logit)))) / GATE_TAU
    row = lax.broadcasted_iota(jnp.int32, (GB, GB), 0)
    col = lax.broadcasted_iota(jnp.int32, (GB, GB), 1)
    tri = jnp.where((row // CH == col // CH) & (col <= row), 1.0, 0.0).astype(F32)
    return logit, _dot(tri, lg, 1, 0, precision=lax.Precision.HIGHEST)


def _gla_factors(hb_ref, b_all, c):
    rs = slice(c * CH, (c + 1) * CH)
    q = hb_ref[rs, 0:B_KW].astype(F32) * (B_DK ** -0.5)
    k = hb_ref[rs, B_KW:2 * B_KW].astype(F32)
    b = b_all[rs]
    bm, bl = b[CH // 2:CH // 2 + 1, :], b[CH - 1:CH, :]
    e1, e2, eb, ek = jnp.exp(b - bm), jnp.exp(bm - b), jnp.exp(b), jnp.exp(bl - b)
    el = jnp.exp(bl)
    return dict(ql=q * e1, kl=k * e2, qu=q * e2, ku=k * e1, qt=q * eb, kh=k * ek, e1=e1, e2=e2, eb=eb, ek=ek, el=el)


class _GlaPairTools:
    def __init__(self, hb_ref, fs):
        self.hb_ref, self.fs, self.made = hb_ref, fs, {}
        row = lax.broadcasted_iota(jnp.int32, (2 * CH, 2 * CH), 0)
        col = lax.broadcasted_iota(jnp.int32, (2 * CH, 2 * CH), 1)
        self.same = (row // CH) == (col // CH)
        self.lower = self.same & ((row % CH) >= (col % CH))
        self.upper = self.same & ((row % CH) < (col % CH))
        self.lower_t = self.same & ((col % CH) >= (row % CH))
        self.upper_t = self.same & ((col % CH) < (row % CH))

    def _once(self, key, make):
        if key not in self.made:
            self.made[key] = make()
        return self.made[key]

    def lanes(self, c, p, name):
        return self.fs[c][name][:, p * 128:(p + 1) * 128]

    def heads(self, c, p, name):
        def make():
            x = self.lanes(c, p, name)
            return jnp.where(self.same, jnp.concatenate([x, x], axis=0), 0.0).astype(BF16)
        return self._once(("heads", c, p, name), make)

    def twice(self, c, p, name):
        def make():
            x = self.lanes(c, p, name).astype(BF16)
            return jnp.concatenate([x, x], axis=0)
        return self._once(("twice", c, p, name), make)

    def vals(self, c, p):
        rows = slice(c * CH, (c + 1) * CH)
        return self._once(("vals", c, p), lambda: jnp.concatenate(
            [self.hb_ref[rows, 512 + (2 * p + sh) * B_DV:512 + (2 * p + sh + 1) * B_DV] for sh in range(2)], axis=0))

    def intra(self, c, p):
        lo = _dot(self.heads(c, p, "ql"), self.twice(c, p, "kl"), 1, 1)
        up = _dot(self.heads(c, p, "qu"), self.twice(c, p, "ku"), 1, 1)
        return jnp.where(self.lower, lo, jnp.where(self.upper, up, 0.0)).astype(BF16)

    def fold(self, x):
        x = jnp.where(self.same, x, 0.0)
        return x[:CH] + x[CH:]


def _gla_fwd(h, r_in, gw, gb, ng):
    s = h.shape[0]
    nb = s // GB
    cpb = GB // CH

    def body(hb_ref, gw_ref, gb_ref, ng_ref, r_any, o_ref, opre_ref, st_ref, state):
        del r_any

        @pl.when(pl.program_id(0) == 0)
        def _():
            state[...] = jnp.zeros_like(state)

        _, b_all = _gla_gates(hb_ref[:, 1536:1664], gw_ref, gb_ref)
        fs = [_gla_factors(hb_ref, b_all, c) for c in range(cpb)]
        pairs = [(c, p) for c in range(cpb) for p in range(N_PAIR)]
        tools = _GlaPairTools(hb_ref, fs)
        a = {k: tools.intra(*k) for k in pairs}
        o_intra = {k: _dot(a[k], tools.vals(*k), 1, 0) for k in pairs}
        upd = {k: _dot(tools.vals(*k), tools.heads(*k, "kh"), 0, 0) for k in pairs}
        st = [state[p] for p in range(N_PAIR)]
        entering = {}
        for c, p in pairs:
            entering[c, p] = st[p]
            st_ref[c, p] = st[p]
            st[p] = st[p] * fs[c]["el"][:, p * 128:(p + 1) * 128] + upd[c, p]
        for p in range(N_PAIR):
            state[p] = st[p]
        for c, p in pairs:
            o2 = o_intra[c, p] + _dot(tools.heads(c, p, "qt"), entering[c, p].astype(BF16), 1, 1)
            for sh in range(2):
                o = o2[sh * CH:(sh + 1) * CH]
                rs, hs = slice(c * CH, (c + 1) * CH), slice((2 * p + sh) * B_DV, (2 * p + sh + 1) * B_DV)
                opre_ref[rs, hs] = o
                rinv = lax.rsqrt(jnp.mean(o * o, axis=1, keepdims=True) + RMS_EPS)
                o_ref[rs, hs] = (o * rinv * ng_ref[...]).astype(BF16)

    (r, opre, states), _ = _hosted(
        body, None, name="gla_fwd",
        out_shape=(jax.ShapeDtypeStruct(r_in.shape, BF16), jax.ShapeDtypeStruct((s, B_W), F32),
                   jax.ShapeDtypeStruct((s // CH, N_PAIR, 128, 128), F32)),
        grid=(nb,),
        in_specs=[pl.BlockSpec((GB, 2048), lambda i: (i, B_BASE // 2048)),
                  pl.BlockSpec((128, B_KW), lambda i: (0, 0)), pl.BlockSpec((1, B_KW), lambda i: (0, 0)),
                  pl.BlockSpec((1, B_DV), lambda i: (0, 0)), pl.BlockSpec(memory_space=pl.ANY)],
        out_specs=(pl.BlockSpec((GB, B_W), lambda i: (i, 1024 // B_W)), pl.BlockSpec((GB, B_W), lambda i: (i, 0)),
                   pl.BlockSpec((cpb, N_PAIR, 128, 128), lambda i: (i, 0, 0, 0))),
        scratch_shapes=[pltpu.VMEM((N_PAIR, 128, 128), F32)],
        input_output_aliases={4: 0},
        compiler_params=_params(("arbitrary",)),
        inputs=(h, gw, gb, ng, r_in))
    return r, opre, states


def _gla_bwd(h, dy, opre, states, gw, gb, ng, dh_in):
    s = h.shape[0]
    nb = s // GB
    cpb = GB // CH

    def body(hb_ref, dy_ref, opre_ref, st_ref, gw_ref, gb_ref, ng_ref, dh_any,
             dh_ref, dgw_ref, dgb_ref, dng_ref, dstate, db_scr, do_scr):
        del dh_any

        @pl.when(pl.program_id(0) == 0)
        def _():
            dstate[...] = jnp.zeros_like(dstate)
            dgw_ref[...] = jnp.zeros_like(dgw_ref)
            dgb_ref[...] = jnp.zeros_like(dgb_ref)
            dng_ref[...] = jnp.zeros_like(dng_ref)

        lr = hb_ref[:, 1536:1664]
        logit, b_all = _gla_gates(lr, gw_ref, gb_ref)
        z = hb_ref[:, 1024:1536].astype(F32)
        sg = _sigmoid(z)
        dyb = dy_ref[...].astype(F32)
        dng = jnp.zeros((1, B_DV), F32)
        for hd in range(B_HEADS):
            hs = slice(hd * B_DV, (hd + 1) * B_DV)
            o = opre_ref[:, hs]
            rinv = lax.rsqrt(jnp.mean(o * o, axis=1, keepdims=True) + RMS_EPS)
            on = o * rinv
            dr = dyb[:, hs] * (z[:, hs] * sg[:, hs])
            dh_ref[:, 1024 + hd * B_DV:1024 + (hd + 1) * B_DV] = (
                dyb[:, hs] * (on * ng_ref[...]) * (sg[:, hs] * (1.0 + z[:, hs] * (1.0 - sg[:, hs])))).astype(BF16)
            dng += jnp.sum(dr * on, axis=0, keepdims=True)
            dn = dr * ng_ref[...]
            do_scr[:, hs] = rinv * (dn - on * jnp.mean(dn * on, axis=1, keepdims=True))
        dng_ref[...] += dng

        rowi = lax.broadcasted_iota(jnp.int32, (CH, 128), 0)
        fs = [_gla_factors(hb_ref, b_all, c) for c in range(cpb)]
        pairs = [(c, p) for c in reversed(range(cpb)) for p in range(N_PAIR)]
        tools = _GlaPairTools(hb_ref, fs)
        pair = tools.lanes

        def douts(c, p):
            rows = slice(c * CH, (c + 1) * CH)
            return tools._once(("douts", c, p), lambda: jnp.concatenate(
                [do_scr[rows, (2 * p + sh) * B_DV:(2 * p + sh + 1) * B_DV] for sh in range(2)], axis=0).astype(BF16))

        at, da, dat, dst_own, g_qt = {}, {}, {}, {}, {}
        for k in pairs:
            c, p = k
            lo_t = _dot(tools.twice(*k, "kl"), tools.heads(*k, "ql"), 1, 1)
            up_t = _dot(tools.twice(*k, "ku"), tools.heads(*k, "qu"), 1, 1)
            at[k] = jnp.where(tools.lower_t, lo_t, jnp.where(tools.upper_t, up_t, 0.0)).astype(BF16)
            da[k] = _dot(douts(*k), tools.vals(*k), 1, 1)
            dat[k] = _dot(tools.vals(*k), douts(*k), 1, 1)
            dst_own[k] = _dot(douts(*k), tools.heads(*k, "qt"), 0, 0)
            g_qt[k] = tools.fold(_dot(douts(*k), st_ref[c, p].astype(BF16), 1, 0))
        dv, g_ql, g_qu, g_kl, g_ku = {}, {}, {}, {}, {}
        for k in pairs:
            dv[k] = _dot(at[k], douts(*k), 1, 0)
            g_ql[k] = tools.fold(_dot(jnp.where(tools.lower, da[k], 0.0).astype(BF16), tools.twice(*k, "kl"), 1, 0))
            g_qu[k] = tools.fold(_dot(jnp.where(tools.upper, da[k], 0.0).astype(BF16), tools.twice(*k, "ku"), 1, 0))
            g_kl[k] = tools.fold(_dot(jnp.where(tools.lower_t, dat[k], 0.0).astype(BF16), tools.heads(*k, "ql"), 1, 0))
            g_ku[k] = tools.fold(_dot(jnp.where(tools.upper_t, dat[k], 0.0).astype(BF16), tools.heads(*k, "qu"), 1, 0))
        dst = [dstate[p] for p in range(N_PAIR)]
        leaving = {}
        for c, p in pairs:
            leaving[c, p] = dst[p]
            dst[p] = dst[p] * pair(c, p, "el") + dst_own[c, p]
        for p in range(N_PAIR):
            dstate[p] = dst[p]
        g_kh = {}
        for k in pairs:
            c, p = k
            dstb = leaving[k].astype(BF16)
            dv2 = dv[k] + _dot(tools.heads(*k, "kh"), dstb, 1, 1)
            for sh in range(2):
                hd = 2 * p + sh
                dh_ref[c * CH:(c + 1) * CH, 512 + hd * B_DV:512 + (hd + 1) * B_DV] = dv2[sh * CH:(sh + 1) * CH].astype(BF16)
            g_kh[k] = tools.fold(_dot(tools.vals(*k), dstb, 1, 0))
        for c in reversed(range(cpb)):
            rs = slice(c * CH, (c + 1) * CH)
            for p in range(N_PAIR):
                k = (c, p)
                dq = (g_ql[k] * pair(c, p, "e1") + g_qu[k] * pair(c, p, "e2") + g_qt[k] * pair(c, p, "eb")) * (B_DK ** -0.5)
                dk = g_kl[k] * pair(c, p, "e2") + g_ku[k] * pair(c, p, "e1") + g_kh[k] * pair(c, p, "ek")
                dkh_kh = g_kh[k] * pair(c, p, "kh")
                db = (g_ql[k] * pair(c, p, "ql") - g_qu[k] * pair(c, p, "qu") + g_qt[k] * pair(c, p, "qt")
                      - g_kl[k] * pair(c, p, "kl") + g_ku[k] * pair(c, p, "ku") - dkh_kh)
                db_last = (pair(c, p, "el") * jnp.sum(leaving[c, p] * st_ref[c, p], axis=0, keepdims=True)
                           + jnp.sum(dkh_kh, axis=0, keepdims=True))
                db = jnp.where(rowi == CH - 1, db + db_last, db)
                dh_ref[rs, p * 128:(p + 1) * 128] = dq.astype(BF16)
                dh_ref[rs, B_KW + p * 128:B_KW + (p + 1) * 128] = dk.astype(BF16)
                db_scr[rs, p * 128:(p + 1) * 128] = db

        row = lax.broadcasted_iota(jnp.int32, (GB, GB), 0)
        col = lax.broadcasted_iota(jnp.int32, (GB, GB), 1)
        trit = jnp.where((row // CH == col // CH) & (col >= row), 1.0, 0.0).astype(F32)
        dlg = _dot(trit, db_scr[...], 1, 0, precision=lax.Precision.HIGHEST)
        dlogit = dlg * (_sigmoid(-logit) / GATE_TAU)
        dlb = dlogit.astype(BF16)
        dgw_ref[...] += _dot(lr, dlb, 0, 0)
        dgb_ref[...] += jnp.sum(dlogit, axis=0, keepdims=True)
        dh_ref[:, 1536:1664] = _dot(dlb, gw_ref[...], 1, 1).astype(BF16)
        dh_ref[:, 1664:2048] = jnp.zeros((GB, 384), BF16)

    rev = lambda i: nb - 1 - i
    (dh, dgw, dgb, dng), _ = _hosted(
        body, None, name="gla_bwd",
        out_shape=(jax.ShapeDtypeStruct(dh_in.shape, BF16), jax.ShapeDtypeStruct((128, B_KW), F32),
                   jax.ShapeDtypeStruct((1, B_KW), F32), jax.ShapeDtypeStruct((1, B_DV), F32)),
        grid=(nb,),
        in_specs=[pl.BlockSpec((GB, 2048), lambda i: (rev(i), B_BASE // 2048)),
                  pl.BlockSpec((GB, B_W), lambda i: (rev(i), 1024 // B_W)),
                  pl.BlockSpec((GB, B_W), lambda i: (rev(i), 0)),
                  pl.BlockSpec((cpb, N_PAIR, 128, 128), lambda i: (rev(i), 0, 0, 0)),
                  pl.BlockSpec((128, B_KW), lambda i: (0, 0)), pl.BlockSpec((1, B_KW), lambda i: (0, 0)),
                  pl.BlockSpec((1, B_DV), lambda i: (0, 0)), pl.BlockSpec(memory_space=pl.ANY)],
        out_specs=(pl.BlockSpec((GB, 2048), lambda i: (rev(i), B_BASE // 2048)),
                   pl.BlockSpec((128, B_KW), lambda i: (0, 0)), pl.BlockSpec((1, B_KW), lambda i: (0, 0)),
                   pl.BlockSpec((1, B_DV), lambda i: (0, 0))),
        scratch_shapes=[pltpu.VMEM((N_PAIR, 128, 128), F32), pltpu.VMEM((GB, B_KW), F32), pltpu.VMEM((GB, B_W), F32)],
        input_output_aliases={7: 0},
        compiler_params=_params(("arbitrary",)),
        inputs=(h, dy, opre, states, gw, gb, ng, dh_in))
    return dh, dgw, dgb, dng


MB = 512


def _mem_probs(q, mk, scale):
    sc = _dot(q, mk, 1, 1) * (scale * LOG2E)
    p = jnp.exp2(sc - jnp.max(sc, axis=1, keepdims=True))
    return p / jnp.sum(p, axis=1, keepdims=True)


def _mem_fwd(h, r_in, mkv):
    s = h.shape[0]
    scale = M_DH ** -0.5

    def body(q_ref, mkv_ref, r_any, o_ref):
        del r_any
        for hd in range(M_HEADS):
            cs = slice(hd * M_DH, (hd + 1) * M_DH)
            p = _mem_probs(q_ref[:, cs], mkv_ref[:, cs], scale)
            o_ref[:, cs] = _dot(p.astype(BF16), mkv_ref[:, M_W + hd * M_DH:M_W + (hd + 1) * M_DH], 1, 0).astype(BF16)

    return pl.pallas_call(
        body, name="mem_fwd",
        out_shape=jax.ShapeDtypeStruct(r_in.shape, BF16),
        grid=(s // MB,),
        in_specs=[pl.BlockSpec((MB, M_W), lambda i: (i, M_BASE // M_W)),
                  pl.BlockSpec((N_MEM, 2 * M_W), lambda i: (0, 0)), pl.BlockSpec(memory_space=pl.ANY)],
        out_specs=pl.BlockSpec((MB, M_W), lambda i: (i, 1536 // M_W)),
        input_output_aliases={2: 0},
        compiler_params=_params(("arbitrary",)),
    )(h, mkv, r_in)


def _mem_bwd(h, r, dy, mkv, dh_in):
    s = h.shape[0]
    scale = M_DH ** -0.5

    def body(q_ref, z_ref, r_ref, dy_ref, mkv_ref, dh_any, dh_ref, dmkv_ref):
        del dh_any

        @pl.when(pl.program_id(0) == 0)
        def _():
            dmkv_ref[...] = jnp.zeros_like(dmkv_ref)

        z = z_ref[...].astype(F32)
        sg = _sigmoid(z)
        dyv = dy_ref[...].astype(F32)
        do_all = dyv * (z * sg)
        dh_ref[:, M_W:2 * M_W] = (dyv * r_ref[...].astype(F32) * (sg * (1.0 + z * (1.0 - sg)))).astype(BF16)
        for hd in range(M_HEADS):
            cs = slice(hd * M_DH, (hd + 1) * M_DH)
            vs = slice(M_W + hd * M_DH, M_W + (hd + 1) * M_DH)
            q = q_ref[:, cs]
            p = _mem_probs(q, mkv_ref[:, cs], scale)
            dob = do_all[:, cs].astype(BF16)
            dp = _dot(dob, mkv_ref[:, vs], 1, 1)
            ds = p * (dp - jnp.sum(p * dp, axis=1, keepdims=True))
            dsb = ds.astype(BF16)
            dh_ref[:, cs] = (_dot(dsb, mkv_ref[:, cs], 1, 0) * scale).astype(BF16)
            dmkv_ref[:, cs] += _dot(dsb, q, 0, 0) * scale
            dmkv_ref[:, vs] += _dot(p.astype(BF16), dob, 0, 0)

    return pl.pallas_call(
        body, name="mem_bwd",
        out_shape=(jax.ShapeDtypeStruct(dh_in.shape, BF16), jax.ShapeDtypeStruct((N_MEM, 2 * M_W), F32)),
        grid=(s // MB,),
        in_specs=[pl.BlockSpec((MB, M_W), lambda i: (i, M_BASE // M_W)),
                  pl.BlockSpec((MB, M_W), lambda i: (i, M_BASE // M_W + 1)),
                  pl.BlockSpec((MB, M_W), lambda i: (i, 1536 // M_W)),
                  pl.BlockSpec((MB, M_W), lambda i: (i, 1536 // M_W)),
                  pl.BlockSpec((N_MEM, 2 * M_W), lambda i: (0, 0)), pl.BlockSpec(memory_space=pl.ANY)],
        out_specs=(pl.BlockSpec((MB, 2 * M_W), lambda i: (i, M_BASE // (2 * M_W))),
                   pl.BlockSpec((N_MEM, 2 * M_W), lambda i: (0, 0))),
        input_output_aliases={5: 0},
        compiler_params=_params(("arbitrary",)),
    )(h, h, r, dy, mkv, dh_in)


OB = 256


def _outproj_ln(h, r, w_out, x, ln_g, ln_b, target=None):
    s = h.shape[0]
    last = target is not None

    def body(za_ref, zb_ref, zm_ref, r_ref, w_ref, x_ref, g_ref, b_ref, *rest):
        if last:
            t_ref, xn_ref, xh_ref, rstd_ref, y_ref, l_ref = rest
        else:
            xn_ref, xh_ref, rstd_ref, y_ref = rest
        z = jnp.concatenate([za_ref[...], zb_ref[...], zm_ref[...]], axis=1).astype(F32)
        y = (r_ref[...].astype(F32) * (z * _sigmoid(z))).astype(BF16)
        y_ref[...] = y
        u = ALPHA * x_ref[...] + _dot(y, w_ref[...], 1, 0)
        mu = jnp.mean(u, axis=1, keepdims=True)
        uc = u - mu
        rstd = lax.rsqrt(jnp.mean(uc * uc, axis=1, keepdims=True) + LN_EPS)
        xh = uc * rstd
        xh_ref[...] = xh
        rstd_ref[...] = rstd
        xn = xh * g_ref[...] + b_ref[...]
        if last:
            @pl.when(pl.program_id(0) == 0)
            def _():
                l_ref[...] = jnp.zeros_like(l_ref)

            e = xn - t_ref[...]
            xn_ref[...] = e / D
            l_ref[...] += 0.5 * jnp.sum(jnp.mean(e * e, axis=1, keepdims=True))
        else:
            xn_ref[...] = xn

    row = lambda w, c: pl.BlockSpec((OB, w), lambda i: (i, c))
    vec = pl.BlockSpec((1, D), lambda i: (0, 0))
    full = jax.ShapeDtypeStruct((s, D), F32)
    return pl.pallas_call(
        body, name="outproj_ln_loss" if last else "outproj_ln",
        out_shape=(full, full, jax.ShapeDtypeStruct((s, 1), F32), jax.ShapeDtypeStruct((s, D), BF16))
        + ((jax.ShapeDtypeStruct((1, 128), F32),) if last else ()),
        grid=(s // OB,),
        in_specs=[row(A_W, A_Z // A_W), row(B_W, (B_BASE + 1024) // B_W), row(M_W, (M_BASE + M_W) // M_W), row(D, 0),
                  pl.BlockSpec((D, D), lambda i: (0, 0)), row(D, 0), vec, vec] + ([row(D, 0)] if last else []),
        out_specs=(row(D, 0), row(D, 0), pl.BlockSpec((OB, 1), lambda i: (i, 0)), row(D, 0))
        + ((pl.BlockSpec((1, 128), lambda i: (0, 0)),) if last else ()),
        compiler_params=_params(("arbitrary",), 56),
    )(h, h, h, r, w_out, x, ln_g, ln_b, *([target] if last else []))


def _ln_bwd_dy(g, xh, rstd, ln_g, w_out, y):
    s = g.shape[0]
    n = s // OB

    def body(g_ref, xh_ref, rstd_ref, lg_ref, w_ref, y_ref, dy_ref, du_ref, dg_ref, db_ref, dw_hbm, dw_acc, stage):
        i = pl.program_id(0)

        @pl.when(i == 0)
        def _():
            dg_ref[...] = jnp.zeros_like(dg_ref)
            db_ref[...] = jnp.zeros_like(db_ref)
            dw_acc[...] = jnp.zeros_like(dw_acc)

        gv, xh = g_ref[...], xh_ref[...]
        dg_ref[...] += jnp.sum(gv * xh, axis=0, keepdims=True)
        db_ref[...] += jnp.sum(gv, axis=0, keepdims=True)
        dxh = gv * lg_ref[...]
        du = rstd_ref[...] * (dxh - jnp.mean(dxh, axis=1, keepdims=True) - xh * jnp.mean(dxh * xh, axis=1, keepdims=True))
        du_ref[...] = du
        dub = du.astype(BF16)
        dy_ref[...] = _dot(dub, w_ref[...], 1, 1).astype(BF16)
        dw_acc[...] += _dot(y_ref[...], dub, 0, 0)

        @pl.when(i == n - 1)
        def _():
            for c in range(D // OB):
                stage[...] = dw_acc[c * OB:(c + 1) * OB, :].astype(BF16)
                pltpu.sync_copy(stage, dw_hbm.at[c * OB:(c + 1) * OB, :])

    row = pl.BlockSpec((OB, D), lambda i: (i, 0))
    vec = pl.BlockSpec((1, D), lambda i: (0, 0))
    return pl.pallas_call(
        body, name="ln_bwd_dy",
        out_shape=(jax.ShapeDtypeStruct((s, D), BF16), jax.ShapeDtypeStruct((s, D), F32),
                   jax.ShapeDtypeStruct((1, D), F32), jax.ShapeDtypeStruct((1, D), F32), jax.ShapeDtypeStruct((D, D), BF16)),
        grid=(n,),
        in_specs=[row, row, pl.BlockSpec((OB, 1), lambda i: (i, 0)), vec, pl.BlockSpec((D, D), lambda i: (0, 0)), row],
        out_specs=(row, row, vec, vec, pl.BlockSpec(memory_space=pl.ANY)),
        scratch_shapes=[pltpu.VMEM((D, D), F32), pltpu.VMEM((OB, D), BF16)],
        compiler_params=_params(("arbitrary",), 60),
    )(g, xh, rstd, ln_g, w_out, y)


class _LocalWeights:
    def __init__(self, w_in_p, w_out_f, w_kv_f, gate_w):
        self.w, self.gate = list(zip(w_in_p, w_out_f, w_kv_f)), gate_w
        self.depth = len(self.w)
        self.grads = [dict() for _ in self.w]

    def w_in(self, l):
        return self.w[l][0]

    def w_rest(self, l):
        return self.w[l][1:]

    def gate_w(self, l):
        return self.gate[l]

    def host(self, where, l, payload=None):
        if payload is not None:
            self.grads[l][where] = payload
        return None

    def landed(self, where, l, outs):
        pass


def _shard_pieces():
    sh = IN_W // N_DEV
    pieces = []
    for j in range(N_DEV):
        lo, hi = j * sh, (j + 1) * sh
        cuts = [lo, NAT_SPLIT, hi] if lo < NAT_SPLIT < hi else [lo, hi]
        for a, b in zip(cuts[:-1], cuts[1:]):
            pieces.append((j, a - lo, a if a < NAT_SPLIT else a + H_PAD, b - a))
    return pieces


RB = 256


def _shards_to_padded(raw):
    def body(x_ref, o_ref):
        for j, src, dst, width in _shard_pieces():
            o_ref[:, dst:dst + width] = x_ref[j, :, src:src + width]
        o_ref[:, NAT_SPLIT:NAT_SPLIT + H_PAD] = jnp.zeros((RB, H_PAD), o_ref.dtype)

    return pl.pallas_call(
        body, name="place_w_in", out_shape=jax.ShapeDtypeStruct((D, H_W), raw.dtype), grid=(D // RB,),
        in_specs=[pl.BlockSpec((N_DEV, RB, raw.shape[2]), lambda i: (0, i, 0))],
        out_specs=pl.BlockSpec((RB, H_W), lambda i: (i, 0)),
        compiler_params=_params(("parallel",)),
    )(raw)


def _padded_to_shards(w):
    sh = IN_W // N_DEV

    def body(x_ref, o_ref):
        for j, src, dst, width in _shard_pieces():
            o_ref[j, :, src:src + width] = x_ref[:, dst:dst + width]

    return pl.pallas_call(
        body, name="split_d_w_in", out_shape=jax.ShapeDtypeStruct((N_DEV, D, sh), w.dtype), grid=(D // RB,),
        in_specs=[pl.BlockSpec((RB, H_W), lambda i: (i, 0))],
        out_specs=pl.BlockSpec((N_DEV, RB, sh), lambda i: (0, i, 0)),
        compiler_params=_params(("parallel",)),
    )(w)


_PART_A = (0, 1, 2, 4, 7)
_PART_B = (3, 5, 6)


class _Fsdp:
    def __init__(self, w_in, w_out, w_kv, gate_w):
        self.sh = (w_in, w_out, w_kv)
        self.gate_sh = gate_w
        self.depth = w_in.shape[0]
        self.raw = [dict() for _ in range(self.depth)]
        self.recv = [dict() for _ in range(self.depth)]
        (self.raw[0]["w_in"],) = _gather_now([w_in[0]], "gather_layer0")

    def w_in(self, l):
        return _shards_to_padded(self.raw[l]["w_in"])

    def w_rest(self, l):
        return self.raw[l]["w_out"].reshape(D, D), self.raw[l]["w_kv"].reshape(D, 2 * M_W)

    def gate_w(self, l):
        return jnp.transpose(self.gate_all[:, l], (1, 0, 2)).reshape(GATE_RANK, -1)

    def host(self, where, l, payload=None):
        w_in, w_out, w_kv = self.sh
        if where == "in_proj":
            xs = [w_out[0], w_kv[0], self.gate_sh] if l == 0 else []
            if l + 1 < self.depth:
                xs += [w_in[l + 1]] if l == 0 else [w_in[l + 1], w_out[l + 1], w_kv[l + 1]]
            return _Gather(xs) if xs else None
        if where == "attn_fwd" and l == 0 and self.depth > 1:
            return _Gather([w_out[1], w_kv[1]])
        if where == "d_w_in":
            d_wout, d_wkv = payload
            return _Scatter([d_wout.reshape(N_DEV, D // N_DEV, D), d_wkv.reshape(N_DEV, D // N_DEV, 2 * M_W)])
        if where == "d_x":
            self.blocks = _padded_to_shards(payload)
            return _Scatter([self.blocks], relations=_PART_A if l > 0 else tuple(range(N_DEV)))
        if where == "attn_bwd" and l + 1 < self.depth:
            return _Scatter([self.blocks], relations=_PART_B, into=[self.recv[l + 1]["w_in"]])
        return None

    def landed(self, where, l, outs):
        if where == "in_proj" and outs:
            outs = list(outs)
            if l == 0:
                self.raw[0]["w_out"], self.raw[0]["w_kv"], self.gate_all = outs[:3]
                outs = outs[3:]
            if outs:
                self.raw[l + 1].update(zip(("w_in", "w_out", "w_kv"), outs))
        elif where == "attn_fwd" and outs:
            self.raw[l + 1]["w_out"], self.raw[l + 1]["w_kv"] = outs
        elif where == "d_w_in":
            self.recv[l]["w_out"], self.recv[l]["w_kv"] = outs
        elif where == "d_x":
            self.recv[l]["w_in"] = outs[0]
        elif where == "attn_bwd" and outs:
            self.recv[l + 1]["w_in"] = outs[0]


def _local_step(x, mem, target, pipe, rel, gate_b, norm_g, ln_g, ln_b):
    depth = pipe.depth
    s = x.shape[0]
    saved = []
    xl = x
    for l in range(depth):
        w_in_p = pipe.w_in(l)
        hmat, landed = _mm(xl, w_in_p, out_dtype=BF16, tm=1024, tn=1024, tk=D, name="in_proj",
                           comm=pipe.host("in_proj", l), dead_n=H_DEAD)
        pipe.landed("in_proj", l, landed)
        w_out_f, w_kv_f = pipe.w_rest(l)
        mkv, _ = _mm(mem, w_kv_f, out_dtype=BF16, tm=N_MEM, tn=1024, tk=D, name="mem_kv")
        bias = _band_bias(rel[l])
        gw = jnp.zeros((128, B_KW), F32).at[:GATE_RANK].set(pipe.gate_w(l)).astype(BF16)
        gb, ng = gate_b[l][None, :], norm_g[l][None, :]
        r, linv, probs, landed = _attn_fwd(hmat, lax.empty((s, D), BF16), bias, pipe.host("attn_fwd", l))
        pipe.landed("attn_fwd", l, landed)
        r, opre, states = _gla_fwd(hmat, r, gw, gb, ng)
        r = _mem_fwd(hmat, r, mkv)
        xn, xh, rstd, y, *loss = _outproj_ln(hmat, r, w_out_f, xl, ln_g[l][None, :], ln_b[l][None, :],
                                             target if l == depth - 1 else None)
        saved.append(dict(x=xl, h=hmat, mkv=mkv, gw=gw, gb=gb, ng=ng, r=r, linv=linv, probs=probs, opre=opre,
                          states=states, xh=xh, rstd=rstd, y=y, w_in_p=w_in_p, w_out_f=w_out_f))
        xl = xn
    (loss,), g = loss, xl

    grads = [None] * depth
    for l in reversed(range(depth)):
        sv = saved[l]
        dy, du, d_lng, d_lnb, d_wout = _ln_bwd_dy(g, sv["xh"], sv["rstd"], ln_g[l][None, :], sv["w_out_f"], sv["y"])
        dh, dbias, landed = _attn_bwd(sv["h"], sv["r"], dy, sv["linv"], sv["probs"], lax.empty((s, H_W), BF16),
                                      pipe.host("attn_bwd", l))
        pipe.landed("attn_bwd", l, landed)
        dh, d_gw, d_gb, d_ng = _gla_bwd(sv["h"], dy, sv["opre"], sv["states"], sv["gw"], sv["gb"], sv["ng"], dh)
        dh, d_mkv = _mem_bwd(sv["h"], sv["r"], dy, sv["mkv"], dh)
        d_wkv, _ = _mm(mem, d_mkv, ta=True, out_dtype=BF16, tm=1024, tn=1024, tk=N_MEM, name="d_w_kv")
        d_win, landed = _mm(sv["x"], dh, ta=True, out_dtype=BF16, tm=1024, tn=1792, tk=1024, name="d_w_in",
                            comm=pipe.host("d_w_in", l, (d_wout, d_wkv)), dead_n=H_DEAD)
        pipe.landed("d_w_in", l, landed)
        g, landed = _mm(dh, sv["w_in_p"], tb=True, out_dtype=F32, tm=1024, tn=1024, tk=1792, name="d_x",
                        adds=((du, ALPHA),), comm=pipe.host("d_x", l, d_win), dead_k=H_DEAD)
        pipe.landed("d_x", l, landed)
        grads[l] = dict(rel=_bias_grad(dbias), gate_w=d_gw[:GATE_RANK], gate_b=d_gb[0], norm_g=d_ng[0],
                        ln_g=d_lng[0], ln_b=d_lnb[0])
    return loss, g, grads


def _adamw(parts, w, m, v, rows_per_step, name):
    depth, rows, cols = w.shape
    n = parts[0].shape[0]
    tr = min(rows_per_step, rows)
    assert rows % tr == 0 and len(parts) == depth

    def body(*refs):
        p_refs = refs[:depth]
        w_ref, m_ref, v_ref, g_ref, d_ref, nm_ref, nv_ref = refs[depth:]
        for l in range(depth):
            @pl.when(pl.program_id(0) == l)
            def _(p_ref=p_refs[l]):
                g = p_ref[0].astype(F32)
                for j in range(1, n):
                    g = g + p_ref[j].astype(F32)
                nm = ADAM_B1 * m_ref[...] + (1.0 - ADAM_B1) * g
                nv = ADAM_B2 * v_ref[...] + (1.0 - ADAM_B2) * (g * g)
                m_hat = nm / (1.0 - ADAM_B1 ** ADAM_STEP)
                v_hat = nv / (1.0 - ADAM_B2 ** ADAM_STEP)
                g_ref[...] = g
                nm_ref[...] = nm
                nv_ref[...] = nv
                d_ref[...] = -ADAM_LR * (m_hat / (jnp.sqrt(v_hat) + ADAM_EPS) + ADAM_WD * w_ref[...])

    def part_spec(l):
        return pl.BlockSpec((n, tr, cols), lambda ll, i: (0, jnp.where(ll == l, i, 0), 0))

    blk = pl.BlockSpec((None, tr, cols), lambda ll, i: (ll, i, 0))
    shape = jax.ShapeDtypeStruct((depth, rows, cols), F32)
    return pl.pallas_call(
        body, name=name,
        out_shape=(shape, shape, shape, shape),
        grid=(depth, rows // tr),
        in_specs=[part_spec(l) for l in range(depth)] + [blk, blk, blk],
        out_specs=(blk, blk, blk, blk),
        compiler_params=_params(("arbitrary", "arbitrary")),
    )(*parts, w, m, v)


SMALL = (("rel", A_HEADS * (2 * MAX_REL + 1)), ("gate_w", GATE_RANK * B_KW), ("gate_b", B_KW), ("norm_g", B_DV),
         ("ln_g", D), ("ln_b", D))


def _pack_small(parts, depth):
    rows = []
    for name, size in SMALL:
        flat = parts[name].reshape(depth * size).astype(F32)
        rows.append(jnp.pad(flat, (0, -(depth * size) % 128)).reshape(-1, 128))
    packed = jnp.concatenate(rows, axis=0)
    return jnp.pad(packed, ((0, -packed.shape[0] % 8), (0, 0)))


def _unpack_small(packed, depth, shapes):
    out, row = {}, 0
    for name, size in SMALL:
        nrow = -(-(depth * size) // 128)
        out[name] = packed[row:row + nrow].reshape(-1)[:depth * size].reshape(shapes[name])
        row += nrow
    return out


def kernel(x, mem, w_in, a_rel_bias, b_gate_w, b_gate_b, b_norm_g, w_mem_kv, w_out, ln_g, ln_b, loss_target, m_w_in, m_a_rel_bias, m_b_gate_w, m_b_gate_b, m_b_norm_g, m_w_mem_kv, m_w_out, m_ln_g, m_ln_b, v_w_in, v_a_rel_bias, v_b_gate_w, v_b_gate_b, v_b_norm_g, v_w_mem_kv, v_w_out, v_ln_g, v_ln_b):
    depth = w_in.shape[0]
    sh_in = w_in.shape[2]
    sh_gw = b_gate_w.shape[2]
    me = 4 * lax.axis_index("x") + 2 * lax.axis_index("y") + lax.axis_index("c")

    pipe = _Fsdp(w_in.astype(BF16), w_out.astype(BF16), w_mem_kv.astype(BF16), b_gate_w)
    loss_dev, dx, grads = _local_step(x[0], mem[0], loss_target[0], pipe, a_rel_bias, b_gate_b, b_norm_g, ln_g, ln_b)
    loss = lax.psum(loss_dev[0, 0], ("x", "y", "c"))

    recv = lambda n: [pipe.recv[l][n] for l in range(depth)]
    big = {"w_in": _adamw(recv("w_in"), w_in, m_w_in, v_w_in, 128, "adamw_w_in"),
           "w_out": _adamw(recv("w_out"), w_out, m_w_out, v_w_out, 64, "adamw_w_out"),
           "w_kv": _adamw(recv("w_kv"), w_mem_kv, m_w_mem_kv, v_w_mem_kv, 128, "adamw_w_kv")}

    shapes = {"rel": a_rel_bias.shape, "gate_w": (depth, GATE_RANK, N_DEV * sh_gw), "gate_b": b_gate_b.shape,
              "norm_g": b_norm_g.shape, "ln_g": ln_g.shape, "ln_b": ln_b.shape}
    part = _pack_small({n: jnp.stack([grads[l][n] for l in range(depth)]) for n, _ in SMALL}, depth)
    (all_parts,) = _gather_now([part], "gather_small")
    zeros_gw = jnp.zeros(shapes["gate_w"], F32)
    w_s = _pack_small(dict(rel=a_rel_bias, gate_w=zeros_gw, gate_b=b_gate_b, norm_g=b_norm_g, ln_g=ln_g, ln_b=ln_b), depth)
    m_s = _pack_small(dict(rel=m_a_rel_bias, gate_w=zeros_gw, gate_b=m_b_gate_b, norm_g=m_b_norm_g, ln_g=m_ln_g, ln_b=m_ln_b), depth)
    v_s = _pack_small(dict(rel=v_a_rel_bias, gate_w=zeros_gw, gate_b=v_b_gate_b, norm_g=v_b_norm_g, ln_g=v_ln_g, ln_b=v_ln_b), depth)
    small = [_unpack_small(t[0], depth, shapes)
             for t in _adamw([all_parts], w_s[None], m_s[None], v_s[None], all_parts.shape[1], "adamw_small")]
    gw_grad = lax.dynamic_slice_in_dim(small[0]["gate_w"], me * sh_gw, sh_gw, axis=2).reshape(1, depth * GATE_RANK, sh_gw)
    flat = lambda t: t.reshape(1, depth * GATE_RANK, sh_gw)
    gw_res = [t.reshape(depth, GATE_RANK, sh_gw)
              for t in _adamw([gw_grad], flat(b_gate_w), flat(m_b_gate_w), flat(v_b_gate_w), depth * GATE_RANK, "adamw_gate_w")]

    def leaves(t):
        return (big["w_in"][t], small[t]["rel"], gw_res[t], small[t]["gate_b"], small[t]["norm_g"],
                big["w_kv"][t], big["w_out"][t], small[t]["ln_g"], small[t]["ln_b"])

    return (loss, dx[None]) + leaves(0) + leaves(1) + leaves(2) + leaves(3)
```

```python
import functools
import math

import numpy as np
import jax
import jax.numpy as jnp
from jax import lax
from jax.experimental import pallas as pl
from jax.experimental.pallas import tpu as pltpu

F32 = jnp.float32
BF16 = jnp.bfloat16

N_DEV = 8
D = 2048
CH = 64
LEFT = 8
MAX_REL = 128
N_MEM = 256
A_HEADS, A_DH, A_W = 8, 128, 1024
B_HEADS, B_DK, B_DV, B_KW, B_W = 4, 64, 128, 256, 512
GATE_RANK, GATE_TAU = 16, 16.0
M_HEADS, M_DH, M_W = 4, 128, 512
IN_W = 6672
NAT_SPLIT = 5648
H_W = 7168
H_PAD = H_W - IN_W
A_Q, A_K, A_V, A_Z = 0, 1024, 2048, 3072
B_BASE = 4096
M_BASE = 6144
H_DEAD = (B_BASE + 1664, M_BASE)
ALPHA = (2.0 * 4) ** 0.25
LOG2E = math.log2(math.e)
LN_EPS = 1e-5
RMS_EPS = 1e-6
NEG = -1e30
QB = 256
KB = 3 * QB
ADAM_LR, ADAM_B1, ADAM_B2, ADAM_EPS, ADAM_WD, ADAM_STEP = 0.001, 0.9, 0.999, 1e-08, 0.01, 10
VMEM_MB = 1024 * 1024


def _params(sem, vmem_mb=48):
    return pltpu.CompilerParams(dimension_semantics=sem, vmem_limit_bytes=vmem_mb * VMEM_MB)


def _sigmoid(x):
    return 1.0 / (1.0 + jnp.exp(-x))


def _dot(a, b, ca, cb, precision=None):
    return lax.dot_general(a, b, (((ca,), (cb,)), ((), ())), preferred_element_type=F32, precision=precision)


MESH = pl.DeviceIdType.MESH


class _Scatter:
    def __init__(self, xs, relations=tuple(range(N_DEV)), into=None):
        self.xs, self.n = list(xs), len(xs)
        self.relations, self.into = tuple(relations), list(into or [])
        self.out_shape = [jax.ShapeDtypeStruct(x.shape, x.dtype) for x in xs]
        self.specs = [pl.BlockSpec(memory_space=pltpu.HBM)] * self.n
        self.scratch = [pltpu.SemaphoreType.DMA((self.n, N_DEV)), pltpu.SemaphoreType.DMA((self.n, N_DEV)),
                        pltpu.SemaphoreType.DMA((self.n,))]

    def _copies(self, x_refs, o_refs, sems):
        send_sems, recv_sems, local_sems = sems
        mx, my, mc = lax.axis_index("x"), lax.axis_index("y"), lax.axis_index("c")
        me = 4 * mx + 2 * my + mc
        own, sends, arrivals = [], [], []
        for k in self.relations:
            if k == 0:
                own = [pltpu.make_async_copy(x_refs[a].at[me], o_refs[a].at[me], local_sems.at[a]) for a in range(self.n)]
                continue
            px = 1 - mx if k & 4 else mx
            py = 1 - my if k & 2 else my
            pc = 1 - mc if k & 1 else mc
            idx = 4 * px + 2 * py + pc
            for a in range(self.n):
                for dst, group in ((o_refs[a].at[me], sends), (o_refs[a].at[idx], arrivals)):
                    group.append(pltpu.make_async_remote_copy(
                        src_ref=x_refs[a].at[idx], dst_ref=dst, send_sem=send_sems.at[a, k], recv_sem=recv_sems.at[a, k],
                        device_id=(px, py, pc), device_id_type=MESH))
        return own, sends, arrivals

    def start(self, x_refs, o_refs, sems):
        own, sends, _ = self._copies(x_refs, o_refs, sems)
        for cp in own + sends:
            cp.start()

    def middle(self, x_refs, o_refs, sems):
        pass

    def finish(self, x_refs, o_refs, sems):
        own, sends, arrivals = self._copies(x_refs, o_refs, sems)
        for cp in sends:
            cp.wait_send()
        for cp in arrivals:
            cp.wait_recv()
        for cp in own:
            cp.wait()


class _Gather:
    into = ()

    def __init__(self, xs):
        self.xs, self.n = list(xs), len(xs)
        self.out_shape = [jax.ShapeDtypeStruct((N_DEV,) + x.shape, x.dtype) for x in xs]
        self.specs = [pl.BlockSpec(memory_space=pltpu.HBM)] * self.n
        self.scratch = [pltpu.SemaphoreType.DMA((self.n, N_DEV - 1)), pltpu.SemaphoreType.DMA((self.n, N_DEV - 1)),
                        pltpu.SemaphoreType.DMA((self.n,))]

    def _copies(self, x_refs, o_refs, sems):
        send_sems, recv_sems, local_sems = sems
        mx, my, mc = lax.axis_index("x"), lax.axis_index("y"), lax.axis_index("c")
        idx = lambda px, py, pc: 4 * px + 2 * py + pc
        me, sibling = (mx, my, mc), (mx, my, 1 - mc)
        chips = [(mx, 1 - my), (1 - mx, my), (1 - mx, 1 - my)]

        def copy(a, k, src, slot, to):
            return pltpu.make_async_remote_copy(
                src_ref=src, dst_ref=o_refs[a].at[idx(*slot)], send_sem=send_sems.at[a, k], recv_sem=recv_sems.at[a, k],
                device_id=to, device_id_type=MESH)

        c = dict(own=[], first=[], passed=[], ici_in=[], late_in=[])
        for a in range(self.n):
            x = x_refs[a]
            c["own"].append(pltpu.make_async_copy(x, o_refs[a].at[idx(*me)], local_sems.at[a]))
            c["first"].append(copy(a, 0, x, me, sibling))
            c["late_in"].append(copy(a, 0, x, sibling, sibling))
            for j, chip in enumerate(chips):
                c["first"].append(copy(a, 1 + j, x, me, (*chip, mc)))
                c["ici_in"].append(copy(a, 1 + j, x, (*chip, mc), (*chip, mc)))
                c["passed"].append(copy(a, 4 + j, o_refs[a].at[idx(*chip, mc)], (*chip, mc), sibling))
                c["late_in"].append(copy(a, 4 + j, x, (*chip, 1 - mc), sibling))
        return c

    def start(self, x_refs, o_refs, sems):
        c = self._copies(x_refs, o_refs, sems)
        for cp in c["own"] + c["first"]:
            cp.start()

    def middle(self, x_refs, o_refs, sems):
        c = self._copies(x_refs, o_refs, sems)
        for arrived, onward in zip(c["ici_in"], c["passed"]):
            arrived.wait_recv()
            onward.start()

    def finish(self, x_refs, o_refs, sems):
        c = self._copies(x_refs, o_refs, sems)
        for cp in c["first"] + c["passed"]:
            cp.wait_send()
        for cp in c["late_in"]:
            cp.wait_recv()
        for cp in c["own"]:
            cp.wait()


def _gather_now(xs, name):
    comm = _Gather(xs)

    def body(*refs):
        x_refs, o_refs, sems = refs[:comm.n], refs[comm.n:2 * comm.n], refs[2 * comm.n:]
        comm.start(x_refs, o_refs, sems)
        comm.middle(x_refs, o_refs, sems)
        comm.finish(x_refs, o_refs, sems)

    return pl.pallas_call(body, name=name, out_shape=tuple(comm.out_shape), in_specs=comm.specs,
                          out_specs=tuple(comm.specs), scratch_shapes=comm.scratch)(*comm.xs)


def _hosted(body, comm, *, name, grid, in_specs, out_specs, out_shape, scratch_shapes, compiler_params, inputs,
            input_output_aliases=None):
    out_specs, out_shape = tuple(out_specs), tuple(out_shape)
    aliases = input_output_aliases or {}
    if comm is None:
        outs = pl.pallas_call(body, name=name, grid=grid, in_specs=list(in_specs), out_specs=out_specs, out_shape=out_shape,
                              scratch_shapes=list(scratch_shapes), compiler_params=compiler_params,
                              input_output_aliases=aliases)(*inputs)
        return tuple(outs), ()
    ni, no, ns, nc = len(in_specs), len(out_specs), len(scratch_shapes), comm.n
    n_into = len(comm.into)
    aliases = {**aliases, **{ni + nc + a: no + a for a in range(n_into)}}

    def wrapped(*refs):
        ins, c_in = refs[:ni], refs[ni:ni + nc]
        refs = refs[ni + nc + n_into:]
        outs, c_out = refs[:no], refs[no:no + nc]
        scr, sems = refs[no + nc:no + nc + ns], refs[no + nc + ns:]
        first = functools.reduce(jnp.logical_and, [pl.program_id(d) == 0 for d in range(len(grid))])
        last = functools.reduce(jnp.logical_and, [pl.program_id(d) == grid[d] - 1 for d in range(len(grid))])

        step = functools.reduce(lambda acc, d: acc * grid[d] + pl.program_id(d), range(len(grid)), 0)
        n_steps = math.prod(grid)

        @pl.when(first)
        def _():
            comm.start(c_in, c_out, sems)

        body(*ins, *outs, *scr)

        @pl.when(step == max(7 * n_steps // 8, 1) - 1)
        def _():
            comm.middle(c_in, c_out, sems)

        @pl.when(last)
        def _():
            comm.finish(c_in, c_out, sems)

    params = pltpu.CompilerParams(dimension_semantics=("arbitrary",) * len(grid),
                                  vmem_limit_bytes=compiler_params.vmem_limit_bytes)
    outs = pl.pallas_call(wrapped, name=name, grid=grid, in_specs=list(in_specs) + comm.specs + comm.specs[:n_into],
                          out_specs=out_specs + tuple(comm.specs), out_shape=out_shape + tuple(comm.out_shape),
                          scratch_shapes=list(scratch_shapes) + comm.scratch, compiler_params=params,
                          input_output_aliases=aliases)(*inputs, *comm.xs, *comm.into)
    return tuple(outs[:no]), tuple(outs[no:])


def _live_pieces(dead, tile):
    lo, hi = dead
    t = lo // tile
    assert (hi - 1) // tile == t and lo % 128 == 0 and hi % 128 == 0
    return t, [(x, y) for x, y in ((0, lo - t * tile), (hi - t * tile, tile)) if y > x]


def _mm(a, b, *, ta=False, tb=False, out_dtype, tm, tn, tk, name, adds=(), vmem_mb=48, comm=None, dead_n=None, dead_k=None):
    m = a.shape[1] if ta else a.shape[0]
    k = a.shape[0] if ta else a.shape[1]
    n = b.shape[0] if tb else b.shape[1]
    assert k == (b.shape[1] if tb else b.shape[0])
    tm, tn, tk = min(tm, m), min(tn, n), min(tk, k)
    assert m % tm == 0 and n % tn == 0 and k % tk == 0, (name, m, n, k)
    nk = k // tk
    n_add = len(adds)
    scales = [s for _, s in adds]
    jd, cols_d = _live_pieces(dead_n, tn) if dead_n else (None, None)
    kd, ks_d = _live_pieces(dead_k, tk) if dead_k else (None, None)

    def body(a_ref, b_ref, *rest):
        add_refs, o_ref = rest[:n_add], rest[n_add]
        acc_ref = rest[n_add + 1] if nk > 1 else None
        jj, kk = pl.program_id(1), pl.program_id(2)

        def product(c0, c1, ks):
            p = None
            for k0, k1 in ks:
                a_blk = a_ref[k0:k1, :] if ta else a_ref[:, k0:k1]
                b_blk = b_ref[c0:c1, k0:k1] if tb else b_ref[k0:k1, c0:c1]
                q = _dot(a_blk.astype(BF16), b_blk.astype(BF16), 0 if ta else 1, 1 if tb else 0)
                p = q if p is None else p + q
            return p

        def step(mode, cols, ks):
            for c0, c1 in cols:
                p = product(c0, c1, ks)
                if mode == "first":
                    acc_ref[:, c0:c1] = p
                elif mode == "mid":
                    acc_ref[:, c0:c1] += p
                else:
                    r = p if mode == "only" else acc_ref[:, c0:c1] + p
                    for ref, s in zip(add_refs, scales):
                        r = r + s * ref[:, c0:c1].astype(F32)
                    o_ref[:, c0:c1] = r.astype(out_dtype)

        modes = [("only", None, range(1))] if nk == 1 else (
            [("first", kk == 0, range(1))] + ([("mid", (kk > 0) & (kk < nk - 1), range(1, nk - 1))] if nk > 2 else [])
            + [("last", kk == nk - 1, range(nk - 1, nk))])
        for mode, when_k, k_range in modes:
            k_cases = [(when_k, [(0, tk)])]
            if kd is not None and kd in k_range:
                is_dead = kk == kd
                k_cases = [(is_dead, ks_d)] + ([(when_k & ~is_dead, [(0, tk)])] if len(k_range) > 1 else [])
            for cond_k, ks in k_cases:
                n_cases = [(None, [(0, tn)])] if jd is None else [(jj == jd, cols_d), (jj != jd, [(0, tn)])]
                for cond_n, cols in n_cases:
                    conds = [c for c in (cond_k, cond_n) if c is not None]
                    run = functools.partial(step, mode, cols, ks)
                    pl.when(functools.reduce(jnp.logical_and, conds))(run) if conds else run()

    a_spec = pl.BlockSpec((tk, tm), lambda i, j, kk: (kk, i)) if ta else pl.BlockSpec((tm, tk), lambda i, j, kk: (i, kk))
    b_spec = pl.BlockSpec((tn, tk), lambda i, j, kk: (j, kk)) if tb else pl.BlockSpec((tk, tn), lambda i, j, kk: (kk, j))
    add_specs = [pl.BlockSpec((tm, tn), lambda i, j, kk: (i, j)) for _ in adds]
    (out,), c_out = _hosted(
        body, comm, name=name,
        out_shape=[jax.ShapeDtypeStruct((m, n), out_dtype)],
        grid=(m // tm, n // tn, nk),
        in_specs=[a_spec, b_spec] + add_specs,
        out_specs=[pl.BlockSpec((tm, tn), lambda i, j, kk: (i, j))],
        scratch_shapes=[pltpu.VMEM((tm, tn), F32)] if nk > 1 else [],
        compiler_params=_params(("parallel", "parallel", "arbitrary"), vmem_mb),
        inputs=(a, b, *[x for x, _ in adds]))
    return out, c_out


NKB = KB // QB
LEAD = NKB - 1


def _band_bias(table):
    i = np.arange(QB)[:, None]
    j = np.arange(KB)[None, :]
    qc = i // CH + LEAD * QB // CH
    kc = j // CH
    valid = (kc <= qc) & (kc >= qc - LEFT)
    n = QB + KB
    c = np.arange(n)
    onehot = np.zeros((2 * MAX_REL + 1, n), np.float32)
    onehot[np.clip(LEAD * QB - (c - (QB - 1)), -MAX_REL, MAX_REL) + MAX_REL, c] = 1.0
    row = jnp.dot(table.astype(F32), jnp.asarray(onehot), precision=lax.Precision.HIGHEST)
    flow = jnp.tile(row, (1, QB))[:, :QB * (n - 1)].reshape(table.shape[0], QB, n - 1)
    return jnp.where(valid[None], flow[:, :, QB - 1:] * LOG2E, NEG)


def _bias_grad(dbias):
    h, n = dbias.shape[0], KB + 1
    flat = jnp.pad(dbias.reshape(h, QB * KB), ((0, 0), (0, -(QB * KB) % n)))
    diag = flat.reshape(h, -1, n).sum(axis=1)
    c = np.arange(n)
    jm = np.where(c < LEAD * QB + CH, c, c - n)
    didx = np.clip(LEAD * QB - jm, -MAX_REL, MAX_REL) + MAX_REL
    onehot = np.zeros((n, 2 * MAX_REL + 1), np.float32)
    onehot[c, didx] = 1.0
    return jnp.dot(diag, jnp.asarray(onehot), precision=lax.Precision.HIGHEST)


def _attn_scores(q, k_refs, cs, bias_h, m, masked, scale):
    parts = []
    for t in range(NKB):
        sc = _dot(q, k_refs[t][:, cs], 1, 1) * scale
        parts.append(jnp.where(m + t - LEAD >= 0, sc, NEG) if masked else sc)
    return jnp.concatenate(parts, axis=1) + bias_h


def _attn_fwd(h, r_in, bias, comm=None):
    s = h.shape[0]
    nq = s // QB
    scale = A_DH ** -0.5

    def body(q_ref, *rest):
        k_refs, v_refs = rest[:NKB], rest[NKB:2 * NKB]
        bias_hbm, r_any, o_ref, linv_ref, p_ref, bias_ref = rest[2 * NKB:]
        del r_any
        m = pl.program_id(0)

        @pl.when(m == 0)
        def _():
            pltpu.sync_copy(bias_hbm, bias_ref)

        def scores(hd, masked):
            cs = slice(hd * A_DH, (hd + 1) * A_DH)
            return _attn_scores(q_ref[:, cs], k_refs, cs, bias_ref[hd], m, masked, scale * LOG2E)

        def step(masked):
            sc_next = scores(0, masked)
            for hd in range(A_HEADS):
                cs = slice(hd * A_DH, (hd + 1) * A_DH)
                sc = sc_next
                if hd + 1 < A_HEADS:
                    sc_next = scores(hd + 1, masked)
                mx = jnp.max(sc, axis=1, keepdims=True)
                p = jnp.exp2(sc - mx)
                l = jnp.sum(p, axis=1, keepdims=True)
                pb = p.astype(BF16)
                p_ref[hd] = pb
                o = _dot(pb[:, :QB], v_refs[0][:, cs], 1, 0)
                for t in range(1, NKB):
                    o += _dot(pb[:, t * QB:(t + 1) * QB], v_refs[t][:, cs], 1, 0)
                linv = 1.0 / l
                o_ref[:, cs] = (o * linv).astype(BF16)
                linv_ref[:, hd:hd + 1] = linv

        pl.when(m < LEAD)(functools.partial(step, True))
        pl.when(m >= LEAD)(functools.partial(step, False))

    def kv_spec(col, t):
        return pl.BlockSpec((QB, A_W), lambda m: (jnp.maximum(m + t - LEAD, 0), col))

    (r, linv, probs), c_out = _hosted(
        body, comm, name="attn_fwd",
        out_shape=(jax.ShapeDtypeStruct(r_in.shape, BF16), jax.ShapeDtypeStruct((s, A_HEADS), F32),
                   jax.ShapeDtypeStruct((nq, A_HEADS, QB, KB), BF16)),
        grid=(nq,),
        in_specs=[pl.BlockSpec((QB, A_W), lambda m: (m, 0))]
        + [kv_spec(1, t) for t in range(NKB)] + [kv_spec(2, t) for t in range(NKB)]
        + [pl.BlockSpec(memory_space=pl.ANY), pl.BlockSpec(memory_space=pl.ANY)],
        out_specs=(pl.BlockSpec((QB, A_W), lambda m: (m, 0)), pl.BlockSpec((QB, A_HEADS), lambda m: (m, 0)),
                   pl.BlockSpec((None, A_HEADS, QB, KB), lambda m: (m, 0, 0, 0))),
        scratch_shapes=[pltpu.VMEM((A_HEADS, QB, KB), F32)],
        input_output_aliases={2 * NKB + 2: 0},
        compiler_params=_params(("arbitrary",)),
        inputs=(h,) * (2 * NKB + 1) + (bias, r_in))
    return r, linv, probs, c_out


def _attn_bwd(h, r, dy, linv, probs, dh_in, comm=None):
    s = h.shape[0]
    nq = s // QB
    scale = A_DH ** -0.5

    def body(q_ref, *rest):
        k_refs, v_refs = rest[:NKB], rest[NKB:2 * NKB]
        (z_ref, r_ref, dy_ref, linv_ref, p_ref, zl_ref, rl_ref, dyl_ref, dh_any,
         dh_ref, dbias_hbm, dk_acc, dv_acc, dq_ring, dbias_ref) = rest[2 * NKB:]
        del dh_any
        m = pl.program_id(0)

        def slot(b):
            return pl.multiple_of(lax.rem(b + NKB, NKB) * QB, QB)

        @pl.when(m == 0)
        def _():
            dk_acc[...] = jnp.zeros_like(dk_acc)
            dv_acc[...] = jnp.zeros_like(dv_acc)
            dq_ring[...] = jnp.zeros_like(dq_ring)
            dbias_ref[...] = jnp.zeros_like(dbias_ref)

        @pl.when(m < nq)
        def _():
            z = z_ref[...].astype(F32)
            do_all = dy_ref[...].astype(F32) * (z * _sigmoid(z))
            o_all = r_ref[...].astype(F32)

            def product(hd):
                cs = slice(hd * A_DH, (hd + 1) * A_DH)
                dob = do_all[:, cs].astype(BF16)
                return jnp.concatenate([_dot(dob, v_refs[t][:, cs], 1, 1) for t in range(NKB)], axis=1)

            ahead = product(0)
            for hd in range(A_HEADS):
                cs = slice(hd * A_DH, (hd + 1) * A_DH)
                q = q_ref[:, cs]
                do = do_all[:, cs]
                dob = do.astype(BF16)
                delta = jnp.sum(do * o_all[:, cs], axis=1, keepdims=True)
                dp = ahead
                if hd + 1 < A_HEADS:
                    ahead = product(hd + 1)
                p = p_ref[hd].astype(F32) * linv_ref[:, hd:hd + 1]
                ds = p * (dp - delta)
                dbias_ref[hd] += ds
                pb, dsb = p.astype(BF16), ds.astype(BF16)
                dq = jnp.zeros((QB, A_DH), F32)
                for t in range(NKB):
                    ts = slice(t * QB, (t + 1) * QB)
                    rows = pl.ds(slot(m - LEAD + t), QB)
                    dq += _dot(dsb[:, ts], k_refs[t][:, cs], 1, 0)
                    dk_acc[rows, cs] += _dot(dsb[:, ts], q, 0, 0) * scale
                    dv_acc[rows, cs] += _dot(pb[:, ts], dob, 0, 0)
                dq_ring[pl.ds(slot(m), QB), cs] = dq * scale

        done = pl.ds(slot(m - LEAD), QB)
        zl = zl_ref[...].astype(F32)
        sg = _sigmoid(zl)
        dz = dyl_ref[...].astype(F32) * rl_ref[...].astype(F32) * (sg * (1.0 + zl * (1.0 - sg)))
        dh_ref[:, A_Q:A_Q + A_W] = dq_ring[done, :].astype(BF16)
        dh_ref[:, A_K:A_K + A_W] = dk_acc[done, :].astype(BF16)
        dh_ref[:, A_V:A_V + A_W] = dv_acc[done, :].astype(BF16)
        dh_ref[:, A_Z:A_Z + A_W] = dz.astype(BF16)
        dk_acc[done, :] = jnp.zeros((QB, A_W), F32)
        dv_acc[done, :] = jnp.zeros((QB, A_W), F32)

        @pl.when(m == nq + LEAD - 1)
        def _():
            pltpu.sync_copy(dbias_ref, dbias_hbm)

    last = nq - 1

    def cur(col):
        return pl.BlockSpec((QB, A_W), lambda m: (jnp.minimum(m, last), col))

    def kv_spec(col, t):
        return pl.BlockSpec((QB, A_W), lambda m: (jnp.clip(m + t - LEAD, 0, last), col))

    def lag(col):
        return pl.BlockSpec((QB, A_W), lambda m: (jnp.clip(m - LEAD, 0, last), col))

    (dh, dbias), c_out = _hosted(
        body, comm, name="attn_bwd",
        out_shape=(jax.ShapeDtypeStruct(dh_in.shape, BF16), jax.ShapeDtypeStruct((A_HEADS, QB, KB), F32)),
        grid=(nq + LEAD,),
        in_specs=[cur(0)] + [kv_spec(1, t) for t in range(NKB)] + [kv_spec(2, t) for t in range(NKB)]
        + [cur(3), cur(0), cur(0), pl.BlockSpec((QB, A_HEADS), lambda m: (jnp.minimum(m, last), 0)),
           pl.BlockSpec((None, A_HEADS, QB, KB), lambda m: (jnp.minimum(m, last), 0, 0, 0)),
           lag(3), lag(0), lag(0), pl.BlockSpec(memory_space=pl.ANY)],
        out_specs=(pl.BlockSpec((QB, 4 * A_W), lambda m: (jnp.clip(m - LEAD, 0, last), 0)),
                   pl.BlockSpec(memory_space=pl.ANY)),
        scratch_shapes=[pltpu.VMEM((KB, A_W), F32), pltpu.VMEM((KB, A_W), F32), pltpu.VMEM((KB, A_W), F32),
                        pltpu.VMEM((A_HEADS, QB, KB), F32)],
        input_output_aliases={2 * NKB + 9: 0},
        compiler_params=_params(("arbitrary",), 56),
        inputs=(h,) * (2 * NKB + 2) + (r, dy, linv, probs, h, r, dy, dh_in))
    return dh, dbias, c_out


GB = 256
N_PAIR = B_HEADS // 2


def _gla_gates(lr, gw_ref, gb_ref):
    logit = _dot(lr, gw_ref[...], 1, 0) + gb_ref[...]
    lg = (jnp.minimum(logit, 0.0) - jnp.log(1.0 + jnp.exp(-jnp.abs(logit)))) / GATE_TAU
    row = lax.broadcasted_iota(jnp.int32, (GB, GB), 0)
    col = lax.broadcasted_iota(jnp.int32, (GB, GB), 1)
    tri = jnp.where((row // CH == col // CH) & (col <= row), 1.0, 0.0).astype(F32)
    return logit, _dot(tri, lg, 1, 0, precision=lax.Precision.HIGHEST)


def _gla_factors(hb_ref, b_all, c):
    rs = slice(c * CH, (c + 1) * CH)
    q = hb_ref[rs, 0:B_KW].astype(F32) * (B_DK ** -0.5)
    k = hb_ref[rs, B_KW:2 * B_KW].astype(F32)
    b = b_all[rs]
    bm, bl = b[CH // 2:CH // 2 + 1, :], b[CH - 1:CH, :]
    e1, e2, eb, ek = jnp.exp(b - bm), jnp.exp(bm - b), jnp.exp(b), jnp.exp(bl - b)
    el = jnp.exp(bl)
    return dict(ql=q * e1, kl=k * e2, qu=q * e2, ku=k * e1, qt=q * eb, kh=k * ek, e1=e1, e2=e2, eb=eb, ek=ek, el=el)


class _GlaPairTools:
    def __init__(self, hb_ref, fs):
        self.hb_ref, self.fs, self.made = hb_ref, fs, {}
        row = lax.broadcasted_iota(jnp.int32, (2 * CH, 2 * CH), 0)
        col = lax.broadcasted_iota(jnp.int32, (2 * CH, 2 * CH), 1)
        self.same = (row // CH) == (col // CH)
        self.lower = self.same & ((row % CH) >= (col % CH))
        self.upper = self.same & ((row % CH) < (col % CH))
        self.lower_t = self.same & ((col % CH) >= (row % CH))
        self.upper_t = self.same & ((col % CH) < (row % CH))

    def _once(self, key, make):
        if key not in self.made:
            self.made[key] = make()
        return self.made[key]

    def lanes(self, c, p, name):
        return self.fs[c][name][:, p * 128:(p + 1) * 128]

    def heads(self, c, p, name):
        def make():
            x = self.lanes(c, p, name)
            return jnp.where(self.same, jnp.concatenate([x, x], axis=0), 0.0).astype(BF16)
        return self._once(("heads", c, p, name), make)

    def twice(self, c, p, name):
        def make():
            x = self.lanes(c, p, name).astype(BF16)
            return jnp.concatenate([x, x], axis=0)
        return self._once(("twice", c, p, name), make)

    def vals(self, c, p):
        rows = slice(c * CH, (c + 1) * CH)
        return self._once(("vals", c, p), lambda: jnp.concatenate(
            [self.hb_ref[rows, 512 + (2 * p + sh) * B_DV:512 + (2 * p + sh + 1) * B_DV] for sh in range(2)], axis=0))

    def intra(self, c, p):
        lo = _dot(self.heads(c, p, "ql"), self.twice(c, p, "kl"), 1, 1)
        up = _dot(self.heads(c, p, "qu"), self.twice(c, p, "ku"), 1, 1)
        return jnp.where(self.lower, lo, jnp.where(self.upper, up, 0.0)).astype(BF16)

    def fold(self, x):
        x = jnp.where(self.same, x, 0.0)
        return x[:CH] + x[CH:]


def _gla_fwd(h, r_in, gw, gb, ng):
    s = h.shape[0]
    nb = s // GB
    cpb = GB // CH

    def body(hb_ref, gw_ref, gb_ref, ng_ref, r_any, o_ref, opre_ref, st_ref, state):
        del r_any

        @pl.when(pl.program_id(0) == 0)
        def _():
            state[...] = jnp.zeros_like(state)

        _, b_all = _gla_gates(hb_ref[:, 1536:1664], gw_ref, gb_ref)
        fs = [_gla_factors(hb_ref, b_all, c) for c in range(cpb)]
        pairs = [(c, p) for c in range(cpb) for p in range(N_PAIR)]
        tools = _GlaPairTools(hb_ref, fs)
        a = {k: tools.intra(*k) for k in pairs}
        o_intra = {k: _dot(a[k], tools.vals(*k), 1, 0) for k in pairs}
        upd = {k: _dot(tools.vals(*k), tools.heads(*k, "kh"), 0, 0) for k in pairs}
        st = [state[p] for p in range(N_PAIR)]
        entering = {}
        for c, p in pairs:
            entering[c, p] = st[p]
            st_ref[c, p] = st[p]
            st[p] = st[p] * fs[c]["el"][:, p * 128:(p + 1) * 128] + upd[c, p]
        for p in range(N_PAIR):
            state[p] = st[p]
        for c, p in pairs:
            o2 = o_intra[c, p] + _dot(tools.heads(c, p, "qt"), entering[c, p].astype(BF16), 1, 1)
            for sh in range(2):
                o = o2[sh * CH:(sh + 1) * CH]
                rs, hs = slice(c * CH, (c + 1) * CH), slice((2 * p + sh) * B_DV, (2 * p + sh + 1) * B_DV)
                opre_ref[rs, hs] = o
                rinv = lax.rsqrt(jnp.mean(o * o, axis=1, keepdims=True) + RMS_EPS)
                o_ref[rs, hs] = (o * rinv * ng_ref[...]).astype(BF16)

    (r, opre, states), _ = _hosted(
        body, None, name="gla_fwd",
        out_shape=(jax.ShapeDtypeStruct(r_in.shape, BF16), jax.ShapeDtypeStruct((s, B_W), F32),
                   jax.ShapeDtypeStruct((s // CH, N_PAIR, 128, 128), F32)),
        grid=(nb,),
        in_specs=[pl.BlockSpec((GB, 2048), lambda i: (i, B_BASE // 2048)),
                  pl.BlockSpec((128, B_KW), lambda i: (0, 0)), pl.BlockSpec((1, B_KW), lambda i: (0, 0)),
                  pl.BlockSpec((1, B_DV), lambda i: (0, 0)), pl.BlockSpec(memory_space=pl.ANY)],
        out_specs=(pl.BlockSpec((GB, B_W), lambda i: (i, 1024 // B_W)), pl.BlockSpec((GB, B_W), lambda i: (i, 0)),
                   pl.BlockSpec((cpb, N_PAIR, 128, 128), lambda i: (i, 0, 0, 0))),
        scratch_shapes=[pltpu.VMEM((N_PAIR, 128, 128), F32)],
        input_output_aliases={4: 0},
        compiler_params=_params(("arbitrary",)),
        inputs=(h, gw, gb, ng, r_in))
    return r, opre, states


def _gla_bwd(h, dy, opre, states, gw, gb, ng, dh_in):
    s = h.shape[0]
    nb = s // GB
    cpb = GB // CH

    def body(hb_ref, dy_ref, opre_ref, st_ref, gw_ref, gb_ref, ng_ref, dh_any,
             dh_ref, dgw_ref, dgb_ref, dng_ref, dstate, db_scr, do_scr):
        del dh_any

        @pl.when(pl.program_id(0) == 0)
        def _():
            dstate[...] = jnp.zeros_like(dstate)
            dgw_ref[...] = jnp.zeros_like(dgw_ref)
            dgb_ref[...] = jnp.zeros_like(dgb_ref)
            dng_ref[...] = jnp.zeros_like(dng_ref)

        lr = hb_ref[:, 1536:1664]
        logit, b_all = _gla_gates(lr, gw_ref, gb_ref)
        z = hb_ref[:, 1024:1536].astype(F32)
        sg = _sigmoid(z)
        dyb = dy_ref[...].astype(F32)
        dng = jnp.zeros((1, B_DV), F32)
        for hd in range(B_HEADS):
            hs = slice(hd * B_DV, (hd + 1) * B_DV)
            o = opre_ref[:, hs]
            rinv = lax.rsqrt(jnp.mean(o * o, axis=1, keepdims=True) + RMS_EPS)
            on = o * rinv
            dr = dyb[:, hs] * (z[:, hs] * sg[:, hs])
            dh_ref[:, 1024 + hd * B_DV:1024 + (hd + 1) * B_DV] = (
                dyb[:, hs] * (on * ng_ref[...]) * (sg[:, hs] * (1.0 + z[:, hs] * (1.0 - sg[:, hs])))).astype(BF16)
            dng += jnp.sum(dr * on, axis=0, keepdims=True)
            dn = dr * ng_ref[...]
            do_scr[:, hs] = rinv * (dn - on * jnp.mean(dn * on, axis=1, keepdims=True))
        dng_ref[...] += dng

        rowi = lax.broadcasted_iota(jnp.int32, (CH, 128), 0)
        fs = [_gla_factors(hb_ref, b_all, c) for c in range(cpb)]
        pairs = [(c, p) for c in reversed(range(cpb)) for p in range(N_PAIR)]
        tools = _GlaPairTools(hb_ref, fs)
        pair = tools.lanes

        def douts(c, p):
            rows = slice(c * CH, (c + 1) * CH)
            return tools._once(("douts", c, p), lambda: jnp.concatenate(
                [do_scr[rows, (2 * p + sh) * B_DV:(2 * p + sh + 1) * B_DV] for sh in range(2)], axis=0).astype(BF16))

        at, da, dat, dst_own, g_qt = {}, {}, {}, {}, {}
        for k in pairs:
            c, p = k
            lo_t = _dot(tools.twice(*k, "kl"), tools.heads(*k, "ql"), 1, 1)
            up_t = _dot(tools.twice(*k, "ku"), tools.heads(*k, "qu"), 1, 1)
            at[k] = jnp.where(tools.lower_t, lo_t, jnp.where(tools.upper_t, up_t, 0.0)).astype(BF16)
            da[k] = _dot(douts(*k), tools.vals(*k), 1, 1)
            dat[k] = _dot(tools.vals(*k), douts(*k), 1, 1)
            dst_own[k] = _dot(douts(*k), tools.heads(*k, "qt"), 0, 0)
            g_qt[k] = tools.fold(_dot(douts(*k), st_ref[c, p].astype(BF16), 1, 0))
        dv, g_ql, g_qu, g_kl, g_ku = {}, {}, {}, {}, {}
        for k in pairs:
            dv[k] = _dot(at[k], douts(*k), 1, 0)
            g_ql[k] = tools.fold(_dot(jnp.where(tools.lower, da[k], 0.0).astype(BF16), tools.twice(*k, "kl"), 1, 0))
            g_qu[k] = tools.fold(_dot(jnp.where(tools.upper, da[k], 0.0).astype(BF16), tools.twice(*k, "ku"), 1, 0))
            g_kl[k] = tools.fold(_dot(jnp.where(tools.lower_t, dat[k], 0.0).astype(BF16), tools.heads(*k, "ql"), 1, 0))
            g_ku[k] = tools.fold(_dot(jnp.where(tools.upper_t, dat[k], 0.0).astype(BF16), tools.heads(*k, "qu"), 1, 0))
        dst = [dstate[p] for p in range(N_PAIR)]
        leaving = {}
        for c, p in pairs:
            leaving[c, p] = dst[p]
            dst[p] = dst[p] * pair(c, p, "el") + dst_own[c, p]
        for p in range(N_PAIR):
            dstate[p] = dst[p]
        g_kh = {}
        for k in pairs:
            c, p = k
            dstb = leaving[k].astype(BF16)
            dv2 = dv[k] + _dot(tools.heads(*k, "kh"), dstb, 1, 1)
            for sh in range(2):
                hd = 2 * p + sh
                dh_ref[c * CH:(c + 1) * CH, 512 + hd * B_DV:512 + (hd + 1) * B_DV] = dv2[sh * CH:(sh + 1) * CH].astype(BF16)
            g_kh[k] = tools.fold(_dot(tools.vals(*k), dstb, 1, 0))
        for c in reversed(range(cpb)):
            rs = slice(c * CH, (c + 1) * CH)
            for p in range(N_PAIR):
                k = (c, p)
                dq = (g_ql[k] * pair(c, p, "e1") + g_qu[k] * pair(c, p, "e2") + g_qt[k] * pair(c, p, "eb")) * (B_DK ** -0.5)
                dk = g_kl[k] * pair(c, p, "e2") + g_ku[k] * pair(c, p, "e1") + g_kh[k] * pair(c, p, "ek")
                dkh_kh = g_kh[k] * pair(c, p, "kh")
                db = (g_ql[k] * pair(c, p, "ql") - g_qu[k] * pair(c, p, "qu") + g_qt[k] * pair(c, p, "qt")
                      - g_kl[k] * pair(c, p, "kl") + g_ku[k] * pair(c, p, "ku") - dkh_kh)
                db_last = (pair(c, p, "el") * jnp.sum(leaving[c, p] * st_ref[c, p], axis=0, keepdims=True)
                           + jnp.sum(dkh_kh, axis=0, keepdims=True))
                db = jnp.where(rowi == CH - 1, db + db_last, db)
                dh_ref[rs, p * 128:(p + 1) * 128] = dq.astype(BF16)
                dh_ref[rs, B_KW + p * 128:B_KW + (p + 1) * 128] = dk.astype(BF16)
                db_scr[rs, p * 128:(p + 1) * 128] = db

        row = lax.broadcasted_iota(jnp.int32, (GB, GB), 0)
        col = lax.broadcasted_iota(jnp.int32, (GB, GB), 1)
        trit = jnp.where((row // CH == col // CH) & (col >= row), 1.0, 0.0).astype(F32)
        dlg = _dot(trit, db_scr[...], 1, 0, precision=lax.Precision.HIGHEST)
        dlogit = dlg * (_sigmoid(-logit) / GATE_TAU)
        dlb = dlogit.astype(BF16)
        dgw_ref[...] += _dot(lr, dlb, 0, 0)
        dgb_ref[...] += jnp.sum(dlogit, axis=0, keepdims=True)
        dh_ref[:, 1536:1664] = _dot(dlb, gw_ref[...], 1, 1).astype(BF16)
        dh_ref[:, 1664:2048] = jnp.zeros((GB, 384), BF16)

    rev = lambda i: nb - 1 - i
    (dh, dgw, dgb, dng), _ = _hosted(
        body, None, name="gla_bwd",
        out_shape=(jax.ShapeDtypeStruct(dh_in.shape, BF16), jax.ShapeDtypeStruct((128, B_KW), F32),
                   jax.ShapeDtypeStruct((1, B_KW), F32), jax.ShapeDtypeStruct((1, B_DV), F32)),
        grid=(nb,),
        in_specs=[pl.BlockSpec((GB, 2048), lambda i: (rev(i), B_BASE // 2048)),
                  pl.BlockSpec((GB, B_W), lambda i: (rev(i), 1024 // B_W)),
                  pl.BlockSpec((GB, B_W), lambda i: (rev(i), 0)),
                  pl.BlockSpec((cpb, N_PAIR, 128, 128), lambda i: (rev(i), 0, 0, 0)),
                  pl.BlockSpec((128, B_KW), lambda i: (0, 0)), pl.BlockSpec((1, B_KW), lambda i: (0, 0)),
                  pl.BlockSpec((1, B_DV), lambda i: (0, 0)), pl.BlockSpec(memory_space=pl.ANY)],
        out_specs=(pl.BlockSpec((GB, 2048), lambda i: (rev(i), B_BASE // 2048)),
                   pl.BlockSpec((128, B_KW), lambda i: (0, 0)), pl.BlockSpec((1, B_KW), lambda i: (0, 0)),
                   pl.BlockSpec((1, B_DV), lambda i: (0, 0))),
        scratch_shapes=[pltpu.VMEM((N_PAIR, 128, 128), F32), pltpu.VMEM((GB, B_KW), F32), pltpu.VMEM((GB, B_W), F32)],
        input_output_aliases={7: 0},
        compiler_params=_params(("arbitrary",)),
        inputs=(h, dy, opre, states, gw, gb, ng, dh_in))
    return dh, dgw, dgb, dng


MB = 512


def _mem_probs(q, mk, scale):
    sc = _dot(q, mk, 1, 1) * (scale * LOG2E)
    p = jnp.exp2(sc - jnp.max(sc, axis=1, keepdims=True))
    return p / jnp.sum(p, axis=1, keepdims=True)


def _mem_fwd(h, r_in, mkv):
    s = h.shape[0]
    scale = M_DH ** -0.5

    def body(q_ref, mkv_ref, r_any, o_ref):
        del r_any
        for hd in range(M_HEADS):
            cs = slice(hd * M_DH, (hd + 1) * M_DH)
            p = _mem_probs(q_ref[:, cs], mkv_ref[:, cs], scale)
            o_ref[:, cs] = _dot(p.astype(BF16), mkv_ref[:, M_W + hd * M_DH:M_W + (hd + 1) * M_DH], 1, 0).astype(BF16)

    return pl.pallas_call(
        body, name="mem_fwd",
        out_shape=jax.ShapeDtypeStruct(r_in.shape, BF16),
        grid=(s // MB,),
        in_specs=[pl.BlockSpec((MB, M_W), lambda i: (i, M_BASE // M_W)),
                  pl.BlockSpec((N_MEM, 2 * M_W), lambda i: (0, 0)), pl.BlockSpec(memory_space=pl.ANY)],
        out_specs=pl.BlockSpec((MB, M_W), lambda i: (i, 1536 // M_W)),
        input_output_aliases={2: 0},
        compiler_params=_params(("arbitrary",)),
    )(h, mkv, r_in)


def _mem_bwd(h, r, dy, mkv, dh_in):
    s = h.shape[0]
    scale = M_DH ** -0.5

    def body(q_ref, z_ref, r_ref, dy_ref, mkv_ref, dh_any, dh_ref, dmkv_ref):
        del dh_any

        @pl.when(pl.program_id(0) == 0)
        def _():
            dmkv_ref[...] = jnp.zeros_like(dmkv_ref)

        z = z_ref[...].astype(F32)
        sg = _sigmoid(z)
        dyv = dy_ref[...].astype(F32)
        do_all = dyv * (z * sg)
        dh_ref[:, M_W:2 * M_W] = (dyv * r_ref[...].astype(F32) * (sg * (1.0 + z * (1.0 - sg)))).astype(BF16)
        for hd in range(M_HEADS):
            cs = slice(hd * M_DH, (hd + 1) * M_DH)
            vs = slice(M_W + hd * M_DH, M_W + (hd + 1) * M_DH)
            q = q_ref[:, cs]
            p = _mem_probs(q, mkv_ref[:, cs], scale)
            dob = do_all[:, cs].astype(BF16)
            dp = _dot(dob, mkv_ref[:, vs], 1, 1)
            ds = p * (dp - jnp.sum(p * dp, axis=1, keepdims=True))
            dsb = ds.astype(BF16)
            dh_ref[:, cs] = (_dot(dsb, mkv_ref[:, cs], 1, 0) * scale).astype(BF16)
            dmkv_ref[:, cs] += _dot(dsb, q, 0, 0) * scale
            dmkv_ref[:, vs] += _dot(p.astype(BF16), dob, 0, 0)

    return pl.pallas_call(
        body, name="mem_bwd",
        out_shape=(jax.ShapeDtypeStruct(dh_in.shape, BF16), jax.ShapeDtypeStruct((N_MEM, 2 * M_W), F32)),
        grid=(s // MB,),
        in_specs=[pl.BlockSpec((MB, M_W), lambda i: (i, M_BASE // M_W)),
                  pl.BlockSpec((MB, M_W), lambda i: (i, M_BASE // M_W + 1)),
                  pl.BlockSpec((MB, M_W), lambda i: (i, 1536 // M_W)),
                  pl.BlockSpec((MB, M_W), lambda i: (i, 1536 // M_W)),
                  pl.BlockSpec((N_MEM, 2 * M_W), lambda i: (0, 0)), pl.BlockSpec(memory_space=pl.ANY)],
        out_specs=(pl.BlockSpec((MB, 2 * M_W), lambda i: (i, M_BASE // (2 * M_W))),
                   pl.BlockSpec((N_MEM, 2 * M_W), lambda i: (0, 0))),
        input_output_aliases={5: 0},
        compiler_params=_params(("arbitrary",)),
    )(h, h, r, dy, mkv, dh_in)


OB = 256


def _outproj_ln(h, r, w_out, x, ln_g, ln_b, target=None):
    s = h.shape[0]
    last = target is not None

    def body(za_ref, zb_ref, zm_ref, r_ref, w_ref, x_ref, g_ref, b_ref, *rest):
        if last:
            t_ref, xn_ref, xh_ref, rstd_ref, y_ref, l_ref = rest
        else:
            xn_ref, xh_ref, rstd_ref, y_ref = rest
        z = jnp.concatenate([za_ref[...], zb_ref[...], zm_ref[...]], axis=1).astype(F32)
        y = (r_ref[...].astype(F32) * (z * _sigmoid(z))).astype(BF16)
        y_ref[...] = y
        u = ALPHA * x_ref[...] + _dot(y, w_ref[...], 1, 0)
        mu = jnp.mean(u, axis=1, keepdims=True)
        uc = u - mu
        rstd = lax.rsqrt(jnp.mean(uc * uc, axis=1, keepdims=True) + LN_EPS)
        xh = uc * rstd
        xh_ref[...] = xh
        rstd_ref[...] = rstd
        xn = xh * g_ref[...] + b_ref[...]
        if last:
            @pl.when(pl.program_id(0) == 0)
            def _():
                l_ref[...] = jnp.zeros_like(l_ref)

            e = xn - t_ref[...]
            xn_ref[...] = e / D
            l_ref[...] += 0.5 * jnp.sum(jnp.mean(e * e, axis=1, keepdims=True))
        else:
            xn_ref[...] = xn

    row = lambda w, c: pl.BlockSpec((OB, w), lambda i: (i, c))
    vec = pl.BlockSpec((1, D), lambda i: (0, 0))
    full = jax.ShapeDtypeStruct((s, D), F32)
    return pl.pallas_call(
        body, name="outproj_ln_loss" if last else "outproj_ln",
        out_shape=(full, full, jax.ShapeDtypeStruct((s, 1), F32), jax.ShapeDtypeStruct((s, D), BF16))
        + ((jax.ShapeDtypeStruct((1, 128), F32),) if last else ()),
        grid=(s // OB,),
        in_specs=[row(A_W, A_Z // A_W), row(B_W, (B_BASE + 1024) // B_W), row(M_W, (M_BASE + M_W) // M_W), row(D, 0),
                  pl.BlockSpec((D, D), lambda i: (0, 0)), row(D, 0), vec, vec] + ([row(D, 0)] if last else []),
        out_specs=(row(D, 0), row(D, 0), pl.BlockSpec((OB, 1), lambda i: (i, 0)), row(D, 0))
        + ((pl.BlockSpec((1, 128), lambda i: (0, 0)),) if last else ()),
        compiler_params=_params(("arbitrary",), 56),
    )(h, h, h, r, w_out, x, ln_g, ln_b, *([target] if last else []))


def _ln_bwd_dy(g, xh, rstd, ln_g, w_out, y):
    s = g.shape[0]
    n = s // OB

    def body(g_ref, xh_ref, rstd_ref, lg_ref, w_ref, y_ref, dy_ref, du_ref, dg_ref, db_ref, dw_hbm, dw_acc, stage):
        i = pl.program_id(0)

        @pl.when(i == 0)
        def _():
            dg_ref[...] = jnp.zeros_like(dg_ref)
            db_ref[...] = jnp.zeros_like(db_ref)
            dw_acc[...] = jnp.zeros_like(dw_acc)

        gv, xh = g_ref[...], xh_ref[...]
        dg_ref[...] += jnp.sum(gv * xh, axis=0, keepdims=True)
        db_ref[...] += jnp.sum(gv, axis=0, keepdims=True)
        dxh = gv * lg_ref[...]
        du = rstd_ref[...] * (dxh - jnp.mean(dxh, axis=1, keepdims=True) - xh * jnp.mean(dxh * xh, axis=1, keepdims=True))
        du_ref[...] = du
        dub = du.astype(BF16)
        dy_ref[...] = _dot(dub, w_ref[...], 1, 1).astype(BF16)
        dw_acc[...] += _dot(y_ref[...], dub, 0, 0)

        @pl.when(i == n - 1)
        def _():
            for c in range(D // OB):
                stage[...] = dw_acc[c * OB:(c + 1) * OB, :].astype(BF16)
                pltpu.sync_copy(stage, dw_hbm.at[c * OB:(c + 1) * OB, :])

    row = pl.BlockSpec((OB, D), lambda i: (i, 0))
    vec = pl.BlockSpec((1, D), lambda i: (0, 0))
    return pl.pallas_call(
        body, name="ln_bwd_dy",
        out_shape=(jax.ShapeDtypeStruct((s, D), BF16), jax.ShapeDtypeStruct((s, D), F32),
                   jax.ShapeDtypeStruct((1, D), F32), jax.ShapeDtypeStruct((1, D), F32), jax.ShapeDtypeStruct((D, D), BF16)),
        grid=(n,),
        in_specs=[row, row, pl.BlockSpec((OB, 1), lambda i: (i, 0)), vec, pl.BlockSpec((D, D), lambda i: (0, 0)), row],
        out_specs=(row, row, vec, vec, pl.BlockSpec(memory_space=pl.ANY)),
        scratch_shapes=[pltpu.VMEM((D, D), F32), pltpu.VMEM((OB, D), BF16)],
        compiler_params=_params(("arbitrary",), 60),
    )(g, xh, rstd, ln_g, w_out, y)


class _LocalWeights:
    def __init__(self, w_in_p, w_out_f, w_kv_f, gate_w):
        self.w, self.gate = list(zip(w_in_p, w_out_f, w_kv_f)), gate_w
        self.depth = len(self.w)
        self.grads = [dict() for _ in self.w]

    def w_in(self, l):
        return self.w[l][0]

    def w_rest(self, l):
        return self.w[l][1:]

    def gate_w(self, l):
        return self.gate[l]

    def host(self, where, l, payload=None):
        if payload is not None:
            self.grads[l][where] = payload
        return None

    def landed(self, where, l, outs):
        pass


def _shard_pieces():
    sh = IN_W // N_DEV
    pieces = []
    for j in range(N_DEV):
        lo, hi = j * sh, (j + 1) * sh
        cuts = [lo, NAT_SPLIT, hi] if lo < NAT_SPLIT < hi else [lo, hi]
        for a, b in zip(cuts[:-1], cuts[1:]):
            pieces.append((j, a - lo, a if a < NAT_SPLIT else a + H_PAD, b - a))
    return pieces


RB = 256


def _shards_to_padded(raw):
    def body(x_ref, o_ref):
        for j, src, dst, width in _shard_pieces():
            o_ref[:, dst:dst + width] = x_ref[j, :, src:src + width]
        o_ref[:, NAT_SPLIT:NAT_SPLIT + H_PAD] = jnp.zeros((RB, H_PAD), o_ref.dtype)

    return pl.pallas_call(
        body, name="place_w_in", out_shape=jax.ShapeDtypeStruct((D, H_W), raw.dtype), grid=(D // RB,),
        in_specs=[pl.BlockSpec((N_DEV, RB, raw.shape[2]), lambda i: (0, i, 0))],
        out_specs=pl.BlockSpec((RB, H_W), lambda i: (i, 0)),
        compiler_params=_params(("parallel",)),
    )(raw)


def _padded_to_shards(w):
    sh = IN_W // N_DEV

    def body(x_ref, o_ref):
        for j, src, dst, width in _shard_pieces():
            o_ref[j, :, src:src + width] = x_ref[:, dst:dst + width]

    return pl.pallas_call(
        body, name="split_d_w_in", out_shape=jax.ShapeDtypeStruct((N_DEV, D, sh), w.dtype), grid=(D // RB,),
        in_specs=[pl.BlockSpec((RB, H_W), lambda i: (i, 0))],
        out_specs=pl.BlockSpec((N_DEV, RB, sh), lambda i: (0, i, 0)),
        compiler_params=_params(("parallel",)),
    )(w)


_PART_A = (0, 1, 2, 4)
_PART_B = (3, 5, 6, 7)


class _Fsdp:
    def __init__(self, w_in, w_out, w_kv, gate_w):
        self.sh = (w_in, w_out, w_kv)
        self.gate_sh = gate_w
        self.depth = w_in.shape[0]
        self.raw = [dict() for _ in range(self.depth)]
        self.recv = [dict() for _ in range(self.depth)]
        (self.raw[0]["w_in"],) = _gather_now([w_in[0]], "gather_layer0")

    def w_in(self, l):
        return _shards_to_padded(self.raw[l]["w_in"])

    def w_rest(self, l):
        return self.raw[l]["w_out"].reshape(D, D), self.raw[l]["w_kv"].reshape(D, 2 * M_W)

    def gate_w(self, l):
        return jnp.transpose(self.gate_all[:, l], (1, 0, 2)).reshape(GATE_RANK, -1)

    def host(self, where, l, payload=None):
        w_in, w_out, w_kv = self.sh
        if where == "in_proj":
            xs = [w_out[0], w_kv[0], self.gate_sh] if l == 0 else []
            if l + 1 < self.depth:
                xs += [w_in[l + 1]] if l == 0 else [w_in[l + 1], w_out[l + 1], w_kv[l + 1]]
            return _Gather(xs) if xs else None
        if where == "attn_fwd" and l == 0 and self.depth > 1:
            return _Gather([w_out[1], w_kv[1]])
        if where == "d_w_in":
            d_wout, d_wkv = payload
            return _Scatter([d_wout.reshape(N_DEV, D // N_DEV, D), d_wkv.reshape(N_DEV, D // N_DEV, 2 * M_W)])
        if where == "d_x":
            self.blocks = _padded_to_shards(payload)
            return _Scatter([self.blocks], relations=_PART_A if l > 0 else tuple(range(N_DEV)))
        if where == "attn_bwd" and l + 1 < self.depth:
            return _Scatter([self.blocks], relations=_PART_B, into=[self.recv[l + 1]["w_in"]])
        return None

    def landed(self, where, l, outs):
        if where == "in_proj" and outs:
            outs = list(outs)
            if l == 0:
                self.raw[0]["w_out"], self.raw[0]["w_kv"], self.gate_all = outs[:3]
                outs = outs[3:]
            if outs:
                self.raw[l + 1].update(zip(("w_in", "w_out", "w_kv"), outs))
        elif where == "attn_fwd" and outs:
            self.raw[l + 1]["w_out"], self.raw[l + 1]["w_kv"] = outs
        elif where == "d_w_in":
            self.recv[l]["w_out"], self.recv[l]["w_kv"] = outs
        elif where == "d_x":
            self.recv[l]["w_in"] = outs[0]
        elif where == "attn_bwd" and outs:
            self.recv[l + 1]["w_in"] = outs[0]


def _local_step(x, mem, target, pipe, rel, gate_b, norm_g, ln_g, ln_b):
    depth = pipe.depth
    s = x.shape[0]
    saved = []
    xl = x
    for l in range(depth):
        w_in_p = pipe.w_in(l)
        hmat, landed = _mm(xl, w_in_p, out_dtype=BF16, tm=1024, tn=1024, tk=D, name="in_proj",
                           comm=pipe.host("in_proj", l), dead_n=H_DEAD)
        pipe.landed("in_proj", l, landed)
        w_out_f, w_kv_f = pipe.w_rest(l)
        mkv, _ = _mm(mem, w_kv_f, out_dtype=BF16, tm=N_MEM, tn=1024, tk=D, name="mem_kv")
        bias = _band_bias(rel[l])
        gw = jnp.zeros((128, B_KW), F32).at[:GATE_RANK].set(pipe.gate_w(l)).astype(BF16)
        gb, ng = gate_b[l][None, :], norm_g[l][None, :]
        r, linv, probs, landed = _attn_fwd(hmat, lax.empty((s, D), BF16), bias, pipe.host("attn_fwd", l))
        pipe.landed("attn_fwd", l, landed)
        r, opre, states = _gla_fwd(hmat, r, gw, gb, ng)
        r = _mem_fwd(hmat, r, mkv)
        xn, xh, rstd, y, *loss = _outproj_ln(hmat, r, w_out_f, xl, ln_g[l][None, :], ln_b[l][None, :],
                                             target if l == depth - 1 else None)
        saved.append(dict(x=xl, h=hmat, mkv=mkv, gw=gw, gb=gb, ng=ng, r=r, linv=linv, probs=probs, opre=opre,
                          states=states, xh=xh, rstd=rstd, y=y, w_in_p=w_in_p, w_out_f=w_out_f))
        xl = xn
    (loss,), g = loss, xl

    grads = [None] * depth
    for l in reversed(range(depth)):
        sv = saved[l]
        dy, du, d_lng, d_lnb, d_wout = _ln_bwd_dy(g, sv["xh"], sv["rstd"], ln_g[l][None, :], sv["w_out_f"], sv["y"])
        dh, dbias, landed = _attn_bwd(sv["h"], sv["r"], dy, sv["linv"], sv["probs"], lax.empty((s, H_W), BF16),
                                      pipe.host("attn_bwd", l))
        pipe.landed("attn_bwd", l, landed)
        dh, d_gw, d_gb, d_ng = _gla_bwd(sv["h"], dy, sv["opre"], sv["states"], sv["gw"], sv["gb"], sv["ng"], dh)
        dh, d_mkv = _mem_bwd(sv["h"], sv["r"], dy, sv["mkv"], dh)
        d_wkv, _ = _mm(mem, d_mkv, ta=True, out_dtype=BF16, tm=1024, tn=1024, tk=N_MEM, name="d_w_kv")
        d_win, landed = _mm(sv["x"], dh, ta=True, out_dtype=BF16, tm=1024, tn=1792, tk=1024, name="d_w_in",
                            comm=pipe.host("d_w_in", l, (d_wout, d_wkv)), dead_n=H_DEAD)
        pipe.landed("d_w_in", l, landed)
        g, landed = _mm(dh, sv["w_in_p"], tb=True, out_dtype=F32, tm=1024, tn=1024, tk=1792, name="d_x",
                        adds=((du, ALPHA),), comm=pipe.host("d_x", l, d_win), dead_k=H_DEAD)
        pipe.landed("d_x", l, landed)
        grads[l] = dict(rel=_bias_grad(dbias), gate_w=d_gw[:GATE_RANK], gate_b=d_gb[0], norm_g=d_ng[0],
                        ln_g=d_lng[0], ln_b=d_lnb[0])
    return loss, g, grads


def _adamw(parts, w, m, v, rows_per_step, name):
    depth, rows, cols = w.shape
    n = parts[0].shape[0]
    tr = min(rows_per_step, rows)
    assert rows % tr == 0 and len(parts) == depth

    def body(*refs):
        p_refs = refs[:depth]
        w_ref, m_ref, v_ref, g_ref, d_ref, nm_ref, nv_ref = refs[depth:]
        for l in range(depth):
            @pl.when(pl.program_id(0) == l)
            def _(p_ref=p_refs[l]):
                g = p_ref[0].astype(F32)
                for j in range(1, n):
                    g = g + p_ref[j].astype(F32)
                nm = ADAM_B1 * m_ref[...] + (1.0 - ADAM_B1) * g
                nv = ADAM_B2 * v_ref[...] + (1.0 - ADAM_B2) * (g * g)
                m_hat = nm / (1.0 - ADAM_B1 ** ADAM_STEP)
                v_hat = nv / (1.0 - ADAM_B2 ** ADAM_STEP)
                g_ref[...] = g
                nm_ref[...] = nm
                nv_ref[...] = nv
                d_ref[...] = -ADAM_LR * (m_hat / (jnp.sqrt(v_hat) + ADAM_EPS) + ADAM_WD * w_ref[...])

    def part_spec(l):
        return pl.BlockSpec((n, tr, cols), lambda ll, i: (0, jnp.where(ll == l, i, 0), 0))

    blk = pl.BlockSpec((None, tr, cols), lambda ll, i: (ll, i, 0))
    shape = jax.ShapeDtypeStruct((depth, rows, cols), F32)
    return pl.pallas_call(
        body, name=name,
        out_shape=(shape, shape, shape, shape),
        grid=(depth, rows // tr),
        in_specs=[part_spec(l) for l in range(depth)] + [blk, blk, blk],
        out_specs=(blk, blk, blk, blk),
        compiler_params=_params(("arbitrary", "arbitrary")),
    )(*parts, w, m, v)


SMALL = (("rel", A_HEADS * (2 * MAX_REL + 1)), ("gate_w", GATE_RANK * B_KW), ("gate_b", B_KW), ("norm_g", B_DV),
         ("ln_g", D), ("ln_b", D))


def _pack_small(parts, depth):
    rows = []
    for name, size in SMALL:
        flat = parts[name].reshape(depth * size).astype(F32)
        rows.append(jnp.pad(flat, (0, -(depth * size) % 128)).reshape(-1, 128))
    packed = jnp.concatenate(rows, axis=0)
    return jnp.pad(packed, ((0, -packed.shape[0] % 8), (0, 0)))


def _unpack_small(packed, depth, shapes):
    out, row = {}, 0
    for name, size in SMALL:
        nrow = -(-(depth * size) // 128)
        out[name] = packed[row:row + nrow].reshape(-1)[:depth * size].reshape(shapes[name])
        row += nrow
    return out


def kernel(x, mem, w_in, a_rel_bias, b_gate_w, b_gate_b, b_norm_g, w_mem_kv, w_out, ln_g, ln_b, loss_target, m_w_in, m_a_rel_bias, m_b_gate_w, m_b_gate_b, m_b_norm_g, m_w_mem_kv, m_w_out, m_ln_g, m_ln_b, v_w_in, v_a_rel_bias, v_b_gate_w, v_b_gate_b, v_b_norm_g, v_w_mem_kv, v_w_out, v_ln_g, v_ln_b):
    depth = w_in.shape[0]
    sh_in = w_in.shape[2]
    sh_gw = b_gate_w.shape[2]
    me = 4 * lax.axis_index("x") + 2 * lax.axis_index("y") + lax.axis_index("c")

    pipe = _Fsdp(w_in.astype(BF16), w_out.astype(BF16), w_mem_kv.astype(BF16), b_gate_w)
    loss_dev, dx, grads = _local_step(x[0], mem[0], loss_target[0], pipe, a_rel_bias, b_gate_b, b_norm_g, ln_g, ln_b)
    loss = lax.psum(loss_dev[0, 0], ("x", "y", "c"))

    recv = lambda n: [pipe.recv[l][n] for l in range(depth)]
    big = {"w_in": _adamw(recv("w_in"), w_in, m_w_in, v_w_in, 128, "adamw_w_in"),
           "w_out": _adamw(recv("w_out"), w_out, m_w_out, v_w_out, 64, "adamw_w_out"),
           "w_kv": _adamw(recv("w_kv"), w_mem_kv, m_w_mem_kv, v_w_mem_kv, 128, "adamw_w_kv")}

    shapes = {"rel": a_rel_bias.shape, "gate_w": (depth, GATE_RANK, N_DEV * sh_gw), "gate_b": b_gate_b.shape,
              "norm_g": b_norm_g.shape, "ln_g": ln_g.shape, "ln_b": ln_b.shape}
    part = _pack_small({n: jnp.stack([grads[l][n] for l in range(depth)]) for n, _ in SMALL}, depth)
    (all_parts,) = _gather_now([part], "gather_small")
    zeros_gw = jnp.zeros(shapes["gate_w"], F32)
    w_s = _pack_small(dict(rel=a_rel_bias, gate_w=zeros_gw, gate_b=b_gate_b, norm_g=b_norm_g, ln_g=ln_g, ln_b=ln_b), depth)
    m_s = _pack_small(dict(rel=m_a_rel_bias, gate_w=zeros_gw, gate_b=m_b_gate_b, norm_g=m_b_norm_g, ln_g=m_ln_g, ln_b=m_ln_b), depth)
    v_s = _pack_small(dict(rel=v_a_rel_bias, gate_w=zeros_gw, gate_b=v_b_gate_b, norm_g=v_b_norm_g, ln_g=v_ln_g, ln_b=v_ln_b), depth)
    small = [_unpack_small(t[0], depth, shapes)
             for t in _adamw([all_parts], w_s[None], m_s[None], v_s[None], all_parts.shape[1], "adamw_small")]
    gw_grad = lax.dynamic_slice_in_dim(small[0]["gate_w"], me * sh_gw, sh_gw, axis=2).reshape(1, depth * GATE_RANK, sh_gw)
    flat = lambda t: t.reshape(1, depth * GATE_RANK, sh_gw)
    gw_res = [t.reshape(depth, GATE_RANK, sh_gw)
              for t in _adamw([gw_grad], flat(b_gate_w), flat(m_b_gate_w), flat(v_b_gate_w), depth * GATE_RANK, "adamw_gate_w")]

    def leaves(t):
        return (big["w_in"][t], small[t]["rel"], gw_res[t], small[t]["gate_b"], small[t]["norm_g"],
                big["w_kv"][t], big["w_out"][t], small[t]["ln_g"], small[t]["ln_b"])

    return (loss, dx[None]) + leaves(0) + leaves(1) + leaves(2) + leaves(3)
```

```python
import functools
import math

import numpy as np
import jax
import jax.numpy as jnp
from jax import lax
from jax.experimental import pallas as pl
from jax.experimental.pallas import tpu as pltpu

F32 = jnp.float32
BF16 = jnp.bfloat16

N_DEV = 8
D = 2048
CH = 64
LEFT = 8
MAX_REL = 128
N_MEM = 256
A_HEADS, A_DH, A_W = 8, 128, 1024
B_HEADS, B_DK, B_DV, B_KW, B_W = 4, 64, 128, 256, 512
GATE_RANK, GATE_TAU = 16, 16.0
M_HEADS, M_DH, M_W = 4, 128, 512
IN_W = 6672
NAT_SPLIT = 5648
H_W = 7168
H_PAD = H_W - IN_W
A_Q, A_K, A_V, A_Z = 0, 1024, 2048, 3072
B_BASE = 4096
M_BASE = 6144
H_DEAD = (B_BASE + 1664, M_BASE)
ALPHA = (2.0 * 4) ** 0.25
LOG2E = math.log2(math.e)
LN_EPS = 1e-5
RMS_EPS = 1e-6
NEG = -1e30
QB = 256
KB = 3 * QB
ADAM_LR, ADAM_B1, ADAM_B2, ADAM_EPS, ADAM_WD, ADAM_STEP = 0.001, 0.9, 0.999, 1e-08, 0.01, 10
VMEM_MB = 1024 * 1024


def _params(sem, vmem_mb=48):
    return pltpu.CompilerParams(dimension_semantics=sem, vmem_limit_bytes=vmem_mb * VMEM_MB)


def _sigmoid(x):
    return 1.0 / (1.0 + jnp.exp(-x))


def _dot(a, b, ca, cb, precision=None):
    return lax.dot_general(a, b, (((ca,), (cb,)), ((), ())), preferred_element_type=F32, precision=precision)


MESH = pl.DeviceIdType.MESH


class _Scatter:
    def __init__(self, xs, relations=tuple(range(N_DEV)), into=None):
        self.xs, self.n = list(xs), len(xs)
        self.relations, self.into = tuple(relations), list(into or [])
        self.out_shape = [jax.ShapeDtypeStruct(x.shape, x.dtype) for x in xs]
        self.specs = [pl.BlockSpec(memory_space=pltpu.HBM)] * self.n
        self.scratch = [pltpu.SemaphoreType.DMA((self.n, N_DEV)), pltpu.SemaphoreType.DMA((self.n, N_DEV)),
                        pltpu.SemaphoreType.DMA((self.n,))]

    def _copies(self, x_refs, o_refs, sems):
        send_sems, recv_sems, local_sems = sems
        mx, my, mc = lax.axis_index("x"), lax.axis_index("y"), lax.axis_index("c")
        me = 4 * mx + 2 * my + mc
        own, sends, arrivals = [], [], []
        for k in self.relations:
            if k == 0:
                own = [pltpu.make_async_copy(x_refs[a].at[me], o_refs[a].at[me], local_sems.at[a]) for a in range(self.n)]
                continue
            px = 1 - mx if k & 4 else mx
            py = 1 - my if k & 2 else my
            pc = 1 - mc if k & 1 else mc
            idx = 4 * px + 2 * py + pc
            for a in range(self.n):
                for dst, group in ((o_refs[a].at[me], sends), (o_refs[a].at[idx], arrivals)):
                    group.append(pltpu.make_async_remote_copy(
                        src_ref=x_refs[a].at[idx], dst_ref=dst, send_sem=send_sems.at[a, k], recv_sem=recv_sems.at[a, k],
                        device_id=(px, py, pc), device_id_type=MESH))
        return own, sends, arrivals

    def start(self, x_refs, o_refs, sems):
        own, sends, _ = self._copies(x_refs, o_refs, sems)
        for cp in own + sends:
            cp.start()

    def middle(self, x_refs, o_refs, sems):
        pass

    def finish(self, x_refs, o_refs, sems):
        own, sends, arrivals = self._copies(x_refs, o_refs, sems)
        for cp in sends:
            cp.wait_send()
        for cp in arrivals:
            cp.wait_recv()
        for cp in own:
            cp.wait()


class _Gather:
    into = ()

    def __init__(self, xs):
        self.xs, self.n = list(xs), len(xs)
        self.out_shape = [jax.ShapeDtypeStruct((N_DEV,) + x.shape, x.dtype) for x in xs]
        self.specs = [pl.BlockSpec(memory_space=pltpu.HBM)] * self.n
        self.scratch = [pltpu.SemaphoreType.DMA((self.n, N_DEV - 1)), pltpu.SemaphoreType.DMA((self.n, N_DEV - 1)),
                        pltpu.SemaphoreType.DMA((self.n,))]

    def _copies(self, x_refs, o_refs, sems):
        send_sems, recv_sems, local_sems = sems
        mx, my, mc = lax.axis_index("x"), lax.axis_index("y"), lax.axis_index("c")
        idx = lambda px, py, pc: 4 * px + 2 * py + pc
        me, sibling = (mx, my, mc), (mx, my, 1 - mc)
        chips = [(mx, 1 - my), (1 - mx, my), (1 - mx, 1 - my)]

        def copy(a, k, src, slot, to):
            return pltpu.make_async_remote_copy(
                src_ref=src, dst_ref=o_refs[a].at[idx(*slot)], send_sem=send_sems.at[a, k], recv_sem=recv_sems.at[a, k],
                device_id=to, device_id_type=MESH)

        c = dict(own=[], first=[], passed=[], ici_in=[], late_in=[])
        for a in range(self.n):
            x = x_refs[a]
            c["own"].append(pltpu.make_async_copy(x, o_refs[a].at[idx(*me)], local_sems.at[a]))
            c["first"].append(copy(a, 0, x, me, sibling))
            c["late_in"].append(copy(a, 0, x, sibling, sibling))
            for j, chip in enumerate(chips):
                c["first"].append(copy(a, 1 + j, x, me, (*chip, mc)))
                c["ici_in"].append(copy(a, 1 + j, x, (*chip, mc), (*chip, mc)))
                c["passed"].append(copy(a, 4 + j, o_refs[a].at[idx(*chip, mc)], (*chip, mc), sibling))
                c["late_in"].append(copy(a, 4 + j, x, (*chip, 1 - mc), sibling))
        return c

    def start(self, x_refs, o_refs, sems):
        c = self._copies(x_refs, o_refs, sems)
        for cp in c["own"] + c["first"]:
            cp.start()

    def middle(self, x_refs, o_refs, sems):
        c = self._copies(x_refs, o_refs, sems)
        for arrived, onward in zip(c["ici_in"], c["passed"]):
            arrived.wait_recv()
            onward.start()

    def finish(self, x_refs, o_refs, sems):
        c = self._copies(x_refs, o_refs, sems)
        for cp in c["first"] + c["passed"]:
            cp.wait_send()
        for cp in c["late_in"]:
            cp.wait_recv()
        for cp in c["own"]:
            cp.wait()


def _gather_now(xs, name):
    comm = _Gather(xs)

    def body(*refs):
        x_refs, o_refs, sems = refs[:comm.n], refs[comm.n:2 * comm.n], refs[2 * comm.n:]
        comm.start(x_refs, o_refs, sems)
        comm.middle(x_refs, o_refs, sems)
        comm.finish(x_refs, o_refs, sems)

    return pl.pallas_call(body, name=name, out_shape=tuple(comm.out_shape), in_specs=comm.specs,
                          out_specs=tuple(comm.specs), scratch_shapes=comm.scratch)(*comm.xs)


def _hosted(body, comm, *, name, grid, in_specs, out_specs, out_shape, scratch_shapes, compiler_params, inputs,
            input_output_aliases=None):
    out_specs, out_shape = tuple(out_specs), tuple(out_shape)
    aliases = input_output_aliases or {}
    if comm is None:
        outs = pl.pallas_call(body, name=name, grid=grid, in_specs=list(in_specs), out_specs=out_specs, out_shape=out_shape,
                              scratch_shapes=list(scratch_shapes), compiler_params=compiler_params,
                              input_output_aliases=aliases)(*inputs)
        return tuple(outs), ()
    ni, no, ns, nc = len(in_specs), len(out_specs), len(scratch_shapes), comm.n
    n_into = len(comm.into)
    aliases = {**aliases, **{ni + nc + a: no + a for a in range(n_into)}}

    def wrapped(*refs):
        ins, c_in = refs[:ni], refs[ni:ni + nc]
        refs = refs[ni + nc + n_into:]
        outs, c_out = refs[:no], refs[no:no + nc]
        scr, sems = refs[no + nc:no + nc + ns], refs[no + nc + ns:]
        first = functools.reduce(jnp.logical_and, [pl.program_id(d) == 0 for d in range(len(grid))])
        last = functools.reduce(jnp.logical_and, [pl.program_id(d) == grid[d] - 1 for d in range(len(grid))])

        step = functools.reduce(lambda acc, d: acc * grid[d] + pl.program_id(d), range(len(grid)), 0)
        n_steps = math.prod(grid)

        @pl.when(first)
        def _():
            comm.start(c_in, c_out, sems)

        body(*ins, *outs, *scr)

        @pl.when(step == max(7 * n_steps // 8, 1) - 1)
        def _():
            comm.middle(c_in, c_out, sems)

        @pl.when(last)
        def _():
            comm.finish(c_in, c_out, sems)

    params = pltpu.CompilerParams(dimension_semantics=("arbitrary",) * len(grid),
                                  vmem_limit_bytes=compiler_params.vmem_limit_bytes)
    outs = pl.pallas_call(wrapped, name=name, grid=grid, in_specs=list(in_specs) + comm.specs + comm.specs[:n_into],
                          out_specs=out_specs + tuple(comm.specs), out_shape=out_shape + tuple(comm.out_shape),
                          scratch_shapes=list(scratch_shapes) + comm.scratch, compiler_params=params,
                          input_output_aliases=aliases)(*inputs, *comm.xs, *comm.into)
    return tuple(outs[:no]), tuple(outs[no:])


def _live_pieces(dead, tile):
    lo, hi = dead
    t = lo // tile
    assert (hi - 1) // tile == t and lo % 128 == 0 and hi % 128 == 0
    return t, [(x, y) for x, y in ((0, lo - t * tile), (hi - t * tile, tile)) if y > x]


def _mm(a, b, *, ta=False, tb=False, out_dtype, tm, tn, tk, name, adds=(), vmem_mb=48, comm=None, dead_n=None, dead_k=None):
    m = a.shape[1] if ta else a.shape[0]
    k = a.shape[0] if ta else a.shape[1]
    n = b.shape[0] if tb else b.shape[1]
    assert k == (b.shape[1] if tb else b.shape[0])
    tm, tn, tk = min(tm, m), min(tn, n), min(tk, k)
    assert m % tm == 0 and n % tn == 0 and k % tk == 0, (name, m, n, k)
    nk = k // tk
    n_add = len(adds)
    scales = [s for _, s in adds]
    jd, cols_d = _live_pieces(dead_n, tn) if dead_n else (None, None)
    kd, ks_d = _live_pieces(dead_k, tk) if dead_k else (None, None)

    def body(a_ref, b_ref, *rest):
        add_refs, o_ref = rest[:n_add], rest[n_add]
        acc_ref = rest[n_add + 1] if nk > 1 else None
        jj, kk = pl.program_id(1), pl.program_id(2)

        def product(c0, c1, ks):
            p = None
            for k0, k1 in ks:
                a_blk = a_ref[k0:k1, :] if ta else a_ref[:, k0:k1]
                b_blk = b_ref[c0:c1, k0:k1] if tb else b_ref[k0:k1, c0:c1]
                q = _dot(a_blk.astype(BF16), b_blk.astype(BF16), 0 if ta else 1, 1 if tb else 0)
                p = q if p is None else p + q
            return p

        def step(mode, cols, ks):
            for c0, c1 in cols:
                p = product(c0, c1, ks)
                if mode == "first":
                    acc_ref[:, c0:c1] = p
                elif mode == "mid":
                    acc_ref[:, c0:c1] += p
                else:
                    r = p if mode == "only" else acc_ref[:, c0:c1] + p
                    for ref, s in zip(add_refs, scales):
                        r = r + s * ref[:, c0:c1].astype(F32)
                    o_ref[:, c0:c1] = r.astype(out_dtype)

        modes = [("only", None, range(1))] if nk == 1 else (
            [("first", kk == 0, range(1))] + ([("mid", (kk > 0) & (kk < nk - 1), range(1, nk - 1))] if nk > 2 else [])
            + [("last", kk == nk - 1, range(nk - 1, nk))])
        for mode, when_k, k_range in modes:
            k_cases = [(when_k, [(0, tk)])]
            if kd is not None and kd in k_range:
                is_dead = kk == kd
                k_cases = [(is_dead, ks_d)] + ([(when_k & ~is_dead, [(0, tk)])] if len(k_range) > 1 else [])
            for cond_k, ks in k_cases:
                n_cases = [(None, [(0, tn)])] if jd is None else [(jj == jd, cols_d), (jj != jd, [(0, tn)])]
                for cond_n, cols in n_cases:
                    conds = [c for c in (cond_k, cond_n) if c is not None]
                    run = functools.partial(step, mode, cols, ks)
                    pl.when(functools.reduce(jnp.logical_and, conds))(run) if conds else run()

    a_spec = pl.BlockSpec((tk, tm), lambda i, j, kk: (kk, i)) if ta else pl.BlockSpec((tm, tk), lambda i, j, kk: (i, kk))
    b_spec = pl.BlockSpec((tn, tk), lambda i, j, kk: (j, kk)) if tb else pl.BlockSpec((tk, tn), lambda i, j, kk: (kk, j))
    add_specs = [pl.BlockSpec((tm, tn), lambda i, j, kk: (i, j)) for _ in adds]
    (out,), c_out = _hosted(
        body, comm, name=name,
        out_shape=[jax.ShapeDtypeStruct((m, n), out_dtype)],
        grid=(m // tm, n // tn, nk),
        in_specs=[a_spec, b_spec] + add_specs,
        out_specs=[pl.BlockSpec((tm, tn), lambda i, j, kk: (i, j))],
        scratch_shapes=[pltpu.VMEM((tm, tn), F32)] if nk > 1 else [],
        compiler_params=_params(("parallel", "parallel", "arbitrary"), vmem_mb),
        inputs=(a, b, *[x for x, _ in adds]))
    return out, c_out


NKB = KB // QB
LEAD = NKB - 1


def _band_bias(table):
    i = np.arange(QB)[:, None]
    j = np.arange(KB)[None, :]
    qc = i // CH + LEAD * QB // CH
    kc = j // CH
    valid = (kc <= qc) & (kc >= qc - LEFT)
    n = QB + KB
    c = np.arange(n)
    onehot = np.zeros((2 * MAX_REL + 1, n), np.float32)
    onehot[np.clip(LEAD * QB - (c - (QB - 1)), -MAX_REL, MAX_REL) + MAX_REL, c] = 1.0
    row = jnp.dot(table.astype(F32), jnp.asarray(onehot), precision=lax.Precision.HIGHEST)
    flow = jnp.tile(row, (1, QB))[:, :QB * (n - 1)].reshape(table.shape[0], QB, n - 1)
    return jnp.where(valid[None], flow[:, :, QB - 1:] * LOG2E, NEG)


def _bias_grad(dbias):
    h, n = dbias.shape[0], KB + 1
    flat = jnp.pad(dbias.reshape(h, QB * KB), ((0, 0), (0, -(QB * KB) % n)))
    diag = flat.reshape(h, -1, n).sum(axis=1)
    c = np.arange(n)
    jm = np.where(c < LEAD * QB + CH, c, c - n)
    didx = np.clip(LEAD * QB - jm, -MAX_REL, MAX_REL) + MAX_REL
    onehot = np.zeros((n, 2 * MAX_REL + 1), np.float32)
    onehot[c, didx] = 1.0
    return jnp.dot(diag, jnp.asarray(onehot), precision=lax.Precision.HIGHEST)


def _attn_scores(q, k_refs, cs, bias_h, m, masked, scale):
    parts = []
    for t in range(NKB):
        sc = _dot(q, k_refs[t][:, cs], 1, 1) * scale
        parts.append(jnp.where(m + t - LEAD >= 0, sc, NEG) if masked else sc)
    return jnp.concatenate(parts, axis=1) + bias_h


def _attn_fwd(h, r_in, bias, comm=None):
    s = h.shape[0]
    nq = s // QB
    scale = A_DH ** -0.5

    def body(q_ref, *rest):
        k_refs, v_refs = rest[:NKB], rest[NKB:2 * NKB]
        bias_hbm, r_any, o_ref, linv_ref, p_ref, bias_ref = rest[2 * NKB:]
        del r_any
        m = pl.program_id(0)

        @pl.when(m == 0)
        def _():
            pltpu.sync_copy(bias_hbm, bias_ref)

        def scores(hd, masked):
            cs = slice(hd * A_DH, (hd + 1) * A_DH)
            return _attn_scores(q_ref[:, cs], k_refs, cs, bias_ref[hd], m, masked, scale * LOG2E)

        def step(masked):
            sc_next = scores(0, masked)
            for hd in range(A_HEADS):
                cs = slice(hd * A_DH, (hd + 1) * A_DH)
                sc = sc_next
                if hd + 1 < A_HEADS:
                    sc_next = scores(hd + 1, masked)
                mx = jnp.max(sc, axis=1, keepdims=True)
                p = jnp.exp2(sc - mx)
                l = jnp.sum(p, axis=1, keepdims=True)
                pb = p.astype(BF16)
                p_ref[hd] = pb
                o = _dot(pb[:, :QB], v_refs[0][:, cs], 1, 0)
                for t in range(1, NKB):
                    o += _dot(pb[:, t * QB:(t + 1) * QB], v_refs[t][:, cs], 1, 0)
                linv = 1.0 / l
                o_ref[:, cs] = (o * linv).astype(BF16)
                linv_ref[:, hd:hd + 1] = linv

        pl.when(m < LEAD)(functools.partial(step, True))
        pl.when(m >= LEAD)(functools.partial(step, False))

    def kv_spec(col, t):
        return pl.BlockSpec((QB, A_W), lambda m: (jnp.maximum(m + t - LEAD, 0), col))

    (r, linv, probs), c_out = _hosted(
        body, comm, name="attn_fwd",
        out_shape=(jax.ShapeDtypeStruct(r_in.shape, BF16), jax.ShapeDtypeStruct((s, A_HEADS), F32),
                   jax.ShapeDtypeStruct((nq, A_HEADS, QB, KB), BF16)),
        grid=(nq,),
        in_specs=[pl.BlockSpec((QB, A_W), lambda m: (m, 0))]
        + [kv_spec(1, t) for t in range(NKB)] + [kv_spec(2, t) for t in range(NKB)]
        + [pl.BlockSpec(memory_space=pl.ANY), pl.BlockSpec(memory_space=pl.ANY)],
        out_specs=(pl.BlockSpec((QB, A_W), lambda m: (m, 0)), pl.BlockSpec((QB, A_HEADS), lambda m: (m, 0)),
                   pl.BlockSpec((None, A_HEADS, QB, KB), lambda m: (m, 0, 0, 0))),
        scratch_shapes=[pltpu.VMEM((A_HEADS, QB, KB), F32)],
        input_output_aliases={2 * NKB + 2: 0},
        compiler_params=_params(("arbitrary",)),
        inputs=(h,) * (2 * NKB + 1) + (bias, r_in))
    return r, linv, probs, c_out


def _attn_bwd(h, r, dy, linv, probs, dh_in, comm=None):
    s = h.shape[0]
    nq = s // QB
    scale = A_DH ** -0.5

    def body(q_ref, *rest):
        k_refs, v_refs = rest[:NKB], rest[NKB:2 * NKB]
        (z_ref, r_ref, dy_ref, linv_ref, p_ref, zl_ref, rl_ref, dyl_ref, dh_any,
         dh_ref, dbias_hbm, dk_acc, dv_acc, dq_ring, dbias_ref) = rest[2 * NKB:]
        del dh_any
        m = pl.program_id(0)

        def slot(b):
            return pl.multiple_of(lax.rem(b + NKB, NKB) * QB, QB)

        @pl.when(m == 0)
        def _():
            dk_acc[...] = jnp.zeros_like(dk_acc)
            dv_acc[...] = jnp.zeros_like(dv_acc)
            dq_ring[...] = jnp.zeros_like(dq_ring)
            dbias_ref[...] = jnp.zeros_like(dbias_ref)

        @pl.when(m < nq)
        def _():
            z = z_ref[...].astype(F32)
            do_all = dy_ref[...].astype(F32) * (z * _sigmoid(z))
            o_all = r_ref[...].astype(F32)

            def product(hd):
                cs = slice(hd * A_DH, (hd + 1) * A_DH)
                dob = do_all[:, cs].astype(BF16)
                return jnp.concatenate([_dot(dob, v_refs[t][:, cs], 1, 1) for t in range(NKB)], axis=1)

            ahead = product(0)
            for hd in range(A_HEADS):
                cs = slice(hd * A_DH, (hd + 1) * A_DH)
                q = q_ref[:, cs]
                do = do_all[:, cs]
                dob = do.astype(BF16)
                delta = jnp.sum(do * o_all[:, cs], axis=1, keepdims=True)
                dp = ahead
                if hd + 1 < A_HEADS:
                    ahead = product(hd + 1)
                p = p_ref[hd].astype(F32) * linv_ref[:, hd:hd + 1]
                ds = p * (dp - delta)
                dbias_ref[hd] += ds
                pb, dsb = p.astype(BF16), ds.astype(BF16)
                dq = jnp.zeros((QB, A_DH), F32)
                for t in range(NKB):
                    ts = slice(t * QB, (t + 1) * QB)
                    rows = pl.ds(slot(m - LEAD + t), QB)
                    dq += _dot(dsb[:, ts], k_refs[t][:, cs], 1, 0)
                    dk_acc[rows, cs] += _dot(dsb[:, ts], q, 0, 0) * scale
                    dv_acc[rows, cs] += _dot(pb[:, ts], dob, 0, 0)
                dq_ring[pl.ds(slot(m), QB), cs] = dq * scale

        done = pl.ds(slot(m - LEAD), QB)
        zl = zl_ref[...].astype(F32)
        sg = _sigmoid(zl)
        dz = dyl_ref[...].astype(F32) * rl_ref[...].astype(F32) * (sg * (1.0 + zl * (1.0 - sg)))
        dh_ref[:, A_Q:A_Q + A_W] = dq_ring[done, :].astype(BF16)
        dh_ref[:, A_K:A_K + A_W] = dk_acc[done, :].astype(BF16)
        dh_ref[:, A_V:A_V + A_W] = dv_acc[done, :].astype(BF16)
        dh_ref[:, A_Z:A_Z + A_W] = dz.astype(BF16)
        dk_acc[done, :] = jnp.zeros((QB, A_W), F32)
        dv_acc[done, :] = jnp.zeros((QB, A_W), F32)

        @pl.when(m == nq + LEAD - 1)
        def _():
            pltpu.sync_copy(dbias_ref, dbias_hbm)

    last = nq - 1

    def cur(col):
        return pl.BlockSpec((QB, A_W), lambda m: (jnp.minimum(m, last), col))

    def kv_spec(col, t):
        return pl.BlockSpec((QB, A_W), lambda m: (jnp.clip(m + t - LEAD, 0, last), col))

    def lag(col):
        return pl.BlockSpec((QB, A_W), lambda m: (jnp.clip(m - LEAD, 0, last), col))

    (dh, dbias), c_out = _hosted(
        body, comm, name="attn_bwd",
        out_shape=(jax.ShapeDtypeStruct(dh_in.shape, BF16), jax.ShapeDtypeStruct((A_HEADS, QB, KB), F32)),
        grid=(nq + LEAD,),
        in_specs=[cur(0)] + [kv_spec(1, t) for t in range(NKB)] + [kv_spec(2, t) for t in range(NKB)]
        + [cur(3), cur(0), cur(0), pl.BlockSpec((QB, A_HEADS), lambda m: (jnp.minimum(m, last), 0)),
           pl.BlockSpec((None, A_HEADS, QB, KB), lambda m: (jnp.minimum(m, last), 0, 0, 0)),
           lag(3), lag(0), lag(0), pl.BlockSpec(memory_space=pl.ANY)],
        out_specs=(pl.BlockSpec((QB, 4 * A_W), lambda m: (jnp.clip(m - LEAD, 0, last), 0)),
                   pl.BlockSpec(memory_space=pl.ANY)),
        scratch_shapes=[pltpu.VMEM((KB, A_W), F32), pltpu.VMEM((KB, A_W), F32), pltpu.VMEM((KB, A_W), F32),
                        pltpu.VMEM((A_HEADS, QB, KB), F32)],
        input_output_aliases={2 * NKB + 9: 0},
        compiler_params=_params(("arbitrary",), 56),
        inputs=(h,) * (2 * NKB + 2) + (r, dy, linv, probs, h, r, dy, dh_in))
    return dh, dbias, c_out


GB = 256
N_PAIR = B_HEADS // 2


def _gla_gates(lr, gw_ref, gb_ref):
    logit = _dot(lr, gw_ref[...], 1, 0) + gb_ref[...]
    lg = (jnp.minimum(logit, 0.0) - jnp.log(1.0 + jnp.exp(-jnp.abs(logit)))) / GATE_TAU
    row = lax.broadcasted_iota(jnp.int32, (GB, GB), 0)
    col = lax.broadcasted_iota(jnp.int32, (GB, GB), 1)
    tri = jnp.where((row // CH == col // CH) & (col <= row), 1.0, 0.0).astype(F32)
    return logit, _dot(tri, lg, 1, 0, precision=lax.Precision.HIGHEST)


def _gla_factors(hb_ref, b_all, c):
    rs = slice(c * CH, (c + 1) * CH)
    q = hb_ref[rs, 0:B_KW].astype(F32) * (B_DK ** -0.5)
    k = hb_ref[rs, B_KW:2 * B_KW].astype(F32)
    b = b_all[rs]
    bm, bl = b[CH // 2:CH // 2 + 1, :], b[CH - 1:CH, :]
    e1, e2, eb, ek = jnp.exp(b - bm), jnp.exp(bm - b), jnp.exp(b), jnp.exp(bl - b)
    el = jnp.exp(bl)
    return dict(ql=q * e1, kl=k * e2, qu=q * e2, ku=k * e1, qt=q * eb, kh=k * ek, e1=e1, e2=e2, eb=eb, ek=ek, el=el)


class _GlaPairTools:
    def __init__(self, hb_ref, fs):
        self.hb_ref, self.fs, self.made = hb_ref, fs, {}
        row = lax.broadcasted_iota(jnp.int32, (2 * CH, 2 * CH), 0)
        col = lax.broadcasted_iota(jnp.int32, (2 * CH, 2 * CH), 1)
        self.same = (row // CH) == (col // CH)
        self.lower = self.same & ((row % CH) >= (col % CH))
        self.upper = self.same & ((row % CH) < (col % CH))
        self.lower_t = self.same & ((col % CH) >= (row % CH))
        self.upper_t = self.same & ((col % CH) < (row % CH))

    def _once(self, key, make):
        if key not in self.made:
            self.made[key] = make()
        return self.made[key]

    def lanes(self, c, p, name):
        return self.fs[c][name][:, p * 128:(p + 1) * 128]

    def heads(self, c, p, name):
        def make():
            x = self.lanes(c, p, name)
            return jnp.where(self.same, jnp.concatenate([x, x], axis=0), 0.0).astype(BF16)
        return self._once(("heads", c, p, name), make)

    def twice(self, c, p, name):
        def make():
            x = self.lanes(c, p, name).astype(BF16)
            return jnp.concatenate([x, x], axis=0)
        return self._once(("twice", c, p, name), make)

    def vals(self, c, p):
        rows = slice(c * CH, (c + 1) * CH)
        return self._once(("vals", c, p), lambda: jnp.concatenate(
            [self.hb_ref[rows, 512 + (2 * p + sh) * B_DV:512 + (2 * p + sh + 1) * B_DV] for sh in range(2)], axis=0))

    def intra(self, c, p):
        lo = _dot(self.heads(c, p, "ql"), self.twice(c, p, "kl"), 1, 1)
        up = _dot(self.heads(c, p, "qu"), self.twice(c, p, "ku"), 1, 1)
        return jnp.where(self.lower, lo, jnp.where(self.upper, up, 0.0)).astype(BF16)

    def fold(self, x):
        x = jnp.where(self.same, x, 0.0)
        return x[:CH] + x[CH:]


def _gla_fwd(h, r_in, gw, gb, ng):
    s = h.shape[0]
    nb = s // GB
    cpb = GB // CH

    def body(hb_ref, gw_ref, gb_ref, ng_ref, r_any, o_ref, opre_ref, st_ref, state):
        del r_any

        @pl.when(pl.program_id(0) == 0)
        def _():
            state[...] = jnp.zeros_like(state)

        _, b_all = _gla_gates(hb_ref[:, 1536:1664], gw_ref, gb_ref)
        fs = [_gla_factors(hb_ref, b_all, c) for c in range(cpb)]
        pairs = [(c, p) for c in range(cpb) for p in range(N_PAIR)]
        tools = _GlaPairTools(hb_ref, fs)
        a = {k: tools.intra(*k) for k in pairs}
        o_intra = {k: _dot(a[k], tools.vals(*k), 1, 0) for k in pairs}
        upd = {k: _dot(tools.vals(*k), tools.heads(*k, "kh"), 0, 0) for k in pairs}
        st = [state[p] for p in range(N_PAIR)]
        entering = {}
        for c, p in pairs:
            entering[c, p] = st[p]
            st_ref[c, p] = st[p]
            st[p] = st[p] * fs[c]["el"][:, p * 128:(p + 1) * 128] + upd[c, p]
        for p in range(N_PAIR):
            state[p] = st[p]
        for c, p in pairs:
            o2 = o_intra[c, p] + _dot(tools.heads(c, p, "qt"), entering[c, p].astype(BF16), 1, 1)
            for sh in range(2):
                o = o2[sh * CH:(sh + 1) * CH]
                rs, hs = slice(c * CH, (c + 1) * CH), slice((2 * p + sh) * B_DV, (2 * p + sh + 1) * B_DV)
                opre_ref[rs, hs] = o
                rinv = lax.rsqrt(jnp.mean(o * o, axis=1, keepdims=True) + RMS_EPS)
                o_ref[rs, hs] = (o * rinv * ng_ref[...]).astype(BF16)

    (r, opre, states), _ = _hosted(
        body, None, name="gla_fwd",
        out_shape=(jax.ShapeDtypeStruct(r_in.shape, BF16), jax.ShapeDtypeStruct((s, B_W), F32),
                   jax.ShapeDtypeStruct((s // CH, N_PAIR, 128, 128), F32)),
        grid=(nb,),
        in_specs=[pl.BlockSpec((GB, 2048), lambda i: (i, B_BASE // 2048)),
                  pl.BlockSpec((128, B_KW), lambda i: (0, 0)), pl.BlockSpec((1, B_KW), lambda i: (0, 0)),
                  pl.BlockSpec((1, B_DV), lambda i: (0, 0)), pl.BlockSpec(memory_space=pl.ANY)],
        out_specs=(pl.BlockSpec((GB, B_W), lambda i: (i, 1024 // B_W)), pl.BlockSpec((GB, B_W), lambda i: (i, 0)),
                   pl.BlockSpec((cpb, N_PAIR, 128, 128), lambda i: (i, 0, 0, 0))),
        scratch_shapes=[pltpu.VMEM((N_PAIR, 128, 128), F32)],
        input_output_aliases={4: 0},
        compiler_params=_params(("arbitrary",)),
        inputs=(h, gw, gb, ng, r_in))
    return r, opre, states


def _gla_bwd(h, dy, opre, states, gw, gb, ng, dh_in):
    s = h.shape[0]
    nb = s // GB
    cpb = GB // CH

    def body(hb_ref, dy_ref, opre_ref, st_ref, gw_ref, gb_ref, ng_ref, dh_any,
             dh_ref, dgw_ref, dgb_ref, dng_ref, dstate, db_scr, do_scr):
        del dh_any

        @pl.when(pl.program_id(0) == 0)
        def _():
            dstate[...] = jnp.zeros_like(dstate)
            dgw_ref[...] = jnp.zeros_like(dgw_ref)
            dgb_ref[...] = jnp.zeros_like(dgb_ref)
            dng_ref[...] = jnp.zeros_like(dng_ref)

        lr = hb_ref[:, 1536:1664]
        logit, b_all = _gla_gates(lr, gw_ref, gb_ref)
        z = hb_ref[:, 1024:1536].astype(F32)
        sg = _sigmoid(z)
        dyb = dy_ref[...].astype(F32)
        dng = jnp.zeros((1, B_DV), F32)
        for hd in range(B_HEADS):
            hs = slice(hd * B_DV, (hd + 1) * B_DV)
            o = opre_ref[:, hs]
            rinv = lax.rsqrt(jnp.mean(o * o, axis=1, keepdims=True) + RMS_EPS)
            on = o * rinv
            dr = dyb[:, hs] * (z[:, hs] * sg[:, hs])
            dh_ref[:, 1024 + hd * B_DV:1024 + (hd + 1) * B_DV] = (
                dyb[:, hs] * (on * ng_ref[...]) * (sg[:, hs] * (1.0 + z[:, hs] * (1.0 - sg[:, hs])))).astype(BF16)
            dng += jnp.sum(dr * on, axis=0, keepdims=True)
            dn = dr * ng_ref[...]
            do_scr[:, hs] = rinv * (dn - on * jnp.mean(dn * on, axis=1, keepdims=True))
        dng_ref[...] += dng

        rowi = lax.broadcasted_iota(jnp.int32, (CH, 128), 0)
        fs = [_gla_factors(hb_ref, b_all, c) for c in range(cpb)]
        pairs = [(c, p) for c in reversed(range(cpb)) for p in range(N_PAIR)]
        tools = _GlaPairTools(hb_ref, fs)
        pair = tools.lanes

        def douts(c, p):
            rows = slice(c * CH, (c + 1) * CH)
            return tools._once(("douts", c, p), lambda: jnp.concatenate(
                [do_scr[rows, (2 * p + sh) * B_DV:(2 * p + sh + 1) * B_DV] for sh in range(2)], axis=0).astype(BF16))

        at, da, dat, dst_own, g_qt = {}, {}, {}, {}, {}
        for k in pairs:
            c, p = k
            lo_t = _dot(tools.twice(*k, "kl"), tools.heads(*k, "ql"), 1, 1)
            up_t = _dot(tools.twice(*k, "ku"), tools.heads(*k, "qu"), 1, 1)
            at[k] = jnp.where(tools.lower_t, lo_t, jnp.where(tools.upper_t, up_t, 0.0)).astype(BF16)
            da[k] = _dot(douts(*k), tools.vals(*k), 1, 1)
            dat[k] = _dot(tools.vals(*k), douts(*k), 1, 1)
            dst_own[k] = _dot(douts(*k), tools.heads(*k, "qt"), 0, 0)
            g_qt[k] = tools.fold(_dot(douts(*k), st_ref[c, p].astype(BF16), 1, 0))
        dv, g_ql, g_qu, g_kl, g_ku = {}, {}, {}, {}, {}
        for k in pairs:
            dv[k] = _dot(at[k], douts(*k), 1, 0)
            g_ql[k] = tools.fold(_dot(jnp.where(tools.lower, da[k], 0.0).astype(BF16), tools.twice(*k, "kl"), 1, 0))
            g_qu[k] = tools.fold(_dot(jnp.where(tools.upper, da[k], 0.0).astype(BF16), tools.twice(*k, "ku"), 1, 0))
            g_kl[k] = tools.fold(_dot(jnp.where(tools.lower_t, dat[k], 0.0).astype(BF16), tools.heads(*k, "ql"), 1, 0))
            g_ku[k] = tools.fold(_dot(jnp.where(tools.upper_t, dat[k], 0.0).astype(BF16), tools.heads(*k, "qu"), 1, 0))
        dst = [dstate[p] for p in range(N_PAIR)]
        leaving = {}
        for c, p in pairs:
            leaving[c, p] = dst[p]
            dst[p] = dst[p] * pair(c, p, "el") + dst_own[c, p]
        for p in range(N_PAIR):
            dstate[p] = dst[p]
        g_kh = {}
        for k in pairs:
            c, p = k
            dstb = leaving[k].astype(BF16)
            dv2 = dv[k] + _dot(tools.heads(*k, "kh"), dstb, 1, 1)
            for sh in range(2):
                hd = 2 * p + sh
                dh_ref[c * CH:(c + 1) * CH, 512 + hd * B_DV:512 + (hd + 1) * B_DV] = dv2[sh * CH:(sh + 1) * CH].astype(BF16)
            g_kh[k] = tools.fold(_dot(tools.vals(*k), dstb, 1, 0))
        for c in reversed(range(cpb)):
            rs = slice(c * CH, (c + 1) * CH)
            for p in range(N_PAIR):
                k = (c, p)
                dq = (g_ql[k] * pair(c, p, "e1") + g_qu[k] * pair(c, p, "e2") + g_qt[k] * pair(c, p, "eb")) * (B_DK ** -0.5)
                dk = g_kl[k] * pair(c, p, "e2") + g_ku[k] * pair(c, p, "e1") + g_kh[k] * pair(c, p, "ek")
                dkh_kh = g_kh[k] * pair(c, p, "kh")
                db = (g_ql[k] * pair(c, p, "ql") - g_qu[k] * pair(c, p, "qu") + g_qt[k] * pair(c, p, "qt")
                      - g_kl[k] * pair(c, p, "kl") + g_ku[k] * pair(c, p, "ku") - dkh_kh)
                db_last = (pair(c, p, "el") * jnp.sum(leaving[c, p] * st_ref[c, p], axis=0, keepdims=True)
                           + jnp.sum(dkh_kh, axis=0, keepdims=True))
                db = jnp.where(rowi == CH - 1, db + db_last, db)
                dh_ref[rs, p * 128:(p + 1) * 128] = dq.astype(BF16)
                dh_ref[rs, B_KW + p * 128:B_KW + (p + 1) * 128] = dk.astype(BF16)
                db_scr[rs, p * 128:(p + 1) * 128] = db

        row = lax.broadcasted_iota(jnp.int32, (GB, GB), 0)
        col = lax.broadcasted_iota(jnp.int32, (GB, GB), 1)
        trit = jnp.where((row // CH == col // CH) & (col >= row), 1.0, 0.0).astype(F32)
        dlg = _dot(trit, db_scr[...], 1, 0, precision=lax.Precision.HIGHEST)
        dlogit = dlg * (_sigmoid(-logit) / GATE_TAU)
        dlb = dlogit.astype(BF16)
        dgw_ref[...] += _dot(lr, dlb, 0, 0)
        dgb_ref[...] += jnp.sum(dlogit, axis=0, keepdims=True)
        dh_ref[:, 1536:1664] = _dot(dlb, gw_ref[...], 1, 1).astype(BF16)
        dh_ref[:, 1664:2048] = jnp.zeros((GB, 384), BF16)

    rev = lambda i: nb - 1 - i
    (dh, dgw, dgb, dng), _ = _hosted(
        body, None, name="gla_bwd",
        out_shape=(jax.ShapeDtypeStruct(dh_in.shape, BF16), jax.ShapeDtypeStruct((128, B_KW), F32),
                   jax.ShapeDtypeStruct((1, B_KW), F32), jax.ShapeDtypeStruct((1, B_DV), F32)),
        grid=(nb,),
        in_specs=[pl.BlockSpec((GB, 2048), lambda i: (rev(i), B_BASE // 2048)),
                  pl.BlockSpec((GB, B_W), lambda i: (rev(i), 1024 // B_W)),
                  pl.BlockSpec((GB, B_W), lambda i: (rev(i), 0)),
                  pl.BlockSpec((cpb, N_PAIR, 128, 128), lambda i: (rev(i), 0, 0, 0)),
                  pl.BlockSpec((128, B_KW), lambda i: (0, 0)), pl.BlockSpec((1, B_KW), lambda i: (0, 0)),
                  pl.BlockSpec((1, B_DV), lambda i: (0, 0)), pl.BlockSpec(memory_space=pl.ANY)],
        out_specs=(pl.BlockSpec((GB, 2048), lambda i: (rev(i), B_BASE // 2048)),
                   pl.BlockSpec((128, B_KW), lambda i: (0, 0)), pl.BlockSpec((1, B_KW), lambda i: (0, 0)),
                   pl.BlockSpec((1, B_DV), lambda i: (0, 0))),
        scratch_shapes=[pltpu.VMEM((N_PAIR, 128, 128), F32), pltpu.VMEM((GB, B_KW), F32), pltpu.VMEM((GB, B_W), F32)],
        input_output_aliases={7: 0},
        compiler_params=_params(("arbitrary",)),
        inputs=(h, dy, opre, states, gw, gb, ng, dh_in))
    return dh, dgw, dgb, dng


MB = 512


def _mem_probs(q, mk, scale):
    sc = _dot(q, mk, 1, 1) * (scale * LOG2E)
    p = jnp.exp2(sc - jnp.max(sc, axis=1, keepdims=True))
    return p / jnp.sum(p, axis=1, keepdims=True)


def _mem_fwd(h, r_in, mkv):
    s = h.shape[0]
    scale = M_DH ** -0.5

    def body(q_ref, mkv_ref, r_any, o_ref):
        del r_any
        for hd in range(M_HEADS):
            cs = slice(hd * M_DH, (hd + 1) * M_DH)
            p = _mem_probs(q_ref[:, cs], mkv_ref[:, cs], scale)
            o_ref[:, cs] = _dot(p.astype(BF16), mkv_ref[:, M_W + hd * M_DH:M_W + (hd + 1) * M_DH], 1, 0).astype(BF16)

    return pl.pallas_call(
        body, name="mem_fwd",
        out_shape=jax.ShapeDtypeStruct(r_in.shape, BF16),
        grid=(s // MB,),
        in_specs=[pl.BlockSpec((MB, M_W), lambda i: (i, M_BASE // M_W)),
                  pl.BlockSpec((N_MEM, 2 * M_W), lambda i: (0, 0)), pl.BlockSpec(memory_space=pl.ANY)],
        out_specs=pl.BlockSpec((MB, M_W), lambda i: (i, 1536 // M_W)),
        input_output_aliases={2: 0},
        compiler_params=_params(("arbitrary",)),
    )(h, mkv, r_in)


def _mem_bwd(h, r, dy, mkv, dh_in):
    s = h.shape[0]
    scale = M_DH ** -0.5

    def body(q_ref, z_ref, r_ref, dy_ref, mkv_ref, dh_any, dh_ref, dmkv_ref):
        del dh_any

        @pl.when(pl.program_id(0) == 0)
        def _():
            dmkv_ref[...] = jnp.zeros_like(dmkv_ref)

        z = z_ref[...].astype(F32)
        sg = _sigmoid(z)
        dyv = dy_ref[...].astype(F32)
        do_all = dyv * (z * sg)
        dh_ref[:, M_W:2 * M_W] = (dyv * r_ref[...].astype(F32) * (sg * (1.0 + z * (1.0 - sg)))).astype(BF16)
        for hd in range(M_HEADS):
            cs = slice(hd * M_DH, (hd + 1) * M_DH)
            vs = slice(M_W + hd * M_DH, M_W + (hd + 1) * M_DH)
            q = q_ref[:, cs]
            p = _mem_probs(q, mkv_ref[:, cs], scale)
            dob = do_all[:, cs].astype(BF16)
            dp = _dot(dob, mkv_ref[:, vs], 1, 1)
            ds = p * (dp - jnp.sum(p * dp, axis=1, keepdims=True))
            dsb = ds.astype(BF16)
            dh_ref[:, cs] = (_dot(dsb, mkv_ref[:, cs], 1, 0) * scale).astype(BF16)
            dmkv_ref[:, cs] += _dot(dsb, q, 0, 0) * scale
            dmkv_ref[:, vs] += _dot(p.astype(BF16), dob, 0, 0)

    return pl.pallas_call(
        body, name="mem_bwd",
        out_shape=(jax.ShapeDtypeStruct(dh_in.shape, BF16), jax.ShapeDtypeStruct((N_MEM, 2 * M_W), F32)),
        grid=(s // MB,),
        in_specs=[pl.BlockSpec((MB, M_W), lambda i: (i, M_BASE // M_W)),
                  pl.BlockSpec((MB, M_W), lambda i: (i, M_BASE // M_W + 1)),
                  pl.BlockSpec((MB, M_W), lambda i: (i, 1536 // M_W)),
                  pl.BlockSpec((MB, M_W), lambda i: (i, 1536 // M_W)),
                  pl.BlockSpec((N_MEM, 2 * M_W), lambda i: (0, 0)), pl.BlockSpec(memory_space=pl.ANY)],
        out_specs=(pl.BlockSpec((MB, 2 * M_W), lambda i: (i, M_BASE // (2 * M_W))),
                   pl.BlockSpec((N_MEM, 2 * M_W), lambda i: (0, 0))),
        input_output_aliases={5: 0},
        compiler_params=_params(("arbitrary",)),
    )(h, h, r, dy, mkv, dh_in)


OB = 256


def _outproj_ln(h, r, w_out, x, ln_g, ln_b, target=None):
    s = h.shape[0]
    last = target is not None

    def body(za_ref, zb_ref, zm_ref, r_ref, w_ref, x_ref, g_ref, b_ref, *rest):
        if last:
            t_ref, xn_ref, xh_ref, rstd_ref, y_ref, l_ref = rest
        else:
            xn_ref, xh_ref, rstd_ref, y_ref = rest
        z = jnp.concatenate([za_ref[...], zb_ref[...], zm_ref[...]], axis=1).astype(F32)
        y = (r_ref[...].astype(F32) * (z * _sigmoid(z))).astype(BF16)
        y_ref[...] = y
        u = ALPHA * x_ref[...] + _dot(y, w_ref[...], 1, 0)
        mu = jnp.mean(u, axis=1, keepdims=True)
        uc = u - mu
        rstd = lax.rsqrt(jnp.mean(uc * uc, axis=1, keepdims=True) + LN_EPS)
        xh = uc * rstd
        xh_ref[...] = xh
        rstd_ref[...] = rstd
        xn = xh * g_ref[...] + b_ref[...]
        if last:
            @pl.when(pl.program_id(0) == 0)
            def _():
                l_ref[...] = jnp.zeros_like(l_ref)

            e = xn - t_ref[...]
            xn_ref[...] = e / D
            l_ref[...] += 0.5 * jnp.sum(jnp.mean(e * e, axis=1, keepdims=True))
        else:
            xn_ref[...] = xn

    row = lambda w, c: pl.BlockSpec((OB, w), lambda i: (i, c))
    vec = pl.BlockSpec((1, D), lambda i: (0, 0))
    full = jax.ShapeDtypeStruct((s, D), F32)
    return pl.pallas_call(
        body, name="outproj_ln_loss" if last else "outproj_ln",
        out_shape=(full, full, jax.ShapeDtypeStruct((s, 1), F32), jax.ShapeDtypeStruct((s, D), BF16))
        + ((jax.ShapeDtypeStruct((1, 128), F32),) if last else ()),
        grid=(s // OB,),
        in_specs=[row(A_W, A_Z // A_W), row(B_W, (B_BASE + 1024) // B_W), row(M_W, (M_BASE + M_W) // M_W), row(D, 0),
                  pl.BlockSpec((D, D), lambda i: (0, 0)), row(D, 0), vec, vec] + ([row(D, 0)] if last else []),
        out_specs=(row(D, 0), row(D, 0), pl.BlockSpec((OB, 1), lambda i: (i, 0)), row(D, 0))
        + ((pl.BlockSpec((1, 128), lambda i: (0, 0)),) if last else ()),
        compiler_params=_params(("arbitrary",), 56),
    )(h, h, h, r, w_out, x, ln_g, ln_b, *([target] if last else []))


def _ln_bwd_dy(g, xh, rstd, ln_g, w_out, y):
    s = g.shape[0]
    n = s // OB

    def body(g_ref, xh_ref, rstd_ref, lg_ref, w_ref, y_ref, dy_ref, du_ref, dg_ref, db_ref, dw_hbm, dw_acc, stage):
        i = pl.program_id(0)

        @pl.when(i == 0)
        def _():
            dg_ref[...] = jnp.zeros_like(dg_ref)
            db_ref[...] = jnp.zeros_like(db_ref)
            dw_acc[...] = jnp.zeros_like(dw_acc)

        gv, xh = g_ref[...], xh_ref[...]
        dg_ref[...] += jnp.sum(gv * xh, axis=0, keepdims=True)
        db_ref[...] += jnp.sum(gv, axis=0, keepdims=True)
        dxh = gv * lg_ref[...]
        du = rstd_ref[...] * (dxh - jnp.mean(dxh, axis=1, keepdims=True) - xh * jnp.mean(dxh * xh, axis=1, keepdims=True))
        du_ref[...] = du
        dub = du.astype(BF16)
        dy_ref[...] = _dot(dub, w_ref[...], 1, 1).astype(BF16)
        dw_acc[...] += _dot(y_ref[...], dub, 0, 0)

        @pl.when(i == n - 1)
        def _():
            for c in range(D // OB):
                stage[...] = dw_acc[c * OB:(c + 1) * OB, :].astype(BF16)
                pltpu.sync_copy(stage, dw_hbm.at[c * OB:(c + 1) * OB, :])

    row = pl.BlockSpec((OB, D), lambda i: (i, 0))
    vec = pl.BlockSpec((1, D), lambda i: (0, 0))
    return pl.pallas_call(
        body, name="ln_bwd_dy",
        out_shape=(jax.ShapeDtypeStruct((s, D), BF16), jax.ShapeDtypeStruct((s, D), F32),
                   jax.ShapeDtypeStruct((1, D), F32), jax.ShapeDtypeStruct((1, D), F32), jax.ShapeDtypeStruct((D, D), BF16)),
        grid=(n,),
        in_specs=[row, row, pl.BlockSpec((OB, 1), lambda i: (i, 0)), vec, pl.BlockSpec((D, D), lambda i: (0, 0)), row],
        out_specs=(row, row, vec, vec, pl.BlockSpec(memory_space=pl.ANY)),
        scratch_shapes=[pltpu.VMEM((D, D), F32), pltpu.VMEM((OB, D), BF16)],
        compiler_params=_params(("arbitrary",), 60),
    )(g, xh, rstd, ln_g, w_out, y)


class _LocalWeights:
    def __init__(self, w_in_p, w_out_f, w_kv_f, gate_w):
        self.w, self.gate = list(zip(w_in_p, w_out_f, w_kv_f)), gate_w
        self.depth = len(self.w)
        self.grads = [dict() for _ in self.w]

    def w_in(self, l):
        return self.w[l][0]

    def w_rest(self, l):
        return self.w[l][1:]

    def gate_w(self, l):
        return self.gate[l]

    def host(self, where, l, payload=None):
        if payload is not None:
            self.grads[l][where] = payload
        return None

    def landed(self, where, l, outs):
        pass


def _shard_pieces():
    sh = IN_W // N_DEV
    pieces = []
    for j in range(N_DEV):
        lo, hi = j * sh, (j + 1) * sh
        cuts = [lo, NAT_SPLIT, hi] if lo < NAT_SPLIT < hi else [lo, hi]
        for a, b in zip(cuts[:-1], cuts[1:]):
            pieces.append((j, a - lo, a if a < NAT_SPLIT else a + H_PAD, b - a))
    return pieces


RB = 512


def _shards_to_padded(raw):
    def body(x_ref, o_ref):
        for j, src, dst, width in _shard_pieces():
            o_ref[:, dst:dst + width] = x_ref[j, :, src:src + width]
        o_ref[:, NAT_SPLIT:NAT_SPLIT + H_PAD] = jnp.zeros((RB, H_PAD), o_ref.dtype)

    return pl.pallas_call(
        body, name="place_w_in", out_shape=jax.ShapeDtypeStruct((D, H_W), raw.dtype), grid=(D // RB,),
        in_specs=[pl.BlockSpec((N_DEV, RB, raw.shape[2]), lambda i: (0, i, 0))],
        out_specs=pl.BlockSpec((RB, H_W), lambda i: (i, 0)),
        compiler_params=_params(("parallel",)),
    )(raw)


def _padded_to_shards(w):
    sh = IN_W // N_DEV

    def body(x_ref, o_ref):
        for j, src, dst, width in _shard_pieces():
            o_ref[j, :, src:src + width] = x_ref[:, dst:dst + width]

    return pl.pallas_call(
        body, name="split_d_w_in", out_shape=jax.ShapeDtypeStruct((N_DEV, D, sh), w.dtype), grid=(D // RB,),
        in_specs=[pl.BlockSpec((RB, H_W), lambda i: (i, 0))],
        out_specs=pl.BlockSpec((N_DEV, RB, sh), lambda i: (0, i, 0)),
        compiler_params=_params(("parallel",)),
    )(w)


_PART_A = (0, 1, 2, 4, 7)
_PART_B = (3, 5, 6)


class _Fsdp:
    def __init__(self, w_in, w_out, w_kv, gate_w):
        self.sh = (w_in, w_out, w_kv)
        self.gate_sh = gate_w
        self.depth = w_in.shape[0]
        self.raw = [dict() for _ in range(self.depth)]
        self.recv = [dict() for _ in range(self.depth)]
        (self.raw[0]["w_in"],) = _gather_now([w_in[0]], "gather_layer0")

    def w_in(self, l):
        return _shards_to_padded(self.raw[l]["w_in"])

    def w_rest(self, l):
        return self.raw[l]["w_out"].reshape(D, D), self.raw[l]["w_kv"].reshape(D, 2 * M_W)

    def gate_w(self, l):
        return jnp.transpose(self.gate_all[:, l], (1, 0, 2)).reshape(GATE_RANK, -1)

    def host(self, where, l, payload=None):
        w_in, w_out, w_kv = self.sh
        if where == "in_proj":
            xs = [w_out[0], w_kv[0], self.gate_sh] if l == 0 else []
            if l + 1 < self.depth:
                xs += [w_in[l + 1]] if l == 0 else [w_in[l + 1], w_out[l + 1], w_kv[l + 1]]
            return _Gather(xs) if xs else None
        if where == "attn_fwd" and l == 0 and self.depth > 1:
            return _Gather([w_out[1], w_kv[1]])
        if where == "d_w_in":
            d_wout, d_wkv = payload
            return _Scatter([d_wout.reshape(N_DEV, D // N_DEV, D), d_wkv.reshape(N_DEV, D // N_DEV, 2 * M_W)])
        if where == "d_x":
            self.blocks = _padded_to_shards(payload)
            return _Scatter([self.blocks], relations=_PART_A if l > 0 else tuple(range(N_DEV)))
        if where == "attn_bwd" and l + 1 < self.depth:
            return _Scatter([self.blocks], relations=_PART_B, into=[self.recv[l + 1]["w_in"]])
        return None

    def landed(self, where, l, outs):
        if where == "in_proj" and outs:
            outs = list(outs)
            if l == 0:
                self.raw[0]["w_out"], self.raw[0]["w_kv"], self.gate_all = outs[:3]
                outs = outs[3:]
            if outs:
                self.raw[l + 1].update(zip(("w_in", "w_out", "w_kv"), outs))
        elif where == "attn_fwd" and outs:
            self.raw[l + 1]["w_out"], self.raw[l + 1]["w_kv"] = outs
        elif where == "d_w_in":
            self.recv[l]["w_out"], self.recv[l]["w_kv"] = outs
        elif where == "d_x":
            self.recv[l]["w_in"] = outs[0]
        elif where == "attn_bwd" and outs:
            self.recv[l + 1]["w_in"] = outs[0]


def _local_step(x, mem, target, pipe, rel, gate_b, norm_g, ln_g, ln_b):
    depth = pipe.depth
    s = x.shape[0]
    saved = []
    xl = x
    for l in range(depth):
        w_in_p = pipe.w_in(l)
        hmat, landed = _mm(xl, w_in_p, out_dtype=BF16, tm=1024, tn=1024, tk=D, name="in_proj",
                           comm=pipe.host("in_proj", l), dead_n=H_DEAD)
        pipe.landed("in_proj", l, landed)
        w_out_f, w_kv_f = pipe.w_rest(l)
        mkv, _ = _mm(mem, w_kv_f, out_dtype=BF16, tm=N_MEM, tn=1024, tk=D, name="mem_kv")
        bias = _band_bias(rel[l])
        gw = jnp.zeros((128, B_KW), F32).at[:GATE_RANK].set(pipe.gate_w(l)).astype(BF16)
        gb, ng = gate_b[l][None, :], norm_g[l][None, :]
        r, linv, probs, landed = _attn_fwd(hmat, lax.empty((s, D), BF16), bias, pipe.host("attn_fwd", l))
        pipe.landed("attn_fwd", l, landed)
        r, opre, states = _gla_fwd(hmat, r, gw, gb, ng)
        r = _mem_fwd(hmat, r, mkv)
        xn, xh, rstd, y, *loss = _outproj_ln(hmat, r, w_out_f, xl, ln_g[l][None, :], ln_b[l][None, :],
                                             target if l == depth - 1 else None)
        saved.append(dict(x=xl, h=hmat, mkv=mkv, gw=gw, gb=gb, ng=ng, r=r, linv=linv, probs=probs, opre=opre,
                          states=states, xh=xh, rstd=rstd, y=y, w_in_p=w_in_p, w_out_f=w_out_f))
        xl = xn
    (loss,), g = loss, xl

    grads = [None] * depth
    for l in reversed(range(depth)):
        sv = saved[l]
        dy, du, d_lng, d_lnb, d_wout = _ln_bwd_dy(g, sv["xh"], sv["rstd"], ln_g[l][None, :], sv["w_out_f"], sv["y"])
        dh, dbias, landed = _attn_bwd(sv["h"], sv["r"], dy, sv["linv"], sv["probs"], lax.empty((s, H_W), BF16),
                                      pipe.host("attn_bwd", l))
        pipe.landed("attn_bwd", l, landed)
        dh, d_gw, d_gb, d_ng = _gla_bwd(sv["h"], dy, sv["opre"], sv["states"], sv["gw"], sv["gb"], sv["ng"], dh)
        dh, d_mkv = _mem_bwd(sv["h"], sv["r"], dy, sv["mkv"], dh)
        d_wkv, _ = _mm(mem, d_mkv, ta=True, out_dtype=BF16, tm=1024, tn=1024, tk=N_MEM, name="d_w_kv")
        d_win, landed = _mm(sv["x"], dh, ta=True, out_dtype=BF16, tm=1024, tn=1792, tk=1024, name="d_w_in",
                            comm=pipe.host("d_w_in", l, (d_wout, d_wkv)), dead_n=H_DEAD)
        pipe.landed("d_w_in", l, landed)
        g, landed = _mm(dh, sv["w_in_p"], tb=True, out_dtype=F32, tm=1024, tn=1024, tk=1792, name="d_x",
                        adds=((du, ALPHA),), comm=pipe.host("d_x", l, d_win), dead_k=H_DEAD)
        pipe.landed("d_x", l, landed)
        grads[l] = dict(rel=_bias_grad(dbias), gate_w=d_gw[:GATE_RANK], gate_b=d_gb[0], norm_g=d_ng[0],
                        ln_g=d_lng[0], ln_b=d_lnb[0])
    return loss, g, grads


def _adamw(parts, w, m, v, rows_per_step, name):
    depth, rows, cols = w.shape
    n = parts[0].shape[0]
    tr = min(rows_per_step, rows)
    assert rows % tr == 0 and len(parts) == depth

    def body(*refs):
        p_refs = refs[:depth]
        w_ref, m_ref, v_ref, g_ref, d_ref, nm_ref, nv_ref = refs[depth:]
        for l in range(depth):
            @pl.when(pl.program_id(0) == l)
            def _(p_ref=p_refs[l]):
                g = p_ref[0].astype(F32)
                for j in range(1, n):
                    g = g + p_ref[j].astype(F32)
                nm = ADAM_B1 * m_ref[...] + (1.0 - ADAM_B1) * g
                nv = ADAM_B2 * v_ref[...] + (1.0 - ADAM_B2) * (g * g)
                m_hat = nm / (1.0 - ADAM_B1 ** ADAM_STEP)
                v_hat = nv / (1.0 - ADAM_B2 ** ADAM_STEP)
                g_ref[...] = g
                nm_ref[...] = nm
                nv_ref[...] = nv
                d_ref[...] = -ADAM_LR * (m_hat / (jnp.sqrt(v_hat) + ADAM_EPS) + ADAM_WD * w_ref[...])

    def part_spec(l):
        return pl.BlockSpec((n, tr, cols), lambda ll, i: (0, jnp.where(ll == l, i, 0), 0))

    blk = pl.BlockSpec((None, tr, cols), lambda ll, i: (ll, i, 0))
    shape = jax.ShapeDtypeStruct((depth, rows, cols), F32)
    return pl.pallas_call(
        body, name=name,
        out_shape=(shape, shape, shape, shape),
        grid=(depth, rows // tr),
        in_specs=[part_spec(l) for l in range(depth)] + [blk, blk, blk],
        out_specs=(blk, blk, blk, blk),
        compiler_params=_params(("arbitrary", "arbitrary")),
    )(*parts, w, m, v)


SMALL = (("rel", A_HEADS * (2 * MAX_REL + 1)), ("gate_w", GATE_RANK * B_KW), ("gate_b", B_KW), ("norm_g", B_DV),
         ("ln_g", D), ("ln_b", D))


def _pack_small(parts, depth):
    rows = []
    for name, size in SMALL:
        flat = parts[name].reshape(depth * size).astype(F32)
        rows.append(jnp.pad(flat, (0, -(depth * size) % 128)).reshape(-1, 128))
    packed = jnp.concatenate(rows, axis=0)
    return jnp.pad(packed, ((0, -packed.shape[0] % 8), (0, 0)))


def _unpack_small(packed, depth, shapes):
    out, row = {}, 0
    for name, size in SMALL:
        nrow = -(-(depth * size) // 128)
        out[name] = packed[row:row + nrow].reshape(-1)[:depth * size].reshape(shapes[name])
        row += nrow
    return out


def kernel(x, mem, w_in, a_rel_bias, b_gate_w, b_gate_b, b_norm_g, w_mem_kv, w_out, ln_g, ln_b, loss_target, m_w_in, m_a_rel_bias, m_b_gate_w, m_b_gate_b, m_b_norm_g, m_w_mem_kv, m_w_out, m_ln_g, m_ln_b, v_w_in, v_a_rel_bias, v_b_gate_w, v_b_gate_b, v_b_norm_g, v_w_mem_kv, v_w_out, v_ln_g, v_ln_b):
    depth = w_in.shape[0]
    sh_in = w_in.shape[2]
    sh_gw = b_gate_w.shape[2]
    me = 4 * lax.axis_index("x") + 2 * lax.axis_index("y") + lax.axis_index("c")

    pipe = _Fsdp(w_in.astype(BF16), w_out.astype(BF16), w_mem_kv.astype(BF16), b_gate_w)
    loss_dev, dx, grads = _local_step(x[0], mem[0], loss_target[0], pipe, a_rel_bias, b_gate_b, b_norm_g, ln_g, ln_b)
    loss = lax.psum(loss_dev[0, 0], ("x", "y", "c"))

    recv = lambda n: [pipe.recv[l][n] for l in range(depth)]
    big = {"w_in": _adamw(recv("w_in"), w_in, m_w_in, v_w_in, 128, "adamw_w_in"),
           "w_out": _adamw(recv("w_out"), w_out, m_w_out, v_w_out, 64, "adamw_w_out"),
           "w_kv": _adamw(recv("w_kv"), w_mem_kv, m_w_mem_kv, v_w_mem_kv, 128, "adamw_w_kv")}

    shapes = {"rel": a_rel_bias.shape, "gate_w": (depth, GATE_RANK, N_DEV * sh_gw), "gate_b": b_gate_b.shape,
              "norm_g": b_norm_g.shape, "ln_g": ln_g.shape, "ln_b": ln_b.shape}
    part = _pack_small({n: jnp.stack([grads[l][n] for l in range(depth)]) for n, _ in SMALL}, depth)
    (all_parts,) = _gather_now([part], "gather_small")
    zeros_gw = jnp.zeros(shapes["gate_w"], F32)
    w_s = _pack_small(dict(rel=a_rel_bias, gate_w=zeros_gw, gate_b=b_gate_b, norm_g=b_norm_g, ln_g=ln_g, ln_b=ln_b), depth)
    m_s = _pack_small(dict(rel=m_a_rel_bias, gate_w=zeros_gw, gate_b=m_b_gate_b, norm_g=m_b_norm_g, ln_g=m_ln_g, ln_b=m_ln_b), depth)
    v_s = _pack_small(dict(rel=v_a_rel_bias, gate_w=zeros_gw, gate_b=v_b_gate_b, norm_g=v_b_norm_g, ln_g=v_ln_g, ln_b=v_ln_b), depth)
    small = [_unpack_small(t[0], depth, shapes)
             for t in _adamw([all_parts], w_s[None], m_s[None], v_s[None], all_parts.shape[1], "adamw_small")]
    gw_grad = lax.dynamic_slice_in_dim(small[0]["gate_w"], me * sh_gw, sh_gw, axis=2).reshape(1, depth * GATE_RANK, sh_gw)
    flat = lambda t: t.reshape(1, depth * GATE_RANK, sh_gw)
    gw_res = [t.reshape(depth, GATE_RANK, sh_gw)
              for t in _adamw([gw_grad], flat(b_gate_w), flat(m_b_gate_w), flat(v_b_gate_w), depth * GATE_RANK, "adamw_gate_w")]

    def leaves(t):
        return (big["w_in"][t], small[t]["rel"], gw_res[t], small[t]["gate_b"], small[t]["norm_g"],
                big["w_kv"][t], big["w_out"][t], small[t]["ln_g"], small[t]["ln_b"])

    return (loss, dx[None]) + leaves(0) + leaves(1) + leaves(2) + leaves(3)
```
